```python
import jax, jax.numpy as jnp
from jax import lax
import numpy as np

D_MODEL = 1024
BATCH = 8
SEQ = 4096
DEPTH = 1

PLE_DIM = 256
D_FF = 2816
D_MIX = D_MODEL
CHUNK = 128
GM_HEADS = 4
GM_HEAD_DIM = 128
GM_WIDTH = GM_HEADS * GM_HEAD_DIM
SB_HEADS = 8
SB_HEAD_DIM = 64
SB_WIDTH = SB_HEADS * SB_HEAD_DIM
SB_BLOCK = 128
MIX_IN_WIDTH = 2 * GM_WIDTH + 3 * SB_WIDTH
EPS = 1e-6

kernel_name = "hybrid_gmlp_stickbreaking_macaron_block"


def rms_norm(x, g):
    xf = x.astype(jnp.float32)
    y = xf * lax.rsqrt(jnp.mean(xf * xf, axis=-1, keepdims=True) + EPS)
    return (y * g.astype(jnp.float32)).astype(x.dtype)


def swiglu(x, w_in, w_out):
    gate, up = jnp.split(x @ w_in, 2, axis=-1)
    return (jax.nn.silu(gate) * up) @ w_out


def chunked_gmlp(u, v, v_gain, w_s, b_s):
    B, S, _ = u.shape
    nc = S // CHUNK
    vn = rms_norm(v, v_gain).reshape(B, nc, CHUNK, GM_HEADS, GM_HEAD_DIM)
    causal = jnp.tril(jnp.ones((CHUNK, CHUNK), dtype=bool))
    w = jnp.where(causal[None], w_s, jnp.zeros_like(w_s)).astype(vn.dtype)
    sv = jnp.einsum('hts,bcshd->bcthd', w, vn) + b_s.T.astype(vn.dtype)[None, None, :, :, None]
    return u * sv.reshape(B, S, GM_WIDTH)


def stick_breaking_attention(q, k, v):
    B, S, H, D = q.shape
    scale = D ** -0.5
    outs = []
    for i in range(S // SB_BLOCK):
        q0 = i * SB_BLOCK
        L = q0 + SB_BLOCK
        qb = q[:, q0:L]
        kp = k[:, :L]
        vp = v[:, :L]
        z = jnp.einsum('bthd,bshd->bhts', qb, kp).astype(jnp.float32) * scale
        t_idx = q0 + jnp.arange(SB_BLOCK)[:, None]
        s_idx = jnp.arange(L)[None, :]
        causal = s_idx < t_idx
        log_1m = jnp.where(causal, -jax.nn.softplus(z), 0.0)
        after = lax.cumsum(log_1m, axis=3, reverse=True) - log_1m
        a = jnp.where(causal, jnp.exp(jax.nn.log_sigmoid(z) + after), 0.0)
        outs.append(jnp.einsum('bhts,bshd->bthd', a.astype(vp.dtype), vp))
    return jnp.concatenate(outs, axis=1)


def _fwd_setup_inputs(seed: int = 0) -> dict:
    key = jax.random.key(seed)
    ks = jax.random.split(key, 18)
    f32 = jnp.float32

    def nrm(k, shape, fan_in):
        return jax.random.normal(k, shape, f32) * (fan_in ** -0.5)

    def gain(k, shape):
        return jnp.ones(shape, f32) + 0.05 * jax.random.normal(k, shape, f32)

    return {
        "x": jax.random.normal(ks[0], (BATCH, SEQ, D_MODEL), f32),
        "p": jax.random.normal(ks[1], (DEPTH, BATCH, SEQ, PLE_DIM), f32),
        "ffn1_norm": gain(ks[2], (DEPTH, D_MODEL)),
        "ffn1_w_in": nrm(ks[3], (DEPTH, D_MODEL, 2 * D_FF), D_MODEL),
        "ffn1_w_out": nrm(ks[4], (DEPTH, D_FF, D_MODEL), D_FF),
        "mix_norm": gain(ks[5], (DEPTH, D_MODEL)),
        "w_mix_in": nrm(ks[6], (DEPTH, D_MODEL, MIX_IN_WIDTH), D_MODEL),
        "gmlp_v_norm": gain(ks[7], (DEPTH, GM_WIDTH)),
        "gmlp_w_s": nrm(ks[8], (DEPTH, GM_HEADS, CHUNK, CHUNK), CHUNK),
        "gmlp_b": jnp.ones((DEPTH, GM_HEADS, CHUNK), f32) + 0.1 * jax.random.normal(ks[9], (DEPTH, GM_HEADS, CHUNK), f32),
        "w_mix_out": nrm(ks[10], (DEPTH, D_MIX, D_MODEL), D_MIX),
        "ffn2_norm": gain(ks[11], (DEPTH, D_MODEL)),
        "ffn2_w_in": nrm(ks[12], (DEPTH, D_MODEL, 2 * D_FF), D_MODEL),
        "ffn2_w_out": nrm(ks[13], (DEPTH, D_FF, D_MODEL), D_FF),
        "ple_norm": gain(ks[14], (DEPTH, D_MODEL)),
        "ple_w_gate": nrm(ks[15], (DEPTH, D_MODEL, D_MODEL), D_MODEL),
        "ple_w_proj": nrm(ks[16], (DEPTH, PLE_DIM, D_MODEL), PLE_DIM),
        "final_norm": gain(ks[17], (D_MODEL,)),
    }


def _fwd_reference(x, p, ffn1_norm, ffn1_w_in, ffn1_w_out, mix_norm, w_mix_in, gmlp_v_norm,
              gmlp_w_s, gmlp_b, w_mix_out, ffn2_norm, ffn2_w_in, ffn2_w_out,
              ple_norm, ple_w_gate, ple_w_proj, final_norm):
    B, S, _ = x.shape
    splits = [GM_WIDTH, 2 * GM_WIDTH, 2 * GM_WIDTH + SB_WIDTH, 2 * GM_WIDTH + 2 * SB_WIDTH]
    h = x
    for i in range(DEPTH):
        h = h + 0.5 * swiglu(rms_norm(h, ffn1_norm[i]), ffn1_w_in[i], ffn1_w_out[i])

        n = rms_norm(h, mix_norm[i])
        zmix = n @ w_mix_in[i]
        gm_u, gm_v, sb_q, sb_k, sb_v = jnp.split(zmix, splits, axis=-1)
        gm_out = chunked_gmlp(jax.nn.gelu(gm_u, approximate=False),
                              jax.nn.gelu(gm_v, approximate=False),
                              gmlp_v_norm[i], gmlp_w_s[i], gmlp_b[i])
        sb_out = stick_breaking_attention(sb_q.reshape(B, S, SB_HEADS, SB_HEAD_DIM),
                                          sb_k.reshape(B, S, SB_HEADS, SB_HEAD_DIM),
                                          sb_v.reshape(B, S, SB_HEADS, SB_HEAD_DIM))
        mixed = jnp.concatenate([gm_out, sb_out.reshape(B, S, SB_WIDTH)], axis=-1)
        h = h + mixed @ w_mix_out[i]

        h = h + 0.5 * swiglu(rms_norm(h, ffn2_norm[i]), ffn2_w_in[i], ffn2_w_out[i])

        gate = jax.nn.sigmoid(rms_norm(h, ple_norm[i]) @ ple_w_gate[i])
        h = h + gate * (p[i] @ ple_w_proj[i])
    return rms_norm(h, final_norm)


import jax as _jax
import jax.numpy as _jnp

TWIN_FORMAT = 'train_step'
FWD_PARAMS = ['x', 'p', 'ffn1_norm', 'ffn1_w_in', 'ffn1_w_out', 'mix_norm', 'w_mix_in', 'gmlp_v_norm', 'gmlp_w_s', 'gmlp_b', 'w_mix_out', 'ffn2_norm', 'ffn2_w_in', 'ffn2_w_out', 'ple_norm', 'ple_w_gate', 'ple_w_proj', 'final_norm']
TWIN_WEIGHTS = ['ffn1_norm', 'ffn1_w_in', 'ffn1_w_out', 'mix_norm', 'w_mix_in', 'gmlp_v_norm', 'gmlp_w_s', 'gmlp_b', 'w_mix_out', 'ffn2_norm', 'ffn2_w_in', 'ffn2_w_out', 'ple_norm', 'ple_w_gate', 'ple_w_proj', 'final_norm']
TWIN_DIFF_INPUT = 'x'
TWIN_INPUTS = ['x', 'p', 'ffn1_norm', 'ffn1_w_in', 'ffn1_w_out', 'mix_norm', 'w_mix_in', 'gmlp_v_norm', 'gmlp_w_s', 'gmlp_b', 'w_mix_out', 'ffn2_norm', 'ffn2_w_in', 'ffn2_w_out', 'ple_norm', 'ple_w_gate', 'ple_w_proj', 'final_norm', 'loss_target', 'm_ffn1_norm', 'm_ffn1_w_in', 'm_ffn1_w_out', 'm_mix_norm', 'm_w_mix_in', 'm_gmlp_v_norm', 'm_gmlp_w_s', 'm_gmlp_b', 'm_w_mix_out', 'm_ffn2_norm', 'm_ffn2_w_in', 'm_ffn2_w_out', 'm_ple_norm', 'm_ple_w_gate', 'm_ple_w_proj', 'm_final_norm', 'v_ffn1_norm', 'v_ffn1_w_in', 'v_ffn1_w_out', 'v_mix_norm', 'v_w_mix_in', 'v_gmlp_v_norm', 'v_gmlp_w_s', 'v_gmlp_b', 'v_w_mix_out', 'v_ffn2_norm', 'v_ffn2_w_in', 'v_ffn2_w_out', 'v_ple_norm', 'v_ple_w_gate', 'v_ple_w_proj', 'v_final_norm']
TWIN_OUTPUTS = ['loss', 'grad_x', 'grad_ffn1_norm', 'grad_ffn1_w_in', 'grad_ffn1_w_out', 'grad_mix_norm', 'grad_w_mix_in', 'grad_gmlp_v_norm', 'grad_gmlp_w_s', 'grad_gmlp_b', 'grad_w_mix_out', 'grad_ffn2_norm', 'grad_ffn2_w_in', 'grad_ffn2_w_out', 'grad_ple_norm', 'grad_ple_w_gate', 'grad_ple_w_proj', 'grad_final_norm', 'delta_ffn1_norm', 'delta_ffn1_w_in', 'delta_ffn1_w_out', 'delta_mix_norm', 'delta_w_mix_in', 'delta_gmlp_v_norm', 'delta_gmlp_w_s', 'delta_gmlp_b', 'delta_w_mix_out', 'delta_ffn2_norm', 'delta_ffn2_w_in', 'delta_ffn2_w_out', 'delta_ple_norm', 'delta_ple_w_gate', 'delta_ple_w_proj', 'delta_final_norm', 'new_m_ffn1_norm', 'new_m_ffn1_w_in', 'new_m_ffn1_w_out', 'new_m_mix_norm', 'new_m_w_mix_in', 'new_m_gmlp_v_norm', 'new_m_gmlp_w_s', 'new_m_gmlp_b', 'new_m_w_mix_out', 'new_m_ffn2_norm', 'new_m_ffn2_w_in', 'new_m_ffn2_w_out', 'new_m_ple_norm', 'new_m_ple_w_gate', 'new_m_ple_w_proj', 'new_m_final_norm', 'new_v_ffn1_norm', 'new_v_ffn1_w_in', 'new_v_ffn1_w_out', 'new_v_mix_norm', 'new_v_w_mix_in', 'new_v_gmlp_v_norm', 'new_v_gmlp_w_s', 'new_v_gmlp_b', 'new_v_w_mix_out', 'new_v_ffn2_norm', 'new_v_ffn2_w_in', 'new_v_ffn2_w_out', 'new_v_ple_norm', 'new_v_ple_w_gate', 'new_v_ple_w_proj', 'new_v_final_norm']
TWIN_LEAF_KINDS = {'loss': 'loss', 'grad_x': 'grad_x', 'grad_ffn1_norm': 'grad_w', 'grad_ffn1_w_in': 'grad_w', 'grad_ffn1_w_out': 'grad_w', 'grad_mix_norm': 'grad_w', 'grad_w_mix_in': 'grad_w', 'grad_gmlp_v_norm': 'grad_w', 'grad_gmlp_w_s': 'grad_w', 'grad_gmlp_b': 'grad_w', 'grad_w_mix_out': 'grad_w', 'grad_ffn2_norm': 'grad_w', 'grad_ffn2_w_in': 'grad_w', 'grad_ffn2_w_out': 'grad_w', 'grad_ple_norm': 'grad_w', 'grad_ple_w_gate': 'grad_w', 'grad_ple_w_proj': 'grad_w', 'grad_final_norm': 'grad_w', 'delta_ffn1_norm': 'delta_w', 'delta_ffn1_w_in': 'delta_w', 'delta_ffn1_w_out': 'delta_w', 'delta_mix_norm': 'delta_w', 'delta_w_mix_in': 'delta_w', 'delta_gmlp_v_norm': 'delta_w', 'delta_gmlp_w_s': 'delta_w', 'delta_gmlp_b': 'delta_w', 'delta_w_mix_out': 'delta_w', 'delta_ffn2_norm': 'delta_w', 'delta_ffn2_w_in': 'delta_w', 'delta_ffn2_w_out': 'delta_w', 'delta_ple_norm': 'delta_w', 'delta_ple_w_gate': 'delta_w', 'delta_ple_w_proj': 'delta_w', 'delta_final_norm': 'delta_w', 'new_m_ffn1_norm': 'new_m', 'new_m_ffn1_w_in': 'new_m', 'new_m_ffn1_w_out': 'new_m', 'new_m_mix_norm': 'new_m', 'new_m_w_mix_in': 'new_m', 'new_m_gmlp_v_norm': 'new_m', 'new_m_gmlp_w_s': 'new_m', 'new_m_gmlp_b': 'new_m', 'new_m_w_mix_out': 'new_m', 'new_m_ffn2_norm': 'new_m', 'new_m_ffn2_w_in': 'new_m', 'new_m_ffn2_w_out': 'new_m', 'new_m_ple_norm': 'new_m', 'new_m_ple_w_gate': 'new_m', 'new_m_ple_w_proj': 'new_m', 'new_m_final_norm': 'new_m', 'new_v_ffn1_norm': 'new_v', 'new_v_ffn1_w_in': 'new_v', 'new_v_ffn1_w_out': 'new_v', 'new_v_mix_norm': 'new_v', 'new_v_w_mix_in': 'new_v', 'new_v_gmlp_v_norm': 'new_v', 'new_v_gmlp_w_s': 'new_v', 'new_v_gmlp_b': 'new_v', 'new_v_w_mix_out': 'new_v', 'new_v_ffn2_norm': 'new_v', 'new_v_ffn2_w_in': 'new_v', 'new_v_ffn2_w_out': 'new_v', 'new_v_ple_norm': 'new_v', 'new_v_ple_w_gate': 'new_v', 'new_v_ple_w_proj': 'new_v', 'new_v_final_norm': 'new_v'}


def _forward(args):
    return _fwd_reference(*[args[k] for k in FWD_PARAMS])


def _output_shape():
    out = _jax.eval_shape(lambda: _forward(_fwd_setup_inputs(0)))
    return out.shape, out.dtype

N_MICROBATCH = 1
ADAM_LR = 0.001
ADAM_B1 = 0.9
ADAM_B2 = 0.999
ADAM_EPS = 1e-08
ADAM_WD = 0.01
ADAM_STEP = 10
PER_EXAMPLE_BATCH_AXIS = {'x': 0, 'p': 1, 'loss_target': 0}
SHARED_INPUTS = []
_WEIGHT_DTYPES = {'ffn1_norm': _jnp.float32, 'ffn1_w_in': _jnp.float32, 'ffn1_w_out': _jnp.float32, 'mix_norm': _jnp.float32, 'w_mix_in': _jnp.float32, 'gmlp_v_norm': _jnp.float32, 'gmlp_w_s': _jnp.float32, 'gmlp_b': _jnp.float32, 'w_mix_out': _jnp.float32, 'ffn2_norm': _jnp.float32, 'ffn2_w_in': _jnp.float32, 'ffn2_w_out': _jnp.float32, 'ple_norm': _jnp.float32, 'ple_w_gate': _jnp.float32, 'ple_w_proj': _jnp.float32, 'final_norm': _jnp.float32}
MOMENT_SCALE = {'ffn1_norm': 7.834267e-02, 'ffn1_w_in': 3.222374e-02, 'ffn1_w_out': 5.267445e-02, 'mix_norm': 1.219017e-01, 'w_mix_in': 7.448759e-02, 'gmlp_v_norm': 6.183799e-02, 'gmlp_w_s': 6.182761e-02, 'gmlp_b': 9.056598e-02, 'w_mix_out': 1.057620e-01, 'ffn2_norm': 6.043948e-02, 'ffn2_w_in': 2.362679e-02, 'ffn2_w_out': 3.864394e-02, 'ple_norm': 2.947740e-02, 'ple_w_gate': 2.956649e-02, 'ple_w_proj': 6.919639e-02, 'final_norm': 3.214503e+01}


def _to_microbatches(a, axis):
    t = _jnp.moveaxis(a, axis, 0)
    t = t.reshape((N_MICROBATCH, t.shape[0] // N_MICROBATCH) + t.shape[1:])
    return _jnp.moveaxis(t, 1, axis + 1)


def setup_inputs(seed: int = 0) -> dict:
    inp = _fwd_setup_inputs(seed)
    key = _jax.random.fold_in(_jax.random.key(seed), 7919)
    shape, _ = _output_shape()
    out = dict(inp)
    out["loss_target"] = _jax.random.normal(_jax.random.fold_in(key, 0), shape, _jnp.float32)
    for i, name in enumerate(TWIN_WEIGHTS):
        w = inp[name].astype(_jnp.float32)
        if MOMENT_SCALE is None:
            s = _jnp.sqrt(_jnp.mean(_jnp.square(w)) + 1e-30)
        else:
            s = MOMENT_SCALE[name]
        km, kv = _jax.random.split(_jax.random.fold_in(key, i + 1))
        out[name] = w
        out["m_" + name] = s * _jax.random.normal(km, w.shape, _jnp.float32)
        out["v_" + name] = (s * s) * _jax.random.uniform(kv, w.shape, _jnp.float32, 0.5, 1.5)
    if N_MICROBATCH > 1:
        for name, axis in PER_EXAMPLE_BATCH_AXIS.items():
            out[name] = _to_microbatches(out[name], axis)
    return {'x': out['x'], 'p': out['p'], 'ffn1_norm': out['ffn1_norm'], 'ffn1_w_in': out['ffn1_w_in'], 'ffn1_w_out': out['ffn1_w_out'], 'mix_norm': out['mix_norm'], 'w_mix_in': out['w_mix_in'], 'gmlp_v_norm': out['gmlp_v_norm'], 'gmlp_w_s': out['gmlp_w_s'], 'gmlp_b': out['gmlp_b'], 'w_mix_out': out['w_mix_out'], 'ffn2_norm': out['ffn2_norm'], 'ffn2_w_in': out['ffn2_w_in'], 'ffn2_w_out': out['ffn2_w_out'], 'ple_norm': out['ple_norm'], 'ple_w_gate': out['ple_w_gate'], 'ple_w_proj': out['ple_w_proj'], 'final_norm': out['final_norm'], 'loss_target': out['loss_target'], 'm_ffn1_norm': out['m_ffn1_norm'], 'm_ffn1_w_in': out['m_ffn1_w_in'], 'm_ffn1_w_out': out['m_ffn1_w_out'], 'm_mix_norm': out['m_mix_norm'], 'm_w_mix_in': out['m_w_mix_in'], 'm_gmlp_v_norm': out['m_gmlp_v_norm'], 'm_gmlp_w_s': out['m_gmlp_w_s'], 'm_gmlp_b': out['m_gmlp_b'], 'm_w_mix_out': out['m_w_mix_out'], 'm_ffn2_norm': out['m_ffn2_norm'], 'm_ffn2_w_in': out['m_ffn2_w_in'], 'm_ffn2_w_out': out['m_ffn2_w_out'], 'm_ple_norm': out['m_ple_norm'], 'm_ple_w_gate': out['m_ple_w_gate'], 'm_ple_w_proj': out['m_ple_w_proj'], 'm_final_norm': out['m_final_norm'], 'v_ffn1_norm': out['v_ffn1_norm'], 'v_ffn1_w_in': out['v_ffn1_w_in'], 'v_ffn1_w_out': out['v_ffn1_w_out'], 'v_mix_norm': out['v_mix_norm'], 'v_w_mix_in': out['v_w_mix_in'], 'v_gmlp_v_norm': out['v_gmlp_v_norm'], 'v_gmlp_w_s': out['v_gmlp_w_s'], 'v_gmlp_b': out['v_gmlp_b'], 'v_w_mix_out': out['v_w_mix_out'], 'v_ffn2_norm': out['v_ffn2_norm'], 'v_ffn2_w_in': out['v_ffn2_w_in'], 'v_ffn2_w_out': out['v_ffn2_w_out'], 'v_ple_norm': out['v_ple_norm'], 'v_ple_w_gate': out['v_ple_w_gate'], 'v_ple_w_proj': out['v_ple_w_proj'], 'v_final_norm': out['v_final_norm']}


def _loss(weights, diff, rest, loss_target):
    with _jax.named_scope("forward"):
        args = {**rest, TWIN_DIFF_INPUT: diff, **{k: w.astype(_WEIGHT_DTYPES[k]) for k, w in weights.items()}}
        y = _forward(args)
    with _jax.named_scope("loss_head"):
        err = _jnp.square(y.astype(_jnp.float32) - loss_target)
        return 0.5 * _jnp.sum(_jnp.mean(err, axis=-1)) if err.ndim else 0.5 * err


def _adamw(w, g, m, v):
    m = ADAM_B1 * m + (1.0 - ADAM_B1) * g
    v = ADAM_B2 * v + (1.0 - ADAM_B2) * _jnp.square(g)
    m_hat = m / (1.0 - ADAM_B1 ** ADAM_STEP)
    v_hat = v / (1.0 - ADAM_B2 ** ADAM_STEP)
    delta = -ADAM_LR * (m_hat / (_jnp.sqrt(v_hat) + ADAM_EPS) + ADAM_WD * w)
    return delta, m, v


def reference(x, p, ffn1_norm, ffn1_w_in, ffn1_w_out, mix_norm, w_mix_in, gmlp_v_norm, gmlp_w_s, gmlp_b, w_mix_out, ffn2_norm, ffn2_w_in, ffn2_w_out, ple_norm, ple_w_gate, ple_w_proj, final_norm, loss_target, m_ffn1_norm, m_ffn1_w_in, m_ffn1_w_out, m_mix_norm, m_w_mix_in, m_gmlp_v_norm, m_gmlp_w_s, m_gmlp_b, m_w_mix_out, m_ffn2_norm, m_ffn2_w_in, m_ffn2_w_out, m_ple_norm, m_ple_w_gate, m_ple_w_proj, m_final_norm, v_ffn1_norm, v_ffn1_w_in, v_ffn1_w_out, v_mix_norm, v_w_mix_in, v_gmlp_v_norm, v_gmlp_w_s, v_gmlp_b, v_w_mix_out, v_ffn2_norm, v_ffn2_w_in, v_ffn2_w_out, v_ple_norm, v_ple_w_gate, v_ple_w_proj, v_final_norm):
    given = dict(x=x, p=p, ffn1_norm=ffn1_norm, ffn1_w_in=ffn1_w_in, ffn1_w_out=ffn1_w_out, mix_norm=mix_norm, w_mix_in=w_mix_in, gmlp_v_norm=gmlp_v_norm, gmlp_w_s=gmlp_w_s, gmlp_b=gmlp_b, w_mix_out=w_mix_out, ffn2_norm=ffn2_norm, ffn2_w_in=ffn2_w_in, ffn2_w_out=ffn2_w_out, ple_norm=ple_norm, ple_w_gate=ple_w_gate, ple_w_proj=ple_w_proj, final_norm=final_norm, loss_target=loss_target, m_ffn1_norm=m_ffn1_norm, m_ffn1_w_in=m_ffn1_w_in, m_ffn1_w_out=m_ffn1_w_out, m_mix_norm=m_mix_norm, m_w_mix_in=m_w_mix_in, m_gmlp_v_norm=m_gmlp_v_norm, m_gmlp_w_s=m_gmlp_w_s, m_gmlp_b=m_gmlp_b, m_w_mix_out=m_w_mix_out, m_ffn2_norm=m_ffn2_norm, m_ffn2_w_in=m_ffn2_w_in, m_ffn2_w_out=m_ffn2_w_out, m_ple_norm=m_ple_norm, m_ple_w_gate=m_ple_w_gate, m_ple_w_proj=m_ple_w_proj, m_final_norm=m_final_norm, v_ffn1_norm=v_ffn1_norm, v_ffn1_w_in=v_ffn1_w_in, v_ffn1_w_out=v_ffn1_w_out, v_mix_norm=v_mix_norm, v_w_mix_in=v_w_mix_in, v_gmlp_v_norm=v_gmlp_v_norm, v_gmlp_w_s=v_gmlp_w_s, v_gmlp_b=v_gmlp_b, v_w_mix_out=v_w_mix_out, v_ffn2_norm=v_ffn2_norm, v_ffn2_w_in=v_ffn2_w_in, v_ffn2_w_out=v_ffn2_w_out, v_ple_norm=v_ple_norm, v_ple_w_gate=v_ple_w_gate, v_ple_w_proj=v_ple_w_proj, v_final_norm=v_final_norm)
    weights = {n: given[n] for n in TWIN_WEIGHTS}
    shared = {n: given[n] for n in SHARED_INPUTS}
    per_example = {n: given[n] for n in ['x', 'p']}
    grad_fn = _jax.value_and_grad(_loss, argnums=(0, 1))

    def one_microbatch(ex, loss_target):
        ex = dict(ex)
        diff = ex.pop(TWIN_DIFF_INPUT)
        return grad_fn(weights, diff, {**shared, **ex}, loss_target)

    if N_MICROBATCH == 1:
        loss, (grad_w, grad_x) = one_microbatch(per_example, given["loss_target"])
    else:
        def body(carry, xs):
            loss_sum, grad_sum = carry
            l_k, (gw_k, gx_k) = one_microbatch(xs[0], xs[1])
            with _jax.named_scope("update"):
                return (loss_sum + l_k, _jax.tree.map(_jnp.add, grad_sum, gw_k)), gx_k

        init = (_jnp.zeros((), _jnp.float32), _jax.tree.map(_jnp.zeros_like, weights))
        (loss, grad_w), grad_x = _jax.lax.scan(body, init, (per_example, given["loss_target"]))
    with _jax.named_scope("update"):
        delta_w, new_m, new_v = {}, {}, {}
        for n in TWIN_WEIGHTS:
            delta_w[n], new_m[n], new_v[n] = _adamw(weights[n], grad_w[n], given["m_" + n], given["v_" + n])
    return (loss, grad_x, *[grad_w[n] for n in TWIN_WEIGHTS], *[delta_w[n] for n in TWIN_WEIGHTS],
            *[new_m[n] for n in TWIN_WEIGHTS], *[new_v[n] for n in TWIN_WEIGHTS])
```

```python
import functools

import jax
import jax.numpy as jnp
from jax import lax
from jax.experimental import pallas as pl
from jax.experimental.pallas import tpu as pltpu

F32 = jnp.float32
BF16 = jnp.bfloat16
MESH = pl.DeviceIdType.MESH

N_DEV = 8
EPS = 1e-6
ADAM_LR = 0.001
ADAM_B1 = 0.9
ADAM_B2 = 0.999
ADAM_EPS = 1e-08
ADAM_WD = 0.01
ADAM_STEP = 10

GM_WIDTH = 512
GM_HEADS = 4
CHUNK = 128
SB_WIDTH = 512
SB_HEAD_DIM = 64
SB_SCALE = 0.125
LANES = 128
SMALL_ROWS_ALIGN = 8

ROW_TILE = 512
ATTN_BLOCK = 128
VMEM_LIMIT = 56 * 1024 * 1024


def _params(*sem):
    return pltpu.CompilerParams(dimension_semantics=sem, vmem_limit_bytes=VMEM_LIMIT)


def _dot(a, b):
    return jnp.dot(a, b, preferred_element_type=F32)


def _dot_nt(a, b):
    return lax.dot_general(a, b, (((1,), (1,)), ((), ())), preferred_element_type=F32)


def _dot_tn(a, b):
    return lax.dot_general(a, b, (((0,), (0,)), ((), ())), preferred_element_type=F32)


def _rms_parts(x):
    r = lax.rsqrt(jnp.mean(x * x, axis=-1, keepdims=True) + EPS)
    return r, x * r


def _rms_bwd(x, g, dy):
    r, xh = _rms_parts(x)
    dyg = dy * g
    dx = r * (dyg - xh * jnp.mean(dyg * xh, axis=-1, keepdims=True))
    return dx, jnp.sum(dy * xh, axis=0, keepdims=True)


def _sigmoid(x):
    return 1.0 / (1.0 + jnp.exp(-x))


_SQRT_HALF = 0.7071067811865476
_INV_SQRT_2PI = 0.3989422804014327


def _gelu(x):
    return 0.5 * x * (1.0 + lax.erf(x * _SQRT_HALF))


def _gelu_grad(x):
    return 0.5 * (1.0 + lax.erf(x * _SQRT_HALF)) + x * (_INV_SQRT_2PI * jnp.exp(-0.5 * x * x))


def _split_bf16(x):
    hi = x.astype(BF16)
    lo = (x - hi.astype(F32)).astype(BF16)
    return hi, lo


def _ffn_fwd(name, h, gain, w_in, w_out):
    T, D = h.shape
    nb, _, FB = w_in.shape
    nh = nb // 2
    tm = min(ROW_TILE, T)

    def body(h_ref, g_ref, wg_ref, wu_ref, wo_ref, ho_ref, n_ref, G_ref, U_ref, a_ref, n_s, acc):
        jj = pl.program_id(1)

        @pl.when(jj == 0)
        def _():
            _, xh = _rms_parts(h_ref[...])
            n = (xh * g_ref[...]).astype(BF16)
            n_s[...] = n
            n_ref[...] = n
            acc[...] = jnp.zeros_like(acc)

        n = n_s[...]
        G = _dot(n, wg_ref[0])
        U = _dot(n, wu_ref[0])
        G_ref[0] = G
        U_ref[0] = U
        a = (G * _sigmoid(G) * U).astype(BF16)
        a_ref[0] = a
        acc[...] += _dot(a, wo_ref[...])

        @pl.when(jj == nh - 1)
        def _():
            ho_ref[...] = h_ref[...] + 0.5 * acc[...]

    row = lambda i, j: (i, 0)
    blk = lambda i, j: (j, i, 0)
    return pl.pallas_call(
        body, name=name, grid=(T // tm, nh),
        in_specs=[pl.BlockSpec((tm, D), row),
                  pl.BlockSpec((1, D), lambda i, j: (0, 0)),
                  pl.BlockSpec((1, D, FB), lambda i, j: (j, 0, 0)),
                  pl.BlockSpec((1, D, FB), lambda i, j: (j + nh, 0, 0)),
                  pl.BlockSpec((FB, D), lambda i, j: (j, 0))],
        out_specs=[pl.BlockSpec((tm, D), row), pl.BlockSpec((tm, D), row),
                   pl.BlockSpec((1, tm, FB), blk), pl.BlockSpec((1, tm, FB), blk),
                   pl.BlockSpec((1, tm, FB), blk)],
        out_shape=[jax.ShapeDtypeStruct((T, D), F32), jax.ShapeDtypeStruct((T, D), BF16),
                   jax.ShapeDtypeStruct((nh, T, FB), F32), jax.ShapeDtypeStruct((nh, T, FB), F32),
                   jax.ShapeDtypeStruct((nh, T, FB), BF16)],
        scratch_shapes=[pltpu.VMEM((tm, D), BF16), pltpu.VMEM((tm, D), F32)],
        compiler_params=_params("arbitrary", "arbitrary"),
    )(h, gain, w_in, w_in, w_out)


def _ffn_bwd(name, dh, h_in, gain, G, U, w_in, w_out):
    T, D = dh.shape
    nb, _, FB = w_in.shape
    nh = nb // 2
    tm = min(ROW_TILE, T)

    def body(dh_ref, h_ref, g_ref, G_ref, U_ref, wg_ref, wu_ref, wo_ref,
             dhin_ref, dg_ref, dG_ref, dU_ref, do_ref, dn_acc, do_s):
        i = pl.program_id(0)
        jj = pl.program_id(1)

        @pl.when(jj == 0)
        def _():
            d_out = (0.5 * dh_ref[...]).astype(BF16)
            do_s[...] = d_out
            do_ref[...] = d_out
            dn_acc[...] = jnp.zeros_like(dn_acc)

        @pl.when((i == 0) & (jj == 0))
        def _():
            dg_ref[...] = jnp.zeros_like(dg_ref)

        da = _dot_nt(do_s[...], wo_ref[...])
        Gv = G_ref[0]
        Uv = U_ref[0]
        sig = _sigmoid(Gv)
        dU = (da * (Gv * sig)).astype(BF16)
        dG = (da * Uv * (sig * (1.0 + Gv * (1.0 - sig)))).astype(BF16)
        dG_ref[0] = dG
        dU_ref[0] = dU
        dn_acc[...] += _dot_nt(dG, wg_ref[0]) + _dot_nt(dU, wu_ref[0])

        @pl.when(jj == nh - 1)
        def _():
            dx, dg = _rms_bwd(h_ref[...], g_ref[...], dn_acc[...])
            dhin_ref[...] = dh_ref[...] + dx
            dg_ref[...] += dg

    row = lambda i, j: (i, 0)
    blk = lambda i, j: (j, i, 0)
    one = lambda i, j: (0, 0)
    return pl.pallas_call(
        body, name=name, grid=(T // tm, nh),
        in_specs=[pl.BlockSpec((tm, D), row), pl.BlockSpec((tm, D), row), pl.BlockSpec((1, D), one),
                  pl.BlockSpec((1, tm, FB), blk), pl.BlockSpec((1, tm, FB), blk),
                  pl.BlockSpec((1, D, FB), lambda i, j: (j, 0, 0)),
                  pl.BlockSpec((1, D, FB), lambda i, j: (j + nh, 0, 0)),
                  pl.BlockSpec((FB, D), lambda i, j: (j, 0))],
        out_specs=[pl.BlockSpec((tm, D), row), pl.BlockSpec((1, D), one),
                   pl.BlockSpec((1, tm, FB), blk), pl.BlockSpec((1, tm, FB), blk),
                   pl.BlockSpec((tm, D), row)],
        out_shape=[jax.ShapeDtypeStruct((T, D), F32), jax.ShapeDtypeStruct((1, D), F32),
                   jax.ShapeDtypeStruct((nh, T, FB), BF16), jax.ShapeDtypeStruct((nh, T, FB), BF16),
                   jax.ShapeDtypeStruct((T, D), BF16)],
        scratch_shapes=[pltpu.VMEM((tm, D), F32), pltpu.VMEM((tm, D), BF16)],
        compiler_params=_params("arbitrary", "arbitrary"),
    )(dh, h_in, gain, G, U, w_in, w_in, w_out)


def _matmul_tn(name, a, b, nj, a_block, a_map, b_block, b_map, out_shape, out_block, out_map):
    T = a.shape[-2]
    tt = a_block[-2]
    nt = T // tt
    kb, nbk = out_block[-2], out_block[-1]

    def body(a_ref, b_ref, o_ref, acc):
        t = pl.program_id(1)

        @pl.when(t == 0)
        def _():
            acc[...] = jnp.zeros_like(acc)

        av = a_ref[0] if len(a_block) == 3 else a_ref[...]
        bv = b_ref[0] if len(b_block) == 3 else b_ref[...]
        acc[...] += _dot_tn(av, bv)

        @pl.when(t == nt - 1)
        def _():
            if len(out_block) == 3:
                o_ref[0] = acc[...].astype(o_ref.dtype)
            else:
                o_ref[...] = acc[...].astype(o_ref.dtype)

    return pl.pallas_call(
        body, name=name, grid=(nj, nt),
        in_specs=[pl.BlockSpec(a_block, a_map), pl.BlockSpec(b_block, b_map)],
        out_specs=pl.BlockSpec(out_block, out_map),
        out_shape=jax.ShapeDtypeStruct(out_shape, BF16),
        scratch_shapes=[pltpu.VMEM((kb, nbk), F32)],
        compiler_params=_params("arbitrary", "arbitrary"),
    )(a, b)


def _mix_in_fwd(h, gain, w):
    T, D = h.shape
    W = w.shape[1]
    nuv = 2 * GM_WIDTH
    tm = min(ROW_TILE, T)

    def body(h_ref, g_ref, w_ref, n_ref, zuv_ref, qkv_ref):
        _, xh = _rms_parts(h_ref[...])
        n = (xh * g_ref[...]).astype(BF16)
        n_ref[...] = n
        z = _dot(n, w_ref[...])
        zuv_ref[...] = z[:, :nuv]
        qkv_ref[...] = z[:, nuv:].astype(BF16)

    row = lambda i: (i, 0)
    return pl.pallas_call(
        body, name="mix_in_fwd", grid=(T // tm,),
        in_specs=[pl.BlockSpec((tm, D), row), pl.BlockSpec((1, D), lambda i: (0, 0)),
                  pl.BlockSpec((D, W), lambda i: (0, 0))],
        out_specs=[pl.BlockSpec((tm, D), row), pl.BlockSpec((tm, nuv), row),
                   pl.BlockSpec((tm, W - nuv), row)],
        out_shape=[jax.ShapeDtypeStruct((T, D), BF16), jax.ShapeDtypeStruct((T, nuv), F32),
                   jax.ShapeDtypeStruct((T, W - nuv), BF16)],
        compiler_params=_params("arbitrary"),
    )(h, gain, w)


def _mix_in_bwd(dzuv, dqkv, w, h, gain, dh):
    T, D = h.shape
    W = w.shape[1]
    nuv = dzuv.shape[1]
    tm = min(ROW_TILE, T)

    def body(dzuv_ref, dqkv_ref, w_ref, h_ref, g_ref, dh_ref, dhin_ref, dg_ref):
        @pl.when(pl.program_id(0) == 0)
        def _():
            dg_ref[...] = jnp.zeros_like(dg_ref)

        dn = _dot_nt(dzuv_ref[...], w_ref[:, :nuv]) + _dot_nt(dqkv_ref[...], w_ref[:, nuv:])
        dx, dg = _rms_bwd(h_ref[...], g_ref[...], dn)
        dhin_ref[...] = dh_ref[...] + dx
        dg_ref[...] += dg

    row = lambda i: (i, 0)
    one = lambda i: (0, 0)
    return pl.pallas_call(
        body, name="mix_in_bwd", grid=(T // tm,),
        in_specs=[pl.BlockSpec((tm, nuv), row), pl.BlockSpec((tm, W - nuv), row),
                  pl.BlockSpec((D, W), one), pl.BlockSpec((tm, D), row), pl.BlockSpec((1, D), one),
                  pl.BlockSpec((tm, D), row)],
        out_specs=[pl.BlockSpec((tm, D), row), pl.BlockSpec((1, D), one)],
        out_shape=[jax.ShapeDtypeStruct((T, D), F32), jax.ShapeDtypeStruct((1, D), F32)],
        compiler_params=_params("arbitrary"),
    )(dzuv, dqkv, w, h, gain, dh)


def _gmlp_norm(zv, gv):
    v = _gelu(zv)
    r, vh = _rms_parts(v)
    return r, vh, (vh * gv).astype(BF16)


def _causal_ws(ws_ref, hd):
    r = lax.broadcasted_iota(jnp.int32, (CHUNK, CHUNK), 0)
    c = lax.broadcasted_iota(jnp.int32, (CHUNK, CHUNK), 1)
    return jnp.where(r >= c, ws_ref[hd], 0.0).astype(BF16)


def _gmlp_fwd(zuv, gv, ws, b_t):
    T = zuv.shape[0]
    tg = min(ROW_TILE, T)

    def body(zu_ref, zv_ref, gv_ref, ws_ref, bt_ref, o_ref):
        u = _gelu(zu_ref[...])
        _, _, vn = _gmlp_norm(zv_ref[...], gv_ref[...])
        for hd in range(GM_HEADS):
            wc = _causal_ws(ws_ref, hd)
            cols = slice(hd * CHUNK, (hd + 1) * CHUNK)
            for c in range(tg // CHUNK):
                rows = slice(c * CHUNK, (c + 1) * CHUNK)
                sv = _dot(wc, vn[rows, cols]) + bt_ref[:, hd:hd + 1]
                o_ref[rows, cols] = (u[rows, cols] * sv).astype(BF16)

    return pl.pallas_call(
        body, name="gmlp_fwd", grid=(T // tg,),
        in_specs=[pl.BlockSpec((tg, GM_WIDTH), lambda i: (i, 0)), pl.BlockSpec((tg, GM_WIDTH), lambda i: (i, 1)),
                  pl.BlockSpec((1, GM_WIDTH), lambda i: (0, 0)),
                  pl.BlockSpec((GM_HEADS, CHUNK, CHUNK), lambda i: (0, 0, 0)),
                  pl.BlockSpec((CHUNK, GM_HEADS), lambda i: (0, 0))],
        out_specs=pl.BlockSpec((tg, GM_WIDTH), lambda i: (i, 0)),
        out_shape=jax.ShapeDtypeStruct((T, GM_WIDTH), BF16),
        compiler_params=_params("arbitrary"),
    )(zuv, zuv, gv, ws, b_t)


def _gmlp_bwd(zuv, d_gm, gv, ws, b_t):
    T = zuv.shape[0]
    tg = min(ROW_TILE, T)
    ng = T // tg

    def body(zu_ref, zv_ref, dgm_ref, gv_ref, ws_ref, bt_ref, dz_ref, dgv_ref, dws_ref, dbt_ref, dsv_acc, dvn_s):
        i = pl.program_id(0)

        @pl.when(i == 0)
        def _():
            dgv_ref[...] = jnp.zeros_like(dgv_ref)
            dws_ref[...] = jnp.zeros_like(dws_ref)
            dsv_acc[...] = jnp.zeros_like(dsv_acc)

        zu = zu_ref[...]
        zv = zv_ref[...]
        dgm = dgm_ref[...]
        gvv = gv_ref[...]
        u = _gelu(zu)
        rv, vh, vn = _gmlp_norm(zv, gvv)
        dsv = dgm * u
        dsv_b = dsv.astype(BF16)
        for hd in range(GM_HEADS):
            wc = _causal_ws(ws_ref, hd)
            cols = slice(hd * CHUNK, (hd + 1) * CHUNK)
            dws = jnp.zeros((CHUNK, CHUNK), F32)
            dsv_sum = jnp.zeros((CHUNK, CHUNK), F32)
            for c in range(tg // CHUNK):
                rows = slice(c * CHUNK, (c + 1) * CHUNK)
                vch = vn[rows, cols]
                sv = _dot(wc, vch) + bt_ref[:, hd:hd + 1]
                dz_ref[rows, cols] = (dgm[rows, cols] * sv * _gelu_grad(zu[rows, cols])).astype(BF16)
                dws += _dot_nt(dsv_b[rows, cols], vch)
                dsv_sum += dsv[rows, cols]
                dvn_s[rows, cols] = _dot_tn(wc, dsv_b[rows, cols])
            dws_ref[hd] += dws
            dsv_acc[:, cols] += dsv_sum
        dvn = dvn_s[...]
        dvh = dvn * gvv
        dv = rv * (dvh - vh * jnp.mean(dvh * vh, axis=-1, keepdims=True))
        dgv_ref[...] += jnp.sum(dvn * vh, axis=0, keepdims=True)
        dz_ref[:, GM_WIDTH:] = (dv * _gelu_grad(zv)).astype(BF16)

        @pl.when(i == ng - 1)
        def _():
            r = lax.broadcasted_iota(jnp.int32, (CHUNK, CHUNK), 0)
            c = lax.broadcasted_iota(jnp.int32, (CHUNK, CHUNK), 1)
            for hd in range(GM_HEADS):
                dws_ref[hd] = jnp.where(r >= c, dws_ref[hd], 0.0)
                dbt_ref[:, hd:hd + 1] = jnp.sum(dsv_acc[:, hd * CHUNK:(hd + 1) * CHUNK], axis=1, keepdims=True)

    return pl.pallas_call(
        body, name="gmlp_bwd", grid=(ng,),
        in_specs=[pl.BlockSpec((tg, GM_WIDTH), lambda i: (i, 0)), pl.BlockSpec((tg, GM_WIDTH), lambda i: (i, 1)),
                  pl.BlockSpec((tg, GM_WIDTH), lambda i: (i, 0)),
                  pl.BlockSpec((1, GM_WIDTH), lambda i: (0, 0)),
                  pl.BlockSpec((GM_HEADS, CHUNK, CHUNK), lambda i: (0, 0, 0)),
                  pl.BlockSpec((CHUNK, GM_HEADS), lambda i: (0, 0))],
        out_specs=[pl.BlockSpec((tg, 2 * GM_WIDTH), lambda i: (i, 0)),
                   pl.BlockSpec((1, GM_WIDTH), lambda i: (0, 0)),
                   pl.BlockSpec((GM_HEADS, CHUNK, CHUNK), lambda i: (0, 0, 0)),
                   pl.BlockSpec((CHUNK, GM_HEADS), lambda i: (0, 0))],
        out_shape=[jax.ShapeDtypeStruct((T, 2 * GM_WIDTH), BF16), jax.ShapeDtypeStruct((1, GM_WIDTH), F32),
                   jax.ShapeDtypeStruct((GM_HEADS, CHUNK, CHUNK), F32),
                   jax.ShapeDtypeStruct((CHUNK, GM_HEADS), F32)],
        scratch_shapes=[pltpu.VMEM((CHUNK, GM_WIDTH), F32), pltpu.VMEM((tg, GM_WIDTH), F32)],
        compiler_params=_params("arbitrary"),
    )(zuv, zuv, d_gm, gv, ws, b_t)


def _scan_matrix(blk, keep):
    r = lax.broadcasted_iota(jnp.int32, (blk, 2 * blk), 0)
    c = lax.broadcasted_iota(jnp.int32, (blk, 2 * blk), 1)
    return jnp.where((c >= blk) | keep(r, c), 1.0, 0.0).astype(BF16)


def _scan(x, mat):
    hi, lo = _split_bf16(x)
    return _dot(hi, mat) + _dot(lo, mat)


def _head_masks(q):
    lane = lax.broadcasted_iota(jnp.int32, q.shape, 1)
    m0 = lane < SB_HEAD_DIM
    zero = jnp.zeros_like(q)
    return m0, jnp.where(m0, q, zero), jnp.where(m0, zero, q)


def _attn_fwd(qkv):
    T = qkv.shape[0]
    blk = min(ATTN_BLOCK, T)
    nq = T // blk
    ngrp = SB_WIDTH // LANES

    def body(q_ref, k_ref, v_ref, o_ref, l_ref, acc, run):
        i = pl.program_id(1)
        suffix = _scan_matrix(blk, lambda r, c: r > c)
        row = lax.broadcasted_iota(jnp.int32, (blk, blk), 0)
        col = lax.broadcasted_iota(jnp.int32, (blk, blk), 1)
        causal = col < row
        m0, q0, q1 = _head_masks(q_ref[...] * SB_SCALE)
        acc[...] = jnp.zeros_like(acc)
        run[...] = jnp.zeros_like(run)

        def tile(j, diag):
            start = pl.multiple_of(j * blk, blk)
            kj = k_ref[pl.ds(start, blk), :]
            vj = v_ref[pl.ds(start, blk), :]
            for hd, qh in enumerate((q0, q1)):
                z = _dot_nt(qh, kj)
                sp = jnp.maximum(z, 0.0) + jnp.log(1.0 + jnp.exp(-jnp.abs(z)))
                lm = jnp.where(causal, -sp, 0.0) if diag else -sp
                res = _scan(lm, suffix)
                a = jnp.exp(z - sp + run[hd] + res[:, :blk])
                if diag:
                    a = jnp.where(causal, a, 0.0)
                acc[hd] += _dot(a.astype(BF16), vj)
                run[hd] += res[:, blk:]

        tile(i, True)

        def step(it, carry):
            tile(i - 1 - it, False)
            return carry

        lax.fori_loop(0, i, step, 0)
        o_ref[...] = jnp.where(m0, acc[0], acc[1]).astype(BF16)
        l_ref[...] = jnp.where(m0, run[0][:, :LANES], run[1][:, :LANES])

    return pl.pallas_call(
        body, name="attn_fwd", grid=(ngrp, nq),
        in_specs=[pl.BlockSpec((blk, LANES), lambda g, i: (i, g)),
                  pl.BlockSpec((T, LANES), lambda g, i: (0, ngrp + g)),
                  pl.BlockSpec((T, LANES), lambda g, i: (0, 2 * ngrp + g))],
        out_specs=[pl.BlockSpec((blk, LANES), lambda g, i: (i, g)),
                   pl.BlockSpec((blk, LANES), lambda g, i: (i, g))],
        out_shape=[jax.ShapeDtypeStruct((T, SB_WIDTH), BF16), jax.ShapeDtypeStruct((T, SB_WIDTH), F32)],
        scratch_shapes=[pltpu.VMEM((2, blk, LANES), F32), pltpu.VMEM((2, blk, blk), F32)],
        compiler_params=_params("arbitrary", "arbitrary"),
    )(qkv, qkv, qkv)


def _attn_bwd(qkv, d_o, ltot):
    T = qkv.shape[0]
    blk = min(ATTN_BLOCK, T)
    nq = T // blk
    ngrp = SB_WIDTH // LANES

    def body(q_ref, k_ref, v_ref, do_ref, l_ref, dq_ref, dk_ref, dv_ref, dq_acc, dk_acc, dv_acc, lpre, ppre):
        i = pl.program_id(1)

        @pl.when(i == 0)
        def _():
            dk_acc[...] = jnp.zeros_like(dk_acc)
            dv_acc[...] = jnp.zeros_like(dv_acc)

        incl = _scan_matrix(blk, lambda r, c: r <= c)
        excl = _scan_matrix(blk, lambda r, c: r < c)
        row = lax.broadcasted_iota(jnp.int32, (blk, blk), 0)
        col = lax.broadcasted_iota(jnp.int32, (blk, blk), 1)
        causal = col < row
        m0, q0, q1 = _head_masks(q_ref[...] * SB_SCALE)
        _, d0, d1 = _head_masks(do_ref[...].astype(BF16))
        lt = l_ref[...]
        ltots = (lt[:, 0:1], lt[:, SB_HEAD_DIM:SB_HEAD_DIM + 1])
        dq_acc[...] = jnp.zeros_like(dq_acc)
        lpre[...] = jnp.zeros_like(lpre)
        ppre[...] = jnp.zeros_like(ppre)

        def tile(j, diag):
            start = pl.multiple_of(j * blk, blk)
            kj = k_ref[pl.ds(start, blk), :]
            vj = v_ref[pl.ds(start, blk), :]
            for hd, (qh, dh) in enumerate(((q0, d0), (q1, d1))):
                z = _dot_nt(qh, kj)
                e = jnp.exp(-jnp.abs(z))
                rinv = 1.0 / (1.0 + e)
                sp = jnp.maximum(z, 0.0) + jnp.log(1.0 + e)
                pos = z >= 0.0
                beta = jnp.where(pos, rinv, e * rinv)
                one_m_beta = jnp.where(pos, e * rinv, rinv)
                lm = jnp.where(causal, -sp, 0.0) if diag else -sp
                res = _scan(lm, incl)
                a = jnp.exp(z - sp + ((ltots[hd] - lpre[hd]) - res[:, :blk]))
                if diag:
                    a = jnp.where(causal, a, 0.0)
                p = a * _dot_nt(dh, vj)
                resp = _scan(p, excl)
                dz = p * one_m_beta - beta * (ppre[hd] + resp[:, :blk])
                if diag:
                    dz = jnp.where(causal, dz, 0.0)
                dzb = dz.astype(BF16)
                dq_acc[hd] += _dot(dzb, kj)
                dk_acc[pl.ds(start, blk), :] += _dot_tn(dzb, qh)
                dv_acc[pl.ds(start, blk), :] += _dot_tn(a.astype(BF16), dh)
                lpre[hd] += res[:, blk:]
                ppre[hd] += resp[:, blk:]

        def step(j, carry):
            tile(j, False)
            return carry

        lax.fori_loop(0, i, step, 0)
        tile(i, True)
        dq_ref[...] = (jnp.where(m0, dq_acc[0], dq_acc[1]) * SB_SCALE).astype(BF16)

        @pl.when(i == nq - 1)
        def _():
            dk_ref[...] = dk_acc[...].astype(BF16)
            dv_ref[...] = dv_acc[...].astype(BF16)

    qmap = lambda g, i: (i, g)
    return pl.pallas_call(
        body, name="attn_bwd", grid=(ngrp, nq),
        in_specs=[pl.BlockSpec((blk, LANES), qmap),
                  pl.BlockSpec((T, LANES), lambda g, i: (0, ngrp + g)),
                  pl.BlockSpec((T, LANES), lambda g, i: (0, 2 * ngrp + g)),
                  pl.BlockSpec((blk, LANES), qmap), pl.BlockSpec((blk, LANES), qmap)],
        out_specs=[pl.BlockSpec((blk, LANES), qmap),
                   pl.BlockSpec((T, LANES), lambda g, i: (0, g)),
                   pl.BlockSpec((T, LANES), lambda g, i: (0, g))],
        out_shape=[jax.ShapeDtypeStruct((T, SB_WIDTH), BF16)] * 3,
        scratch_shapes=[pltpu.VMEM((2, blk, LANES), F32), pltpu.VMEM((T, LANES), F32),
                        pltpu.VMEM((T, LANES), F32), pltpu.VMEM((2, blk, blk), F32),
                        pltpu.VMEM((2, blk, blk), F32)],
        compiler_params=_params("arbitrary", "arbitrary"),
    )(qkv, qkv, qkv, d_o, ltot)


def _mix_out_fwd(h, gm, sb, w):
    T, D = h.shape
    tm = min(ROW_TILE, T)

    def body(h_ref, gm_ref, sb_ref, w_ref, o_ref):
        o_ref[...] = h_ref[...] + _dot(gm_ref[...], w_ref[:GM_WIDTH, :]) + _dot(sb_ref[...], w_ref[GM_WIDTH:, :])

    row = lambda i: (i, 0)
    return pl.pallas_call(
        body, name="mix_out_fwd", grid=(T // tm,),
        in_specs=[pl.BlockSpec((tm, D), row), pl.BlockSpec((tm, GM_WIDTH), row), pl.BlockSpec((tm, SB_WIDTH), row),
                  pl.BlockSpec((GM_WIDTH + SB_WIDTH, D), lambda i: (0, 0))],
        out_specs=pl.BlockSpec((tm, D), row),
        out_shape=jax.ShapeDtypeStruct((T, D), F32),
        compiler_params=_params("arbitrary"),
    )(h, gm, sb, w)


def _mix_out_bwd(dh, w):
    T, D = dh.shape
    tm = min(ROW_TILE, T)

    def body(dh_ref, w_ref, dgm_ref, dsb_ref, dhb_ref):
        dhb = dh_ref[...].astype(BF16)
        dhb_ref[...] = dhb
        dgm_ref[...] = _dot_nt(dhb, w_ref[:GM_WIDTH, :])
        dsb_ref[...] = _dot_nt(dhb, w_ref[GM_WIDTH:, :])

    row = lambda i: (i, 0)
    return pl.pallas_call(
        body, name="mix_out_bwd", grid=(T // tm,),
        in_specs=[pl.BlockSpec((tm, D), row), pl.BlockSpec((GM_WIDTH + SB_WIDTH, D), lambda i: (0, 0))],
        out_specs=[pl.BlockSpec((tm, GM_WIDTH), row), pl.BlockSpec((tm, SB_WIDTH), row), pl.BlockSpec((tm, D), row)],
        out_shape=[jax.ShapeDtypeStruct((T, GM_WIDTH), F32), jax.ShapeDtypeStruct((T, SB_WIDTH), F32),
                   jax.ShapeDtypeStruct((T, D), BF16)],
        compiler_params=_params("arbitrary"),
    )(dh, w)


def _tail(h3, p, target, g_ple, g_fin, w_gate, w_proj):
    T, D = h3.shape
    PD = p.shape[1]
    tm = min(ROW_TILE, T)

    def body(h_ref, p_ref, t_ref, gp_ref, gf_ref, wg_ref, wp_ref,
             loss_ref, dh_ref, n4_ref, dgl_ref, dpp_ref, dgp_ref, dgf_ref):
        @pl.when(pl.program_id(0) == 0)
        def _():
            loss_ref[...] = jnp.zeros_like(loss_ref)
            dgp_ref[...] = jnp.zeros_like(dgp_ref)
            dgf_ref[...] = jnp.zeros_like(dgf_ref)

        h3v = h_ref[...]
        gp = gp_ref[...]
        gf = gf_ref[...]
        r3, xh3 = _rms_parts(h3v)
        n4 = (xh3 * gp).astype(BF16)
        n4_ref[...] = n4
        gate = _sigmoid(_dot(n4, wg_ref[...]))
        pp = _dot(p_ref[...], wp_ref[...])
        h4 = h3v + gate * pp
        r4, xh4 = _rms_parts(h4)
        err = xh4 * gf - t_ref[...]
        loss_ref[...] += jnp.full(loss_ref.shape, (0.5 / D) * jnp.sum(err * err), F32)
        dy = err * (1.0 / D)
        dgf_ref[...] += jnp.sum(dy * xh4, axis=0, keepdims=True)
        dyg = dy * gf
        dh4 = r4 * (dyg - xh4 * jnp.mean(dyg * xh4, axis=-1, keepdims=True))
        dpp_ref[...] = (dh4 * gate).astype(BF16)
        dgl = (dh4 * pp * gate * (1.0 - gate)).astype(BF16)
        dgl_ref[...] = dgl
        dn4 = _dot_nt(dgl, wg_ref[...])
        dgp_ref[...] += jnp.sum(dn4 * xh3, axis=0, keepdims=True)
        dn4g = dn4 * gp
        dh_ref[...] = dh4 + r3 * (dn4g - xh3 * jnp.mean(dn4g * xh3, axis=-1, keepdims=True))

    row = lambda i: (i, 0)
    one = lambda i: (0, 0)
    return pl.pallas_call(
        body, name="tail", grid=(T // tm,),
        in_specs=[pl.BlockSpec((tm, D), row), pl.BlockSpec((tm, PD), row), pl.BlockSpec((tm, D), row),
                  pl.BlockSpec((1, D), one), pl.BlockSpec((1, D), one),
                  pl.BlockSpec((D, D), one), pl.BlockSpec((PD, D), one)],
        out_specs=[pl.BlockSpec((1, LANES), one), pl.BlockSpec((tm, D), row), pl.BlockSpec((tm, D), row),
                   pl.BlockSpec((tm, D), row), pl.BlockSpec((tm, D), row),
                   pl.BlockSpec((1, D), one), pl.BlockSpec((1, D), one)],
        out_shape=[jax.ShapeDtypeStruct((1, LANES), F32), jax.ShapeDtypeStruct((T, D), F32),
                   jax.ShapeDtypeStruct((T, D), BF16), jax.ShapeDtypeStruct((T, D), BF16),
                   jax.ShapeDtypeStruct((T, D), BF16),
                   jax.ShapeDtypeStruct((1, D), F32), jax.ShapeDtypeStruct((1, D), F32)],
        compiler_params=_params("arbitrary"),
    )(h3, p, target, g_ple, g_fin, w_gate, w_proj)


def _local_step(x, p_bf, target, g1, gmix, gv, ws, b_t, g2, gple, gfin,
                w_in1, w_out1, w_mi, w_mo, w_in2, w_out2, w_gate, w_proj):
    T, D = x.shape
    nb, _, FB = w_in1.shape
    nh = nb // 2
    tm = min(ROW_TILE, T)

    h1, n1, G1, U1, a1 = _ffn_fwd("ffn1_fwd", x, g1, w_in1, w_out1)
    n2, zuv, qkv = _mix_in_fwd(h1, gmix, w_mi)
    gm = _gmlp_fwd(zuv, gv, ws, b_t)
    sb, ltot = _attn_fwd(qkv)
    h2 = _mix_out_fwd(h1, gm, sb, w_mo)
    h3, n3, G2, U2, a2 = _ffn_fwd("ffn2_fwd", h2, g2, w_in2, w_out2)
    loss, dh3, n4, d_gl, d_pp, dg_ple, dg_fin = _tail(h3, p_bf, target, gple, gfin, w_gate, w_proj)

    def weight_grads(tag, n, a, d_out, dG, dU):
        nmap = lambda j, t: (t, 0)
        zmap = lambda j, t: (j, t, 0)
        omap = lambda j, t: (j, 0, 0)
        dwg = _matmul_tn(tag + "_dw_gate", n, dG, nh, (tm, D), nmap, (1, tm, FB), zmap, (nh, D, FB), (1, D, FB), omap)
        dwu = _matmul_tn(tag + "_dw_up", n, dU, nh, (tm, D), nmap, (1, tm, FB), zmap, (nh, D, FB), (1, D, FB), omap)
        dwo = _matmul_tn(tag + "_dw_out", a, d_out, nh, (1, tm, FB), zmap, (tm, D), nmap, (nh, FB, D), (1, FB, D), omap)
        return jnp.concatenate([dwg, dwu], axis=0), dwo

    def dense_tn(name, a, b, ncol):
        ka, nbw = a.shape[1], b.shape[1] // ncol
        return _matmul_tn(name, a, b, ncol, (tm, ka), lambda j, t: (t, 0), (tm, nbw), lambda j, t: (t, j),
                          (ka, b.shape[1]), (ka, nbw), lambda j, t: (0, j))

    dw_gate = dense_tn("dw_ple_gate", n4, d_gl, 2)
    dw_proj = dense_tn("dw_ple_proj", p_bf, d_pp, 1)

    dh2, dg2, dG2, dU2, dout2 = _ffn_bwd("ffn2_bwd", dh3, h2, g2, G2, U2, w_in2, w_out2)
    dw_in2, dw_out2 = weight_grads("ffn2", n3, a2, dout2, dG2, dU2)

    d_gm, d_sb, dh2_bf = _mix_out_bwd(dh2, w_mo)
    dw_mo = jnp.concatenate([dense_tn("dw_mix_out_gm", gm, dh2_bf, 1), dense_tn("dw_mix_out_sb", sb, dh2_bf, 1)], axis=0)
    dzuv, dgv, dws, db_t = _gmlp_bwd(zuv, d_gm, gv, ws, b_t)
    dq, dk, dv = _attn_bwd(qkv, d_sb, ltot)
    dqkv = jnp.concatenate([dq, dk, dv], axis=1)
    dw_mi = jnp.concatenate([dense_tn("dw_mix_in_uv", n2, dzuv, 2), dense_tn("dw_mix_in_qkv", n2, dqkv, 3)], axis=1)
    dh1, dgmix = _mix_in_bwd(dzuv, dqkv, w_mi, h1, gmix, dh2)

    dx, dg1, dG1, dU1, dout1 = _ffn_bwd("ffn1_bwd", dh1, x, g1, G1, U1, w_in1, w_out1)
    dw_in1, dw_out1 = weight_grads("ffn1", n1, a1, dout1, dG1, dU1)

    small = dict(g1=dg1, gmix=dgmix, gv=dgv, ws=dws, b_t=db_t, g2=dg2, gple=dg_ple, gfin=dg_fin)
    big = dict(w_in1=dw_in1, w_out1=dw_out1, w_mi=dw_mi, w_mo=dw_mo, w_in2=dw_in2, w_out2=dw_out2,
               w_gate=dw_gate, w_proj=dw_proj)
    return loss, dx, small, big


def _peer(d):
    x, y, c = lax.axis_index("x"), lax.axis_index("y"), lax.axis_index("c")
    px = 1 - x if d & 4 else x
    py = 1 - y if d & 2 else y
    pc = 1 - c if d & 1 else c
    return (px, py, pc), 4 * px + 2 * py + pc


def _exchange(name, arrays, scatter):
    n = len(arrays)

    def body(*refs):
        ins, outs = refs[:n], refs[n:2 * n]
        send, recv, local = refs[2 * n:]
        _, me = _peer(0)
        started = []
        for t in range(n):
            src = ins[t].at[me] if scatter else ins[t]
            cp = pltpu.make_async_copy(src, outs[t].at[me], local.at[t])
            cp.start()
            started.append(cp)
        for d in range(1, N_DEV):
            peer, pidx = _peer(d)
            for t in range(n):
                src = ins[t].at[pidx] if scatter else ins[t]
                cp = pltpu.make_async_remote_copy(src_ref=src, dst_ref=outs[t].at[me], send_sem=send.at[t, d - 1],
                                                  recv_sem=recv.at[t, d - 1], device_id=peer, device_id_type=MESH)
                cp.start()
                started.append(cp)
        for cp in started:
            cp.wait()

    out_shape = [jax.ShapeDtypeStruct(a.shape if scatter else (N_DEV,) + a.shape, a.dtype) for a in arrays]
    anyspec = pl.BlockSpec(memory_space=pl.ANY)
    return pl.pallas_call(
        body, name=name,
        in_specs=[anyspec] * n, out_specs=[anyspec] * n, out_shape=out_shape,
        scratch_shapes=[pltpu.SemaphoreType.DMA((n, N_DEV - 1)), pltpu.SemaphoreType.DMA((n, N_DEV - 1)),
                        pltpu.SemaphoreType.DMA((n,))],
        compiler_params=pltpu.CompilerParams(has_side_effects=True),
    )(*arrays)


def _adamw_math(g, w, m, v):
    m_new = ADAM_B1 * m + (1.0 - ADAM_B1) * g
    v_new = ADAM_B2 * v + (1.0 - ADAM_B2) * (g * g)
    m_hat = m_new / (1.0 - ADAM_B1 ** ADAM_STEP)
    v_hat = v_new / (1.0 - ADAM_B2 ** ADAM_STEP)
    delta = -ADAM_LR * (m_hat / (jnp.sqrt(v_hat) + ADAM_EPS) + ADAM_WD * w)
    return delta, m_new, v_new


def _adamw(name, parts, w, m, v):
    R, C = w.shape
    tr = R
    for cand in (256, 128, 64, 32, 16, 8):
        if R % cand == 0:
            tr = cand
            break

    def body(p_ref, w_ref, m_ref, v_ref, g_ref, d_ref, nm_ref, nv_ref):
        g = p_ref[0].astype(F32)
        for j in range(1, N_DEV):
            g = g + p_ref[j].astype(F32)
        g_ref[...] = g
        d_ref[...], nm_ref[...], nv_ref[...] = _adamw_math(g, w_ref[...], m_ref[...], v_ref[...])

    row = lambda i: (i, 0)
    spec = pl.BlockSpec((tr, C), row)
    return pl.pallas_call(
        body, name=name, grid=(R // tr,),
        in_specs=[pl.BlockSpec((N_DEV, tr, C), lambda i: (0, i, 0)), spec, spec, spec],
        out_specs=[spec] * 4,
        out_shape=[jax.ShapeDtypeStruct((R, C), F32)] * 4,
        compiler_params=_params("arbitrary"),
    )(parts, w, m, v)


def _rows128(a):
    flat = a.reshape(-1, LANES)
    pad = (-flat.shape[0]) % SMALL_ROWS_ALIGN
    return jnp.pad(flat, ((0, pad), (0, 0))) if pad else flat


def _unrows(packed, like):
    n = like.size // LANES
    return packed[:n].reshape(like.shape)


def kernel(x, p, ffn1_norm, ffn1_w_in, ffn1_w_out, mix_norm, w_mix_in, gmlp_v_norm, gmlp_w_s, gmlp_b, w_mix_out, ffn2_norm, ffn2_w_in, ffn2_w_out, ple_norm, ple_w_gate, ple_w_proj, final_norm, loss_target, m_ffn1_norm, m_ffn1_w_in, m_ffn1_w_out, m_mix_norm, m_w_mix_in, m_gmlp_v_norm, m_gmlp_w_s, m_gmlp_b, m_w_mix_out, m_ffn2_norm, m_ffn2_w_in, m_ffn2_w_out, m_ple_norm, m_ple_w_gate, m_ple_w_proj, m_final_norm, v_ffn1_norm, v_ffn1_w_in, v_ffn1_w_out, v_mix_norm, v_w_mix_in, v_gmlp_v_norm, v_gmlp_w_s, v_gmlp_b, v_w_mix_out, v_ffn2_norm, v_ffn2_w_in, v_ffn2_w_out, v_ple_norm, v_ple_w_gate, v_ple_w_proj, v_final_norm):
    names = ["ffn1_norm", "ffn1_w_in", "ffn1_w_out", "mix_norm", "w_mix_in", "gmlp_v_norm", "gmlp_w_s", "gmlp_b",
             "w_mix_out", "ffn2_norm", "ffn2_w_in", "ffn2_w_out", "ple_norm", "ple_w_gate", "ple_w_proj", "final_norm"]
    W = dict(zip(names, [ffn1_norm, ffn1_w_in, ffn1_w_out, mix_norm, w_mix_in, gmlp_v_norm, gmlp_w_s, gmlp_b,
                         w_mix_out, ffn2_norm, ffn2_w_in, ffn2_w_out, ple_norm, ple_w_gate, ple_w_proj, final_norm]))
    M = dict(zip(names, [m_ffn1_norm, m_ffn1_w_in, m_ffn1_w_out, m_mix_norm, m_w_mix_in, m_gmlp_v_norm, m_gmlp_w_s,
                         m_gmlp_b, m_w_mix_out, m_ffn2_norm, m_ffn2_w_in, m_ffn2_w_out, m_ple_norm, m_ple_w_gate,
                         m_ple_w_proj, m_final_norm]))
    V = dict(zip(names, [v_ffn1_norm, v_ffn1_w_in, v_ffn1_w_out, v_mix_norm, v_w_mix_in, v_gmlp_v_norm, v_gmlp_w_s,
                         v_gmlp_b, v_w_mix_out, v_ffn2_norm, v_ffn2_w_in, v_ffn2_w_out, v_ple_norm, v_ple_w_gate,
                         v_ple_w_proj, v_final_norm]))
    big = ["ffn1_w_in", "ffn1_w_out", "w_mix_in", "w_mix_out", "ffn2_w_in", "ffn2_w_out", "ple_w_gate", "ple_w_proj"]
    small = [n for n in names if n not in big]

    shards = [W[n][0].astype(BF16) for n in big]
    full = dict(zip(big, _exchange("gather_weights", shards, scatter=False)))
    D = x.shape[-1]

    def rows_full(a):
        return a.reshape(-1, a.shape[-1])

    def cols_full(a):
        return jnp.transpose(a, (1, 0, 2)).reshape(a.shape[1], -1)

    def cols_blocks(a):
        return jnp.transpose(a.reshape(a.shape[0], N_DEV, -1), (1, 0, 2))

    b_t = jnp.transpose(W["gmlp_b"][0])
    loss_part, dx, gs, gb = _local_step(
        x[0], p[0, 0].astype(BF16), loss_target[0],
        W["ffn1_norm"], W["mix_norm"], W["gmlp_v_norm"], W["gmlp_w_s"][0], b_t, W["ffn2_norm"], W["ple_norm"],
        W["final_norm"].reshape(1, D),
        full["ffn1_w_in"], rows_full(full["ffn1_w_out"]), cols_full(full["w_mix_in"]), rows_full(full["w_mix_out"]),
        full["ffn2_w_in"], rows_full(full["ffn2_w_out"]), rows_full(full["ple_w_gate"]), cols_full(full["ple_w_proj"]))

    def rows_blocks(a):
        return a.reshape(N_DEV, -1, a.shape[-1])

    send = [gb["w_in1"], rows_blocks(gb["w_out1"]), cols_blocks(gb["w_mi"]), rows_blocks(gb["w_mo"]),
            gb["w_in2"], rows_blocks(gb["w_out2"]), rows_blocks(gb["w_gate"]), cols_blocks(gb["w_proj"])]
    parts = dict(zip(big, _exchange("scatter_grads", send, scatter=True)))

    small_grads = dict(ffn1_norm=gs["g1"], mix_norm=gs["gmix"], gmlp_v_norm=gs["gv"], gmlp_w_s=gs["ws"],
                       gmlp_b=jnp.transpose(gs["b_t"]), ffn2_norm=gs["g2"], ple_norm=gs["gple"], final_norm=gs["gfin"])
    pieces = [_rows128(small_grads[n]) for n in small] + [jnp.broadcast_to(loss_part, (SMALL_ROWS_ALIGN, LANES))]
    offs = [0]
    for pc in pieces:
        offs.append(offs[-1] + pc.shape[0])
    small_parts = _exchange("gather_small", [jnp.concatenate(pieces, axis=0)], scatter=False)[0]

    out = {}
    for n in big:
        out[n] = _adamw("adamw_" + n, parts[n], W[n][0], M[n][0], V[n][0])
    pack = lambda src: jnp.concatenate([_rows128(src[n]) for n in small] + [jnp.zeros((SMALL_ROWS_ALIGN, LANES), F32)], axis=0)
    sg, sd, sm, sv = _adamw("adamw_small", small_parts, pack(W), pack(M), pack(V))
    for k, n in enumerate(small):
        out[n] = tuple(_unrows(arr[offs[k]:offs[k + 1]], W[n]) for arr in (sg, sd, sm, sv))
    loss = sg[offs[len(small)], 0]

    res = [loss, dx[None]]
    for k in range(4):
        res += [out[n][k].reshape(W[n].shape) for n in names]
    return tuple(res)
```

```python
import functools

import jax
import jax.numpy as jnp
from jax import lax
from jax.experimental import pallas as pl
from jax.experimental.pallas import tpu as pltpu

F32 = jnp.float32
BF16 = jnp.bfloat16
MESH = pl.DeviceIdType.MESH

N_DEV = 8
EPS = 1e-6
ADAM_LR = 0.001
ADAM_B1 = 0.9
ADAM_B2 = 0.999
ADAM_EPS = 1e-08
ADAM_WD = 0.01
ADAM_STEP = 10

GM_WIDTH = 512
GM_HEADS = 4
CHUNK = 128
SB_WIDTH = 512
SB_HEAD_DIM = 64
SB_SCALE = 0.125
LANES = 128
SMALL_ROWS_ALIGN = 8

ROW_TILE = 512
ATTN_Q_ROWS = 512
ATTN_KEY_BLOCK = 256
VMEM_LIMIT = 56 * 1024 * 1024


def _params(*sem):
    return pltpu.CompilerParams(dimension_semantics=sem, vmem_limit_bytes=VMEM_LIMIT)


def _dot(a, b):
    return jnp.dot(a, b, preferred_element_type=F32)


def _dot_nt(a, b):
    return lax.dot_general(a, b, (((1,), (1,)), ((), ())), preferred_element_type=F32)


def _dot_tn(a, b):
    return lax.dot_general(a, b, (((0,), (0,)), ((), ())), preferred_element_type=F32)


def _rms_parts(x):
    r = lax.rsqrt(jnp.mean(x * x, axis=-1, keepdims=True) + EPS)
    return r, x * r


def _rms_bwd(x, g, dy):
    r, xh = _rms_parts(x)
    dyg = dy * g
    dx = r * (dyg - xh * jnp.mean(dyg * xh, axis=-1, keepdims=True))
    return dx, jnp.sum(dy * xh, axis=0, keepdims=True)


def _sigmoid(x):
    return 1.0 / (1.0 + jnp.exp(-x))


_SQRT_HALF = 0.7071067811865476
_INV_SQRT_2PI = 0.3989422804014327


def _gelu(x):
    return 0.5 * x * (1.0 + lax.erf(x * _SQRT_HALF))


def _gelu_grad(x):
    return 0.5 * (1.0 + lax.erf(x * _SQRT_HALF)) + x * (_INV_SQRT_2PI * jnp.exp(-0.5 * x * x))


def _split_bf16(x):
    hi = x.astype(BF16)
    lo = (x - hi.astype(F32)).astype(BF16)
    return hi, lo


def _ffn_fwd(name, h, gain, w_in, w_out):
    T, D = h.shape
    nb, _, FB = w_in.shape
    nh = nb // 2
    tm = min(ROW_TILE, T)

    def body(h_ref, g_ref, wg_ref, wu_ref, wo_ref, ho_ref, n_ref, G_ref, U_ref, a_ref, n_s, acc):
        jj = pl.program_id(1)

        @pl.when(jj == 0)
        def _():
            _, xh = _rms_parts(h_ref[...])
            n = (xh * g_ref[...]).astype(BF16)
            n_s[...] = n
            n_ref[...] = n
            acc[...] = jnp.zeros_like(acc)

        n = n_s[...]
        G = _dot(n, wg_ref[0])
        U = _dot(n, wu_ref[0])
        G_ref[0] = G
        U_ref[0] = U
        a = (G * _sigmoid(G) * U).astype(BF16)
        a_ref[0] = a
        acc[...] += _dot(a, wo_ref[...])

        @pl.when(jj == nh - 1)
        def _():
            ho_ref[...] = h_ref[...] + 0.5 * acc[...]

    row = lambda i, j: (i, 0)
    blk = lambda i, j: (j, i, 0)
    return pl.pallas_call(
        body, name=name, grid=(T // tm, nh),
        in_specs=[pl.BlockSpec((tm, D), row),
                  pl.BlockSpec((1, D), lambda i, j: (0, 0)),
                  pl.BlockSpec((1, D, FB), lambda i, j: (j, 0, 0)),
                  pl.BlockSpec((1, D, FB), lambda i, j: (j + nh, 0, 0)),
                  pl.BlockSpec((FB, D), lambda i, j: (j, 0))],
        out_specs=[pl.BlockSpec((tm, D), row), pl.BlockSpec((tm, D), row),
                   pl.BlockSpec((1, tm, FB), blk), pl.BlockSpec((1, tm, FB), blk),
                   pl.BlockSpec((1, tm, FB), blk)],
        out_shape=[jax.ShapeDtypeStruct((T, D), F32), jax.ShapeDtypeStruct((T, D), BF16),
                   jax.ShapeDtypeStruct((nh, T, FB), F32), jax.ShapeDtypeStruct((nh, T, FB), F32),
                   jax.ShapeDtypeStruct((nh, T, FB), BF16)],
        scratch_shapes=[pltpu.VMEM((tm, D), BF16), pltpu.VMEM((tm, D), F32)],
        compiler_params=_params("arbitrary", "arbitrary"),
    )(h, gain, w_in, w_in, w_out)


def _ffn_bwd(name, dh, h_in, gain, G, U, w_in, w_out):
    T, D = dh.shape
    nb, _, FB = w_in.shape
    nh = nb // 2
    tm = min(ROW_TILE, T)

    def body(dh_ref, h_ref, g_ref, G_ref, U_ref, wg_ref, wu_ref, wo_ref,
             dhin_ref, dg_ref, dG_ref, dU_ref, do_ref, dn_acc, do_s):
        i = pl.program_id(0)
        jj = pl.program_id(1)

        @pl.when(jj == 0)
        def _():
            d_out = (0.5 * dh_ref[...]).astype(BF16)
            do_s[...] = d_out
            do_ref[...] = d_out
            dn_acc[...] = jnp.zeros_like(dn_acc)

        @pl.when((i == 0) & (jj == 0))
        def _():
            dg_ref[...] = jnp.zeros_like(dg_ref)

        da = _dot_nt(do_s[...], wo_ref[...])
        Gv = G_ref[0]
        Uv = U_ref[0]
        sig = _sigmoid(Gv)
        dU = (da * (Gv * sig)).astype(BF16)
        dG = (da * Uv * (sig * (1.0 + Gv * (1.0 - sig)))).astype(BF16)
        dG_ref[0] = dG
        dU_ref[0] = dU
        dn_acc[...] += _dot_nt(dG, wg_ref[0]) + _dot_nt(dU, wu_ref[0])

        @pl.when(jj == nh - 1)
        def _():
            dx, dg = _rms_bwd(h_ref[...], g_ref[...], dn_acc[...])
            dhin_ref[...] = dh_ref[...] + dx
            dg_ref[...] += dg

    row = lambda i, j: (i, 0)
    blk = lambda i, j: (j, i, 0)
    one = lambda i, j: (0, 0)
    return pl.pallas_call(
        body, name=name, grid=(T // tm, nh),
        in_specs=[pl.BlockSpec((tm, D), row), pl.BlockSpec((tm, D), row), pl.BlockSpec((1, D), one),
                  pl.BlockSpec((1, tm, FB), blk), pl.BlockSpec((1, tm, FB), blk),
                  pl.BlockSpec((1, D, FB), lambda i, j: (j, 0, 0)),
                  pl.BlockSpec((1, D, FB), lambda i, j: (j + nh, 0, 0)),
                  pl.BlockSpec((FB, D), lambda i, j: (j, 0))],
        out_specs=[pl.BlockSpec((tm, D), row), pl.BlockSpec((1, D), one),
                   pl.BlockSpec((1, tm, FB), blk), pl.BlockSpec((1, tm, FB), blk),
                   pl.BlockSpec((tm, D), row)],
        out_shape=[jax.ShapeDtypeStruct((T, D), F32), jax.ShapeDtypeStruct((1, D), F32),
                   jax.ShapeDtypeStruct((nh, T, FB), BF16), jax.ShapeDtypeStruct((nh, T, FB), BF16),
                   jax.ShapeDtypeStruct((T, D), BF16)],
        scratch_shapes=[pltpu.VMEM((tm, D), F32), pltpu.VMEM((tm, D), BF16)],
        compiler_params=_params("arbitrary", "arbitrary"),
    )(dh, h_in, gain, G, U, w_in, w_in, w_out)


def _matmul_tn(name, a, b, nj, a_block, a_map, b_block, b_map, out_shape, out_block, out_map):
    T = a.shape[-2]
    tt = a_block[-2]
    nt = T // tt
    kb, nbk = out_block[-2], out_block[-1]

    def body(a_ref, b_ref, o_ref, acc):
        t = pl.program_id(1)

        @pl.when(t == 0)
        def _():
            acc[...] = jnp.zeros_like(acc)

        av = a_ref[0] if len(a_block) == 3 else a_ref[...]
        bv = b_ref[0] if len(b_block) == 3 else b_ref[...]
        acc[...] += _dot_tn(av, bv)

        @pl.when(t == nt - 1)
        def _():
            if len(out_block) == 3:
                o_ref[0] = acc[...].astype(o_ref.dtype)
            else:
                o_ref[...] = acc[...].astype(o_ref.dtype)

    return pl.pallas_call(
        body, name=name, grid=(nj, nt),
        in_specs=[pl.BlockSpec(a_block, a_map), pl.BlockSpec(b_block, b_map)],
        out_specs=pl.BlockSpec(out_block, out_map),
        out_shape=jax.ShapeDtypeStruct(out_shape, BF16),
        scratch_shapes=[pltpu.VMEM((kb, nbk), F32)],
        compiler_params=_params("arbitrary", "arbitrary"),
    )(a, b)


def _mix_in_fwd(h, gain, w):
    T, D = h.shape
    W = w.shape[1]
    nuv = 2 * GM_WIDTH
    tm = min(ROW_TILE, T)

    def body(h_ref, g_ref, w_ref, n_ref, zuv_ref, qkv_ref):
        _, xh = _rms_parts(h_ref[...])
        n = (xh * g_ref[...]).astype(BF16)
        n_ref[...] = n
        z = _dot(n, w_ref[...])
        zuv_ref[...] = z[:, :nuv]
        qkv_ref[...] = z[:, nuv:].astype(BF16)

    row = lambda i: (i, 0)
    return pl.pallas_call(
        body, name="mix_in_fwd", grid=(T // tm,),
        in_specs=[pl.BlockSpec((tm, D), row), pl.BlockSpec((1, D), lambda i: (0, 0)),
                  pl.BlockSpec((D, W), lambda i: (0, 0))],
        out_specs=[pl.BlockSpec((tm, D), row), pl.BlockSpec((tm, nuv), row),
                   pl.BlockSpec((tm, W - nuv), row)],
        out_shape=[jax.ShapeDtypeStruct((T, D), BF16), jax.ShapeDtypeStruct((T, nuv), F32),
                   jax.ShapeDtypeStruct((T, W - nuv), BF16)],
        compiler_params=_params("arbitrary"),
    )(h, gain, w)


def _mix_in_bwd(dzuv, dqkv, w, h, gain, dh):
    T, D = h.shape
    W = w.shape[1]
    nuv = dzuv.shape[1]
    tm = min(ROW_TILE, T)

    def body(dzuv_ref, dqkv_ref, w_ref, h_ref, g_ref, dh_ref, dhin_ref, dg_ref):
        @pl.when(pl.program_id(0) == 0)
        def _():
            dg_ref[...] = jnp.zeros_like(dg_ref)

        dn = _dot_nt(dzuv_ref[...], w_ref[:, :nuv]) + _dot_nt(dqkv_ref[...], w_ref[:, nuv:])
        dx, dg = _rms_bwd(h_ref[...], g_ref[...], dn)
        dhin_ref[...] = dh_ref[...] + dx
        dg_ref[...] += dg

    row = lambda i: (i, 0)
    one = lambda i: (0, 0)
    return pl.pallas_call(
        body, name="mix_in_bwd", grid=(T // tm,),
        in_specs=[pl.BlockSpec((tm, nuv), row), pl.BlockSpec((tm, W - nuv), row),
                  pl.BlockSpec((D, W), one), pl.BlockSpec((tm, D), row), pl.BlockSpec((1, D), one),
                  pl.BlockSpec((tm, D), row)],
        out_specs=[pl.BlockSpec((tm, D), row), pl.BlockSpec((1, D), one)],
        out_shape=[jax.ShapeDtypeStruct((T, D), F32), jax.ShapeDtypeStruct((1, D), F32)],
        compiler_params=_params("arbitrary"),
    )(dzuv, dqkv, w, h, gain, dh)


def _gmlp_norm(zv, gv):
    v = _gelu(zv)
    r, vh = _rms_parts(v)
    return r, vh, (vh * gv).astype(BF16)


def _causal_ws(ws_ref, hd):
    r = lax.broadcasted_iota(jnp.int32, (CHUNK, CHUNK), 0)
    c = lax.broadcasted_iota(jnp.int32, (CHUNK, CHUNK), 1)
    return jnp.where(r >= c, ws_ref[hd], 0.0).astype(BF16)


def _gmlp_fwd(zuv, gv, ws, b_t):
    T = zuv.shape[0]
    tg = min(ROW_TILE, T)

    def body(zu_ref, zv_ref, gv_ref, ws_ref, bt_ref, o_ref):
        u = _gelu(zu_ref[...])
        _, _, vn = _gmlp_norm(zv_ref[...], gv_ref[...])
        for hd in range(GM_HEADS):
            wc = _causal_ws(ws_ref, hd)
            cols = slice(hd * CHUNK, (hd + 1) * CHUNK)
            for c in range(tg // CHUNK):
                rows = slice(c * CHUNK, (c + 1) * CHUNK)
                sv = _dot(wc, vn[rows, cols]) + bt_ref[:, hd:hd + 1]
                o_ref[rows, cols] = (u[rows, cols] * sv).astype(BF16)

    return pl.pallas_call(
        body, name="gmlp_fwd", grid=(T // tg,),
        in_specs=[pl.BlockSpec((tg, GM_WIDTH), lambda i: (i, 0)), pl.BlockSpec((tg, GM_WIDTH), lambda i: (i, 1)),
                  pl.BlockSpec((1, GM_WIDTH), lambda i: (0, 0)),
                  pl.BlockSpec((GM_HEADS, CHUNK, CHUNK), lambda i: (0, 0, 0)),
                  pl.BlockSpec((CHUNK, GM_HEADS), lambda i: (0, 0))],
        out_specs=pl.BlockSpec((tg, GM_WIDTH), lambda i: (i, 0)),
        out_shape=jax.ShapeDtypeStruct((T, GM_WIDTH), BF16),
        compiler_params=_params("arbitrary"),
    )(zuv, zuv, gv, ws, b_t)


def _gmlp_bwd(zuv, d_gm, gv, ws, b_t):
    T = zuv.shape[0]
    tg = min(ROW_TILE, T)
    ng = T // tg

    def body(zu_ref, zv_ref, dgm_ref, gv_ref, ws_ref, bt_ref, dz_ref, dgv_ref, dws_ref, dbt_ref, dsv_acc, dvn_s):
        i = pl.program_id(0)

        @pl.when(i == 0)
        def _():
            dgv_ref[...] = jnp.zeros_like(dgv_ref)
            dws_ref[...] = jnp.zeros_like(dws_ref)
            dsv_acc[...] = jnp.zeros_like(dsv_acc)

        zu = zu_ref[...]
        zv = zv_ref[...]
        dgm = dgm_ref[...]
        gvv = gv_ref[...]
        u = _gelu(zu)
        rv, vh, vn = _gmlp_norm(zv, gvv)
        dsv = dgm * u
        dsv_b = dsv.astype(BF16)
        for hd in range(GM_HEADS):
            wc = _causal_ws(ws_ref, hd)
            cols = slice(hd * CHUNK, (hd + 1) * CHUNK)
            dws = jnp.zeros((CHUNK, CHUNK), F32)
            dsv_sum = jnp.zeros((CHUNK, CHUNK), F32)
            for c in range(tg // CHUNK):
                rows = slice(c * CHUNK, (c + 1) * CHUNK)
                vch = vn[rows, cols]
                sv = _dot(wc, vch) + bt_ref[:, hd:hd + 1]
                dz_ref[rows, cols] = (dgm[rows, cols] * sv * _gelu_grad(zu[rows, cols])).astype(BF16)
                dws += _dot_nt(dsv_b[rows, cols], vch)
                dsv_sum += dsv[rows, cols]
                dvn_s[rows, cols] = _dot_tn(wc, dsv_b[rows, cols])
            dws_ref[hd] += dws
            dsv_acc[:, cols] += dsv_sum
        dvn = dvn_s[...]
        dvh = dvn * gvv
        dv = rv * (dvh - vh * jnp.mean(dvh * vh, axis=-1, keepdims=True))
        dgv_ref[...] += jnp.sum(dvn * vh, axis=0, keepdims=True)
        dz_ref[:, GM_WIDTH:] = (dv * _gelu_grad(zv)).astype(BF16)

        @pl.when(i == ng - 1)
        def _():
            r = lax.broadcasted_iota(jnp.int32, (CHUNK, CHUNK), 0)
            c = lax.broadcasted_iota(jnp.int32, (CHUNK, CHUNK), 1)
            for hd in range(GM_HEADS):
                dws_ref[hd] = jnp.where(r >= c, dws_ref[hd], 0.0)
                dbt_ref[:, hd:hd + 1] = jnp.sum(dsv_acc[:, hd * CHUNK:(hd + 1) * CHUNK], axis=1, keepdims=True)

    return pl.pallas_call(
        body, name="gmlp_bwd", grid=(ng,),
        in_specs=[pl.BlockSpec((tg, GM_WIDTH), lambda i: (i, 0)), pl.BlockSpec((tg, GM_WIDTH), lambda i: (i, 1)),
                  pl.BlockSpec((tg, GM_WIDTH), lambda i: (i, 0)),
                  pl.BlockSpec((1, GM_WIDTH), lambda i: (0, 0)),
                  pl.BlockSpec((GM_HEADS, CHUNK, CHUNK), lambda i: (0, 0, 0)),
                  pl.BlockSpec((CHUNK, GM_HEADS), lambda i: (0, 0))],
        out_specs=[pl.BlockSpec((tg, 2 * GM_WIDTH), lambda i: (i, 0)),
                   pl.BlockSpec((1, GM_WIDTH), lambda i: (0, 0)),
                   pl.BlockSpec((GM_HEADS, CHUNK, CHUNK), lambda i: (0, 0, 0)),
                   pl.BlockSpec((CHUNK, GM_HEADS), lambda i: (0, 0))],
        out_shape=[jax.ShapeDtypeStruct((T, 2 * GM_WIDTH), BF16), jax.ShapeDtypeStruct((1, GM_WIDTH), F32),
                   jax.ShapeDtypeStruct((GM_HEADS, CHUNK, CHUNK), F32),
                   jax.ShapeDtypeStruct((CHUNK, GM_HEADS), F32)],
        scratch_shapes=[pltpu.VMEM((CHUNK, GM_WIDTH), F32), pltpu.VMEM((tg, GM_WIDTH), F32)],
        compiler_params=_params("arbitrary"),
    )(zuv, zuv, d_gm, gv, ws, b_t)


def _scan_matrix(blk, keep):
    r = lax.broadcasted_iota(jnp.int32, (blk, blk), 0)
    c = lax.broadcasted_iota(jnp.int32, (blk, blk), 1)
    return jnp.where(keep(r, c), 1.0, 0.0).astype(BF16)


def _scan_matrix2(blk, keep, value):
    m = _scan_matrix(blk, keep) * value
    return jnp.concatenate([m, m], axis=0)


def _scan(x, mat2):
    hi, lo = _split_bf16(x)
    return _dot(jnp.concatenate([hi, lo], axis=1), mat2)


def _head_masks(q):
    lane = lax.broadcasted_iota(jnp.int32, q.shape, 1)
    m0 = lane < SB_HEAD_DIM
    zero = jnp.zeros_like(q)
    return m0, jnp.where(m0, q, zero), jnp.where(m0, zero, q)


_LOG2E = 1.4426950408889634


def _softplus_parts(z):
    e = jnp.exp2(jnp.abs(z) * (-_LOG2E))
    ope = 1.0 + e
    return e, ope, jnp.maximum(z, 0.0) + jnp.log(ope)


def _attn_fwd(qkv):
    T = qkv.shape[0]
    tk = ATTN_KEY_BLOCK
    tq = min(ATTN_Q_ROWS, T)
    band = tq // tk
    ngrp = SB_WIDTH // LANES

    def body(q_ref, k_ref, v_ref, o_ref, l_ref, acc, run):
        i = pl.program_id(1)
        suffix = _scan_matrix2(tk, lambda r, c: r >= c, -1.0)
        row = lax.broadcasted_iota(jnp.int32, (tq, tk), 0)
        col = lax.broadcasted_iota(jnp.int32, (tq, tk), 1)
        m0, q0, q1 = _head_masks(q_ref[...] * SB_SCALE)
        acc[...] = jnp.zeros_like(acc)
        run[...] = jnp.zeros_like(run)

        def tile(j, causal):
            start = pl.multiple_of(j * tk, tk)
            kj = k_ref[pl.ds(start, tk), :]
            vj = v_ref[pl.ds(start, tk), :]
            for hd, qh in enumerate((q0, q1)):
                z = _dot_nt(qh, kj)
                _, _, sp = _softplus_parts(z)
                if causal is not None:
                    sp = jnp.where(causal, sp, 0.0)
                res = _scan(sp, suffix)
                a = jnp.exp(z + (run[hd] + res))
                if causal is not None:
                    a = jnp.where(causal, a, 0.0)
                acc[hd] += _dot(a.astype(BF16), vj)
                run[hd] += res[:, 0:1]

        def band_step(it, carry):
            jb = band - 1 - it
            tile(i * band + jb, jb * tk + col < row)
            return carry

        def full_step(it, carry):
            tile(i * band - 1 - it, None)
            return carry

        lax.fori_loop(0, band, band_step, 0)
        lax.fori_loop(0, i * band, full_step, 0)
        o_ref[...] = jnp.where(m0, acc[0], acc[1]).astype(BF16)
        l_ref[...] = jnp.where(m0, jnp.broadcast_to(run[0], (tq, LANES)), jnp.broadcast_to(run[1], (tq, LANES)))

    return pl.pallas_call(
        body, name="attn_fwd", grid=(ngrp, T // tq),
        in_specs=[pl.BlockSpec((tq, LANES), lambda g, i: (i, g)),
                  pl.BlockSpec((T, LANES), lambda g, i: (0, ngrp + g)),
                  pl.BlockSpec((T, LANES), lambda g, i: (0, 2 * ngrp + g))],
        out_specs=[pl.BlockSpec((tq, LANES), lambda g, i: (i, g)),
                   pl.BlockSpec((tq, LANES), lambda g, i: (i, g))],
        out_shape=[jax.ShapeDtypeStruct((T, SB_WIDTH), BF16), jax.ShapeDtypeStruct((T, SB_WIDTH), F32)],
        scratch_shapes=[pltpu.VMEM((2, tq, LANES), F32), pltpu.VMEM((2, tq, 1), F32)],
        compiler_params=_params("arbitrary", "arbitrary"),
    )(qkv, qkv, qkv)


def _attn_bwd(qkv, d_o, ltot):
    T = qkv.shape[0]
    tk = ATTN_KEY_BLOCK
    tq = min(ATTN_Q_ROWS, T)
    band = tq // tk
    nq = T // tq
    ngrp = SB_WIDTH // LANES

    def body(q_ref, k_ref, v_ref, do_ref, l_ref, dq_ref, dk_ref, dv_ref, dq_acc, dk_acc, dv_acc, lpre, ppre):
        i = pl.program_id(1)

        @pl.when(i == 0)
        def _():
            dk_acc[...] = jnp.zeros_like(dk_acc)
            dv_acc[...] = jnp.zeros_like(dv_acc)

        excl = _scan_matrix(tk, lambda r, c: r < c)
        excl2 = jnp.concatenate([excl, excl], axis=0)
        row = lax.broadcasted_iota(jnp.int32, (tq, tk), 0)
        col = lax.broadcasted_iota(jnp.int32, (tq, tk), 1)
        m0, q0, q1 = _head_masks(q_ref[...] * SB_SCALE)
        _, d0, d1 = _head_masks(do_ref[...].astype(BF16))
        lt = l_ref[...]
        ltots = (lt[:, 0:1], lt[:, SB_HEAD_DIM:SB_HEAD_DIM + 1])
        dq_acc[...] = jnp.zeros_like(dq_acc)
        lpre[...] = jnp.zeros_like(lpre)
        ppre[...] = jnp.zeros_like(ppre)

        def tile(j, causal):
            start = pl.multiple_of(j * tk, tk)
            kj = k_ref[pl.ds(start, tk), :]
            vj = v_ref[pl.ds(start, tk), :]
            for hd, (qh, dh) in enumerate(((q0, d0), (q1, d1))):
                z = _dot_nt(qh, kj)
                e, ope, sp = _softplus_parts(z)
                rinv = 1.0 / ope
                small = e * rinv
                pos = z >= 0.0
                beta = jnp.where(pos, rinv, small)
                one_m_beta = jnp.where(pos, small, rinv)
                if causal is not None:
                    sp = jnp.where(causal, sp, 0.0)
                res = _scan(sp, excl2)
                a = jnp.exp(z + ((ltots[hd] + lpre[hd]) + res))
                if causal is not None:
                    a = jnp.where(causal, a, 0.0)
                p = a * _dot_nt(dh, vj)
                resp = _dot(p.astype(BF16), excl)
                dz = p * one_m_beta - beta * (ppre[hd] + resp)
                if causal is not None:
                    dz = jnp.where(causal, dz, 0.0)
                dzb = dz.astype(BF16)
                dq_acc[hd] += _dot(dzb, kj)
                dk_acc[pl.ds(start, tk), :] += _dot_tn(dzb, qh)
                dv_acc[pl.ds(start, tk), :] += _dot_tn(a.astype(BF16), dh)
                lpre[hd] += res[:, tk - 1:tk] + sp[:, tk - 1:tk]
                ppre[hd] += resp[:, tk - 1:tk] + p[:, tk - 1:tk]

        def full_step(j, carry):
            tile(j, None)
            return carry

        def band_step(jb, carry):
            tile(i * band + jb, jb * tk + col < row)
            return carry

        lax.fori_loop(0, i * band, full_step, 0)
        lax.fori_loop(0, band, band_step, 0)
        dq_ref[...] = (jnp.where(m0, dq_acc[0], dq_acc[1]) * SB_SCALE).astype(BF16)

        @pl.when(i == nq - 1)
        def _():
            dk_ref[...] = dk_acc[...].astype(BF16)
            dv_ref[...] = dv_acc[...].astype(BF16)

    qmap = lambda g, i: (i, g)
    return pl.pallas_call(
        body, name="attn_bwd", grid=(ngrp, nq),
        in_specs=[pl.BlockSpec((tq, LANES), qmap),
                  pl.BlockSpec((T, LANES), lambda g, i: (0, ngrp + g)),
                  pl.BlockSpec((T, LANES), lambda g, i: (0, 2 * ngrp + g)),
                  pl.BlockSpec((tq, LANES), qmap), pl.BlockSpec((tq, LANES), qmap)],
        out_specs=[pl.BlockSpec((tq, LANES), qmap),
                   pl.BlockSpec((T, LANES), lambda g, i: (0, g)),
                   pl.BlockSpec((T, LANES), lambda g, i: (0, g))],
        out_shape=[jax.ShapeDtypeStruct((T, SB_WIDTH), BF16)] * 3,
        scratch_shapes=[pltpu.VMEM((2, tq, LANES), F32), pltpu.VMEM((T, LANES), F32),
                        pltpu.VMEM((T, LANES), F32), pltpu.VMEM((2, tq, 1), F32),
                        pltpu.VMEM((2, tq, 1), F32)],
        compiler_params=_params("arbitrary", "arbitrary"),
    )(qkv, qkv, qkv, d_o, ltot)


def _mix_out_fwd(h, gm, sb, w):
    T, D = h.shape
    tm = min(ROW_TILE, T)

    def body(h_ref, gm_ref, sb_ref, w_ref, o_ref):
        o_ref[...] = h_ref[...] + _dot(gm_ref[...], w_ref[:GM_WIDTH, :]) + _dot(sb_ref[...], w_ref[GM_WIDTH:, :])

    row = lambda i: (i, 0)
    return pl.pallas_call(
        body, name="mix_out_fwd", grid=(T // tm,),
        in_specs=[pl.BlockSpec((tm, D), row), pl.BlockSpec((tm, GM_WIDTH), row), pl.BlockSpec((tm, SB_WIDTH), row),
                  pl.BlockSpec((GM_WIDTH + SB_WIDTH, D), lambda i: (0, 0))],
        out_specs=pl.BlockSpec((tm, D), row),
        out_shape=jax.ShapeDtypeStruct((T, D), F32),
        compiler_params=_params("arbitrary"),
    )(h, gm, sb, w)


def _mix_out_bwd(dh, w):
    T, D = dh.shape
    tm = min(ROW_TILE, T)

    def body(dh_ref, w_ref, dgm_ref, dsb_ref, dhb_ref):
        dhb = dh_ref[...].astype(BF16)
        dhb_ref[...] = dhb
        dgm_ref[...] = _dot_nt(dhb, w_ref[:GM_WIDTH, :])
        dsb_ref[...] = _dot_nt(dhb, w_ref[GM_WIDTH:, :])

    row = lambda i: (i, 0)
    return pl.pallas_call(
        body, name="mix_out_bwd", grid=(T // tm,),
        in_specs=[pl.BlockSpec((tm, D), row), pl.BlockSpec((GM_WIDTH + SB_WIDTH, D), lambda i: (0, 0))],
        out_specs=[pl.BlockSpec((tm, GM_WIDTH), row), pl.BlockSpec((tm, SB_WIDTH), row), pl.BlockSpec((tm, D), row)],
        out_shape=[jax.ShapeDtypeStruct((T, GM_WIDTH), F32), jax.ShapeDtypeStruct((T, SB_WIDTH), F32),
                   jax.ShapeDtypeStruct((T, D), BF16)],
        compiler_params=_params("arbitrary"),
    )(dh, w)


def _tail(h3, p, target, g_ple, g_fin, w_gate, w_proj):
    T, D = h3.shape
    PD = p.shape[1]
    tm = min(ROW_TILE, T)

    def body(h_ref, p_ref, t_ref, gp_ref, gf_ref, wg_ref, wp_ref,
             loss_ref, dh_ref, n4_ref, dgl_ref, dpp_ref, dgp_ref, dgf_ref):
        @pl.when(pl.program_id(0) == 0)
        def _():
            loss_ref[...] = jnp.zeros_like(loss_ref)
            dgp_ref[...] = jnp.zeros_like(dgp_ref)
            dgf_ref[...] = jnp.zeros_like(dgf_ref)

        h3v = h_ref[...]
        gp = gp_ref[...]
        gf = gf_ref[...]
        r3, xh3 = _rms_parts(h3v)
        n4 = (xh3 * gp).astype(BF16)
        n4_ref[...] = n4
        gate = _sigmoid(_dot(n4, wg_ref[...]))
        pp = _dot(p_ref[...], wp_ref[...])
        h4 = h3v + gate * pp
        r4, xh4 = _rms_parts(h4)
        err = xh4 * gf - t_ref[...]
        loss_ref[...] += jnp.full(loss_ref.shape, (0.5 / D) * jnp.sum(err * err), F32)
        dy = err * (1.0 / D)
        dgf_ref[...] += jnp.sum(dy * xh4, axis=0, keepdims=True)
        dyg = dy * gf
        dh4 = r4 * (dyg - xh4 * jnp.mean(dyg * xh4, axis=-1, keepdims=True))
        dpp_ref[...] = (dh4 * gate).astype(BF16)
        dgl = (dh4 * pp * gate * (1.0 - gate)).astype(BF16)
        dgl_ref[...] = dgl
        dn4 = _dot_nt(dgl, wg_ref[...])
        dgp_ref[...] += jnp.sum(dn4 * xh3, axis=0, keepdims=True)
        dn4g = dn4 * gp
        dh_ref[...] = dh4 + r3 * (dn4g - xh3 * jnp.mean(dn4g * xh3, axis=-1, keepdims=True))

    row = lambda i: (i, 0)
    one = lambda i: (0, 0)
    return pl.pallas_call(
        body, name="tail", grid=(T // tm,),
        in_specs=[pl.BlockSpec((tm, D), row), pl.BlockSpec((tm, PD), row), pl.BlockSpec((tm, D), row),
                  pl.BlockSpec((1, D), one), pl.BlockSpec((1, D), one),
                  pl.BlockSpec((D, D), one), pl.BlockSpec((PD, D), one)],
        out_specs=[pl.BlockSpec((1, LANES), one), pl.BlockSpec((tm, D), row), pl.BlockSpec((tm, D), row),
                   pl.BlockSpec((tm, D), row), pl.BlockSpec((tm, D), row),
                   pl.BlockSpec((1, D), one), pl.BlockSpec((1, D), one)],
        out_shape=[jax.ShapeDtypeStruct((1, LANES), F32), jax.ShapeDtypeStruct((T, D), F32),
                   jax.ShapeDtypeStruct((T, D), BF16), jax.ShapeDtypeStruct((T, D), BF16),
                   jax.ShapeDtypeStruct((T, D), BF16),
                   jax.ShapeDtypeStruct((1, D), F32), jax.ShapeDtypeStruct((1, D), F32)],
        compiler_params=_params("arbitrary"),
    )(h3, p, target, g_ple, g_fin, w_gate, w_proj)


def _local_step(x, p_bf, target, g1, gmix, gv, ws, b_t, g2, gple, gfin,
                w_in1, w_out1, w_mi, w_mo, w_in2, w_out2, w_gate, w_proj):
    T, D = x.shape
    nb, _, FB = w_in1.shape
    nh = nb // 2
    tm = min(ROW_TILE, T)

    h1, n1, G1, U1, a1 = _ffn_fwd("ffn1_fwd", x, g1, w_in1, w_out1)
    n2, zuv, qkv = _mix_in_fwd(h1, gmix, w_mi)
    gm = _gmlp_fwd(zuv, gv, ws, b_t)
    sb, ltot = _attn_fwd(qkv)
    h2 = _mix_out_fwd(h1, gm, sb, w_mo)
    h3, n3, G2, U2, a2 = _ffn_fwd("ffn2_fwd", h2, g2, w_in2, w_out2)
    loss, dh3, n4, d_gl, d_pp, dg_ple, dg_fin = _tail(h3, p_bf, target, gple, gfin, w_gate, w_proj)

    def weight_grads(tag, n, a, d_out, dG, dU):
        nmap = lambda j, t: (t, 0)
        zmap = lambda j, t: (j, t, 0)
        omap = lambda j, t: (j, 0, 0)
        dwg = _matmul_tn(tag + "_dw_gate", n, dG, nh, (tm, D), nmap, (1, tm, FB), zmap, (nh, D, FB), (1, D, FB), omap)
        dwu = _matmul_tn(tag + "_dw_up", n, dU, nh, (tm, D), nmap, (1, tm, FB), zmap, (nh, D, FB), (1, D, FB), omap)
        dwo = _matmul_tn(tag + "_dw_out", a, d_out, nh, (1, tm, FB), zmap, (tm, D), nmap, (nh, FB, D), (1, FB, D), omap)
        return jnp.concatenate([dwg, dwu], axis=0), dwo

    def dense_tn(name, a, b, ncol):
        ka, nbw = a.shape[1], b.shape[1] // ncol
        return _matmul_tn(name, a, b, ncol, (tm, ka), lambda j, t: (t, 0), (tm, nbw), lambda j, t: (t, j),
                          (ka, b.shape[1]), (ka, nbw), lambda j, t: (0, j))

    dw_gate = dense_tn("dw_ple_gate", n4, d_gl, 2)
    dw_proj = dense_tn("dw_ple_proj", p_bf, d_pp, 1)

    dh2, dg2, dG2, dU2, dout2 = _ffn_bwd("ffn2_bwd", dh3, h2, g2, G2, U2, w_in2, w_out2)
    dw_in2, dw_out2 = weight_grads("ffn2", n3, a2, dout2, dG2, dU2)

    d_gm, d_sb, dh2_bf = _mix_out_bwd(dh2, w_mo)
    dw_mo = jnp.concatenate([dense_tn("dw_mix_out_gm", gm, dh2_bf, 1), dense_tn("dw_mix_out_sb", sb, dh2_bf, 1)], axis=0)
    dzuv, dgv, dws, db_t = _gmlp_bwd(zuv, d_gm, gv, ws, b_t)
    dq, dk, dv = _attn_bwd(qkv, d_sb, ltot)
    dqkv = jnp.concatenate([dq, dk, dv], axis=1)
    dw_mi = jnp.concatenate([dense_tn("dw_mix_in_uv", n2, dzuv, 2), dense_tn("dw_mix_in_qkv", n2, dqkv, 3)], axis=1)
    dh1, dgmix = _mix_in_bwd(dzuv, dqkv, w_mi, h1, gmix, dh2)

    dx, dg1, dG1, dU1, dout1 = _ffn_bwd("ffn1_bwd", dh1, x, g1, G1, U1, w_in1, w_out1)
    dw_in1, dw_out1 = weight_grads("ffn1", n1, a1, dout1, dG1, dU1)

    small = dict(g1=dg1, gmix=dgmix, gv=dgv, ws=dws, b_t=db_t, g2=dg2, gple=dg_ple, gfin=dg_fin)
    big = dict(w_in1=dw_in1, w_out1=dw_out1, w_mi=dw_mi, w_mo=dw_mo, w_in2=dw_in2, w_out2=dw_out2,
               w_gate=dw_gate, w_proj=dw_proj)
    return loss, dx, small, big


def _peer(d):
    x, y, c = lax.axis_index("x"), lax.axis_index("y"), lax.axis_index("c")
    px = 1 - x if d & 4 else x
    py = 1 - y if d & 2 else y
    pc = 1 - c if d & 1 else c
    return (px, py, pc), 4 * px + 2 * py + pc


def _exchange(name, arrays, scatter):
    n = len(arrays)

    def body(*refs):
        ins, outs = refs[:n], refs[n:2 * n]
        send, recv, local = refs[2 * n:]
        _, me = _peer(0)
        started = []
        for t in range(n):
            src = ins[t].at[me] if scatter else ins[t]
            cp = pltpu.make_async_copy(src, outs[t].at[me], local.at[t])
            cp.start()
            started.append(cp)
        for d in range(1, N_DEV):
            peer, pidx = _peer(d)
            for t in range(n):
                src = ins[t].at[pidx] if scatter else ins[t]
                cp = pltpu.make_async_remote_copy(src_ref=src, dst_ref=outs[t].at[me], send_sem=send.at[t, d - 1],
                                                  recv_sem=recv.at[t, d - 1], device_id=peer, device_id_type=MESH)
                cp.start()
                started.append(cp)
        for cp in started:
            cp.wait()

    out_shape = [jax.ShapeDtypeStruct(a.shape if scatter else (N_DEV,) + a.shape, a.dtype) for a in arrays]
    anyspec = pl.BlockSpec(memory_space=pl.ANY)
    return pl.pallas_call(
        body, name=name,
        in_specs=[anyspec] * n, out_specs=[anyspec] * n, out_shape=out_shape,
        scratch_shapes=[pltpu.SemaphoreType.DMA((n, N_DEV - 1)), pltpu.SemaphoreType.DMA((n, N_DEV - 1)),
                        pltpu.SemaphoreType.DMA((n,))],
        compiler_params=pltpu.CompilerParams(has_side_effects=True),
    )(*arrays)


def _adamw_math(g, w, m, v):
    m_new = ADAM_B1 * m + (1.0 - ADAM_B1) * g
    v_new = ADAM_B2 * v + (1.0 - ADAM_B2) * (g * g)
    m_hat = m_new / (1.0 - ADAM_B1 ** ADAM_STEP)
    v_hat = v_new / (1.0 - ADAM_B2 ** ADAM_STEP)
    delta = -ADAM_LR * (m_hat / (jnp.sqrt(v_hat) + ADAM_EPS) + ADAM_WD * w)
    return delta, m_new, v_new


def _adamw(name, parts, w, m, v):
    R, C = w.shape
    tr = R
    for cand in (256, 128, 64, 32, 16, 8):
        if R % cand == 0:
            tr = cand
            break

    def body(p_ref, w_ref, m_ref, v_ref, g_ref, d_ref, nm_ref, nv_ref):
        g = p_ref[0].astype(F32)
        for j in range(1, N_DEV):
            g = g + p_ref[j].astype(F32)
        g_ref[...] = g
        d_ref[...], nm_ref[...], nv_ref[...] = _adamw_math(g, w_ref[...], m_ref[...], v_ref[...])

    row = lambda i: (i, 0)
    spec = pl.BlockSpec((tr, C), row)
    return pl.pallas_call(
        body, name=name, grid=(R // tr,),
        in_specs=[pl.BlockSpec((N_DEV, tr, C), lambda i: (0, i, 0)), spec, spec, spec],
        out_specs=[spec] * 4,
        out_shape=[jax.ShapeDtypeStruct((R, C), F32)] * 4,
        compiler_params=_params("arbitrary"),
    )(parts, w, m, v)


def _rows128(a):
    flat = a.reshape(-1, LANES)
    pad = (-flat.shape[0]) % SMALL_ROWS_ALIGN
    return jnp.pad(flat, ((0, pad), (0, 0))) if pad else flat


def _unrows(packed, like):
    n = like.size // LANES
    return packed[:n].reshape(like.shape)


def kernel(x, p, ffn1_norm, ffn1_w_in, ffn1_w_out, mix_norm, w_mix_in, gmlp_v_norm, gmlp_w_s, gmlp_b, w_mix_out, ffn2_norm, ffn2_w_in, ffn2_w_out, ple_norm, ple_w_gate, ple_w_proj, final_norm, loss_target, m_ffn1_norm, m_ffn1_w_in, m_ffn1_w_out, m_mix_norm, m_w_mix_in, m_gmlp_v_norm, m_gmlp_w_s, m_gmlp_b, m_w_mix_out, m_ffn2_norm, m_ffn2_w_in, m_ffn2_w_out, m_ple_norm, m_ple_w_gate, m_ple_w_proj, m_final_norm, v_ffn1_norm, v_ffn1_w_in, v_ffn1_w_out, v_mix_norm, v_w_mix_in, v_gmlp_v_norm, v_gmlp_w_s, v_gmlp_b, v_w_mix_out, v_ffn2_norm, v_ffn2_w_in, v_ffn2_w_out, v_ple_norm, v_ple_w_gate, v_ple_w_proj, v_final_norm):
    names = ["ffn1_norm", "ffn1_w_in", "ffn1_w_out", "mix_norm", "w_mix_in", "gmlp_v_norm", "gmlp_w_s", "gmlp_b",
             "w_mix_out", "ffn2_norm", "ffn2_w_in", "ffn2_w_out", "ple_norm", "ple_w_gate", "ple_w_proj", "final_norm"]
    W = dict(zip(names, [ffn1_norm, ffn1_w_in, ffn1_w_out, mix_norm, w_mix_in, gmlp_v_norm, gmlp_w_s, gmlp_b,
                         w_mix_out, ffn2_norm, ffn2_w_in, ffn2_w_out, ple_norm, ple_w_gate, ple_w_proj, final_norm]))
    M = dict(zip(names, [m_ffn1_norm, m_ffn1_w_in, m_ffn1_w_out, m_mix_norm, m_w_mix_in, m_gmlp_v_norm, m_gmlp_w_s,
                         m_gmlp_b, m_w_mix_out, m_ffn2_norm, m_ffn2_w_in, m_ffn2_w_out, m_ple_norm, m_ple_w_gate,
                         m_ple_w_proj, m_final_norm]))
    V = dict(zip(names, [v_ffn1_norm, v_ffn1_w_in, v_ffn1_w_out, v_mix_norm, v_w_mix_in, v_gmlp_v_norm, v_gmlp_w_s,
                         v_gmlp_b, v_w_mix_out, v_ffn2_norm, v_ffn2_w_in, v_ffn2_w_out, v_ple_norm, v_ple_w_gate,
                         v_ple_w_proj, v_final_norm]))
    big = ["ffn1_w_in", "ffn1_w_out", "w_mix_in", "w_mix_out", "ffn2_w_in", "ffn2_w_out", "ple_w_gate", "ple_w_proj"]
    small = [n for n in names if n not in big]

    shards = [W[n][0].astype(BF16) for n in big]
    full = dict(zip(big, _exchange("gather_weights", shards, scatter=False)))
    D = x.shape[-1]

    def rows_full(a):
        return a.reshape(-1, a.shape[-1])

    def cols_full(a):
        return jnp.transpose(a, (1, 0, 2)).reshape(a.shape[1], -1)

    def cols_blocks(a):
        return jnp.transpose(a.reshape(a.shape[0], N_DEV, -1), (1, 0, 2))

    b_t = jnp.transpose(W["gmlp_b"][0])
    loss_part, dx, gs, gb = _local_step(
        x[0], p[0, 0].astype(BF16), loss_target[0],
        W["ffn1_norm"], W["mix_norm"], W["gmlp_v_norm"], W["gmlp_w_s"][0], b_t, W["ffn2_norm"], W["ple_norm"],
        W["final_norm"].reshape(1, D),
        full["ffn1_w_in"], rows_full(full["ffn1_w_out"]), cols_full(full["w_mix_in"]), rows_full(full["w_mix_out"]),
        full["ffn2_w_in"], rows_full(full["ffn2_w_out"]), rows_full(full["ple_w_gate"]), cols_full(full["ple_w_proj"]))

    def rows_blocks(a):
        return a.reshape(N_DEV, -1, a.shape[-1])

    send = [gb["w_in1"], rows_blocks(gb["w_out1"]), cols_blocks(gb["w_mi"]), rows_blocks(gb["w_mo"]),
            gb["w_in2"], rows_blocks(gb["w_out2"]), rows_blocks(gb["w_gate"]), cols_blocks(gb["w_proj"])]
    parts = dict(zip(big, _exchange("scatter_grads", send, scatter=True)))

    small_grads = dict(ffn1_norm=gs["g1"], mix_norm=gs["gmix"], gmlp_v_norm=gs["gv"], gmlp_w_s=gs["ws"],
                       gmlp_b=jnp.transpose(gs["b_t"]), ffn2_norm=gs["g2"], ple_norm=gs["gple"], final_norm=gs["gfin"])
    pieces = [_rows128(small_grads[n]) for n in small] + [jnp.broadcast_to(loss_part, (SMALL_ROWS_ALIGN, LANES))]
    offs = [0]
    for pc in pieces:
        offs.append(offs[-1] + pc.shape[0])
    small_parts = _exchange("gather_small", [jnp.concatenate(pieces, axis=0)], scatter=False)[0]

    out = {}
    for n in big:
        out[n] = _adamw("adamw_" + n, parts[n], W[n][0], M[n][0], V[n][0])
    pack = lambda src: jnp.concatenate([_rows128(src[n]) for n in small] + [jnp.zeros((SMALL_ROWS_ALIGN, LANES), F32)], axis=0)
    sg, sd, sm, sv = _adamw("adamw_small", small_parts, pack(W), pack(M), pack(V))
    for k, n in enumerate(small):
        out[n] = tuple(_unrows(arr[offs[k]:offs[k + 1]], W[n]) for arr in (sg, sd, sm, sv))
    loss = sg[offs[len(small)], 0]

    res = [loss, dx[None]]
    for k in range(4):
        res += [out[n][k].reshape(W[n].shape) for n in names]
    return tuple(res)
```

```python
import functools

import jax
import jax.numpy as jnp
from jax import lax
from jax.experimental import pallas as pl
from jax.experimental.pallas import tpu as pltpu

F32 = jnp.float32
BF16 = jnp.bfloat16
MESH = pl.DeviceIdType.MESH

N_DEV = 8
EPS = 1e-6
ADAM_LR = 0.001
ADAM_B1 = 0.9
ADAM_B2 = 0.999
ADAM_EPS = 1e-08
ADAM_WD = 0.01
ADAM_STEP = 10

GM_WIDTH = 512
GM_HEADS = 4
CHUNK = 128
SB_WIDTH = 512
SB_HEAD_DIM = 64
SB_SCALE = 0.125
LANES = 128
SMALL_ROWS_ALIGN = 8

ROW_TILE = 512
ATTN_Q_ROWS = 512
ATTN_KEY_BLOCK = 256
VMEM_LIMIT = 56 * 1024 * 1024


def _params(*sem):
    return pltpu.CompilerParams(dimension_semantics=sem, vmem_limit_bytes=VMEM_LIMIT)


def _dot(a, b):
    return jnp.dot(a, b, preferred_element_type=F32)


def _dot_nt(a, b):
    return lax.dot_general(a, b, (((1,), (1,)), ((), ())), preferred_element_type=F32)


def _dot_tn(a, b):
    return lax.dot_general(a, b, (((0,), (0,)), ((), ())), preferred_element_type=F32)


def _rms_parts(x):
    r = lax.rsqrt(jnp.mean(x * x, axis=-1, keepdims=True) + EPS)
    return r, x * r


def _rms_bwd(x, g, dy):
    r, xh = _rms_parts(x)
    dyg = dy * g
    dx = r * (dyg - xh * jnp.mean(dyg * xh, axis=-1, keepdims=True))
    return dx, jnp.sum(dy * xh, axis=0, keepdims=True)


def _sigmoid(x):
    return 1.0 / (1.0 + jnp.exp(-x))


_SQRT_HALF = 0.7071067811865476
_INV_SQRT_2PI = 0.3989422804014327


def _gelu(x):
    return 0.5 * x * (1.0 + lax.erf(x * _SQRT_HALF))


def _gelu_grad(x):
    return 0.5 * (1.0 + lax.erf(x * _SQRT_HALF)) + x * (_INV_SQRT_2PI * jnp.exp(-0.5 * x * x))


def _split_bf16(x):
    hi = x.astype(BF16)
    lo = (x - hi.astype(F32)).astype(BF16)
    return hi, lo


def _ffn_fwd(name, h, gain, w_in, w_out, rider=None):
    T, D = h.shape
    nb, _, FB = w_in.shape
    nh = nb // 2
    tm = min(ROW_TILE, T)

    def body(h_ref, g_ref, wg_ref, wu_ref, wo_ref, ho_ref, n_ref, G_ref, U_ref, a_ref, n_s, acc):
        jj = pl.program_id(1)

        @pl.when(jj == 0)
        def _():
            _, xh = _rms_parts(h_ref[...])
            n = (xh * g_ref[...]).astype(BF16)
            n_s[...] = n
            n_ref[...] = n
            acc[...] = jnp.zeros_like(acc)

        n = n_s[...]
        G = _dot(n, wg_ref[0])
        U = _dot(n, wu_ref[0])
        G_ref[0] = G
        U_ref[0] = U
        a = (G * _sigmoid(G) * U).astype(BF16)
        a_ref[0] = a
        acc[...] += _dot(a, wo_ref[...])

        @pl.when(jj == nh - 1)
        def _():
            ho_ref[...] = h_ref[...] + 0.5 * acc[...]

    row = lambda i, j: (i, 0)
    blk = lambda i, j: (j, i, 0)
    return _pallas(
        body, rider, name=name, grid=(T // tm, nh),
        in_specs=[pl.BlockSpec((tm, D), row),
                  pl.BlockSpec((1, D), lambda i, j: (0, 0)),
                  pl.BlockSpec((1, D, FB), lambda i, j: (j, 0, 0)),
                  pl.BlockSpec((1, D, FB), lambda i, j: (j + nh, 0, 0)),
                  pl.BlockSpec((FB, D), lambda i, j: (j, 0))],
        out_specs=[pl.BlockSpec((tm, D), row), pl.BlockSpec((tm, D), row),
                   pl.BlockSpec((1, tm, FB), blk), pl.BlockSpec((1, tm, FB), blk),
                   pl.BlockSpec((1, tm, FB), blk)],
        out_shape=[jax.ShapeDtypeStruct((T, D), F32), jax.ShapeDtypeStruct((T, D), BF16),
                   jax.ShapeDtypeStruct((nh, T, FB), F32), jax.ShapeDtypeStruct((nh, T, FB), F32),
                   jax.ShapeDtypeStruct((nh, T, FB), BF16)],
        scratch_shapes=[pltpu.VMEM((tm, D), BF16), pltpu.VMEM((tm, D), F32)],
        compiler_params=_params("arbitrary", "arbitrary"),
    )(h, gain, w_in, w_in, w_out)


def _ffn_bwd(name, dh, h_in, gain, G, U, w_in, w_out, rider=None):
    T, D = dh.shape
    nb, _, FB = w_in.shape
    nh = nb // 2
    tm = min(ROW_TILE, T)

    def body(dh_ref, h_ref, g_ref, G_ref, U_ref, wg_ref, wu_ref, wo_ref,
             dhin_ref, dg_ref, dG_ref, dU_ref, do_ref, dn_acc, do_s):
        i = pl.program_id(0)
        jj = pl.program_id(1)

        @pl.when(jj == 0)
        def _():
            d_out = (0.5 * dh_ref[...]).astype(BF16)
            do_s[...] = d_out
            do_ref[...] = d_out
            dn_acc[...] = jnp.zeros_like(dn_acc)

        @pl.when((i == 0) & (jj == 0))
        def _():
            dg_ref[...] = jnp.zeros_like(dg_ref)

        da = _dot_nt(do_s[...], wo_ref[...])
        Gv = G_ref[0]
        Uv = U_ref[0]
        sig = _sigmoid(Gv)
        dU = (da * (Gv * sig)).astype(BF16)
        dG = (da * Uv * (sig * (1.0 + Gv * (1.0 - sig)))).astype(BF16)
        dG_ref[0] = dG
        dU_ref[0] = dU
        dn_acc[...] += _dot_nt(dG, wg_ref[0]) + _dot_nt(dU, wu_ref[0])

        @pl.when(jj == nh - 1)
        def _():
            dx, dg = _rms_bwd(h_ref[...], g_ref[...], dn_acc[...])
            dhin_ref[...] = dh_ref[...] + dx
            dg_ref[...] += dg

    row = lambda i, j: (i, 0)
    blk = lambda i, j: (j, i, 0)
    one = lambda i, j: (0, 0)
    return _pallas(
        body, rider, name=name, grid=(T // tm, nh),
        in_specs=[pl.BlockSpec((tm, D), row), pl.BlockSpec((tm, D), row), pl.BlockSpec((1, D), one),
                  pl.BlockSpec((1, tm, FB), blk), pl.BlockSpec((1, tm, FB), blk),
                  pl.BlockSpec((1, D, FB), lambda i, j: (j, 0, 0)),
                  pl.BlockSpec((1, D, FB), lambda i, j: (j + nh, 0, 0)),
                  pl.BlockSpec((FB, D), lambda i, j: (j, 0))],
        out_specs=[pl.BlockSpec((tm, D), row), pl.BlockSpec((1, D), one),
                   pl.BlockSpec((1, tm, FB), blk), pl.BlockSpec((1, tm, FB), blk),
                   pl.BlockSpec((tm, D), row)],
        out_shape=[jax.ShapeDtypeStruct((T, D), F32), jax.ShapeDtypeStruct((1, D), F32),
                   jax.ShapeDtypeStruct((nh, T, FB), BF16), jax.ShapeDtypeStruct((nh, T, FB), BF16),
                   jax.ShapeDtypeStruct((T, D), BF16)],
        scratch_shapes=[pltpu.VMEM((tm, D), F32), pltpu.VMEM((tm, D), BF16)],
        compiler_params=_params("arbitrary", "arbitrary"),
    )(dh, h_in, gain, G, U, w_in, w_in, w_out)


def _matmul_tn(name, a, b, nj, a_block, a_map, b_block, b_map, out_shape, out_block, out_map):
    T = a.shape[-2]
    tt = a_block[-2]
    nt = T // tt
    kb, nbk = out_block[-2], out_block[-1]

    def body(a_ref, b_ref, o_ref, acc):
        t = pl.program_id(1)

        @pl.when(t == 0)
        def _():
            acc[...] = jnp.zeros_like(acc)

        av = a_ref[0] if len(a_block) == 3 else a_ref[...]
        bv = b_ref[0] if len(b_block) == 3 else b_ref[...]
        acc[...] += _dot_tn(av, bv)

        @pl.when(t == nt - 1)
        def _():
            if len(out_block) == 3:
                o_ref[0] = acc[...].astype(o_ref.dtype)
            else:
                o_ref[...] = acc[...].astype(o_ref.dtype)

    return pl.pallas_call(
        body, name=name, grid=(nj, nt),
        in_specs=[pl.BlockSpec(a_block, a_map), pl.BlockSpec(b_block, b_map)],
        out_specs=pl.BlockSpec(out_block, out_map),
        out_shape=jax.ShapeDtypeStruct(out_shape, BF16),
        scratch_shapes=[pltpu.VMEM((kb, nbk), F32)],
        compiler_params=_params("arbitrary", "arbitrary"),
    )(a, b)


def _mix_in_fwd(h, gain, w):
    T, D = h.shape
    W = w.shape[1]
    nuv = 2 * GM_WIDTH
    tm = min(ROW_TILE, T)

    def body(h_ref, g_ref, w_ref, n_ref, zuv_ref, qkv_ref):
        _, xh = _rms_parts(h_ref[...])
        n = (xh * g_ref[...]).astype(BF16)
        n_ref[...] = n
        z = _dot(n, w_ref[...])
        zuv_ref[...] = z[:, :nuv]
        qkv_ref[...] = z[:, nuv:].astype(BF16)

    row = lambda i: (i, 0)
    return pl.pallas_call(
        body, name="mix_in_fwd", grid=(T // tm,),
        in_specs=[pl.BlockSpec((tm, D), row), pl.BlockSpec((1, D), lambda i: (0, 0)),
                  pl.BlockSpec((D, W), lambda i: (0, 0))],
        out_specs=[pl.BlockSpec((tm, D), row), pl.BlockSpec((tm, nuv), row),
                   pl.BlockSpec((tm, W - nuv), row)],
        out_shape=[jax.ShapeDtypeStruct((T, D), BF16), jax.ShapeDtypeStruct((T, nuv), F32),
                   jax.ShapeDtypeStruct((T, W - nuv), BF16)],
        compiler_params=_params("arbitrary"),
    )(h, gain, w)


def _mix_in_bwd(dzuv, dqkv, w, h, gain, dh):
    T, D = h.shape
    W = w.shape[1]
    nuv = dzuv.shape[1]
    tm = min(ROW_TILE, T)

    def body(dzuv_ref, dqkv_ref, w_ref, h_ref, g_ref, dh_ref, dhin_ref, dg_ref):
        @pl.when(pl.program_id(0) == 0)
        def _():
            dg_ref[...] = jnp.zeros_like(dg_ref)

        dn = _dot_nt(dzuv_ref[...], w_ref[:, :nuv]) + _dot_nt(dqkv_ref[...], w_ref[:, nuv:])
        dx, dg = _rms_bwd(h_ref[...], g_ref[...], dn)
        dhin_ref[...] = dh_ref[...] + dx
        dg_ref[...] += dg

    row = lambda i: (i, 0)
    one = lambda i: (0, 0)
    return pl.pallas_call(
        body, name="mix_in_bwd", grid=(T // tm,),
        in_specs=[pl.BlockSpec((tm, nuv), row), pl.BlockSpec((tm, W - nuv), row),
                  pl.BlockSpec((D, W), one), pl.BlockSpec((tm, D), row), pl.BlockSpec((1, D), one),
                  pl.BlockSpec((tm, D), row)],
        out_specs=[pl.BlockSpec((tm, D), row), pl.BlockSpec((1, D), one)],
        out_shape=[jax.ShapeDtypeStruct((T, D), F32), jax.ShapeDtypeStruct((1, D), F32)],
        compiler_params=_params("arbitrary"),
    )(dzuv, dqkv, w, h, gain, dh)


def _gmlp_norm(zv, gv):
    v = _gelu(zv)
    r, vh = _rms_parts(v)
    return r, vh, (vh * gv).astype(BF16)


def _causal_ws(ws_ref, hd):
    r = lax.broadcasted_iota(jnp.int32, (CHUNK, CHUNK), 0)
    c = lax.broadcasted_iota(jnp.int32, (CHUNK, CHUNK), 1)
    return jnp.where(r >= c, ws_ref[hd], 0.0).astype(BF16)


def _gmlp_fwd(zuv, gv, ws, b_t):
    T = zuv.shape[0]
    tg = min(ROW_TILE, T)

    def body(zu_ref, zv_ref, gv_ref, ws_ref, bt_ref, o_ref):
        u = _gelu(zu_ref[...])
        _, _, vn = _gmlp_norm(zv_ref[...], gv_ref[...])
        for hd in range(GM_HEADS):
            wc = _causal_ws(ws_ref, hd)
            cols = slice(hd * CHUNK, (hd + 1) * CHUNK)
            for c in range(tg // CHUNK):
                rows = slice(c * CHUNK, (c + 1) * CHUNK)
                sv = _dot(wc, vn[rows, cols]) + bt_ref[:, hd:hd + 1]
                o_ref[rows, cols] = (u[rows, cols] * sv).astype(BF16)

    return pl.pallas_call(
        body, name="gmlp_fwd", grid=(T // tg,),
        in_specs=[pl.BlockSpec((tg, GM_WIDTH), lambda i: (i, 0)), pl.BlockSpec((tg, GM_WIDTH), lambda i: (i, 1)),
                  pl.BlockSpec((1, GM_WIDTH), lambda i: (0, 0)),
                  pl.BlockSpec((GM_HEADS, CHUNK, CHUNK), lambda i: (0, 0, 0)),
                  pl.BlockSpec((CHUNK, GM_HEADS), lambda i: (0, 0))],
        out_specs=pl.BlockSpec((tg, GM_WIDTH), lambda i: (i, 0)),
        out_shape=jax.ShapeDtypeStruct((T, GM_WIDTH), BF16),
        compiler_params=_params("arbitrary"),
    )(zuv, zuv, gv, ws, b_t)


def _gmlp_bwd(zuv, d_gm, gv, ws, b_t):
    T = zuv.shape[0]
    tg = min(ROW_TILE, T)
    ng = T // tg

    def body(zu_ref, zv_ref, dgm_ref, gv_ref, ws_ref, bt_ref, dz_ref, dgv_ref, dws_ref, dbt_ref, dsv_acc, dvn_s):
        i = pl.program_id(0)

        @pl.when(i == 0)
        def _():
            dgv_ref[...] = jnp.zeros_like(dgv_ref)
            dws_ref[...] = jnp.zeros_like(dws_ref)
            dsv_acc[...] = jnp.zeros_like(dsv_acc)

        zu = zu_ref[...]
        zv = zv_ref[...]
        dgm = dgm_ref[...]
        gvv = gv_ref[...]
        u = _gelu(zu)
        rv, vh, vn = _gmlp_norm(zv, gvv)
        dsv = dgm * u
        dsv_b = dsv.astype(BF16)
        for hd in range(GM_HEADS):
            wc = _causal_ws(ws_ref, hd)
            cols = slice(hd * CHUNK, (hd + 1) * CHUNK)
            dws = jnp.zeros((CHUNK, CHUNK), F32)
            dsv_sum = jnp.zeros((CHUNK, CHUNK), F32)
            for c in range(tg // CHUNK):
                rows = slice(c * CHUNK, (c + 1) * CHUNK)
                vch = vn[rows, cols]
                sv = _dot(wc, vch) + bt_ref[:, hd:hd + 1]
                dz_ref[rows, cols] = (dgm[rows, cols] * sv * _gelu_grad(zu[rows, cols])).astype(BF16)
                dws += _dot_nt(dsv_b[rows, cols], vch)
                dsv_sum += dsv[rows, cols]
                dvn_s[rows, cols] = _dot_tn(wc, dsv_b[rows, cols])
            dws_ref[hd] += dws
            dsv_acc[:, cols] += dsv_sum
        dvn = dvn_s[...]
        dvh = dvn * gvv
        dv = rv * (dvh - vh * jnp.mean(dvh * vh, axis=-1, keepdims=True))
        dgv_ref[...] += jnp.sum(dvn * vh, axis=0, keepdims=True)
        dz_ref[:, GM_WIDTH:] = (dv * _gelu_grad(zv)).astype(BF16)

        @pl.when(i == ng - 1)
        def _():
            r = lax.broadcasted_iota(jnp.int32, (CHUNK, CHUNK), 0)
            c = lax.broadcasted_iota(jnp.int32, (CHUNK, CHUNK), 1)
            for hd in range(GM_HEADS):
                dws_ref[hd] = jnp.where(r >= c, dws_ref[hd], 0.0)
                dbt_ref[:, hd:hd + 1] = jnp.sum(dsv_acc[:, hd * CHUNK:(hd + 1) * CHUNK], axis=1, keepdims=True)

    return pl.pallas_call(
        body, name="gmlp_bwd", grid=(ng,),
        in_specs=[pl.BlockSpec((tg, GM_WIDTH), lambda i: (i, 0)), pl.BlockSpec((tg, GM_WIDTH), lambda i: (i, 1)),
                  pl.BlockSpec((tg, GM_WIDTH), lambda i: (i, 0)),
                  pl.BlockSpec((1, GM_WIDTH), lambda i: (0, 0)),
                  pl.BlockSpec((GM_HEADS, CHUNK, CHUNK), lambda i: (0, 0, 0)),
                  pl.BlockSpec((CHUNK, GM_HEADS), lambda i: (0, 0))],
        out_specs=[pl.BlockSpec((tg, 2 * GM_WIDTH), lambda i: (i, 0)),
                   pl.BlockSpec((1, GM_WIDTH), lambda i: (0, 0)),
                   pl.BlockSpec((GM_HEADS, CHUNK, CHUNK), lambda i: (0, 0, 0)),
                   pl.BlockSpec((CHUNK, GM_HEADS), lambda i: (0, 0))],
        out_shape=[jax.ShapeDtypeStruct((T, 2 * GM_WIDTH), BF16), jax.ShapeDtypeStruct((1, GM_WIDTH), F32),
                   jax.ShapeDtypeStruct((GM_HEADS, CHUNK, CHUNK), F32),
                   jax.ShapeDtypeStruct((CHUNK, GM_HEADS), F32)],
        scratch_shapes=[pltpu.VMEM((CHUNK, GM_WIDTH), F32), pltpu.VMEM((tg, GM_WIDTH), F32)],
        compiler_params=_params("arbitrary"),
    )(zuv, zuv, d_gm, gv, ws, b_t)


def _scan_matrix(blk, keep):
    r = lax.broadcasted_iota(jnp.int32, (blk, blk), 0)
    c = lax.broadcasted_iota(jnp.int32, (blk, blk), 1)
    return jnp.where(keep(r, c), 1.0, 0.0).astype(BF16)


def _scan_matrix2(blk, keep, value):
    m = _scan_matrix(blk, keep) * value
    return jnp.concatenate([m, m], axis=0)


def _scan(x, mat2):
    hi, lo = _split_bf16(x)
    return _dot(jnp.concatenate([hi, lo], axis=1), mat2)


def _head_masks(q):
    lane = lax.broadcasted_iota(jnp.int32, q.shape, 1)
    m0 = lane < SB_HEAD_DIM
    zero = jnp.zeros_like(q)
    return m0, jnp.where(m0, q, zero), jnp.where(m0, zero, q)


_LOG2E = 1.4426950408889634


def _softplus_parts(z):
    e = jnp.exp2(jnp.abs(z) * (-_LOG2E))
    ope = 1.0 + e
    return e, ope, jnp.maximum(z, 0.0) + jnp.log(ope)


def _attn_fwd(qkv, rider=None):
    T = qkv.shape[0]
    tk = ATTN_KEY_BLOCK
    tq = min(ATTN_Q_ROWS, T)
    band = tq // tk
    ngrp = SB_WIDTH // LANES

    def body(q_ref, k_ref, v_ref, o_ref, l_ref, acc, run):
        i = pl.program_id(1)
        suffix = _scan_matrix2(tk, lambda r, c: r >= c, -1.0)
        row = lax.broadcasted_iota(jnp.int32, (tq, tk), 0)
        col = lax.broadcasted_iota(jnp.int32, (tq, tk), 1)
        m0, q0, q1 = _head_masks(q_ref[...] * SB_SCALE)
        acc[...] = jnp.zeros_like(acc)
        run[...] = jnp.zeros_like(run)

        def tile(j, causal):
            start = pl.multiple_of(j * tk, tk)
            kj = k_ref[pl.ds(start, tk), :]
            vj = v_ref[pl.ds(start, tk), :]
            for hd, qh in enumerate((q0, q1)):
                z = _dot_nt(qh, kj)
                _, _, sp = _softplus_parts(z)
                if causal is not None:
                    sp = jnp.where(causal, sp, 0.0)
                res = _scan(sp, suffix)
                a = jnp.exp(z + (run[hd] + res))
                if causal is not None:
                    a = jnp.where(causal, a, 0.0)
                acc[hd] += _dot(a.astype(BF16), vj)
                run[hd] += res[:, 0:1]

        def band_step(it, carry):
            jb = band - 1 - it
            tile(i * band + jb, jb * tk + col < row)
            return carry

        def full_step(it, carry):
            tile(i * band - 1 - it, None)
            return carry

        lax.fori_loop(0, band, band_step, 0)
        lax.fori_loop(0, i * band, full_step, 0)
        o_ref[...] = jnp.where(m0, acc[0], acc[1]).astype(BF16)
        l_ref[...] = jnp.where(m0, jnp.broadcast_to(run[0], (tq, LANES)), jnp.broadcast_to(run[1], (tq, LANES)))

    return _pallas(
        body, rider, name="attn_fwd", grid=(ngrp, T // tq),
        in_specs=[pl.BlockSpec((tq, LANES), lambda g, i: (i, g)),
                  pl.BlockSpec((T, LANES), lambda g, i: (0, ngrp + g)),
                  pl.BlockSpec((T, LANES), lambda g, i: (0, 2 * ngrp + g))],
        out_specs=[pl.BlockSpec((tq, LANES), lambda g, i: (i, g)),
                   pl.BlockSpec((tq, LANES), lambda g, i: (i, g))],
        out_shape=[jax.ShapeDtypeStruct((T, SB_WIDTH), BF16), jax.ShapeDtypeStruct((T, SB_WIDTH), F32)],
        scratch_shapes=[pltpu.VMEM((2, tq, LANES), F32), pltpu.VMEM((2, tq, 1), F32)],
        compiler_params=_params("arbitrary", "arbitrary"),
    )(qkv, qkv, qkv)


def _attn_bwd(qkv, d_o, ltot, rider=None):
    T = qkv.shape[0]
    tk = ATTN_KEY_BLOCK
    tq = min(ATTN_Q_ROWS, T)
    band = tq // tk
    nq = T // tq
    ngrp = SB_WIDTH // LANES

    def body(q_ref, k_ref, v_ref, do_ref, l_ref, dq_ref, dk_ref, dv_ref, dq_acc, dk_acc, dv_acc, lpre, ppre):
        i = pl.program_id(1)

        @pl.when(i == 0)
        def _():
            dk_acc[...] = jnp.zeros_like(dk_acc)
            dv_acc[...] = jnp.zeros_like(dv_acc)

        excl = _scan_matrix(tk, lambda r, c: r < c)
        excl2 = jnp.concatenate([excl, excl], axis=0)
        row = lax.broadcasted_iota(jnp.int32, (tq, tk), 0)
        col = lax.broadcasted_iota(jnp.int32, (tq, tk), 1)
        m0, q0, q1 = _head_masks(q_ref[...] * SB_SCALE)
        _, d0, d1 = _head_masks(do_ref[...].astype(BF16))
        lt = l_ref[...]
        ltots = (lt[:, 0:1], lt[:, SB_HEAD_DIM:SB_HEAD_DIM + 1])
        dq_acc[...] = jnp.zeros_like(dq_acc)
        lpre[...] = jnp.zeros_like(lpre)
        ppre[...] = jnp.zeros_like(ppre)

        def tile(j, causal):
            start = pl.multiple_of(j * tk, tk)
            kj = k_ref[pl.ds(start, tk), :]
            vj = v_ref[pl.ds(start, tk), :]
            for hd, (qh, dh) in enumerate(((q0, d0), (q1, d1))):
                z = _dot_nt(qh, kj)
                e, ope, sp = _softplus_parts(z)
                rinv = 1.0 / ope
                small = e * rinv
                pos = z >= 0.0
                beta = jnp.where(pos, rinv, small)
                one_m_beta = jnp.where(pos, small, rinv)
                if causal is not None:
                    sp = jnp.where(causal, sp, 0.0)
                res = _scan(sp, excl2)
                a = jnp.exp(z + ((ltots[hd] + lpre[hd]) + res))
                if causal is not None:
                    a = jnp.where(causal, a, 0.0)
                p = a * _dot_nt(dh, vj)
                resp = _dot(p.astype(BF16), excl)
                dz = p * one_m_beta - beta * (ppre[hd] + resp)
                if causal is not None:
                    dz = jnp.where(causal, dz, 0.0)
                dzb = dz.astype(BF16)
                dq_acc[hd] += _dot(dzb, kj)
                dk_acc[pl.ds(start, tk), :] += _dot_tn(dzb, qh)
                dv_acc[pl.ds(start, tk), :] += _dot_tn(a.astype(BF16), dh)
                lpre[hd] += res[:, tk - 1:tk] + sp[:, tk - 1:tk]
                ppre[hd] += resp[:, tk - 1:tk] + p[:, tk - 1:tk]

        def full_step(j, carry):
            tile(j, None)
            return carry

        def band_step(jb, carry):
            tile(i * band + jb, jb * tk + col < row)
            return carry

        lax.fori_loop(0, i * band, full_step, 0)
        lax.fori_loop(0, band, band_step, 0)
        dq_ref[...] = (jnp.where(m0, dq_acc[0], dq_acc[1]) * SB_SCALE).astype(BF16)

        @pl.when(i == nq - 1)
        def _():
            dk_ref[...] = dk_acc[...].astype(BF16)
            dv_ref[...] = dv_acc[...].astype(BF16)

    qmap = lambda g, i: (i, g)
    return _pallas(
        body, rider, name="attn_bwd", grid=(ngrp, nq),
        in_specs=[pl.BlockSpec((tq, LANES), qmap),
                  pl.BlockSpec((T, LANES), lambda g, i: (0, ngrp + g)),
                  pl.BlockSpec((T, LANES), lambda g, i: (0, 2 * ngrp + g)),
                  pl.BlockSpec((tq, LANES), qmap), pl.BlockSpec((tq, LANES), qmap)],
        out_specs=[pl.BlockSpec((tq, LANES), qmap),
                   pl.BlockSpec((T, LANES), lambda g, i: (0, g)),
                   pl.BlockSpec((T, LANES), lambda g, i: (0, g))],
        out_shape=[jax.ShapeDtypeStruct((T, SB_WIDTH), BF16)] * 3,
        scratch_shapes=[pltpu.VMEM((2, tq, LANES), F32), pltpu.VMEM((T, LANES), F32),
                        pltpu.VMEM((T, LANES), F32), pltpu.VMEM((2, tq, 1), F32),
                        pltpu.VMEM((2, tq, 1), F32)],
        compiler_params=_params("arbitrary", "arbitrary"),
    )(qkv, qkv, qkv, d_o, ltot)


def _mix_out_fwd(h, gm, sb, w):
    T, D = h.shape
    tm = min(ROW_TILE, T)

    def body(h_ref, gm_ref, sb_ref, w_ref, o_ref):
        o_ref[...] = h_ref[...] + _dot(gm_ref[...], w_ref[:GM_WIDTH, :]) + _dot(sb_ref[...], w_ref[GM_WIDTH:, :])

    row = lambda i: (i, 0)
    return pl.pallas_call(
        body, name="mix_out_fwd", grid=(T // tm,),
        in_specs=[pl.BlockSpec((tm, D), row), pl.BlockSpec((tm, GM_WIDTH), row), pl.BlockSpec((tm, SB_WIDTH), row),
                  pl.BlockSpec((GM_WIDTH + SB_WIDTH, D), lambda i: (0, 0))],
        out_specs=pl.BlockSpec((tm, D), row),
        out_shape=jax.ShapeDtypeStruct((T, D), F32),
        compiler_params=_params("arbitrary"),
    )(h, gm, sb, w)


def _mix_out_bwd(dh, w):
    T, D = dh.shape
    tm = min(ROW_TILE, T)

    def body(dh_ref, w_ref, dgm_ref, dsb_ref, dhb_ref):
        dhb = dh_ref[...].astype(BF16)
        dhb_ref[...] = dhb
        dgm_ref[...] = _dot_nt(dhb, w_ref[:GM_WIDTH, :])
        dsb_ref[...] = _dot_nt(dhb, w_ref[GM_WIDTH:, :])

    row = lambda i: (i, 0)
    return pl.pallas_call(
        body, name="mix_out_bwd", grid=(T // tm,),
        in_specs=[pl.BlockSpec((tm, D), row), pl.BlockSpec((GM_WIDTH + SB_WIDTH, D), lambda i: (0, 0))],
        out_specs=[pl.BlockSpec((tm, GM_WIDTH), row), pl.BlockSpec((tm, SB_WIDTH), row), pl.BlockSpec((tm, D), row)],
        out_shape=[jax.ShapeDtypeStruct((T, GM_WIDTH), F32), jax.ShapeDtypeStruct((T, SB_WIDTH), F32),
                   jax.ShapeDtypeStruct((T, D), BF16)],
        compiler_params=_params("arbitrary"),
    )(dh, w)


def _tail(h3, p, target, g_ple, g_fin, w_gate, w_proj):
    T, D = h3.shape
    PD = p.shape[1]
    tm = min(ROW_TILE, T)

    def body(h_ref, p_ref, t_ref, gp_ref, gf_ref, wg_ref, wp_ref,
             loss_ref, dh_ref, n4_ref, dgl_ref, dpp_ref, dgp_ref, dgf_ref):
        @pl.when(pl.program_id(0) == 0)
        def _():
            loss_ref[...] = jnp.zeros_like(loss_ref)
            dgp_ref[...] = jnp.zeros_like(dgp_ref)
            dgf_ref[...] = jnp.zeros_like(dgf_ref)

        h3v = h_ref[...]
        gp = gp_ref[...]
        gf = gf_ref[...]
        r3, xh3 = _rms_parts(h3v)
        n4 = (xh3 * gp).astype(BF16)
        n4_ref[...] = n4
        gate = _sigmoid(_dot(n4, wg_ref[...]))
        pp = _dot(p_ref[...], wp_ref[...])
        h4 = h3v + gate * pp
        r4, xh4 = _rms_parts(h4)
        err = xh4 * gf - t_ref[...]
        loss_ref[...] += jnp.full(loss_ref.shape, (0.5 / D) * jnp.sum(err * err), F32)
        dy = err * (1.0 / D)
        dgf_ref[...] += jnp.sum(dy * xh4, axis=0, keepdims=True)
        dyg = dy * gf
        dh4 = r4 * (dyg - xh4 * jnp.mean(dyg * xh4, axis=-1, keepdims=True))
        dpp_ref[...] = (dh4 * gate).astype(BF16)
        dgl = (dh4 * pp * gate * (1.0 - gate)).astype(BF16)
        dgl_ref[...] = dgl
        dn4 = _dot_nt(dgl, wg_ref[...])
        dgp_ref[...] += jnp.sum(dn4 * xh3, axis=0, keepdims=True)
        dn4g = dn4 * gp
        dh_ref[...] = dh4 + r3 * (dn4g - xh3 * jnp.mean(dn4g * xh3, axis=-1, keepdims=True))

    row = lambda i: (i, 0)
    one = lambda i: (0, 0)
    return pl.pallas_call(
        body, name="tail", grid=(T // tm,),
        in_specs=[pl.BlockSpec((tm, D), row), pl.BlockSpec((tm, PD), row), pl.BlockSpec((tm, D), row),
                  pl.BlockSpec((1, D), one), pl.BlockSpec((1, D), one),
                  pl.BlockSpec((D, D), one), pl.BlockSpec((PD, D), one)],
        out_specs=[pl.BlockSpec((1, LANES), one), pl.BlockSpec((tm, D), row), pl.BlockSpec((tm, D), row),
                   pl.BlockSpec((tm, D), row), pl.BlockSpec((tm, D), row),
                   pl.BlockSpec((1, D), one), pl.BlockSpec((1, D), one)],
        out_shape=[jax.ShapeDtypeStruct((1, LANES), F32), jax.ShapeDtypeStruct((T, D), F32),
                   jax.ShapeDtypeStruct((T, D), BF16), jax.ShapeDtypeStruct((T, D), BF16),
                   jax.ShapeDtypeStruct((T, D), BF16),
                   jax.ShapeDtypeStruct((1, D), F32), jax.ShapeDtypeStruct((1, D), F32)],
        compiler_params=_params("arbitrary"),
    )(h3, p, target, g_ple, g_fin, w_gate, w_proj)


FFN1_W = ("ffn1_w_in", "ffn1_w_out")
MIX_W = ("w_mix_in", "w_mix_out")
REST_W = ("ffn2_w_in", "ffn2_w_out", "ple_w_gate", "ple_w_proj")
BIG_W = FFN1_W + MIX_W + REST_W
COLUMN_SHARDED = ("w_mix_in", "ple_w_proj")


class _Traffic:
    def __init__(self, shards):
        self.shards = shards
        self.parts = {}

    @staticmethod
    def _full(name, gathered):
        if name in COLUMN_SHARDED:
            return jnp.transpose(gathered, (1, 0, 2)).reshape(gathered.shape[1], -1)
        if name.endswith("_w_in"):
            return gathered
        return gathered.reshape(-1, gathered.shape[-1])

    @staticmethod
    def _blocks(name, grad):
        if name in COLUMN_SHARDED:
            return jnp.transpose(grad.reshape(grad.shape[0], N_DEV, -1), (1, 0, 2))
        if name.endswith("_w_in"):
            return grad
        return grad.reshape(N_DEV, -1, grad.shape[-1])

    def gather_now(self, names):
        got = _exchange("gather_" + names[0], [self.shards[n] for n in names], [False] * len(names))
        return self.gathered(names, got)

    def gather_rider(self, names):
        return [self.shards[n] for n in names], [False] * len(names)

    def gathered(self, names, got):
        return {n: self._full(n, g) for n, g in zip(names, got)}

    def scatter_rider(self, grads):
        return [self._blocks(n, g) for n, g in grads.items()], [True] * len(grads)

    def scattered(self, names, got):
        self.parts.update(zip(names, got))

    def finish(self, grads, small):
        arrays, flags = self.scatter_rider(grads)
        got = _exchange("scatter_last", arrays + [small], flags + [False])
        self.scattered(list(grads), got[:-1])
        return got[-1]


def _local_step(traffic, x, p_bf, target, g1, gmix, gv, ws, b_t, g2, gple, gfin, pack_small):
    T, D = x.shape
    tm = min(ROW_TILE, T)

    w = traffic.gather_now(FFN1_W)
    h1, n1, G1, U1, a1, *got = _ffn_fwd("ffn1_fwd", x, g1, w["ffn1_w_in"], w["ffn1_w_out"],
                                        rider=traffic.gather_rider(MIX_W))
    w.update(traffic.gathered(MIX_W, got))
    n2, zuv, qkv = _mix_in_fwd(h1, gmix, w["w_mix_in"])
    gm = _gmlp_fwd(zuv, gv, ws, b_t)
    sb, ltot, *got = _attn_fwd(qkv, rider=traffic.gather_rider(REST_W))
    w.update(traffic.gathered(REST_W, got))
    h2 = _mix_out_fwd(h1, gm, sb, w["w_mix_out"])
    h3, n3, G2, U2, a2 = _ffn_fwd("ffn2_fwd", h2, g2, w["ffn2_w_in"], w["ffn2_w_out"])
    loss, dh3, n4, d_gl, d_pp, dg_ple, dg_fin = _tail(h3, p_bf, target, gple, gfin, w["ple_w_gate"], w["ple_w_proj"])

    nb, _, FB = w["ffn1_w_in"].shape
    nh = nb // 2

    def weight_grads(tag, n, a, d_out, dG, dU):
        nmap = lambda j, t: (t, 0)
        zmap = lambda j, t: (j, t, 0)
        omap = lambda j, t: (j, 0, 0)
        dwg = _matmul_tn(tag + "_dw_gate", n, dG, nh, (tm, D), nmap, (1, tm, FB), zmap, (nh, D, FB), (1, D, FB), omap)
        dwu = _matmul_tn(tag + "_dw_up", n, dU, nh, (tm, D), nmap, (1, tm, FB), zmap, (nh, D, FB), (1, D, FB), omap)
        dwo = _matmul_tn(tag + "_dw_out", a, d_out, nh, (1, tm, FB), zmap, (tm, D), nmap, (nh, FB, D), (1, FB, D), omap)
        return jnp.concatenate([dwg, dwu], axis=0), dwo

    def dense_tn(name, a, b, ncol):
        ka, nbw = a.shape[1], b.shape[1] // ncol
        return _matmul_tn(name, a, b, ncol, (tm, ka), lambda j, t: (t, 0), (tm, nbw), lambda j, t: (t, j),
                          (ka, b.shape[1]), (ka, nbw), lambda j, t: (0, j))

    grads = dict(ple_w_gate=dense_tn("dw_ple_gate", n4, d_gl, 2), ple_w_proj=dense_tn("dw_ple_proj", p_bf, d_pp, 1))
    dh2, dg2, dG2, dU2, dout2 = _ffn_bwd("ffn2_bwd", dh3, h2, g2, G2, U2, w["ffn2_w_in"], w["ffn2_w_out"])
    grads["ffn2_w_in"], grads["ffn2_w_out"] = weight_grads("ffn2", n3, a2, dout2, dG2, dU2)
    grads = {n: grads[n] for n in REST_W}

    d_gm, d_sb, dh2_bf = _mix_out_bwd(dh2, w["w_mix_out"])
    dw_mo = jnp.concatenate([dense_tn("dw_mix_out_gm", gm, dh2_bf, 1), dense_tn("dw_mix_out_sb", sb, dh2_bf, 1)], axis=0)
    dzuv, dgv, dws, db_t = _gmlp_bwd(zuv, d_gm, gv, ws, b_t)
    dq, dk, dv, *got = _attn_bwd(qkv, d_sb, ltot, rider=traffic.scatter_rider(grads))
    traffic.scattered(REST_W, got)
    dqkv = jnp.concatenate([dq, dk, dv], axis=1)
    dw_mi = jnp.concatenate([dense_tn("dw_mix_in_uv", n2, dzuv, 2), dense_tn("dw_mix_in_qkv", n2, dqkv, 3)], axis=1)
    dh1, dgmix = _mix_in_bwd(dzuv, dqkv, w["w_mix_in"], h1, gmix, dh2)

    grads = dict(w_mix_in=dw_mi, w_mix_out=dw_mo)
    dx, dg1, dG1, dU1, dout1, *got = _ffn_bwd("ffn1_bwd", dh1, x, g1, G1, U1, w["ffn1_w_in"], w["ffn1_w_out"],
                                             rider=traffic.scatter_rider(grads))
    traffic.scattered(MIX_W, got)
    dw_in1, dw_out1 = weight_grads("ffn1", n1, a1, dout1, dG1, dU1)

    small = pack_small(dict(ffn1_norm=dg1, mix_norm=dgmix, gmlp_v_norm=dgv, gmlp_w_s=dws, gmlp_b=jnp.transpose(db_t),
                            ffn2_norm=dg2, ple_norm=dg_ple, final_norm=dg_fin), loss)
    return dx, traffic.finish(dict(ffn1_w_in=dw_in1, ffn1_w_out=dw_out1), small)


def _peer(d):
    x, y, c = lax.axis_index("x"), lax.axis_index("y"), lax.axis_index("c")
    px = 1 - x if d & 4 else x
    py = 1 - y if d & 2 else y
    pc = 1 - c if d & 1 else c
    return (px, py, pc), 4 * px + 2 * py + pc


def _exchange_copies(ins, outs, send, recv, local, scatter):
    _, me = _peer(0)
    copies = []
    for t in range(len(ins)):
        src = ins[t].at[me] if scatter[t] else ins[t]
        copies.append(pltpu.make_async_copy(src, outs[t].at[me], local.at[t]))
    for d in range(1, N_DEV):
        peer, pidx = _peer(d)
        for t in range(len(ins)):
            src = ins[t].at[pidx] if scatter[t] else ins[t]
            copies.append(pltpu.make_async_remote_copy(
                src_ref=src, dst_ref=outs[t].at[me], send_sem=send.at[t, d - 1], recv_sem=recv.at[t, d - 1],
                device_id=peer, device_id_type=MESH))
    return copies


def _exchange_shapes(arrays, scatter):
    return [jax.ShapeDtypeStruct(a.shape if sc else (N_DEV,) + a.shape, a.dtype) for a, sc in zip(arrays, scatter)]


def _exchange_sems(n):
    return [pltpu.SemaphoreType.DMA((n, N_DEV - 1)), pltpu.SemaphoreType.DMA((n, N_DEV - 1)),
            pltpu.SemaphoreType.DMA((n,))]


_ANY = pl.BlockSpec(memory_space=pl.ANY)


def _exchange(name, arrays, scatter):
    n = len(arrays)

    def body(*refs):
        copies = _exchange_copies(refs[:n], refs[n:2 * n], *refs[2 * n:], scatter)
        for cp in copies:
            cp.start()
        for cp in copies:
            cp.wait()

    return pl.pallas_call(
        body, name=name, in_specs=[_ANY] * n, out_specs=[_ANY] * n, out_shape=_exchange_shapes(arrays, scatter),
        scratch_shapes=_exchange_sems(n),
    )(*arrays)


def _pallas(body, rider, *, name, grid, in_specs, out_specs, out_shape, scratch_shapes=(), compiler_params=None):
    if rider is None:
        return pl.pallas_call(body, name=name, grid=grid, in_specs=in_specs, out_specs=out_specs, out_shape=out_shape,
                              scratch_shapes=list(scratch_shapes), compiler_params=compiler_params)
    arrays, scatter = rider
    n, ni, no, ns = len(arrays), len(in_specs), len(out_specs), len(scratch_shapes)

    def carried(*refs):
        ins, r_in = refs[:ni], refs[ni:ni + n]
        outs, r_out = refs[ni + n:ni + n + no], refs[ni + n + no:ni + 2 * n + no]
        scratch, sems = refs[ni + 2 * n + no:ni + 2 * n + no + ns], refs[ni + 2 * n + no + ns:]
        ids = [pl.program_id(ax) for ax in range(len(grid))]
        first = functools.reduce(jnp.logical_and, [i == 0 for i in ids])
        last = functools.reduce(jnp.logical_and, [i == g - 1 for i, g in zip(ids, grid)])

        @pl.when(first)
        def _():
            for cp in _exchange_copies(r_in, r_out, *sems, scatter):
                cp.start()

        body(*ins, *outs, *scratch)

        @pl.when(last)
        def _():
            for cp in _exchange_copies(r_in, r_out, *sems, scatter):
                cp.wait()

    call = pl.pallas_call(
        carried, name=name, grid=grid, in_specs=list(in_specs) + [_ANY] * n, out_specs=list(out_specs) + [_ANY] * n,
        out_shape=list(out_shape) + _exchange_shapes(arrays, scatter),
        scratch_shapes=list(scratch_shapes) + _exchange_sems(n), compiler_params=compiler_params)
    return lambda *args: call(*args, *arrays)


def _adamw_math(g, w, m, v):
    m_new = ADAM_B1 * m + (1.0 - ADAM_B1) * g
    v_new = ADAM_B2 * v + (1.0 - ADAM_B2) * (g * g)
    m_hat = m_new / (1.0 - ADAM_B1 ** ADAM_STEP)
    v_hat = v_new / (1.0 - ADAM_B2 ** ADAM_STEP)
    delta = -ADAM_LR * (m_hat / (jnp.sqrt(v_hat) + ADAM_EPS) + ADAM_WD * w)
    return delta, m_new, v_new


def _adamw(name, parts, w, m, v):
    R, C = w.shape
    tr = R
    for cand in (256, 128, 64, 32, 16, 8):
        if R % cand == 0:
            tr = cand
            break

    def body(p_ref, w_ref, m_ref, v_ref, g_ref, d_ref, nm_ref, nv_ref):
        g = p_ref[0].astype(F32)
        for j in range(1, N_DEV):
            g = g + p_ref[j].astype(F32)
        g_ref[...] = g
        d_ref[...], nm_ref[...], nv_ref[...] = _adamw_math(g, w_ref[...], m_ref[...], v_ref[...])

    row = lambda i: (i, 0)
    spec = pl.BlockSpec((tr, C), row)
    return pl.pallas_call(
        body, name=name, grid=(R // tr,),
        in_specs=[pl.BlockSpec((N_DEV, tr, C), lambda i: (0, i, 0)), spec, spec, spec],
        out_specs=[spec] * 4,
        out_shape=[jax.ShapeDtypeStruct((R, C), F32)] * 4,
        compiler_params=_params("arbitrary"),
    )(parts, w, m, v)


def _rows128(a):
    flat = a.reshape(-1, LANES)
    pad = (-flat.shape[0]) % SMALL_ROWS_ALIGN
    return jnp.pad(flat, ((0, pad), (0, 0))) if pad else flat


def _unrows(packed, like):
    n = like.size // LANES
    return packed[:n].reshape(like.shape)


def kernel(x, p, ffn1_norm, ffn1_w_in, ffn1_w_out, mix_norm, w_mix_in, gmlp_v_norm, gmlp_w_s, gmlp_b, w_mix_out, ffn2_norm, ffn2_w_in, ffn2_w_out, ple_norm, ple_w_gate, ple_w_proj, final_norm, loss_target, m_ffn1_norm, m_ffn1_w_in, m_ffn1_w_out, m_mix_norm, m_w_mix_in, m_gmlp_v_norm, m_gmlp_w_s, m_gmlp_b, m_w_mix_out, m_ffn2_norm, m_ffn2_w_in, m_ffn2_w_out, m_ple_norm, m_ple_w_gate, m_ple_w_proj, m_final_norm, v_ffn1_norm, v_ffn1_w_in, v_ffn1_w_out, v_mix_norm, v_w_mix_in, v_gmlp_v_norm, v_gmlp_w_s, v_gmlp_b, v_w_mix_out, v_ffn2_norm, v_ffn2_w_in, v_ffn2_w_out, v_ple_norm, v_ple_w_gate, v_ple_w_proj, v_final_norm):
    names = ["ffn1_norm", "ffn1_w_in", "ffn1_w_out", "mix_norm", "w_mix_in", "gmlp_v_norm", "gmlp_w_s", "gmlp_b",
             "w_mix_out", "ffn2_norm", "ffn2_w_in", "ffn2_w_out", "ple_norm", "ple_w_gate", "ple_w_proj", "final_norm"]
    W = dict(zip(names, [ffn1_norm, ffn1_w_in, ffn1_w_out, mix_norm, w_mix_in, gmlp_v_norm, gmlp_w_s, gmlp_b,
                         w_mix_out, ffn2_norm, ffn2_w_in, ffn2_w_out, ple_norm, ple_w_gate, ple_w_proj, final_norm]))
    M = dict(zip(names, [m_ffn1_norm, m_ffn1_w_in, m_ffn1_w_out, m_mix_norm, m_w_mix_in, m_gmlp_v_norm, m_gmlp_w_s,
                         m_gmlp_b, m_w_mix_out, m_ffn2_norm, m_ffn2_w_in, m_ffn2_w_out, m_ple_norm, m_ple_w_gate,
                         m_ple_w_proj, m_final_norm]))
    V = dict(zip(names, [v_ffn1_norm, v_ffn1_w_in, v_ffn1_w_out, v_mix_norm, v_w_mix_in, v_gmlp_v_norm, v_gmlp_w_s,
                         v_gmlp_b, v_w_mix_out, v_ffn2_norm, v_ffn2_w_in, v_ffn2_w_out, v_ple_norm, v_ple_w_gate,
                         v_ple_w_proj, v_final_norm]))
    small = [n for n in names if n not in BIG_W]
    D = x.shape[-1]

    def pack(src, last):
        return jnp.concatenate([_rows128(src[n]) for n in small] + [last], axis=0)

    offs = [0]
    for n in small:
        offs.append(offs[-1] + _rows128(W[n]).shape[0])

    traffic = _Traffic({n: W[n][0].astype(BF16) for n in BIG_W})
    dx, small_parts = _local_step(
        traffic, x[0], p[0, 0].astype(BF16), loss_target[0],
        W["ffn1_norm"], W["mix_norm"], W["gmlp_v_norm"], W["gmlp_w_s"][0], jnp.transpose(W["gmlp_b"][0]),
        W["ffn2_norm"], W["ple_norm"], W["final_norm"].reshape(1, D),
        lambda grads, loss_part: pack(grads, jnp.broadcast_to(loss_part, (SMALL_ROWS_ALIGN, LANES))))

    out = {}
    for n in BIG_W:
        out[n] = _adamw("adamw_" + n, traffic.parts[n], W[n][0], M[n][0], V[n][0])
    zeros = jnp.zeros((SMALL_ROWS_ALIGN, LANES), F32)
    sg, sd, sm, sv = _adamw("adamw_small", small_parts, pack(W, zeros), pack(M, zeros), pack(V, zeros))
    for k, n in enumerate(small):
        out[n] = tuple(_unrows(arr[offs[k]:offs[k + 1]], W[n]) for arr in (sg, sd, sm, sv))
    loss = sg[offs[len(small)], 0]

    res = [loss, dx[None]]
    for k in range(4):
        res += [out[n][k].reshape(W[n].shape) for n in names]
    return tuple(res)
```

```python
import functools

import jax
import jax.numpy as jnp
from jax import lax
from jax.experimental import pallas as pl
from jax.experimental.pallas import tpu as pltpu

F32 = jnp.float32
BF16 = jnp.bfloat16
MESH = pl.DeviceIdType.MESH

N_DEV = 8
EPS = 1e-6
ADAM_LR = 0.001
ADAM_B1 = 0.9
ADAM_B2 = 0.999
ADAM_EPS = 1e-08
ADAM_WD = 0.01
ADAM_STEP = 10

GM_WIDTH = 512
GM_HEADS = 4
CHUNK = 128
SB_WIDTH = 512
SB_HEAD_DIM = 64
SB_SCALE = 0.125
LANES = 128
SMALL_ROWS_ALIGN = 8

ROW_TILE = 512
GRAD_ROW_TILE = 2048
ATTN_Q_ROWS = 512
ATTN_KEY_BLOCK = 256
ATTN_UNROLL = 2
VMEM_LIMIT = 56 * 1024 * 1024


def _params(*sem):
    return pltpu.CompilerParams(dimension_semantics=sem, vmem_limit_bytes=VMEM_LIMIT)


def _dot(a, b):
    return jnp.dot(a, b, preferred_element_type=F32)


def _dot_nt(a, b):
    return lax.dot_general(a, b, (((1,), (1,)), ((), ())), preferred_element_type=F32)


def _dot_tn(a, b):
    return lax.dot_general(a, b, (((0,), (0,)), ((), ())), preferred_element_type=F32)


def _rms_parts(x):
    r = lax.rsqrt(jnp.mean(x * x, axis=-1, keepdims=True) + EPS)
    return r, x * r


def _rms_bwd(x, g, dy):
    r, xh = _rms_parts(x)
    dyg = dy * g
    dx = r * (dyg - xh * jnp.mean(dyg * xh, axis=-1, keepdims=True))
    return dx, jnp.sum(dy * xh, axis=0, keepdims=True)


def _sigmoid(x):
    return 1.0 / (1.0 + jnp.exp(-x))


_SQRT_HALF = 0.7071067811865476
_INV_SQRT_2PI = 0.3989422804014327


def _gelu(x):
    return 0.5 * x * (1.0 + lax.erf(x * _SQRT_HALF))


def _gelu_grad(x):
    return 0.5 * (1.0 + lax.erf(x * _SQRT_HALF)) + x * (_INV_SQRT_2PI * jnp.exp(-0.5 * x * x))


def _split_bf16(x):
    hi = x.astype(BF16)
    lo = (x - hi.astype(F32)).astype(BF16)
    return hi, lo


def _ffn_fwd(name, h, gain, w_in, w_out, rider=None):
    T, D = h.shape
    nb, _, FB = w_in.shape
    nh = nb // 2
    tm = min(ROW_TILE, T)

    def body(h_ref, g_ref, wg_ref, wu_ref, wo_ref, ho_ref, n_ref, G_ref, U_ref, a_ref, n_s, acc):
        jj = pl.program_id(1)

        @pl.when(jj == 0)
        def _():
            _, xh = _rms_parts(h_ref[...])
            n = (xh * g_ref[...]).astype(BF16)
            n_s[...] = n
            n_ref[...] = n
            acc[...] = jnp.zeros_like(acc)

        n = n_s[...]
        G = _dot(n, wg_ref[0])
        U = _dot(n, wu_ref[0])
        G_ref[0] = G
        U_ref[0] = U
        a = (G * _sigmoid(G) * U).astype(BF16)
        a_ref[0] = a
        acc[...] += _dot(a, wo_ref[...])

        @pl.when(jj == nh - 1)
        def _():
            ho_ref[...] = h_ref[...] + 0.5 * acc[...]

    row = lambda i, j: (i, 0)
    blk = lambda i, j: (j, i, 0)
    return _pallas(
        body, rider, name=name, grid=(T // tm, nh),
        in_specs=[pl.BlockSpec((tm, D), row),
                  pl.BlockSpec((1, D), lambda i, j: (0, 0)),
                  pl.BlockSpec((1, D, FB), lambda i, j: (j, 0, 0)),
                  pl.BlockSpec((1, D, FB), lambda i, j: (j + nh, 0, 0)),
                  pl.BlockSpec((FB, D), lambda i, j: (j, 0))],
        out_specs=[pl.BlockSpec((tm, D), row), pl.BlockSpec((tm, D), row),
                   pl.BlockSpec((1, tm, FB), blk), pl.BlockSpec((1, tm, FB), blk),
                   pl.BlockSpec((1, tm, FB), blk)],
        out_shape=[jax.ShapeDtypeStruct((T, D), F32), jax.ShapeDtypeStruct((T, D), BF16),
                   jax.ShapeDtypeStruct((nh, T, FB), F32), jax.ShapeDtypeStruct((nh, T, FB), F32),
                   jax.ShapeDtypeStruct((nh, T, FB), BF16)],
        scratch_shapes=[pltpu.VMEM((tm, D), BF16), pltpu.VMEM((tm, D), F32)],
        compiler_params=_params("arbitrary", "arbitrary"),
    )(h, gain, w_in, w_in, w_out)


def _ffn_bwd(name, dh, h_in, gain, G, U, w_in, w_out, rider=None):
    T, D = dh.shape
    nb, _, FB = w_in.shape
    nh = nb // 2
    tm = min(ROW_TILE, T)

    def body(dh_ref, h_ref, g_ref, G_ref, U_ref, wg_ref, wu_ref, wo_ref,
             dhin_ref, dg_ref, dG_ref, dU_ref, do_ref, dn_acc, do_s):
        i = pl.program_id(0)
        jj = pl.program_id(1)

        @pl.when(jj == 0)
        def _():
            d_out = (0.5 * dh_ref[...]).astype(BF16)
            do_s[...] = d_out
            do_ref[...] = d_out
            dn_acc[...] = jnp.zeros_like(dn_acc)

        @pl.when((i == 0) & (jj == 0))
        def _():
            dg_ref[...] = jnp.zeros_like(dg_ref)

        da = _dot_nt(do_s[...], wo_ref[...])
        Gv = G_ref[0]
        Uv = U_ref[0]
        sig = _sigmoid(Gv)
        dU = (da * (Gv * sig)).astype(BF16)
        dG = (da * Uv * (sig * (1.0 + Gv * (1.0 - sig)))).astype(BF16)
        dG_ref[0] = dG
        dU_ref[0] = dU
        dn_acc[...] += _dot_nt(dG, wg_ref[0]) + _dot_nt(dU, wu_ref[0])

        @pl.when(jj == nh - 1)
        def _():
            dx, dg = _rms_bwd(h_ref[...], g_ref[...], dn_acc[...])
            dhin_ref[...] = dh_ref[...] + dx
            dg_ref[...] += dg

    row = lambda i, j: (i, 0)
    blk = lambda i, j: (j, i, 0)
    one = lambda i, j: (0, 0)
    return _pallas(
        body, rider, name=name, grid=(T // tm, nh),
        in_specs=[pl.BlockSpec((tm, D), row), pl.BlockSpec((tm, D), row), pl.BlockSpec((1, D), one),
                  pl.BlockSpec((1, tm, FB), blk), pl.BlockSpec((1, tm, FB), blk),
                  pl.BlockSpec((1, D, FB), lambda i, j: (j, 0, 0)),
                  pl.BlockSpec((1, D, FB), lambda i, j: (j + nh, 0, 0)),
                  pl.BlockSpec((FB, D), lambda i, j: (j, 0))],
        out_specs=[pl.BlockSpec((tm, D), row), pl.BlockSpec((1, D), one),
                   pl.BlockSpec((1, tm, FB), blk), pl.BlockSpec((1, tm, FB), blk),
                   pl.BlockSpec((tm, D), row)],
        out_shape=[jax.ShapeDtypeStruct((T, D), F32), jax.ShapeDtypeStruct((1, D), F32),
                   jax.ShapeDtypeStruct((nh, T, FB), BF16), jax.ShapeDtypeStruct((nh, T, FB), BF16),
                   jax.ShapeDtypeStruct((T, D), BF16)],
        scratch_shapes=[pltpu.VMEM((tm, D), F32), pltpu.VMEM((tm, D), BF16)],
        compiler_params=_params("arbitrary", "arbitrary"),
    )(dh, h_in, gain, G, U, w_in, w_in, w_out)


def _matmul_tn(name, a, b, nj, a_block, a_map, b_block, b_map, out_shape, out_block, out_map):
    T = a.shape[-2]
    tt = a_block[-2]
    nt = T // tt
    kb, nbk = out_block[-2], out_block[-1]

    def body(a_ref, b_ref, o_ref, acc):
        t = pl.program_id(1)

        @pl.when(t == 0)
        def _():
            acc[...] = jnp.zeros_like(acc)

        av = a_ref[0] if len(a_block) == 3 else a_ref[...]
        bv = b_ref[0] if len(b_block) == 3 else b_ref[...]
        acc[...] += _dot_tn(av, bv)

        @pl.when(t == nt - 1)
        def _():
            if len(out_block) == 3:
                o_ref[0] = acc[...].astype(o_ref.dtype)
            else:
                o_ref[...] = acc[...].astype(o_ref.dtype)

    return pl.pallas_call(
        body, name=name, grid=(nj, nt),
        in_specs=[pl.BlockSpec(a_block, a_map), pl.BlockSpec(b_block, b_map)],
        out_specs=pl.BlockSpec(out_block, out_map),
        out_shape=jax.ShapeDtypeStruct(out_shape, BF16),
        scratch_shapes=[pltpu.VMEM((kb, nbk), F32)],
        compiler_params=_params("arbitrary", "arbitrary"),
    )(a, b)


def _dw_in(name, n, dG, dU, part, nparts, rider=None):
    T, D = n.shape
    nh, _, FB = dG.shape
    kr = D // nparts
    tt = min(GRAD_ROW_TILE, T)
    nt = T // tt

    def body(n_ref, dg_ref, du_ref, o_ref, acc):
        j = pl.program_id(0)
        t = pl.program_id(1)

        @pl.when(t == 0)
        def _():
            acc[...] = jnp.zeros_like(acc)

        @pl.when(j < nh)
        def _():
            acc[...] += _dot_tn(n_ref[...], dg_ref[0])

        @pl.when(j >= nh)
        def _():
            acc[...] += _dot_tn(n_ref[...], du_ref[0])

        @pl.when(t == nt - 1)
        def _():
            o_ref[0] = acc[...].astype(BF16)

    return _pallas(
        body, rider, name=name, grid=(2 * nh, nt),
        in_specs=[pl.BlockSpec((tt, kr), lambda j, t: (t, part)),
                  pl.BlockSpec((1, tt, FB), lambda j, t: (jnp.minimum(j, nh - 1), t, 0)),
                  pl.BlockSpec((1, tt, FB), lambda j, t: (jnp.maximum(j - nh, 0), t, 0))],
        out_specs=[pl.BlockSpec((1, kr, FB), lambda j, t: (j, 0, 0))],
        out_shape=[jax.ShapeDtypeStruct((2 * nh, kr, FB), BF16)],
        scratch_shapes=[pltpu.VMEM((kr, FB), F32)],
        compiler_params=_params("arbitrary", "arbitrary"),
    )(n, dG, dU)


def _mix_in_fwd(h, gain, w):
    T, D = h.shape
    W = w.shape[1]
    nuv = 2 * GM_WIDTH
    tm = min(ROW_TILE, T)

    def body(h_ref, g_ref, w_ref, n_ref, zuv_ref, qkv_ref):
        _, xh = _rms_parts(h_ref[...])
        n = (xh * g_ref[...]).astype(BF16)
        n_ref[...] = n
        z = _dot(n, w_ref[...])
        zuv_ref[...] = z[:, :nuv]
        qkv_ref[...] = z[:, nuv:].astype(BF16)

    row = lambda i: (i, 0)
    return pl.pallas_call(
        body, name="mix_in_fwd", grid=(T // tm,),
        in_specs=[pl.BlockSpec((tm, D), row), pl.BlockSpec((1, D), lambda i: (0, 0)),
                  pl.BlockSpec((D, W), lambda i: (0, 0))],
        out_specs=[pl.BlockSpec((tm, D), row), pl.BlockSpec((tm, nuv), row),
                   pl.BlockSpec((tm, W - nuv), row)],
        out_shape=[jax.ShapeDtypeStruct((T, D), BF16), jax.ShapeDtypeStruct((T, nuv), F32),
                   jax.ShapeDtypeStruct((T, W - nuv), BF16)],
        compiler_params=_params("arbitrary"),
    )(h, gain, w)


def _mix_in_bwd(dzuv, dqkv, w, h, gain, dh):
    T, D = h.shape
    W = w.shape[1]
    nuv = dzuv.shape[1]
    tm = min(ROW_TILE, T)

    def body(dzuv_ref, dqkv_ref, w_ref, h_ref, g_ref, dh_ref, dhin_ref, dg_ref, half_ref):
        @pl.when(pl.program_id(0) == 0)
        def _():
            dg_ref[...] = jnp.zeros_like(dg_ref)

        dn = _dot_nt(dzuv_ref[...], w_ref[:, :nuv]) + _dot_nt(dqkv_ref[...], w_ref[:, nuv:])
        dx, dg = _rms_bwd(h_ref[...], g_ref[...], dn)
        dh_in = dh_ref[...] + dx
        dhin_ref[...] = dh_in
        half_ref[...] = (0.5 * dh_in).astype(BF16)
        dg_ref[...] += dg

    row = lambda i: (i, 0)
    one = lambda i: (0, 0)
    return pl.pallas_call(
        body, name="mix_in_bwd", grid=(T // tm,),
        in_specs=[pl.BlockSpec((tm, nuv), row), pl.BlockSpec((tm, W - nuv), row),
                  pl.BlockSpec((D, W), one), pl.BlockSpec((tm, D), row), pl.BlockSpec((1, D), one),
                  pl.BlockSpec((tm, D), row)],
        out_specs=[pl.BlockSpec((tm, D), row), pl.BlockSpec((1, D), one), pl.BlockSpec((tm, D), row)],
        out_shape=[jax.ShapeDtypeStruct((T, D), F32), jax.ShapeDtypeStruct((1, D), F32),
                   jax.ShapeDtypeStruct((T, D), BF16)],
        compiler_params=_params("arbitrary"),
    )(dzuv, dqkv, w, h, gain, dh)


def _gmlp_norm(zv, gv):
    v = _gelu(zv)
    r, vh = _rms_parts(v)
    return r, vh, (vh * gv).astype(BF16)


def _causal_ws(ws_ref, hd):
    r = lax.broadcasted_iota(jnp.int32, (CHUNK, CHUNK), 0)
    c = lax.broadcasted_iota(jnp.int32, (CHUNK, CHUNK), 1)
    return jnp.where(r >= c, ws_ref[hd], 0.0).astype(BF16)


def _gmlp_fwd(zuv, gv, ws, b_t):
    T = zuv.shape[0]
    tg = min(ROW_TILE, T)

    def body(zu_ref, zv_ref, gv_ref, ws_ref, bt_ref, o_ref):
        u = _gelu(zu_ref[...])
        _, _, vn = _gmlp_norm(zv_ref[...], gv_ref[...])
        for hd in range(GM_HEADS):
            wc = _causal_ws(ws_ref, hd)
            cols = slice(hd * CHUNK, (hd + 1) * CHUNK)
            for c in range(tg // CHUNK):
                rows = slice(c * CHUNK, (c + 1) * CHUNK)
                sv = _dot(wc, vn[rows, cols]) + bt_ref[:, hd:hd + 1]
                o_ref[rows, cols] = (u[rows, cols] * sv).astype(BF16)

    return pl.pallas_call(
        body, name="gmlp_fwd", grid=(T // tg,),
        in_specs=[pl.BlockSpec((tg, GM_WIDTH), lambda i: (i, 0)), pl.BlockSpec((tg, GM_WIDTH), lambda i: (i, 1)),
                  pl.BlockSpec((1, GM_WIDTH), lambda i: (0, 0)),
                  pl.BlockSpec((GM_HEADS, CHUNK, CHUNK), lambda i: (0, 0, 0)),
                  pl.BlockSpec((CHUNK, GM_HEADS), lambda i: (0, 0))],
        out_specs=pl.BlockSpec((tg, GM_WIDTH), lambda i: (i, 0)),
        out_shape=jax.ShapeDtypeStruct((T, GM_WIDTH), BF16),
        compiler_params=_params("arbitrary"),
    )(zuv, zuv, gv, ws, b_t)


def _gmlp_bwd(zuv, d_gm, gv, ws, b_t):
    T = zuv.shape[0]
    tg = min(ROW_TILE, T)
    ng = T // tg

    def body(zu_ref, zv_ref, dgm_ref, gv_ref, ws_ref, bt_ref, dz_ref, dgv_ref, dws_ref, dbt_ref, dsv_acc, dvn_s):
        i = pl.program_id(0)

        @pl.when(i == 0)
        def _():
            dgv_ref[...] = jnp.zeros_like(dgv_ref)
            dws_ref[...] = jnp.zeros_like(dws_ref)
            dsv_acc[...] = jnp.zeros_like(dsv_acc)

        zu = zu_ref[...]
        zv = zv_ref[...]
        dgm = dgm_ref[...]
        gvv = gv_ref[...]
        u = _gelu(zu)
        rv, vh, vn = _gmlp_norm(zv, gvv)
        dsv = dgm * u
        dsv_b = dsv.astype(BF16)
        for hd in range(GM_HEADS):
            wc = _causal_ws(ws_ref, hd)
            cols = slice(hd * CHUNK, (hd + 1) * CHUNK)
            dws = jnp.zeros((CHUNK, CHUNK), F32)
            dsv_sum = jnp.zeros((CHUNK, CHUNK), F32)
            for c in range(tg // CHUNK):
                rows = slice(c * CHUNK, (c + 1) * CHUNK)
                vch = vn[rows, cols]
                sv = _dot(wc, vch) + bt_ref[:, hd:hd + 1]
                dz_ref[rows, cols] = (dgm[rows, cols] * sv * _gelu_grad(zu[rows, cols])).astype(BF16)
                dws += _dot_nt(dsv_b[rows, cols], vch)
                dsv_sum += dsv[rows, cols]
                dvn_s[rows, cols] = _dot_tn(wc, dsv_b[rows, cols])
            dws_ref[hd] += dws
            dsv_acc[:, cols] += dsv_sum
        dvn = dvn_s[...]
        dvh = dvn * gvv
        dv = rv * (dvh - vh * jnp.mean(dvh * vh, axis=-1, keepdims=True))
        dgv_ref[...] += jnp.sum(dvn * vh, axis=0, keepdims=True)
        dz_ref[:, GM_WIDTH:] = (dv * _gelu_grad(zv)).astype(BF16)

        @pl.when(i == ng - 1)
        def _():
            r = lax.broadcasted_iota(jnp.int32, (CHUNK, CHUNK), 0)
            c = lax.broadcasted_iota(jnp.int32, (CHUNK, CHUNK), 1)
            for hd in range(GM_HEADS):
                dws_ref[hd] = jnp.where(r >= c, dws_ref[hd], 0.0)
                dbt_ref[:, hd:hd + 1] = jnp.sum(dsv_acc[:, hd * CHUNK:(hd + 1) * CHUNK], axis=1, keepdims=True)

    return pl.pallas_call(
        body, name="gmlp_bwd", grid=(ng,),
        in_specs=[pl.BlockSpec((tg, GM_WIDTH), lambda i: (i, 0)), pl.BlockSpec((tg, GM_WIDTH), lambda i: (i, 1)),
                  pl.BlockSpec((tg, GM_WIDTH), lambda i: (i, 0)),
                  pl.BlockSpec((1, GM_WIDTH), lambda i: (0, 0)),
                  pl.BlockSpec((GM_HEADS, CHUNK, CHUNK), lambda i: (0, 0, 0)),
                  pl.BlockSpec((CHUNK, GM_HEADS), lambda i: (0, 0))],
        out_specs=[pl.BlockSpec((tg, 2 * GM_WIDTH), lambda i: (i, 0)),
                   pl.BlockSpec((1, GM_WIDTH), lambda i: (0, 0)),
                   pl.BlockSpec((GM_HEADS, CHUNK, CHUNK), lambda i: (0, 0, 0)),
                   pl.BlockSpec((CHUNK, GM_HEADS), lambda i: (0, 0))],
        out_shape=[jax.ShapeDtypeStruct((T, 2 * GM_WIDTH), BF16), jax.ShapeDtypeStruct((1, GM_WIDTH), F32),
                   jax.ShapeDtypeStruct((GM_HEADS, CHUNK, CHUNK), F32),
                   jax.ShapeDtypeStruct((CHUNK, GM_HEADS), F32)],
        scratch_shapes=[pltpu.VMEM((CHUNK, GM_WIDTH), F32), pltpu.VMEM((tg, GM_WIDTH), F32)],
        compiler_params=_params("arbitrary"),
    )(zuv, zuv, d_gm, gv, ws, b_t)


def _scan_matrix(blk, keep):
    r = lax.broadcasted_iota(jnp.int32, (blk, blk), 0)
    c = lax.broadcasted_iota(jnp.int32, (blk, blk), 1)
    return jnp.where(keep(r, c), 1.0, 0.0).astype(BF16)


def _scan_matrix2(blk, keep, value):
    m = _scan_matrix(blk, keep) * value
    return jnp.concatenate([m, m], axis=0)


def _scan(x, mat2):
    hi, lo = _split_bf16(x)
    return _dot(jnp.concatenate([hi, lo], axis=1), mat2)


def _head_masks(q):
    lane = lax.broadcasted_iota(jnp.int32, q.shape, 1)
    m0 = lane < SB_HEAD_DIM
    zero = jnp.zeros_like(q)
    return m0, jnp.where(m0, q, zero), jnp.where(m0, zero, q)


_LOG2E = 1.4426950408889634


def _softplus_parts(z):
    e = jnp.exp2(jnp.abs(z) * (-_LOG2E))
    ope = 1.0 + e
    return e, ope, jnp.maximum(z, 0.0) + jnp.log(ope)


def _attn_fwd(qkv, rider=None):
    T = qkv.shape[0]
    tk = ATTN_KEY_BLOCK
    tq = min(ATTN_Q_ROWS, T)
    band = tq // tk
    assert band % ATTN_UNROLL == 0 or T == tq
    ngrp = SB_WIDTH // LANES

    def body(q_ref, k_ref, v_ref, o_ref, l_ref, acc, run):
        i = pl.program_id(1)
        suffix = _scan_matrix2(tk, lambda r, c: r >= c, -1.0)
        row = lax.broadcasted_iota(jnp.int32, (tq, tk), 0)
        col = lax.broadcasted_iota(jnp.int32, (tq, tk), 1)
        m0, q0, q1 = _head_masks(q_ref[...] * SB_SCALE)
        acc[...] = jnp.zeros_like(acc)
        run[...] = jnp.zeros_like(run)

        def tile(j, causal):
            start = pl.multiple_of(j * tk, tk)
            kj = k_ref[pl.ds(start, tk), :]
            vj = v_ref[pl.ds(start, tk), :]
            for hd, qh in enumerate((q0, q1)):
                z = _dot_nt(qh, kj)
                _, _, sp = _softplus_parts(z)
                if causal is not None:
                    sp = jnp.where(causal, sp, 0.0)
                res = _scan(sp, suffix)
                a = jnp.exp(z + (run[hd] + res))
                if causal is not None:
                    a = jnp.where(causal, a, 0.0)
                acc[hd] += _dot(a.astype(BF16), vj)
                run[hd] += res[:, 0:1]

        for jb in reversed(range(band)):
            tile(i * band + jb, jb * tk + col < row)

        def full_step(it, carry):
            for u in range(ATTN_UNROLL):
                tile(i * band - 1 - ATTN_UNROLL * it - u, None)
            return carry

        lax.fori_loop(0, i * (band // ATTN_UNROLL), full_step, 0)
        o_ref[...] = jnp.where(m0, acc[0], acc[1]).astype(BF16)
        l_ref[...] = jnp.where(m0, jnp.broadcast_to(run[0], (tq, LANES)), jnp.broadcast_to(run[1], (tq, LANES)))

    return _pallas(
        body, rider, name="attn_fwd", grid=(ngrp, T // tq),
        in_specs=[pl.BlockSpec((tq, LANES), lambda g, i: (i, g)),
                  pl.BlockSpec((T, LANES), lambda g, i: (0, ngrp + g)),
                  pl.BlockSpec((T, LANES), lambda g, i: (0, 2 * ngrp + g))],
        out_specs=[pl.BlockSpec((tq, LANES), lambda g, i: (i, g)),
                   pl.BlockSpec((tq, LANES), lambda g, i: (i, g))],
        out_shape=[jax.ShapeDtypeStruct((T, SB_WIDTH), BF16), jax.ShapeDtypeStruct((T, SB_WIDTH), F32)],
        scratch_shapes=[pltpu.VMEM((2, tq, LANES), F32), pltpu.VMEM((2, tq, 1), F32)],
        compiler_params=_params("arbitrary", "arbitrary"),
    )(qkv, qkv, qkv)


def _attn_bwd(qkv, d_o, ltot, rider=None):
    T = qkv.shape[0]
    tk = ATTN_KEY_BLOCK
    tq = min(ATTN_Q_ROWS, T)
    band = tq // tk
    assert band % ATTN_UNROLL == 0 or T == tq
    nq = T // tq
    ngrp = SB_WIDTH // LANES

    def body(q_ref, k_ref, v_ref, do_ref, l_ref, dq_ref, dk_ref, dv_ref, dq_acc, dk_acc, dv_acc, lpre, ppre):
        i = pl.program_id(1)

        @pl.when(i == 0)
        def _():
            dk_acc[...] = jnp.zeros_like(dk_acc)
            dv_acc[...] = jnp.zeros_like(dv_acc)

        excl = _scan_matrix(tk, lambda r, c: r < c)
        excl2 = jnp.concatenate([excl, excl], axis=0)
        row = lax.broadcasted_iota(jnp.int32, (tq, tk), 0)
        col = lax.broadcasted_iota(jnp.int32, (tq, tk), 1)
        m0, q0, q1 = _head_masks(q_ref[...] * SB_SCALE)
        _, d0, d1 = _head_masks(do_ref[...].astype(BF16))
        lt = l_ref[...]
        ltots = (lt[:, 0:1], lt[:, SB_HEAD_DIM:SB_HEAD_DIM + 1])
        dq_acc[...] = jnp.zeros_like(dq_acc)
        lpre[...] = jnp.zeros_like(lpre)
        ppre[...] = jnp.zeros_like(ppre)

        def tile(j, causal):
            start = pl.multiple_of(j * tk, tk)
            kj = k_ref[pl.ds(start, tk), :]
            vj = v_ref[pl.ds(start, tk), :]
            for hd, (qh, dh) in enumerate(((q0, d0), (q1, d1))):
                z = _dot_nt(qh, kj)
                e, ope, sp = _softplus_parts(z)
                rinv = 1.0 / ope
                small = e * rinv
                pos = z >= 0.0
                beta = jnp.where(pos, rinv, small)
                one_m_beta = jnp.where(pos, small, rinv)
                if causal is not None:
                    sp = jnp.where(causal, sp, 0.0)
                res = _scan(sp, excl2)
                a = jnp.exp(z + ((ltots[hd] + lpre[hd]) + res))
                if causal is not None:
                    a = jnp.where(causal, a, 0.0)
                p = a * _dot_nt(dh, vj)
                resp = _dot(p.astype(BF16), excl)
                dz = p * one_m_beta - beta * (ppre[hd] + resp)
                if causal is not None:
                    dz = jnp.where(causal, dz, 0.0)
                dzb = dz.astype(BF16)
                dq_acc[hd] += _dot(dzb, kj)
                dk_acc[pl.ds(start, tk), :] += _dot_tn(dzb, qh)
                dv_acc[pl.ds(start, tk), :] += _dot_tn(a.astype(BF16), dh)
                lpre[hd] += res[:, tk - 1:tk] + sp[:, tk - 1:tk]
                ppre[hd] += resp[:, tk - 1:tk] + p[:, tk - 1:tk]

        def full_step(it, carry):
            for u in range(ATTN_UNROLL):
                tile(ATTN_UNROLL * it + u, None)
            return carry

        lax.fori_loop(0, i * (band // ATTN_UNROLL), full_step, 0)
        for jb in range(band):
            tile(i * band + jb, jb * tk + col < row)
        dq_ref[...] = (jnp.where(m0, dq_acc[0], dq_acc[1]) * SB_SCALE).astype(BF16)

        @pl.when(i == nq - 1)
        def _():
            dk_ref[...] = dk_acc[...].astype(BF16)
            dv_ref[...] = dv_acc[...].astype(BF16)

    qmap = lambda g, i: (i, g)
    return _pallas(
        body, rider, name="attn_bwd", grid=(ngrp, nq),
        in_specs=[pl.BlockSpec((tq, LANES), qmap),
                  pl.BlockSpec((T, LANES), lambda g, i: (0, ngrp + g)),
                  pl.BlockSpec((T, LANES), lambda g, i: (0, 2 * ngrp + g)),
                  pl.BlockSpec((tq, LANES), qmap), pl.BlockSpec((tq, LANES), qmap)],
        out_specs=[pl.BlockSpec((tq, LANES), qmap),
                   pl.BlockSpec((T, LANES), lambda g, i: (0, g)),
                   pl.BlockSpec((T, LANES), lambda g, i: (0, g))],
        out_shape=[jax.ShapeDtypeStruct((T, SB_WIDTH), BF16)] * 3,
        scratch_shapes=[pltpu.VMEM((2, tq, LANES), F32), pltpu.VMEM((T, LANES), F32),
                        pltpu.VMEM((T, LANES), F32), pltpu.VMEM((2, tq, 1), F32),
                        pltpu.VMEM((2, tq, 1), F32)],
        compiler_params=_params("arbitrary", "arbitrary"),
    )(qkv, qkv, qkv, d_o, ltot)


def _mix_out_fwd(h, gm, sb, w):
    T, D = h.shape
    tm = min(ROW_TILE, T)

    def body(h_ref, gm_ref, sb_ref, w_ref, o_ref):
        o_ref[...] = h_ref[...] + _dot(gm_ref[...], w_ref[:GM_WIDTH, :]) + _dot(sb_ref[...], w_ref[GM_WIDTH:, :])

    row = lambda i: (i, 0)
    return pl.pallas_call(
        body, name="mix_out_fwd", grid=(T // tm,),
        in_specs=[pl.BlockSpec((tm, D), row), pl.BlockSpec((tm, GM_WIDTH), row), pl.BlockSpec((tm, SB_WIDTH), row),
                  pl.BlockSpec((GM_WIDTH + SB_WIDTH, D), lambda i: (0, 0))],
        out_specs=pl.BlockSpec((tm, D), row),
        out_shape=jax.ShapeDtypeStruct((T, D), F32),
        compiler_params=_params("arbitrary"),
    )(h, gm, sb, w)


def _mix_out_bwd(dh, w):
    T, D = dh.shape
    tm = min(ROW_TILE, T)

    def body(dh_ref, w_ref, dgm_ref, dsb_ref, dhb_ref):
        dhb = dh_ref[...].astype(BF16)
        dhb_ref[...] = dhb
        dgm_ref[...] = _dot_nt(dhb, w_ref[:GM_WIDTH, :])
        dsb_ref[...] = _dot_nt(dhb, w_ref[GM_WIDTH:, :])

    row = lambda i: (i, 0)
    return pl.pallas_call(
        body, name="mix_out_bwd", grid=(T // tm,),
        in_specs=[pl.BlockSpec((tm, D), row), pl.BlockSpec((GM_WIDTH + SB_WIDTH, D), lambda i: (0, 0))],
        out_specs=[pl.BlockSpec((tm, GM_WIDTH), row), pl.BlockSpec((tm, SB_WIDTH), row), pl.BlockSpec((tm, D), row)],
        out_shape=[jax.ShapeDtypeStruct((T, GM_WIDTH), F32), jax.ShapeDtypeStruct((T, SB_WIDTH), F32),
                   jax.ShapeDtypeStruct((T, D), BF16)],
        compiler_params=_params("arbitrary"),
    )(dh, w)


def _tail(h3, p, target, g_ple, g_fin, w_gate, w_proj):
    T, D = h3.shape
    PD = p.shape[1]
    tm = min(ROW_TILE, T)

    def body(h_ref, p_ref, t_ref, gp_ref, gf_ref, wg_ref, wp_ref,
             loss_ref, dh_ref, n4_ref, dgl_ref, dpp_ref, dgp_ref, dgf_ref):
        @pl.when(pl.program_id(0) == 0)
        def _():
            loss_ref[...] = jnp.zeros_like(loss_ref)
            dgp_ref[...] = jnp.zeros_like(dgp_ref)
            dgf_ref[...] = jnp.zeros_like(dgf_ref)

        h3v = h_ref[...]
        gp = gp_ref[...]
        gf = gf_ref[...]
        r3, xh3 = _rms_parts(h3v)
        n4 = (xh3 * gp).astype(BF16)
        n4_ref[...] = n4
        gate = _sigmoid(_dot(n4, wg_ref[...]))
        pp = _dot(p_ref[...], wp_ref[...])
        h4 = h3v + gate * pp
        r4, xh4 = _rms_parts(h4)
        err = xh4 * gf - t_ref[...]
        loss_ref[...] += jnp.full(loss_ref.shape, (0.5 / D) * jnp.sum(err * err), F32)
        dy = err * (1.0 / D)
        dgf_ref[...] += jnp.sum(dy * xh4, axis=0, keepdims=True)
        dyg = dy * gf
        dh4 = r4 * (dyg - xh4 * jnp.mean(dyg * xh4, axis=-1, keepdims=True))
        dpp_ref[...] = (dh4 * gate).astype(BF16)
        dgl = (dh4 * pp * gate * (1.0 - gate)).astype(BF16)
        dgl_ref[...] = dgl
        dn4 = _dot_nt(dgl, wg_ref[...])
        dgp_ref[...] += jnp.sum(dn4 * xh3, axis=0, keepdims=True)
        dn4g = dn4 * gp
        dh_ref[...] = dh4 + r3 * (dn4g - xh3 * jnp.mean(dn4g * xh3, axis=-1, keepdims=True))

    row = lambda i: (i, 0)
    one = lambda i: (0, 0)
    return pl.pallas_call(
        body, name="tail", grid=(T // tm,),
        in_specs=[pl.BlockSpec((tm, D), row), pl.BlockSpec((tm, PD), row), pl.BlockSpec((tm, D), row),
                  pl.BlockSpec((1, D), one), pl.BlockSpec((1, D), one),
                  pl.BlockSpec((D, D), one), pl.BlockSpec((PD, D), one)],
        out_specs=[pl.BlockSpec((1, LANES), one), pl.BlockSpec((tm, D), row), pl.BlockSpec((tm, D), row),
                   pl.BlockSpec((tm, D), row), pl.BlockSpec((tm, D), row),
                   pl.BlockSpec((1, D), one), pl.BlockSpec((1, D), one)],
        out_shape=[jax.ShapeDtypeStruct((1, LANES), F32), jax.ShapeDtypeStruct((T, D), F32),
                   jax.ShapeDtypeStruct((T, D), BF16), jax.ShapeDtypeStruct((T, D), BF16),
                   jax.ShapeDtypeStruct((T, D), BF16),
                   jax.ShapeDtypeStruct((1, D), F32), jax.ShapeDtypeStruct((1, D), F32)],
        compiler_params=_params("arbitrary"),
    )(h3, p, target, g_ple, g_fin, w_gate, w_proj)


FFN1_W = ("ffn1_w_in", "ffn1_w_out")
MIX_W = ("w_mix_in", "w_mix_out")
REST_W = ("ffn2_w_in", "ffn2_w_out", "ple_w_gate", "ple_w_proj")
BIG_W = FFN1_W + MIX_W + REST_W
COLUMN_SHARDED = ("w_mix_in", "ple_w_proj")


class _Traffic:
    def __init__(self, shards):
        self.shards = shards
        self.parts = {}

    @staticmethod
    def _full(name, gathered):
        if name in COLUMN_SHARDED:
            return jnp.transpose(gathered, (1, 0, 2)).reshape(gathered.shape[1], -1)
        if name.endswith("_w_in"):
            return gathered
        return gathered.reshape(-1, gathered.shape[-1])

    @staticmethod
    def _blocks(name, grad):
        name = name.split("/")[0]
        if name in COLUMN_SHARDED:
            return jnp.transpose(grad.reshape(grad.shape[0], N_DEV, -1), (1, 0, 2))
        if name.endswith("_w_in"):
            return grad
        return grad.reshape(N_DEV, -1, grad.shape[-1])

    def gather_now(self, names):
        got = _exchange("gather_" + names[0], [self.shards[n] for n in names], [False] * len(names))
        return self.gathered(names, got)

    def gather_rider(self, names):
        return [self.shards[n] for n in names], [False] * len(names)

    def gathered(self, names, got):
        return {n: self._full(n, g) for n, g in zip(names, got)}

    def scatter_rider(self, grads):
        return [self._blocks(n, g) for n, g in grads.items()], [True] * len(grads)

    def scattered(self, names, got):
        self.parts.update(zip(names, got))

    def finish(self, grads, small):
        arrays, flags = self.scatter_rider(grads)
        got = _exchange("scatter_last", arrays + [small], flags + [False])
        self.scattered(list(grads), got[:-1])
        return got[-1]


def _local_step(traffic, x, p_bf, target, g1, gmix, gv, ws, b_t, g2, gple, gfin, pack_small):
    T, D = x.shape
    tm = min(ROW_TILE, T)

    w = traffic.gather_now(FFN1_W)
    h1, n1, G1, U1, a1, *got = _ffn_fwd("ffn1_fwd", x, g1, w["ffn1_w_in"], w["ffn1_w_out"],
                                        rider=traffic.gather_rider(MIX_W))
    w.update(traffic.gathered(MIX_W, got))
    n2, zuv, qkv = _mix_in_fwd(h1, gmix, w["w_mix_in"])
    gm = _gmlp_fwd(zuv, gv, ws, b_t)
    sb, ltot, *got = _attn_fwd(qkv, rider=traffic.gather_rider(REST_W))
    w.update(traffic.gathered(REST_W, got))
    h2 = _mix_out_fwd(h1, gm, sb, w["w_mix_out"])
    h3, n3, G2, U2, a2 = _ffn_fwd("ffn2_fwd", h2, g2, w["ffn2_w_in"], w["ffn2_w_out"])
    loss, dh3, n4, d_gl, d_pp, dg_ple, dg_fin = _tail(h3, p_bf, target, gple, gfin, w["ple_w_gate"], w["ple_w_proj"])

    nb, _, FB = w["ffn1_w_in"].shape
    nh = nb // 2

    tt = min(GRAD_ROW_TILE, T)

    def dw_out(name, a, d_out):
        return _matmul_tn(name, a, d_out, nh, (1, tt, FB), lambda j, t: (j, t, 0), (tt, D), lambda j, t: (t, 0),
                          (nh, FB, D), (1, FB, D), lambda j, t: (j, 0, 0))

    def dense_tn(name, a, b, ncol):
        ka, nbw = a.shape[1], b.shape[1] // ncol
        return _matmul_tn(name, a, b, ncol, (tt, ka), lambda j, t: (t, 0), (tt, nbw), lambda j, t: (t, j),
                          (ka, b.shape[1]), (ka, nbw), lambda j, t: (0, j))

    grads = dict(ple_w_gate=dense_tn("dw_ple_gate", n4, d_gl, 2), ple_w_proj=dense_tn("dw_ple_proj", p_bf, d_pp, 1))
    dh2, dg2, dG2, dU2, dout2 = _ffn_bwd("ffn2_bwd", dh3, h2, g2, G2, U2, w["ffn2_w_in"], w["ffn2_w_out"])
    grads["ffn2_w_in"], = _dw_in("ffn2_dw_in", n3, dG2, dU2, 0, 1)
    grads["ffn2_w_out"] = dw_out("ffn2_dw_out", a2, dout2)
    grads = {n: grads[n] for n in REST_W}

    d_gm, d_sb, dh2_bf = _mix_out_bwd(dh2, w["w_mix_out"])
    dw_mo = jnp.concatenate([dense_tn("dw_mix_out_gm", gm, dh2_bf, 1), dense_tn("dw_mix_out_sb", sb, dh2_bf, 1)], axis=0)
    dzuv, dgv, dws, db_t = _gmlp_bwd(zuv, d_gm, gv, ws, b_t)
    dq, dk, dv, *got = _attn_bwd(qkv, d_sb, ltot, rider=traffic.scatter_rider(grads))
    traffic.scattered(REST_W, got)
    dqkv = jnp.concatenate([dq, dk, dv], axis=1)
    dw_mi = jnp.concatenate([dense_tn("dw_mix_in_uv", n2, dzuv, 2), dense_tn("dw_mix_in_qkv", n2, dqkv, 3)], axis=1)
    dh1, dgmix, dout1 = _mix_in_bwd(dzuv, dqkv, w["w_mix_in"], h1, gmix, dh2)

    grads = dict(w_mix_in=dw_mi, w_mix_out=dw_mo, ffn1_w_out=dw_out("ffn1_dw_out", a1, dout1))
    dx, dg1, dG1, dU1, _, *got = _ffn_bwd("ffn1_bwd", dh1, x, g1, G1, U1, w["ffn1_w_in"], w["ffn1_w_out"],
                                         rider=traffic.scatter_rider(grads))
    traffic.scattered(list(grads), got)
    top, = _dw_in("ffn1_dw_in_top", n1, dG1, dU1, 0, 2)
    bottom, *got = _dw_in("ffn1_dw_in_bottom", n1, dG1, dU1, 1, 2, rider=traffic.scatter_rider({"ffn1_w_in/0": top}))
    traffic.scattered(["ffn1_w_in/0"], got)

    small = pack_small(dict(ffn1_norm=dg1, mix_norm=dgmix, gmlp_v_norm=dgv, gmlp_w_s=dws, gmlp_b=jnp.transpose(db_t),
                            ffn2_norm=dg2, ple_norm=dg_ple, final_norm=dg_fin), loss)
    return dx, traffic.finish({"ffn1_w_in/1": bottom}, small)


def _peer(d):
    x, y, c = lax.axis_index("x"), lax.axis_index("y"), lax.axis_index("c")
    px = 1 - x if d & 4 else x
    py = 1 - y if d & 2 else y
    pc = 1 - c if d & 1 else c
    return (px, py, pc), 4 * px + 2 * py + pc


def _exchange_copies(ins, outs, send, recv, local, scatter):
    _, me = _peer(0)
    copies = []
    for t in range(len(ins)):
        src = ins[t].at[me] if scatter[t] else ins[t]
        copies.append(pltpu.make_async_copy(src, outs[t].at[me], local.at[t]))
    for d in range(1, N_DEV):
        peer, pidx = _peer(d)
        for t in range(len(ins)):
            src = ins[t].at[pidx] if scatter[t] else ins[t]
            copies.append(pltpu.make_async_remote_copy(
                src_ref=src, dst_ref=outs[t].at[me], send_sem=send.at[t, d - 1], recv_sem=recv.at[t, d - 1],
                device_id=peer, device_id_type=MESH))
    return copies


def _exchange_shapes(arrays, scatter):
    return [jax.ShapeDtypeStruct(a.shape if sc else (N_DEV,) + a.shape, a.dtype) for a, sc in zip(arrays, scatter)]


def _exchange_sems(n):
    return [pltpu.SemaphoreType.DMA((n, N_DEV - 1)), pltpu.SemaphoreType.DMA((n, N_DEV - 1)),
            pltpu.SemaphoreType.DMA((n,))]


_ANY = pl.BlockSpec(memory_space=pl.ANY)


def _exchange(name, arrays, scatter):
    n = len(arrays)

    def body(*refs):
        copies = _exchange_copies(refs[:n], refs[n:2 * n], *refs[2 * n:], scatter)
        for cp in copies:
            cp.start()
        for cp in copies:
            cp.wait()

    return pl.pallas_call(
        body, name=name, in_specs=[_ANY] * n, out_specs=[_ANY] * n, out_shape=_exchange_shapes(arrays, scatter),
        scratch_shapes=_exchange_sems(n),
    )(*arrays)


def _pallas(body, rider, *, name, grid, in_specs, out_specs, out_shape, scratch_shapes=(), compiler_params=None):
    if rider is None:
        return pl.pallas_call(body, name=name, grid=grid, in_specs=in_specs, out_specs=out_specs, out_shape=out_shape,
                              scratch_shapes=list(scratch_shapes), compiler_params=compiler_params)
    arrays, scatter = rider
    n, ni, no, ns = len(arrays), len(in_specs), len(out_specs), len(scratch_shapes)

    def carried(*refs):
        ins, r_in = refs[:ni], refs[ni:ni + n]
        outs, r_out = refs[ni + n:ni + n + no], refs[ni + n + no:ni + 2 * n + no]
        scratch, sems = refs[ni + 2 * n + no:ni + 2 * n + no + ns], refs[ni + 2 * n + no + ns:]
        ids = [pl.program_id(ax) for ax in range(len(grid))]
        first = functools.reduce(jnp.logical_and, [i == 0 for i in ids])
        last = functools.reduce(jnp.logical_and, [i == g - 1 for i, g in zip(ids, grid)])

        @pl.when(first)
        def _():
            for cp in _exchange_copies(r_in, r_out, *sems, scatter):
                cp.start()

        body(*ins, *outs, *scratch)

        @pl.when(last)
        def _():
            for cp in _exchange_copies(r_in, r_out, *sems, scatter):
                cp.wait()

    call = pl.pallas_call(
        carried, name=name, grid=grid, in_specs=list(in_specs) + [_ANY] * n, out_specs=list(out_specs) + [_ANY] * n,
        out_shape=list(out_shape) + _exchange_shapes(arrays, scatter),
        scratch_shapes=list(scratch_shapes) + _exchange_sems(n), compiler_params=compiler_params)
    return lambda *args: call(*args, *arrays)


def _adamw_math(g, w, m, v):
    m_new = ADAM_B1 * m + (1.0 - ADAM_B1) * g
    v_new = ADAM_B2 * v + (1.0 - ADAM_B2) * (g * g)
    m_hat = m_new / (1.0 - ADAM_B1 ** ADAM_STEP)
    v_hat = v_new / (1.0 - ADAM_B2 ** ADAM_STEP)
    delta = -ADAM_LR * (m_hat / (jnp.sqrt(v_hat) + ADAM_EPS) + ADAM_WD * w)
    return delta, m_new, v_new


def _adamw(name, parts, w, m, v):
    R, C = w.shape
    tr = R
    for cand in (256, 128, 64, 32, 16, 8):
        if R % cand == 0:
            tr = cand
            break

    def body(p_ref, w_ref, m_ref, v_ref, g_ref, d_ref, nm_ref, nv_ref):
        g = p_ref[0].astype(F32)
        for j in range(1, N_DEV):
            g = g + p_ref[j].astype(F32)
        g_ref[...] = g
        d_ref[...], nm_ref[...], nv_ref[...] = _adamw_math(g, w_ref[...], m_ref[...], v_ref[...])

    row = lambda i: (i, 0)
    spec = pl.BlockSpec((tr, C), row)
    return pl.pallas_call(
        body, name=name, grid=(R // tr,),
        in_specs=[pl.BlockSpec((N_DEV, tr, C), lambda i: (0, i, 0)), spec, spec, spec],
        out_specs=[spec] * 4,
        out_shape=[jax.ShapeDtypeStruct((R, C), F32)] * 4,
        compiler_params=_params("arbitrary"),
    )(parts, w, m, v)


def _rows128(a):
    flat = a.reshape(-1, LANES)
    pad = (-flat.shape[0]) % SMALL_ROWS_ALIGN
    return jnp.pad(flat, ((0, pad), (0, 0))) if pad else flat


def _unrows(packed, like):
    n = like.size // LANES
    return packed[:n].reshape(like.shape)


def kernel(x, p, ffn1_norm, ffn1_w_in, ffn1_w_out, mix_norm, w_mix_in, gmlp_v_norm, gmlp_w_s, gmlp_b, w_mix_out, ffn2_norm, ffn2_w_in, ffn2_w_out, ple_norm, ple_w_gate, ple_w_proj, final_norm, loss_target, m_ffn1_norm, m_ffn1_w_in, m_ffn1_w_out, m_mix_norm, m_w_mix_in, m_gmlp_v_norm, m_gmlp_w_s, m_gmlp_b, m_w_mix_out, m_ffn2_norm, m_ffn2_w_in, m_ffn2_w_out, m_ple_norm, m_ple_w_gate, m_ple_w_proj, m_final_norm, v_ffn1_norm, v_ffn1_w_in, v_ffn1_w_out, v_mix_norm, v_w_mix_in, v_gmlp_v_norm, v_gmlp_w_s, v_gmlp_b, v_w_mix_out, v_ffn2_norm, v_ffn2_w_in, v_ffn2_w_out, v_ple_norm, v_ple_w_gate, v_ple_w_proj, v_final_norm):
    names = ["ffn1_norm", "ffn1_w_in", "ffn1_w_out", "mix_norm", "w_mix_in", "gmlp_v_norm", "gmlp_w_s", "gmlp_b",
             "w_mix_out", "ffn2_norm", "ffn2_w_in", "ffn2_w_out", "ple_norm", "ple_w_gate", "ple_w_proj", "final_norm"]
    W = dict(zip(names, [ffn1_norm, ffn1_w_in, ffn1_w_out, mix_norm, w_mix_in, gmlp_v_norm, gmlp_w_s, gmlp_b,
                         w_mix_out, ffn2_norm, ffn2_w_in, ffn2_w_out, ple_norm, ple_w_gate, ple_w_proj, final_norm]))
    M = dict(zip(names, [m_ffn1_norm, m_ffn1_w_in, m_ffn1_w_out, m_mix_norm, m_w_mix_in, m_gmlp_v_norm, m_gmlp_w_s,
                         m_gmlp_b, m_w_mix_out, m_ffn2_norm, m_ffn2_w_in, m_ffn2_w_out, m_ple_norm, m_ple_w_gate,
                         m_ple_w_proj, m_final_norm]))
    V = dict(zip(names, [v_ffn1_norm, v_ffn1_w_in, v_ffn1_w_out, v_mix_norm, v_w_mix_in, v_gmlp_v_norm, v_gmlp_w_s,
                         v_gmlp_b, v_w_mix_out, v_ffn2_norm, v_ffn2_w_in, v_ffn2_w_out, v_ple_norm, v_ple_w_gate,
                         v_ple_w_proj, v_final_norm]))
    small = [n for n in names if n not in BIG_W]
    D = x.shape[-1]

    def pack(src, last):
        return jnp.concatenate([_rows128(src[n]) for n in small] + [last], axis=0)

    offs = [0]
    for n in small:
        offs.append(offs[-1] + _rows128(W[n]).shape[0])

    traffic = _Traffic({n: W[n][0].astype(BF16) for n in BIG_W})
    dx, small_parts = _local_step(
        traffic, x[0], p[0, 0].astype(BF16), loss_target[0],
        W["ffn1_norm"], W["mix_norm"], W["gmlp_v_norm"], W["gmlp_w_s"][0], jnp.transpose(W["gmlp_b"][0]),
        W["ffn2_norm"], W["ple_norm"], W["final_norm"].reshape(1, D),
        lambda grads, loss_part: pack(grads, jnp.broadcast_to(loss_part, (SMALL_ROWS_ALIGN, LANES))))

    parts = traffic.parts
    parts["ffn1_w_in"] = jnp.concatenate([parts["ffn1_w_in/0"], parts["ffn1_w_in/1"]], axis=1)
    out = {}
    for n in BIG_W:
        out[n] = _adamw("adamw_" + n, parts[n], W[n][0], M[n][0], V[n][0])
    zeros = jnp.zeros((SMALL_ROWS_ALIGN, LANES), F32)
    sg, sd, sm, sv = _adamw("adamw_small", small_parts, pack(W, zeros), pack(M, zeros), pack(V, zeros))
    for k, n in enumerate(small):
        out[n] = tuple(_unrows(arr[offs[k]:offs[k + 1]], W[n]) for arr in (sg, sd, sm, sv))
    loss = sg[offs[len(small)], 0]

    res = [loss, dx[None]]
    for k in range(4):
        res += [out[n][k].reshape(W[n].shape) for n in names]
    return tuple(res)
```

```python
import functools

import jax
import jax.numpy as jnp
from jax import lax
from jax.experimental import pallas as pl
from jax.experimental.pallas import tpu as pltpu

F32 = jnp.float32
BF16 = jnp.bfloat16
MESH = pl.DeviceIdType.MESH

N_DEV = 8
EPS = 1e-6
ADAM_LR = 0.001
ADAM_B1 = 0.9
ADAM_B2 = 0.999
ADAM_EPS = 1e-08
ADAM_WD = 0.01
ADAM_STEP = 10

GM_WIDTH = 512
GM_HEADS = 4
CHUNK = 128
SB_WIDTH = 512
SB_HEAD_DIM = 64
SB_SCALE = 0.125
LANES = 128
SMALL_ROWS_ALIGN = 8

ROW_TILE = 512
GRAD_ROW_TILE = 2048
FFN_FWD_ROW_TILE = 1024
ATTN_Q_ROWS = 512
ATTN_BWD_Q_ROWS = 512
ATTN_KEY_BLOCK = 256
ATTN_UNROLL = 2
VMEM_LIMIT = 56 * 1024 * 1024


def _params(*sem):
    return pltpu.CompilerParams(dimension_semantics=sem, vmem_limit_bytes=VMEM_LIMIT)


def _dot(a, b):
    return jnp.dot(a, b, preferred_element_type=F32)


def _dot_nt(a, b):
    return lax.dot_general(a, b, (((1,), (1,)), ((), ())), preferred_element_type=F32)


def _dot_tn(a, b):
    return lax.dot_general(a, b, (((0,), (0,)), ((), ())), preferred_element_type=F32)


def _rms_parts(x):
    r = lax.rsqrt(jnp.mean(x * x, axis=-1, keepdims=True) + EPS)
    return r, x * r


def _rms_bwd(x, g, dy):
    r, xh = _rms_parts(x)
    dyg = dy * g
    dx = r * (dyg - xh * jnp.mean(dyg * xh, axis=-1, keepdims=True))
    return dx, jnp.sum(dy * xh, axis=0, keepdims=True)


def _sigmoid(x):
    return 1.0 / (1.0 + jnp.exp(-x))


_SQRT_HALF = 0.7071067811865476
_INV_SQRT_2PI = 0.3989422804014327


def _gelu(x):
    return 0.5 * x * (1.0 + lax.erf(x * _SQRT_HALF))


def _gelu_grad(x):
    return 0.5 * (1.0 + lax.erf(x * _SQRT_HALF)) + x * (_INV_SQRT_2PI * jnp.exp(-0.5 * x * x))


def _split_bf16(x):
    hi = x.astype(BF16)
    lo = (x - hi.astype(F32)).astype(BF16)
    return hi, lo


def _ffn_fwd(name, h, gain, w_in, w_out, rider=None):
    T, D = h.shape
    nb, _, FB = w_in.shape
    nh = nb // 2
    tm = min(FFN_FWD_ROW_TILE, T)

    def body(h_ref, g_ref, wg_ref, wu_ref, wo_ref, ho_ref, n_ref, G_ref, U_ref, a_ref, n_s, acc):
        jj = pl.program_id(1)

        @pl.when(jj == 0)
        def _():
            _, xh = _rms_parts(h_ref[...])
            n = (xh * g_ref[...]).astype(BF16)
            n_s[...] = n
            n_ref[...] = n
            acc[...] = jnp.zeros_like(acc)

        n = n_s[...]
        G = _dot(n, wg_ref[0])
        U = _dot(n, wu_ref[0])
        G_ref[0] = G.astype(BF16)
        U_ref[0] = U.astype(BF16)
        a = (G * _sigmoid(G) * U).astype(BF16)
        a_ref[0] = a
        acc[...] += _dot(a, wo_ref[...])

        @pl.when(jj == nh - 1)
        def _():
            ho_ref[...] = h_ref[...] + 0.5 * acc[...]

    row = lambda i, j: (i, 0)
    blk = lambda i, j: (j, i, 0)
    return _pallas(
        body, rider, name=name, grid=(T // tm, nh),
        in_specs=[pl.BlockSpec((tm, D), row),
                  pl.BlockSpec((1, D), lambda i, j: (0, 0)),
                  pl.BlockSpec((1, D, FB), lambda i, j: (j, 0, 0)),
                  pl.BlockSpec((1, D, FB), lambda i, j: (j + nh, 0, 0)),
                  pl.BlockSpec((FB, D), lambda i, j: (j, 0))],
        out_specs=[pl.BlockSpec((tm, D), row), pl.BlockSpec((tm, D), row),
                   pl.BlockSpec((1, tm, FB), blk), pl.BlockSpec((1, tm, FB), blk),
                   pl.BlockSpec((1, tm, FB), blk)],
        out_shape=[jax.ShapeDtypeStruct((T, D), F32), jax.ShapeDtypeStruct((T, D), BF16),
                   jax.ShapeDtypeStruct((nh, T, FB), BF16), jax.ShapeDtypeStruct((nh, T, FB), BF16),
                   jax.ShapeDtypeStruct((nh, T, FB), BF16)],
        scratch_shapes=[pltpu.VMEM((tm, D), BF16), pltpu.VMEM((tm, D), F32)],
        compiler_params=_params("arbitrary", "arbitrary"),
    )(h, gain, w_in, w_in, w_out)


def _ffn_bwd(name, dh, h_in, gain, G, U, w_in, w_out, rider=None):
    T, D = dh.shape
    nb, _, FB = w_in.shape
    nh = nb // 2
    tm = min(ROW_TILE, T)

    def body(dh_ref, h_ref, g_ref, G_ref, U_ref, wg_ref, wu_ref, wo_ref,
             dhin_ref, dg_ref, dG_ref, dU_ref, do_ref, dn_acc, do_s):
        i = pl.program_id(0)
        jj = pl.program_id(1)

        @pl.when(jj == 0)
        def _():
            d_out = (0.5 * dh_ref[...]).astype(BF16)
            do_s[...] = d_out
            do_ref[...] = d_out
            dn_acc[...] = jnp.zeros_like(dn_acc)

        @pl.when((i == 0) & (jj == 0))
        def _():
            dg_ref[...] = jnp.zeros_like(dg_ref)

        da = _dot_nt(do_s[...], wo_ref[...])
        Gv = G_ref[0].astype(F32)
        Uv = U_ref[0].astype(F32)
        sig = _sigmoid(Gv)
        dU = (da * (Gv * sig)).astype(BF16)
        dG = (da * Uv * (sig * (1.0 + Gv * (1.0 - sig)))).astype(BF16)
        dG_ref[0] = dG
        dU_ref[0] = dU
        dn_acc[...] += _dot_nt(dG, wg_ref[0]) + _dot_nt(dU, wu_ref[0])

        @pl.when(jj == nh - 1)
        def _():
            dx, dg = _rms_bwd(h_ref[...], g_ref[...], dn_acc[...])
            dhin_ref[...] = dh_ref[...] + dx
            dg_ref[...] += dg

    row = lambda i, j: (i, 0)
    blk = lambda i, j: (j, i, 0)
    one = lambda i, j: (0, 0)
    return _pallas(
        body, rider, name=name, grid=(T // tm, nh),
        in_specs=[pl.BlockSpec((tm, D), row), pl.BlockSpec((tm, D), row), pl.BlockSpec((1, D), one),
                  pl.BlockSpec((1, tm, FB), blk), pl.BlockSpec((1, tm, FB), blk),
                  pl.BlockSpec((1, D, FB), lambda i, j: (j, 0, 0)),
                  pl.BlockSpec((1, D, FB), lambda i, j: (j + nh, 0, 0)),
                  pl.BlockSpec((FB, D), lambda i, j: (j, 0))],
        out_specs=[pl.BlockSpec((tm, D), row), pl.BlockSpec((1, D), one),
                   pl.BlockSpec((1, tm, FB), blk), pl.BlockSpec((1, tm, FB), blk),
                   pl.BlockSpec((tm, D), row)],
        out_shape=[jax.ShapeDtypeStruct((T, D), F32), jax.ShapeDtypeStruct((1, D), F32),
                   jax.ShapeDtypeStruct((nh, T, FB), BF16), jax.ShapeDtypeStruct((nh, T, FB), BF16),
                   jax.ShapeDtypeStruct((T, D), BF16)],
        scratch_shapes=[pltpu.VMEM((tm, D), F32), pltpu.VMEM((tm, D), BF16)],
        compiler_params=_params("arbitrary", "arbitrary"),
    )(dh, h_in, gain, G, U, w_in, w_in, w_out)


def _matmul_tn(name, a, b, nj, a_block, a_map, b_block, b_map, out_shape, out_block, out_map):
    T = a.shape[-2]
    tt = a_block[-2]
    nt = T // tt
    kb, nbk = out_block[-2], out_block[-1]

    def body(a_ref, b_ref, o_ref, acc):
        t = pl.program_id(1)

        @pl.when(t == 0)
        def _():
            acc[...] = jnp.zeros_like(acc)

        av = a_ref[0] if len(a_block) == 3 else a_ref[...]
        bv = b_ref[0] if len(b_block) == 3 else b_ref[...]
        acc[...] += _dot_tn(av, bv)

        @pl.when(t == nt - 1)
        def _():
            if len(out_block) == 3:
                o_ref[0] = acc[...].astype(o_ref.dtype)
            else:
                o_ref[...] = acc[...].astype(o_ref.dtype)

    return pl.pallas_call(
        body, name=name, grid=(nj, nt),
        in_specs=[pl.BlockSpec(a_block, a_map), pl.BlockSpec(b_block, b_map)],
        out_specs=pl.BlockSpec(out_block, out_map),
        out_shape=jax.ShapeDtypeStruct(out_shape, BF16),
        scratch_shapes=[pltpu.VMEM((kb, nbk), F32)],
        compiler_params=_params("arbitrary", "arbitrary"),
    )(a, b)


def _dw_in(name, n, dG, dU, part, nparts, rider=None):
    T, D = n.shape
    nh, _, FB = dG.shape
    kr = D // nparts
    tt = min(GRAD_ROW_TILE, T)
    nt = T // tt

    def body(n_ref, dg_ref, du_ref, o_ref, acc):
        j = pl.program_id(0)
        t = pl.program_id(1)

        @pl.when(t == 0)
        def _():
            acc[...] = jnp.zeros_like(acc)

        @pl.when(j < nh)
        def _():
            acc[...] += _dot_tn(n_ref[...], dg_ref[0])

        @pl.when(j >= nh)
        def _():
            acc[...] += _dot_tn(n_ref[...], du_ref[0])

        @pl.when(t == nt - 1)
        def _():
            o_ref[0] = acc[...].astype(BF16)

    return _pallas(
        body, rider, name=name, grid=(2 * nh, nt),
        in_specs=[pl.BlockSpec((tt, kr), lambda j, t: (t, part)),
                  pl.BlockSpec((1, tt, FB), lambda j, t: (jnp.minimum(j, nh - 1), t, 0)),
                  pl.BlockSpec((1, tt, FB), lambda j, t: (jnp.maximum(j - nh, 0), t, 0))],
        out_specs=[pl.BlockSpec((1, kr, FB), lambda j, t: (j, 0, 0))],
        out_shape=[jax.ShapeDtypeStruct((2 * nh, kr, FB), BF16)],
        scratch_shapes=[pltpu.VMEM((kr, FB), F32)],
        compiler_params=_params("arbitrary", "arbitrary"),
    )(n, dG, dU)


def _mix_in_fwd(h, gain, w):
    T, D = h.shape
    W = w.shape[1]
    nuv = 2 * GM_WIDTH
    tm = min(ROW_TILE, T)

    def body(h_ref, g_ref, w_ref, n_ref, zuv_ref, qkv_ref):
        _, xh = _rms_parts(h_ref[...])
        n = (xh * g_ref[...]).astype(BF16)
        n_ref[...] = n
        z = _dot(n, w_ref[...])
        zuv_ref[...] = z[:, :nuv]
        qkv_ref[...] = z[:, nuv:].astype(BF16)

    row = lambda i: (i, 0)
    return pl.pallas_call(
        body, name="mix_in_fwd", grid=(T // tm,),
        in_specs=[pl.BlockSpec((tm, D), row), pl.BlockSpec((1, D), lambda i: (0, 0)),
                  pl.BlockSpec((D, W), lambda i: (0, 0))],
        out_specs=[pl.BlockSpec((tm, D), row), pl.BlockSpec((tm, nuv), row),
                   pl.BlockSpec((tm, W - nuv), row)],
        out_shape=[jax.ShapeDtypeStruct((T, D), BF16), jax.ShapeDtypeStruct((T, nuv), F32),
                   jax.ShapeDtypeStruct((T, W - nuv), BF16)],
        compiler_params=_params("arbitrary"),
    )(h, gain, w)


def _mix_in_bwd(dzuv, dqkv, w, h, gain, dh):
    T, D = h.shape
    W = w.shape[1]
    nuv = dzuv.shape[1]
    tm = min(ROW_TILE, T)

    def body(dzuv_ref, dqkv_ref, w_ref, h_ref, g_ref, dh_ref, dhin_ref, dg_ref, half_ref):
        @pl.when(pl.program_id(0) == 0)
        def _():
            dg_ref[...] = jnp.zeros_like(dg_ref)

        dn = _dot_nt(dzuv_ref[...], w_ref[:, :nuv]) + _dot_nt(dqkv_ref[...], w_ref[:, nuv:])
        dx, dg = _rms_bwd(h_ref[...], g_ref[...], dn)
        dh_in = dh_ref[...] + dx
        dhin_ref[...] = dh_in
        half_ref[...] = (0.5 * dh_in).astype(BF16)
        dg_ref[...] += dg

    row = lambda i: (i, 0)
    one = lambda i: (0, 0)
    return pl.pallas_call(
        body, name="mix_in_bwd", grid=(T // tm,),
        in_specs=[pl.BlockSpec((tm, nuv), row), pl.BlockSpec((tm, W - nuv), row),
                  pl.BlockSpec((D, W), one), pl.BlockSpec((tm, D), row), pl.BlockSpec((1, D), one),
                  pl.BlockSpec((tm, D), row)],
        out_specs=[pl.BlockSpec((tm, D), row), pl.BlockSpec((1, D), one), pl.BlockSpec((tm, D), row)],
        out_shape=[jax.ShapeDtypeStruct((T, D), F32), jax.ShapeDtypeStruct((1, D), F32),
                   jax.ShapeDtypeStruct((T, D), BF16)],
        compiler_params=_params("arbitrary"),
    )(dzuv, dqkv, w, h, gain, dh)


def _gmlp_norm(zv, gv):
    v = _gelu(zv)
    r, vh = _rms_parts(v)
    return r, vh, (vh * gv).astype(BF16)


def _causal_ws(ws_ref, hd):
    r = lax.broadcasted_iota(jnp.int32, (CHUNK, CHUNK), 0)
    c = lax.broadcasted_iota(jnp.int32, (CHUNK, CHUNK), 1)
    return jnp.where(r >= c, ws_ref[hd], 0.0).astype(BF16)


def _gmlp_fwd(zuv, gv, ws, b_t):
    T = zuv.shape[0]
    tg = min(ROW_TILE, T)

    def body(zu_ref, zv_ref, gv_ref, ws_ref, bt_ref, o_ref):
        u = _gelu(zu_ref[...])
        _, _, vn = _gmlp_norm(zv_ref[...], gv_ref[...])
        for hd in range(GM_HEADS):
            wc = _causal_ws(ws_ref, hd)
            cols = slice(hd * CHUNK, (hd + 1) * CHUNK)
            for c in range(tg // CHUNK):
                rows = slice(c * CHUNK, (c + 1) * CHUNK)
                sv = _dot(wc, vn[rows, cols]) + bt_ref[:, hd:hd + 1]
                o_ref[rows, cols] = (u[rows, cols] * sv).astype(BF16)

    return pl.pallas_call(
        body, name="gmlp_fwd", grid=(T // tg,),
        in_specs=[pl.BlockSpec((tg, GM_WIDTH), lambda i: (i, 0)), pl.BlockSpec((tg, GM_WIDTH), lambda i: (i, 1)),
                  pl.BlockSpec((1, GM_WIDTH), lambda i: (0, 0)),
                  pl.BlockSpec((GM_HEADS, CHUNK, CHUNK), lambda i: (0, 0, 0)),
                  pl.BlockSpec((CHUNK, GM_HEADS), lambda i: (0, 0))],
        out_specs=pl.BlockSpec((tg, GM_WIDTH), lambda i: (i, 0)),
        out_shape=jax.ShapeDtypeStruct((T, GM_WIDTH), BF16),
        compiler_params=_params("arbitrary"),
    )(zuv, zuv, gv, ws, b_t)


def _gmlp_bwd(zuv, d_gm, gv, ws, b_t):
    T = zuv.shape[0]
    tg = min(ROW_TILE, T)
    ng = T // tg

    def body(zu_ref, zv_ref, dgm_ref, gv_ref, ws_ref, bt_ref, dz_ref, dgv_ref, dws_ref, dbt_ref, dsv_acc, dvn_s):
        i = pl.program_id(0)

        @pl.when(i == 0)
        def _():
            dgv_ref[...] = jnp.zeros_like(dgv_ref)
            dws_ref[...] = jnp.zeros_like(dws_ref)
            dsv_acc[...] = jnp.zeros_like(dsv_acc)

        zu = zu_ref[...]
        zv = zv_ref[...]
        dgm = dgm_ref[...]
        gvv = gv_ref[...]
        u = _gelu(zu)
        rv, vh, vn = _gmlp_norm(zv, gvv)
        dsv = dgm * u
        dsv_b = dsv.astype(BF16)
        for hd in range(GM_HEADS):
            wc = _causal_ws(ws_ref, hd)
            cols = slice(hd * CHUNK, (hd + 1) * CHUNK)
            dws = jnp.zeros((CHUNK, CHUNK), F32)
            dsv_sum = jnp.zeros((CHUNK, CHUNK), F32)
            for c in range(tg // CHUNK):
                rows = slice(c * CHUNK, (c + 1) * CHUNK)
                vch = vn[rows, cols]
                sv = _dot(wc, vch) + bt_ref[:, hd:hd + 1]
                dz_ref[rows, cols] = (dgm[rows, cols] * sv * _gelu_grad(zu[rows, cols])).astype(BF16)
                dws += _dot_nt(dsv_b[rows, cols], vch)
                dsv_sum += dsv[rows, cols]
                dvn_s[rows, cols] = _dot_tn(wc, dsv_b[rows, cols])
            dws_ref[hd] += dws
            dsv_acc[:, cols] += dsv_sum
        dvn = dvn_s[...]
        dvh = dvn * gvv
        dv = rv * (dvh - vh * jnp.mean(dvh * vh, axis=-1, keepdims=True))
        dgv_ref[...] += jnp.sum(dvn * vh, axis=0, keepdims=True)
        dz_ref[:, GM_WIDTH:] = (dv * _gelu_grad(zv)).astype(BF16)

        @pl.when(i == ng - 1)
        def _():
            r = lax.broadcasted_iota(jnp.int32, (CHUNK, CHUNK), 0)
            c = lax.broadcasted_iota(jnp.int32, (CHUNK, CHUNK), 1)
            for hd in range(GM_HEADS):
                dws_ref[hd] = jnp.where(r >= c, dws_ref[hd], 0.0)
                dbt_ref[:, hd:hd + 1] = jnp.sum(dsv_acc[:, hd * CHUNK:(hd + 1) * CHUNK], axis=1, keepdims=True)

    return pl.pallas_call(
        body, name="gmlp_bwd", grid=(ng,),
        in_specs=[pl.BlockSpec((tg, GM_WIDTH), lambda i: (i, 0)), pl.BlockSpec((tg, GM_WIDTH), lambda i: (i, 1)),
                  pl.BlockSpec((tg, GM_WIDTH), lambda i: (i, 0)),
                  pl.BlockSpec((1, GM_WIDTH), lambda i: (0, 0)),
                  pl.BlockSpec((GM_HEADS, CHUNK, CHUNK), lambda i: (0, 0, 0)),
                  pl.BlockSpec((CHUNK, GM_HEADS), lambda i: (0, 0))],
        out_specs=[pl.BlockSpec((tg, 2 * GM_WIDTH), lambda i: (i, 0)),
                   pl.BlockSpec((1, GM_WIDTH), lambda i: (0, 0)),
                   pl.BlockSpec((GM_HEADS, CHUNK, CHUNK), lambda i: (0, 0, 0)),
                   pl.BlockSpec((CHUNK, GM_HEADS), lambda i: (0, 0))],
        out_shape=[jax.ShapeDtypeStruct((T, 2 * GM_WIDTH), BF16), jax.ShapeDtypeStruct((1, GM_WIDTH), F32),
                   jax.ShapeDtypeStruct((GM_HEADS, CHUNK, CHUNK), F32),
                   jax.ShapeDtypeStruct((CHUNK, GM_HEADS), F32)],
        scratch_shapes=[pltpu.VMEM((CHUNK, GM_WIDTH), F32), pltpu.VMEM((tg, GM_WIDTH), F32)],
        compiler_params=_params("arbitrary"),
    )(zuv, zuv, d_gm, gv, ws, b_t)


def _scan_matrix(blk, keep):
    r = lax.broadcasted_iota(jnp.int32, (blk, blk), 0)
    c = lax.broadcasted_iota(jnp.int32, (blk, blk), 1)
    return jnp.where(keep(r, c), 1.0, 0.0).astype(BF16)


def _scan_matrix2(blk, keep, value):
    m = _scan_matrix(blk, keep) * value
    return jnp.concatenate([m, m], axis=0)


def _scan(x, mat2):
    hi, lo = _split_bf16(x)
    return _dot(jnp.concatenate([hi, lo], axis=1), mat2)


def _head_masks(q):
    lane = lax.broadcasted_iota(jnp.int32, q.shape, 1)
    m0 = lane < SB_HEAD_DIM
    zero = jnp.zeros_like(q)
    return m0, jnp.where(m0, q, zero), jnp.where(m0, zero, q)


_LOG2E = 1.4426950408889634


def _softplus_parts(z):
    e = jnp.exp2(jnp.abs(z) * (-_LOG2E))
    ope = 1.0 + e
    return e, ope, jnp.maximum(z, 0.0) + jnp.log(ope)


def _attn_fwd(qkv, rider=None):
    T = qkv.shape[0]
    tk = ATTN_KEY_BLOCK
    tq = min(ATTN_Q_ROWS, T)
    band = tq // tk
    assert band % ATTN_UNROLL == 0 or T == tq
    ngrp = SB_WIDTH // LANES

    def body(q_ref, k_ref, v_ref, o_ref, l_ref, acc, run):
        i = pl.program_id(1)
        suffix = _scan_matrix2(tk, lambda r, c: r >= c, -1.0)
        row = lax.broadcasted_iota(jnp.int32, (tq, tk), 0)
        col = lax.broadcasted_iota(jnp.int32, (tq, tk), 1)
        m0, q0, q1 = _head_masks(q_ref[...] * SB_SCALE)
        acc[...] = jnp.zeros_like(acc)
        run[...] = jnp.zeros_like(run)

        def tiles(work):
            heads = (q0, q1)
            kv = []
            for j, _ in work:
                start = pl.multiple_of(j * tk, tk)
                kv.append((k_ref[pl.ds(start, tk), :], v_ref[pl.ds(start, tk), :]))
            z = [[_dot_nt(qh, kj) for qh in heads] for kj, _ in kv]
            sp = [[_softplus_parts(zz)[2] for zz in zt] for zt in z]
            sp = [[s if m is None else jnp.where(m, s, 0.0) for s in st] for st, (_, m) in zip(sp, work)]
            res = [[_scan(s, suffix) for s in st] for st in sp]
            runs = [run[hd] for hd in range(len(heads))]
            a = []
            for t, (_, m) in enumerate(work):
                at = []
                for hd in range(len(heads)):
                    av = jnp.exp(z[t][hd] + (runs[hd] + res[t][hd]))
                    at.append(av if m is None else jnp.where(m, av, 0.0))
                    runs[hd] = runs[hd] + res[t][hd][:, 0:1]
                a.append(at)
            for hd in range(len(heads)):
                run[hd] = runs[hd]
                upd = _dot(a[0][hd].astype(BF16), kv[0][1])
                for t in range(1, len(work)):
                    upd = upd + _dot(a[t][hd].astype(BF16), kv[t][1])
                acc[hd] += upd

        tiles([(i * band + jb, jb * tk + col < row) for jb in reversed(range(band))])

        def full_step(it, carry):
            tiles([(i * band - 1 - ATTN_UNROLL * it - u, None) for u in range(ATTN_UNROLL)])
            return carry

        lax.fori_loop(0, i * (band // ATTN_UNROLL), full_step, 0)
        o_ref[...] = jnp.where(m0, acc[0], acc[1]).astype(BF16)
        l_ref[...] = jnp.where(m0, jnp.broadcast_to(run[0], (tq, LANES)), jnp.broadcast_to(run[1], (tq, LANES)))

    return _pallas(
        body, rider, name="attn_fwd", grid=(ngrp, T // tq),
        in_specs=[pl.BlockSpec((tq, LANES), lambda g, i: (i, g)),
                  pl.BlockSpec((T, LANES), lambda g, i: (0, ngrp + g)),
                  pl.BlockSpec((T, LANES), lambda g, i: (0, 2 * ngrp + g))],
        out_specs=[pl.BlockSpec((tq, LANES), lambda g, i: (i, g)),
                   pl.BlockSpec((tq, LANES), lambda g, i: (i, g))],
        out_shape=[jax.ShapeDtypeStruct((T, SB_WIDTH), BF16), jax.ShapeDtypeStruct((T, SB_WIDTH), F32)],
        scratch_shapes=[pltpu.VMEM((2, tq, LANES), F32), pltpu.VMEM((2, tq, 1), F32)],
        compiler_params=_params("arbitrary", "arbitrary"),
    )(qkv, qkv, qkv)


def _attn_bwd(qkv, d_o, ltot, rider=None):
    T = qkv.shape[0]
    tk = ATTN_KEY_BLOCK
    tq = min(ATTN_BWD_Q_ROWS, T)
    band = tq // tk
    nq = T // tq
    ngrp = SB_WIDTH // LANES

    def body(q_ref, k_ref, v_ref, do_ref, l_ref, dq_ref, dk_ref, dv_ref, dq_acc, dk_acc, dv_acc, lpre, ppre):
        i = pl.program_id(1)

        @pl.when(i == 0)
        def _():
            dk_acc[...] = jnp.zeros_like(dk_acc)
            dv_acc[...] = jnp.zeros_like(dv_acc)

        excl = _scan_matrix(tk, lambda r, c: r < c)
        excl2 = jnp.concatenate([excl, excl], axis=0)
        row = lax.broadcasted_iota(jnp.int32, (tq, tk), 0)
        col = lax.broadcasted_iota(jnp.int32, (tq, tk), 1)
        m0, q0, q1 = _head_masks(q_ref[...] * SB_SCALE)
        _, d0, d1 = _head_masks(do_ref[...].astype(BF16))
        lt = l_ref[...]
        ltots = (lt[:, 0:1], lt[:, SB_HEAD_DIM:SB_HEAD_DIM + 1])
        dq_acc[...] = jnp.zeros_like(dq_acc)
        lpre[...] = jnp.zeros_like(lpre)
        ppre[...] = jnp.zeros_like(ppre)

        def tiles(work):
            heads = ((q0, d0), (q1, d1))
            nhd = len(heads)
            starts = [pl.multiple_of(j * tk, tk) for j, _ in work]
            kv = [(k_ref[pl.ds(st, tk), :], v_ref[pl.ds(st, tk), :]) for st in starts]
            masks = [m for _, m in work]
            every = [(t, hd) for t in range(len(work)) for hd in range(nhd)]
            z = {(t, hd): _dot_nt(heads[hd][0], kv[t][0]) for t, hd in every}
            da = {(t, hd): _dot_nt(heads[hd][1], kv[t][1]) for t, hd in every}
            sp, beta, one_m_beta = {}, {}, {}
            for key in every:
                e, ope, s = _softplus_parts(z[key])
                rinv = 1.0 / ope
                small = e * rinv
                pos = z[key] >= 0.0
                beta[key] = jnp.where(pos, rinv, small)
                one_m_beta[key] = jnp.where(pos, small, rinv)
                sp[key] = s if masks[key[0]] is None else jnp.where(masks[key[0]], s, 0.0)
            res = {key: _scan(sp[key], excl2) for key in every}
            lp = [lpre[hd] for hd in range(nhd)]
            a, p = {}, {}
            for t, hd in every:
                av = jnp.exp(z[t, hd] + ((ltots[hd] + lp[hd]) + res[t, hd]))
                a[t, hd] = av if masks[t] is None else jnp.where(masks[t], av, 0.0)
                p[t, hd] = a[t, hd] * da[t, hd]
                lp[hd] = lp[hd] + (res[t, hd][:, tk - 1:tk] + sp[t, hd][:, tk - 1:tk])
            resp = {key: _dot(p[key].astype(BF16), excl) for key in every}
            pp = [ppre[hd] for hd in range(nhd)]
            dzb = {}
            for t, hd in every:
                dz = p[t, hd] * one_m_beta[t, hd] - beta[t, hd] * (pp[hd] + resp[t, hd])
                if masks[t] is not None:
                    dz = jnp.where(masks[t], dz, 0.0)
                dzb[t, hd] = dz.astype(BF16)
                pp[hd] = pp[hd] + (resp[t, hd][:, tk - 1:tk] + p[t, hd][:, tk - 1:tk])
            for hd in range(nhd):
                lpre[hd] = lp[hd]
                ppre[hd] = pp[hd]
                upd = _dot(dzb[0, hd], kv[0][0])
                for t in range(1, len(work)):
                    upd = upd + _dot(dzb[t, hd], kv[t][0])
                dq_acc[hd] += upd
            for t, st in enumerate(starts):
                dk = _dot_tn(dzb[t, 0], heads[0][0])
                dv = _dot_tn(a[t, 0].astype(BF16), heads[0][1])
                for hd in range(1, nhd):
                    dk = dk + _dot_tn(dzb[t, hd], heads[hd][0])
                    dv = dv + _dot_tn(a[t, hd].astype(BF16), heads[hd][1])
                dk_acc[pl.ds(st, tk), :] += dk
                dv_acc[pl.ds(st, tk), :] += dv

        def full_step(j, carry):
            tiles([(j, None)])
            return carry

        lax.fori_loop(0, i * band, full_step, 0)
        for jb in range(band):
            tiles([(i * band + jb, jb * tk + col < row)])
        dq_ref[...] = (jnp.where(m0, dq_acc[0], dq_acc[1]) * SB_SCALE).astype(BF16)

        @pl.when(i == nq - 1)
        def _():
            dk_ref[...] = dk_acc[...].astype(BF16)
            dv_ref[...] = dv_acc[...].astype(BF16)

    qmap = lambda g, i: (i, g)
    return _pallas(
        body, rider, name="attn_bwd", grid=(ngrp, nq),
        in_specs=[pl.BlockSpec((tq, LANES), qmap),
                  pl.BlockSpec((T, LANES), lambda g, i: (0, ngrp + g)),
                  pl.BlockSpec((T, LANES), lambda g, i: (0, 2 * ngrp + g)),
                  pl.BlockSpec((tq, LANES), qmap), pl.BlockSpec((tq, LANES), qmap)],
        out_specs=[pl.BlockSpec((tq, LANES), qmap),
                   pl.BlockSpec((T, LANES), lambda g, i: (0, g)),
                   pl.BlockSpec((T, LANES), lambda g, i: (0, g))],
        out_shape=[jax.ShapeDtypeStruct((T, SB_WIDTH), BF16)] * 3,
        scratch_shapes=[pltpu.VMEM((2, tq, LANES), F32), pltpu.VMEM((T, LANES), F32),
                        pltpu.VMEM((T, LANES), F32), pltpu.VMEM((2, tq, 1), F32),
                        pltpu.VMEM((2, tq, 1), F32)],
        compiler_params=_params("arbitrary", "arbitrary"),
    )(qkv, qkv, qkv, d_o, ltot)


def _mix_out_fwd(h, gm, sb, w):
    T, D = h.shape
    tm = min(ROW_TILE, T)

    def body(h_ref, gm_ref, sb_ref, w_ref, o_ref):
        o_ref[...] = h_ref[...] + _dot(gm_ref[...], w_ref[:GM_WIDTH, :]) + _dot(sb_ref[...], w_ref[GM_WIDTH:, :])

    row = lambda i: (i, 0)
    return pl.pallas_call(
        body, name="mix_out_fwd", grid=(T // tm,),
        in_specs=[pl.BlockSpec((tm, D), row), pl.BlockSpec((tm, GM_WIDTH), row), pl.BlockSpec((tm, SB_WIDTH), row),
                  pl.BlockSpec((GM_WIDTH + SB_WIDTH, D), lambda i: (0, 0))],
        out_specs=pl.BlockSpec((tm, D), row),
        out_shape=jax.ShapeDtypeStruct((T, D), F32),
        compiler_params=_params("arbitrary"),
    )(h, gm, sb, w)


def _mix_out_bwd(dh, w):
    T, D = dh.shape
    tm = min(ROW_TILE, T)

    def body(dh_ref, w_ref, dgm_ref, dsb_ref, dhb_ref):
        dhb = dh_ref[...].astype(BF16)
        dhb_ref[...] = dhb
        dgm_ref[...] = _dot_nt(dhb, w_ref[:GM_WIDTH, :])
        dsb_ref[...] = _dot_nt(dhb, w_ref[GM_WIDTH:, :])

    row = lambda i: (i, 0)
    return pl.pallas_call(
        body, name="mix_out_bwd", grid=(T // tm,),
        in_specs=[pl.BlockSpec((tm, D), row), pl.BlockSpec((GM_WIDTH + SB_WIDTH, D), lambda i: (0, 0))],
        out_specs=[pl.BlockSpec((tm, GM_WIDTH), row), pl.BlockSpec((tm, SB_WIDTH), row), pl.BlockSpec((tm, D), row)],
        out_shape=[jax.ShapeDtypeStruct((T, GM_WIDTH), F32), jax.ShapeDtypeStruct((T, SB_WIDTH), F32),
                   jax.ShapeDtypeStruct((T, D), BF16)],
        compiler_params=_params("arbitrary"),
    )(dh, w)


def _tail(h3, p, target, g_ple, g_fin, w_gate, w_proj):
    T, D = h3.shape
    PD = p.shape[1]
    tm = min(ROW_TILE, T)

    def body(h_ref, p_ref, t_ref, gp_ref, gf_ref, wg_ref, wp_ref,
             loss_ref, dh_ref, n4_ref, dgl_ref, dpp_ref, dgp_ref, dgf_ref):
        @pl.when(pl.program_id(0) == 0)
        def _():
            loss_ref[...] = jnp.zeros_like(loss_ref)
            dgp_ref[...] = jnp.zeros_like(dgp_ref)
            dgf_ref[...] = jnp.zeros_like(dgf_ref)

        h3v = h_ref[...]
        gp = gp_ref[...]
        gf = gf_ref[...]
        r3, xh3 = _rms_parts(h3v)
        n4 = (xh3 * gp).astype(BF16)
        n4_ref[...] = n4
        gate = _sigmoid(_dot(n4, wg_ref[...]))
        pp = _dot(p_ref[...], wp_ref[...])
        h4 = h3v + gate * pp
        r4, xh4 = _rms_parts(h4)
        err = xh4 * gf - t_ref[...]
        loss_ref[...] += jnp.full(loss_ref.shape, (0.5 / D) * jnp.sum(err * err), F32)
        dy = err * (1.0 / D)
        dgf_ref[...] += jnp.sum(dy * xh4, axis=0, keepdims=True)
        dyg = dy * gf
        dh4 = r4 * (dyg - xh4 * jnp.mean(dyg * xh4, axis=-1, keepdims=True))
        dpp_ref[...] = (dh4 * gate).astype(BF16)
        dgl = (dh4 * pp * gate * (1.0 - gate)).astype(BF16)
        dgl_ref[...] = dgl
        dn4 = _dot_nt(dgl, wg_ref[...])
        dgp_ref[...] += jnp.sum(dn4 * xh3, axis=0, keepdims=True)
        dn4g = dn4 * gp
        dh_ref[...] = dh4 + r3 * (dn4g - xh3 * jnp.mean(dn4g * xh3, axis=-1, keepdims=True))

    row = lambda i: (i, 0)
    one = lambda i: (0, 0)
    return pl.pallas_call(
        body, name="tail", grid=(T // tm,),
        in_specs=[pl.BlockSpec((tm, D), row), pl.BlockSpec((tm, PD), row), pl.BlockSpec((tm, D), row),
                  pl.BlockSpec((1, D), one), pl.BlockSpec((1, D), one),
                  pl.BlockSpec((D, D), one), pl.BlockSpec((PD, D), one)],
        out_specs=[pl.BlockSpec((1, LANES), one), pl.BlockSpec((tm, D), row), pl.BlockSpec((tm, D), row),
                   pl.BlockSpec((tm, D), row), pl.BlockSpec((tm, D), row),
                   pl.BlockSpec((1, D), one), pl.BlockSpec((1, D), one)],
        out_shape=[jax.ShapeDtypeStruct((1, LANES), F32), jax.ShapeDtypeStruct((T, D), F32),
                   jax.ShapeDtypeStruct((T, D), BF16), jax.ShapeDtypeStruct((T, D), BF16),
                   jax.ShapeDtypeStruct((T, D), BF16),
                   jax.ShapeDtypeStruct((1, D), F32), jax.ShapeDtypeStruct((1, D), F32)],
        compiler_params=_params("arbitrary"),
    )(h3, p, target, g_ple, g_fin, w_gate, w_proj)


FFN1_W = ("ffn1_w_in", "ffn1_w_out")
MIX_W = ("w_mix_in", "w_mix_out")
REST_W = ("ffn2_w_in", "ffn2_w_out", "ple_w_gate", "ple_w_proj")
BIG_W = FFN1_W + MIX_W + REST_W
COLUMN_SHARDED = ("w_mix_in", "ple_w_proj")


class _Traffic:
    def __init__(self, shards):
        self.shards = shards
        self.parts = {}

    @staticmethod
    def _full(name, gathered):
        if name in COLUMN_SHARDED:
            return jnp.transpose(gathered, (1, 0, 2)).reshape(gathered.shape[1], -1)
        if name.endswith("_w_in"):
            return gathered
        return gathered.reshape(-1, gathered.shape[-1])

    @staticmethod
    def _blocks(name, grad):
        name = name.split("/")[0]
        if name in COLUMN_SHARDED:
            return jnp.transpose(grad.reshape(grad.shape[0], N_DEV, -1), (1, 0, 2))
        if name.endswith("_w_in"):
            return grad
        return grad.reshape(N_DEV, -1, grad.shape[-1])

    def gather_now(self, names):
        got = _exchange("gather_" + names[0], [self.shards[n] for n in names], [False] * len(names))
        return self.gathered(names, got)

    def gather_rider(self, names):
        return [self.shards[n] for n in names], [False] * len(names)

    def gathered(self, names, got):
        return {n: self._full(n, g) for n, g in zip(names, got)}

    def scatter_rider(self, grads):
        return [self._blocks(n, g) for n, g in grads.items()], [True] * len(grads)

    def scattered(self, names, got):
        self.parts.update(zip(names, got))

    def finish(self, grads, small):
        arrays, flags = self.scatter_rider(grads)
        got = _exchange("scatter_last", arrays + [small], flags + [False])
        self.scattered(list(grads), got[:-1])
        return got[-1]


def _local_step(traffic, x, p_bf, target, g1, gmix, gv, ws, b_t, g2, gple, gfin, pack_small):
    T, D = x.shape
    tm = min(ROW_TILE, T)

    w = traffic.gather_now(FFN1_W)
    h1, n1, G1, U1, a1, *got = _ffn_fwd("ffn1_fwd", x, g1, w["ffn1_w_in"], w["ffn1_w_out"],
                                        rider=traffic.gather_rider(MIX_W))
    w.update(traffic.gathered(MIX_W, got))
    n2, zuv, qkv = _mix_in_fwd(h1, gmix, w["w_mix_in"])
    gm = _gmlp_fwd(zuv, gv, ws, b_t)
    sb, ltot, *got = _attn_fwd(qkv, rider=traffic.gather_rider(REST_W))
    w.update(traffic.gathered(REST_W, got))
    h2 = _mix_out_fwd(h1, gm, sb, w["w_mix_out"])
    h3, n3, G2, U2, a2 = _ffn_fwd("ffn2_fwd", h2, g2, w["ffn2_w_in"], w["ffn2_w_out"])
    loss, dh3, n4, d_gl, d_pp, dg_ple, dg_fin = _tail(h3, p_bf, target, gple, gfin, w["ple_w_gate"], w["ple_w_proj"])

    nb, _, FB = w["ffn1_w_in"].shape
    nh = nb // 2

    tt = min(GRAD_ROW_TILE, T)

    def dw_out(name, a, d_out):
        return _matmul_tn(name, a, d_out, nh, (1, tt, FB), lambda j, t: (j, t, 0), (tt, D), lambda j, t: (t, 0),
                          (nh, FB, D), (1, FB, D), lambda j, t: (j, 0, 0))

    def dense_tn(name, a, b, ncol):
        ka, nbw = a.shape[1], b.shape[1] // ncol
        return _matmul_tn(name, a, b, ncol, (tt, ka), lambda j, t: (t, 0), (tt, nbw), lambda j, t: (t, j),
                          (ka, b.shape[1]), (ka, nbw), lambda j, t: (0, j))

    grads = dict(ple_w_gate=dense_tn("dw_ple_gate", n4, d_gl, 2), ple_w_proj=dense_tn("dw_ple_proj", p_bf, d_pp, 1))
    dh2, dg2, dG2, dU2, dout2 = _ffn_bwd("ffn2_bwd", dh3, h2, g2, G2, U2, w["ffn2_w_in"], w["ffn2_w_out"])
    grads["ffn2_w_in"], = _dw_in("ffn2_dw_in", n3, dG2, dU2, 0, 1)
    grads["ffn2_w_out"] = dw_out("ffn2_dw_out", a2, dout2)
    grads = {n: grads[n] for n in REST_W}

    d_gm, d_sb, dh2_bf = _mix_out_bwd(dh2, w["w_mix_out"])
    dw_mo = jnp.concatenate([dense_tn("dw_mix_out_gm", gm, dh2_bf, 1), dense_tn("dw_mix_out_sb", sb, dh2_bf, 1)], axis=0)
    dzuv, dgv, dws, db_t = _gmlp_bwd(zuv, d_gm, gv, ws, b_t)
    dq, dk, dv, *got = _attn_bwd(qkv, d_sb, ltot, rider=traffic.scatter_rider(grads))
    traffic.scattered(REST_W, got)
    dqkv = jnp.concatenate([dq, dk, dv], axis=1)
    dw_mi = jnp.concatenate([dense_tn("dw_mix_in_uv", n2, dzuv, 2), dense_tn("dw_mix_in_qkv", n2, dqkv, 3)], axis=1)
    dh1, dgmix, dout1 = _mix_in_bwd(dzuv, dqkv, w["w_mix_in"], h1, gmix, dh2)

    grads = dict(w_mix_in=dw_mi, w_mix_out=dw_mo, ffn1_w_out=dw_out("ffn1_dw_out", a1, dout1))
    dx, dg1, dG1, dU1, _, *got = _ffn_bwd("ffn1_bwd", dh1, x, g1, G1, U1, w["ffn1_w_in"], w["ffn1_w_out"],
                                         rider=traffic.scatter_rider(grads))
    traffic.scattered(list(grads), got)
    top, = _dw_in("ffn1_dw_in_top", n1, dG1, dU1, 0, 2)
    bottom, *got = _dw_in("ffn1_dw_in_bottom", n1, dG1, dU1, 1, 2, rider=traffic.scatter_rider({"ffn1_w_in/0": top}))
    traffic.scattered(["ffn1_w_in/0"], got)

    small = pack_small(dict(ffn1_norm=dg1, mix_norm=dgmix, gmlp_v_norm=dgv, gmlp_w_s=dws, gmlp_b=jnp.transpose(db_t),
                            ffn2_norm=dg2, ple_norm=dg_ple, final_norm=dg_fin), loss)
    return dx, traffic.finish({"ffn1_w_in/1": bottom}, small)


def _peer(d):
    x, y, c = lax.axis_index("x"), lax.axis_index("y"), lax.axis_index("c")
    px = 1 - x if d & 4 else x
    py = 1 - y if d & 2 else y
    pc = 1 - c if d & 1 else c
    return (px, py, pc), 4 * px + 2 * py + pc


def _exchange_copies(ins, outs, send, recv, local, scatter):
    _, me = _peer(0)
    copies = []
    for t in range(len(ins)):
        src = ins[t].at[me] if scatter[t] else ins[t]
        copies.append(pltpu.make_async_copy(src, outs[t].at[me], local.at[t]))
    for d in range(1, N_DEV):
        peer, pidx = _peer(d)
        for t in range(len(ins)):
            src = ins[t].at[pidx] if scatter[t] else ins[t]
            copies.append(pltpu.make_async_remote_copy(
                src_ref=src, dst_ref=outs[t].at[me], send_sem=send.at[t, d - 1], recv_sem=recv.at[t, d - 1],
                device_id=peer, device_id_type=MESH))
    return copies


def _exchange_shapes(arrays, scatter):
    return [jax.ShapeDtypeStruct(a.shape if sc else (N_DEV,) + a.shape, a.dtype) for a, sc in zip(arrays, scatter)]


def _exchange_sems(n):
    return [pltpu.SemaphoreType.DMA((n, N_DEV - 1)), pltpu.SemaphoreType.DMA((n, N_DEV - 1)),
            pltpu.SemaphoreType.DMA((n,))]


_ANY = pl.BlockSpec(memory_space=pl.ANY)


def _exchange(name, arrays, scatter):
    n = len(arrays)

    def body(*refs):
        copies = _exchange_copies(refs[:n], refs[n:2 * n], *refs[2 * n:], scatter)
        for cp in copies:
            cp.start()
        for cp in copies:
            cp.wait()

    return pl.pallas_call(
        body, name=name, in_specs=[_ANY] * n, out_specs=[_ANY] * n, out_shape=_exchange_shapes(arrays, scatter),
        scratch_shapes=_exchange_sems(n),
    )(*arrays)


def _pallas(body, rider, *, name, grid, in_specs, out_specs, out_shape, scratch_shapes=(), compiler_params=None):
    if rider is None:
        return pl.pallas_call(body, name=name, grid=grid, in_specs=in_specs, out_specs=out_specs, out_shape=out_shape,
                              scratch_shapes=list(scratch_shapes), compiler_params=compiler_params)
    arrays, scatter = rider
    n, ni, no, ns = len(arrays), len(in_specs), len(out_specs), len(scratch_shapes)

    def carried(*refs):
        ins, r_in = refs[:ni], refs[ni:ni + n]
        outs, r_out = refs[ni + n:ni + n + no], refs[ni + n + no:ni + 2 * n + no]
        scratch, sems = refs[ni + 2 * n + no:ni + 2 * n + no + ns], refs[ni + 2 * n + no + ns:]
        ids = [pl.program_id(ax) for ax in range(len(grid))]
        first = functools.reduce(jnp.logical_and, [i == 0 for i in ids])
        last = functools.reduce(jnp.logical_and, [i == g - 1 for i, g in zip(ids, grid)])

        @pl.when(first)
        def _():
            for cp in _exchange_copies(r_in, r_out, *sems, scatter):
                cp.start()

        body(*ins, *outs, *scratch)

        @pl.when(last)
        def _():
            for cp in _exchange_copies(r_in, r_out, *sems, scatter):
                cp.wait()

    call = pl.pallas_call(
        carried, name=name, grid=grid, in_specs=list(in_specs) + [_ANY] * n, out_specs=list(out_specs) + [_ANY] * n,
        out_shape=list(out_shape) + _exchange_shapes(arrays, scatter),
        scratch_shapes=list(scratch_shapes) + _exchange_sems(n), compiler_params=compiler_params)
    return lambda *args: call(*args, *arrays)


def _adamw_math(g, w, m, v):
    m_new = ADAM_B1 * m + (1.0 - ADAM_B1) * g
    v_new = ADAM_B2 * v + (1.0 - ADAM_B2) * (g * g)
    m_hat = m_new / (1.0 - ADAM_B1 ** ADAM_STEP)
    v_hat = v_new / (1.0 - ADAM_B2 ** ADAM_STEP)
    delta = -ADAM_LR * (m_hat / (jnp.sqrt(v_hat) + ADAM_EPS) + ADAM_WD * w)
    return delta, m_new, v_new


def _adamw(name, parts, w, m, v):
    R, C = w.shape
    tr = R
    for cand in (256, 128, 64, 32, 16, 8):
        if R % cand == 0:
            tr = cand
            break

    def body(p_ref, w_ref, m_ref, v_ref, g_ref, d_ref, nm_ref, nv_ref):
        g = p_ref[0].astype(F32)
        for j in range(1, N_DEV):
            g = g + p_ref[j].astype(F32)
        g_ref[...] = g
        d_ref[...], nm_ref[...], nv_ref[...] = _adamw_math(g, w_ref[...], m_ref[...], v_ref[...])

    row = lambda i: (i, 0)
    spec = pl.BlockSpec((tr, C), row)
    return pl.pallas_call(
        body, name=name, grid=(R // tr,),
        in_specs=[pl.BlockSpec((N_DEV, tr, C), lambda i: (0, i, 0)), spec, spec, spec],
        out_specs=[spec] * 4,
        out_shape=[jax.ShapeDtypeStruct((R, C), F32)] * 4,
        compiler_params=_params("arbitrary"),
    )(parts, w, m, v)


def _rows128(a):
    flat = a.reshape(-1, LANES)
    pad = (-flat.shape[0]) % SMALL_ROWS_ALIGN
    return jnp.pad(flat, ((0, pad), (0, 0))) if pad else flat


def _unrows(packed, like):
    n = like.size // LANES
    return packed[:n].reshape(like.shape)


def kernel(x, p, ffn1_norm, ffn1_w_in, ffn1_w_out, mix_norm, w_mix_in, gmlp_v_norm, gmlp_w_s, gmlp_b, w_mix_out, ffn2_norm, ffn2_w_in, ffn2_w_out, ple_norm, ple_w_gate, ple_w_proj, final_norm, loss_target, m_ffn1_norm, m_ffn1_w_in, m_ffn1_w_out, m_mix_norm, m_w_mix_in, m_gmlp_v_norm, m_gmlp_w_s, m_gmlp_b, m_w_mix_out, m_ffn2_norm, m_ffn2_w_in, m_ffn2_w_out, m_ple_norm, m_ple_w_gate, m_ple_w_proj, m_final_norm, v_ffn1_norm, v_ffn1_w_in, v_ffn1_w_out, v_mix_norm, v_w_mix_in, v_gmlp_v_norm, v_gmlp_w_s, v_gmlp_b, v_w_mix_out, v_ffn2_norm, v_ffn2_w_in, v_ffn2_w_out, v_ple_norm, v_ple_w_gate, v_ple_w_proj, v_final_norm):
    names = ["ffn1_norm", "ffn1_w_in", "ffn1_w_out", "mix_norm", "w_mix_in", "gmlp_v_norm", "gmlp_w_s", "gmlp_b",
             "w_mix_out", "ffn2_norm", "ffn2_w_in", "ffn2_w_out", "ple_norm", "ple_w_gate", "ple_w_proj", "final_norm"]
    W = dict(zip(names, [ffn1_norm, ffn1_w_in, ffn1_w_out, mix_norm, w_mix_in, gmlp_v_norm, gmlp_w_s, gmlp_b,
                         w_mix_out, ffn2_norm, ffn2_w_in, ffn2_w_out, ple_norm, ple_w_gate, ple_w_proj, final_norm]))
    M = dict(zip(names, [m_ffn1_norm, m_ffn1_w_in, m_ffn1_w_out, m_mix_norm, m_w_mix_in, m_gmlp_v_norm, m_gmlp_w_s,
                         m_gmlp_b, m_w_mix_out, m_ffn2_norm, m_ffn2_w_in, m_ffn2_w_out, m_ple_norm, m_ple_w_gate,
                         m_ple_w_proj, m_final_norm]))
    V = dict(zip(names, [v_ffn1_norm, v_ffn1_w_in, v_ffn1_w_out, v_mix_norm, v_w_mix_in, v_gmlp_v_norm, v_gmlp_w_s,
                         v_gmlp_b, v_w_mix_out, v_ffn2_norm, v_ffn2_w_in, v_ffn2_w_out, v_ple_norm, v_ple_w_gate,
                         v_ple_w_proj, v_final_norm]))
    small = [n for n in names if n not in BIG_W]
    D = x.shape[-1]

    def pack(src, last):
        return jnp.concatenate([_rows128(src[n]) for n in small] + [last], axis=0)

    offs = [0]
    for n in small:
        offs.append(offs[-1] + _rows128(W[n]).shape[0])

    traffic = _Traffic({n: W[n][0].astype(BF16) for n in BIG_W})
    dx, small_parts = _local_step(
        traffic, x[0], p[0, 0].astype(BF16), loss_target[0],
        W["ffn1_norm"], W["mix_norm"], W["gmlp_v_norm"], W["gmlp_w_s"][0], jnp.transpose(W["gmlp_b"][0]),
        W["ffn2_norm"], W["ple_norm"], W["final_norm"].reshape(1, D),
        lambda grads, loss_part: pack(grads, jnp.broadcast_to(loss_part, (SMALL_ROWS_ALIGN, LANES))))

    parts = traffic.parts
    parts["ffn1_w_in"] = jnp.concatenate([parts["ffn1_w_in/0"], parts["ffn1_w_in/1"]], axis=1)
    out = {}
    for n in BIG_W:
        out[n] = _adamw("adamw_" + n, parts[n], W[n][0], M[n][0], V[n][0])
    zeros = jnp.zeros((SMALL_ROWS_ALIGN, LANES), F32)
    sg, sd, sm, sv = _adamw("adamw_small", small_parts, pack(W, zeros), pack(M, zeros), pack(V, zeros))
    for k, n in enumerate(small):
        out[n] = tuple(_unrows(arr[offs[k]:offs[k + 1]], W[n]) for arr in (sg, sd, sm, sv))
    loss = sg[offs[len(small)], 0]

    res = [loss, dx[None]]
    for k in range(4):
        res += [out[n][k].reshape(W[n].shape) for n in names]
    return tuple(res)
```

```python
import functools

import jax
import jax.numpy as jnp
from jax import lax
from jax.experimental import pallas as pl
from jax.experimental.pallas import tpu as pltpu

F32 = jnp.float32
BF16 = jnp.bfloat16
MESH = pl.DeviceIdType.MESH

N_DEV = 8
EPS = 1e-6
ADAM_LR = 0.001
ADAM_B1 = 0.9
ADAM_B2 = 0.999
ADAM_EPS = 1e-08
ADAM_WD = 0.01
ADAM_STEP = 10

GM_WIDTH = 512
GM_HEADS = 4
CHUNK = 128
SB_WIDTH = 512
SB_HEAD_DIM = 64
SB_SCALE = 0.125
LANES = 128
SMALL_ROWS_ALIGN = 8

ROW_TILE = 512
GRAD_ROW_TILE = 2048
FFN_FWD_ROW_TILE = 1024
ATTN_Q_ROWS = 512
ATTN_BWD_Q_ROWS = 512
ATTN_KEY_BLOCK = 256
ATTN_UNROLL = 2
VMEM_LIMIT = 56 * 1024 * 1024


def _params(*sem):
    return pltpu.CompilerParams(dimension_semantics=sem, vmem_limit_bytes=VMEM_LIMIT)


def _dot(a, b):
    return jnp.dot(a, b, preferred_element_type=F32)


def _dot_nt(a, b):
    return lax.dot_general(a, b, (((1,), (1,)), ((), ())), preferred_element_type=F32)


def _dot_tn(a, b):
    return lax.dot_general(a, b, (((0,), (0,)), ((), ())), preferred_element_type=F32)


def _rms_parts(x):
    r = lax.rsqrt(jnp.mean(x * x, axis=-1, keepdims=True) + EPS)
    return r, x * r


def _rms_bwd(x, g, dy):
    r, xh = _rms_parts(x)
    dyg = dy * g
    dx = r * (dyg - xh * jnp.mean(dyg * xh, axis=-1, keepdims=True))
    return dx, jnp.sum(dy * xh, axis=0, keepdims=True)


def _sigmoid(x):
    return 1.0 / (1.0 + jnp.exp(-x))


_SQRT_HALF = 0.7071067811865476
_INV_SQRT_2PI = 0.3989422804014327


def _gelu(x):
    return 0.5 * x * (1.0 + lax.erf(x * _SQRT_HALF))


def _gelu_grad(x):
    return 0.5 * (1.0 + lax.erf(x * _SQRT_HALF)) + x * (_INV_SQRT_2PI * jnp.exp(-0.5 * x * x))


def _split_bf16(x):
    hi = x.astype(BF16)
    lo = (x - hi.astype(F32)).astype(BF16)
    return hi, lo


def _ffn_fwd(name, h, gain, w_in, w_out, rider=None):
    T, D = h.shape
    nb, _, FB = w_in.shape
    nh = nb // 2
    tm = min(FFN_FWD_ROW_TILE, T)

    def body(h_ref, g_ref, wg_ref, wu_ref, wo_ref, ho_ref, n_ref, G_ref, U_ref, a_ref, n_s, acc):
        jj = pl.program_id(1)

        @pl.when(jj == 0)
        def _():
            _, xh = _rms_parts(h_ref[...])
            n = (xh * g_ref[...]).astype(BF16)
            n_s[...] = n
            n_ref[...] = n
            acc[...] = jnp.zeros_like(acc)

        n = n_s[...]
        G = _dot(n, wg_ref[0])
        U = _dot(n, wu_ref[0])
        G_ref[0] = G.astype(BF16)
        U_ref[0] = U.astype(BF16)
        a = (G * _sigmoid(G) * U).astype(BF16)
        a_ref[0] = a
        acc[...] += _dot(a, wo_ref[...])

        @pl.when(jj == nh - 1)
        def _():
            ho_ref[...] = h_ref[...] + 0.5 * acc[...]

    row = lambda i, j: (i, 0)
    blk = lambda i, j: (j, i, 0)
    return _pallas(
        body, rider, name=name, grid=(T // tm, nh),
        in_specs=[pl.BlockSpec((tm, D), row),
                  pl.BlockSpec((1, D), lambda i, j: (0, 0)),
                  pl.BlockSpec((1, D, FB), lambda i, j: (j, 0, 0)),
                  pl.BlockSpec((1, D, FB), lambda i, j: (j + nh, 0, 0)),
                  pl.BlockSpec((FB, D), lambda i, j: (j, 0))],
        out_specs=[pl.BlockSpec((tm, D), row), pl.BlockSpec((tm, D), row),
                   pl.BlockSpec((1, tm, FB), blk), pl.BlockSpec((1, tm, FB), blk),
                   pl.BlockSpec((1, tm, FB), blk)],
        out_shape=[jax.ShapeDtypeStruct((T, D), F32), jax.ShapeDtypeStruct((T, D), BF16),
                   jax.ShapeDtypeStruct((nh, T, FB), BF16), jax.ShapeDtypeStruct((nh, T, FB), BF16),
                   jax.ShapeDtypeStruct((nh, T, FB), BF16)],
        scratch_shapes=[pltpu.VMEM((tm, D), BF16), pltpu.VMEM((tm, D), F32)],
        compiler_params=_params("arbitrary", "arbitrary"),
    )(h, gain, w_in, w_in, w_out)


def _ffn_bwd(name, dh, h_in, gain, G, U, w_in_t, w_out_t, rider=None):
    T, D = dh.shape
    nb, FB, _ = w_in_t.shape
    nh = nb // 2
    tm = min(ROW_TILE, T)

    def body(dh_ref, h_ref, g_ref, G_ref, U_ref, wg_ref, wu_ref, wo_ref,
             dhin_ref, dg_ref, dG_ref, dU_ref, do_ref, dn_acc, do_s):
        i = pl.program_id(0)
        jj = pl.program_id(1)

        @pl.when(jj == 0)
        def _():
            d_out = (0.5 * dh_ref[...]).astype(BF16)
            do_s[...] = d_out
            do_ref[...] = d_out
            dn_acc[...] = jnp.zeros_like(dn_acc)

        @pl.when((i == 0) & (jj == 0))
        def _():
            dg_ref[...] = jnp.zeros_like(dg_ref)

        halves = [slice(0, tm // 2), slice(tm // 2, tm)]
        da = [_dot(do_s[rows, :], wo_ref[0]) for rows in halves]
        dGU = []
        for rows, dav in zip(halves, da):
            Gv = G_ref[0, rows, :].astype(F32)
            Uv = U_ref[0, rows, :].astype(F32)
            sig = _sigmoid(Gv)
            dU = (dav * (Gv * sig)).astype(BF16)
            dG = (dav * Uv * (sig * (1.0 + Gv * (1.0 - sig)))).astype(BF16)
            dG_ref[0, rows, :] = dG
            dU_ref[0, rows, :] = dU
            dGU.append((dG, dU))
        dn = [_dot(dG, wg_ref[0]) for dG, _ in dGU]
        dn = [d + _dot(dU, wu_ref[0]) for d, (_, dU) in zip(dn, dGU)]
        for rows, d in zip(halves, dn):
            dn_acc[rows, :] += d

        @pl.when(jj == nh - 1)
        def _():
            dx, dg = _rms_bwd(h_ref[...], g_ref[...], dn_acc[...])
            dhin_ref[...] = dh_ref[...] + dx
            dg_ref[...] += dg

    row = lambda i, j: (i, 0)
    blk = lambda i, j: (j, i, 0)
    one = lambda i, j: (0, 0)
    return _pallas(
        body, rider, name=name, grid=(T // tm, nh),
        in_specs=[pl.BlockSpec((tm, D), row), pl.BlockSpec((tm, D), row), pl.BlockSpec((1, D), one),
                  pl.BlockSpec((1, tm, FB), blk), pl.BlockSpec((1, tm, FB), blk),
                  pl.BlockSpec((1, FB, D), lambda i, j: (j, 0, 0)),
                  pl.BlockSpec((1, FB, D), lambda i, j: (j + nh, 0, 0)),
                  pl.BlockSpec((1, D, FB), lambda i, j: (j, 0, 0))],
        out_specs=[pl.BlockSpec((tm, D), row), pl.BlockSpec((1, D), one),
                   pl.BlockSpec((1, tm, FB), blk), pl.BlockSpec((1, tm, FB), blk),
                   pl.BlockSpec((tm, D), row)],
        out_shape=[jax.ShapeDtypeStruct((T, D), F32), jax.ShapeDtypeStruct((1, D), F32),
                   jax.ShapeDtypeStruct((nh, T, FB), BF16), jax.ShapeDtypeStruct((nh, T, FB), BF16),
                   jax.ShapeDtypeStruct((T, D), BF16)],
        scratch_shapes=[pltpu.VMEM((tm, D), F32), pltpu.VMEM((tm, D), BF16)],
        compiler_params=_params("arbitrary", "arbitrary"),
    )(dh, h_in, gain, G, U, w_in_t, w_in_t, w_out_t)


def _matmul_tn(name, a, b, nj, a_block, a_map, b_block, b_map, out_shape, out_block, out_map):
    T = a.shape[-2]
    tt = a_block[-2]
    nt = T // tt
    kb, nbk = out_block[-2], out_block[-1]

    def body(a_ref, b_ref, o_ref, acc):
        t = pl.program_id(1)

        @pl.when(t == 0)
        def _():
            acc[...] = jnp.zeros_like(acc)

        av = a_ref[0] if len(a_block) == 3 else a_ref[...]
        bv = b_ref[0] if len(b_block) == 3 else b_ref[...]
        acc[...] += _dot_tn(av, bv)

        @pl.when(t == nt - 1)
        def _():
            if len(out_block) == 3:
                o_ref[0] = acc[...].astype(o_ref.dtype)
            else:
                o_ref[...] = acc[...].astype(o_ref.dtype)

    return pl.pallas_call(
        body, name=name, grid=(nj, nt),
        in_specs=[pl.BlockSpec(a_block, a_map), pl.BlockSpec(b_block, b_map)],
        out_specs=pl.BlockSpec(out_block, out_map),
        out_shape=jax.ShapeDtypeStruct(out_shape, BF16),
        scratch_shapes=[pltpu.VMEM((kb, nbk), F32)],
        compiler_params=_params("arbitrary", "arbitrary"),
    )(a, b)


def _dw_in(name, n, dG, dU, part, nparts, rider=None):
    T, D = n.shape
    nh, _, FB = dG.shape
    kr = D // nparts
    tt = min(GRAD_ROW_TILE, T)
    nt = T // tt

    def body(n_ref, dg_ref, du_ref, o_ref, acc):
        j = pl.program_id(0)
        t = pl.program_id(1)

        @pl.when(t == 0)
        def _():
            acc[...] = jnp.zeros_like(acc)

        def add(dz_ref):
            for rows in (slice(0, kr // 2), slice(kr // 2, kr)):
                acc[rows, :] += _dot_tn(n_ref[:, rows], dz_ref[0])

        @pl.when(j < nh)
        def _():
            add(dg_ref)

        @pl.when(j >= nh)
        def _():
            add(du_ref)

        @pl.when(t == nt - 1)
        def _():
            o_ref[0] = acc[...].astype(BF16)

    return _pallas(
        body, rider, name=name, grid=(2 * nh, nt),
        in_specs=[pl.BlockSpec((tt, kr), lambda j, t: (t, part)),
                  pl.BlockSpec((1, tt, FB), lambda j, t: (jnp.minimum(j, nh - 1), t, 0)),
                  pl.BlockSpec((1, tt, FB), lambda j, t: (jnp.maximum(j - nh, 0), t, 0))],
        out_specs=[pl.BlockSpec((1, kr, FB), lambda j, t: (j, 0, 0))],
        out_shape=[jax.ShapeDtypeStruct((2 * nh, kr, FB), BF16)],
        scratch_shapes=[pltpu.VMEM((kr, FB), F32)],
        compiler_params=_params("arbitrary", "arbitrary"),
    )(n, dG, dU)


def _mix_in_fwd(h, gain, w):
    T, D = h.shape
    W = w.shape[1]
    nuv = 2 * GM_WIDTH
    tm = min(ROW_TILE, T)

    def body(h_ref, g_ref, w_ref, n_ref, zuv_ref, qkv_ref):
        _, xh = _rms_parts(h_ref[...])
        n = (xh * g_ref[...]).astype(BF16)
        n_ref[...] = n
        z = _dot(n, w_ref[...])
        zuv_ref[...] = z[:, :nuv]
        qkv_ref[...] = z[:, nuv:].astype(BF16)

    row = lambda i: (i, 0)
    return pl.pallas_call(
        body, name="mix_in_fwd", grid=(T // tm,),
        in_specs=[pl.BlockSpec((tm, D), row), pl.BlockSpec((1, D), lambda i: (0, 0)),
                  pl.BlockSpec((D, W), lambda i: (0, 0))],
        out_specs=[pl.BlockSpec((tm, D), row), pl.BlockSpec((tm, nuv), row),
                   pl.BlockSpec((tm, W - nuv), row)],
        out_shape=[jax.ShapeDtypeStruct((T, D), BF16), jax.ShapeDtypeStruct((T, nuv), F32),
                   jax.ShapeDtypeStruct((T, W - nuv), BF16)],
        compiler_params=_params("arbitrary"),
    )(h, gain, w)


def _mix_in_bwd(dzuv, dqkv, w, h, gain, dh):
    T, D = h.shape
    W = w.shape[1]
    nuv = dzuv.shape[1]
    tm = min(ROW_TILE, T)

    def body(dzuv_ref, dqkv_ref, w_ref, h_ref, g_ref, dh_ref, dhin_ref, dg_ref, half_ref):
        @pl.when(pl.program_id(0) == 0)
        def _():
            dg_ref[...] = jnp.zeros_like(dg_ref)

        dn = _dot_nt(dzuv_ref[...], w_ref[:, :nuv]) + _dot_nt(dqkv_ref[...], w_ref[:, nuv:])
        dx, dg = _rms_bwd(h_ref[...], g_ref[...], dn)
        dh_in = dh_ref[...] + dx
        dhin_ref[...] = dh_in
        half_ref[...] = (0.5 * dh_in).astype(BF16)
        dg_ref[...] += dg

    row = lambda i: (i, 0)
    one = lambda i: (0, 0)
    return pl.pallas_call(
        body, name="mix_in_bwd", grid=(T // tm,),
        in_specs=[pl.BlockSpec((tm, nuv), row), pl.BlockSpec((tm, W - nuv), row),
                  pl.BlockSpec((D, W), one), pl.BlockSpec((tm, D), row), pl.BlockSpec((1, D), one),
                  pl.BlockSpec((tm, D), row)],
        out_specs=[pl.BlockSpec((tm, D), row), pl.BlockSpec((1, D), one), pl.BlockSpec((tm, D), row)],
        out_shape=[jax.ShapeDtypeStruct((T, D), F32), jax.ShapeDtypeStruct((1, D), F32),
                   jax.ShapeDtypeStruct((T, D), BF16)],
        compiler_params=_params("arbitrary"),
    )(dzuv, dqkv, w, h, gain, dh)


def _gmlp_norm(zv, gv):
    v = _gelu(zv)
    r, vh = _rms_parts(v)
    return r, vh, (vh * gv).astype(BF16)


def _causal_ws(ws_ref, hd):
    r = lax.broadcasted_iota(jnp.int32, (CHUNK, CHUNK), 0)
    c = lax.broadcasted_iota(jnp.int32, (CHUNK, CHUNK), 1)
    return jnp.where(r >= c, ws_ref[hd], 0.0).astype(BF16)


def _gmlp_fwd(zuv, gv, ws, b_t):
    T = zuv.shape[0]
    tg = min(ROW_TILE, T)

    def body(zu_ref, zv_ref, gv_ref, ws_ref, bt_ref, o_ref):
        u = _gelu(zu_ref[...])
        _, _, vn = _gmlp_norm(zv_ref[...], gv_ref[...])
        for hd in range(GM_HEADS):
            wc = _causal_ws(ws_ref, hd)
            cols = slice(hd * CHUNK, (hd + 1) * CHUNK)
            for c in range(tg // CHUNK):
                rows = slice(c * CHUNK, (c + 1) * CHUNK)
                sv = _dot(wc, vn[rows, cols]) + bt_ref[:, hd:hd + 1]
                o_ref[rows, cols] = (u[rows, cols] * sv).astype(BF16)

    return pl.pallas_call(
        body, name="gmlp_fwd", grid=(T // tg,),
        in_specs=[pl.BlockSpec((tg, GM_WIDTH), lambda i: (i, 0)), pl.BlockSpec((tg, GM_WIDTH), lambda i: (i, 1)),
                  pl.BlockSpec((1, GM_WIDTH), lambda i: (0, 0)),
                  pl.BlockSpec((GM_HEADS, CHUNK, CHUNK), lambda i: (0, 0, 0)),
                  pl.BlockSpec((CHUNK, GM_HEADS), lambda i: (0, 0))],
        out_specs=pl.BlockSpec((tg, GM_WIDTH), lambda i: (i, 0)),
        out_shape=jax.ShapeDtypeStruct((T, GM_WIDTH), BF16),
        compiler_params=_params("arbitrary"),
    )(zuv, zuv, gv, ws, b_t)


def _gmlp_bwd(zuv, d_gm, gv, ws, b_t):
    T = zuv.shape[0]
    tg = min(ROW_TILE, T)
    ng = T // tg

    def body(zu_ref, zv_ref, dgm_ref, gv_ref, ws_ref, bt_ref, dz_ref, dgv_ref, dws_ref, dbt_ref, dsv_acc, dvn_s):
        i = pl.program_id(0)

        @pl.when(i == 0)
        def _():
            dgv_ref[...] = jnp.zeros_like(dgv_ref)
            dws_ref[...] = jnp.zeros_like(dws_ref)
            dsv_acc[...] = jnp.zeros_like(dsv_acc)

        zu = zu_ref[...]
        zv = zv_ref[...]
        dgm = dgm_ref[...]
        gvv = gv_ref[...]
        u = _gelu(zu)
        rv, vh, vn = _gmlp_norm(zv, gvv)
        dsv = dgm * u
        dsv_b = dsv.astype(BF16)
        for hd in range(GM_HEADS):
            wc = _causal_ws(ws_ref, hd)
            cols = slice(hd * CHUNK, (hd + 1) * CHUNK)
            dws = jnp.zeros((CHUNK, CHUNK), F32)
            dsv_sum = jnp.zeros((CHUNK, CHUNK), F32)
            for c in range(tg // CHUNK):
                rows = slice(c * CHUNK, (c + 1) * CHUNK)
                vch = vn[rows, cols]
                sv = _dot(wc, vch) + bt_ref[:, hd:hd + 1]
                dz_ref[rows, cols] = (dgm[rows, cols] * sv * _gelu_grad(zu[rows, cols])).astype(BF16)
                dws += _dot_nt(dsv_b[rows, cols], vch)
                dsv_sum += dsv[rows, cols]
                dvn_s[rows, cols] = _dot_tn(wc, dsv_b[rows, cols])
            dws_ref[hd] += dws
            dsv_acc[:, cols] += dsv_sum
        dvn = dvn_s[...]
        dvh = dvn * gvv
        dv = rv * (dvh - vh * jnp.mean(dvh * vh, axis=-1, keepdims=True))
        dgv_ref[...] += jnp.sum(dvn * vh, axis=0, keepdims=True)
        dz_ref[:, GM_WIDTH:] = (dv * _gelu_grad(zv)).astype(BF16)

        @pl.when(i == ng - 1)
        def _():
            r = lax.broadcasted_iota(jnp.int32, (CHUNK, CHUNK), 0)
            c = lax.broadcasted_iota(jnp.int32, (CHUNK, CHUNK), 1)
            for hd in range(GM_HEADS):
                dws_ref[hd] = jnp.where(r >= c, dws_ref[hd], 0.0)
                dbt_ref[:, hd:hd + 1] = jnp.sum(dsv_acc[:, hd * CHUNK:(hd + 1) * CHUNK], axis=1, keepdims=True)

    return pl.pallas_call(
        body, name="gmlp_bwd", grid=(ng,),
        in_specs=[pl.BlockSpec((tg, GM_WIDTH), lambda i: (i, 0)), pl.BlockSpec((tg, GM_WIDTH), lambda i: (i, 1)),
                  pl.BlockSpec((tg, GM_WIDTH), lambda i: (i, 0)),
                  pl.BlockSpec((1, GM_WIDTH), lambda i: (0, 0)),
                  pl.BlockSpec((GM_HEADS, CHUNK, CHUNK), lambda i: (0, 0, 0)),
                  pl.BlockSpec((CHUNK, GM_HEADS), lambda i: (0, 0))],
        out_specs=[pl.BlockSpec((tg, 2 * GM_WIDTH), lambda i: (i, 0)),
                   pl.BlockSpec((1, GM_WIDTH), lambda i: (0, 0)),
                   pl.BlockSpec((GM_HEADS, CHUNK, CHUNK), lambda i: (0, 0, 0)),
                   pl.BlockSpec((CHUNK, GM_HEADS), lambda i: (0, 0))],
        out_shape=[jax.ShapeDtypeStruct((T, 2 * GM_WIDTH), BF16), jax.ShapeDtypeStruct((1, GM_WIDTH), F32),
                   jax.ShapeDtypeStruct((GM_HEADS, CHUNK, CHUNK), F32),
                   jax.ShapeDtypeStruct((CHUNK, GM_HEADS), F32)],
        scratch_shapes=[pltpu.VMEM((CHUNK, GM_WIDTH), F32), pltpu.VMEM((tg, GM_WIDTH), F32)],
        compiler_params=_params("arbitrary"),
    )(zuv, zuv, d_gm, gv, ws, b_t)


def _scan_matrix(blk, keep):
    r = lax.broadcasted_iota(jnp.int32, (blk, blk), 0)
    c = lax.broadcasted_iota(jnp.int32, (blk, blk), 1)
    return jnp.where(keep(r, c), 1.0, 0.0).astype(BF16)


def _scan_matrix2(blk, keep, value):
    m = _scan_matrix(blk, keep) * value
    return jnp.concatenate([m, m], axis=0)


def _scan(x, mat2):
    hi, lo = _split_bf16(x)
    return _dot(jnp.concatenate([hi, lo], axis=1), mat2)


def _head_masks(q):
    lane = lax.broadcasted_iota(jnp.int32, q.shape, 1)
    m0 = lane < SB_HEAD_DIM
    zero = jnp.zeros_like(q)
    return m0, jnp.where(m0, q, zero), jnp.where(m0, zero, q)


_LOG2E = 1.4426950408889634


def _softplus_parts(z):
    e = jnp.exp2(jnp.abs(z) * (-_LOG2E))
    ope = 1.0 + e
    return e, ope, jnp.maximum(z, 0.0) + jnp.log(ope)


def _attn_fwd(qkv, rider=None):
    T = qkv.shape[0]
    tk = ATTN_KEY_BLOCK
    tq = min(ATTN_Q_ROWS, T)
    band = tq // tk
    assert band % ATTN_UNROLL == 0 or T == tq
    ngrp = SB_WIDTH // LANES

    def body(q_ref, k_ref, v_ref, o_ref, l_ref, acc, run):
        i = pl.program_id(1)
        suffix = _scan_matrix2(tk, lambda r, c: r >= c, -1.0)
        row = lax.broadcasted_iota(jnp.int32, (tq, tk), 0)
        col = lax.broadcasted_iota(jnp.int32, (tq, tk), 1)
        m0, q0, q1 = _head_masks(q_ref[...] * SB_SCALE)
        acc[...] = jnp.zeros_like(acc)
        run[...] = jnp.zeros_like(run)

        def tiles(work):
            heads = (q0, q1)
            kv = []
            for j, _ in work:
                start = pl.multiple_of(j * tk, tk)
                kv.append((k_ref[pl.ds(start, tk), :], v_ref[pl.ds(start, tk), :]))
            z = [[_dot_nt(qh, kj) for qh in heads] for kj, _ in kv]
            sp = [[_softplus_parts(zz)[2] for zz in zt] for zt in z]
            sp = [[s if m is None else jnp.where(m, s, 0.0) for s in st] for st, (_, m) in zip(sp, work)]
            res = [[_scan(s, suffix) for s in st] for st in sp]
            runs = [run[hd] for hd in range(len(heads))]
            a = []
            for t, (_, m) in enumerate(work):
                at = []
                for hd in range(len(heads)):
                    av = jnp.exp(z[t][hd] + (runs[hd] + res[t][hd]))
                    at.append(av if m is None else jnp.where(m, av, 0.0))
                    runs[hd] = runs[hd] + res[t][hd][:, 0:1]
                a.append(at)
            for hd in range(len(heads)):
                run[hd] = runs[hd]
                upd = _dot(a[0][hd].astype(BF16), kv[0][1])
                for t in range(1, len(work)):
                    upd = upd + _dot(a[t][hd].astype(BF16), kv[t][1])
                acc[hd] += upd

        tiles([(i * band + jb, jb * tk + col < row) for jb in reversed(range(band))])

        def full_step(it, carry):
            tiles([(i * band - 1 - ATTN_UNROLL * it - u, None) for u in range(ATTN_UNROLL)])
            return carry

        lax.fori_loop(0, i * (band // ATTN_UNROLL), full_step, 0)
        o_ref[...] = jnp.where(m0, acc[0], acc[1]).astype(BF16)
        l_ref[...] = jnp.where(m0, jnp.broadcast_to(run[0], (tq, LANES)), jnp.broadcast_to(run[1], (tq, LANES)))

    return _pallas(
        body, rider, name="attn_fwd", grid=(ngrp, T // tq),
        in_specs=[pl.BlockSpec((tq, LANES), lambda g, i: (i, g)),
                  pl.BlockSpec((T, LANES), lambda g, i: (0, ngrp + g)),
                  pl.BlockSpec((T, LANES), lambda g, i: (0, 2 * ngrp + g))],
        out_specs=[pl.BlockSpec((tq, LANES), lambda g, i: (i, g)),
                   pl.BlockSpec((tq, LANES), lambda g, i: (i, g))],
        out_shape=[jax.ShapeDtypeStruct((T, SB_WIDTH), BF16), jax.ShapeDtypeStruct((T, SB_WIDTH), F32)],
        scratch_shapes=[pltpu.VMEM((2, tq, LANES), F32), pltpu.VMEM((2, tq, 1), F32)],
        compiler_params=_params("arbitrary", "arbitrary"),
    )(qkv, qkv, qkv)


def _attn_bwd(qkv, d_o, ltot, rider=None):
    T = qkv.shape[0]
    tk = ATTN_KEY_BLOCK
    tq = min(ATTN_BWD_Q_ROWS, T)
    band = tq // tk
    nq = T // tq
    ngrp = SB_WIDTH // LANES

    def body(q_ref, k_ref, v_ref, do_ref, l_ref, dq_ref, dk_ref, dv_ref, dq_acc, dk_acc, dv_acc, lpre, ppre):
        i = pl.program_id(1)

        @pl.when(i == 0)
        def _():
            dk_acc[...] = jnp.zeros_like(dk_acc)
            dv_acc[...] = jnp.zeros_like(dv_acc)

        excl = _scan_matrix(tk, lambda r, c: r < c)
        excl2 = jnp.concatenate([excl, excl], axis=0)
        row = lax.broadcasted_iota(jnp.int32, (tq, tk), 0)
        col = lax.broadcasted_iota(jnp.int32, (tq, tk), 1)
        m0, q0, q1 = _head_masks(q_ref[...] * SB_SCALE)
        _, d0, d1 = _head_masks(do_ref[...].astype(BF16))
        lt = l_ref[...]
        ltots = (lt[:, 0:1], lt[:, SB_HEAD_DIM:SB_HEAD_DIM + 1])
        dq_acc[...] = jnp.zeros_like(dq_acc)
        lpre[...] = jnp.zeros_like(lpre)
        ppre[...] = jnp.zeros_like(ppre)

        def tiles(work):
            heads = ((q0, d0), (q1, d1))
            nhd = len(heads)
            starts = [pl.multiple_of(j * tk, tk) for j, _ in work]
            kv = [(k_ref[pl.ds(st, tk), :], v_ref[pl.ds(st, tk), :]) for st in starts]
            masks = [m for _, m in work]
            every = [(t, hd) for t in range(len(work)) for hd in range(nhd)]
            z = {(t, hd): _dot_nt(heads[hd][0], kv[t][0]) for t, hd in every}
            da = {(t, hd): _dot_nt(heads[hd][1], kv[t][1]) for t, hd in every}
            sp, beta, one_m_beta = {}, {}, {}
            for key in every:
                e, ope, s = _softplus_parts(z[key])
                rinv = 1.0 / ope
                small = e * rinv
                pos = z[key] >= 0.0
                beta[key] = jnp.where(pos, rinv, small)
                one_m_beta[key] = jnp.where(pos, small, rinv)
                sp[key] = s if masks[key[0]] is None else jnp.where(masks[key[0]], s, 0.0)
            res = {key: _scan(sp[key], excl2) for key in every}
            lp = [lpre[hd] for hd in range(nhd)]
            a, p = {}, {}
            for t, hd in every:
                av = jnp.exp(z[t, hd] + ((ltots[hd] + lp[hd]) + res[t, hd]))
                a[t, hd] = av if masks[t] is None else jnp.where(masks[t], av, 0.0)
                p[t, hd] = a[t, hd] * da[t, hd]
                lp[hd] = lp[hd] + (res[t, hd][:, tk - 1:tk] + sp[t, hd][:, tk - 1:tk])
            resp = {key: _dot(p[key].astype(BF16), excl) for key in every}
            pp = [ppre[hd] for hd in range(nhd)]
            dzb = {}
            for t, hd in every:
                dz = p[t, hd] * one_m_beta[t, hd] - beta[t, hd] * (pp[hd] + resp[t, hd])
                if masks[t] is not None:
                    dz = jnp.where(masks[t], dz, 0.0)
                dzb[t, hd] = dz.astype(BF16)
                pp[hd] = pp[hd] + (resp[t, hd][:, tk - 1:tk] + p[t, hd][:, tk - 1:tk])
            for hd in range(nhd):
                lpre[hd] = lp[hd]
                ppre[hd] = pp[hd]
                upd = _dot(dzb[0, hd], kv[0][0])
                for t in range(1, len(work)):
                    upd = upd + _dot(dzb[t, hd], kv[t][0])
                dq_acc[hd] += upd
            for t, st in enumerate(starts):
                dk = _dot_tn(dzb[t, 0], heads[0][0])
                dv = _dot_tn(a[t, 0].astype(BF16), heads[0][1])
                for hd in range(1, nhd):
                    dk = dk + _dot_tn(dzb[t, hd], heads[hd][0])
                    dv = dv + _dot_tn(a[t, hd].astype(BF16), heads[hd][1])
                dk_acc[pl.ds(st, tk), :] += dk
                dv_acc[pl.ds(st, tk), :] += dv

        def full_step(j, carry):
            tiles([(j, None)])
            return carry

        lax.fori_loop(0, i * band, full_step, 0)
        for jb in range(band):
            tiles([(i * band + jb, jb * tk + col < row)])
        dq_ref[...] = (jnp.where(m0, dq_acc[0], dq_acc[1]) * SB_SCALE).astype(BF16)

        @pl.when(i == nq - 1)
        def _():
            dk_ref[...] = dk_acc[...].astype(BF16)
            dv_ref[...] = dv_acc[...].astype(BF16)

    qmap = lambda g, i: (i, g)
    return _pallas(
        body, rider, name="attn_bwd", grid=(ngrp, nq),
        in_specs=[pl.BlockSpec((tq, LANES), qmap),
                  pl.BlockSpec((T, LANES), lambda g, i: (0, ngrp + g)),
                  pl.BlockSpec((T, LANES), lambda g, i: (0, 2 * ngrp + g)),
                  pl.BlockSpec((tq, LANES), qmap), pl.BlockSpec((tq, LANES), qmap)],
        out_specs=[pl.BlockSpec((tq, LANES), qmap),
                   pl.BlockSpec((T, LANES), lambda g, i: (0, g)),
                   pl.BlockSpec((T, LANES), lambda g, i: (0, g))],
        out_shape=[jax.ShapeDtypeStruct((T, SB_WIDTH), BF16)] * 3,
        scratch_shapes=[pltpu.VMEM((2, tq, LANES), F32), pltpu.VMEM((T, LANES), F32),
                        pltpu.VMEM((T, LANES), F32), pltpu.VMEM((2, tq, 1), F32),
                        pltpu.VMEM((2, tq, 1), F32)],
        compiler_params=_params("arbitrary", "arbitrary"),
    )(qkv, qkv, qkv, d_o, ltot)


def _mix_out_fwd(h, gm, sb, w):
    T, D = h.shape
    tm = min(ROW_TILE, T)

    def body(h_ref, gm_ref, sb_ref, w_ref, o_ref):
        o_ref[...] = h_ref[...] + _dot(gm_ref[...], w_ref[:GM_WIDTH, :]) + _dot(sb_ref[...], w_ref[GM_WIDTH:, :])

    row = lambda i: (i, 0)
    return pl.pallas_call(
        body, name="mix_out_fwd", grid=(T // tm,),
        in_specs=[pl.BlockSpec((tm, D), row), pl.BlockSpec((tm, GM_WIDTH), row), pl.BlockSpec((tm, SB_WIDTH), row),
                  pl.BlockSpec((GM_WIDTH + SB_WIDTH, D), lambda i: (0, 0))],
        out_specs=pl.BlockSpec((tm, D), row),
        out_shape=jax.ShapeDtypeStruct((T, D), F32),
        compiler_params=_params("arbitrary"),
    )(h, gm, sb, w)


def _mix_out_bwd(dh, w):
    T, D = dh.shape
    tm = min(ROW_TILE, T)

    def body(dh_ref, w_ref, dgm_ref, dsb_ref, dhb_ref):
        dhb = dh_ref[...].astype(BF16)
        dhb_ref[...] = dhb
        dgm_ref[...] = _dot_nt(dhb, w_ref[:GM_WIDTH, :])
        dsb_ref[...] = _dot_nt(dhb, w_ref[GM_WIDTH:, :])

    row = lambda i: (i, 0)
    return pl.pallas_call(
        body, name="mix_out_bwd", grid=(T // tm,),
        in_specs=[pl.BlockSpec((tm, D), row), pl.BlockSpec((GM_WIDTH + SB_WIDTH, D), lambda i: (0, 0))],
        out_specs=[pl.BlockSpec((tm, GM_WIDTH), row), pl.BlockSpec((tm, SB_WIDTH), row), pl.BlockSpec((tm, D), row)],
        out_shape=[jax.ShapeDtypeStruct((T, GM_WIDTH), F32), jax.ShapeDtypeStruct((T, SB_WIDTH), F32),
                   jax.ShapeDtypeStruct((T, D), BF16)],
        compiler_params=_params("arbitrary"),
    )(dh, w)


def _tail(h3, p, target, g_ple, g_fin, w_gate, w_proj):
    T, D = h3.shape
    PD = p.shape[1]
    tm = min(ROW_TILE, T)

    def body(h_ref, p_ref, t_ref, gp_ref, gf_ref, wg_ref, wp_ref,
             loss_ref, dh_ref, n4_ref, dgl_ref, dpp_ref, dgp_ref, dgf_ref):
        @pl.when(pl.program_id(0) == 0)
        def _():
            loss_ref[...] = jnp.zeros_like(loss_ref)
            dgp_ref[...] = jnp.zeros_like(dgp_ref)
            dgf_ref[...] = jnp.zeros_like(dgf_ref)

        h3v = h_ref[...]
        gp = gp_ref[...]
        gf = gf_ref[...]
        r3, xh3 = _rms_parts(h3v)
        n4 = (xh3 * gp).astype(BF16)
        n4_ref[...] = n4
        gate = _sigmoid(_dot(n4, wg_ref[...]))
        pp = _dot(p_ref[...], wp_ref[...])
        h4 = h3v + gate * pp
        r4, xh4 = _rms_parts(h4)
        err = xh4 * gf - t_ref[...]
        loss_ref[...] += jnp.full(loss_ref.shape, (0.5 / D) * jnp.sum(err * err), F32)
        dy = err * (1.0 / D)
        dgf_ref[...] += jnp.sum(dy * xh4, axis=0, keepdims=True)
        dyg = dy * gf
        dh4 = r4 * (dyg - xh4 * jnp.mean(dyg * xh4, axis=-1, keepdims=True))
        dpp_ref[...] = (dh4 * gate).astype(BF16)
        dgl = (dh4 * pp * gate * (1.0 - gate)).astype(BF16)
        dgl_ref[...] = dgl
        dn4 = _dot_nt(dgl, wg_ref[...])
        dgp_ref[...] += jnp.sum(dn4 * xh3, axis=0, keepdims=True)
        dn4g = dn4 * gp
        dh_ref[...] = dh4 + r3 * (dn4g - xh3 * jnp.mean(dn4g * xh3, axis=-1, keepdims=True))

    row = lambda i: (i, 0)
    one = lambda i: (0, 0)
    return pl.pallas_call(
        body, name="tail", grid=(T // tm,),
        in_specs=[pl.BlockSpec((tm, D), row), pl.BlockSpec((tm, PD), row), pl.BlockSpec((tm, D), row),
                  pl.BlockSpec((1, D), one), pl.BlockSpec((1, D), one),
                  pl.BlockSpec((D, D), one), pl.BlockSpec((PD, D), one)],
        out_specs=[pl.BlockSpec((1, LANES), one), pl.BlockSpec((tm, D), row), pl.BlockSpec((tm, D), row),
                   pl.BlockSpec((tm, D), row), pl.BlockSpec((tm, D), row),
                   pl.BlockSpec((1, D), one), pl.BlockSpec((1, D), one)],
        out_shape=[jax.ShapeDtypeStruct((1, LANES), F32), jax.ShapeDtypeStruct((T, D), F32),
                   jax.ShapeDtypeStruct((T, D), BF16), jax.ShapeDtypeStruct((T, D), BF16),
                   jax.ShapeDtypeStruct((T, D), BF16),
                   jax.ShapeDtypeStruct((1, D), F32), jax.ShapeDtypeStruct((1, D), F32)],
        compiler_params=_params("arbitrary"),
    )(h3, p, target, g_ple, g_fin, w_gate, w_proj)


FFN1_W = ("ffn1_w_in", "ffn1_w_out")
MIX_W = ("w_mix_in", "w_mix_out")
REST_W = ("ffn2_w_in", "ffn2_w_out", "ple_w_gate", "ple_w_proj")
BIG_W = FFN1_W + MIX_W + REST_W
COLUMN_SHARDED = ("w_mix_in", "ple_w_proj")


class _Traffic:
    def __init__(self, shards):
        self.shards = shards
        self.parts = {}

    @staticmethod
    def _full(name, gathered):
        if name in COLUMN_SHARDED:
            return jnp.transpose(gathered, (1, 0, 2)).reshape(gathered.shape[1], -1)
        if name.endswith("_w_in"):
            return gathered
        return gathered.reshape(-1, gathered.shape[-1])

    @staticmethod
    def _blocks(name, grad):
        name = name.split("/")[0]
        if name in COLUMN_SHARDED:
            return jnp.transpose(grad.reshape(grad.shape[0], N_DEV, -1), (1, 0, 2))
        if name.endswith("_w_in"):
            return grad
        return grad.reshape(N_DEV, -1, grad.shape[-1])

    def gather_now(self, names):
        got = _exchange("gather_" + names[0], [self.shards[n] for n in names], [False] * len(names))
        return self.gathered(names, got)

    def gather_rider(self, names):
        return [self.shards[n] for n in names], [False] * len(names)

    def gathered(self, names, got):
        return {n: self._full(n, g) for n, g in zip(names, got)}

    def scatter_rider(self, grads):
        return [self._blocks(n, g) for n, g in grads.items()], [True] * len(grads)

    def scattered(self, names, got):
        self.parts.update(zip(names, got))

    def finish(self, grads, small):
        arrays, flags = self.scatter_rider(grads)
        got = _exchange("scatter_last", arrays + [small], flags + [False])
        self.scattered(list(grads), got[:-1])
        return got[-1]


def _local_step(traffic, x, p_bf, target, g1, gmix, gv, ws, b_t, g2, gple, gfin, pack_small):
    T, D = x.shape
    tm = min(ROW_TILE, T)

    w = traffic.gather_now(FFN1_W)
    h1, n1, G1, U1, a1, *got = _ffn_fwd("ffn1_fwd", x, g1, w["ffn1_w_in"], w["ffn1_w_out"],
                                        rider=traffic.gather_rider(MIX_W))
    w.update(traffic.gathered(MIX_W, got))
    n2, zuv, qkv = _mix_in_fwd(h1, gmix, w["w_mix_in"])
    gm = _gmlp_fwd(zuv, gv, ws, b_t)
    sb, ltot, *got = _attn_fwd(qkv, rider=traffic.gather_rider(REST_W))
    w.update(traffic.gathered(REST_W, got))
    h2 = _mix_out_fwd(h1, gm, sb, w["w_mix_out"])
    h3, n3, G2, U2, a2 = _ffn_fwd("ffn2_fwd", h2, g2, w["ffn2_w_in"], w["ffn2_w_out"])
    loss, dh3, n4, d_gl, d_pp, dg_ple, dg_fin = _tail(h3, p_bf, target, gple, gfin, w["ple_w_gate"], w["ple_w_proj"])

    nb, _, FB = w["ffn1_w_in"].shape
    nh = nb // 2

    def transposed(tag):
        return (jnp.transpose(w[tag + "_w_in"], (0, 2, 1)),
                jnp.transpose(w[tag + "_w_out"].reshape(nh, FB, D), (0, 2, 1)))

    tt = min(GRAD_ROW_TILE, T)

    def dw_out(name, a, d_out):
        return _matmul_tn(name, a, d_out, nh, (1, tt, FB), lambda j, t: (j, t, 0), (tt, D), lambda j, t: (t, 0),
                          (nh, FB, D), (1, FB, D), lambda j, t: (j, 0, 0))

    def dense_tn(name, a, b, ncol):
        ka, nbw = a.shape[1], b.shape[1] // ncol
        return _matmul_tn(name, a, b, ncol, (tt, ka), lambda j, t: (t, 0), (tt, nbw), lambda j, t: (t, j),
                          (ka, b.shape[1]), (ka, nbw), lambda j, t: (0, j))

    grads = dict(ple_w_gate=dense_tn("dw_ple_gate", n4, d_gl, 2), ple_w_proj=dense_tn("dw_ple_proj", p_bf, d_pp, 1))
    dh2, dg2, dG2, dU2, dout2 = _ffn_bwd("ffn2_bwd", dh3, h2, g2, G2, U2, *transposed("ffn2"))
    grads["ffn2_w_in"], = _dw_in("ffn2_dw_in", n3, dG2, dU2, 0, 1)
    grads["ffn2_w_out"] = dw_out("ffn2_dw_out", a2, dout2)
    grads = {n: grads[n] for n in REST_W}

    d_gm, d_sb, dh2_bf = _mix_out_bwd(dh2, w["w_mix_out"])
    dw_mo = jnp.concatenate([dense_tn("dw_mix_out_gm", gm, dh2_bf, 1), dense_tn("dw_mix_out_sb", sb, dh2_bf, 1)], axis=0)
    dzuv, dgv, dws, db_t = _gmlp_bwd(zuv, d_gm, gv, ws, b_t)
    dq, dk, dv, *got = _attn_bwd(qkv, d_sb, ltot, rider=traffic.scatter_rider(grads))
    traffic.scattered(REST_W, got)
    dqkv = jnp.concatenate([dq, dk, dv], axis=1)
    dw_mi = jnp.concatenate([dense_tn("dw_mix_in_uv", n2, dzuv, 2), dense_tn("dw_mix_in_qkv", n2, dqkv, 3)], axis=1)
    dh1, dgmix, dout1 = _mix_in_bwd(dzuv, dqkv, w["w_mix_in"], h1, gmix, dh2)

    grads = dict(w_mix_in=dw_mi, w_mix_out=dw_mo, ffn1_w_out=dw_out("ffn1_dw_out", a1, dout1))
    dx, dg1, dG1, dU1, _, *got = _ffn_bwd("ffn1_bwd", dh1, x, g1, G1, U1, *transposed("ffn1"),
                                         rider=traffic.scatter_rider(grads))
    traffic.scattered(list(grads), got)
    top, = _dw_in("ffn1_dw_in_top", n1, dG1, dU1, 0, 2)
    bottom, *got = _dw_in("ffn1_dw_in_bottom", n1, dG1, dU1, 1, 2, rider=traffic.scatter_rider({"ffn1_w_in/0": top}))
    traffic.scattered(["ffn1_w_in/0"], got)

    small = pack_small(dict(ffn1_norm=dg1, mix_norm=dgmix, gmlp_v_norm=dgv, gmlp_w_s=dws, gmlp_b=jnp.transpose(db_t),
                            ffn2_norm=dg2, ple_norm=dg_ple, final_norm=dg_fin), loss)
    return dx, traffic.finish({"ffn1_w_in/1": bottom}, small)


def _peer(d):
    x, y, c = lax.axis_index("x"), lax.axis_index("y"), lax.axis_index("c")
    px = 1 - x if d & 4 else x
    py = 1 - y if d & 2 else y
    pc = 1 - c if d & 1 else c
    return (px, py, pc), 4 * px + 2 * py + pc


N_CHIPS_AWAY = 3


class _ExchangePlan:
    def __init__(self, ins, outs, send, recv, local, scatter):
        self.ins, self.outs, self.send, self.recv, self.local, self.scatter = ins, outs, send, recv, local, scatter
        self.me = _peer(0)[1]

    def _remote(self, t, sem, src, slot, peer):
        return pltpu.make_async_remote_copy(
            src_ref=src, dst_ref=self.outs[t].at[slot], send_sem=self.send.at[t, sem], recv_sem=self.recv.at[t, sem],
            device_id=peer, device_id_type=MESH)

    def _own(self, t):
        src = self.ins[t].at[self.me] if self.scatter[t] else self.ins[t]
        return pltpu.make_async_copy(src, self.outs[t].at[self.me], self.local.at[t])

    def _n_direct(self, t):
        return N_DEV - 1 if self.scatter[t] else N_CHIPS_AWAY + 1

    def _direct(self, t, k):
        if self.scatter[t]:
            peer, slot = _peer(k + 1)
            return self._remote(t, k, self.ins[t].at[slot], self.me, peer)
        return self._remote(t, k, self.ins[t], self.me, _peer(2 * k if k else 1)[0])

    def _relay(self, t, c):
        slot = _peer(2 * c)[1]
        return self._remote(t, N_CHIPS_AWAY + c, self.outs[t].at[slot], slot, _peer(1)[0])

    def start(self):
        for t in range(len(self.ins)):
            self._own(t).start()
            for k in range(self._n_direct(t)):
                self._direct(t, k).start()

    def finish(self):
        gathers = [t for t in range(len(self.ins)) if not self.scatter[t]]
        for t in gathers:
            for c in range(1, N_CHIPS_AWAY + 1):
                self._direct(t, c).wait_recv()
                self._relay(t, c).start()
        for t in range(len(self.ins)):
            self._own(t).wait()
            for k in range(self._n_direct(t)):
                self._direct(t, k).wait_send()
                if self.scatter[t] or k == 0:
                    self._direct(t, k).wait_recv()
        for t in gathers:
            for c in range(1, N_CHIPS_AWAY + 1):
                self._relay(t, c).wait()


def _exchange_shapes(arrays, scatter):
    return [jax.ShapeDtypeStruct(a.shape if sc else (N_DEV,) + a.shape, a.dtype) for a, sc in zip(arrays, scatter)]


def _exchange_sems(n):
    return [pltpu.SemaphoreType.DMA((n, N_DEV - 1)), pltpu.SemaphoreType.DMA((n, N_DEV - 1)),
            pltpu.SemaphoreType.DMA((n,))]


_ANY = pl.BlockSpec(memory_space=pl.ANY)


def _exchange(name, arrays, scatter):
    n = len(arrays)

    def body(*refs):
        plan = _ExchangePlan(refs[:n], refs[n:2 * n], *refs[2 * n:], scatter)
        plan.start()
        plan.finish()

    return pl.pallas_call(
        body, name=name, in_specs=[_ANY] * n, out_specs=[_ANY] * n, out_shape=_exchange_shapes(arrays, scatter),
        scratch_shapes=_exchange_sems(n),
    )(*arrays)


def _pallas(body, rider, *, name, grid, in_specs, out_specs, out_shape, scratch_shapes=(), compiler_params=None):
    if rider is None:
        return pl.pallas_call(body, name=name, grid=grid, in_specs=in_specs, out_specs=out_specs, out_shape=out_shape,
                              scratch_shapes=list(scratch_shapes), compiler_params=compiler_params)
    arrays, scatter = rider
    n, ni, no, ns = len(arrays), len(in_specs), len(out_specs), len(scratch_shapes)

    def carried(*refs):
        ins, r_in = refs[:ni], refs[ni:ni + n]
        outs, r_out = refs[ni + n:ni + n + no], refs[ni + n + no:ni + 2 * n + no]
        scratch, sems = refs[ni + 2 * n + no:ni + 2 * n + no + ns], refs[ni + 2 * n + no + ns:]
        ids = [pl.program_id(ax) for ax in range(len(grid))]
        first = functools.reduce(jnp.logical_and, [i == 0 for i in ids])
        last = functools.reduce(jnp.logical_and, [i == g - 1 for i, g in zip(ids, grid)])

        @pl.when(first)
        def _():
            _ExchangePlan(r_in, r_out, *sems, scatter).start()

        body(*ins, *outs, *scratch)

        @pl.when(last)
        def _():
            _ExchangePlan(r_in, r_out, *sems, scatter).finish()

    call = pl.pallas_call(
        carried, name=name, grid=grid, in_specs=list(in_specs) + [_ANY] * n, out_specs=list(out_specs) + [_ANY] * n,
        out_shape=list(out_shape) + _exchange_shapes(arrays, scatter),
        scratch_shapes=list(scratch_shapes) + _exchange_sems(n), compiler_params=compiler_params)
    return lambda *args: call(*args, *arrays)


def _adamw_math(g, w, m, v):
    m_new = ADAM_B1 * m + (1.0 - ADAM_B1) * g
    v_new = ADAM_B2 * v + (1.0 - ADAM_B2) * (g * g)
    m_hat = m_new / (1.0 - ADAM_B1 ** ADAM_STEP)
    v_hat = v_new / (1.0 - ADAM_B2 ** ADAM_STEP)
    delta = -ADAM_LR * (m_hat / (jnp.sqrt(v_hat) + ADAM_EPS) + ADAM_WD * w)
    return delta, m_new, v_new


def _adamw(name, parts, w, m, v):
    R, C = w.shape
    tr = R
    for cand in (256, 128, 64, 32, 16, 8):
        if R % cand == 0:
            tr = cand
            break

    def body(p_ref, w_ref, m_ref, v_ref, g_ref, d_ref, nm_ref, nv_ref):
        g = p_ref[0].astype(F32)
        for j in range(1, N_DEV):
            g = g + p_ref[j].astype(F32)
        g_ref[...] = g
        d_ref[...], nm_ref[...], nv_ref[...] = _adamw_math(g, w_ref[...], m_ref[...], v_ref[...])

    row = lambda i: (i, 0)
    spec = pl.BlockSpec((tr, C), row)
    return pl.pallas_call(
        body, name=name, grid=(R // tr,),
        in_specs=[pl.BlockSpec((N_DEV, tr, C), lambda i: (0, i, 0)), spec, spec, spec],
        out_specs=[spec] * 4,
        out_shape=[jax.ShapeDtypeStruct((R, C), F32)] * 4,
        compiler_params=_params("arbitrary"),
    )(parts, w, m, v)


def _rows128(a):
    flat = a.reshape(-1, LANES)
    pad = (-flat.shape[0]) % SMALL_ROWS_ALIGN
    return jnp.pad(flat, ((0, pad), (0, 0))) if pad else flat


def _unrows(packed, like):
    n = like.size // LANES
    return packed[:n].reshape(like.shape)


def kernel(x, p, ffn1_norm, ffn1_w_in, ffn1_w_out, mix_norm, w_mix_in, gmlp_v_norm, gmlp_w_s, gmlp_b, w_mix_out, ffn2_norm, ffn2_w_in, ffn2_w_out, ple_norm, ple_w_gate, ple_w_proj, final_norm, loss_target, m_ffn1_norm, m_ffn1_w_in, m_ffn1_w_out, m_mix_norm, m_w_mix_in, m_gmlp_v_norm, m_gmlp_w_s, m_gmlp_b, m_w_mix_out, m_ffn2_norm, m_ffn2_w_in, m_ffn2_w_out, m_ple_norm, m_ple_w_gate, m_ple_w_proj, m_final_norm, v_ffn1_norm, v_ffn1_w_in, v_ffn1_w_out, v_mix_norm, v_w_mix_in, v_gmlp_v_norm, v_gmlp_w_s, v_gmlp_b, v_w_mix_out, v_ffn2_norm, v_ffn2_w_in, v_ffn2_w_out, v_ple_norm, v_ple_w_gate, v_ple_w_proj, v_final_norm):
    names = ["ffn1_norm", "ffn1_w_in", "ffn1_w_out", "mix_norm", "w_mix_in", "gmlp_v_norm", "gmlp_w_s", "gmlp_b",
             "w_mix_out", "ffn2_norm", "ffn2_w_in", "ffn2_w_out", "ple_norm", "ple_w_gate", "ple_w_proj", "final_norm"]
    W = dict(zip(names, [ffn1_norm, ffn1_w_in, ffn1_w_out, mix_norm, w_mix_in, gmlp_v_norm, gmlp_w_s, gmlp_b,
                         w_mix_out, ffn2_norm, ffn2_w_in, ffn2_w_out, ple_norm, ple_w_gate, ple_w_proj, final_norm]))
    M = dict(zip(names, [m_ffn1_norm, m_ffn1_w_in, m_ffn1_w_out, m_mix_norm, m_w_mix_in, m_gmlp_v_norm, m_gmlp_w_s,
                         m_gmlp_b, m_w_mix_out, m_ffn2_norm, m_ffn2_w_in, m_ffn2_w_out, m_ple_norm, m_ple_w_gate,
                         m_ple_w_proj, m_final_norm]))
    V = dict(zip(names, [v_ffn1_norm, v_ffn1_w_in, v_ffn1_w_out, v_mix_norm, v_w_mix_in, v_gmlp_v_norm, v_gmlp_w_s,
                         v_gmlp_b, v_w_mix_out, v_ffn2_norm, v_ffn2_w_in, v_ffn2_w_out, v_ple_norm, v_ple_w_gate,
                         v_ple_w_proj, v_final_norm]))
    small = [n for n in names if n not in BIG_W]
    D = x.shape[-1]

    def pack(src, last):
        return jnp.concatenate([_rows128(src[n]) for n in small] + [last], axis=0)

    offs = [0]
    for n in small:
        offs.append(offs[-1] + _rows128(W[n]).shape[0])

    traffic = _Traffic({n: W[n][0].astype(BF16) for n in BIG_W})
    dx, small_parts = _local_step(
        traffic, x[0], p[0, 0].astype(BF16), loss_target[0],
        W["ffn1_norm"], W["mix_norm"], W["gmlp_v_norm"], W["gmlp_w_s"][0], jnp.transpose(W["gmlp_b"][0]),
        W["ffn2_norm"], W["ple_norm"], W["final_norm"].reshape(1, D),
        lambda grads, loss_part: pack(grads, jnp.broadcast_to(loss_part, (SMALL_ROWS_ALIGN, LANES))))

    parts = traffic.parts
    parts["ffn1_w_in"] = jnp.concatenate([parts["ffn1_w_in/0"], parts["ffn1_w_in/1"]], axis=1)
    out = {}
    for n in BIG_W:
        out[n] = _adamw("adamw_" + n, parts[n], W[n][0], M[n][0], V[n][0])
    zeros = jnp.zeros((SMALL_ROWS_ALIGN, LANES), F32)
    sg, sd, sm, sv = _adamw("adamw_small", small_parts, pack(W, zeros), pack(M, zeros), pack(V, zeros))
    for k, n in enumerate(small):
        out[n] = tuple(_unrows(arr[offs[k]:offs[k + 1]], W[n]) for arr in (sg, sd, sm, sv))
    loss = sg[offs[len(small)], 0]

    res = [loss, dx[None]]
    for k in range(4):
        res += [out[n][k].reshape(W[n].shape) for n in names]
    return tuple(res)
```

```python
import functools

import jax
import jax.numpy as jnp
from jax import lax
from jax.experimental import pallas as pl
from jax.experimental.pallas import tpu as pltpu

F32 = jnp.float32
BF16 = jnp.bfloat16
MESH = pl.DeviceIdType.MESH

N_DEV = 8
EPS = 1e-6
ADAM_LR = 0.001
ADAM_B1 = 0.9
ADAM_B2 = 0.999
ADAM_EPS = 1e-08
ADAM_WD = 0.01
ADAM_STEP = 10

GM_WIDTH = 512
GM_HEADS = 4
CHUNK = 128
SB_WIDTH = 512
SB_HEAD_DIM = 64
SB_SCALE = 0.125
LANES = 128
SMALL_ROWS_ALIGN = 8

ROW_TILE = 512
GRAD_ROW_TILE = 2048
DW_IN_ROW_TILE = 4096
FFN_FWD_ROW_TILE = 1024
ATTN_Q_ROWS = 512
ATTN_BWD_Q_ROWS = 512
ATTN_KEY_BLOCK = 256
ATTN_UNROLL = 2
VMEM_LIMIT = 56 * 1024 * 1024


def _params(*sem):
    return pltpu.CompilerParams(dimension_semantics=sem, vmem_limit_bytes=VMEM_LIMIT)


def _dot(a, b):
    return jnp.dot(a, b, preferred_element_type=F32)


def _dot_nt(a, b):
    return lax.dot_general(a, b, (((1,), (1,)), ((), ())), preferred_element_type=F32)


def _dot_tn(a, b):
    return lax.dot_general(a, b, (((0,), (0,)), ((), ())), preferred_element_type=F32)


def _rms_parts(x):
    r = lax.rsqrt(jnp.mean(x * x, axis=-1, keepdims=True) + EPS)
    return r, x * r


def _rms_bwd(x, g, dy):
    r, xh = _rms_parts(x)
    dyg = dy * g
    dx = r * (dyg - xh * jnp.mean(dyg * xh, axis=-1, keepdims=True))
    return dx, jnp.sum(dy * xh, axis=0, keepdims=True)


def _sigmoid(x):
    return 1.0 / (1.0 + jnp.exp(-x))


_SQRT_HALF = 0.7071067811865476
_INV_SQRT_2PI = 0.3989422804014327


def _gelu(x):
    return 0.5 * x * (1.0 + lax.erf(x * _SQRT_HALF))


def _gelu_grad(x):
    return 0.5 * (1.0 + lax.erf(x * _SQRT_HALF)) + x * (_INV_SQRT_2PI * jnp.exp(-0.5 * x * x))


def _split_bf16(x):
    hi = x.astype(BF16)
    lo = (x - hi.astype(F32)).astype(BF16)
    return hi, lo


def _ffn_fwd(name, h, gain, w_in, w_out, rider=None):
    T, D = h.shape
    nb, _, FB = w_in.shape
    nh = nb // 2
    tm = min(FFN_FWD_ROW_TILE, T)

    def body(h_ref, g_ref, wg_ref, wu_ref, wo_ref, ho_ref, n_ref, G_ref, U_ref, a_ref, n_s, acc):
        jj = pl.program_id(1)

        @pl.when(jj == 0)
        def _():
            _, xh = _rms_parts(h_ref[...])
            n = (xh * g_ref[...]).astype(BF16)
            n_s[...] = n
            n_ref[...] = n
            acc[...] = jnp.zeros_like(acc)

        n = n_s[...]
        G = _dot(n, wg_ref[0])
        U = _dot(n, wu_ref[0])
        G_ref[0] = G.astype(BF16)
        U_ref[0] = U.astype(BF16)
        a = (G * _sigmoid(G) * U).astype(BF16)
        a_ref[0] = a
        acc[...] += _dot(a, wo_ref[...])

        @pl.when(jj == nh - 1)
        def _():
            ho_ref[...] = h_ref[...] + 0.5 * acc[...]

    row = lambda i, j: (i, 0)
    blk = lambda i, j: (j, i, 0)
    return _pallas(
        body, rider, name=name, grid=(T // tm, nh),
        in_specs=[pl.BlockSpec((tm, D), row),
                  pl.BlockSpec((1, D), lambda i, j: (0, 0)),
                  pl.BlockSpec((1, D, FB), lambda i, j: (j, 0, 0)),
                  pl.BlockSpec((1, D, FB), lambda i, j: (j + nh, 0, 0)),
                  pl.BlockSpec((FB, D), lambda i, j: (j, 0))],
        out_specs=[pl.BlockSpec((tm, D), row), pl.BlockSpec((tm, D), row),
                   pl.BlockSpec((1, tm, FB), blk), pl.BlockSpec((1, tm, FB), blk),
                   pl.BlockSpec((1, tm, FB), blk)],
        out_shape=[jax.ShapeDtypeStruct((T, D), F32), jax.ShapeDtypeStruct((T, D), BF16),
                   jax.ShapeDtypeStruct((nh, T, FB), BF16), jax.ShapeDtypeStruct((nh, T, FB), BF16),
                   jax.ShapeDtypeStruct((nh, T, FB), BF16)],
        scratch_shapes=[pltpu.VMEM((tm, D), BF16), pltpu.VMEM((tm, D), F32)],
        compiler_params=_params("arbitrary", "arbitrary"),
    )(h, gain, w_in, w_in, w_out)


def _ffn_bwd(name, dh, h_in, gain, G, U, w_in, w_out, rider=None):
    T, D = dh.shape
    nb, _, FB = w_in.shape
    nh = nb // 2
    tm = min(ROW_TILE, T)

    def body(dh_ref, h_ref, g_ref, G_ref, U_ref, wg_ref, wu_ref, wo_ref,
             dhin_ref, dg_ref, dG_ref, dU_ref, do_ref, dn_acc, do_s):
        i = pl.program_id(0)
        jj = pl.program_id(1)

        @pl.when(jj == 0)
        def _():
            d_out = (0.5 * dh_ref[...]).astype(BF16)
            do_s[...] = d_out
            do_ref[...] = d_out
            dn_acc[...] = jnp.zeros_like(dn_acc)

        @pl.when((i == 0) & (jj == 0))
        def _():
            dg_ref[...] = jnp.zeros_like(dg_ref)

        halves = [slice(0, tm // 2), slice(tm // 2, tm)]
        da = [_dot_nt(do_s[rows, :], wo_ref[...]) for rows in halves]
        dGU = []
        for rows, dav in zip(halves, da):
            Gv = G_ref[0, rows, :].astype(F32)
            Uv = U_ref[0, rows, :].astype(F32)
            sig = _sigmoid(Gv)
            dU = (dav * (Gv * sig)).astype(BF16)
            dG = (dav * Uv * (sig * (1.0 + Gv * (1.0 - sig)))).astype(BF16)
            dG_ref[0, rows, :] = dG
            dU_ref[0, rows, :] = dU
            dGU.append((dG, dU))
        dn = [_dot_nt(dG, wg_ref[0]) for dG, _ in dGU]
        dn = [d + _dot_nt(dU, wu_ref[0]) for d, (_, dU) in zip(dn, dGU)]
        for rows, d in zip(halves, dn):
            dn_acc[rows, :] += d

        @pl.when(jj == nh - 1)
        def _():
            dx, dg = _rms_bwd(h_ref[...], g_ref[...], dn_acc[...])
            dhin_ref[...] = dh_ref[...] + dx
            dg_ref[...] += dg

    row = lambda i, j: (i, 0)
    blk = lambda i, j: (j, i, 0)
    one = lambda i, j: (0, 0)
    return _pallas(
        body, rider, name=name, grid=(T // tm, nh),
        in_specs=[pl.BlockSpec((tm, D), row), pl.BlockSpec((tm, D), row), pl.BlockSpec((1, D), one),
                  pl.BlockSpec((1, tm, FB), blk), pl.BlockSpec((1, tm, FB), blk),
                  pl.BlockSpec((1, D, FB), lambda i, j: (j, 0, 0)),
                  pl.BlockSpec((1, D, FB), lambda i, j: (j + nh, 0, 0)),
                  pl.BlockSpec((FB, D), lambda i, j: (j, 0))],
        out_specs=[pl.BlockSpec((tm, D), row), pl.BlockSpec((1, D), one),
                   pl.BlockSpec((1, tm, FB), blk), pl.BlockSpec((1, tm, FB), blk),
                   pl.BlockSpec((tm, D), row)],
        out_shape=[jax.ShapeDtypeStruct((T, D), F32), jax.ShapeDtypeStruct((1, D), F32),
                   jax.ShapeDtypeStruct((nh, T, FB), BF16), jax.ShapeDtypeStruct((nh, T, FB), BF16),
                   jax.ShapeDtypeStruct((T, D), BF16)],
        scratch_shapes=[pltpu.VMEM((tm, D), F32), pltpu.VMEM((tm, D), BF16)],
        compiler_params=_params("arbitrary", "arbitrary"),
    )(dh, h_in, gain, G, U, w_in, w_in, w_out)


def _matmul_tn(name, a, b, nj, a_block, a_map, b_block, b_map, out_shape, out_block, out_map):
    T = a.shape[-2]
    tt = a_block[-2]
    nt = T // tt
    kb, nbk = out_block[-2], out_block[-1]

    def body(a_ref, b_ref, o_ref, acc):
        t = pl.program_id(1)

        @pl.when(t == 0)
        def _():
            acc[...] = jnp.zeros_like(acc)

        av = a_ref[0] if len(a_block) == 3 else a_ref[...]
        bv = b_ref[0] if len(b_block) == 3 else b_ref[...]
        acc[...] += _dot_tn(av, bv)

        @pl.when(t == nt - 1)
        def _():
            if len(out_block) == 3:
                o_ref[0] = acc[...].astype(o_ref.dtype)
            else:
                o_ref[...] = acc[...].astype(o_ref.dtype)

    return pl.pallas_call(
        body, name=name, grid=(nj, nt),
        in_specs=[pl.BlockSpec(a_block, a_map), pl.BlockSpec(b_block, b_map)],
        out_specs=pl.BlockSpec(out_block, out_map),
        out_shape=jax.ShapeDtypeStruct(out_shape, BF16),
        scratch_shapes=[pltpu.VMEM((kb, nbk), F32)],
        compiler_params=_params("arbitrary", "arbitrary"),
    )(a, b)


def _dw_in(name, n, dG, dU, part, nparts, rider=None):
    T, D = n.shape
    nh, _, FB = dG.shape
    kr = D // nparts
    tt = min(DW_IN_ROW_TILE, T)
    nt = T // tt

    def body(n_ref, dg_ref, du_ref, o_ref, acc):
        j = pl.program_id(0)
        t = pl.program_id(1)

        @pl.when(t == 0)
        def _():
            acc[...] = jnp.zeros_like(acc)

        def add(dz_ref):
            for rows in (slice(0, kr // 2), slice(kr // 2, kr)):
                acc[rows, :] += _dot_tn(n_ref[:, rows], dz_ref[0])

        @pl.when(j < nh)
        def _():
            add(dg_ref)

        @pl.when(j >= nh)
        def _():
            add(du_ref)

        @pl.when(t == nt - 1)
        def _():
            o_ref[0] = acc[...].astype(BF16)

    return _pallas(
        body, rider, name=name, grid=(2 * nh, nt),
        in_specs=[pl.BlockSpec((tt, kr), lambda j, t: (t, part)),
                  pl.BlockSpec((1, tt, FB), lambda j, t: (jnp.minimum(j, nh - 1), t, 0)),
                  pl.BlockSpec((1, tt, FB), lambda j, t: (jnp.maximum(j - nh, 0), t, 0))],
        out_specs=[pl.BlockSpec((1, kr, FB), lambda j, t: (j, 0, 0))],
        out_shape=[jax.ShapeDtypeStruct((2 * nh, kr, FB), BF16)],
        scratch_shapes=[pltpu.VMEM((kr, FB), F32)],
        compiler_params=_params("arbitrary", "arbitrary"),
    )(n, dG, dU)


def _mix_in_fwd(h, gain, w):
    T, D = h.shape
    W = w.shape[1]
    nuv = 2 * GM_WIDTH
    tm = min(ROW_TILE, T)

    def body(h_ref, g_ref, w_ref, n_ref, zuv_ref, qkv_ref):
        _, xh = _rms_parts(h_ref[...])
        n = (xh * g_ref[...]).astype(BF16)
        n_ref[...] = n
        z = _dot(n, w_ref[...])
        zuv_ref[...] = z[:, :nuv]
        qkv_ref[...] = z[:, nuv:].astype(BF16)

    row = lambda i: (i, 0)
    return pl.pallas_call(
        body, name="mix_in_fwd", grid=(T // tm,),
        in_specs=[pl.BlockSpec((tm, D), row), pl.BlockSpec((1, D), lambda i: (0, 0)),
                  pl.BlockSpec((D, W), lambda i: (0, 0))],
        out_specs=[pl.BlockSpec((tm, D), row), pl.BlockSpec((tm, nuv), row),
                   pl.BlockSpec((tm, W - nuv), row)],
        out_shape=[jax.ShapeDtypeStruct((T, D), BF16), jax.ShapeDtypeStruct((T, nuv), F32),
                   jax.ShapeDtypeStruct((T, W - nuv), BF16)],
        compiler_params=_params("arbitrary"),
    )(h, gain, w)


def _mix_in_bwd(dzuv, dqkv, w, h, gain, dh):
    T, D = h.shape
    W = w.shape[1]
    nuv = dzuv.shape[1]
    tm = min(ROW_TILE, T)

    def body(dzuv_ref, dqkv_ref, w_ref, h_ref, g_ref, dh_ref, dhin_ref, dg_ref, half_ref):
        @pl.when(pl.program_id(0) == 0)
        def _():
            dg_ref[...] = jnp.zeros_like(dg_ref)

        dn = _dot_nt(dzuv_ref[...], w_ref[:, :nuv]) + _dot_nt(dqkv_ref[...], w_ref[:, nuv:])
        dx, dg = _rms_bwd(h_ref[...], g_ref[...], dn)
        dh_in = dh_ref[...] + dx
        dhin_ref[...] = dh_in
        half_ref[...] = (0.5 * dh_in).astype(BF16)
        dg_ref[...] += dg

    row = lambda i: (i, 0)
    one = lambda i: (0, 0)
    return pl.pallas_call(
        body, name="mix_in_bwd", grid=(T // tm,),
        in_specs=[pl.BlockSpec((tm, nuv), row), pl.BlockSpec((tm, W - nuv), row),
                  pl.BlockSpec((D, W), one), pl.BlockSpec((tm, D), row), pl.BlockSpec((1, D), one),
                  pl.BlockSpec((tm, D), row)],
        out_specs=[pl.BlockSpec((tm, D), row), pl.BlockSpec((1, D), one), pl.BlockSpec((tm, D), row)],
        out_shape=[jax.ShapeDtypeStruct((T, D), F32), jax.ShapeDtypeStruct((1, D), F32),
                   jax.ShapeDtypeStruct((T, D), BF16)],
        compiler_params=_params("arbitrary"),
    )(dzuv, dqkv, w, h, gain, dh)


def _gmlp_norm(zv, gv):
    v = _gelu(zv)
    r, vh = _rms_parts(v)
    return r, vh, (vh * gv).astype(BF16)


def _causal_ws(ws_ref, hd):
    r = lax.broadcasted_iota(jnp.int32, (CHUNK, CHUNK), 0)
    c = lax.broadcasted_iota(jnp.int32, (CHUNK, CHUNK), 1)
    return jnp.where(r >= c, ws_ref[hd], 0.0).astype(BF16)


def _gmlp_fwd(zuv, gv, ws, b_t):
    T = zuv.shape[0]
    tg = min(ROW_TILE, T)

    def body(zu_ref, zv_ref, gv_ref, ws_ref, bt_ref, o_ref):
        u = _gelu(zu_ref[...])
        _, _, vn = _gmlp_norm(zv_ref[...], gv_ref[...])
        for hd in range(GM_HEADS):
            wc = _causal_ws(ws_ref, hd)
            cols = slice(hd * CHUNK, (hd + 1) * CHUNK)
            for c in range(tg // CHUNK):
                rows = slice(c * CHUNK, (c + 1) * CHUNK)
                sv = _dot(wc, vn[rows, cols]) + bt_ref[:, hd:hd + 1]
                o_ref[rows, cols] = (u[rows, cols] * sv).astype(BF16)

    return pl.pallas_call(
        body, name="gmlp_fwd", grid=(T // tg,),
        in_specs=[pl.BlockSpec((tg, GM_WIDTH), lambda i: (i, 0)), pl.BlockSpec((tg, GM_WIDTH), lambda i: (i, 1)),
                  pl.BlockSpec((1, GM_WIDTH), lambda i: (0, 0)),
                  pl.BlockSpec((GM_HEADS, CHUNK, CHUNK), lambda i: (0, 0, 0)),
                  pl.BlockSpec((CHUNK, GM_HEADS), lambda i: (0, 0))],
        out_specs=pl.BlockSpec((tg, GM_WIDTH), lambda i: (i, 0)),
        out_shape=jax.ShapeDtypeStruct((T, GM_WIDTH), BF16),
        compiler_params=_params("arbitrary"),
    )(zuv, zuv, gv, ws, b_t)


def _gmlp_bwd(zuv, d_gm, gv, ws, b_t):
    T = zuv.shape[0]
    tg = min(ROW_TILE, T)
    ng = T // tg

    def body(zu_ref, zv_ref, dgm_ref, gv_ref, ws_ref, bt_ref, dz_ref, dgv_ref, dws_ref, dbt_ref, dsv_acc, dvn_s):
        i = pl.program_id(0)

        @pl.when(i == 0)
        def _():
            dgv_ref[...] = jnp.zeros_like(dgv_ref)
            dws_ref[...] = jnp.zeros_like(dws_ref)
            dsv_acc[...] = jnp.zeros_like(dsv_acc)

        zu = zu_ref[...]
        zv = zv_ref[...]
        dgm = dgm_ref[...]
        gvv = gv_ref[...]
        u = _gelu(zu)
        rv, vh, vn = _gmlp_norm(zv, gvv)
        dsv = dgm * u
        dsv_b = dsv.astype(BF16)
        for hd in range(GM_HEADS):
            wc = _causal_ws(ws_ref, hd)
            cols = slice(hd * CHUNK, (hd + 1) * CHUNK)
            dws = jnp.zeros((CHUNK, CHUNK), F32)
            dsv_sum = jnp.zeros((CHUNK, CHUNK), F32)
            for c in range(tg // CHUNK):
                rows = slice(c * CHUNK, (c + 1) * CHUNK)
                vch = vn[rows, cols]
                sv = _dot(wc, vch) + bt_ref[:, hd:hd + 1]
                dz_ref[rows, cols] = (dgm[rows, cols] * sv * _gelu_grad(zu[rows, cols])).astype(BF16)
                dws += _dot_nt(dsv_b[rows, cols], vch)
                dsv_sum += dsv[rows, cols]
                dvn_s[rows, cols] = _dot_tn(wc, dsv_b[rows, cols])
            dws_ref[hd] += dws
            dsv_acc[:, cols] += dsv_sum
        dvn = dvn_s[...]
        dvh = dvn * gvv
        dv = rv * (dvh - vh * jnp.mean(dvh * vh, axis=-1, keepdims=True))
        dgv_ref[...] += jnp.sum(dvn * vh, axis=0, keepdims=True)
        dz_ref[:, GM_WIDTH:] = (dv * _gelu_grad(zv)).astype(BF16)

        @pl.when(i == ng - 1)
        def _():
            r = lax.broadcasted_iota(jnp.int32, (CHUNK, CHUNK), 0)
            c = lax.broadcasted_iota(jnp.int32, (CHUNK, CHUNK), 1)
            for hd in range(GM_HEADS):
                dws_ref[hd] = jnp.where(r >= c, dws_ref[hd], 0.0)
                dbt_ref[:, hd:hd + 1] = jnp.sum(dsv_acc[:, hd * CHUNK:(hd + 1) * CHUNK], axis=1, keepdims=True)

    return pl.pallas_call(
        body, name="gmlp_bwd", grid=(ng,),
        in_specs=[pl.BlockSpec((tg, GM_WIDTH), lambda i: (i, 0)), pl.BlockSpec((tg, GM_WIDTH), lambda i: (i, 1)),
                  pl.BlockSpec((tg, GM_WIDTH), lambda i: (i, 0)),
                  pl.BlockSpec((1, GM_WIDTH), lambda i: (0, 0)),
                  pl.BlockSpec((GM_HEADS, CHUNK, CHUNK), lambda i: (0, 0, 0)),
                  pl.BlockSpec((CHUNK, GM_HEADS), lambda i: (0, 0))],
        out_specs=[pl.BlockSpec((tg, 2 * GM_WIDTH), lambda i: (i, 0)),
                   pl.BlockSpec((1, GM_WIDTH), lambda i: (0, 0)),
                   pl.BlockSpec((GM_HEADS, CHUNK, CHUNK), lambda i: (0, 0, 0)),
                   pl.BlockSpec((CHUNK, GM_HEADS), lambda i: (0, 0))],
        out_shape=[jax.ShapeDtypeStruct((T, 2 * GM_WIDTH), BF16), jax.ShapeDtypeStruct((1, GM_WIDTH), F32),
                   jax.ShapeDtypeStruct((GM_HEADS, CHUNK, CHUNK), F32),
                   jax.ShapeDtypeStruct((CHUNK, GM_HEADS), F32)],
        scratch_shapes=[pltpu.VMEM((CHUNK, GM_WIDTH), F32), pltpu.VMEM((tg, GM_WIDTH), F32)],
        compiler_params=_params("arbitrary"),
    )(zuv, zuv, d_gm, gv, ws, b_t)


def _scan_matrix(blk, keep):
    r = lax.broadcasted_iota(jnp.int32, (blk, blk), 0)
    c = lax.broadcasted_iota(jnp.int32, (blk, blk), 1)
    return jnp.where(keep(r, c), 1.0, 0.0).astype(BF16)


def _scan_matrix2(blk, keep, value):
    m = _scan_matrix(blk, keep) * value
    return jnp.concatenate([m, m], axis=0)


def _scan(x, mat2):
    hi, lo = _split_bf16(x)
    return _dot(jnp.concatenate([hi, lo], axis=1), mat2)


def _head_masks(q):
    lane = lax.broadcasted_iota(jnp.int32, q.shape, 1)
    m0 = lane < SB_HEAD_DIM
    zero = jnp.zeros_like(q)
    return m0, jnp.where(m0, q, zero), jnp.where(m0, zero, q)


_LOG2E = 1.4426950408889634


def _softplus_parts(z):
    e = jnp.exp2(jnp.abs(z) * (-_LOG2E))
    ope = 1.0 + e
    return e, ope, jnp.maximum(z, 0.0) + jnp.log(ope)


def _attn_fwd(qkv, rider=None):
    T = qkv.shape[0]
    tk = ATTN_KEY_BLOCK
    tq = min(ATTN_Q_ROWS, T)
    band = tq // tk
    assert band % ATTN_UNROLL == 0 or T == tq
    ngrp = SB_WIDTH // LANES

    def body(q_ref, k_ref, v_ref, o_ref, l_ref, acc, run):
        i = pl.program_id(1)
        suffix = _scan_matrix2(tk, lambda r, c: r >= c, -1.0)
        row = lax.broadcasted_iota(jnp.int32, (tq, tk), 0)
        col = lax.broadcasted_iota(jnp.int32, (tq, tk), 1)
        m0, q0, q1 = _head_masks(q_ref[...] * SB_SCALE)
        acc[...] = jnp.zeros_like(acc)
        run[...] = jnp.zeros_like(run)

        def tiles(work):
            heads = (q0, q1)
            kv = []
            for j, _ in work:
                start = pl.multiple_of(j * tk, tk)
                kv.append((k_ref[pl.ds(start, tk), :], v_ref[pl.ds(start, tk), :]))
            z = [[_dot_nt(qh, kj) for qh in heads] for kj, _ in kv]
            sp = [[_softplus_parts(zz)[2] for zz in zt] for zt in z]
            sp = [[s if m is None else jnp.where(m, s, 0.0) for s in st] for st, (_, m) in zip(sp, work)]
            res = [[_scan(s, suffix) for s in st] for st in sp]
            runs = [run[hd] for hd in range(len(heads))]
            a = []
            for t, (_, m) in enumerate(work):
                at = []
                for hd in range(len(heads)):
                    av = jnp.exp(z[t][hd] + (runs[hd] + res[t][hd]))
                    at.append(av if m is None else jnp.where(m, av, 0.0))
                    runs[hd] = runs[hd] + res[t][hd][:, 0:1]
                a.append(at)
            for hd in range(len(heads)):
                run[hd] = runs[hd]
                upd = _dot(a[0][hd].astype(BF16), kv[0][1])
                for t in range(1, len(work)):
                    upd = upd + _dot(a[t][hd].astype(BF16), kv[t][1])
                acc[hd] += upd

        tiles([(i * band + jb, jb * tk + col < row) for jb in reversed(range(band))])

        def full_step(it, carry):
            tiles([(i * band - 1 - ATTN_UNROLL * it - u, None) for u in range(ATTN_UNROLL)])
            return carry

        lax.fori_loop(0, i * (band // ATTN_UNROLL), full_step, 0)
        o_ref[...] = jnp.where(m0, acc[0], acc[1]).astype(BF16)
        l_ref[...] = jnp.where(m0, jnp.broadcast_to(run[0], (tq, LANES)), jnp.broadcast_to(run[1], (tq, LANES)))

    return _pallas(
        body, rider, name="attn_fwd", grid=(ngrp, T // tq),
        in_specs=[pl.BlockSpec((tq, LANES), lambda g, i: (i, g)),
                  pl.BlockSpec((T, LANES), lambda g, i: (0, ngrp + g)),
                  pl.BlockSpec((T, LANES), lambda g, i: (0, 2 * ngrp + g))],
        out_specs=[pl.BlockSpec((tq, LANES), lambda g, i: (i, g)),
                   pl.BlockSpec((tq, LANES), lambda g, i: (i, g))],
        out_shape=[jax.ShapeDtypeStruct((T, SB_WIDTH), BF16), jax.ShapeDtypeStruct((T, SB_WIDTH), F32)],
        scratch_shapes=[pltpu.VMEM((2, tq, LANES), F32), pltpu.VMEM((2, tq, 1), F32)],
        compiler_params=_params("arbitrary", "arbitrary"),
    )(qkv, qkv, qkv)


def _attn_bwd(qkv, d_o, ltot, rider=None):
    T = qkv.shape[0]
    tk = ATTN_KEY_BLOCK
    tq = min(ATTN_BWD_Q_ROWS, T)
    band = tq // tk
    nq = T // tq
    ngrp = SB_WIDTH // LANES

    def body(q_ref, k_ref, v_ref, do_ref, l_ref, dq_ref, dk_ref, dv_ref, dq_acc, dk_acc, dv_acc, lpre, ppre):
        i = pl.program_id(1)

        @pl.when(i == 0)
        def _():
            dk_acc[...] = jnp.zeros_like(dk_acc)
            dv_acc[...] = jnp.zeros_like(dv_acc)

        excl = _scan_matrix(tk, lambda r, c: r < c)
        excl2 = jnp.concatenate([excl, excl], axis=0)
        row = lax.broadcasted_iota(jnp.int32, (tq, tk), 0)
        col = lax.broadcasted_iota(jnp.int32, (tq, tk), 1)
        m0, q0, q1 = _head_masks(q_ref[...] * SB_SCALE)
        _, d0, d1 = _head_masks(do_ref[...].astype(BF16))
        lt = l_ref[...]
        ltots = (lt[:, 0:1], lt[:, SB_HEAD_DIM:SB_HEAD_DIM + 1])
        dq_acc[...] = jnp.zeros_like(dq_acc)
        lpre[...] = jnp.zeros_like(lpre)
        ppre[...] = jnp.zeros_like(ppre)

        def tiles(work):
            heads = ((q0, d0), (q1, d1))
            nhd = len(heads)
            starts = [pl.multiple_of(j * tk, tk) for j, _ in work]
            kv = [(k_ref[pl.ds(st, tk), :], v_ref[pl.ds(st, tk), :]) for st in starts]
            masks = [m for _, m in work]
            every = [(t, hd) for t in range(len(work)) for hd in range(nhd)]
            z = {(t, hd): _dot_nt(heads[hd][0], kv[t][0]) for t, hd in every}
            da = {(t, hd): _dot_nt(heads[hd][1], kv[t][1]) for t, hd in every}
            sp, beta = {}, {}
            for key in every:
                s = _softplus_parts(z[key])[2]
                beta[key] = jnp.exp(z[key] - s)
                sp[key] = s if masks[key[0]] is None else jnp.where(masks[key[0]], s, 0.0)
            res = {key: _scan(sp[key], excl2) for key in every}
            lp = [lpre[hd] for hd in range(nhd)]
            a, p = {}, {}
            for t, hd in every:
                av = jnp.exp(z[t, hd] + ((ltots[hd] + lp[hd]) + res[t, hd]))
                a[t, hd] = av if masks[t] is None else jnp.where(masks[t], av, 0.0)
                p[t, hd] = a[t, hd] * da[t, hd]
                lp[hd] = lp[hd] + (res[t, hd][:, tk - 1:tk] + sp[t, hd][:, tk - 1:tk])
            resp = {key: _dot(p[key].astype(BF16), excl) for key in every}
            pp = [ppre[hd] for hd in range(nhd)]
            dzb = {}
            for t, hd in every:
                dz = p[t, hd] - beta[t, hd] * (p[t, hd] + (pp[hd] + resp[t, hd]))
                if masks[t] is not None:
                    dz = jnp.where(masks[t], dz, 0.0)
                dzb[t, hd] = dz.astype(BF16)
                pp[hd] = pp[hd] + (resp[t, hd][:, tk - 1:tk] + p[t, hd][:, tk - 1:tk])
            for hd in range(nhd):
                lpre[hd] = lp[hd]
                ppre[hd] = pp[hd]
                upd = _dot(dzb[0, hd], kv[0][0])
                for t in range(1, len(work)):
                    upd = upd + _dot(dzb[t, hd], kv[t][0])
                dq_acc[hd] += upd
            for t, st in enumerate(starts):
                dk = _dot_tn(dzb[t, 0], heads[0][0])
                dv = _dot_tn(a[t, 0].astype(BF16), heads[0][1])
                for hd in range(1, nhd):
                    dk = dk + _dot_tn(dzb[t, hd], heads[hd][0])
                    dv = dv + _dot_tn(a[t, hd].astype(BF16), heads[hd][1])
                dk_acc[pl.ds(st, tk), :] += dk
                dv_acc[pl.ds(st, tk), :] += dv

        def full_step(j, carry):
            tiles([(j, None)])
            return carry

        lax.fori_loop(0, i * band, full_step, 0)
        for jb in range(band):
            tiles([(i * band + jb, jb * tk + col < row)])
        dq_ref[...] = (jnp.where(m0, dq_acc[0], dq_acc[1]) * SB_SCALE).astype(BF16)

        @pl.when(i == nq - 1)
        def _():
            dk_ref[...] = dk_acc[...].astype(BF16)
            dv_ref[...] = dv_acc[...].astype(BF16)

    qmap = lambda g, i: (i, g)
    return _pallas(
        body, rider, name="attn_bwd", grid=(ngrp, nq),
        in_specs=[pl.BlockSpec((tq, LANES), qmap),
                  pl.BlockSpec((T, LANES), lambda g, i: (0, ngrp + g)),
                  pl.BlockSpec((T, LANES), lambda g, i: (0, 2 * ngrp + g)),
                  pl.BlockSpec((tq, LANES), qmap), pl.BlockSpec((tq, LANES), qmap)],
        out_specs=[pl.BlockSpec((tq, LANES), qmap),
                   pl.BlockSpec((T, LANES), lambda g, i: (0, g)),
                   pl.BlockSpec((T, LANES), lambda g, i: (0, g))],
        out_shape=[jax.ShapeDtypeStruct((T, SB_WIDTH), BF16)] * 3,
        scratch_shapes=[pltpu.VMEM((2, tq, LANES), F32), pltpu.VMEM((T, LANES), F32),
                        pltpu.VMEM((T, LANES), F32), pltpu.VMEM((2, tq, 1), F32),
                        pltpu.VMEM((2, tq, 1), F32)],
        compiler_params=_params("arbitrary", "arbitrary"),
    )(qkv, qkv, qkv, d_o, ltot)


def _mix_out_fwd(h, gm, sb, w):
    T, D = h.shape
    tm = min(ROW_TILE, T)

    def body(h_ref, gm_ref, sb_ref, w_ref, o_ref):
        o_ref[...] = h_ref[...] + _dot(gm_ref[...], w_ref[:GM_WIDTH, :]) + _dot(sb_ref[...], w_ref[GM_WIDTH:, :])

    row = lambda i: (i, 0)
    return pl.pallas_call(
        body, name="mix_out_fwd", grid=(T // tm,),
        in_specs=[pl.BlockSpec((tm, D), row), pl.BlockSpec((tm, GM_WIDTH), row), pl.BlockSpec((tm, SB_WIDTH), row),
                  pl.BlockSpec((GM_WIDTH + SB_WIDTH, D), lambda i: (0, 0))],
        out_specs=pl.BlockSpec((tm, D), row),
        out_shape=jax.ShapeDtypeStruct((T, D), F32),
        compiler_params=_params("arbitrary"),
    )(h, gm, sb, w)


def _mix_out_bwd(dh, w):
    T, D = dh.shape
    tm = min(ROW_TILE, T)

    def body(dh_ref, w_ref, dgm_ref, dsb_ref, dhb_ref):
        dhb = dh_ref[...].astype(BF16)
        dhb_ref[...] = dhb
        dgm_ref[...] = _dot_nt(dhb, w_ref[:GM_WIDTH, :])
        dsb_ref[...] = _dot_nt(dhb, w_ref[GM_WIDTH:, :])

    row = lambda i: (i, 0)
    return pl.pallas_call(
        body, name="mix_out_bwd", grid=(T // tm,),
        in_specs=[pl.BlockSpec((tm, D), row), pl.BlockSpec((GM_WIDTH + SB_WIDTH, D), lambda i: (0, 0))],
        out_specs=[pl.BlockSpec((tm, GM_WIDTH), row), pl.BlockSpec((tm, SB_WIDTH), row), pl.BlockSpec((tm, D), row)],
        out_shape=[jax.ShapeDtypeStruct((T, GM_WIDTH), F32), jax.ShapeDtypeStruct((T, SB_WIDTH), F32),
                   jax.ShapeDtypeStruct((T, D), BF16)],
        compiler_params=_params("arbitrary"),
    )(dh, w)


def _tail(h3, p, target, g_ple, g_fin, w_gate, w_proj):
    T, D = h3.shape
    PD = p.shape[1]
    tm = min(ROW_TILE, T)

    def body(h_ref, p_ref, t_ref, gp_ref, gf_ref, wg_ref, wp_ref,
             loss_ref, dh_ref, n4_ref, dgl_ref, dpp_ref, dgp_ref, dgf_ref):
        @pl.when(pl.program_id(0) == 0)
        def _():
            loss_ref[...] = jnp.zeros_like(loss_ref)
            dgp_ref[...] = jnp.zeros_like(dgp_ref)
            dgf_ref[...] = jnp.zeros_like(dgf_ref)

        h3v = h_ref[...]
        gp = gp_ref[...]
        gf = gf_ref[...]
        r3, xh3 = _rms_parts(h3v)
        n4 = (xh3 * gp).astype(BF16)
        n4_ref[...] = n4
        gate = _sigmoid(_dot(n4, wg_ref[...]))
        pp = _dot(p_ref[...], wp_ref[...])
        h4 = h3v + gate * pp
        r4, xh4 = _rms_parts(h4)
        err = xh4 * gf - t_ref[...]
        loss_ref[...] += jnp.full(loss_ref.shape, (0.5 / D) * jnp.sum(err * err), F32)
        dy = err * (1.0 / D)
        dgf_ref[...] += jnp.sum(dy * xh4, axis=0, keepdims=True)
        dyg = dy * gf
        dh4 = r4 * (dyg - xh4 * jnp.mean(dyg * xh4, axis=-1, keepdims=True))
        dpp_ref[...] = (dh4 * gate).astype(BF16)
        dgl = (dh4 * pp * gate * (1.0 - gate)).astype(BF16)
        dgl_ref[...] = dgl
        dn4 = _dot_nt(dgl, wg_ref[...])
        dgp_ref[...] += jnp.sum(dn4 * xh3, axis=0, keepdims=True)
        dn4g = dn4 * gp
        dh_ref[...] = dh4 + r3 * (dn4g - xh3 * jnp.mean(dn4g * xh3, axis=-1, keepdims=True))

    row = lambda i: (i, 0)
    one = lambda i: (0, 0)
    return pl.pallas_call(
        body, name="tail", grid=(T // tm,),
        in_specs=[pl.BlockSpec((tm, D), row), pl.BlockSpec((tm, PD), row), pl.BlockSpec((tm, D), row),
                  pl.BlockSpec((1, D), one), pl.BlockSpec((1, D), one),
                  pl.BlockSpec((D, D), one), pl.BlockSpec((PD, D), one)],
        out_specs=[pl.BlockSpec((1, LANES), one), pl.BlockSpec((tm, D), row), pl.BlockSpec((tm, D), row),
                   pl.BlockSpec((tm, D), row), pl.BlockSpec((tm, D), row),
                   pl.BlockSpec((1, D), one), pl.BlockSpec((1, D), one)],
        out_shape=[jax.ShapeDtypeStruct((1, LANES), F32), jax.ShapeDtypeStruct((T, D), F32),
                   jax.ShapeDtypeStruct((T, D), BF16), jax.ShapeDtypeStruct((T, D), BF16),
                   jax.ShapeDtypeStruct((T, D), BF16),
                   jax.ShapeDtypeStruct((1, D), F32), jax.ShapeDtypeStruct((1, D), F32)],
        compiler_params=_params("arbitrary"),
    )(h3, p, target, g_ple, g_fin, w_gate, w_proj)


FFN1_W = ("ffn1_w_in", "ffn1_w_out")
MIX_W = ("w_mix_in", "w_mix_out")
REST_W = ("ffn2_w_in", "ffn2_w_out", "ple_w_gate", "ple_w_proj")
BIG_W = FFN1_W + MIX_W + REST_W
COLUMN_SHARDED = ("w_mix_in", "ple_w_proj")


class _Traffic:
    def __init__(self, shards):
        self.shards = shards
        self.parts = {}

    @staticmethod
    def _full(name, gathered):
        if name in COLUMN_SHARDED:
            return jnp.transpose(gathered, (1, 0, 2)).reshape(gathered.shape[1], -1)
        if name.endswith("_w_in"):
            return gathered
        return gathered.reshape(-1, gathered.shape[-1])

    @staticmethod
    def _blocks(name, grad):
        name = name.split("/")[0]
        if name in COLUMN_SHARDED:
            return jnp.transpose(grad.reshape(grad.shape[0], N_DEV, -1), (1, 0, 2))
        if name.endswith("_w_in"):
            return grad
        return grad.reshape(N_DEV, -1, grad.shape[-1])

    def gather_now(self, names):
        got = _exchange("gather_" + names[0], [self.shards[n] for n in names], [False] * len(names))
        return self.gathered(names, got)

    def gather_rider(self, names):
        return [self.shards[n] for n in names], [False] * len(names)

    def gathered(self, names, got):
        return {n: self._full(n, g) for n, g in zip(names, got)}

    def scatter_rider(self, grads):
        return [self._blocks(n, g) for n, g in grads.items()], [True] * len(grads)

    def scattered(self, names, got):
        self.parts.update(zip(names, got))

    def finish(self, grads, small):
        arrays, flags = self.scatter_rider(grads)
        got = _exchange("scatter_last", arrays + [small], flags + [False])
        self.scattered(list(grads), got[:-1])
        return got[-1]


def _local_step(traffic, x, p_bf, target, g1, gmix, gv, ws, b_t, g2, gple, gfin, pack_small):
    T, D = x.shape
    tm = min(ROW_TILE, T)

    w = traffic.gather_now(FFN1_W)
    h1, n1, G1, U1, a1, *got = _ffn_fwd("ffn1_fwd", x, g1, w["ffn1_w_in"], w["ffn1_w_out"],
                                        rider=traffic.gather_rider(MIX_W))
    w.update(traffic.gathered(MIX_W, got))
    n2, zuv, qkv = _mix_in_fwd(h1, gmix, w["w_mix_in"])
    gm = _gmlp_fwd(zuv, gv, ws, b_t)
    sb, ltot, *got = _attn_fwd(qkv, rider=traffic.gather_rider(REST_W))
    w.update(traffic.gathered(REST_W, got))
    h2 = _mix_out_fwd(h1, gm, sb, w["w_mix_out"])
    h3, n3, G2, U2, a2 = _ffn_fwd("ffn2_fwd", h2, g2, w["ffn2_w_in"], w["ffn2_w_out"])
    loss, dh3, n4, d_gl, d_pp, dg_ple, dg_fin = _tail(h3, p_bf, target, gple, gfin, w["ple_w_gate"], w["ple_w_proj"])

    nb, _, FB = w["ffn1_w_in"].shape
    nh = nb // 2

    tt = min(GRAD_ROW_TILE, T)

    def dw_out(name, a, d_out):
        return _matmul_tn(name, a, d_out, nh, (1, tt, FB), lambda j, t: (j, t, 0), (tt, D), lambda j, t: (t, 0),
                          (nh, FB, D), (1, FB, D), lambda j, t: (j, 0, 0))

    def dense_tn(name, a, b, ncol):
        ka, nbw = a.shape[1], b.shape[1] // ncol
        return _matmul_tn(name, a, b, ncol, (tt, ka), lambda j, t: (t, 0), (tt, nbw), lambda j, t: (t, j),
                          (ka, b.shape[1]), (ka, nbw), lambda j, t: (0, j))

    grads = dict(ple_w_gate=dense_tn("dw_ple_gate", n4, d_gl, 2), ple_w_proj=dense_tn("dw_ple_proj", p_bf, d_pp, 1))
    dh2, dg2, dG2, dU2, dout2 = _ffn_bwd("ffn2_bwd", dh3, h2, g2, G2, U2, w["ffn2_w_in"], w["ffn2_w_out"])
    grads["ffn2_w_in"], = _dw_in("ffn2_dw_in", n3, dG2, dU2, 0, 1)
    grads["ffn2_w_out"] = dw_out("ffn2_dw_out", a2, dout2)
    grads = {n: grads[n] for n in REST_W}

    d_gm, d_sb, dh2_bf = _mix_out_bwd(dh2, w["w_mix_out"])
    dw_mo = jnp.concatenate([dense_tn("dw_mix_out_gm", gm, dh2_bf, 1), dense_tn("dw_mix_out_sb", sb, dh2_bf, 1)], axis=0)
    dzuv, dgv, dws, db_t = _gmlp_bwd(zuv, d_gm, gv, ws, b_t)
    dq, dk, dv, *got = _attn_bwd(qkv, d_sb, ltot, rider=traffic.scatter_rider(grads))
    traffic.scattered(REST_W, got)
    dqkv = jnp.concatenate([dq, dk, dv], axis=1)
    dw_mi = jnp.concatenate([dense_tn("dw_mix_in_uv", n2, dzuv, 2), dense_tn("dw_mix_in_qkv", n2, dqkv, 3)], axis=1)
    dh1, dgmix, dout1 = _mix_in_bwd(dzuv, dqkv, w["w_mix_in"], h1, gmix, dh2)

    dw_out1 = dw_out("ffn1_dw_out", a1, dout1)
    grads = dict(w_mix_in=dw_mi, w_mix_out=dw_mo)
    dx, dg1, dG1, dU1, _, *got = _ffn_bwd("ffn1_bwd", dh1, x, g1, G1, U1, w["ffn1_w_in"], w["ffn1_w_out"],
                                         rider=traffic.scatter_rider(grads))
    traffic.scattered(list(grads), got)
    top, *got = _dw_in("ffn1_dw_in_top", n1, dG1, dU1, 0, 2, rider=traffic.scatter_rider({"ffn1_w_out": dw_out1}))
    traffic.scattered(["ffn1_w_out"], got)
    bottom, *got = _dw_in("ffn1_dw_in_bottom", n1, dG1, dU1, 1, 2, rider=traffic.scatter_rider({"ffn1_w_in/0": top}))
    traffic.scattered(["ffn1_w_in/0"], got)

    small = pack_small(dict(ffn1_norm=dg1, mix_norm=dgmix, gmlp_v_norm=dgv, gmlp_w_s=dws, gmlp_b=jnp.transpose(db_t),
                            ffn2_norm=dg2, ple_norm=dg_ple, final_norm=dg_fin), loss)
    return dx, traffic.finish({"ffn1_w_in/1": bottom}, small)


def _peer(d):
    x, y, c = lax.axis_index("x"), lax.axis_index("y"), lax.axis_index("c")
    px = 1 - x if d & 4 else x
    py = 1 - y if d & 2 else y
    pc = 1 - c if d & 1 else c
    return (px, py, pc), 4 * px + 2 * py + pc


N_CHIPS_AWAY = 3


class _ExchangePlan:
    def __init__(self, ins, outs, send, recv, local, scatter):
        self.ins, self.outs, self.send, self.recv, self.local, self.scatter = ins, outs, send, recv, local, scatter
        self.me = _peer(0)[1]

    def _remote(self, t, sem, src, slot, peer):
        return pltpu.make_async_remote_copy(
            src_ref=src, dst_ref=self.outs[t].at[slot], send_sem=self.send.at[t, sem], recv_sem=self.recv.at[t, sem],
            device_id=peer, device_id_type=MESH)

    def _own(self, t):
        src = self.ins[t].at[self.me] if self.scatter[t] else self.ins[t]
        return pltpu.make_async_copy(src, self.outs[t].at[self.me], self.local.at[t])

    def _n_direct(self, t):
        return N_DEV - 1 if self.scatter[t] else N_CHIPS_AWAY + 1

    def _direct(self, t, k):
        if self.scatter[t]:
            peer, slot = _peer(k + 1)
            return self._remote(t, k, self.ins[t].at[slot], self.me, peer)
        return self._remote(t, k, self.ins[t], self.me, _peer(2 * k if k else 1)[0])

    def _relay(self, t, c):
        slot = _peer(2 * c)[1]
        return self._remote(t, N_CHIPS_AWAY + c, self.outs[t].at[slot], slot, _peer(1)[0])

    def start(self):
        for t in range(len(self.ins)):
            self._own(t).start()
            for k in range(self._n_direct(t)):
                self._direct(t, k).start()

    def finish(self):
        gathers = [t for t in range(len(self.ins)) if not self.scatter[t]]
        for t in gathers:
            for c in range(1, N_CHIPS_AWAY + 1):
                self._direct(t, c).wait_recv()
                self._relay(t, c).start()
        for t in range(len(self.ins)):
            self._own(t).wait()
            for k in range(self._n_direct(t)):
                self._direct(t, k).wait_send()
                if self.scatter[t] or k == 0:
                    self._direct(t, k).wait_recv()
        for t in gathers:
            for c in range(1, N_CHIPS_AWAY + 1):
                self._relay(t, c).wait()


def _exchange_shapes(arrays, scatter):
    return [jax.ShapeDtypeStruct(a.shape if sc else (N_DEV,) + a.shape, a.dtype) for a, sc in zip(arrays, scatter)]


def _exchange_sems(n):
    return [pltpu.SemaphoreType.DMA((n, N_DEV - 1)), pltpu.SemaphoreType.DMA((n, N_DEV - 1)),
            pltpu.SemaphoreType.DMA((n,))]


_ANY = pl.BlockSpec(memory_space=pl.ANY)


def _exchange(name, arrays, scatter):
    n = len(arrays)

    def body(*refs):
        plan = _ExchangePlan(refs[:n], refs[n:2 * n], *refs[2 * n:], scatter)
        plan.start()
        plan.finish()

    return pl.pallas_call(
        body, name=name, in_specs=[_ANY] * n, out_specs=[_ANY] * n, out_shape=_exchange_shapes(arrays, scatter),
        scratch_shapes=_exchange_sems(n),
    )(*arrays)


def _pallas(body, rider, *, name, grid, in_specs, out_specs, out_shape, scratch_shapes=(), compiler_params=None):
    if rider is None:
        return pl.pallas_call(body, name=name, grid=grid, in_specs=in_specs, out_specs=out_specs, out_shape=out_shape,
                              scratch_shapes=list(scratch_shapes), compiler_params=compiler_params)
    arrays, scatter = rider
    n, ni, no, ns = len(arrays), len(in_specs), len(out_specs), len(scratch_shapes)

    def carried(*refs):
        ins, r_in = refs[:ni], refs[ni:ni + n]
        outs, r_out = refs[ni + n:ni + n + no], refs[ni + n + no:ni + 2 * n + no]
        scratch, sems = refs[ni + 2 * n + no:ni + 2 * n + no + ns], refs[ni + 2 * n + no + ns:]
        ids = [pl.program_id(ax) for ax in range(len(grid))]
        first = functools.reduce(jnp.logical_and, [i == 0 for i in ids])
        last = functools.reduce(jnp.logical_and, [i == g - 1 for i, g in zip(ids, grid)])

        @pl.when(first)
        def _():
            _ExchangePlan(r_in, r_out, *sems, scatter).start()

        body(*ins, *outs, *scratch)

        @pl.when(last)
        def _():
            _ExchangePlan(r_in, r_out, *sems, scatter).finish()

    call = pl.pallas_call(
        carried, name=name, grid=grid, in_specs=list(in_specs) + [_ANY] * n, out_specs=list(out_specs) + [_ANY] * n,
        out_shape=list(out_shape) + _exchange_shapes(arrays, scatter),
        scratch_shapes=list(scratch_shapes) + _exchange_sems(n), compiler_params=compiler_params)
    return lambda *args: call(*args, *arrays)


def _adamw_math(g, w, m, v):
    m_new = ADAM_B1 * m + (1.0 - ADAM_B1) * g
    v_new = ADAM_B2 * v + (1.0 - ADAM_B2) * (g * g)
    m_hat = m_new / (1.0 - ADAM_B1 ** ADAM_STEP)
    v_hat = v_new / (1.0 - ADAM_B2 ** ADAM_STEP)
    delta = -ADAM_LR * (m_hat / (jnp.sqrt(v_hat) + ADAM_EPS) + ADAM_WD * w)
    return delta, m_new, v_new


def _adamw(name, parts, w, m, v):
    R, C = w.shape
    tr = R
    for cand in (256, 128, 64, 32, 16, 8):
        if R % cand == 0:
            tr = cand
            break

    def body(p_ref, w_ref, m_ref, v_ref, g_ref, d_ref, nm_ref, nv_ref):
        g = p_ref[0].astype(F32)
        for j in range(1, N_DEV):
            g = g + p_ref[j].astype(F32)
        g_ref[...] = g
        d_ref[...], nm_ref[...], nv_ref[...] = _adamw_math(g, w_ref[...], m_ref[...], v_ref[...])

    row = lambda i: (i, 0)
    spec = pl.BlockSpec((tr, C), row)
    return pl.pallas_call(
        body, name=name, grid=(R // tr,),
        in_specs=[pl.BlockSpec((N_DEV, tr, C), lambda i: (0, i, 0)), spec, spec, spec],
        out_specs=[spec] * 4,
        out_shape=[jax.ShapeDtypeStruct((R, C), F32)] * 4,
        compiler_params=_params("arbitrary"),
    )(parts, w, m, v)


def _rows128(a):
    flat = a.reshape(-1, LANES)
    pad = (-flat.shape[0]) % SMALL_ROWS_ALIGN
    return jnp.pad(flat, ((0, pad), (0, 0))) if pad else flat


def _unrows(packed, like):
    n = like.size // LANES
    return packed[:n].reshape(like.shape)


def kernel(x, p, ffn1_norm, ffn1_w_in, ffn1_w_out, mix_norm, w_mix_in, gmlp_v_norm, gmlp_w_s, gmlp_b, w_mix_out, ffn2_norm, ffn2_w_in, ffn2_w_out, ple_norm, ple_w_gate, ple_w_proj, final_norm, loss_target, m_ffn1_norm, m_ffn1_w_in, m_ffn1_w_out, m_mix_norm, m_w_mix_in, m_gmlp_v_norm, m_gmlp_w_s, m_gmlp_b, m_w_mix_out, m_ffn2_norm, m_ffn2_w_in, m_ffn2_w_out, m_ple_norm, m_ple_w_gate, m_ple_w_proj, m_final_norm, v_ffn1_norm, v_ffn1_w_in, v_ffn1_w_out, v_mix_norm, v_w_mix_in, v_gmlp_v_norm, v_gmlp_w_s, v_gmlp_b, v_w_mix_out, v_ffn2_norm, v_ffn2_w_in, v_ffn2_w_out, v_ple_norm, v_ple_w_gate, v_ple_w_proj, v_final_norm):
    names = ["ffn1_norm", "ffn1_w_in", "ffn1_w_out", "mix_norm", "w_mix_in", "gmlp_v_norm", "gmlp_w_s", "gmlp_b",
             "w_mix_out", "ffn2_norm", "ffn2_w_in", "ffn2_w_out", "ple_norm", "ple_w_gate", "ple_w_proj", "final_norm"]
    W = dict(zip(names, [ffn1_norm, ffn1_w_in, ffn1_w_out, mix_norm, w_mix_in, gmlp_v_norm, gmlp_w_s, gmlp_b,
                         w_mix_out, ffn2_norm, ffn2_w_in, ffn2_w_out, ple_norm, ple_w_gate, ple_w_proj, final_norm]))
    M = dict(zip(names, [m_ffn1_norm, m_ffn1_w_in, m_ffn1_w_out, m_mix_norm, m_w_mix_in, m_gmlp_v_norm, m_gmlp_w_s,
                         m_gmlp_b, m_w_mix_out, m_ffn2_norm, m_ffn2_w_in, m_ffn2_w_out, m_ple_norm, m_ple_w_gate,
                         m_ple_w_proj, m_final_norm]))
    V = dict(zip(names, [v_ffn1_norm, v_ffn1_w_in, v_ffn1_w_out, v_mix_norm, v_w_mix_in, v_gmlp_v_norm, v_gmlp_w_s,
                         v_gmlp_b, v_w_mix_out, v_ffn2_norm, v_ffn2_w_in, v_ffn2_w_out, v_ple_norm, v_ple_w_gate,
                         v_ple_w_proj, v_final_norm]))
    small = [n for n in names if n not in BIG_W]
    D = x.shape[-1]

    def pack(src, last):
        return jnp.concatenate([_rows128(src[n]) for n in small] + [last], axis=0)

    offs = [0]
    for n in small:
        offs.append(offs[-1] + _rows128(W[n]).shape[0])

    traffic = _Traffic({n: W[n][0].astype(BF16) for n in BIG_W})
    dx, small_parts = _local_step(
        traffic, x[0], p[0, 0].astype(BF16), loss_target[0],
        W["ffn1_norm"], W["mix_norm"], W["gmlp_v_norm"], W["gmlp_w_s"][0], jnp.transpose(W["gmlp_b"][0]),
        W["ffn2_norm"], W["ple_norm"], W["final_norm"].reshape(1, D),
        lambda grads, loss_part: pack(grads, jnp.broadcast_to(loss_part, (SMALL_ROWS_ALIGN, LANES))))

    parts = traffic.parts
    parts["ffn1_w_in"] = jnp.concatenate([parts["ffn1_w_in/0"], parts["ffn1_w_in/1"]], axis=1)
    out = {}
    for n in BIG_W:
        out[n] = _adamw("adamw_" + n, parts[n], W[n][0], M[n][0], V[n][0])
    zeros = jnp.zeros((SMALL_ROWS_ALIGN, LANES), F32)
    sg, sd, sm, sv = _adamw("adamw_small", small_parts, pack(W, zeros), pack(M, zeros), pack(V, zeros))
    for k, n in enumerate(small):
        out[n] = tuple(_unrows(arr[offs[k]:offs[k + 1]], W[n]) for arr in (sg, sd, sm, sv))
    loss = sg[offs[len(small)], 0]

    res = [loss, dx[None]]
    for k in range(4):
        res += [out[n][k].reshape(W[n].shape) for n in names]
    return tuple(res)
```

```python
import functools

import jax
import jax.numpy as jnp
from jax import lax
from jax.experimental import pallas as pl
from jax.experimental.pallas import tpu as pltpu

F32 = jnp.float32
BF16 = jnp.bfloat16
MESH = pl.DeviceIdType.MESH

N_DEV = 8
EPS = 1e-6
ADAM_LR = 0.001
ADAM_B1 = 0.9
ADAM_B2 = 0.999
ADAM_EPS = 1e-08
ADAM_WD = 0.01
ADAM_STEP = 10

GM_WIDTH = 512
GM_HEADS = 4
CHUNK = 128
SB_WIDTH = 512
SB_HEAD_DIM = 64
SB_SCALE = 0.125
LANES = 128
SMALL_ROWS_ALIGN = 8

ROW_TILE = 512
GRAD_ROW_TILE = 2048
DW_IN_ROW_TILE = 4096
FFN_FWD_ROW_TILE = 1024
ATTN_Q_ROWS = 512
ATTN_BWD_Q_ROWS = 512
ATTN_KEY_BLOCK = 256
ATTN_UNROLL = 2
VMEM_LIMIT = 56 * 1024 * 1024


def _params(*sem):
    return pltpu.CompilerParams(dimension_semantics=sem, vmem_limit_bytes=VMEM_LIMIT)


def _dot(a, b):
    return jnp.dot(a, b, preferred_element_type=F32)


def _dot_nt(a, b):
    return lax.dot_general(a, b, (((1,), (1,)), ((), ())), preferred_element_type=F32)


def _dot_tn(a, b):
    return lax.dot_general(a, b, (((0,), (0,)), ((), ())), preferred_element_type=F32)


def _rms_parts(x):
    r = lax.rsqrt(jnp.mean(x * x, axis=-1, keepdims=True) + EPS)
    return r, x * r


def _rms_bwd(x, g, dy):
    r, xh = _rms_parts(x)
    dyg = dy * g
    dx = r * (dyg - xh * jnp.mean(dyg * xh, axis=-1, keepdims=True))
    return dx, jnp.sum(dy * xh, axis=0, keepdims=True)


def _sigmoid(x):
    return 1.0 / (1.0 + jnp.exp(-x))


_SQRT_HALF = 0.7071067811865476
_INV_SQRT_2PI = 0.3989422804014327


def _gelu(x):
    return 0.5 * x * (1.0 + lax.erf(x * _SQRT_HALF))


def _gelu_grad(x):
    return 0.5 * (1.0 + lax.erf(x * _SQRT_HALF)) + x * (_INV_SQRT_2PI * jnp.exp(-0.5 * x * x))


def _split_bf16(x):
    hi = x.astype(BF16)
    lo = (x - hi.astype(F32)).astype(BF16)
    return hi, lo


def _ffn_fwd(name, h, gain, w_in, w_out, rider=None):
    T, D = h.shape
    nb, _, FB = w_in.shape
    nh = nb // 2
    tm = min(FFN_FWD_ROW_TILE, T)

    def body(h_ref, g_ref, wg_ref, wu_ref, wo_ref, ho_ref, n_ref, G_ref, U_ref, a_ref, n_s, acc):
        jj = pl.program_id(1)

        @pl.when(jj == 0)
        def _():
            _, xh = _rms_parts(h_ref[...])
            n = (xh * g_ref[...]).astype(BF16)
            n_s[...] = n
            n_ref[...] = n
            acc[...] = jnp.zeros_like(acc)

        n = n_s[...]
        G = _dot(n, wg_ref[0])
        U = _dot(n, wu_ref[0])
        G_ref[0] = G.astype(BF16)
        U_ref[0] = U.astype(BF16)
        a = (G * _sigmoid(G) * U).astype(BF16)
        a_ref[0] = a
        acc[...] += _dot(a, wo_ref[...])

        @pl.when(jj == nh - 1)
        def _():
            ho_ref[...] = h_ref[...] + 0.5 * acc[...]

    row = lambda i, j: (i, 0)
    blk = lambda i, j: (j, i, 0)
    return _pallas(
        body, rider, name=name, grid=(T // tm, nh),
        in_specs=[pl.BlockSpec((tm, D), row),
                  pl.BlockSpec((1, D), lambda i, j: (0, 0)),
                  pl.BlockSpec((1, D, FB), lambda i, j: (j, 0, 0)),
                  pl.BlockSpec((1, D, FB), lambda i, j: (j + nh, 0, 0)),
                  pl.BlockSpec((FB, D), lambda i, j: (j, 0))],
        out_specs=[pl.BlockSpec((tm, D), row), pl.BlockSpec((tm, D), row),
                   pl.BlockSpec((1, tm, FB), blk), pl.BlockSpec((1, tm, FB), blk),
                   pl.BlockSpec((1, tm, FB), blk)],
        out_shape=[jax.ShapeDtypeStruct((T, D), F32), jax.ShapeDtypeStruct((T, D), BF16),
                   jax.ShapeDtypeStruct((nh, T, FB), BF16), jax.ShapeDtypeStruct((nh, T, FB), BF16),
                   jax.ShapeDtypeStruct((nh, T, FB), BF16)],
        scratch_shapes=[pltpu.VMEM((tm, D), BF16), pltpu.VMEM((tm, D), F32)],
        compiler_params=_params("arbitrary", "arbitrary"),
    )(h, gain, w_in, w_in, w_out)


def _ffn_bwd(name, dh, h_in, gain, G, U, w_in, w_out, rider=None):
    T, D = dh.shape
    nb, _, FB = w_in.shape
    nh = nb // 2
    tm = min(ROW_TILE, T)

    def body(dh_ref, h_ref, g_ref, G_ref, U_ref, wg_ref, wu_ref, wo_ref,
             dhin_ref, dg_ref, dG_ref, dU_ref, do_ref, dn_acc, do_s):
        i = pl.program_id(0)
        jj = pl.program_id(1)

        @pl.when(jj == 0)
        def _():
            d_out = (0.5 * dh_ref[...]).astype(BF16)
            do_s[...] = d_out
            do_ref[...] = d_out
            dn_acc[...] = jnp.zeros_like(dn_acc)

        @pl.when((i == 0) & (jj == 0))
        def _():
            dg_ref[...] = jnp.zeros_like(dg_ref)

        halves = [slice(0, tm // 2), slice(tm // 2, tm)]
        da = [_dot_nt(do_s[rows, :], wo_ref[...]) for rows in halves]
        dGU = []
        for rows, dav in zip(halves, da):
            Gv = G_ref[0, rows, :].astype(F32)
            Uv = U_ref[0, rows, :].astype(F32)
            sig = _sigmoid(Gv)
            dU = (dav * (Gv * sig)).astype(BF16)
            dG = (dav * Uv * (sig * (1.0 + Gv * (1.0 - sig)))).astype(BF16)
            dG_ref[0, rows, :] = dG
            dU_ref[0, rows, :] = dU
            dGU.append((dG, dU))
        dn = [_dot_nt(dG, wg_ref[0]) for dG, _ in dGU]
        dn = [d + _dot_nt(dU, wu_ref[0]) for d, (_, dU) in zip(dn, dGU)]
        for rows, d in zip(halves, dn):
            dn_acc[rows, :] += d

        @pl.when(jj == nh - 1)
        def _():
            dx, dg = _rms_bwd(h_ref[...], g_ref[...], dn_acc[...])
            dhin_ref[...] = dh_ref[...] + dx
            dg_ref[...] += dg

    row = lambda i, j: (i, 0)
    blk = lambda i, j: (j, i, 0)
    one = lambda i, j: (0, 0)
    return _pallas(
        body, rider, name=name, grid=(T // tm, nh),
        in_specs=[pl.BlockSpec((tm, D), row), pl.BlockSpec((tm, D), row), pl.BlockSpec((1, D), one),
                  pl.BlockSpec((1, tm, FB), blk), pl.BlockSpec((1, tm, FB), blk),
                  pl.BlockSpec((1, D, FB), lambda i, j: (j, 0, 0)),
                  pl.BlockSpec((1, D, FB), lambda i, j: (j + nh, 0, 0)),
                  pl.BlockSpec((FB, D), lambda i, j: (j, 0))],
        out_specs=[pl.BlockSpec((tm, D), row), pl.BlockSpec((1, D), one),
                   pl.BlockSpec((1, tm, FB), blk), pl.BlockSpec((1, tm, FB), blk),
                   pl.BlockSpec((tm, D), row)],
        out_shape=[jax.ShapeDtypeStruct((T, D), F32), jax.ShapeDtypeStruct((1, D), F32),
                   jax.ShapeDtypeStruct((nh, T, FB), BF16), jax.ShapeDtypeStruct((nh, T, FB), BF16),
                   jax.ShapeDtypeStruct((T, D), BF16)],
        scratch_shapes=[pltpu.VMEM((tm, D), F32), pltpu.VMEM((tm, D), BF16)],
        compiler_params=_params("arbitrary", "arbitrary"),
    )(dh, h_in, gain, G, U, w_in, w_in, w_out)


def _matmul_tn(name, a, b, nj, a_block, a_map, b_block, b_map, out_shape, out_block, out_map):
    T = a.shape[-2]
    tt = a_block[-2]
    nt = T // tt
    kb, nbk = out_block[-2], out_block[-1]

    def body(a_ref, b_ref, o_ref, acc):
        t = pl.program_id(1)

        @pl.when(t == 0)
        def _():
            acc[...] = jnp.zeros_like(acc)

        av = (a_ref[0] if len(a_block) == 3 else a_ref[...]).astype(BF16)
        bv = b_ref[0] if len(b_block) == 3 else b_ref[...]
        acc[...] += _dot_tn(av, bv)

        @pl.when(t == nt - 1)
        def _():
            if len(out_block) == 3:
                o_ref[0] = acc[...].astype(o_ref.dtype)
            else:
                o_ref[...] = acc[...].astype(o_ref.dtype)

    return pl.pallas_call(
        body, name=name, grid=(nj, nt),
        in_specs=[pl.BlockSpec(a_block, a_map), pl.BlockSpec(b_block, b_map)],
        out_specs=pl.BlockSpec(out_block, out_map),
        out_shape=jax.ShapeDtypeStruct(out_shape, BF16),
        scratch_shapes=[pltpu.VMEM((kb, nbk), F32)],
        compiler_params=_params("arbitrary", "arbitrary"),
    )(a, b)


def _dw_in(name, n, dG, dU, part, nparts, rider=None):
    T, D = n.shape
    nh, _, FB = dG.shape
    kr = D // nparts
    tt = min(DW_IN_ROW_TILE, T)
    nt = T // tt

    def body(n_ref, dg_ref, du_ref, o_ref, acc):
        j = pl.program_id(0)
        t = pl.program_id(1)

        @pl.when(t == 0)
        def _():
            acc[...] = jnp.zeros_like(acc)

        def add(dz_ref):
            for rows in (slice(0, kr // 2), slice(kr // 2, kr)):
                acc[rows, :] += _dot_tn(n_ref[:, rows], dz_ref[0])

        @pl.when(j < nh)
        def _():
            add(dg_ref)

        @pl.when(j >= nh)
        def _():
            add(du_ref)

        @pl.when(t == nt - 1)
        def _():
            o_ref[0] = acc[...].astype(BF16)

    return _pallas(
        body, rider, name=name, grid=(2 * nh, nt),
        in_specs=[pl.BlockSpec((tt, kr), lambda j, t: (t, part)),
                  pl.BlockSpec((1, tt, FB), lambda j, t: (jnp.minimum(j, nh - 1), t, 0)),
                  pl.BlockSpec((1, tt, FB), lambda j, t: (jnp.maximum(j - nh, 0), t, 0))],
        out_specs=[pl.BlockSpec((1, kr, FB), lambda j, t: (j, 0, 0))],
        out_shape=[jax.ShapeDtypeStruct((2 * nh, kr, FB), BF16)],
        scratch_shapes=[pltpu.VMEM((kr, FB), F32)],
        compiler_params=_params("arbitrary", "arbitrary"),
    )(n, dG, dU)


def _mix_in_fwd(h, gain, w):
    T, D = h.shape
    W = w.shape[1]
    nuv = 2 * GM_WIDTH
    tm = min(ROW_TILE, T)

    def body(h_ref, g_ref, w_ref, n_ref, zuv_ref, qkv_ref):
        _, xh = _rms_parts(h_ref[...])
        n = (xh * g_ref[...]).astype(BF16)
        n_ref[...] = n
        z = _dot(n, w_ref[...])
        zuv_ref[...] = z[:, :nuv]
        qkv_ref[...] = z[:, nuv:].astype(BF16)

    row = lambda i: (i, 0)
    return pl.pallas_call(
        body, name="mix_in_fwd", grid=(T // tm,),
        in_specs=[pl.BlockSpec((tm, D), row), pl.BlockSpec((1, D), lambda i: (0, 0)),
                  pl.BlockSpec((D, W), lambda i: (0, 0))],
        out_specs=[pl.BlockSpec((tm, D), row), pl.BlockSpec((tm, nuv), row),
                   pl.BlockSpec((tm, W - nuv), row)],
        out_shape=[jax.ShapeDtypeStruct((T, D), BF16), jax.ShapeDtypeStruct((T, nuv), F32),
                   jax.ShapeDtypeStruct((T, W - nuv), BF16)],
        compiler_params=_params("arbitrary"),
    )(h, gain, w)


def _mix_in_bwd(dzuv, dqkv, w, h, gain, dh):
    T, D = h.shape
    W = w.shape[1]
    nuv = dzuv.shape[1]
    tm = min(ROW_TILE, T)

    def body(dzuv_ref, dqkv_ref, w_ref, h_ref, g_ref, dh_ref, dhin_ref, dg_ref, half_ref):
        @pl.when(pl.program_id(0) == 0)
        def _():
            dg_ref[...] = jnp.zeros_like(dg_ref)

        dn = _dot_nt(dzuv_ref[...], w_ref[:, :nuv]) + _dot_nt(dqkv_ref[...], w_ref[:, nuv:])
        dx, dg = _rms_bwd(h_ref[...], g_ref[...], dn)
        dh_in = dh_ref[...] + dx
        dhin_ref[...] = dh_in
        half_ref[...] = (0.5 * dh_in).astype(BF16)
        dg_ref[...] += dg

    row = lambda i: (i, 0)
    one = lambda i: (0, 0)
    return pl.pallas_call(
        body, name="mix_in_bwd", grid=(T // tm,),
        in_specs=[pl.BlockSpec((tm, nuv), row), pl.BlockSpec((tm, W - nuv), row),
                  pl.BlockSpec((D, W), one), pl.BlockSpec((tm, D), row), pl.BlockSpec((1, D), one),
                  pl.BlockSpec((tm, D), row)],
        out_specs=[pl.BlockSpec((tm, D), row), pl.BlockSpec((1, D), one), pl.BlockSpec((tm, D), row)],
        out_shape=[jax.ShapeDtypeStruct((T, D), F32), jax.ShapeDtypeStruct((1, D), F32),
                   jax.ShapeDtypeStruct((T, D), BF16)],
        compiler_params=_params("arbitrary"),
    )(dzuv, dqkv, w, h, gain, dh)


def _gmlp_norm(zv, gv):
    v = _gelu(zv)
    r, vh = _rms_parts(v)
    return r, vh, (vh * gv).astype(BF16)


def _causal_ws(ws_ref, hd):
    r = lax.broadcasted_iota(jnp.int32, (CHUNK, CHUNK), 0)
    c = lax.broadcasted_iota(jnp.int32, (CHUNK, CHUNK), 1)
    return jnp.where(r >= c, ws_ref[hd], 0.0).astype(BF16)


def _gmlp_fwd(zuv, gv, ws, b_t):
    T = zuv.shape[0]
    tg = min(ROW_TILE, T)

    def body(zu_ref, zv_ref, gv_ref, ws_ref, bt_ref, o_ref):
        u = _gelu(zu_ref[...])
        _, _, vn = _gmlp_norm(zv_ref[...], gv_ref[...])
        for hd in range(GM_HEADS):
            wc = _causal_ws(ws_ref, hd)
            cols = slice(hd * CHUNK, (hd + 1) * CHUNK)
            for c in range(tg // CHUNK):
                rows = slice(c * CHUNK, (c + 1) * CHUNK)
                sv = _dot(wc, vn[rows, cols]) + bt_ref[:, hd:hd + 1]
                o_ref[rows, cols] = (u[rows, cols] * sv).astype(BF16)

    return pl.pallas_call(
        body, name="gmlp_fwd", grid=(T // tg,),
        in_specs=[pl.BlockSpec((tg, GM_WIDTH), lambda i: (i, 0)), pl.BlockSpec((tg, GM_WIDTH), lambda i: (i, 1)),
                  pl.BlockSpec((1, GM_WIDTH), lambda i: (0, 0)),
                  pl.BlockSpec((GM_HEADS, CHUNK, CHUNK), lambda i: (0, 0, 0)),
                  pl.BlockSpec((CHUNK, GM_HEADS), lambda i: (0, 0))],
        out_specs=pl.BlockSpec((tg, GM_WIDTH), lambda i: (i, 0)),
        out_shape=jax.ShapeDtypeStruct((T, GM_WIDTH), BF16),
        compiler_params=_params("arbitrary"),
    )(zuv, zuv, gv, ws, b_t)


def _gmlp_bwd(zuv, d_gm, gv, ws, b_t):
    T = zuv.shape[0]
    tg = min(ROW_TILE, T)
    ng = T // tg

    def body(zu_ref, zv_ref, dgm_ref, gv_ref, ws_ref, bt_ref, dz_ref, dgv_ref, dws_ref, dbt_ref, dsv_acc, dvn_s):
        i = pl.program_id(0)

        @pl.when(i == 0)
        def _():
            dgv_ref[...] = jnp.zeros_like(dgv_ref)
            dws_ref[...] = jnp.zeros_like(dws_ref)
            dsv_acc[...] = jnp.zeros_like(dsv_acc)

        zu = zu_ref[...]
        zv = zv_ref[...]
        dgm = dgm_ref[...]
        gvv = gv_ref[...]
        u = _gelu(zu)
        rv, vh, vn = _gmlp_norm(zv, gvv)
        dsv = dgm * u
        dsv_b = dsv.astype(BF16)
        for hd in range(GM_HEADS):
            wc = _causal_ws(ws_ref, hd)
            cols = slice(hd * CHUNK, (hd + 1) * CHUNK)
            dws = jnp.zeros((CHUNK, CHUNK), F32)
            dsv_sum = jnp.zeros((CHUNK, CHUNK), F32)
            for c in range(tg // CHUNK):
                rows = slice(c * CHUNK, (c + 1) * CHUNK)
                vch = vn[rows, cols]
                sv = _dot(wc, vch) + bt_ref[:, hd:hd + 1]
                dz_ref[rows, cols] = (dgm[rows, cols] * sv * _gelu_grad(zu[rows, cols])).astype(BF16)
                dws += _dot_nt(dsv_b[rows, cols], vch)
                dsv_sum += dsv[rows, cols]
                dvn_s[rows, cols] = _dot_tn(wc, dsv_b[rows, cols])
            dws_ref[hd] += dws
            dsv_acc[:, cols] += dsv_sum
        dvn = dvn_s[...]
        dvh = dvn * gvv
        dv = rv * (dvh - vh * jnp.mean(dvh * vh, axis=-1, keepdims=True))
        dgv_ref[...] += jnp.sum(dvn * vh, axis=0, keepdims=True)
        dz_ref[:, GM_WIDTH:] = (dv * _gelu_grad(zv)).astype(BF16)

        @pl.when(i == ng - 1)
        def _():
            r = lax.broadcasted_iota(jnp.int32, (CHUNK, CHUNK), 0)
            c = lax.broadcasted_iota(jnp.int32, (CHUNK, CHUNK), 1)
            for hd in range(GM_HEADS):
                dws_ref[hd] = jnp.where(r >= c, dws_ref[hd], 0.0)
                dbt_ref[:, hd:hd + 1] = jnp.sum(dsv_acc[:, hd * CHUNK:(hd + 1) * CHUNK], axis=1, keepdims=True)

    return pl.pallas_call(
        body, name="gmlp_bwd", grid=(ng,),
        in_specs=[pl.BlockSpec((tg, GM_WIDTH), lambda i: (i, 0)), pl.BlockSpec((tg, GM_WIDTH), lambda i: (i, 1)),
                  pl.BlockSpec((tg, GM_WIDTH), lambda i: (i, 0)),
                  pl.BlockSpec((1, GM_WIDTH), lambda i: (0, 0)),
                  pl.BlockSpec((GM_HEADS, CHUNK, CHUNK), lambda i: (0, 0, 0)),
                  pl.BlockSpec((CHUNK, GM_HEADS), lambda i: (0, 0))],
        out_specs=[pl.BlockSpec((tg, 2 * GM_WIDTH), lambda i: (i, 0)),
                   pl.BlockSpec((1, GM_WIDTH), lambda i: (0, 0)),
                   pl.BlockSpec((GM_HEADS, CHUNK, CHUNK), lambda i: (0, 0, 0)),
                   pl.BlockSpec((CHUNK, GM_HEADS), lambda i: (0, 0))],
        out_shape=[jax.ShapeDtypeStruct((T, 2 * GM_WIDTH), BF16), jax.ShapeDtypeStruct((1, GM_WIDTH), F32),
                   jax.ShapeDtypeStruct((GM_HEADS, CHUNK, CHUNK), F32),
                   jax.ShapeDtypeStruct((CHUNK, GM_HEADS), F32)],
        scratch_shapes=[pltpu.VMEM((CHUNK, GM_WIDTH), F32), pltpu.VMEM((tg, GM_WIDTH), F32)],
        compiler_params=_params("arbitrary"),
    )(zuv, zuv, d_gm, gv, ws, b_t)


def _scan_matrix(blk, keep):
    r = lax.broadcasted_iota(jnp.int32, (blk, blk), 0)
    c = lax.broadcasted_iota(jnp.int32, (blk, blk), 1)
    return jnp.where(keep(r, c), 1.0, 0.0).astype(BF16)


def _scan_matrix2(blk, keep, value):
    m = _scan_matrix(blk, keep) * value
    return jnp.concatenate([m, m], axis=0)


def _scan(x, mat2):
    hi, lo = _split_bf16(x)
    return _dot(jnp.concatenate([hi, lo], axis=1), mat2)


def _head_masks(q):
    lane = lax.broadcasted_iota(jnp.int32, q.shape, 1)
    m0 = lane < SB_HEAD_DIM
    zero = jnp.zeros_like(q)
    return m0, jnp.where(m0, q, zero), jnp.where(m0, zero, q)


_LOG2E = 1.4426950408889634


def _softplus_parts(z):
    e = jnp.exp2(jnp.abs(z) * (-_LOG2E))
    ope = 1.0 + e
    return e, ope, jnp.maximum(z, 0.0) + jnp.log(ope)


def _attn_fwd(qkv, rider=None):
    T = qkv.shape[0]
    tk = ATTN_KEY_BLOCK
    tq = min(ATTN_Q_ROWS, T)
    band = tq // tk
    assert band % ATTN_UNROLL == 0 or T == tq
    ngrp = SB_WIDTH // LANES

    def body(q_ref, k_ref, v_ref, o_ref, l_ref, acc, run):
        i = pl.program_id(1)
        suffix = _scan_matrix2(tk, lambda r, c: r >= c, -1.0)
        row = lax.broadcasted_iota(jnp.int32, (tq, tk), 0)
        col = lax.broadcasted_iota(jnp.int32, (tq, tk), 1)
        m0, q0, q1 = _head_masks(q_ref[...] * SB_SCALE)
        acc[...] = jnp.zeros_like(acc)
        run[...] = jnp.zeros_like(run)

        def tiles(work):
            heads = (q0, q1)
            kv = []
            for j, _ in work:
                start = pl.multiple_of(j * tk, tk)
                kv.append((k_ref[pl.ds(start, tk), :], v_ref[pl.ds(start, tk), :]))
            z = [[_dot_nt(qh, kj) for qh in heads] for kj, _ in kv]
            sp = [[_softplus_parts(zz)[2] for zz in zt] for zt in z]
            sp = [[s if m is None else jnp.where(m, s, 0.0) for s in st] for st, (_, m) in zip(sp, work)]
            res = [[_scan(s, suffix) for s in st] for st in sp]
            runs = [run[hd] for hd in range(len(heads))]
            a = []
            for t, (_, m) in enumerate(work):
                at = []
                for hd in range(len(heads)):
                    av = jnp.exp(z[t][hd] + (runs[hd] + res[t][hd]))
                    at.append(av if m is None else jnp.where(m, av, 0.0))
                    runs[hd] = runs[hd] + res[t][hd][:, 0:1]
                a.append(at)
            for hd in range(len(heads)):
                run[hd] = runs[hd]
                upd = _dot(a[0][hd].astype(BF16), kv[0][1])
                for t in range(1, len(work)):
                    upd = upd + _dot(a[t][hd].astype(BF16), kv[t][1])
                acc[hd] += upd

        tiles([(i * band + jb, jb * tk + col < row) for jb in reversed(range(band))])

        def full_step(it, carry):
            tiles([(i * band - 1 - ATTN_UNROLL * it - u, None) for u in range(ATTN_UNROLL)])
            return carry

        lax.fori_loop(0, i * (band // ATTN_UNROLL), full_step, 0)
        o_ref[...] = jnp.where(m0, acc[0], acc[1]).astype(BF16)
        l_ref[...] = jnp.where(m0, jnp.broadcast_to(run[0], (tq, LANES)), jnp.broadcast_to(run[1], (tq, LANES)))

    return _pallas(
        body, rider, name="attn_fwd", grid=(ngrp, T // tq),
        in_specs=[pl.BlockSpec((tq, LANES), lambda g, i: (i, g)),
                  pl.BlockSpec((T, LANES), lambda g, i: (0, ngrp + g)),
                  pl.BlockSpec((T, LANES), lambda g, i: (0, 2 * ngrp + g))],
        out_specs=[pl.BlockSpec((tq, LANES), lambda g, i: (i, g)),
                   pl.BlockSpec((tq, LANES), lambda g, i: (i, g))],
        out_shape=[jax.ShapeDtypeStruct((T, SB_WIDTH), BF16), jax.ShapeDtypeStruct((T, SB_WIDTH), F32)],
        scratch_shapes=[pltpu.VMEM((2, tq, LANES), F32), pltpu.VMEM((2, tq, 1), F32)],
        compiler_params=_params("arbitrary", "arbitrary"),
    )(qkv, qkv, qkv)


def _attn_bwd(qkv, d_o, ltot, rider=None):
    T = qkv.shape[0]
    tk = ATTN_KEY_BLOCK
    tq = min(ATTN_BWD_Q_ROWS, T)
    band = tq // tk
    nq = T // tq
    ngrp = SB_WIDTH // LANES

    def body(q_ref, k_ref, v_ref, do_ref, l_ref, dq_ref, dk_ref, dv_ref, dq_acc, dk_acc, dv_acc, lpre, ppre):
        i = pl.program_id(1)

        @pl.when(i == 0)
        def _():
            dk_acc[...] = jnp.zeros_like(dk_acc)
            dv_acc[...] = jnp.zeros_like(dv_acc)

        excl = _scan_matrix(tk, lambda r, c: r < c)
        excl2 = jnp.concatenate([excl, excl], axis=0)
        row = lax.broadcasted_iota(jnp.int32, (tq, tk), 0)
        col = lax.broadcasted_iota(jnp.int32, (tq, tk), 1)
        m0, q0, q1 = _head_masks(q_ref[...] * SB_SCALE)
        _, d0, d1 = _head_masks(do_ref[...].astype(BF16))
        lt = l_ref[...]
        ltots = (lt[:, 0:1], lt[:, SB_HEAD_DIM:SB_HEAD_DIM + 1])
        dq_acc[...] = jnp.zeros_like(dq_acc)
        lpre[...] = jnp.zeros_like(lpre)
        ppre[...] = jnp.zeros_like(ppre)

        def tiles(work):
            heads = ((q0, d0), (q1, d1))
            nhd = len(heads)
            starts = [pl.multiple_of(j * tk, tk) for j, _ in work]
            kv = [(k_ref[pl.ds(st, tk), :], v_ref[pl.ds(st, tk), :]) for st in starts]
            masks = [m for _, m in work]
            every = [(t, hd) for t in range(len(work)) for hd in range(nhd)]
            z = {(t, hd): _dot_nt(heads[hd][0], kv[t][0]) for t, hd in every}
            da = {(t, hd): _dot_nt(heads[hd][1], kv[t][1]) for t, hd in every}
            sp, beta = {}, {}
            for key in every:
                s = _softplus_parts(z[key])[2]
                beta[key] = jnp.exp(z[key] - s)
                sp[key] = s if masks[key[0]] is None else jnp.where(masks[key[0]], s, 0.0)
            res = {key: _scan(sp[key], excl2) for key in every}
            lp = [lpre[hd] for hd in range(nhd)]
            a, p = {}, {}
            for t, hd in every:
                av = jnp.exp(z[t, hd] + ((ltots[hd] + lp[hd]) + res[t, hd]))
                a[t, hd] = av if masks[t] is None else jnp.where(masks[t], av, 0.0)
                p[t, hd] = a[t, hd] * da[t, hd]
                lp[hd] = lp[hd] + (res[t, hd][:, tk - 1:tk] + sp[t, hd][:, tk - 1:tk])
            resp = {key: _dot(p[key].astype(BF16), excl) for key in every}
            pp = [ppre[hd] for hd in range(nhd)]
            dzb = {}
            for t, hd in every:
                dz = p[t, hd] - beta[t, hd] * (p[t, hd] + (pp[hd] + resp[t, hd]))
                if masks[t] is not None:
                    dz = jnp.where(masks[t], dz, 0.0)
                dzb[t, hd] = dz.astype(BF16)
                pp[hd] = pp[hd] + (resp[t, hd][:, tk - 1:tk] + p[t, hd][:, tk - 1:tk])
            for hd in range(nhd):
                lpre[hd] = lp[hd]
                ppre[hd] = pp[hd]
                upd = _dot(dzb[0, hd], kv[0][0])
                for t in range(1, len(work)):
                    upd = upd + _dot(dzb[t, hd], kv[t][0])
                dq_acc[hd] += upd
            for t, st in enumerate(starts):
                dk = _dot_tn(dzb[t, 0], heads[0][0])
                dv = _dot_tn(a[t, 0].astype(BF16), heads[0][1])
                for hd in range(1, nhd):
                    dk = dk + _dot_tn(dzb[t, hd], heads[hd][0])
                    dv = dv + _dot_tn(a[t, hd].astype(BF16), heads[hd][1])
                dk_acc[pl.ds(st, tk), :] += dk
                dv_acc[pl.ds(st, tk), :] += dv

        def full_step(j, carry):
            tiles([(j, None)])
            return carry

        lax.fori_loop(0, i * band, full_step, 0)
        for jb in range(band):
            tiles([(i * band + jb, jb * tk + col < row)])
        dq_ref[...] = (jnp.where(m0, dq_acc[0], dq_acc[1]) * SB_SCALE).astype(BF16)

        @pl.when(i == nq - 1)
        def _():
            dk_ref[...] = dk_acc[...].astype(BF16)
            dv_ref[...] = dv_acc[...].astype(BF16)

    qmap = lambda g, i: (i, g)
    return _pallas(
        body, rider, name="attn_bwd", grid=(ngrp, nq),
        in_specs=[pl.BlockSpec((tq, LANES), qmap),
                  pl.BlockSpec((T, LANES), lambda g, i: (0, ngrp + g)),
                  pl.BlockSpec((T, LANES), lambda g, i: (0, 2 * ngrp + g)),
                  pl.BlockSpec((tq, LANES), qmap), pl.BlockSpec((tq, LANES), qmap)],
        out_specs=[pl.BlockSpec((tq, LANES), qmap),
                   pl.BlockSpec((T, LANES), lambda g, i: (0, g)),
                   pl.BlockSpec((T, LANES), lambda g, i: (0, g))],
        out_shape=[jax.ShapeDtypeStruct((T, SB_WIDTH), BF16)] * 3,
        scratch_shapes=[pltpu.VMEM((2, tq, LANES), F32), pltpu.VMEM((T, LANES), F32),
                        pltpu.VMEM((T, LANES), F32), pltpu.VMEM((2, tq, 1), F32),
                        pltpu.VMEM((2, tq, 1), F32)],
        compiler_params=_params("arbitrary", "arbitrary"),
    )(qkv, qkv, qkv, d_o, ltot)


def _mix_out_fwd(h, gm, sb, w):
    T, D = h.shape
    tm = min(ROW_TILE, T)

    def body(h_ref, gm_ref, sb_ref, w_ref, o_ref):
        o_ref[...] = h_ref[...] + _dot(gm_ref[...], w_ref[:GM_WIDTH, :]) + _dot(sb_ref[...], w_ref[GM_WIDTH:, :])

    row = lambda i: (i, 0)
    return pl.pallas_call(
        body, name="mix_out_fwd", grid=(T // tm,),
        in_specs=[pl.BlockSpec((tm, D), row), pl.BlockSpec((tm, GM_WIDTH), row), pl.BlockSpec((tm, SB_WIDTH), row),
                  pl.BlockSpec((GM_WIDTH + SB_WIDTH, D), lambda i: (0, 0))],
        out_specs=pl.BlockSpec((tm, D), row),
        out_shape=jax.ShapeDtypeStruct((T, D), F32),
        compiler_params=_params("arbitrary"),
    )(h, gm, sb, w)


def _mix_out_bwd(dh, w):
    T, D = dh.shape
    tm = min(ROW_TILE, T)

    def body(dh_ref, w_ref, dgm_ref, dsb_ref, dhb_ref):
        dhb = dh_ref[...].astype(BF16)
        dhb_ref[...] = dhb
        dgm_ref[...] = _dot_nt(dhb, w_ref[:GM_WIDTH, :])
        dsb_ref[...] = _dot_nt(dhb, w_ref[GM_WIDTH:, :])

    row = lambda i: (i, 0)
    return pl.pallas_call(
        body, name="mix_out_bwd", grid=(T // tm,),
        in_specs=[pl.BlockSpec((tm, D), row), pl.BlockSpec((GM_WIDTH + SB_WIDTH, D), lambda i: (0, 0))],
        out_specs=[pl.BlockSpec((tm, GM_WIDTH), row), pl.BlockSpec((tm, SB_WIDTH), row), pl.BlockSpec((tm, D), row)],
        out_shape=[jax.ShapeDtypeStruct((T, GM_WIDTH), F32), jax.ShapeDtypeStruct((T, SB_WIDTH), F32),
                   jax.ShapeDtypeStruct((T, D), BF16)],
        compiler_params=_params("arbitrary"),
    )(dh, w)


def _tail(h3, p, target, g_ple, g_fin, w_gate, w_proj):
    T, D = h3.shape
    PD = p.shape[1]
    tm = min(ROW_TILE, T)

    def body(h_ref, p_ref, t_ref, gp_ref, gf_ref, wg_ref, wp_ref,
             loss_ref, dh_ref, n4_ref, dgl_ref, dpp_ref, dgp_ref, dgf_ref):
        @pl.when(pl.program_id(0) == 0)
        def _():
            loss_ref[...] = jnp.zeros_like(loss_ref)
            dgp_ref[...] = jnp.zeros_like(dgp_ref)
            dgf_ref[...] = jnp.zeros_like(dgf_ref)

        h3v = h_ref[...]
        gp = gp_ref[...]
        gf = gf_ref[...]
        r3, xh3 = _rms_parts(h3v)
        n4 = (xh3 * gp).astype(BF16)
        n4_ref[...] = n4
        gate = _sigmoid(_dot(n4, wg_ref[...]))
        pp = _dot(p_ref[...].astype(BF16), wp_ref[...])
        h4 = h3v + gate * pp
        r4, xh4 = _rms_parts(h4)
        err = xh4 * gf - t_ref[...]
        loss_ref[...] += jnp.full(loss_ref.shape, (0.5 / D) * jnp.sum(err * err), F32)
        dy = err * (1.0 / D)
        dgf_ref[...] += jnp.sum(dy * xh4, axis=0, keepdims=True)
        dyg = dy * gf
        dh4 = r4 * (dyg - xh4 * jnp.mean(dyg * xh4, axis=-1, keepdims=True))
        dpp_ref[...] = (dh4 * gate).astype(BF16)
        dgl = (dh4 * pp * gate * (1.0 - gate)).astype(BF16)
        dgl_ref[...] = dgl
        dn4 = _dot_nt(dgl, wg_ref[...])
        dgp_ref[...] += jnp.sum(dn4 * xh3, axis=0, keepdims=True)
        dn4g = dn4 * gp
        dh_ref[...] = dh4 + r3 * (dn4g - xh3 * jnp.mean(dn4g * xh3, axis=-1, keepdims=True))

    row = lambda i: (i, 0)
    one = lambda i: (0, 0)
    return pl.pallas_call(
        body, name="tail", grid=(T // tm,),
        in_specs=[pl.BlockSpec((tm, D), row), pl.BlockSpec((tm, PD), row), pl.BlockSpec((tm, D), row),
                  pl.BlockSpec((1, D), one), pl.BlockSpec((1, D), one),
                  pl.BlockSpec((D, D), one), pl.BlockSpec((PD, D), one)],
        out_specs=[pl.BlockSpec((1, LANES), one), pl.BlockSpec((tm, D), row), pl.BlockSpec((tm, D), row),
                   pl.BlockSpec((tm, D), row), pl.BlockSpec((tm, D), row),
                   pl.BlockSpec((1, D), one), pl.BlockSpec((1, D), one)],
        out_shape=[jax.ShapeDtypeStruct((1, LANES), F32), jax.ShapeDtypeStruct((T, D), F32),
                   jax.ShapeDtypeStruct((T, D), BF16), jax.ShapeDtypeStruct((T, D), BF16),
                   jax.ShapeDtypeStruct((T, D), BF16),
                   jax.ShapeDtypeStruct((1, D), F32), jax.ShapeDtypeStruct((1, D), F32)],
        compiler_params=_params("arbitrary"),
    )(h3, p, target, g_ple, g_fin, w_gate, w_proj)


FFN1_W = ("ffn1_w_in", "ffn1_w_out")
MIX_W = ("w_mix_in", "w_mix_out")
REST_W = ("ffn2_w_in", "ffn2_w_out", "ple_w_gate", "ple_w_proj")
BIG_W = FFN1_W + MIX_W + REST_W
COLUMN_SHARDED = ("w_mix_in", "ple_w_proj")


class _Traffic:
    def __init__(self, shards):
        self.shards = shards
        self.parts = {}

    @staticmethod
    def _full(name, gathered):
        if name in COLUMN_SHARDED:
            return jnp.transpose(gathered, (1, 0, 2)).reshape(gathered.shape[1], -1)
        if name.endswith("_w_in"):
            return gathered
        return gathered.reshape(-1, gathered.shape[-1])

    @staticmethod
    def _blocks(name, grad):
        name = name.split("/")[0]
        if name in COLUMN_SHARDED:
            return jnp.transpose(grad.reshape(grad.shape[0], N_DEV, -1), (1, 0, 2))
        if name.endswith("_w_in"):
            return grad
        return grad.reshape(N_DEV, -1, grad.shape[-1])

    def gather_now(self, names):
        got = _exchange("gather_" + names[0], [self.shards[n] for n in names], [False] * len(names))
        return self.gathered(names, got)

    def gather_rider(self, names):
        return [self.shards[n] for n in names], [False] * len(names)

    def gathered(self, names, got):
        return {n: self._full(n, g) for n, g in zip(names, got)}

    def scatter_rider(self, grads, gather=()):
        return ([self._blocks(n, g) for n, g in grads.items()] + list(gather),
                [True] * len(grads) + [False] * len(gather))

    def scattered(self, names, got):
        self.parts.update(zip(names, got))
        return got[len(names):]

    def finish(self, grads):
        arrays, flags = self.scatter_rider(grads)
        self.scattered(list(grads), _exchange("scatter_last", arrays, flags))


def _local_step(traffic, x, p, target, g1, gmix, gv, ws, b_t, g2, gple, gfin, pack_small):
    T, D = x.shape
    tm = min(ROW_TILE, T)

    w = traffic.gather_now(FFN1_W)
    h1, n1, G1, U1, a1, *got = _ffn_fwd("ffn1_fwd", x, g1, w["ffn1_w_in"], w["ffn1_w_out"],
                                        rider=traffic.gather_rider(MIX_W))
    w.update(traffic.gathered(MIX_W, got))
    n2, zuv, qkv = _mix_in_fwd(h1, gmix, w["w_mix_in"])
    gm = _gmlp_fwd(zuv, gv, ws, b_t)
    sb, ltot, *got = _attn_fwd(qkv, rider=traffic.gather_rider(REST_W))
    w.update(traffic.gathered(REST_W, got))
    h2 = _mix_out_fwd(h1, gm, sb, w["w_mix_out"])
    h3, n3, G2, U2, a2 = _ffn_fwd("ffn2_fwd", h2, g2, w["ffn2_w_in"], w["ffn2_w_out"])
    loss, dh3, n4, d_gl, d_pp, dg_ple, dg_fin = _tail(h3, p, target, gple, gfin, w["ple_w_gate"], w["ple_w_proj"])

    nb, _, FB = w["ffn1_w_in"].shape
    nh = nb // 2

    tt = min(GRAD_ROW_TILE, T)

    def dw_out(name, a, d_out):
        return _matmul_tn(name, a, d_out, nh, (1, tt, FB), lambda j, t: (j, t, 0), (tt, D), lambda j, t: (t, 0),
                          (nh, FB, D), (1, FB, D), lambda j, t: (j, 0, 0))

    def dense_tn(name, a, b, ncol):
        ka, nbw = a.shape[1], b.shape[1] // ncol
        return _matmul_tn(name, a, b, ncol, (tt, ka), lambda j, t: (t, 0), (tt, nbw), lambda j, t: (t, j),
                          (ka, b.shape[1]), (ka, nbw), lambda j, t: (0, j))

    grads = dict(ple_w_gate=dense_tn("dw_ple_gate", n4, d_gl, 2), ple_w_proj=dense_tn("dw_ple_proj", p, d_pp, 1))
    dh2, dg2, dG2, dU2, dout2 = _ffn_bwd("ffn2_bwd", dh3, h2, g2, G2, U2, w["ffn2_w_in"], w["ffn2_w_out"])
    grads["ffn2_w_in"], = _dw_in("ffn2_dw_in", n3, dG2, dU2, 0, 1)
    grads["ffn2_w_out"] = dw_out("ffn2_dw_out", a2, dout2)
    grads = {n: grads[n] for n in REST_W}

    d_gm, d_sb, dh2_bf = _mix_out_bwd(dh2, w["w_mix_out"])
    grads["w_mix_out"] = jnp.concatenate([dense_tn("dw_mix_out_gm", gm, dh2_bf, 1),
                                          dense_tn("dw_mix_out_sb", sb, dh2_bf, 1)], axis=0)
    dzuv, dgv, dws, db_t = _gmlp_bwd(zuv, d_gm, gv, ws, b_t)
    dq, dk, dv, *got = _attn_bwd(qkv, d_sb, ltot, rider=traffic.scatter_rider(grads))
    traffic.scattered(list(grads), got)
    dqkv = jnp.concatenate([dq, dk, dv], axis=1)
    dw_mi = jnp.concatenate([dense_tn("dw_mix_in_uv", n2, dzuv, 2), dense_tn("dw_mix_in_qkv", n2, dqkv, 3)], axis=1)
    dh1, dgmix, dout1 = _mix_in_bwd(dzuv, dqkv, w["w_mix_in"], h1, gmix, dh2)

    grads = dict(w_mix_in=dw_mi, ffn1_w_out=dw_out("ffn1_dw_out", a1, dout1))
    dx, dg1, dG1, dU1, _, *got = _ffn_bwd("ffn1_bwd", dh1, x, g1, G1, U1, w["ffn1_w_in"], w["ffn1_w_out"],
                                         rider=traffic.scatter_rider(grads))
    traffic.scattered(list(grads), got)
    top, = _dw_in("ffn1_dw_in_top", n1, dG1, dU1, 0, 2)
    small = pack_small(dict(ffn1_norm=dg1, mix_norm=dgmix, gmlp_v_norm=dgv, gmlp_w_s=dws, gmlp_b=jnp.transpose(db_t),
                            ffn2_norm=dg2, ple_norm=dg_ple, final_norm=dg_fin), loss)
    bottom, *got = _dw_in("ffn1_dw_in_bottom", n1, dG1, dU1, 1, 2,
                          rider=traffic.scatter_rider({"ffn1_w_in/0": top}, gather=[small]))
    small_parts, = traffic.scattered(["ffn1_w_in/0"], got)
    traffic.finish({"ffn1_w_in/1": bottom})
    return dx, small_parts


def _peer(d):
    x, y, c = lax.axis_index("x"), lax.axis_index("y"), lax.axis_index("c")
    px = 1 - x if d & 4 else x
    py = 1 - y if d & 2 else y
    pc = 1 - c if d & 1 else c
    return (px, py, pc), 4 * px + 2 * py + pc


N_CHIPS_AWAY = 3


class _ExchangePlan:
    def __init__(self, ins, outs, send, recv, local, scatter):
        self.ins, self.outs, self.send, self.recv, self.local, self.scatter = ins, outs, send, recv, local, scatter
        self.me = _peer(0)[1]

    def _remote(self, t, sem, src, slot, peer):
        return pltpu.make_async_remote_copy(
            src_ref=src, dst_ref=self.outs[t].at[slot], send_sem=self.send.at[t, sem], recv_sem=self.recv.at[t, sem],
            device_id=peer, device_id_type=MESH)

    def _own(self, t):
        src = self.ins[t].at[self.me] if self.scatter[t] else self.ins[t]
        return pltpu.make_async_copy(src, self.outs[t].at[self.me], self.local.at[t])

    def _n_direct(self, t):
        return N_DEV - 1 if self.scatter[t] else N_CHIPS_AWAY + 1

    def _direct(self, t, k):
        if self.scatter[t]:
            peer, slot = _peer(k + 1)
            return self._remote(t, k, self.ins[t].at[slot], self.me, peer)
        return self._remote(t, k, self.ins[t], self.me, _peer(2 * k if k else 1)[0])

    def _relay(self, t, c):
        slot = _peer(2 * c)[1]
        return self._remote(t, N_CHIPS_AWAY + c, self.outs[t].at[slot], slot, _peer(1)[0])

    def start(self):
        for t in range(len(self.ins)):
            self._own(t).start()
            for k in range(self._n_direct(t)):
                self._direct(t, k).start()

    def finish(self):
        gathers = [t for t in range(len(self.ins)) if not self.scatter[t]]
        for t in gathers:
            for c in range(1, N_CHIPS_AWAY + 1):
                self._direct(t, c).wait_recv()
                self._relay(t, c).start()
        for t in range(len(self.ins)):
            self._own(t).wait()
            for k in range(self._n_direct(t)):
                self._direct(t, k).wait_send()
                if self.scatter[t] or k == 0:
                    self._direct(t, k).wait_recv()
        for t in gathers:
            for c in range(1, N_CHIPS_AWAY + 1):
                self._relay(t, c).wait()


def _exchange_shapes(arrays, scatter):
    return [jax.ShapeDtypeStruct(a.shape if sc else (N_DEV,) + a.shape, a.dtype) for a, sc in zip(arrays, scatter)]


def _exchange_sems(n):
    return [pltpu.SemaphoreType.DMA((n, N_DEV - 1)), pltpu.SemaphoreType.DMA((n, N_DEV - 1)),
            pltpu.SemaphoreType.DMA((n,))]


_ANY = pl.BlockSpec(memory_space=pl.ANY)


def _exchange(name, arrays, scatter):
    n = len(arrays)

    def body(*refs):
        plan = _ExchangePlan(refs[:n], refs[n:2 * n], *refs[2 * n:], scatter)
        plan.start()
        plan.finish()

    return pl.pallas_call(
        body, name=name, in_specs=[_ANY] * n, out_specs=[_ANY] * n, out_shape=_exchange_shapes(arrays, scatter),
        scratch_shapes=_exchange_sems(n),
    )(*arrays)


def _pallas(body, rider, *, name, grid, in_specs, out_specs, out_shape, scratch_shapes=(), compiler_params=None):
    if rider is None:
        return pl.pallas_call(body, name=name, grid=grid, in_specs=in_specs, out_specs=out_specs, out_shape=out_shape,
                              scratch_shapes=list(scratch_shapes), compiler_params=compiler_params)
    arrays, scatter = rider
    n, ni, no, ns = len(arrays), len(in_specs), len(out_specs), len(scratch_shapes)

    def carried(*refs):
        ins, r_in = refs[:ni], refs[ni:ni + n]
        outs, r_out = refs[ni + n:ni + n + no], refs[ni + n + no:ni + 2 * n + no]
        scratch, sems = refs[ni + 2 * n + no:ni + 2 * n + no + ns], refs[ni + 2 * n + no + ns:]
        ids = [pl.program_id(ax) for ax in range(len(grid))]
        first = functools.reduce(jnp.logical_and, [i == 0 for i in ids])
        last = functools.reduce(jnp.logical_and, [i == g - 1 for i, g in zip(ids, grid)])

        @pl.when(first)
        def _():
            _ExchangePlan(r_in, r_out, *sems, scatter).start()

        body(*ins, *outs, *scratch)

        @pl.when(last)
        def _():
            _ExchangePlan(r_in, r_out, *sems, scatter).finish()

    call = pl.pallas_call(
        carried, name=name, grid=grid, in_specs=list(in_specs) + [_ANY] * n, out_specs=list(out_specs) + [_ANY] * n,
        out_shape=list(out_shape) + _exchange_shapes(arrays, scatter),
        scratch_shapes=list(scratch_shapes) + _exchange_sems(n), compiler_params=compiler_params)
    return lambda *args: call(*args, *arrays)


def _adamw_math(g, w, m, v):
    m_new = ADAM_B1 * m + (1.0 - ADAM_B1) * g
    v_new = ADAM_B2 * v + (1.0 - ADAM_B2) * (g * g)
    m_hat = m_new / (1.0 - ADAM_B1 ** ADAM_STEP)
    v_hat = v_new / (1.0 - ADAM_B2 ** ADAM_STEP)
    delta = -ADAM_LR * (m_hat / (jnp.sqrt(v_hat) + ADAM_EPS) + ADAM_WD * w)
    return delta, m_new, v_new


def _adamw(name, parts, w, m, v):
    R, C = w.shape
    tr = R
    for cand in (256, 128, 64, 32, 16, 8):
        if R % cand == 0:
            tr = cand
            break

    def body(p_ref, w_ref, m_ref, v_ref, g_ref, d_ref, nm_ref, nv_ref):
        g = p_ref[0].astype(F32)
        for j in range(1, N_DEV):
            g = g + p_ref[j].astype(F32)
        g_ref[...] = g
        d_ref[...], nm_ref[...], nv_ref[...] = _adamw_math(g, w_ref[...], m_ref[...], v_ref[...])

    row = lambda i: (i, 0)
    spec = pl.BlockSpec((tr, C), row)
    return pl.pallas_call(
        body, name=name, grid=(R // tr,),
        in_specs=[pl.BlockSpec((N_DEV, tr, C), lambda i: (0, i, 0)), spec, spec, spec],
        out_specs=[spec] * 4,
        out_shape=[jax.ShapeDtypeStruct((R, C), F32)] * 4,
        compiler_params=_params("arbitrary"),
    )(parts, w, m, v)


def _rows128(a):
    flat = a.reshape(-1, LANES)
    pad = (-flat.shape[0]) % SMALL_ROWS_ALIGN
    return jnp.pad(flat, ((0, pad), (0, 0))) if pad else flat


def _unrows(packed, like):
    n = like.size // LANES
    return packed[:n].reshape(like.shape)


def kernel(x, p, ffn1_norm, ffn1_w_in, ffn1_w_out, mix_norm, w_mix_in, gmlp_v_norm, gmlp_w_s, gmlp_b, w_mix_out, ffn2_norm, ffn2_w_in, ffn2_w_out, ple_norm, ple_w_gate, ple_w_proj, final_norm, loss_target, m_ffn1_norm, m_ffn1_w_in, m_ffn1_w_out, m_mix_norm, m_w_mix_in, m_gmlp_v_norm, m_gmlp_w_s, m_gmlp_b, m_w_mix_out, m_ffn2_norm, m_ffn2_w_in, m_ffn2_w_out, m_ple_norm, m_ple_w_gate, m_ple_w_proj, m_final_norm, v_ffn1_norm, v_ffn1_w_in, v_ffn1_w_out, v_mix_norm, v_w_mix_in, v_gmlp_v_norm, v_gmlp_w_s, v_gmlp_b, v_w_mix_out, v_ffn2_norm, v_ffn2_w_in, v_ffn2_w_out, v_ple_norm, v_ple_w_gate, v_ple_w_proj, v_final_norm):
    names = ["ffn1_norm", "ffn1_w_in", "ffn1_w_out", "mix_norm", "w_mix_in", "gmlp_v_norm", "gmlp_w_s", "gmlp_b",
             "w_mix_out", "ffn2_norm", "ffn2_w_in", "ffn2_w_out", "ple_norm", "ple_w_gate", "ple_w_proj", "final_norm"]
    W = dict(zip(names, [ffn1_norm, ffn1_w_in, ffn1_w_out, mix_norm, w_mix_in, gmlp_v_norm, gmlp_w_s, gmlp_b,
                         w_mix_out, ffn2_norm, ffn2_w_in, ffn2_w_out, ple_norm, ple_w_gate, ple_w_proj, final_norm]))
    M = dict(zip(names, [m_ffn1_norm, m_ffn1_w_in, m_ffn1_w_out, m_mix_norm, m_w_mix_in, m_gmlp_v_norm, m_gmlp_w_s,
                         m_gmlp_b, m_w_mix_out, m_ffn2_norm, m_ffn2_w_in, m_ffn2_w_out, m_ple_norm, m_ple_w_gate,
                         m_ple_w_proj, m_final_norm]))
    V = dict(zip(names, [v_ffn1_norm, v_ffn1_w_in, v_ffn1_w_out, v_mix_norm, v_w_mix_in, v_gmlp_v_norm, v_gmlp_w_s,
                         v_gmlp_b, v_w_mix_out, v_ffn2_norm, v_ffn2_w_in, v_ffn2_w_out, v_ple_norm, v_ple_w_gate,
                         v_ple_w_proj, v_final_norm]))
    small = [n for n in names if n not in BIG_W]
    D = x.shape[-1]

    def pack(src, last):
        return jnp.concatenate([_rows128(src[n]) for n in small] + [last], axis=0)

    offs = [0]
    for n in small:
        offs.append(offs[-1] + _rows128(W[n]).shape[0])

    traffic = _Traffic({n: W[n][0].astype(BF16) for n in BIG_W})
    dx, small_parts = _local_step(
        traffic, x[0], p[0, 0], loss_target[0],
        W["ffn1_norm"], W["mix_norm"], W["gmlp_v_norm"], W["gmlp_w_s"][0], jnp.transpose(W["gmlp_b"][0]),
        W["ffn2_norm"], W["ple_norm"], W["final_norm"].reshape(1, D),
        lambda grads, loss_part: pack(grads, jnp.broadcast_to(loss_part, (SMALL_ROWS_ALIGN, LANES))))

    parts = traffic.parts
    parts["ffn1_w_in"] = jnp.concatenate([parts["ffn1_w_in/0"], parts["ffn1_w_in/1"]], axis=1)
    out = {}
    for n in BIG_W:
        out[n] = _adamw("adamw_" + n, parts[n], W[n][0], M[n][0], V[n][0])
    zeros = jnp.zeros((SMALL_ROWS_ALIGN, LANES), F32)
    sg, sd, sm, sv = _adamw("adamw_small", small_parts, pack(W, zeros), pack(M, zeros), pack(V, zeros))
    for k, n in enumerate(small):
        out[n] = tuple(_unrows(arr[offs[k]:offs[k + 1]], W[n]) for arr in (sg, sd, sm, sv))
    loss = sg[offs[len(small)], 0]

    res = [loss, dx[None]]
    for k in range(4):
        res += [out[n][k].reshape(W[n].shape) for n in names]
    return tuple(res)
```

```python
import functools

import jax
import jax.numpy as jnp
from jax import lax
from jax.experimental import pallas as pl
from jax.experimental.pallas import tpu as pltpu

F32 = jnp.float32
BF16 = jnp.bfloat16
MESH = pl.DeviceIdType.MESH

N_DEV = 8
EPS = 1e-6
ADAM_LR = 0.001
ADAM_B1 = 0.9
ADAM_B2 = 0.999
ADAM_EPS = 1e-08
ADAM_WD = 0.01
ADAM_STEP = 10

GM_WIDTH = 512
GM_HEADS = 4
CHUNK = 128
SB_WIDTH = 512
SB_HEAD_DIM = 64
SB_SCALE = 0.125
LANES = 128
SMALL_ROWS_ALIGN = 8

ROW_TILE = 512
GRAD_ROW_TILE = 2048
DW_IN_ROW_TILE = 4096
DW_IN_FIRST_ROWS = 384
FFN_FWD_ROW_TILE = 1024
ATTN_Q_ROWS = 512
ATTN_BWD_Q_ROWS = 512
ATTN_KEY_BLOCK = 256
ATTN_UNROLL = 2
VMEM_LIMIT = 56 * 1024 * 1024


def _params(*sem):
    return pltpu.CompilerParams(dimension_semantics=sem, vmem_limit_bytes=VMEM_LIMIT)


def _dot(a, b):
    return jnp.dot(a, b, preferred_element_type=F32)


def _dot_nt(a, b):
    return lax.dot_general(a, b, (((1,), (1,)), ((), ())), preferred_element_type=F32)


def _dot_tn(a, b):
    return lax.dot_general(a, b, (((0,), (0,)), ((), ())), preferred_element_type=F32)


def _rms_parts(x):
    r = lax.rsqrt(jnp.mean(x * x, axis=-1, keepdims=True) + EPS)
    return r, x * r


def _rms_bwd(x, g, dy):
    r, xh = _rms_parts(x)
    dyg = dy * g
    dx = r * (dyg - xh * jnp.mean(dyg * xh, axis=-1, keepdims=True))
    return dx, jnp.sum(dy * xh, axis=0, keepdims=True)


def _sigmoid(x):
    return 1.0 / (1.0 + jnp.exp(-x))


_SQRT_HALF = 0.7071067811865476
_INV_SQRT_2PI = 0.3989422804014327


def _gelu(x):
    return 0.5 * x * (1.0 + lax.erf(x * _SQRT_HALF))


def _gelu_grad(x):
    return 0.5 * (1.0 + lax.erf(x * _SQRT_HALF)) + x * (_INV_SQRT_2PI * jnp.exp(-0.5 * x * x))


def _split_bf16(x):
    hi = x.astype(BF16)
    lo = (x - hi.astype(F32)).astype(BF16)
    return hi, lo


def _ffn_fwd(name, h, gain, w_in, w_out, rider=None):
    T, D = h.shape
    nb, _, FB = w_in.shape
    nh = nb // 2
    tm = min(FFN_FWD_ROW_TILE, T)

    def body(h_ref, g_ref, wg_ref, wu_ref, wo_ref, ho_ref, n_ref, G_ref, U_ref, a_ref, n_s, acc):
        jj = pl.program_id(1)

        @pl.when(jj == 0)
        def _():
            _, xh = _rms_parts(h_ref[...])
            n = (xh * g_ref[...]).astype(BF16)
            n_s[...] = n
            n_ref[...] = n
            acc[...] = jnp.zeros_like(acc)

        n = n_s[...]
        G = _dot(n, wg_ref[0])
        U = _dot(n, wu_ref[0])
        G_ref[0] = G.astype(BF16)
        U_ref[0] = U.astype(BF16)
        a = (G * _sigmoid(G) * U).astype(BF16)
        a_ref[0] = a
        acc[...] += _dot(a, wo_ref[...])

        @pl.when(jj == nh - 1)
        def _():
            ho_ref[...] = h_ref[...] + 0.5 * acc[...]

    row = lambda i, j: (i, 0)
    blk = lambda i, j: (j, i, 0)
    return _pallas(
        body, rider, name=name, grid=(T // tm, nh),
        in_specs=[pl.BlockSpec((tm, D), row),
                  pl.BlockSpec((1, D), lambda i, j: (0, 0)),
                  pl.BlockSpec((1, D, FB), lambda i, j: (j, 0, 0)),
                  pl.BlockSpec((1, D, FB), lambda i, j: (j + nh, 0, 0)),
                  pl.BlockSpec((FB, D), lambda i, j: (j, 0))],
        out_specs=[pl.BlockSpec((tm, D), row), pl.BlockSpec((tm, D), row),
                   pl.BlockSpec((1, tm, FB), blk), pl.BlockSpec((1, tm, FB), blk),
                   pl.BlockSpec((1, tm, FB), blk)],
        out_shape=[jax.ShapeDtypeStruct((T, D), F32), jax.ShapeDtypeStruct((T, D), BF16),
                   jax.ShapeDtypeStruct((nh, T, FB), BF16), jax.ShapeDtypeStruct((nh, T, FB), BF16),
                   jax.ShapeDtypeStruct((nh, T, FB), BF16)],
        scratch_shapes=[pltpu.VMEM((tm, D), BF16), pltpu.VMEM((tm, D), F32)],
        compiler_params=_params("arbitrary", "arbitrary"),
    )(h, gain, w_in, w_in, w_out)


def _ffn_bwd(name, dh, h_in, gain, G, U, w_in, w_out, rider=None):
    T, D = dh.shape
    nb, _, FB = w_in.shape
    nh = nb // 2
    tm = min(ROW_TILE, T)

    def body(dh_ref, h_ref, g_ref, G_ref, U_ref, wg_ref, wu_ref, wo_ref,
             dhin_ref, dg_ref, dG_ref, dU_ref, do_ref, dn_acc, do_s):
        i = pl.program_id(0)
        jj = pl.program_id(1)

        @pl.when(jj == 0)
        def _():
            d_out = (0.5 * dh_ref[...]).astype(BF16)
            do_s[...] = d_out
            do_ref[...] = d_out
            dn_acc[...] = jnp.zeros_like(dn_acc)

        @pl.when((i == 0) & (jj == 0))
        def _():
            dg_ref[...] = jnp.zeros_like(dg_ref)

        halves = [slice(0, tm // 2), slice(tm // 2, tm)]
        da = [_dot_nt(do_s[rows, :], wo_ref[...]) for rows in halves]
        dGU = []
        for rows, dav in zip(halves, da):
            Gv = G_ref[0, rows, :].astype(F32)
            Uv = U_ref[0, rows, :].astype(F32)
            sig = _sigmoid(Gv)
            dU = (dav * (Gv * sig)).astype(BF16)
            dG = (dav * Uv * (sig * (1.0 + Gv * (1.0 - sig)))).astype(BF16)
            dG_ref[0, rows, :] = dG
            dU_ref[0, rows, :] = dU
            dGU.append((dG, dU))
        dn = [_dot_nt(dG, wg_ref[0]) for dG, _ in dGU]
        dn = [d + _dot_nt(dU, wu_ref[0]) for d, (_, dU) in zip(dn, dGU)]
        for rows, d in zip(halves, dn):
            dn_acc[rows, :] += d

        @pl.when(jj == nh - 1)
        def _():
            dx, dg = _rms_bwd(h_ref[...], g_ref[...], dn_acc[...])
            dhin_ref[...] = dh_ref[...] + dx
            dg_ref[...] += dg

    row = lambda i, j: (i, 0)
    blk = lambda i, j: (j, i, 0)
    one = lambda i, j: (0, 0)
    return _pallas(
        body, rider, name=name, grid=(T // tm, nh),
        in_specs=[pl.BlockSpec((tm, D), row), pl.BlockSpec((tm, D), row), pl.BlockSpec((1, D), one),
                  pl.BlockSpec((1, tm, FB), blk), pl.BlockSpec((1, tm, FB), blk),
                  pl.BlockSpec((1, D, FB), lambda i, j: (j, 0, 0)),
                  pl.BlockSpec((1, D, FB), lambda i, j: (j + nh, 0, 0)),
                  pl.BlockSpec((FB, D), lambda i, j: (j, 0))],
        out_specs=[pl.BlockSpec((tm, D), row), pl.BlockSpec((1, D), one),
                   pl.BlockSpec((1, tm, FB), blk), pl.BlockSpec((1, tm, FB), blk),
                   pl.BlockSpec((tm, D), row)],
        out_shape=[jax.ShapeDtypeStruct((T, D), F32), jax.ShapeDtypeStruct((1, D), F32),
                   jax.ShapeDtypeStruct((nh, T, FB), BF16), jax.ShapeDtypeStruct((nh, T, FB), BF16),
                   jax.ShapeDtypeStruct((T, D), BF16)],
        scratch_shapes=[pltpu.VMEM((tm, D), F32), pltpu.VMEM((tm, D), BF16)],
        compiler_params=_params("arbitrary", "arbitrary"),
    )(dh, h_in, gain, G, U, w_in, w_in, w_out)


def _matmul_tn(name, a, b, nj, a_block, a_map, b_block, b_map, out_shape, out_block, out_map):
    T = a.shape[-2]
    tt = a_block[-2]
    nt = T // tt
    kb, nbk = out_block[-2], out_block[-1]

    def body(a_ref, b_ref, o_ref, acc):
        t = pl.program_id(1)

        @pl.when(t == 0)
        def _():
            acc[...] = jnp.zeros_like(acc)

        av = (a_ref[0] if len(a_block) == 3 else a_ref[...]).astype(BF16)
        bv = b_ref[0] if len(b_block) == 3 else b_ref[...]
        acc[...] += _dot_tn(av, bv)

        @pl.when(t == nt - 1)
        def _():
            if len(out_block) == 3:
                o_ref[0] = acc[...].astype(o_ref.dtype)
            else:
                o_ref[...] = acc[...].astype(o_ref.dtype)

    return pl.pallas_call(
        body, name=name, grid=(nj, nt),
        in_specs=[pl.BlockSpec(a_block, a_map), pl.BlockSpec(b_block, b_map)],
        out_specs=pl.BlockSpec(out_block, out_map),
        out_shape=jax.ShapeDtypeStruct(out_shape, BF16),
        scratch_shapes=[pltpu.VMEM((kb, nbk), F32)],
        compiler_params=_params("arbitrary", "arbitrary"),
    )(a, b)


def _dw_in(name, n, dG, dU, rider=None):
    T, kr = n.shape
    nh, _, FB = dG.shape
    tt = min(DW_IN_ROW_TILE, T)
    nt = T // tt
    cut = LANES * ((kr // LANES + 1) // 2)

    def body(n_ref, dg_ref, du_ref, o_ref, acc):
        j = pl.program_id(0)
        t = pl.program_id(1)

        @pl.when(t == 0)
        def _():
            acc[...] = jnp.zeros_like(acc)

        def add(dz_ref):
            for rows in (slice(0, cut), slice(cut, kr)):
                acc[rows, :] += _dot_tn(n_ref[:, rows], dz_ref[0])

        @pl.when(j < nh)
        def _():
            add(dg_ref)

        @pl.when(j >= nh)
        def _():
            add(du_ref)

        @pl.when(t == nt - 1)
        def _():
            o_ref[0] = acc[...].astype(BF16)

    return _pallas(
        body, rider, name=name, grid=(2 * nh, nt),
        in_specs=[pl.BlockSpec((tt, kr), lambda j, t: (t, 0)),
                  pl.BlockSpec((1, tt, FB), lambda j, t: (jnp.minimum(j, nh - 1), t, 0)),
                  pl.BlockSpec((1, tt, FB), lambda j, t: (jnp.maximum(j - nh, 0), t, 0))],
        out_specs=[pl.BlockSpec((1, kr, FB), lambda j, t: (j, 0, 0))],
        out_shape=[jax.ShapeDtypeStruct((2 * nh, kr, FB), BF16)],
        scratch_shapes=[pltpu.VMEM((kr, FB), F32)],
        compiler_params=_params("arbitrary", "arbitrary"),
    )(n, dG, dU)


def _mix_in_fwd(h, gain, w):
    T, D = h.shape
    W = w.shape[1]
    nuv = 2 * GM_WIDTH
    tm = min(ROW_TILE, T)

    def body(h_ref, g_ref, w_ref, n_ref, zuv_ref, qkv_ref):
        _, xh = _rms_parts(h_ref[...])
        n = (xh * g_ref[...]).astype(BF16)
        n_ref[...] = n
        z = _dot(n, w_ref[...])
        zuv_ref[...] = z[:, :nuv]
        qkv_ref[...] = z[:, nuv:].astype(BF16)

    row = lambda i: (i, 0)
    return pl.pallas_call(
        body, name="mix_in_fwd", grid=(T // tm,),
        in_specs=[pl.BlockSpec((tm, D), row), pl.BlockSpec((1, D), lambda i: (0, 0)),
                  pl.BlockSpec((D, W), lambda i: (0, 0))],
        out_specs=[pl.BlockSpec((tm, D), row), pl.BlockSpec((tm, nuv), row),
                   pl.BlockSpec((tm, W - nuv), row)],
        out_shape=[jax.ShapeDtypeStruct((T, D), BF16), jax.ShapeDtypeStruct((T, nuv), F32),
                   jax.ShapeDtypeStruct((T, W - nuv), BF16)],
        compiler_params=_params("arbitrary"),
    )(h, gain, w)


def _mix_in_bwd(dzuv, dqkv, w, h, gain, dh):
    T, D = h.shape
    W = w.shape[1]
    nuv = dzuv.shape[1]
    tm = min(ROW_TILE, T)

    def body(dzuv_ref, dqkv_ref, w_ref, h_ref, g_ref, dh_ref, dhin_ref, dg_ref, half_ref):
        @pl.when(pl.program_id(0) == 0)
        def _():
            dg_ref[...] = jnp.zeros_like(dg_ref)

        dn = _dot_nt(dzuv_ref[...], w_ref[:, :nuv]) + _dot_nt(dqkv_ref[...], w_ref[:, nuv:])
        dx, dg = _rms_bwd(h_ref[...], g_ref[...], dn)
        dh_in = dh_ref[...] + dx
        dhin_ref[...] = dh_in
        half_ref[...] = (0.5 * dh_in).astype(BF16)
        dg_ref[...] += dg

    row = lambda i: (i, 0)
    one = lambda i: (0, 0)
    return pl.pallas_call(
        body, name="mix_in_bwd", grid=(T // tm,),
        in_specs=[pl.BlockSpec((tm, nuv), row), pl.BlockSpec((tm, W - nuv), row),
                  pl.BlockSpec((D, W), one), pl.BlockSpec((tm, D), row), pl.BlockSpec((1, D), one),
                  pl.BlockSpec((tm, D), row)],
        out_specs=[pl.BlockSpec((tm, D), row), pl.BlockSpec((1, D), one), pl.BlockSpec((tm, D), row)],
        out_shape=[jax.ShapeDtypeStruct((T, D), F32), jax.ShapeDtypeStruct((1, D), F32),
                   jax.ShapeDtypeStruct((T, D), BF16)],
        compiler_params=_params("arbitrary"),
    )(dzuv, dqkv, w, h, gain, dh)


def _gmlp_norm(zv, gv):
    v = _gelu(zv)
    r, vh = _rms_parts(v)
    return r, vh, (vh * gv).astype(BF16)


def _causal_ws(ws_ref, hd):
    r = lax.broadcasted_iota(jnp.int32, (CHUNK, CHUNK), 0)
    c = lax.broadcasted_iota(jnp.int32, (CHUNK, CHUNK), 1)
    return jnp.where(r >= c, ws_ref[hd], 0.0).astype(BF16)


def _gmlp_fwd(zuv, gv, ws, b_t):
    T = zuv.shape[0]
    tg = min(ROW_TILE, T)

    def body(zu_ref, zv_ref, gv_ref, ws_ref, bt_ref, o_ref):
        u = _gelu(zu_ref[...])
        _, _, vn = _gmlp_norm(zv_ref[...], gv_ref[...])
        for hd in range(GM_HEADS):
            wc = _causal_ws(ws_ref, hd)
            cols = slice(hd * CHUNK, (hd + 1) * CHUNK)
            for c in range(tg // CHUNK):
                rows = slice(c * CHUNK, (c + 1) * CHUNK)
                sv = _dot(wc, vn[rows, cols]) + bt_ref[:, hd:hd + 1]
                o_ref[rows, cols] = (u[rows, cols] * sv).astype(BF16)

    return pl.pallas_call(
        body, name="gmlp_fwd", grid=(T // tg,),
        in_specs=[pl.BlockSpec((tg, GM_WIDTH), lambda i: (i, 0)), pl.BlockSpec((tg, GM_WIDTH), lambda i: (i, 1)),
                  pl.BlockSpec((1, GM_WIDTH), lambda i: (0, 0)),
                  pl.BlockSpec((GM_HEADS, CHUNK, CHUNK), lambda i: (0, 0, 0)),
                  pl.BlockSpec((CHUNK, GM_HEADS), lambda i: (0, 0))],
        out_specs=pl.BlockSpec((tg, GM_WIDTH), lambda i: (i, 0)),
        out_shape=jax.ShapeDtypeStruct((T, GM_WIDTH), BF16),
        compiler_params=_params("arbitrary"),
    )(zuv, zuv, gv, ws, b_t)


def _gmlp_bwd(zuv, d_gm, gv, ws, b_t):
    T = zuv.shape[0]
    tg = min(ROW_TILE, T)
    ng = T // tg

    def body(zu_ref, zv_ref, dgm_ref, gv_ref, ws_ref, bt_ref, dz_ref, dgv_ref, dws_ref, dbt_ref, dsv_acc, dvn_s):
        i = pl.program_id(0)

        @pl.when(i == 0)
        def _():
            dgv_ref[...] = jnp.zeros_like(dgv_ref)
            dws_ref[...] = jnp.zeros_like(dws_ref)
            dsv_acc[...] = jnp.zeros_like(dsv_acc)

        zu = zu_ref[...]
        zv = zv_ref[...]
        dgm = dgm_ref[...]
        gvv = gv_ref[...]
        u = _gelu(zu)
        rv, vh, vn = _gmlp_norm(zv, gvv)
        dsv = dgm * u
        dsv_b = dsv.astype(BF16)
        for hd in range(GM_HEADS):
            wc = _causal_ws(ws_ref, hd)
            cols = slice(hd * CHUNK, (hd + 1) * CHUNK)
            dws = jnp.zeros((CHUNK, CHUNK), F32)
            dsv_sum = jnp.zeros((CHUNK, CHUNK), F32)
            for c in range(tg // CHUNK):
                rows = slice(c * CHUNK, (c + 1) * CHUNK)
                vch = vn[rows, cols]
                sv = _dot(wc, vch) + bt_ref[:, hd:hd + 1]
                dz_ref[rows, cols] = (dgm[rows, cols] * sv * _gelu_grad(zu[rows, cols])).astype(BF16)
                dws += _dot_nt(dsv_b[rows, cols], vch)
                dsv_sum += dsv[rows, cols]
                dvn_s[rows, cols] = _dot_tn(wc, dsv_b[rows, cols])
            dws_ref[hd] += dws
            dsv_acc[:, cols] += dsv_sum
        dvn = dvn_s[...]
        dvh = dvn * gvv
        dv = rv * (dvh - vh * jnp.mean(dvh * vh, axis=-1, keepdims=True))
        dgv_ref[...] += jnp.sum(dvn * vh, axis=0, keepdims=True)
        dz_ref[:, GM_WIDTH:] = (dv * _gelu_grad(zv)).astype(BF16)

        @pl.when(i == ng - 1)
        def _():
            r = lax.broadcasted_iota(jnp.int32, (CHUNK, CHUNK), 0)
            c = lax.broadcasted_iota(jnp.int32, (CHUNK, CHUNK), 1)
            for hd in range(GM_HEADS):
                dws_ref[hd] = jnp.where(r >= c, dws_ref[hd], 0.0)
                dbt_ref[:, hd:hd + 1] = jnp.sum(dsv_acc[:, hd * CHUNK:(hd + 1) * CHUNK], axis=1, keepdims=True)

    return pl.pallas_call(
        body, name="gmlp_bwd", grid=(ng,),
        in_specs=[pl.BlockSpec((tg, GM_WIDTH), lambda i: (i, 0)), pl.BlockSpec((tg, GM_WIDTH), lambda i: (i, 1)),
                  pl.BlockSpec((tg, GM_WIDTH), lambda i: (i, 0)),
                  pl.BlockSpec((1, GM_WIDTH), lambda i: (0, 0)),
                  pl.BlockSpec((GM_HEADS, CHUNK, CHUNK), lambda i: (0, 0, 0)),
                  pl.BlockSpec((CHUNK, GM_HEADS), lambda i: (0, 0))],
        out_specs=[pl.BlockSpec((tg, 2 * GM_WIDTH), lambda i: (i, 0)),
                   pl.BlockSpec((1, GM_WIDTH), lambda i: (0, 0)),
                   pl.BlockSpec((GM_HEADS, CHUNK, CHUNK), lambda i: (0, 0, 0)),
                   pl.BlockSpec((CHUNK, GM_HEADS), lambda i: (0, 0))],
        out_shape=[jax.ShapeDtypeStruct((T, 2 * GM_WIDTH), BF16), jax.ShapeDtypeStruct((1, GM_WIDTH), F32),
                   jax.ShapeDtypeStruct((GM_HEADS, CHUNK, CHUNK), F32),
                   jax.ShapeDtypeStruct((CHUNK, GM_HEADS), F32)],
        scratch_shapes=[pltpu.VMEM((CHUNK, GM_WIDTH), F32), pltpu.VMEM((tg, GM_WIDTH), F32)],
        compiler_params=_params("arbitrary"),
    )(zuv, zuv, d_gm, gv, ws, b_t)


def _scan_matrix(blk, keep):
    r = lax.broadcasted_iota(jnp.int32, (blk, blk), 0)
    c = lax.broadcasted_iota(jnp.int32, (blk, blk), 1)
    return jnp.where(keep(r, c), 1.0, 0.0).astype(BF16)


def _scan_matrix2(blk, keep, value):
    m = _scan_matrix(blk, keep) * value
    return jnp.concatenate([m, m], axis=0)


def _scan(x, mat2):
    hi, lo = _split_bf16(x)
    return _dot(jnp.concatenate([hi, lo], axis=1), mat2)


def _head_masks(q):
    lane = lax.broadcasted_iota(jnp.int32, q.shape, 1)
    m0 = lane < SB_HEAD_DIM
    zero = jnp.zeros_like(q)
    return m0, jnp.where(m0, q, zero), jnp.where(m0, zero, q)


_LOG2E = 1.4426950408889634


def _softplus_parts(z):
    e = jnp.exp2(jnp.abs(z) * (-_LOG2E))
    ope = 1.0 + e
    return e, ope, jnp.maximum(z, 0.0) + jnp.log(ope)


def _attn_fwd(qkv, rider=None):
    T = qkv.shape[0]
    tk = ATTN_KEY_BLOCK
    tq = min(ATTN_Q_ROWS, T)
    band = tq // tk
    assert band % ATTN_UNROLL == 0 or T == tq
    ngrp = SB_WIDTH // LANES

    def body(q_ref, k_ref, v_ref, o_ref, l_ref, acc, run):
        i = pl.program_id(1)
        suffix = _scan_matrix2(tk, lambda r, c: r >= c, -1.0)
        row = lax.broadcasted_iota(jnp.int32, (tq, tk), 0)
        col = lax.broadcasted_iota(jnp.int32, (tq, tk), 1)
        m0, q0, q1 = _head_masks(q_ref[...] * SB_SCALE)
        acc[...] = jnp.zeros_like(acc)
        run[...] = jnp.zeros_like(run)

        def tiles(work):
            heads = (q0, q1)
            kv = []
            for j, _ in work:
                start = pl.multiple_of(j * tk, tk)
                kv.append((k_ref[pl.ds(start, tk), :], v_ref[pl.ds(start, tk), :]))
            z = [[_dot_nt(qh, kj) for qh in heads] for kj, _ in kv]
            sp = [[_softplus_parts(zz)[2] for zz in zt] for zt in z]
            sp = [[s if m is None else jnp.where(m, s, 0.0) for s in st] for st, (_, m) in zip(sp, work)]
            res = [[_scan(s, suffix) for s in st] for st in sp]
            runs = [run[hd] for hd in range(len(heads))]
            a = []
            for t, (_, m) in enumerate(work):
                at = []
                for hd in range(len(heads)):
                    av = jnp.exp(z[t][hd] + (runs[hd] + res[t][hd]))
                    at.append(av if m is None else jnp.where(m, av, 0.0))
                    runs[hd] = runs[hd] + res[t][hd][:, 0:1]
                a.append(at)
            for hd in range(len(heads)):
                run[hd] = runs[hd]
                upd = _dot(a[0][hd].astype(BF16), kv[0][1])
                for t in range(1, len(work)):
                    upd = upd + _dot(a[t][hd].astype(BF16), kv[t][1])
                acc[hd] += upd

        tiles([(i * band + jb, jb * tk + col < row) for jb in reversed(range(band))])

        def full_step(it, carry):
            tiles([(i * band - 1 - ATTN_UNROLL * it - u, None) for u in range(ATTN_UNROLL)])
            return carry

        lax.fori_loop(0, i * (band // ATTN_UNROLL), full_step, 0)
        o_ref[...] = jnp.where(m0, acc[0], acc[1]).astype(BF16)
        l_ref[...] = jnp.where(m0, jnp.broadcast_to(run[0], (tq, LANES)), jnp.broadcast_to(run[1], (tq, LANES)))

    return _pallas(
        body, rider, name="attn_fwd", grid=(ngrp, T // tq),
        in_specs=[pl.BlockSpec((tq, LANES), lambda g, i: (i, g)),
                  pl.BlockSpec((T, LANES), lambda g, i: (0, ngrp + g)),
                  pl.BlockSpec((T, LANES), lambda g, i: (0, 2 * ngrp + g))],
        out_specs=[pl.BlockSpec((tq, LANES), lambda g, i: (i, g)),
                   pl.BlockSpec((tq, LANES), lambda g, i: (i, g))],
        out_shape=[jax.ShapeDtypeStruct((T, SB_WIDTH), BF16), jax.ShapeDtypeStruct((T, SB_WIDTH), F32)],
        scratch_shapes=[pltpu.VMEM((2, tq, LANES), F32), pltpu.VMEM((2, tq, 1), F32)],
        compiler_params=_params("arbitrary", "arbitrary"),
    )(qkv, qkv, qkv)


def _attn_bwd(qkv, d_o, ltot, rider=None):
    T = qkv.shape[0]
    tk = ATTN_KEY_BLOCK
    tq = min(ATTN_BWD_Q_ROWS, T)
    band = tq // tk
    nq = T // tq
    ngrp = SB_WIDTH // LANES

    def body(q_ref, k_ref, v_ref, do_ref, l_ref, dq_ref, dk_ref, dv_ref, dq_acc, dk_acc, dv_acc, lpre, ppre):
        i = pl.program_id(1)

        @pl.when(i == 0)
        def _():
            dk_acc[...] = jnp.zeros_like(dk_acc)
            dv_acc[...] = jnp.zeros_like(dv_acc)

        excl = _scan_matrix(tk, lambda r, c: r < c)
        excl2 = jnp.concatenate([excl, excl], axis=0)
        row = lax.broadcasted_iota(jnp.int32, (tq, tk), 0)
        col = lax.broadcasted_iota(jnp.int32, (tq, tk), 1)
        m0, q0, q1 = _head_masks(q_ref[...] * SB_SCALE)
        _, d0, d1 = _head_masks(do_ref[...].astype(BF16))
        lt = l_ref[...]
        ltots = (lt[:, 0:1], lt[:, SB_HEAD_DIM:SB_HEAD_DIM + 1])
        dq_acc[...] = jnp.zeros_like(dq_acc)
        lpre[...] = jnp.zeros_like(lpre)
        ppre[...] = jnp.zeros_like(ppre)

        def tiles(work):
            heads = ((q0, d0), (q1, d1))
            nhd = len(heads)
            starts = [pl.multiple_of(j * tk, tk) for j, _ in work]
            kv = [(k_ref[pl.ds(st, tk), :], v_ref[pl.ds(st, tk), :]) for st in starts]
            masks = [m for _, m in work]
            every = [(t, hd) for t in range(len(work)) for hd in range(nhd)]
            z = {(t, hd): _dot_nt(heads[hd][0], kv[t][0]) for t, hd in every}
            da = {(t, hd): _dot_nt(heads[hd][1], kv[t][1]) for t, hd in every}
            sp, beta = {}, {}
            for key in every:
                s = _softplus_parts(z[key])[2]
                beta[key] = jnp.exp(z[key] - s)
                sp[key] = s if masks[key[0]] is None else jnp.where(masks[key[0]], s, 0.0)
            res = {key: _scan(sp[key], excl2) for key in every}
            lp = [lpre[hd] for hd in range(nhd)]
            a, p = {}, {}
            for t, hd in every:
                av = jnp.exp(z[t, hd] + ((ltots[hd] + lp[hd]) + res[t, hd]))
                a[t, hd] = av if masks[t] is None else jnp.where(masks[t], av, 0.0)
                p[t, hd] = a[t, hd] * da[t, hd]
                lp[hd] = lp[hd] + (res[t, hd][:, tk - 1:tk] + sp[t, hd][:, tk - 1:tk])
            resp = {key: _dot(p[key].astype(BF16), excl) for key in every}
            pp = [ppre[hd] for hd in range(nhd)]
            dzb = {}
            for t, hd in every:
                dz = p[t, hd] - beta[t, hd] * (p[t, hd] + (pp[hd] + resp[t, hd]))
                if masks[t] is not None:
                    dz = jnp.where(masks[t], dz, 0.0)
                dzb[t, hd] = dz.astype(BF16)
                pp[hd] = pp[hd] + (resp[t, hd][:, tk - 1:tk] + p[t, hd][:, tk - 1:tk])
            for hd in range(nhd):
                lpre[hd] = lp[hd]
                ppre[hd] = pp[hd]
                upd = _dot(dzb[0, hd], kv[0][0])
                for t in range(1, len(work)):
                    upd = upd + _dot(dzb[t, hd], kv[t][0])
                dq_acc[hd] += upd
            for t, st in enumerate(starts):
                dk = _dot_tn(dzb[t, 0], heads[0][0])
                dv = _dot_tn(a[t, 0].astype(BF16), heads[0][1])
                for hd in range(1, nhd):
                    dk = dk + _dot_tn(dzb[t, hd], heads[hd][0])
                    dv = dv + _dot_tn(a[t, hd].astype(BF16), heads[hd][1])
                dk_acc[pl.ds(st, tk), :] += dk
                dv_acc[pl.ds(st, tk), :] += dv

        def full_step(j, carry):
            tiles([(j, None)])
            return carry

        lax.fori_loop(0, i * band, full_step, 0)
        for jb in range(band):
            tiles([(i * band + jb, jb * tk + col < row)])
        dq_ref[...] = (jnp.where(m0, dq_acc[0], dq_acc[1]) * SB_SCALE).astype(BF16)

        @pl.when(i == nq - 1)
        def _():
            dk_ref[...] = dk_acc[...].astype(BF16)
            dv_ref[...] = dv_acc[...].astype(BF16)

    qmap = lambda g, i: (i, g)
    return _pallas(
        body, rider, name="attn_bwd", grid=(ngrp, nq),
        in_specs=[pl.BlockSpec((tq, LANES), qmap),
                  pl.BlockSpec((T, LANES), lambda g, i: (0, ngrp + g)),
                  pl.BlockSpec((T, LANES), lambda g, i: (0, 2 * ngrp + g)),
                  pl.BlockSpec((tq, LANES), qmap), pl.BlockSpec((tq, LANES), qmap)],
        out_specs=[pl.BlockSpec((tq, LANES), qmap),
                   pl.BlockSpec((T, LANES), lambda g, i: (0, g)),
                   pl.BlockSpec((T, LANES), lambda g, i: (0, g))],
        out_shape=[jax.ShapeDtypeStruct((T, SB_WIDTH), BF16)] * 3,
        scratch_shapes=[pltpu.VMEM((2, tq, LANES), F32), pltpu.VMEM((T, LANES), F32),
                        pltpu.VMEM((T, LANES), F32), pltpu.VMEM((2, tq, 1), F32),
                        pltpu.VMEM((2, tq, 1), F32)],
        compiler_params=_params("arbitrary", "arbitrary"),
    )(qkv, qkv, qkv, d_o, ltot)


def _mix_out_fwd(h, gm, sb, w):
    T, D = h.shape
    tm = min(ROW_TILE, T)

    def body(h_ref, gm_ref, sb_ref, w_ref, o_ref):
        o_ref[...] = h_ref[...] + _dot(gm_ref[...], w_ref[:GM_WIDTH, :]) + _dot(sb_ref[...], w_ref[GM_WIDTH:, :])

    row = lambda i: (i, 0)
    return pl.pallas_call(
        body, name="mix_out_fwd", grid=(T // tm,),
        in_specs=[pl.BlockSpec((tm, D), row), pl.BlockSpec((tm, GM_WIDTH), row), pl.BlockSpec((tm, SB_WIDTH), row),
                  pl.BlockSpec((GM_WIDTH + SB_WIDTH, D), lambda i: (0, 0))],
        out_specs=pl.BlockSpec((tm, D), row),
        out_shape=jax.ShapeDtypeStruct((T, D), F32),
        compiler_params=_params("arbitrary"),
    )(h, gm, sb, w)


def _mix_out_bwd(dh, w):
    T, D = dh.shape
    tm = min(ROW_TILE, T)

    def body(dh_ref, w_ref, dgm_ref, dsb_ref, dhb_ref):
        dhb = dh_ref[...].astype(BF16)
        dhb_ref[...] = dhb
        dgm_ref[...] = _dot_nt(dhb, w_ref[:GM_WIDTH, :])
        dsb_ref[...] = _dot_nt(dhb, w_ref[GM_WIDTH:, :])

    row = lambda i: (i, 0)
    return pl.pallas_call(
        body, name="mix_out_bwd", grid=(T // tm,),
        in_specs=[pl.BlockSpec((tm, D), row), pl.BlockSpec((GM_WIDTH + SB_WIDTH, D), lambda i: (0, 0))],
        out_specs=[pl.BlockSpec((tm, GM_WIDTH), row), pl.BlockSpec((tm, SB_WIDTH), row), pl.BlockSpec((tm, D), row)],
        out_shape=[jax.ShapeDtypeStruct((T, GM_WIDTH), F32), jax.ShapeDtypeStruct((T, SB_WIDTH), F32),
                   jax.ShapeDtypeStruct((T, D), BF16)],
        compiler_params=_params("arbitrary"),
    )(dh, w)


def _tail(h3, p, target, g_ple, g_fin, w_gate, w_proj):
    T, D = h3.shape
    PD = p.shape[1]
    tm = min(ROW_TILE, T)

    def body(h_ref, p_ref, t_ref, gp_ref, gf_ref, wg_ref, wp_ref,
             loss_ref, dh_ref, n4_ref, dgl_ref, dpp_ref, dgp_ref, dgf_ref):
        @pl.when(pl.program_id(0) == 0)
        def _():
            loss_ref[...] = jnp.zeros_like(loss_ref)
            dgp_ref[...] = jnp.zeros_like(dgp_ref)
            dgf_ref[...] = jnp.zeros_like(dgf_ref)

        h3v = h_ref[...]
        gp = gp_ref[...]
        gf = gf_ref[...]
        r3, xh3 = _rms_parts(h3v)
        n4 = (xh3 * gp).astype(BF16)
        n4_ref[...] = n4
        gate = _sigmoid(_dot(n4, wg_ref[...]))
        pp = _dot(p_ref[...].astype(BF16), wp_ref[...])
        h4 = h3v + gate * pp
        r4, xh4 = _rms_parts(h4)
        err = xh4 * gf - t_ref[...]
        loss_ref[...] += jnp.full(loss_ref.shape, (0.5 / D) * jnp.sum(err * err), F32)
        dy = err * (1.0 / D)
        dgf_ref[...] += jnp.sum(dy * xh4, axis=0, keepdims=True)
        dyg = dy * gf
        dh4 = r4 * (dyg - xh4 * jnp.mean(dyg * xh4, axis=-1, keepdims=True))
        dpp_ref[...] = (dh4 * gate).astype(BF16)
        dgl = (dh4 * pp * gate * (1.0 - gate)).astype(BF16)
        dgl_ref[...] = dgl
        dn4 = _dot_nt(dgl, wg_ref[...])
        dgp_ref[...] += jnp.sum(dn4 * xh3, axis=0, keepdims=True)
        dn4g = dn4 * gp
        dh_ref[...] = dh4 + r3 * (dn4g - xh3 * jnp.mean(dn4g * xh3, axis=-1, keepdims=True))

    row = lambda i: (i, 0)
    one = lambda i: (0, 0)
    return pl.pallas_call(
        body, name="tail", grid=(T // tm,),
        in_specs=[pl.BlockSpec((tm, D), row), pl.BlockSpec((tm, PD), row), pl.BlockSpec((tm, D), row),
                  pl.BlockSpec((1, D), one), pl.BlockSpec((1, D), one),
                  pl.BlockSpec((D, D), one), pl.BlockSpec((PD, D), one)],
        out_specs=[pl.BlockSpec((1, LANES), one), pl.BlockSpec((tm, D), row), pl.BlockSpec((tm, D), row),
                   pl.BlockSpec((tm, D), row), pl.BlockSpec((tm, D), row),
                   pl.BlockSpec((1, D), one), pl.BlockSpec((1, D), one)],
        out_shape=[jax.ShapeDtypeStruct((1, LANES), F32), jax.ShapeDtypeStruct((T, D), F32),
                   jax.ShapeDtypeStruct((T, D), BF16), jax.ShapeDtypeStruct((T, D), BF16),
                   jax.ShapeDtypeStruct((T, D), BF16),
                   jax.ShapeDtypeStruct((1, D), F32), jax.ShapeDtypeStruct((1, D), F32)],
        compiler_params=_params("arbitrary"),
    )(h3, p, target, g_ple, g_fin, w_gate, w_proj)


FFN1_W = ("ffn1_w_in", "ffn1_w_out")
MIX_W = ("w_mix_in", "w_mix_out")
REST_W = ("ffn2_w_in", "ffn2_w_out", "ple_w_gate", "ple_w_proj")
BIG_W = FFN1_W + MIX_W + REST_W
COLUMN_SHARDED = ("w_mix_in", "ple_w_proj")


class _Traffic:
    def __init__(self, shards):
        self.shards = shards
        self.parts = {}

    @staticmethod
    def _full(name, gathered):
        if name in COLUMN_SHARDED:
            return jnp.transpose(gathered, (1, 0, 2)).reshape(gathered.shape[1], -1)
        if name.endswith("_w_in"):
            return gathered
        return gathered.reshape(-1, gathered.shape[-1])

    @staticmethod
    def _blocks(name, grad):
        name = name.split("/")[0]
        if name in COLUMN_SHARDED:
            return jnp.transpose(grad.reshape(grad.shape[0], N_DEV, -1), (1, 0, 2))
        if name.endswith("_w_in"):
            return grad
        return grad.reshape(N_DEV, -1, grad.shape[-1])

    def gather_now(self, names):
        got = _exchange("gather_" + names[0], [self.shards[n] for n in names], [False] * len(names))
        return self.gathered(names, got)

    def gather_rider(self, names):
        return [self.shards[n] for n in names], [False] * len(names)

    def gathered(self, names, got):
        return {n: self._full(n, g) for n, g in zip(names, got)}

    def scatter_rider(self, grads, gather=()):
        return ([self._blocks(n, g) for n, g in grads.items()] + list(gather),
                [True] * len(grads) + [False] * len(gather))

    def scattered(self, names, got):
        self.parts.update(zip(names, got))
        return got[len(names):]

    def finish(self, grads):
        arrays, flags = self.scatter_rider(grads)
        self.scattered(list(grads), _exchange("scatter_last", arrays, flags))


def _local_step(traffic, x, p, target, g1, gmix, gv, ws, b_t, g2, gple, gfin, pack_small):
    T, D = x.shape
    tm = min(ROW_TILE, T)

    w = traffic.gather_now(FFN1_W)
    h1, n1, G1, U1, a1, *got = _ffn_fwd("ffn1_fwd", x, g1, w["ffn1_w_in"], w["ffn1_w_out"],
                                        rider=traffic.gather_rider(MIX_W))
    w.update(traffic.gathered(MIX_W, got))
    n2, zuv, qkv = _mix_in_fwd(h1, gmix, w["w_mix_in"])
    gm = _gmlp_fwd(zuv, gv, ws, b_t)
    sb, ltot, *got = _attn_fwd(qkv, rider=traffic.gather_rider(REST_W))
    w.update(traffic.gathered(REST_W, got))
    h2 = _mix_out_fwd(h1, gm, sb, w["w_mix_out"])
    h3, n3, G2, U2, a2 = _ffn_fwd("ffn2_fwd", h2, g2, w["ffn2_w_in"], w["ffn2_w_out"])
    loss, dh3, n4, d_gl, d_pp, dg_ple, dg_fin = _tail(h3, p, target, gple, gfin, w["ple_w_gate"], w["ple_w_proj"])

    nb, _, FB = w["ffn1_w_in"].shape
    nh = nb // 2

    tt = min(GRAD_ROW_TILE, T)

    def dw_out(name, a, d_out):
        return _matmul_tn(name, a, d_out, nh, (1, tt, FB), lambda j, t: (j, t, 0), (tt, D), lambda j, t: (t, 0),
                          (nh, FB, D), (1, FB, D), lambda j, t: (j, 0, 0))

    def dense_tn(name, a, b, ncol):
        ka, nbw = a.shape[1], b.shape[1] // ncol
        return _matmul_tn(name, a, b, ncol, (tt, ka), lambda j, t: (t, 0), (tt, nbw), lambda j, t: (t, j),
                          (ka, b.shape[1]), (ka, nbw), lambda j, t: (0, j))

    grads = dict(ple_w_gate=dense_tn("dw_ple_gate", n4, d_gl, 2), ple_w_proj=dense_tn("dw_ple_proj", p, d_pp, 1))
    dh2, dg2, dG2, dU2, dout2 = _ffn_bwd("ffn2_bwd", dh3, h2, g2, G2, U2, w["ffn2_w_in"], w["ffn2_w_out"])
    grads["ffn2_w_in"], = _dw_in("ffn2_dw_in", n3, dG2, dU2)
    grads["ffn2_w_out"] = dw_out("ffn2_dw_out", a2, dout2)
    grads = {n: grads[n] for n in REST_W}

    d_gm, d_sb, dh2_bf = _mix_out_bwd(dh2, w["w_mix_out"])
    grads["w_mix_out"] = jnp.concatenate([dense_tn("dw_mix_out_gm", gm, dh2_bf, 1),
                                          dense_tn("dw_mix_out_sb", sb, dh2_bf, 1)], axis=0)
    dzuv, dgv, dws, db_t = _gmlp_bwd(zuv, d_gm, gv, ws, b_t)
    dq, dk, dv, *got = _attn_bwd(qkv, d_sb, ltot, rider=traffic.scatter_rider(grads))
    traffic.scattered(list(grads), got)
    dqkv = jnp.concatenate([dq, dk, dv], axis=1)
    dw_mi = jnp.concatenate([dense_tn("dw_mix_in_uv", n2, dzuv, 2), dense_tn("dw_mix_in_qkv", n2, dqkv, 3)], axis=1)
    dh1, dgmix, dout1 = _mix_in_bwd(dzuv, dqkv, w["w_mix_in"], h1, gmix, dh2)

    grads = dict(w_mix_in=dw_mi, ffn1_w_out=dw_out("ffn1_dw_out", a1, dout1))
    dx, dg1, dG1, dU1, _, *got = _ffn_bwd("ffn1_bwd", dh1, x, g1, G1, U1, w["ffn1_w_in"], w["ffn1_w_out"],
                                         rider=traffic.scatter_rider(grads))
    traffic.scattered(list(grads), got)
    cut = DW_IN_FIRST_ROWS * D // 1024
    top, = _dw_in("ffn1_dw_in_top", n1[:, :cut], dG1, dU1)
    small = pack_small(dict(ffn1_norm=dg1, mix_norm=dgmix, gmlp_v_norm=dgv, gmlp_w_s=dws, gmlp_b=jnp.transpose(db_t),
                            ffn2_norm=dg2, ple_norm=dg_ple, final_norm=dg_fin), loss)
    bottom, *got = _dw_in("ffn1_dw_in_bottom", n1[:, cut:], dG1, dU1,
                          rider=traffic.scatter_rider({"ffn1_w_in/0": top}, gather=[small]))
    small_parts, = traffic.scattered(["ffn1_w_in/0"], got)
    traffic.finish({"ffn1_w_in/1": bottom})
    return dx, small_parts


def _peer(d):
    x, y, c = lax.axis_index("x"), lax.axis_index("y"), lax.axis_index("c")
    px = 1 - x if d & 4 else x
    py = 1 - y if d & 2 else y
    pc = 1 - c if d & 1 else c
    return (px, py, pc), 4 * px + 2 * py + pc


N_CHIPS_AWAY = 3


class _ExchangePlan:
    def __init__(self, ins, outs, send, recv, local, scatter):
        self.ins, self.outs, self.send, self.recv, self.local, self.scatter = ins, outs, send, recv, local, scatter
        self.me = _peer(0)[1]

    def _remote(self, t, sem, src, slot, peer):
        return pltpu.make_async_remote_copy(
            src_ref=src, dst_ref=self.outs[t].at[slot], send_sem=self.send.at[t, sem], recv_sem=self.recv.at[t, sem],
            device_id=peer, device_id_type=MESH)

    def _own(self, t):
        src = self.ins[t].at[self.me] if self.scatter[t] else self.ins[t]
        return pltpu.make_async_copy(src, self.outs[t].at[self.me], self.local.at[t])

    def _n_direct(self, t):
        return N_DEV - 1 if self.scatter[t] else N_CHIPS_AWAY + 1

    def _direct(self, t, k):
        if self.scatter[t]:
            peer, slot = _peer(k + 1)
            return self._remote(t, k, self.ins[t].at[slot], self.me, peer)
        return self._remote(t, k, self.ins[t], self.me, _peer(2 * k if k else 1)[0])

    def _relay(self, t, c):
        slot = _peer(2 * c)[1]
        return self._remote(t, N_CHIPS_AWAY + c, self.outs[t].at[slot], slot, _peer(1)[0])

    def start(self):
        for t in range(len(self.ins)):
            self._own(t).start()
            for k in range(self._n_direct(t)):
                self._direct(t, k).start()

    def relay(self):
        for t in self._gathers():
            for c in range(1, N_CHIPS_AWAY + 1):
                self._direct(t, c).wait_recv()
                self._relay(t, c).start()

    def _gathers(self):
        return [t for t in range(len(self.ins)) if not self.scatter[t]]

    def finish(self):
        for t in range(len(self.ins)):
            self._own(t).wait()
            for k in range(self._n_direct(t)):
                self._direct(t, k).wait_send()
                if self.scatter[t] or k == 0:
                    self._direct(t, k).wait_recv()
        for t in self._gathers():
            for c in range(1, N_CHIPS_AWAY + 1):
                self._relay(t, c).wait()


def _exchange_shapes(arrays, scatter):
    return [jax.ShapeDtypeStruct(a.shape if sc else (N_DEV,) + a.shape, a.dtype) for a, sc in zip(arrays, scatter)]


def _exchange_sems(n):
    return [pltpu.SemaphoreType.DMA((n, N_DEV - 1)), pltpu.SemaphoreType.DMA((n, N_DEV - 1)),
            pltpu.SemaphoreType.DMA((n,))]


_ANY = pl.BlockSpec(memory_space=pl.ANY)


def _exchange(name, arrays, scatter):
    n = len(arrays)

    def body(*refs):
        plan = _ExchangePlan(refs[:n], refs[n:2 * n], *refs[2 * n:], scatter)
        plan.start()
        plan.relay()
        plan.finish()

    return pl.pallas_call(
        body, name=name, in_specs=[_ANY] * n, out_specs=[_ANY] * n, out_shape=_exchange_shapes(arrays, scatter),
        scratch_shapes=_exchange_sems(n),
    )(*arrays)


def _pallas(body, rider, *, name, grid, in_specs, out_specs, out_shape, scratch_shapes=(), compiler_params=None):
    if rider is None:
        return pl.pallas_call(body, name=name, grid=grid, in_specs=in_specs, out_specs=out_specs, out_shape=out_shape,
                              scratch_shapes=list(scratch_shapes), compiler_params=compiler_params)
    arrays, scatter = rider
    n, ni, no, ns = len(arrays), len(in_specs), len(out_specs), len(scratch_shapes)

    def carried(*refs):
        ins, r_in = refs[:ni], refs[ni:ni + n]
        outs, r_out = refs[ni + n:ni + n + no], refs[ni + n + no:ni + 2 * n + no]
        scratch, sems = refs[ni + 2 * n + no:ni + 2 * n + no + ns], refs[ni + 2 * n + no + ns:]
        step = 0
        for ax, g in enumerate(grid):
            step = step * g + pl.program_id(ax)
        steps = functools.reduce(lambda a, b: a * b, grid)
        assert steps >= 3

        @pl.when(step == 0)
        def _():
            _ExchangePlan(r_in, r_out, *sems, scatter).start()

        @pl.when(step == steps // 2)
        def _():
            _ExchangePlan(r_in, r_out, *sems, scatter).relay()

        body(*ins, *outs, *scratch)

        @pl.when(step == steps - 1)
        def _():
            _ExchangePlan(r_in, r_out, *sems, scatter).finish()

    call = pl.pallas_call(
        carried, name=name, grid=grid, in_specs=list(in_specs) + [_ANY] * n, out_specs=list(out_specs) + [_ANY] * n,
        out_shape=list(out_shape) + _exchange_shapes(arrays, scatter),
        scratch_shapes=list(scratch_shapes) + _exchange_sems(n), compiler_params=compiler_params)
    return lambda *args: call(*args, *arrays)


def _adamw_math(g, w, m, v):
    m_new = ADAM_B1 * m + (1.0 - ADAM_B1) * g
    v_new = ADAM_B2 * v + (1.0 - ADAM_B2) * (g * g)
    m_hat = m_new / (1.0 - ADAM_B1 ** ADAM_STEP)
    v_hat = v_new / (1.0 - ADAM_B2 ** ADAM_STEP)
    delta = -ADAM_LR * (m_hat / (jnp.sqrt(v_hat) + ADAM_EPS) + ADAM_WD * w)
    return delta, m_new, v_new


def _adamw(name, parts, w, m, v):
    R, C = w.shape
    tr = R
    for cand in (256, 128, 64, 32, 16, 8):
        if R % cand == 0:
            tr = cand
            break

    def body(p_ref, w_ref, m_ref, v_ref, g_ref, d_ref, nm_ref, nv_ref):
        g = p_ref[0].astype(F32)
        for j in range(1, N_DEV):
            g = g + p_ref[j].astype(F32)
        g_ref[...] = g
        d_ref[...], nm_ref[...], nv_ref[...] = _adamw_math(g, w_ref[...], m_ref[...], v_ref[...])

    row = lambda i: (i, 0)
    spec = pl.BlockSpec((tr, C), row)
    return pl.pallas_call(
        body, name=name, grid=(R // tr,),
        in_specs=[pl.BlockSpec((N_DEV, tr, C), lambda i: (0, i, 0)), spec, spec, spec],
        out_specs=[spec] * 4,
        out_shape=[jax.ShapeDtypeStruct((R, C), F32)] * 4,
        compiler_params=_params("arbitrary"),
    )(parts, w, m, v)


def _rows128(a):
    flat = a.reshape(-1, LANES)
    pad = (-flat.shape[0]) % SMALL_ROWS_ALIGN
    return jnp.pad(flat, ((0, pad), (0, 0))) if pad else flat


def _unrows(packed, like):
    n = like.size // LANES
    return packed[:n].reshape(like.shape)


def kernel(x, p, ffn1_norm, ffn1_w_in, ffn1_w_out, mix_norm, w_mix_in, gmlp_v_norm, gmlp_w_s, gmlp_b, w_mix_out, ffn2_norm, ffn2_w_in, ffn2_w_out, ple_norm, ple_w_gate, ple_w_proj, final_norm, loss_target, m_ffn1_norm, m_ffn1_w_in, m_ffn1_w_out, m_mix_norm, m_w_mix_in, m_gmlp_v_norm, m_gmlp_w_s, m_gmlp_b, m_w_mix_out, m_ffn2_norm, m_ffn2_w_in, m_ffn2_w_out, m_ple_norm, m_ple_w_gate, m_ple_w_proj, m_final_norm, v_ffn1_norm, v_ffn1_w_in, v_ffn1_w_out, v_mix_norm, v_w_mix_in, v_gmlp_v_norm, v_gmlp_w_s, v_gmlp_b, v_w_mix_out, v_ffn2_norm, v_ffn2_w_in, v_ffn2_w_out, v_ple_norm, v_ple_w_gate, v_ple_w_proj, v_final_norm):
    names = ["ffn1_norm", "ffn1_w_in", "ffn1_w_out", "mix_norm", "w_mix_in", "gmlp_v_norm", "gmlp_w_s", "gmlp_b",
             "w_mix_out", "ffn2_norm", "ffn2_w_in", "ffn2_w_out", "ple_norm", "ple_w_gate", "ple_w_proj", "final_norm"]
    W = dict(zip(names, [ffn1_norm, ffn1_w_in, ffn1_w_out, mix_norm, w_mix_in, gmlp_v_norm, gmlp_w_s, gmlp_b,
                         w_mix_out, ffn2_norm, ffn2_w_in, ffn2_w_out, ple_norm, ple_w_gate, ple_w_proj, final_norm]))
    M = dict(zip(names, [m_ffn1_norm, m_ffn1_w_in, m_ffn1_w_out, m_mix_norm, m_w_mix_in, m_gmlp_v_norm, m_gmlp_w_s,
                         m_gmlp_b, m_w_mix_out, m_ffn2_norm, m_ffn2_w_in, m_ffn2_w_out, m_ple_norm, m_ple_w_gate,
                         m_ple_w_proj, m_final_norm]))
    V = dict(zip(names, [v_ffn1_norm, v_ffn1_w_in, v_ffn1_w_out, v_mix_norm, v_w_mix_in, v_gmlp_v_norm, v_gmlp_w_s,
                         v_gmlp_b, v_w_mix_out, v_ffn2_norm, v_ffn2_w_in, v_ffn2_w_out, v_ple_norm, v_ple_w_gate,
                         v_ple_w_proj, v_final_norm]))
    small = [n for n in names if n not in BIG_W]
    D = x.shape[-1]

    def pack(src, last):
        return jnp.concatenate([_rows128(src[n]) for n in small] + [last], axis=0)

    offs = [0]
    for n in small:
        offs.append(offs[-1] + _rows128(W[n]).shape[0])

    traffic = _Traffic({n: W[n][0].astype(BF16) for n in BIG_W})
    dx, small_parts = _local_step(
        traffic, x[0], p[0, 0], loss_target[0],
        W["ffn1_norm"], W["mix_norm"], W["gmlp_v_norm"], W["gmlp_w_s"][0], jnp.transpose(W["gmlp_b"][0]),
        W["ffn2_norm"], W["ple_norm"], W["final_norm"].reshape(1, D),
        lambda grads, loss_part: pack(grads, jnp.broadcast_to(loss_part, (SMALL_ROWS_ALIGN, LANES))))

    parts = traffic.parts
    parts["ffn1_w_in"] = jnp.concatenate([parts["ffn1_w_in/0"], parts["ffn1_w_in/1"]], axis=1)
    out = {}
    for n in BIG_W:
        out[n] = _adamw("adamw_" + n, parts[n], W[n][0], M[n][0], V[n][0])
    zeros = jnp.zeros((SMALL_ROWS_ALIGN, LANES), F32)
    sg, sd, sm, sv = _adamw("adamw_small", small_parts, pack(W, zeros), pack(M, zeros), pack(V, zeros))
    for k, n in enumerate(small):
        out[n] = tuple(_unrows(arr[offs[k]:offs[k + 1]], W[n]) for arr in (sg, sd, sm, sv))
    loss = sg[offs[len(small)], 0]

    res = [loss, dx[None]]
    for k in range(4):
        res += [out[n][k].reshape(W[n].shape) for n in names]
    return tuple(res)
```

```python
import functools

import jax
import jax.numpy as jnp
from jax import lax
from jax.experimental import pallas as pl
from jax.experimental.pallas import tpu as pltpu

F32 = jnp.float32
BF16 = jnp.bfloat16
MESH = pl.DeviceIdType.MESH

N_DEV = 8
EPS = 1e-6
ADAM_LR = 0.001
ADAM_B1 = 0.9
ADAM_B2 = 0.999
ADAM_EPS = 1e-08
ADAM_WD = 0.01
ADAM_STEP = 10

GM_WIDTH = 512
GM_HEADS = 4
CHUNK = 128
SB_WIDTH = 512
SB_HEAD_DIM = 64
SB_SCALE = 0.125
LANES = 128
SMALL_ROWS_ALIGN = 8

ROW_TILE = 512
GRAD_ROW_TILE = 2048
DW_IN_ROW_TILE = 4096
FFN_FWD_ROW_TILE = 1024
ATTN_Q_ROWS = 512
ATTN_BWD_Q_ROWS = 512
ATTN_KEY_BLOCK = 256
ATTN_UNROLL = 2
VMEM_LIMIT = 56 * 1024 * 1024


def _params(*sem):
    return pltpu.CompilerParams(dimension_semantics=sem, vmem_limit_bytes=VMEM_LIMIT)


def _dot(a, b):
    return jnp.dot(a, b, preferred_element_type=F32)


def _dot_nt(a, b):
    return lax.dot_general(a, b, (((1,), (1,)), ((), ())), preferred_element_type=F32)


def _dot_tn(a, b):
    return lax.dot_general(a, b, (((0,), (0,)), ((), ())), preferred_element_type=F32)


def _rms_parts(x):
    r = lax.rsqrt(jnp.mean(x * x, axis=-1, keepdims=True) + EPS)
    return r, x * r


def _rms_bwd(x, g, dy):
    r, xh = _rms_parts(x)
    dyg = dy * g
    dx = r * (dyg - xh * jnp.mean(dyg * xh, axis=-1, keepdims=True))
    return dx, jnp.sum(dy * xh, axis=0, keepdims=True)


def _sigmoid(x):
    return 1.0 / (1.0 + jnp.exp(-x))


_SQRT_HALF = 0.7071067811865476
_INV_SQRT_2PI = 0.3989422804014327


def _gelu(x):
    return 0.5 * x * (1.0 + lax.erf(x * _SQRT_HALF))


def _gelu_grad(x):
    return 0.5 * (1.0 + lax.erf(x * _SQRT_HALF)) + x * (_INV_SQRT_2PI * jnp.exp(-0.5 * x * x))


def _split_bf16(x):
    hi = x.astype(BF16)
    lo = (x - hi.astype(F32)).astype(BF16)
    return hi, lo


def _ffn_fwd(name, h, gain, w_in, w_out, rider=None):
    T, D = h.shape
    nb, _, FB = w_in.shape
    nh = nb // 2
    tm = min(FFN_FWD_ROW_TILE, T)

    def body(h_ref, g_ref, wg_ref, wu_ref, wo_ref, ho_ref, n_ref, G_ref, U_ref, a_ref, n_s, acc):
        jj = pl.program_id(1)

        @pl.when(jj == 0)
        def _():
            _, xh = _rms_parts(h_ref[...])
            n = (xh * g_ref[...]).astype(BF16)
            n_s[...] = n
            n_ref[...] = n
            acc[...] = jnp.zeros_like(acc)

        n = n_s[...]
        G = _dot(n, wg_ref[0])
        U = _dot(n, wu_ref[0])
        G_ref[0] = G.astype(BF16)
        U_ref[0] = U.astype(BF16)
        a = (G * _sigmoid(G) * U).astype(BF16)
        a_ref[0] = a
        acc[...] += _dot(a, wo_ref[...])

        @pl.when(jj == nh - 1)
        def _():
            ho_ref[...] = h_ref[...] + 0.5 * acc[...]

    row = lambda i, j: (i, 0)
    blk = lambda i, j: (j, i, 0)
    return _pallas(
        body, rider, name=name, grid=(T // tm, nh),
        in_specs=[pl.BlockSpec((tm, D), row),
                  pl.BlockSpec((1, D), lambda i, j: (0, 0)),
                  pl.BlockSpec((1, D, FB), lambda i, j: (j, 0, 0)),
                  pl.BlockSpec((1, D, FB), lambda i, j: (j + nh, 0, 0)),
                  pl.BlockSpec((FB, D), lambda i, j: (j, 0))],
        out_specs=[pl.BlockSpec((tm, D), row), pl.BlockSpec((tm, D), row),
                   pl.BlockSpec((1, tm, FB), blk), pl.BlockSpec((1, tm, FB), blk),
                   pl.BlockSpec((1, tm, FB), blk)],
        out_shape=[jax.ShapeDtypeStruct((T, D), F32), jax.ShapeDtypeStruct((T, D), BF16),
                   jax.ShapeDtypeStruct((nh, T, FB), BF16), jax.ShapeDtypeStruct((nh, T, FB), BF16),
                   jax.ShapeDtypeStruct((nh, T, FB), BF16)],
        scratch_shapes=[pltpu.VMEM((tm, D), BF16), pltpu.VMEM((tm, D), F32)],
        compiler_params=_params("arbitrary", "arbitrary"),
    )(h, gain, w_in, w_in, w_out)


def _ffn_bwd(name, dh, h_in, gain, G, U, w_in, w_out, rider=None):
    T, D = dh.shape
    nb, _, FB = w_in.shape
    nh = nb // 2
    tm = min(ROW_TILE, T)

    def body(dh_ref, h_ref, g_ref, G_ref, U_ref, wg_ref, wu_ref, wo_ref,
             dhin_ref, dg_ref, dG_ref, dU_ref, do_ref, dn_acc, do_s):
        i = pl.program_id(0)
        jj = pl.program_id(1)

        @pl.when(jj == 0)
        def _():
            d_out = (0.5 * dh_ref[...]).astype(BF16)
            do_s[...] = d_out
            do_ref[...] = d_out
            dn_acc[...] = jnp.zeros_like(dn_acc)

        @pl.when((i == 0) & (jj == 0))
        def _():
            dg_ref[...] = jnp.zeros_like(dg_ref)

        halves = [slice(0, tm // 2), slice(tm // 2, tm)]
        da = [_dot_nt(do_s[rows, :], wo_ref[...]) for rows in halves]
        dGU = []
        for rows, dav in zip(halves, da):
            dG, dU = _gate_grads(dav, G_ref[0, rows, :].astype(F32), U_ref[0, rows, :].astype(F32))
            dG_ref[0, rows, :] = dG
            dU_ref[0, rows, :] = dU
            dGU.append((dG, dU))
        dn = [_dot_nt(dG, wg_ref[0]) for dG, _ in dGU]
        dn = [d + _dot_nt(dU, wu_ref[0]) for d, (_, dU) in zip(dn, dGU)]
        for rows, d in zip(halves, dn):
            dn_acc[rows, :] += d

        @pl.when(jj == nh - 1)
        def _():
            dx, dg = _rms_bwd(h_ref[...], g_ref[...], dn_acc[...])
            dhin_ref[...] = dh_ref[...] + dx
            dg_ref[...] += dg

    row = lambda i, j: (i, 0)
    blk = lambda i, j: (j, i, 0)
    one = lambda i, j: (0, 0)
    return _pallas(
        body, rider, name=name, grid=(T // tm, nh),
        in_specs=[pl.BlockSpec((tm, D), row), pl.BlockSpec((tm, D), row), pl.BlockSpec((1, D), one),
                  pl.BlockSpec((1, tm, FB), blk), pl.BlockSpec((1, tm, FB), blk),
                  pl.BlockSpec((1, D, FB), lambda i, j: (j, 0, 0)),
                  pl.BlockSpec((1, D, FB), lambda i, j: (j + nh, 0, 0)),
                  pl.BlockSpec((FB, D), lambda i, j: (j, 0))],
        out_specs=[pl.BlockSpec((tm, D), row), pl.BlockSpec((1, D), one),
                   pl.BlockSpec((1, tm, FB), blk), pl.BlockSpec((1, tm, FB), blk),
                   pl.BlockSpec((tm, D), row)],
        out_shape=[jax.ShapeDtypeStruct((T, D), F32), jax.ShapeDtypeStruct((1, D), F32),
                   jax.ShapeDtypeStruct((nh, T, FB), BF16), jax.ShapeDtypeStruct((nh, T, FB), BF16),
                   jax.ShapeDtypeStruct((T, D), BF16)],
        scratch_shapes=[pltpu.VMEM((tm, D), F32), pltpu.VMEM((tm, D), BF16)],
        compiler_params=_params("arbitrary", "arbitrary"),
    )(dh, h_in, gain, G, U, w_in, w_in, w_out)


def _gate_grads(dav, Gv, Uv):
    sig = _sigmoid(Gv)
    return (dav * Uv * (sig * (1.0 + Gv * (1.0 - sig)))).astype(BF16), (dav * (Gv * sig)).astype(BF16)


def _ffn_bwd_gates(name, d_out, G, U, w_out, rider=None):
    T, D = d_out.shape
    nh, _, FB = G.shape
    tm = min(FFN_FWD_ROW_TILE, T)

    def body(do_ref, G_ref, U_ref, wo_ref, dG_ref, dU_ref):
        halves = [slice(0, tm // 2), slice(tm // 2, tm)]
        da = [_dot_nt(do_ref[rows, :], wo_ref[...]) for rows in halves]
        for rows, dav in zip(halves, da):
            dG_ref[0, rows, :], dU_ref[0, rows, :] = _gate_grads(
                dav, G_ref[0, rows, :].astype(F32), U_ref[0, rows, :].astype(F32))

    blk = lambda i, j: (j, i, 0)
    return _pallas(
        body, rider, name=name, grid=(T // tm, nh),
        in_specs=[pl.BlockSpec((tm, D), lambda i, j: (i, 0)), pl.BlockSpec((1, tm, FB), blk),
                  pl.BlockSpec((1, tm, FB), blk), pl.BlockSpec((FB, D), lambda i, j: (j, 0))],
        out_specs=[pl.BlockSpec((1, tm, FB), blk), pl.BlockSpec((1, tm, FB), blk)],
        out_shape=[jax.ShapeDtypeStruct((nh, T, FB), BF16), jax.ShapeDtypeStruct((nh, T, FB), BF16)],
        compiler_params=_params("arbitrary", "arbitrary"),
    )(d_out, G, U, w_out)


def _ffn_bwd_input(name, dh, h_in, gain, dG, dU, w_in, rider=None):
    T, D = dh.shape
    nb, _, FB = w_in.shape
    nh = nb // 2
    tm = min(FFN_FWD_ROW_TILE, T)

    def body(dh_ref, h_ref, g_ref, dG_ref, dU_ref, wg_ref, wu_ref, dhin_ref, dg_ref, dn_acc):
        i = pl.program_id(0)
        jj = pl.program_id(1)

        @pl.when(jj == 0)
        def _():
            dn_acc[...] = jnp.zeros_like(dn_acc)

        @pl.when((i == 0) & (jj == 0))
        def _():
            dg_ref[...] = jnp.zeros_like(dg_ref)

        halves = [slice(0, tm // 2), slice(tm // 2, tm)]
        dn = [_dot_nt(dG_ref[0, rows, :], wg_ref[0]) for rows in halves]
        dn = [d + _dot_nt(dU_ref[0, rows, :], wu_ref[0]) for d, rows in zip(dn, halves)]
        for rows, d in zip(halves, dn):
            dn_acc[rows, :] += d

        @pl.when(jj == nh - 1)
        def _():
            dx, dg = _rms_bwd(h_ref[...], g_ref[...], dn_acc[...])
            dhin_ref[...] = dh_ref[...] + dx
            dg_ref[...] += dg

    row = lambda i, j: (i, 0)
    blk = lambda i, j: (j, i, 0)
    one = lambda i, j: (0, 0)
    return _pallas(
        body, rider, name=name, grid=(T // tm, nh),
        in_specs=[pl.BlockSpec((tm, D), row), pl.BlockSpec((tm, D), row), pl.BlockSpec((1, D), one),
                  pl.BlockSpec((1, tm, FB), blk), pl.BlockSpec((1, tm, FB), blk),
                  pl.BlockSpec((1, D, FB), lambda i, j: (j, 0, 0)),
                  pl.BlockSpec((1, D, FB), lambda i, j: (j + nh, 0, 0))],
        out_specs=[pl.BlockSpec((tm, D), row), pl.BlockSpec((1, D), one)],
        out_shape=[jax.ShapeDtypeStruct((T, D), F32), jax.ShapeDtypeStruct((1, D), F32)],
        scratch_shapes=[pltpu.VMEM((tm, D), F32)],
        compiler_params=_params("arbitrary", "arbitrary"),
    )(dh, h_in, gain, dG, dU, w_in, w_in)


def _matmul_tn(name, a, b, nj, a_block, a_map, b_block, b_map, out_shape, out_block, out_map):
    T = a.shape[-2]
    tt = a_block[-2]
    nt = T // tt
    kb, nbk = out_block[-2], out_block[-1]

    def body(a_ref, b_ref, o_ref, acc):
        t = pl.program_id(1)

        @pl.when(t == 0)
        def _():
            acc[...] = jnp.zeros_like(acc)

        av = (a_ref[0] if len(a_block) == 3 else a_ref[...]).astype(BF16)
        bv = b_ref[0] if len(b_block) == 3 else b_ref[...]
        acc[...] += _dot_tn(av, bv)

        @pl.when(t == nt - 1)
        def _():
            if len(out_block) == 3:
                o_ref[0] = acc[...].astype(o_ref.dtype)
            else:
                o_ref[...] = acc[...].astype(o_ref.dtype)

    return pl.pallas_call(
        body, name=name, grid=(nj, nt),
        in_specs=[pl.BlockSpec(a_block, a_map), pl.BlockSpec(b_block, b_map)],
        out_specs=pl.BlockSpec(out_block, out_map),
        out_shape=jax.ShapeDtypeStruct(out_shape, BF16),
        scratch_shapes=[pltpu.VMEM((kb, nbk), F32)],
        compiler_params=_params("arbitrary", "arbitrary"),
    )(a, b)


def _dw_in(name, n, dG, dU, rider=None):
    T, kr = n.shape
    nh, _, FB = dG.shape
    tt = min(DW_IN_ROW_TILE, T)
    nt = T // tt
    cut = LANES * ((kr // LANES + 1) // 2)

    def body(n_ref, dg_ref, du_ref, o_ref, acc):
        j = pl.program_id(0)
        t = pl.program_id(1)

        @pl.when(t == 0)
        def _():
            acc[...] = jnp.zeros_like(acc)

        def add(dz_ref):
            for rows in ((slice(0, cut), slice(cut, kr)) if cut < kr else (slice(0, kr),)):
                acc[rows, :] += _dot_tn(n_ref[:, rows], dz_ref[0])

        @pl.when(j < nh)
        def _():
            add(dg_ref)

        @pl.when(j >= nh)
        def _():
            add(du_ref)

        @pl.when(t == nt - 1)
        def _():
            o_ref[0] = acc[...].astype(BF16)

    return _pallas(
        body, rider, name=name, grid=(2 * nh, nt),
        in_specs=[pl.BlockSpec((tt, kr), lambda j, t: (t, 0)),
                  pl.BlockSpec((1, tt, FB), lambda j, t: (jnp.minimum(j, nh - 1), t, 0)),
                  pl.BlockSpec((1, tt, FB), lambda j, t: (jnp.maximum(j - nh, 0), t, 0))],
        out_specs=[pl.BlockSpec((1, kr, FB), lambda j, t: (j, 0, 0))],
        out_shape=[jax.ShapeDtypeStruct((2 * nh, kr, FB), BF16)],
        scratch_shapes=[pltpu.VMEM((kr, FB), F32)],
        compiler_params=_params("arbitrary", "arbitrary"),
    )(n, dG, dU)


def _mix_in_fwd(h, gain, w):
    T, D = h.shape
    W = w.shape[1]
    nuv = 2 * GM_WIDTH
    tm = min(ROW_TILE, T)

    def body(h_ref, g_ref, w_ref, n_ref, zuv_ref, qkv_ref):
        _, xh = _rms_parts(h_ref[...])
        n = (xh * g_ref[...]).astype(BF16)
        n_ref[...] = n
        z = _dot(n, w_ref[...])
        zuv_ref[...] = z[:, :nuv]
        qkv_ref[...] = z[:, nuv:].astype(BF16)

    row = lambda i: (i, 0)
    return pl.pallas_call(
        body, name="mix_in_fwd", grid=(T // tm,),
        in_specs=[pl.BlockSpec((tm, D), row), pl.BlockSpec((1, D), lambda i: (0, 0)),
                  pl.BlockSpec((D, W), lambda i: (0, 0))],
        out_specs=[pl.BlockSpec((tm, D), row), pl.BlockSpec((tm, nuv), row),
                   pl.BlockSpec((tm, W - nuv), row)],
        out_shape=[jax.ShapeDtypeStruct((T, D), BF16), jax.ShapeDtypeStruct((T, nuv), F32),
                   jax.ShapeDtypeStruct((T, W - nuv), BF16)],
        compiler_params=_params("arbitrary"),
    )(h, gain, w)


def _mix_in_bwd(dzuv, dqkv, w, h, gain, dh):
    T, D = h.shape
    W = w.shape[1]
    nuv = dzuv.shape[1]
    tm = min(ROW_TILE, T)

    def body(dzuv_ref, dqkv_ref, w_ref, h_ref, g_ref, dh_ref, dhin_ref, dg_ref, half_ref):
        @pl.when(pl.program_id(0) == 0)
        def _():
            dg_ref[...] = jnp.zeros_like(dg_ref)

        dn = _dot_nt(dzuv_ref[...], w_ref[:, :nuv]) + _dot_nt(dqkv_ref[...], w_ref[:, nuv:])
        dx, dg = _rms_bwd(h_ref[...], g_ref[...], dn)
        dh_in = dh_ref[...] + dx
        dhin_ref[...] = dh_in
        half_ref[...] = (0.5 * dh_in).astype(BF16)
        dg_ref[...] += dg

    row = lambda i: (i, 0)
    one = lambda i: (0, 0)
    return pl.pallas_call(
        body, name="mix_in_bwd", grid=(T // tm,),
        in_specs=[pl.BlockSpec((tm, nuv), row), pl.BlockSpec((tm, W - nuv), row),
                  pl.BlockSpec((D, W), one), pl.BlockSpec((tm, D), row), pl.BlockSpec((1, D), one),
                  pl.BlockSpec((tm, D), row)],
        out_specs=[pl.BlockSpec((tm, D), row), pl.BlockSpec((1, D), one), pl.BlockSpec((tm, D), row)],
        out_shape=[jax.ShapeDtypeStruct((T, D), F32), jax.ShapeDtypeStruct((1, D), F32),
                   jax.ShapeDtypeStruct((T, D), BF16)],
        compiler_params=_params("arbitrary"),
    )(dzuv, dqkv, w, h, gain, dh)


def _gmlp_norm(zv, gv):
    v = _gelu(zv)
    r, vh = _rms_parts(v)
    return r, vh, (vh * gv).astype(BF16)


def _causal_ws(ws_ref, hd):
    r = lax.broadcasted_iota(jnp.int32, (CHUNK, CHUNK), 0)
    c = lax.broadcasted_iota(jnp.int32, (CHUNK, CHUNK), 1)
    return jnp.where(r >= c, ws_ref[hd], 0.0).astype(BF16)


def _gmlp_fwd(zuv, gv, ws, b_t):
    T = zuv.shape[0]
    tg = min(ROW_TILE, T)

    def body(zu_ref, zv_ref, gv_ref, ws_ref, bt_ref, o_ref):
        u = _gelu(zu_ref[...])
        _, _, vn = _gmlp_norm(zv_ref[...], gv_ref[...])
        for hd in range(GM_HEADS):
            wc = _causal_ws(ws_ref, hd)
            cols = slice(hd * CHUNK, (hd + 1) * CHUNK)
            for c in range(tg // CHUNK):
                rows = slice(c * CHUNK, (c + 1) * CHUNK)
                sv = _dot(wc, vn[rows, cols]) + bt_ref[:, hd:hd + 1]
                o_ref[rows, cols] = (u[rows, cols] * sv).astype(BF16)

    return pl.pallas_call(
        body, name="gmlp_fwd", grid=(T // tg,),
        in_specs=[pl.BlockSpec((tg, GM_WIDTH), lambda i: (i, 0)), pl.BlockSpec((tg, GM_WIDTH), lambda i: (i, 1)),
                  pl.BlockSpec((1, GM_WIDTH), lambda i: (0, 0)),
                  pl.BlockSpec((GM_HEADS, CHUNK, CHUNK), lambda i: (0, 0, 0)),
                  pl.BlockSpec((CHUNK, GM_HEADS), lambda i: (0, 0))],
        out_specs=pl.BlockSpec((tg, GM_WIDTH), lambda i: (i, 0)),
        out_shape=jax.ShapeDtypeStruct((T, GM_WIDTH), BF16),
        compiler_params=_params("arbitrary"),
    )(zuv, zuv, gv, ws, b_t)


def _gmlp_bwd(zuv, d_gm, gv, ws, b_t):
    T = zuv.shape[0]
    tg = min(ROW_TILE, T)
    ng = T // tg

    def body(zu_ref, zv_ref, dgm_ref, gv_ref, ws_ref, bt_ref, dz_ref, dgv_ref, dws_ref, dbt_ref, dsv_acc, dvn_s):
        i = pl.program_id(0)

        @pl.when(i == 0)
        def _():
            dgv_ref[...] = jnp.zeros_like(dgv_ref)
            dws_ref[...] = jnp.zeros_like(dws_ref)
            dsv_acc[...] = jnp.zeros_like(dsv_acc)

        zu = zu_ref[...]
        zv = zv_ref[...]
        dgm = dgm_ref[...]
        gvv = gv_ref[...]
        u = _gelu(zu)
        rv, vh, vn = _gmlp_norm(zv, gvv)
        dsv = dgm * u
        dsv_b = dsv.astype(BF16)
        for hd in range(GM_HEADS):
            wc = _causal_ws(ws_ref, hd)
            cols = slice(hd * CHUNK, (hd + 1) * CHUNK)
            dws = jnp.zeros((CHUNK, CHUNK), F32)
            dsv_sum = jnp.zeros((CHUNK, CHUNK), F32)
            for c in range(tg // CHUNK):
                rows = slice(c * CHUNK, (c + 1) * CHUNK)
                vch = vn[rows, cols]
                sv = _dot(wc, vch) + bt_ref[:, hd:hd + 1]
                dz_ref[rows, cols] = (dgm[rows, cols] * sv * _gelu_grad(zu[rows, cols])).astype(BF16)
                dws += _dot_nt(dsv_b[rows, cols], vch)
                dsv_sum += dsv[rows, cols]
                dvn_s[rows, cols] = _dot_tn(wc, dsv_b[rows, cols])
            dws_ref[hd] += dws
            dsv_acc[:, cols] += dsv_sum
        dvn = dvn_s[...]
        dvh = dvn * gvv
        dv = rv * (dvh - vh * jnp.mean(dvh * vh, axis=-1, keepdims=True))
        dgv_ref[...] += jnp.sum(dvn * vh, axis=0, keepdims=True)
        dz_ref[:, GM_WIDTH:] = (dv * _gelu_grad(zv)).astype(BF16)

        @pl.when(i == ng - 1)
        def _():
            r = lax.broadcasted_iota(jnp.int32, (CHUNK, CHUNK), 0)
            c = lax.broadcasted_iota(jnp.int32, (CHUNK, CHUNK), 1)
            for hd in range(GM_HEADS):
                dws_ref[hd] = jnp.where(r >= c, dws_ref[hd], 0.0)
                dbt_ref[:, hd:hd + 1] = jnp.sum(dsv_acc[:, hd * CHUNK:(hd + 1) * CHUNK], axis=1, keepdims=True)

    return pl.pallas_call(
        body, name="gmlp_bwd", grid=(ng,),
        in_specs=[pl.BlockSpec((tg, GM_WIDTH), lambda i: (i, 0)), pl.BlockSpec((tg, GM_WIDTH), lambda i: (i, 1)),
                  pl.BlockSpec((tg, GM_WIDTH), lambda i: (i, 0)),
                  pl.BlockSpec((1, GM_WIDTH), lambda i: (0, 0)),
                  pl.BlockSpec((GM_HEADS, CHUNK, CHUNK), lambda i: (0, 0, 0)),
                  pl.BlockSpec((CHUNK, GM_HEADS), lambda i: (0, 0))],
        out_specs=[pl.BlockSpec((tg, 2 * GM_WIDTH), lambda i: (i, 0)),
                   pl.BlockSpec((1, GM_WIDTH), lambda i: (0, 0)),
                   pl.BlockSpec((GM_HEADS, CHUNK, CHUNK), lambda i: (0, 0, 0)),
                   pl.BlockSpec((CHUNK, GM_HEADS), lambda i: (0, 0))],
        out_shape=[jax.ShapeDtypeStruct((T, 2 * GM_WIDTH), BF16), jax.ShapeDtypeStruct((1, GM_WIDTH), F32),
                   jax.ShapeDtypeStruct((GM_HEADS, CHUNK, CHUNK), F32),
                   jax.ShapeDtypeStruct((CHUNK, GM_HEADS), F32)],
        scratch_shapes=[pltpu.VMEM((CHUNK, GM_WIDTH), F32), pltpu.VMEM((tg, GM_WIDTH), F32)],
        compiler_params=_params("arbitrary"),
    )(zuv, zuv, d_gm, gv, ws, b_t)


def _scan_matrix(blk, keep):
    r = lax.broadcasted_iota(jnp.int32, (blk, blk), 0)
    c = lax.broadcasted_iota(jnp.int32, (blk, blk), 1)
    return jnp.where(keep(r, c), 1.0, 0.0).astype(BF16)


def _scan_matrix2(blk, keep, value):
    m = _scan_matrix(blk, keep) * value
    return jnp.concatenate([m, m], axis=0)


def _scan(x, mat2):
    hi, lo = _split_bf16(x)
    return _dot(jnp.concatenate([hi, lo], axis=1), mat2)


def _head_masks(q):
    lane = lax.broadcasted_iota(jnp.int32, q.shape, 1)
    m0 = lane < SB_HEAD_DIM
    zero = jnp.zeros_like(q)
    return m0, jnp.where(m0, q, zero), jnp.where(m0, zero, q)


_LOG2E = 1.4426950408889634


def _softplus_parts(z):
    e = jnp.exp2(jnp.abs(z) * (-_LOG2E))
    ope = 1.0 + e
    return e, ope, jnp.maximum(z, 0.0) + jnp.log(ope)


def _attn_fwd(qkv, rider=None):
    T = qkv.shape[0]
    tk = ATTN_KEY_BLOCK
    tq = min(ATTN_Q_ROWS, T)
    band = tq // tk
    assert band % ATTN_UNROLL == 0 or T == tq
    ngrp = SB_WIDTH // LANES

    def body(q_ref, k_ref, v_ref, o_ref, l_ref, acc, run):
        i = pl.program_id(1)
        suffix = _scan_matrix2(tk, lambda r, c: r >= c, -1.0)
        row = lax.broadcasted_iota(jnp.int32, (tq, tk), 0)
        col = lax.broadcasted_iota(jnp.int32, (tq, tk), 1)
        m0, q0, q1 = _head_masks(q_ref[...] * SB_SCALE)
        acc[...] = jnp.zeros_like(acc)
        run[...] = jnp.zeros_like(run)

        def tiles(work):
            heads = (q0, q1)
            kv = []
            for j, _ in work:
                start = pl.multiple_of(j * tk, tk)
                kv.append((k_ref[pl.ds(start, tk), :], v_ref[pl.ds(start, tk), :]))
            z = [[_dot_nt(qh, kj) for qh in heads] for kj, _ in kv]
            sp = [[_softplus_parts(zz)[2] for zz in zt] for zt in z]
            sp = [[s if m is None else jnp.where(m, s, 0.0) for s in st] for st, (_, m) in zip(sp, work)]
            res = [[_scan(s, suffix) for s in st] for st in sp]
            runs = [run[hd] for hd in range(len(heads))]
            a = []
            for t, (_, m) in enumerate(work):
                at = []
                for hd in range(len(heads)):
                    av = jnp.exp(z[t][hd] + (runs[hd] + res[t][hd]))
                    at.append(av if m is None else jnp.where(m, av, 0.0))
                    runs[hd] = runs[hd] + res[t][hd][:, 0:1]
                a.append(at)
            for hd in range(len(heads)):
                run[hd] = runs[hd]
                upd = _dot(a[0][hd].astype(BF16), kv[0][1])
                for t in range(1, len(work)):
                    upd = upd + _dot(a[t][hd].astype(BF16), kv[t][1])
                acc[hd] += upd

        tiles([(i * band + jb, jb * tk + col < row) for jb in reversed(range(band))])

        def full_step(it, carry):
            tiles([(i * band - 1 - ATTN_UNROLL * it - u, None) for u in range(ATTN_UNROLL)])
            return carry

        lax.fori_loop(0, i * (band // ATTN_UNROLL), full_step, 0)
        o_ref[...] = jnp.where(m0, acc[0], acc[1]).astype(BF16)
        l_ref[...] = jnp.where(m0, jnp.broadcast_to(run[0], (tq, LANES)), jnp.broadcast_to(run[1], (tq, LANES)))

    return _pallas(
        body, rider, name="attn_fwd", grid=(ngrp, T // tq),
        in_specs=[pl.BlockSpec((tq, LANES), lambda g, i: (i, g)),
                  pl.BlockSpec((T, LANES), lambda g, i: (0, ngrp + g)),
                  pl.BlockSpec((T, LANES), lambda g, i: (0, 2 * ngrp + g))],
        out_specs=[pl.BlockSpec((tq, LANES), lambda g, i: (i, g)),
                   pl.BlockSpec((tq, LANES), lambda g, i: (i, g))],
        out_shape=[jax.ShapeDtypeStruct((T, SB_WIDTH), BF16), jax.ShapeDtypeStruct((T, SB_WIDTH), F32)],
        scratch_shapes=[pltpu.VMEM((2, tq, LANES), F32), pltpu.VMEM((2, tq, 1), F32)],
        compiler_params=_params("arbitrary", "arbitrary"),
    )(qkv, qkv, qkv)


def _attn_bwd(qkv, d_o, ltot, rider=None):
    T = qkv.shape[0]
    tk = ATTN_KEY_BLOCK
    tq = min(ATTN_BWD_Q_ROWS, T)
    band = tq // tk
    nq = T // tq
    ngrp = SB_WIDTH // LANES

    def body(q_ref, k_ref, v_ref, do_ref, l_ref, dq_ref, dk_ref, dv_ref, dq_acc, dk_acc, dv_acc, lpre, ppre):
        i = pl.program_id(1)

        @pl.when(i == 0)
        def _():
            dk_acc[...] = jnp.zeros_like(dk_acc)
            dv_acc[...] = jnp.zeros_like(dv_acc)

        excl = _scan_matrix(tk, lambda r, c: r < c)
        excl2 = jnp.concatenate([excl, excl], axis=0)
        row = lax.broadcasted_iota(jnp.int32, (tq, tk), 0)
        col = lax.broadcasted_iota(jnp.int32, (tq, tk), 1)
        m0, q0, q1 = _head_masks(q_ref[...] * SB_SCALE)
        _, d0, d1 = _head_masks(do_ref[...].astype(BF16))
        lt = l_ref[...]
        ltots = (lt[:, 0:1], lt[:, SB_HEAD_DIM:SB_HEAD_DIM + 1])
        dq_acc[...] = jnp.zeros_like(dq_acc)
        lpre[...] = jnp.zeros_like(lpre)
        ppre[...] = jnp.zeros_like(ppre)

        def tiles(work):
            heads = ((q0, d0), (q1, d1))
            nhd = len(heads)
            starts = [pl.multiple_of(j * tk, tk) for j, _ in work]
            kv = [(k_ref[pl.ds(st, tk), :], v_ref[pl.ds(st, tk), :]) for st in starts]
            masks = [m for _, m in work]
            every = [(t, hd) for t in range(len(work)) for hd in range(nhd)]
            z = {(t, hd): _dot_nt(heads[hd][0], kv[t][0]) for t, hd in every}
            da = {(t, hd): _dot_nt(heads[hd][1], kv[t][1]) for t, hd in every}
            sp, beta = {}, {}
            for key in every:
                s = _softplus_parts(z[key])[2]
                beta[key] = jnp.exp(z[key] - s)
                sp[key] = s if masks[key[0]] is None else jnp.where(masks[key[0]], s, 0.0)
            res = {key: _scan(sp[key], excl2) for key in every}
            lp = [lpre[hd] for hd in range(nhd)]
            a, p = {}, {}
            for t, hd in every:
                av = jnp.exp(z[t, hd] + ((ltots[hd] + lp[hd]) + res[t, hd]))
                a[t, hd] = av if masks[t] is None else jnp.where(masks[t], av, 0.0)
                p[t, hd] = a[t, hd] * da[t, hd]
                lp[hd] = lp[hd] + (res[t, hd][:, tk - 1:tk] + sp[t, hd][:, tk - 1:tk])
            resp = {key: _dot(p[key].astype(BF16), excl) for key in every}
            pp = [ppre[hd] for hd in range(nhd)]
            dzb = {}
            for t, hd in every:
                dz = p[t, hd] - beta[t, hd] * (p[t, hd] + (pp[hd] + resp[t, hd]))
                if masks[t] is not None:
                    dz = jnp.where(masks[t], dz, 0.0)
                dzb[t, hd] = dz.astype(BF16)
                pp[hd] = pp[hd] + (resp[t, hd][:, tk - 1:tk] + p[t, hd][:, tk - 1:tk])
            for hd in range(nhd):
                lpre[hd] = lp[hd]
                ppre[hd] = pp[hd]
                upd = _dot(dzb[0, hd], kv[0][0])
                for t in range(1, len(work)):
                    upd = upd + _dot(dzb[t, hd], kv[t][0])
                dq_acc[hd] += upd
            for t, st in enumerate(starts):
                dk = _dot_tn(dzb[t, 0], heads[0][0])
                dv = _dot_tn(a[t, 0].astype(BF16), heads[0][1])
                for hd in range(1, nhd):
                    dk = dk + _dot_tn(dzb[t, hd], heads[hd][0])
                    dv = dv + _dot_tn(a[t, hd].astype(BF16), heads[hd][1])
                dk_acc[pl.ds(st, tk), :] += dk
                dv_acc[pl.ds(st, tk), :] += dv

        def full_step(j, carry):
            tiles([(j, None)])
            return carry

        lax.fori_loop(0, i * band, full_step, 0)
        for jb in range(band):
            tiles([(i * band + jb, jb * tk + col < row)])
        dq_ref[...] = (jnp.where(m0, dq_acc[0], dq_acc[1]) * SB_SCALE).astype(BF16)

        @pl.when(i == nq - 1)
        def _():
            dk_ref[...] = dk_acc[...].astype(BF16)
            dv_ref[...] = dv_acc[...].astype(BF16)

    qmap = lambda g, i: (i, g)
    return _pallas(
        body, rider, name="attn_bwd", grid=(ngrp, nq),
        in_specs=[pl.BlockSpec((tq, LANES), qmap),
                  pl.BlockSpec((T, LANES), lambda g, i: (0, ngrp + g)),
                  pl.BlockSpec((T, LANES), lambda g, i: (0, 2 * ngrp + g)),
                  pl.BlockSpec((tq, LANES), qmap), pl.BlockSpec((tq, LANES), qmap)],
        out_specs=[pl.BlockSpec((tq, LANES), qmap),
                   pl.BlockSpec((T, LANES), lambda g, i: (0, g)),
                   pl.BlockSpec((T, LANES), lambda g, i: (0, g))],
        out_shape=[jax.ShapeDtypeStruct((T, SB_WIDTH), BF16)] * 3,
        scratch_shapes=[pltpu.VMEM((2, tq, LANES), F32), pltpu.VMEM((T, LANES), F32),
                        pltpu.VMEM((T, LANES), F32), pltpu.VMEM((2, tq, 1), F32),
                        pltpu.VMEM((2, tq, 1), F32)],
        compiler_params=_params("arbitrary", "arbitrary"),
    )(qkv, qkv, qkv, d_o, ltot)


def _mix_out_fwd(h, gm, sb, w):
    T, D = h.shape
    tm = min(ROW_TILE, T)

    def body(h_ref, gm_ref, sb_ref, w_ref, o_ref):
        o_ref[...] = h_ref[...] + _dot(gm_ref[...], w_ref[:GM_WIDTH, :]) + _dot(sb_ref[...], w_ref[GM_WIDTH:, :])

    row = lambda i: (i, 0)
    return pl.pallas_call(
        body, name="mix_out_fwd", grid=(T // tm,),
        in_specs=[pl.BlockSpec((tm, D), row), pl.BlockSpec((tm, GM_WIDTH), row), pl.BlockSpec((tm, SB_WIDTH), row),
                  pl.BlockSpec((GM_WIDTH + SB_WIDTH, D), lambda i: (0, 0))],
        out_specs=pl.BlockSpec((tm, D), row),
        out_shape=jax.ShapeDtypeStruct((T, D), F32),
        compiler_params=_params("arbitrary"),
    )(h, gm, sb, w)


def _mix_out_bwd(dh, w):
    T, D = dh.shape
    tm = min(ROW_TILE, T)

    def body(dh_ref, w_ref, dgm_ref, dsb_ref, dhb_ref):
        dhb = dh_ref[...].astype(BF16)
        dhb_ref[...] = dhb
        dgm_ref[...] = _dot_nt(dhb, w_ref[:GM_WIDTH, :])
        dsb_ref[...] = _dot_nt(dhb, w_ref[GM_WIDTH:, :])

    row = lambda i: (i, 0)
    return pl.pallas_call(
        body, name="mix_out_bwd", grid=(T // tm,),
        in_specs=[pl.BlockSpec((tm, D), row), pl.BlockSpec((GM_WIDTH + SB_WIDTH, D), lambda i: (0, 0))],
        out_specs=[pl.BlockSpec((tm, GM_WIDTH), row), pl.BlockSpec((tm, SB_WIDTH), row), pl.BlockSpec((tm, D), row)],
        out_shape=[jax.ShapeDtypeStruct((T, GM_WIDTH), F32), jax.ShapeDtypeStruct((T, SB_WIDTH), F32),
                   jax.ShapeDtypeStruct((T, D), BF16)],
        compiler_params=_params("arbitrary"),
    )(dh, w)


def _tail(h3, p, target, g_ple, g_fin, w_gate, w_proj):
    T, D = h3.shape
    PD = p.shape[1]
    tm = min(ROW_TILE, T)

    def body(h_ref, p_ref, t_ref, gp_ref, gf_ref, wg_ref, wp_ref,
             loss_ref, dh_ref, n4_ref, dgl_ref, dpp_ref, dgp_ref, dgf_ref):
        @pl.when(pl.program_id(0) == 0)
        def _():
            loss_ref[...] = jnp.zeros_like(loss_ref)
            dgp_ref[...] = jnp.zeros_like(dgp_ref)
            dgf_ref[...] = jnp.zeros_like(dgf_ref)

        h3v = h_ref[...]
        gp = gp_ref[...]
        gf = gf_ref[...]
        r3, xh3 = _rms_parts(h3v)
        n4 = (xh3 * gp).astype(BF16)
        n4_ref[...] = n4
        gate = _sigmoid(_dot(n4, wg_ref[...]))
        pp = _dot(p_ref[...].astype(BF16), wp_ref[...])
        h4 = h3v + gate * pp
        r4, xh4 = _rms_parts(h4)
        err = xh4 * gf - t_ref[...]
        loss_ref[...] += jnp.full(loss_ref.shape, (0.5 / D) * jnp.sum(err * err), F32)
        dy = err * (1.0 / D)
        dgf_ref[...] += jnp.sum(dy * xh4, axis=0, keepdims=True)
        dyg = dy * gf
        dh4 = r4 * (dyg - xh4 * jnp.mean(dyg * xh4, axis=-1, keepdims=True))
        dpp_ref[...] = (dh4 * gate).astype(BF16)
        dgl = (dh4 * pp * gate * (1.0 - gate)).astype(BF16)
        dgl_ref[...] = dgl
        dn4 = _dot_nt(dgl, wg_ref[...])
        dgp_ref[...] += jnp.sum(dn4 * xh3, axis=0, keepdims=True)
        dn4g = dn4 * gp
        dh_ref[...] = dh4 + r3 * (dn4g - xh3 * jnp.mean(dn4g * xh3, axis=-1, keepdims=True))

    row = lambda i: (i, 0)
    one = lambda i: (0, 0)
    return pl.pallas_call(
        body, name="tail", grid=(T // tm,),
        in_specs=[pl.BlockSpec((tm, D), row), pl.BlockSpec((tm, PD), row), pl.BlockSpec((tm, D), row),
                  pl.BlockSpec((1, D), one), pl.BlockSpec((1, D), one),
                  pl.BlockSpec((D, D), one), pl.BlockSpec((PD, D), one)],
        out_specs=[pl.BlockSpec((1, LANES), one), pl.BlockSpec((tm, D), row), pl.BlockSpec((tm, D), row),
                   pl.BlockSpec((tm, D), row), pl.BlockSpec((tm, D), row),
                   pl.BlockSpec((1, D), one), pl.BlockSpec((1, D), one)],
        out_shape=[jax.ShapeDtypeStruct((1, LANES), F32), jax.ShapeDtypeStruct((T, D), F32),
                   jax.ShapeDtypeStruct((T, D), BF16), jax.ShapeDtypeStruct((T, D), BF16),
                   jax.ShapeDtypeStruct((T, D), BF16),
                   jax.ShapeDtypeStruct((1, D), F32), jax.ShapeDtypeStruct((1, D), F32)],
        compiler_params=_params("arbitrary"),
    )(h3, p, target, g_ple, g_fin, w_gate, w_proj)


FFN1_W = ("ffn1_w_in", "ffn1_w_out")
MIX_W = ("w_mix_in", "w_mix_out")
REST_W = ("ffn2_w_in", "ffn2_w_out", "ple_w_gate", "ple_w_proj")
BIG_W = FFN1_W + MIX_W + REST_W
COLUMN_SHARDED = ("w_mix_in", "ple_w_proj")


class _Traffic:
    def __init__(self, shards):
        self.shards = shards
        self.parts = {}

    @staticmethod
    def _full(name, gathered):
        if name in COLUMN_SHARDED:
            return jnp.transpose(gathered, (1, 0, 2)).reshape(gathered.shape[1], -1)
        if name.endswith("_w_in"):
            return gathered
        return gathered.reshape(-1, gathered.shape[-1])

    @staticmethod
    def _blocks(name, grad):
        name = name.split("/")[0]
        if name in COLUMN_SHARDED:
            return jnp.transpose(grad.reshape(grad.shape[0], N_DEV, -1), (1, 0, 2))
        if name.endswith("_w_in"):
            return grad
        return grad.reshape(N_DEV, -1, grad.shape[-1])

    def gather_now(self, names):
        got = _exchange("gather_" + names[0], [self.shards[n] for n in names], [False] * len(names))
        return self.gathered(names, got)

    def gather_rider(self, names):
        return [self.shards[n] for n in names], [False] * len(names)

    def gathered(self, names, got):
        return {n: self._full(n, g) for n, g in zip(names, got)}

    def scatter_rider(self, grads, gather=()):
        return ([self._blocks(n, g) for n, g in grads.items()] + list(gather),
                [True] * len(grads) + [False] * len(gather))

    def scattered(self, names, got):
        self.parts.update(zip(names, got))
        return got[len(names):]


def _local_step(traffic, x, p, target, g1, gmix, gv, ws, b_t, g2, gple, gfin, pack_small):
    T, D = x.shape
    tm = min(ROW_TILE, T)

    w = traffic.gather_now(FFN1_W)
    h1, n1, G1, U1, a1, *got = _ffn_fwd("ffn1_fwd", x, g1, w["ffn1_w_in"], w["ffn1_w_out"],
                                        rider=traffic.gather_rider(MIX_W))
    w.update(traffic.gathered(MIX_W, got))
    n2, zuv, qkv = _mix_in_fwd(h1, gmix, w["w_mix_in"])
    gm = _gmlp_fwd(zuv, gv, ws, b_t)
    sb, ltot, *got = _attn_fwd(qkv, rider=traffic.gather_rider(REST_W))
    w.update(traffic.gathered(REST_W, got))
    h2 = _mix_out_fwd(h1, gm, sb, w["w_mix_out"])
    h3, n3, G2, U2, a2 = _ffn_fwd("ffn2_fwd", h2, g2, w["ffn2_w_in"], w["ffn2_w_out"])
    loss, dh3, n4, d_gl, d_pp, dg_ple, dg_fin = _tail(h3, p, target, gple, gfin, w["ple_w_gate"], w["ple_w_proj"])

    nb, _, FB = w["ffn1_w_in"].shape
    nh = nb // 2

    tt = min(GRAD_ROW_TILE, T)

    def dw_out(name, a, d_out):
        return _matmul_tn(name, a, d_out, nh, (1, tt, FB), lambda j, t: (j, t, 0), (tt, D), lambda j, t: (t, 0),
                          (nh, FB, D), (1, FB, D), lambda j, t: (j, 0, 0))

    def dense_tn(name, a, b, ncol):
        ka, nbw = a.shape[1], b.shape[1] // ncol
        return _matmul_tn(name, a, b, ncol, (tt, ka), lambda j, t: (t, 0), (tt, nbw), lambda j, t: (t, j),
                          (ka, b.shape[1]), (ka, nbw), lambda j, t: (0, j))

    grads = dict(ple_w_gate=dense_tn("dw_ple_gate", n4, d_gl, 2), ple_w_proj=dense_tn("dw_ple_proj", p, d_pp, 1))
    dh2, dg2, dG2, dU2, dout2 = _ffn_bwd("ffn2_bwd", dh3, h2, g2, G2, U2, w["ffn2_w_in"], w["ffn2_w_out"])
    grads["ffn2_w_in"], = _dw_in("ffn2_dw_in", n3, dG2, dU2)
    grads["ffn2_w_out"] = dw_out("ffn2_dw_out", a2, dout2)
    grads = {n: grads[n] for n in REST_W}

    d_gm, d_sb, dh2_bf = _mix_out_bwd(dh2, w["w_mix_out"])
    grads["w_mix_out"] = jnp.concatenate([dense_tn("dw_mix_out_gm", gm, dh2_bf, 1),
                                          dense_tn("dw_mix_out_sb", sb, dh2_bf, 1)], axis=0)
    dzuv, dgv, dws, db_t = _gmlp_bwd(zuv, d_gm, gv, ws, b_t)
    dq, dk, dv, *got = _attn_bwd(qkv, d_sb, ltot, rider=traffic.scatter_rider(grads))
    traffic.scattered(list(grads), got)
    dqkv = jnp.concatenate([dq, dk, dv], axis=1)
    dw_mi = jnp.concatenate([dense_tn("dw_mix_in_uv", n2, dzuv, 2), dense_tn("dw_mix_in_qkv", n2, dqkv, 3)], axis=1)
    dh1, dgmix, dout1 = _mix_in_bwd(dzuv, dqkv, w["w_mix_in"], h1, gmix, dh2)

    dw_out1 = dw_out("ffn1_dw_out", a1, dout1)
    dG1, dU1, *got = _ffn_bwd_gates("ffn1_bwd_gates", dout1, G1, U1, w["ffn1_w_out"],
                                    rider=traffic.scatter_rider(dict(w_mix_in=dw_mi)))
    traffic.scattered(["w_mix_in"], got)
    half = D // 2
    top, *got = _dw_in("ffn1_dw_in_top", n1[:, :half], dG1, dU1, rider=traffic.scatter_rider(dict(ffn1_w_out=dw_out1)))
    traffic.scattered(["ffn1_w_out"], got)
    bottom, *got = _dw_in("ffn1_dw_in_bottom", n1[:, half:], dG1, dU1, rider=traffic.scatter_rider({"ffn1_w_in/0": top}))
    traffic.scattered(["ffn1_w_in/0"], got)
    dx, dg1, *got = _ffn_bwd_input("ffn1_bwd_input", dh1, x, g1, dG1, dU1, w["ffn1_w_in"],
                                   rider=traffic.scatter_rider({"ffn1_w_in/1": bottom}))
    traffic.scattered(["ffn1_w_in/1"], got)

    small = pack_small(dict(ffn1_norm=dg1, mix_norm=dgmix, gmlp_v_norm=dgv, gmlp_w_s=dws, gmlp_b=jnp.transpose(db_t),
                            ffn2_norm=dg2, ple_norm=dg_ple, final_norm=dg_fin), loss)
    return dx, small


def _peer(d):
    x, y, c = lax.axis_index("x"), lax.axis_index("y"), lax.axis_index("c")
    px = 1 - x if d & 4 else x
    py = 1 - y if d & 2 else y
    pc = 1 - c if d & 1 else c
    return (px, py, pc), 4 * px + 2 * py + pc


N_CHIPS_AWAY = 3


class _ExchangePlan:
    def __init__(self, ins, outs, send, recv, local, scatter):
        self.ins, self.outs, self.send, self.recv, self.local, self.scatter = ins, outs, send, recv, local, scatter
        self.me = _peer(0)[1]

    def _remote(self, t, sem, src, slot, peer):
        return pltpu.make_async_remote_copy(
            src_ref=src, dst_ref=self.outs[t].at[slot], send_sem=self.send.at[t, sem], recv_sem=self.recv.at[t, sem],
            device_id=peer, device_id_type=MESH)

    def _own(self, t):
        src = self.ins[t].at[self.me] if self.scatter[t] else self.ins[t]
        return pltpu.make_async_copy(src, self.outs[t].at[self.me], self.local.at[t])

    def _n_direct(self, t):
        return N_DEV - 1 if self.scatter[t] else N_CHIPS_AWAY + 1

    def _direct(self, t, k):
        if self.scatter[t]:
            peer, slot = _peer(k + 1)
            return self._remote(t, k, self.ins[t].at[slot], self.me, peer)
        return self._remote(t, k, self.ins[t], self.me, _peer(2 * k if k else 1)[0])

    def _relay(self, t, c):
        slot = _peer(2 * c)[1]
        return self._remote(t, N_CHIPS_AWAY + c, self.outs[t].at[slot], slot, _peer(1)[0])

    def start(self):
        for t in range(len(self.ins)):
            self._own(t).start()
            for k in range(self._n_direct(t)):
                self._direct(t, k).start()

    def relay(self):
        for t in self._gathers():
            for c in range(1, N_CHIPS_AWAY + 1):
                self._direct(t, c).wait_recv()
                self._relay(t, c).start()

    def _gathers(self):
        return [t for t in range(len(self.ins)) if not self.scatter[t]]

    def finish(self):
        for t in range(len(self.ins)):
            self._own(t).wait()
            for k in range(self._n_direct(t)):
                self._direct(t, k).wait_send()
                if self.scatter[t] or k == 0:
                    self._direct(t, k).wait_recv()
        for t in self._gathers():
            for c in range(1, N_CHIPS_AWAY + 1):
                self._relay(t, c).wait()


def _exchange_shapes(arrays, scatter):
    return [jax.ShapeDtypeStruct(a.shape if sc else (N_DEV,) + a.shape, a.dtype) for a, sc in zip(arrays, scatter)]


def _exchange_sems(n):
    return [pltpu.SemaphoreType.DMA((n, N_DEV - 1)), pltpu.SemaphoreType.DMA((n, N_DEV - 1)),
            pltpu.SemaphoreType.DMA((n,))]


_ANY = pl.BlockSpec(memory_space=pl.ANY)


def _exchange(name, arrays, scatter):
    n = len(arrays)

    def body(*refs):
        plan = _ExchangePlan(refs[:n], refs[n:2 * n], *refs[2 * n:], scatter)
        plan.start()
        plan.relay()
        plan.finish()

    return pl.pallas_call(
        body, name=name, in_specs=[_ANY] * n, out_specs=[_ANY] * n, out_shape=_exchange_shapes(arrays, scatter),
        scratch_shapes=_exchange_sems(n),
    )(*arrays)


def _pallas(body, rider, *, name, grid, in_specs, out_specs, out_shape, scratch_shapes=(), compiler_params=None):
    if rider is None:
        return pl.pallas_call(body, name=name, grid=grid, in_specs=in_specs, out_specs=out_specs, out_shape=out_shape,
                              scratch_shapes=list(scratch_shapes), compiler_params=compiler_params)
    arrays, scatter = rider
    n, ni, no, ns = len(arrays), len(in_specs), len(out_specs), len(scratch_shapes)

    def carried(*refs):
        ins, r_in = refs[:ni], refs[ni:ni + n]
        outs, r_out = refs[ni + n:ni + n + no], refs[ni + n + no:ni + 2 * n + no]
        scratch, sems = refs[ni + 2 * n + no:ni + 2 * n + no + ns], refs[ni + 2 * n + no + ns:]
        step = 0
        for ax, g in enumerate(grid):
            step = step * g + pl.program_id(ax)
        steps = functools.reduce(lambda a, b: a * b, grid)

        @pl.when(step == 0)
        def _():
            _ExchangePlan(r_in, r_out, *sems, scatter).start()

        @pl.when(step == steps // 2)
        def _():
            _ExchangePlan(r_in, r_out, *sems, scatter).relay()

        body(*ins, *outs, *scratch)

        @pl.when(step == steps - 1)
        def _():
            _ExchangePlan(r_in, r_out, *sems, scatter).finish()

    call = pl.pallas_call(
        carried, name=name, grid=grid, in_specs=list(in_specs) + [_ANY] * n, out_specs=list(out_specs) + [_ANY] * n,
        out_shape=list(out_shape) + _exchange_shapes(arrays, scatter),
        scratch_shapes=list(scratch_shapes) + _exchange_sems(n), compiler_params=compiler_params)
    return lambda *args: call(*args, *arrays)


def _adamw_math(g, w, m, v):
    m_new = ADAM_B1 * m + (1.0 - ADAM_B1) * g
    v_new = ADAM_B2 * v + (1.0 - ADAM_B2) * (g * g)
    m_hat = m_new / (1.0 - ADAM_B1 ** ADAM_STEP)
    v_hat = v_new / (1.0 - ADAM_B2 ** ADAM_STEP)
    delta = -ADAM_LR * (m_hat / (jnp.sqrt(v_hat) + ADAM_EPS) + ADAM_WD * w)
    return delta, m_new, v_new


def _adamw(name, parts, w, m, v, rider=None):
    R, C = w.shape
    tr = R
    for cand in (256, 128, 64, 32, 16, 8):
        if R % cand == 0:
            tr = cand
            break

    def body(p_ref, w_ref, m_ref, v_ref, g_ref, d_ref, nm_ref, nv_ref):
        g = p_ref[0].astype(F32)
        for j in range(1, N_DEV):
            g = g + p_ref[j].astype(F32)
        g_ref[...] = g
        d_ref[...], nm_ref[...], nv_ref[...] = _adamw_math(g, w_ref[...], m_ref[...], v_ref[...])

    row = lambda i: (i, 0)
    spec = pl.BlockSpec((tr, C), row)
    return _pallas(
        body, rider, name=name, grid=(R // tr,),
        in_specs=[pl.BlockSpec((N_DEV, tr, C), lambda i: (0, i, 0)), spec, spec, spec],
        out_specs=[spec] * 4,
        out_shape=[jax.ShapeDtypeStruct((R, C), F32)] * 4,
        compiler_params=_params("arbitrary"),
    )(parts, w, m, v)


def _rows128(a):
    flat = a.reshape(-1, LANES)
    pad = (-flat.shape[0]) % SMALL_ROWS_ALIGN
    return jnp.pad(flat, ((0, pad), (0, 0))) if pad else flat


def _unrows(packed, like):
    n = like.size // LANES
    return packed[:n].reshape(like.shape)


def kernel(x, p, ffn1_norm, ffn1_w_in, ffn1_w_out, mix_norm, w_mix_in, gmlp_v_norm, gmlp_w_s, gmlp_b, w_mix_out, ffn2_norm, ffn2_w_in, ffn2_w_out, ple_norm, ple_w_gate, ple_w_proj, final_norm, loss_target, m_ffn1_norm, m_ffn1_w_in, m_ffn1_w_out, m_mix_norm, m_w_mix_in, m_gmlp_v_norm, m_gmlp_w_s, m_gmlp_b, m_w_mix_out, m_ffn2_norm, m_ffn2_w_in, m_ffn2_w_out, m_ple_norm, m_ple_w_gate, m_ple_w_proj, m_final_norm, v_ffn1_norm, v_ffn1_w_in, v_ffn1_w_out, v_mix_norm, v_w_mix_in, v_gmlp_v_norm, v_gmlp_w_s, v_gmlp_b, v_w_mix_out, v_ffn2_norm, v_ffn2_w_in, v_ffn2_w_out, v_ple_norm, v_ple_w_gate, v_ple_w_proj, v_final_norm):
    names = ["ffn1_norm", "ffn1_w_in", "ffn1_w_out", "mix_norm", "w_mix_in", "gmlp_v_norm", "gmlp_w_s", "gmlp_b",
             "w_mix_out", "ffn2_norm", "ffn2_w_in", "ffn2_w_out", "ple_norm", "ple_w_gate", "ple_w_proj", "final_norm"]
    W = dict(zip(names, [ffn1_norm, ffn1_w_in, ffn1_w_out, mix_norm, w_mix_in, gmlp_v_norm, gmlp_w_s, gmlp_b,
                         w_mix_out, ffn2_norm, ffn2_w_in, ffn2_w_out, ple_norm, ple_w_gate, ple_w_proj, final_norm]))
    M = dict(zip(names, [m_ffn1_norm, m_ffn1_w_in, m_ffn1_w_out, m_mix_norm, m_w_mix_in, m_gmlp_v_norm, m_gmlp_w_s,
                         m_gmlp_b, m_w_mix_out, m_ffn2_norm, m_ffn2_w_in, m_ffn2_w_out, m_ple_norm, m_ple_w_gate,
                         m_ple_w_proj, m_final_norm]))
    V = dict(zip(names, [v_ffn1_norm, v_ffn1_w_in, v_ffn1_w_out, v_mix_norm, v_w_mix_in, v_gmlp_v_norm, v_gmlp_w_s,
                         v_gmlp_b, v_w_mix_out, v_ffn2_norm, v_ffn2_w_in, v_ffn2_w_out, v_ple_norm, v_ple_w_gate,
                         v_ple_w_proj, v_final_norm]))
    small = [n for n in names if n not in BIG_W]
    D = x.shape[-1]

    def pack(src, last):
        return jnp.concatenate([_rows128(src[n]) for n in small] + [last], axis=0)

    offs = [0]
    for n in small:
        offs.append(offs[-1] + _rows128(W[n]).shape[0])

    traffic = _Traffic({n: W[n][0].astype(BF16) for n in BIG_W})
    dx, small_mine = _local_step(
        traffic, x[0], p[0, 0], loss_target[0],
        W["ffn1_norm"], W["mix_norm"], W["gmlp_v_norm"], W["gmlp_w_s"][0], jnp.transpose(W["gmlp_b"][0]),
        W["ffn2_norm"], W["ple_norm"], W["final_norm"].reshape(1, D),
        lambda grads, loss_part: pack(grads, jnp.broadcast_to(loss_part, (SMALL_ROWS_ALIGN, LANES))))

    parts = traffic.parts
    parts["ffn1_w_in"] = jnp.concatenate([parts["ffn1_w_in/0"], parts["ffn1_w_in/1"]], axis=1)
    out = {}
    carrier = "ffn2_w_out"
    *out[carrier], small_parts = _adamw("adamw_" + carrier, parts[carrier], W[carrier][0], M[carrier][0], V[carrier][0],
                                        rider=([small_mine], [False]))
    for n in BIG_W:
        if n != carrier:
            out[n] = _adamw("adamw_" + n, parts[n], W[n][0], M[n][0], V[n][0])
    zeros = jnp.zeros((SMALL_ROWS_ALIGN, LANES), F32)
    sg, sd, sm, sv = _adamw("adamw_small", small_parts, pack(W, zeros), pack(M, zeros), pack(V, zeros))
    for k, n in enumerate(small):
        out[n] = tuple(_unrows(arr[offs[k]:offs[k + 1]], W[n]) for arr in (sg, sd, sm, sv))
    loss = sg[offs[len(small)], 0]

    res = [loss, dx[None]]
    for k in range(4):
        res += [out[n][k].reshape(W[n].shape) for n in names]
    return tuple(res)
```

```python
import functools

import jax
import jax.numpy as jnp
from jax import lax
from jax.experimental import pallas as pl
from jax.experimental.pallas import tpu as pltpu

F32 = jnp.float32
BF16 = jnp.bfloat16
MESH = pl.DeviceIdType.MESH

N_DEV = 8
EPS = 1e-6
ADAM_LR = 0.001
ADAM_B1 = 0.9
ADAM_B2 = 0.999
ADAM_EPS = 1e-08
ADAM_WD = 0.01
ADAM_STEP = 10

GM_WIDTH = 512
GM_HEADS = 4
CHUNK = 128
SB_WIDTH = 512
SB_HEAD_DIM = 64
SB_SCALE = 0.125
LANES = 128
SMALL_ROWS_ALIGN = 8

ROW_TILE = 512
GRAD_ROW_TILE = 2048
DW_IN_ROW_TILE = 4096
FFN_FWD_ROW_TILE = 1024
ATTN_Q_ROWS = 512
ATTN_BWD_Q_ROWS = 512
ATTN_KEY_BLOCK = 256
ATTN_UNROLL = 2
VMEM_LIMIT = 56 * 1024 * 1024


def _params(*sem):
    return pltpu.CompilerParams(dimension_semantics=sem, vmem_limit_bytes=VMEM_LIMIT)


def _dot(a, b):
    return jnp.dot(a, b, preferred_element_type=F32)


def _dot_nt(a, b):
    return lax.dot_general(a, b, (((1,), (1,)), ((), ())), preferred_element_type=F32)


def _dot_tn(a, b):
    return lax.dot_general(a, b, (((0,), (0,)), ((), ())), preferred_element_type=F32)


def _rms_parts(x):
    r = lax.rsqrt(jnp.mean(x * x, axis=-1, keepdims=True) + EPS)
    return r, x * r


def _rms_bwd(x, g, dy):
    r, xh = _rms_parts(x)
    dyg = dy * g
    dx = r * (dyg - xh * jnp.mean(dyg * xh, axis=-1, keepdims=True))
    return dx, jnp.sum(dy * xh, axis=0, keepdims=True)


def _sigmoid(x):
    return 1.0 / (1.0 + jnp.exp(-x))


_SQRT_HALF = 0.7071067811865476
_INV_SQRT_2PI = 0.3989422804014327


def _gelu(x):
    return 0.5 * x * (1.0 + lax.erf(x * _SQRT_HALF))


def _gelu_grad(x):
    return 0.5 * (1.0 + lax.erf(x * _SQRT_HALF)) + x * (_INV_SQRT_2PI * jnp.exp(-0.5 * x * x))


def _split_bf16(x):
    hi = x.astype(BF16)
    lo = (x - hi.astype(F32)).astype(BF16)
    return hi, lo


def _ffn_fwd(name, h, gain, w_in, w_out, rider=None):
    T, D = h.shape
    nb, _, FB = w_in.shape
    nh = nb // 2
    tm = min(FFN_FWD_ROW_TILE, T)

    def body(h_ref, g_ref, wg_ref, wu_ref, wo_ref, ho_ref, n_ref, G_ref, U_ref, a_ref, n_s, acc):
        jj = pl.program_id(1)

        @pl.when(jj == 0)
        def _():
            _, xh = _rms_parts(h_ref[...])
            n = (xh * g_ref[...]).astype(BF16)
            n_s[...] = n
            n_ref[...] = n
            acc[...] = jnp.zeros_like(acc)

        n = n_s[...]
        G = _dot(n, wg_ref[0])
        U = _dot(n, wu_ref[0])
        G_ref[0] = G.astype(BF16)
        U_ref[0] = U.astype(BF16)
        a = (G * _sigmoid(G) * U).astype(BF16)
        a_ref[0] = a
        acc[...] += _dot(a, wo_ref[...])

        @pl.when(jj == nh - 1)
        def _():
            ho_ref[...] = h_ref[...] + 0.5 * acc[...]

    row = lambda i, j: (i, 0)
    blk = lambda i, j: (j, i, 0)
    return _pallas(
        body, rider, name=name, grid=(T // tm, nh),
        in_specs=[pl.BlockSpec((tm, D), row),
                  pl.BlockSpec((1, D), lambda i, j: (0, 0)),
                  pl.BlockSpec((1, D, FB), lambda i, j: (j, 0, 0)),
                  pl.BlockSpec((1, D, FB), lambda i, j: (j + nh, 0, 0)),
                  pl.BlockSpec((FB, D), lambda i, j: (j, 0))],
        out_specs=[pl.BlockSpec((tm, D), row), pl.BlockSpec((tm, D), row),
                   pl.BlockSpec((1, tm, FB), blk), pl.BlockSpec((1, tm, FB), blk),
                   pl.BlockSpec((1, tm, FB), blk)],
        out_shape=[jax.ShapeDtypeStruct((T, D), F32), jax.ShapeDtypeStruct((T, D), BF16),
                   jax.ShapeDtypeStruct((nh, T, FB), BF16), jax.ShapeDtypeStruct((nh, T, FB), BF16),
                   jax.ShapeDtypeStruct((nh, T, FB), BF16)],
        scratch_shapes=[pltpu.VMEM((tm, D), BF16), pltpu.VMEM((tm, D), F32)],
        compiler_params=_params("arbitrary", "arbitrary"),
    )(h, gain, w_in, w_in, w_out)


def _ffn_bwd(name, dh, h_in, gain, G, U, w_in, w_out, rider=None):
    T, D = dh.shape
    nb, _, FB = w_in.shape
    nh = nb // 2
    tm = min(ROW_TILE, T)

    def body(dh_ref, h_ref, g_ref, G_ref, U_ref, wg_ref, wu_ref, wo_ref,
             dhin_ref, dg_ref, dG_ref, dU_ref, do_ref, dn_acc, do_s):
        i = pl.program_id(0)
        jj = pl.program_id(1)

        @pl.when(jj == 0)
        def _():
            d_out = (0.5 * dh_ref[...]).astype(BF16)
            do_s[...] = d_out
            do_ref[...] = d_out
            dn_acc[...] = jnp.zeros_like(dn_acc)

        @pl.when((i == 0) & (jj == 0))
        def _():
            dg_ref[...] = jnp.zeros_like(dg_ref)

        halves = [slice(0, tm // 2), slice(tm // 2, tm)]
        da = [_dot_nt(do_s[rows, :], wo_ref[...]) for rows in halves]
        dGU = []
        for rows, dav in zip(halves, da):
            dG, dU = _gate_grads(dav, G_ref[0, rows, :].astype(F32), U_ref[0, rows, :].astype(F32))
            dG_ref[0, rows, :] = dG
            dU_ref[0, rows, :] = dU
            dGU.append((dG, dU))
        dn = [_dot_nt(dG, wg_ref[0]) for dG, _ in dGU]
        dn = [d + _dot_nt(dU, wu_ref[0]) for d, (_, dU) in zip(dn, dGU)]
        for rows, d in zip(halves, dn):
            dn_acc[rows, :] += d

        @pl.when(jj == nh - 1)
        def _():
            dx, dg = _rms_bwd(h_ref[...], g_ref[...], dn_acc[...])
            dhin_ref[...] = dh_ref[...] + dx
            dg_ref[...] += dg

    row = lambda i, j: (i, 0)
    blk = lambda i, j: (j, i, 0)
    one = lambda i, j: (0, 0)
    return _pallas(
        body, rider, name=name, grid=(T // tm, nh),
        in_specs=[pl.BlockSpec((tm, D), row), pl.BlockSpec((tm, D), row), pl.BlockSpec((1, D), one),
                  pl.BlockSpec((1, tm, FB), blk), pl.BlockSpec((1, tm, FB), blk),
                  pl.BlockSpec((1, D, FB), lambda i, j: (j, 0, 0)),
                  pl.BlockSpec((1, D, FB), lambda i, j: (j + nh, 0, 0)),
                  pl.BlockSpec((FB, D), lambda i, j: (j, 0))],
        out_specs=[pl.BlockSpec((tm, D), row), pl.BlockSpec((1, D), one),
                   pl.BlockSpec((1, tm, FB), blk), pl.BlockSpec((1, tm, FB), blk),
                   pl.BlockSpec((tm, D), row)],
        out_shape=[jax.ShapeDtypeStruct((T, D), F32), jax.ShapeDtypeStruct((1, D), F32),
                   jax.ShapeDtypeStruct((nh, T, FB), BF16), jax.ShapeDtypeStruct((nh, T, FB), BF16),
                   jax.ShapeDtypeStruct((T, D), BF16)],
        scratch_shapes=[pltpu.VMEM((tm, D), F32), pltpu.VMEM((tm, D), BF16)],
        compiler_params=_params("arbitrary", "arbitrary"),
    )(dh, h_in, gain, G, U, w_in, w_in, w_out)


def _gate_grads(dav, Gv, Uv):
    sig = _sigmoid(Gv)
    return (dav * Uv * (sig * (1.0 + Gv * (1.0 - sig)))).astype(BF16), (dav * (Gv * sig)).astype(BF16)


def _ffn_bwd_gates(name, d_out, G, U, w_out, rider=None):
    T, D = d_out.shape
    nh, _, FB = G.shape
    tm = min(FFN_FWD_ROW_TILE, T)

    def body(do_ref, G_ref, U_ref, wo_ref, dG_ref, dU_ref):
        halves = [slice(0, tm // 2), slice(tm // 2, tm)]
        da = [_dot_nt(do_ref[rows, :], wo_ref[...]) for rows in halves]
        for rows, dav in zip(halves, da):
            dG_ref[0, rows, :], dU_ref[0, rows, :] = _gate_grads(
                dav, G_ref[0, rows, :].astype(F32), U_ref[0, rows, :].astype(F32))

    blk = lambda i, j: (j, i, 0)
    return _pallas(
        body, rider, name=name, grid=(T // tm, nh),
        in_specs=[pl.BlockSpec((tm, D), lambda i, j: (i, 0)), pl.BlockSpec((1, tm, FB), blk),
                  pl.BlockSpec((1, tm, FB), blk), pl.BlockSpec((FB, D), lambda i, j: (j, 0))],
        out_specs=[pl.BlockSpec((1, tm, FB), blk), pl.BlockSpec((1, tm, FB), blk)],
        out_shape=[jax.ShapeDtypeStruct((nh, T, FB), BF16), jax.ShapeDtypeStruct((nh, T, FB), BF16)],
        compiler_params=_params("arbitrary", "arbitrary"),
    )(d_out, G, U, w_out)


def _ffn_bwd_input(name, dh, h_in, gain, dG, dU, w_in, rider=None):
    T, D = dh.shape
    nb, _, FB = w_in.shape
    nh = nb // 2
    tm = min(FFN_FWD_ROW_TILE, T)

    def body(dh_ref, h_ref, g_ref, dG_ref, dU_ref, wg_ref, wu_ref, dhin_ref, dg_ref, dn_acc):
        i = pl.program_id(0)
        jj = pl.program_id(1)

        @pl.when(jj == 0)
        def _():
            dn_acc[...] = jnp.zeros_like(dn_acc)

        @pl.when((i == 0) & (jj == 0))
        def _():
            dg_ref[...] = jnp.zeros_like(dg_ref)

        halves = [slice(0, tm // 2), slice(tm // 2, tm)]
        dn = [_dot_nt(dG_ref[0, rows, :], wg_ref[0]) for rows in halves]
        dn = [d + _dot_nt(dU_ref[0, rows, :], wu_ref[0]) for d, rows in zip(dn, halves)]
        for rows, d in zip(halves, dn):
            dn_acc[rows, :] += d

        @pl.when(jj == nh - 1)
        def _():
            dx, dg = _rms_bwd(h_ref[...], g_ref[...], dn_acc[...])
            dhin_ref[...] = dh_ref[...] + dx
            dg_ref[...] += dg

    row = lambda i, j: (i, 0)
    blk = lambda i, j: (j, i, 0)
    one = lambda i, j: (0, 0)
    return _pallas(
        body, rider, name=name, grid=(T // tm, nh),
        in_specs=[pl.BlockSpec((tm, D), row), pl.BlockSpec((tm, D), row), pl.BlockSpec((1, D), one),
                  pl.BlockSpec((1, tm, FB), blk), pl.BlockSpec((1, tm, FB), blk),
                  pl.BlockSpec((1, D, FB), lambda i, j: (j, 0, 0)),
                  pl.BlockSpec((1, D, FB), lambda i, j: (j + nh, 0, 0))],
        out_specs=[pl.BlockSpec((tm, D), row), pl.BlockSpec((1, D), one)],
        out_shape=[jax.ShapeDtypeStruct((T, D), F32), jax.ShapeDtypeStruct((1, D), F32)],
        scratch_shapes=[pltpu.VMEM((tm, D), F32)],
        compiler_params=_params("arbitrary", "arbitrary"),
    )(dh, h_in, gain, dG, dU, w_in, w_in)


def _matmul_tn(name, a, b, nj, a_block, a_map, b_block, b_map, out_shape, out_block, out_map, rider=None):
    T = a.shape[-2]
    tt = a_block[-2]
    nt = T // tt
    kb, nbk = out_block[-2], out_block[-1]

    def body(a_ref, b_ref, o_ref, acc):
        t = pl.program_id(1)

        @pl.when(t == 0)
        def _():
            acc[...] = jnp.zeros_like(acc)

        av = (a_ref[0] if len(a_block) == 3 else a_ref[...]).astype(BF16)
        bv = b_ref[0] if len(b_block) == 3 else b_ref[...]
        acc[...] += _dot_tn(av, bv)

        @pl.when(t == nt - 1)
        def _():
            if len(out_block) == 3:
                o_ref[0] = acc[...].astype(o_ref.dtype)
            else:
                o_ref[...] = acc[...].astype(o_ref.dtype)

    got = _pallas(
        body, rider, name=name, grid=(nj, nt),
        in_specs=[pl.BlockSpec(a_block, a_map), pl.BlockSpec(b_block, b_map)],
        out_specs=[pl.BlockSpec(out_block, out_map)],
        out_shape=[jax.ShapeDtypeStruct(out_shape, BF16)],
        scratch_shapes=[pltpu.VMEM((kb, nbk), F32)],
        compiler_params=_params("arbitrary", "arbitrary"),
    )(a, b)
    return got[0] if rider is None else got


def _dw_in(name, n, dG, dU, rider=None):
    T, kr = n.shape
    nh, _, FB = dG.shape
    tt = min(DW_IN_ROW_TILE, T)
    nt = T // tt
    cut = LANES * ((kr // LANES + 1) // 2)

    def body(n_ref, dg_ref, du_ref, o_ref, acc):
        j = pl.program_id(0)
        t = pl.program_id(1)

        @pl.when(t == 0)
        def _():
            acc[...] = jnp.zeros_like(acc)

        def add(dz_ref):
            for rows in ((slice(0, cut), slice(cut, kr)) if cut < kr else (slice(0, kr),)):
                acc[rows, :] += _dot_tn(n_ref[:, rows], dz_ref[0])

        @pl.when(j < nh)
        def _():
            add(dg_ref)

        @pl.when(j >= nh)
        def _():
            add(du_ref)

        @pl.when(t == nt - 1)
        def _():
            o_ref[0] = acc[...].astype(BF16)

    return _pallas(
        body, rider, name=name, grid=(2 * nh, nt),
        in_specs=[pl.BlockSpec((tt, kr), lambda j, t: (t, 0)),
                  pl.BlockSpec((1, tt, FB), lambda j, t: (jnp.minimum(j, nh - 1), t, 0)),
                  pl.BlockSpec((1, tt, FB), lambda j, t: (jnp.maximum(j - nh, 0), t, 0))],
        out_specs=[pl.BlockSpec((1, kr, FB), lambda j, t: (j, 0, 0))],
        out_shape=[jax.ShapeDtypeStruct((2 * nh, kr, FB), BF16)],
        scratch_shapes=[pltpu.VMEM((kr, FB), F32)],
        compiler_params=_params("arbitrary", "arbitrary"),
    )(n, dG, dU)


def _mix_in_fwd(h, gain, w):
    T, D = h.shape
    W = w.shape[1]
    nuv = 2 * GM_WIDTH
    tm = min(ROW_TILE, T)

    def body(h_ref, g_ref, w_ref, n_ref, zuv_ref, qkv_ref):
        _, xh = _rms_parts(h_ref[...])
        n = (xh * g_ref[...]).astype(BF16)
        n_ref[...] = n
        z = _dot(n, w_ref[...])
        zuv_ref[...] = z[:, :nuv]
        qkv_ref[...] = z[:, nuv:].astype(BF16)

    row = lambda i: (i, 0)
    return pl.pallas_call(
        body, name="mix_in_fwd", grid=(T // tm,),
        in_specs=[pl.BlockSpec((tm, D), row), pl.BlockSpec((1, D), lambda i: (0, 0)),
                  pl.BlockSpec((D, W), lambda i: (0, 0))],
        out_specs=[pl.BlockSpec((tm, D), row), pl.BlockSpec((tm, nuv), row),
                   pl.BlockSpec((tm, W - nuv), row)],
        out_shape=[jax.ShapeDtypeStruct((T, D), BF16), jax.ShapeDtypeStruct((T, nuv), F32),
                   jax.ShapeDtypeStruct((T, W - nuv), BF16)],
        compiler_params=_params("arbitrary"),
    )(h, gain, w)


def _mix_in_bwd(dzuv, dqkv, w, h, gain, dh):
    T, D = h.shape
    W = w.shape[1]
    nuv = dzuv.shape[1]
    tm = min(ROW_TILE, T)

    def body(dzuv_ref, dqkv_ref, w_ref, h_ref, g_ref, dh_ref, dhin_ref, dg_ref, half_ref):
        @pl.when(pl.program_id(0) == 0)
        def _():
            dg_ref[...] = jnp.zeros_like(dg_ref)

        dn = _dot_nt(dzuv_ref[...], w_ref[:, :nuv]) + _dot_nt(dqkv_ref[...], w_ref[:, nuv:])
        dx, dg = _rms_bwd(h_ref[...], g_ref[...], dn)
        dh_in = dh_ref[...] + dx
        dhin_ref[...] = dh_in
        half_ref[...] = (0.5 * dh_in).astype(BF16)
        dg_ref[...] += dg

    row = lambda i: (i, 0)
    one = lambda i: (0, 0)
    return pl.pallas_call(
        body, name="mix_in_bwd", grid=(T // tm,),
        in_specs=[pl.BlockSpec((tm, nuv), row), pl.BlockSpec((tm, W - nuv), row),
                  pl.BlockSpec((D, W), one), pl.BlockSpec((tm, D), row), pl.BlockSpec((1, D), one),
                  pl.BlockSpec((tm, D), row)],
        out_specs=[pl.BlockSpec((tm, D), row), pl.BlockSpec((1, D), one), pl.BlockSpec((tm, D), row)],
        out_shape=[jax.ShapeDtypeStruct((T, D), F32), jax.ShapeDtypeStruct((1, D), F32),
                   jax.ShapeDtypeStruct((T, D), BF16)],
        compiler_params=_params("arbitrary"),
    )(dzuv, dqkv, w, h, gain, dh)


def _gmlp_norm(zv, gv):
    v = _gelu(zv)
    r, vh = _rms_parts(v)
    return r, vh, (vh * gv).astype(BF16)


def _causal_ws(ws_ref, hd):
    r = lax.broadcasted_iota(jnp.int32, (CHUNK, CHUNK), 0)
    c = lax.broadcasted_iota(jnp.int32, (CHUNK, CHUNK), 1)
    return jnp.where(r >= c, ws_ref[hd], 0.0).astype(BF16)


def _gmlp_fwd(zuv, gv, ws, b_t):
    T = zuv.shape[0]
    tg = min(ROW_TILE, T)

    def body(zu_ref, zv_ref, gv_ref, ws_ref, bt_ref, o_ref):
        u = _gelu(zu_ref[...])
        _, _, vn = _gmlp_norm(zv_ref[...], gv_ref[...])
        for hd in range(GM_HEADS):
            wc = _causal_ws(ws_ref, hd)
            cols = slice(hd * CHUNK, (hd + 1) * CHUNK)
            for c in range(tg // CHUNK):
                rows = slice(c * CHUNK, (c + 1) * CHUNK)
                sv = _dot(wc, vn[rows, cols]) + bt_ref[:, hd:hd + 1]
                o_ref[rows, cols] = (u[rows, cols] * sv).astype(BF16)

    return pl.pallas_call(
        body, name="gmlp_fwd", grid=(T // tg,),
        in_specs=[pl.BlockSpec((tg, GM_WIDTH), lambda i: (i, 0)), pl.BlockSpec((tg, GM_WIDTH), lambda i: (i, 1)),
                  pl.BlockSpec((1, GM_WIDTH), lambda i: (0, 0)),
                  pl.BlockSpec((GM_HEADS, CHUNK, CHUNK), lambda i: (0, 0, 0)),
                  pl.BlockSpec((CHUNK, GM_HEADS), lambda i: (0, 0))],
        out_specs=pl.BlockSpec((tg, GM_WIDTH), lambda i: (i, 0)),
        out_shape=jax.ShapeDtypeStruct((T, GM_WIDTH), BF16),
        compiler_params=_params("arbitrary"),
    )(zuv, zuv, gv, ws, b_t)


def _gmlp_bwd(zuv, d_gm, gv, ws, b_t):
    T = zuv.shape[0]
    tg = min(ROW_TILE, T)
    ng = T // tg

    def body(zu_ref, zv_ref, dgm_ref, gv_ref, ws_ref, bt_ref, dz_ref, dgv_ref, dws_ref, dbt_ref, dsv_acc, dvn_s):
        i = pl.program_id(0)

        @pl.when(i == 0)
        def _():
            dgv_ref[...] = jnp.zeros_like(dgv_ref)
            dws_ref[...] = jnp.zeros_like(dws_ref)
            dsv_acc[...] = jnp.zeros_like(dsv_acc)

        zu = zu_ref[...]
        zv = zv_ref[...]
        dgm = dgm_ref[...]
        gvv = gv_ref[...]
        u = _gelu(zu)
        rv, vh, vn = _gmlp_norm(zv, gvv)
        dsv = dgm * u
        dsv_b = dsv.astype(BF16)
        for hd in range(GM_HEADS):
            wc = _causal_ws(ws_ref, hd)
            cols = slice(hd * CHUNK, (hd + 1) * CHUNK)
            dws = jnp.zeros((CHUNK, CHUNK), F32)
            dsv_sum = jnp.zeros((CHUNK, CHUNK), F32)
            for c in range(tg // CHUNK):
                rows = slice(c * CHUNK, (c + 1) * CHUNK)
                vch = vn[rows, cols]
                sv = _dot(wc, vch) + bt_ref[:, hd:hd + 1]
                dz_ref[rows, cols] = (dgm[rows, cols] * sv * _gelu_grad(zu[rows, cols])).astype(BF16)
                dws += _dot_nt(dsv_b[rows, cols], vch)
                dsv_sum += dsv[rows, cols]
                dvn_s[rows, cols] = _dot_tn(wc, dsv_b[rows, cols])
            dws_ref[hd] += dws
            dsv_acc[:, cols] += dsv_sum
        dvn = dvn_s[...]
        dvh = dvn * gvv
        dv = rv * (dvh - vh * jnp.mean(dvh * vh, axis=-1, keepdims=True))
        dgv_ref[...] += jnp.sum(dvn * vh, axis=0, keepdims=True)
        dz_ref[:, GM_WIDTH:] = (dv * _gelu_grad(zv)).astype(BF16)

        @pl.when(i == ng - 1)
        def _():
            r = lax.broadcasted_iota(jnp.int32, (CHUNK, CHUNK), 0)
            c = lax.broadcasted_iota(jnp.int32, (CHUNK, CHUNK), 1)
            for hd in range(GM_HEADS):
                dws_ref[hd] = jnp.where(r >= c, dws_ref[hd], 0.0)
                dbt_ref[:, hd:hd + 1] = jnp.sum(dsv_acc[:, hd * CHUNK:(hd + 1) * CHUNK], axis=1, keepdims=True)

    return pl.pallas_call(
        body, name="gmlp_bwd", grid=(ng,),
        in_specs=[pl.BlockSpec((tg, GM_WIDTH), lambda i: (i, 0)), pl.BlockSpec((tg, GM_WIDTH), lambda i: (i, 1)),
                  pl.BlockSpec((tg, GM_WIDTH), lambda i: (i, 0)),
                  pl.BlockSpec((1, GM_WIDTH), lambda i: (0, 0)),
                  pl.BlockSpec((GM_HEADS, CHUNK, CHUNK), lambda i: (0, 0, 0)),
                  pl.BlockSpec((CHUNK, GM_HEADS), lambda i: (0, 0))],
        out_specs=[pl.BlockSpec((tg, 2 * GM_WIDTH), lambda i: (i, 0)),
                   pl.BlockSpec((1, GM_WIDTH), lambda i: (0, 0)),
                   pl.BlockSpec((GM_HEADS, CHUNK, CHUNK), lambda i: (0, 0, 0)),
                   pl.BlockSpec((CHUNK, GM_HEADS), lambda i: (0, 0))],
        out_shape=[jax.ShapeDtypeStruct((T, 2 * GM_WIDTH), BF16), jax.ShapeDtypeStruct((1, GM_WIDTH), F32),
                   jax.ShapeDtypeStruct((GM_HEADS, CHUNK, CHUNK), F32),
                   jax.ShapeDtypeStruct((CHUNK, GM_HEADS), F32)],
        scratch_shapes=[pltpu.VMEM((CHUNK, GM_WIDTH), F32), pltpu.VMEM((tg, GM_WIDTH), F32)],
        compiler_params=_params("arbitrary"),
    )(zuv, zuv, d_gm, gv, ws, b_t)


def _scan_matrix(blk, keep):
    r = lax.broadcasted_iota(jnp.int32, (blk, blk), 0)
    c = lax.broadcasted_iota(jnp.int32, (blk, blk), 1)
    return jnp.where(keep(r, c), 1.0, 0.0).astype(BF16)


def _scan_matrix2(blk, keep, value):
    m = _scan_matrix(blk, keep) * value
    return jnp.concatenate([m, m], axis=0)


def _scan(x, mat2):
    hi, lo = _split_bf16(x)
    return _dot(jnp.concatenate([hi, lo], axis=1), mat2)


def _head_masks(q):
    lane = lax.broadcasted_iota(jnp.int32, q.shape, 1)
    m0 = lane < SB_HEAD_DIM
    zero = jnp.zeros_like(q)
    return m0, jnp.where(m0, q, zero), jnp.where(m0, zero, q)


_LOG2E = 1.4426950408889634


def _softplus_parts(z):
    e = jnp.exp2(jnp.abs(z) * (-_LOG2E))
    ope = 1.0 + e
    return e, ope, jnp.maximum(z, 0.0) + jnp.log(ope)


def _attn_fwd(qkv, rider=None):
    T = qkv.shape[0]
    tk = ATTN_KEY_BLOCK
    tq = min(ATTN_Q_ROWS, T)
    band = tq // tk
    assert band % ATTN_UNROLL == 0 or T == tq
    ngrp = SB_WIDTH // LANES

    def body(q_ref, k_ref, v_ref, o_ref, l_ref, acc, run):
        i = pl.program_id(1)
        suffix = _scan_matrix2(tk, lambda r, c: r >= c, -1.0)
        row = lax.broadcasted_iota(jnp.int32, (tq, tk), 0)
        col = lax.broadcasted_iota(jnp.int32, (tq, tk), 1)
        m0, q0, q1 = _head_masks(q_ref[...] * SB_SCALE)
        acc[...] = jnp.zeros_like(acc)
        run[...] = jnp.zeros_like(run)

        def tiles(work):
            heads = (q0, q1)
            kv = []
            for j, _ in work:
                start = pl.multiple_of(j * tk, tk)
                kv.append((k_ref[pl.ds(start, tk), :], v_ref[pl.ds(start, tk), :]))
            z = [[_dot_nt(qh, kj) for qh in heads] for kj, _ in kv]
            sp = [[_softplus_parts(zz)[2] for zz in zt] for zt in z]
            sp = [[s if m is None else jnp.where(m, s, 0.0) for s in st] for st, (_, m) in zip(sp, work)]
            res = [[_scan(s, suffix) for s in st] for st in sp]
            runs = [run[hd] for hd in range(len(heads))]
            a = []
            for t, (_, m) in enumerate(work):
                at = []
                for hd in range(len(heads)):
                    av = jnp.exp(z[t][hd] + (runs[hd] + res[t][hd]))
                    at.append(av if m is None else jnp.where(m, av, 0.0))
                    runs[hd] = runs[hd] + res[t][hd][:, 0:1]
                a.append(at)
            for hd in range(len(heads)):
                run[hd] = runs[hd]
                upd = _dot(a[0][hd].astype(BF16), kv[0][1])
                for t in range(1, len(work)):
                    upd = upd + _dot(a[t][hd].astype(BF16), kv[t][1])
                acc[hd] += upd

        tiles([(i * band + jb, jb * tk + col < row) for jb in reversed(range(band))])

        def full_step(it, carry):
            tiles([(i * band - 1 - ATTN_UNROLL * it - u, None) for u in range(ATTN_UNROLL)])
            return carry

        lax.fori_loop(0, i * (band // ATTN_UNROLL), full_step, 0)
        o_ref[...] = jnp.where(m0, acc[0], acc[1]).astype(BF16)
        l_ref[...] = jnp.where(m0, jnp.broadcast_to(run[0], (tq, LANES)), jnp.broadcast_to(run[1], (tq, LANES)))

    return _pallas(
        body, rider, name="attn_fwd", grid=(ngrp, T // tq),
        in_specs=[pl.BlockSpec((tq, LANES), lambda g, i: (i, g)),
                  pl.BlockSpec((T, LANES), lambda g, i: (0, ngrp + g)),
                  pl.BlockSpec((T, LANES), lambda g, i: (0, 2 * ngrp + g))],
        out_specs=[pl.BlockSpec((tq, LANES), lambda g, i: (i, g)),
                   pl.BlockSpec((tq, LANES), lambda g, i: (i, g))],
        out_shape=[jax.ShapeDtypeStruct((T, SB_WIDTH), BF16), jax.ShapeDtypeStruct((T, SB_WIDTH), F32)],
        scratch_shapes=[pltpu.VMEM((2, tq, LANES), F32), pltpu.VMEM((2, tq, 1), F32)],
        compiler_params=_params("arbitrary", "arbitrary"),
    )(qkv, qkv, qkv)


def _attn_bwd(qkv, d_o, ltot, rider=None):
    T = qkv.shape[0]
    tk = ATTN_KEY_BLOCK
    tq = min(ATTN_BWD_Q_ROWS, T)
    band = tq // tk
    nq = T // tq
    ngrp = SB_WIDTH // LANES

    def body(q_ref, k_ref, v_ref, do_ref, l_ref, dq_ref, dk_ref, dv_ref, dq_acc, dk_acc, dv_acc, lpre, ppre):
        i = pl.program_id(1)

        @pl.when(i == 0)
        def _():
            dk_acc[...] = jnp.zeros_like(dk_acc)
            dv_acc[...] = jnp.zeros_like(dv_acc)

        excl = _scan_matrix(tk, lambda r, c: r < c)
        excl2 = jnp.concatenate([excl, excl], axis=0)
        row = lax.broadcasted_iota(jnp.int32, (tq, tk), 0)
        col = lax.broadcasted_iota(jnp.int32, (tq, tk), 1)
        m0, q0, q1 = _head_masks(q_ref[...] * SB_SCALE)
        _, d0, d1 = _head_masks(do_ref[...].astype(BF16))
        lt = l_ref[...]
        ltots = (lt[:, 0:1], lt[:, SB_HEAD_DIM:SB_HEAD_DIM + 1])
        dq_acc[...] = jnp.zeros_like(dq_acc)
        lpre[...] = jnp.zeros_like(lpre)
        ppre[...] = jnp.zeros_like(ppre)

        def tiles(work):
            heads = ((q0, d0), (q1, d1))
            nhd = len(heads)
            starts = [pl.multiple_of(j * tk, tk) for j, _ in work]
            kv = [(k_ref[pl.ds(st, tk), :], v_ref[pl.ds(st, tk), :]) for st in starts]
            masks = [m for _, m in work]
            every = [(t, hd) for t in range(len(work)) for hd in range(nhd)]
            z = {(t, hd): _dot_nt(heads[hd][0], kv[t][0]) for t, hd in every}
            da = {(t, hd): _dot_nt(heads[hd][1], kv[t][1]) for t, hd in every}
            sp, beta = {}, {}
            for key in every:
                s = _softplus_parts(z[key])[2]
                beta[key] = jnp.exp(z[key] - s)
                sp[key] = s if masks[key[0]] is None else jnp.where(masks[key[0]], s, 0.0)
            res = {key: _scan(sp[key], excl2) for key in every}
            lp = [lpre[hd] for hd in range(nhd)]
            a, p = {}, {}
            for t, hd in every:
                av = jnp.exp(z[t, hd] + ((ltots[hd] + lp[hd]) + res[t, hd]))
                a[t, hd] = av if masks[t] is None else jnp.where(masks[t], av, 0.0)
                p[t, hd] = a[t, hd] * da[t, hd]
                lp[hd] = lp[hd] + (res[t, hd][:, tk - 1:tk] + sp[t, hd][:, tk - 1:tk])
            resp = {key: _dot(p[key].astype(BF16), excl) for key in every}
            pp = [ppre[hd] for hd in range(nhd)]
            dzb = {}
            for t, hd in every:
                dz = p[t, hd] - beta[t, hd] * (p[t, hd] + (pp[hd] + resp[t, hd]))
                if masks[t] is not None:
                    dz = jnp.where(masks[t], dz, 0.0)
                dzb[t, hd] = dz.astype(BF16)
                pp[hd] = pp[hd] + (resp[t, hd][:, tk - 1:tk] + p[t, hd][:, tk - 1:tk])
            for hd in range(nhd):
                lpre[hd] = lp[hd]
                ppre[hd] = pp[hd]
                upd = _dot(dzb[0, hd], kv[0][0])
                for t in range(1, len(work)):
                    upd = upd + _dot(dzb[t, hd], kv[t][0])
                dq_acc[hd] += upd
            for t, st in enumerate(starts):
                dk = _dot_tn(dzb[t, 0], heads[0][0])
                dv = _dot_tn(a[t, 0].astype(BF16), heads[0][1])
                for hd in range(1, nhd):
                    dk = dk + _dot_tn(dzb[t, hd], heads[hd][0])
                    dv = dv + _dot_tn(a[t, hd].astype(BF16), heads[hd][1])
                dk_acc[pl.ds(st, tk), :] += dk
                dv_acc[pl.ds(st, tk), :] += dv

        def full_step(j, carry):
            tiles([(j, None)])
            return carry

        lax.fori_loop(0, i * band, full_step, 0)
        for jb in range(band):
            tiles([(i * band + jb, jb * tk + col < row)])
        dq_ref[...] = (jnp.where(m0, dq_acc[0], dq_acc[1]) * SB_SCALE).astype(BF16)

        @pl.when(i == nq - 1)
        def _():
            dk_ref[...] = dk_acc[...].astype(BF16)
            dv_ref[...] = dv_acc[...].astype(BF16)

    qmap = lambda g, i: (i, g)
    return _pallas(
        body, rider, name="attn_bwd", grid=(ngrp, nq),
        in_specs=[pl.BlockSpec((tq, LANES), qmap),
                  pl.BlockSpec((T, LANES), lambda g, i: (0, ngrp + g)),
                  pl.BlockSpec((T, LANES), lambda g, i: (0, 2 * ngrp + g)),
                  pl.BlockSpec((tq, LANES), qmap), pl.BlockSpec((tq, LANES), qmap)],
        out_specs=[pl.BlockSpec((tq, LANES), qmap),
                   pl.BlockSpec((T, LANES), lambda g, i: (0, g)),
                   pl.BlockSpec((T, LANES), lambda g, i: (0, g))],
        out_shape=[jax.ShapeDtypeStruct((T, SB_WIDTH), BF16)] * 3,
        scratch_shapes=[pltpu.VMEM((2, tq, LANES), F32), pltpu.VMEM((T, LANES), F32),
                        pltpu.VMEM((T, LANES), F32), pltpu.VMEM((2, tq, 1), F32),
                        pltpu.VMEM((2, tq, 1), F32)],
        compiler_params=_params("arbitrary", "arbitrary"),
    )(qkv, qkv, qkv, d_o, ltot)


def _mix_out_fwd(h, gm, sb, w):
    T, D = h.shape
    tm = min(ROW_TILE, T)

    def body(h_ref, gm_ref, sb_ref, w_ref, o_ref):
        o_ref[...] = h_ref[...] + _dot(gm_ref[...], w_ref[:GM_WIDTH, :]) + _dot(sb_ref[...], w_ref[GM_WIDTH:, :])

    row = lambda i: (i, 0)
    return pl.pallas_call(
        body, name="mix_out_fwd", grid=(T // tm,),
        in_specs=[pl.BlockSpec((tm, D), row), pl.BlockSpec((tm, GM_WIDTH), row), pl.BlockSpec((tm, SB_WIDTH), row),
                  pl.BlockSpec((GM_WIDTH + SB_WIDTH, D), lambda i: (0, 0))],
        out_specs=pl.BlockSpec((tm, D), row),
        out_shape=jax.ShapeDtypeStruct((T, D), F32),
        compiler_params=_params("arbitrary"),
    )(h, gm, sb, w)


def _mix_out_bwd(dh, w):
    T, D = dh.shape
    tm = min(ROW_TILE, T)

    def body(dh_ref, w_ref, dgm_ref, dsb_ref, dhb_ref):
        dhb = dh_ref[...].astype(BF16)
        dhb_ref[...] = dhb
        dgm_ref[...] = _dot_nt(dhb, w_ref[:GM_WIDTH, :])
        dsb_ref[...] = _dot_nt(dhb, w_ref[GM_WIDTH:, :])

    row = lambda i: (i, 0)
    return pl.pallas_call(
        body, name="mix_out_bwd", grid=(T // tm,),
        in_specs=[pl.BlockSpec((tm, D), row), pl.BlockSpec((GM_WIDTH + SB_WIDTH, D), lambda i: (0, 0))],
        out_specs=[pl.BlockSpec((tm, GM_WIDTH), row), pl.BlockSpec((tm, SB_WIDTH), row), pl.BlockSpec((tm, D), row)],
        out_shape=[jax.ShapeDtypeStruct((T, GM_WIDTH), F32), jax.ShapeDtypeStruct((T, SB_WIDTH), F32),
                   jax.ShapeDtypeStruct((T, D), BF16)],
        compiler_params=_params("arbitrary"),
    )(dh, w)


def _tail(h3, p, target, g_ple, g_fin, w_gate, w_proj):
    T, D = h3.shape
    PD = p.shape[1]
    tm = min(ROW_TILE, T)

    def body(h_ref, p_ref, t_ref, gp_ref, gf_ref, wg_ref, wp_ref,
             loss_ref, dh_ref, n4_ref, dgl_ref, dpp_ref, dgp_ref, dgf_ref):
        @pl.when(pl.program_id(0) == 0)
        def _():
            loss_ref[...] = jnp.zeros_like(loss_ref)
            dgp_ref[...] = jnp.zeros_like(dgp_ref)
            dgf_ref[...] = jnp.zeros_like(dgf_ref)

        h3v = h_ref[...]
        gp = gp_ref[...]
        gf = gf_ref[...]
        r3, xh3 = _rms_parts(h3v)
        n4 = (xh3 * gp).astype(BF16)
        n4_ref[...] = n4
        gate = _sigmoid(_dot(n4, wg_ref[...]))
        pp = _dot(p_ref[...].astype(BF16), wp_ref[...])
        h4 = h3v + gate * pp
        r4, xh4 = _rms_parts(h4)
        err = xh4 * gf - t_ref[...]
        loss_ref[...] += jnp.full(loss_ref.shape, (0.5 / D) * jnp.sum(err * err), F32)
        dy = err * (1.0 / D)
        dgf_ref[...] += jnp.sum(dy * xh4, axis=0, keepdims=True)
        dyg = dy * gf
        dh4 = r4 * (dyg - xh4 * jnp.mean(dyg * xh4, axis=-1, keepdims=True))
        dpp_ref[...] = (dh4 * gate).astype(BF16)
        dgl = (dh4 * pp * gate * (1.0 - gate)).astype(BF16)
        dgl_ref[...] = dgl
        dn4 = _dot_nt(dgl, wg_ref[...])
        dgp_ref[...] += jnp.sum(dn4 * xh3, axis=0, keepdims=True)
        dn4g = dn4 * gp
        dh_ref[...] = dh4 + r3 * (dn4g - xh3 * jnp.mean(dn4g * xh3, axis=-1, keepdims=True))

    row = lambda i: (i, 0)
    one = lambda i: (0, 0)
    return pl.pallas_call(
        body, name="tail", grid=(T // tm,),
        in_specs=[pl.BlockSpec((tm, D), row), pl.BlockSpec((tm, PD), row), pl.BlockSpec((tm, D), row),
                  pl.BlockSpec((1, D), one), pl.BlockSpec((1, D), one),
                  pl.BlockSpec((D, D), one), pl.BlockSpec((PD, D), one)],
        out_specs=[pl.BlockSpec((1, LANES), one), pl.BlockSpec((tm, D), row), pl.BlockSpec((tm, D), row),
                   pl.BlockSpec((tm, D), row), pl.BlockSpec((tm, D), row),
                   pl.BlockSpec((1, D), one), pl.BlockSpec((1, D), one)],
        out_shape=[jax.ShapeDtypeStruct((1, LANES), F32), jax.ShapeDtypeStruct((T, D), F32),
                   jax.ShapeDtypeStruct((T, D), BF16), jax.ShapeDtypeStruct((T, D), BF16),
                   jax.ShapeDtypeStruct((T, D), BF16),
                   jax.ShapeDtypeStruct((1, D), F32), jax.ShapeDtypeStruct((1, D), F32)],
        compiler_params=_params("arbitrary"),
    )(h3, p, target, g_ple, g_fin, w_gate, w_proj)


FFN1_W = ("ffn1_w_in", "ffn1_w_out")
MIX_W = ("w_mix_in", "w_mix_out")
REST_W = ("ffn2_w_in", "ffn2_w_out", "ple_w_gate", "ple_w_proj")
BIG_W = FFN1_W + MIX_W + REST_W
COLUMN_SHARDED = ("w_mix_in", "ple_w_proj")


class _Traffic:
    def __init__(self, shards):
        self.shards = shards
        self.parts = {}
        self.blocks, self.sums, self.near = {}, {}, {}

    @staticmethod
    def _full(name, gathered):
        if name in COLUMN_SHARDED:
            return jnp.transpose(gathered, (1, 0, 2)).reshape(gathered.shape[1], -1)
        if name.endswith("_w_in"):
            return gathered
        return gathered.reshape(-1, gathered.shape[-1])

    @staticmethod
    def _blocks(name, grad):
        name = name.split("/")[0]
        if name in COLUMN_SHARDED:
            return jnp.transpose(grad.reshape(grad.shape[0], N_DEV, -1), (1, 0, 2))
        if name.endswith("_w_in"):
            return grad
        return grad.reshape(N_DEV, -1, grad.shape[-1])

    def gather_now(self, names):
        got = _exchange("gather_" + names[0], [self.shards[n] for n in names], [GATHER] * len(names))
        return self.gathered(names, got)

    def gather_rider(self, names):
        return [self.shards[n] for n in names], [GATHER] * len(names)

    def gathered(self, names, got):
        return {n: self._full(n, g) for n, g in zip(names, got)}

    def scatter_rider(self, grads, gather=()):
        return ([self._blocks(n, g) for n, g in grads.items()] + list(gather),
                [SCATTER] * len(grads) + [GATHER] * len(gather))

    def scattered(self, names, got):
        self.parts.update(zip(names, got))
        return got[len(names):]


    def pair_rider(self, name, grad):
        self.blocks[name] = self._blocks(name, grad)
        return [self.blocks[name]], [PAIR]

    def paired(self, name, got):
        blocks, me = self.blocks[name], _my_index()
        mine = jnp.stack([lax.dynamic_index_in_dim(blocks, jnp.bitwise_xor(me, 2 * c), 0, keepdims=False)
                          for c in range(1, N_CHIPS_AWAY + 1)])
        self.sums[name] = _add_bf16("pair_sum_" + name.replace("/", "_"), mine, got[:N_CHIPS_AWAY])
        self.near[name] = jnp.stack([lax.dynamic_index_in_dim(blocks, me, 0, keepdims=False), got[N_CHIPS_AWAY]])

    def chip_rider(self, name):
        return [self.sums[name]], [CHIP]

    def chipped(self, name, got):
        self.parts[name] = jnp.concatenate([self.near[name], got], axis=0)


def _local_step(traffic, x, p, target, g1, gmix, gv, ws, b_t, g2, gple, gfin, pack_small):
    T, D = x.shape
    tm = min(ROW_TILE, T)

    w = traffic.gather_now(FFN1_W)
    h1, n1, G1, U1, a1, *got = _ffn_fwd("ffn1_fwd", x, g1, w["ffn1_w_in"], w["ffn1_w_out"],
                                        rider=traffic.gather_rider(MIX_W))
    w.update(traffic.gathered(MIX_W, got))
    n2, zuv, qkv = _mix_in_fwd(h1, gmix, w["w_mix_in"])
    gm = _gmlp_fwd(zuv, gv, ws, b_t)
    sb, ltot, *got = _attn_fwd(qkv, rider=traffic.gather_rider(REST_W))
    w.update(traffic.gathered(REST_W, got))
    h2 = _mix_out_fwd(h1, gm, sb, w["w_mix_out"])
    h3, n3, G2, U2, a2 = _ffn_fwd("ffn2_fwd", h2, g2, w["ffn2_w_in"], w["ffn2_w_out"])
    loss, dh3, n4, d_gl, d_pp, dg_ple, dg_fin = _tail(h3, p, target, gple, gfin, w["ple_w_gate"], w["ple_w_proj"])

    nb, _, FB = w["ffn1_w_in"].shape
    nh = nb // 2

    tt = min(GRAD_ROW_TILE, T)

    def dw_out(name, a, d_out):
        return _matmul_tn(name, a, d_out, nh, (1, tt, FB), lambda j, t: (j, t, 0), (tt, D), lambda j, t: (t, 0),
                          (nh, FB, D), (1, FB, D), lambda j, t: (j, 0, 0))

    def dense_tn(name, a, b, ncol):
        ka, nbw = a.shape[1], b.shape[1] // ncol
        return _matmul_tn(name, a, b, ncol, (tt, ka), lambda j, t: (t, 0), (tt, nbw), lambda j, t: (t, j),
                          (ka, b.shape[1]), (ka, nbw), lambda j, t: (0, j))

    grads = dict(ple_w_gate=dense_tn("dw_ple_gate", n4, d_gl, 2), ple_w_proj=dense_tn("dw_ple_proj", p, d_pp, 1))
    dh2, dg2, dG2, dU2, dout2 = _ffn_bwd("ffn2_bwd", dh3, h2, g2, G2, U2, w["ffn2_w_in"], w["ffn2_w_out"])
    grads["ffn2_w_in"], = _dw_in("ffn2_dw_in", n3, dG2, dU2)
    grads["ffn2_w_out"] = dw_out("ffn2_dw_out", a2, dout2)
    grads = {n: grads[n] for n in REST_W}

    d_gm, d_sb, dh2_bf = _mix_out_bwd(dh2, w["w_mix_out"])
    grads["w_mix_out"] = jnp.concatenate([dense_tn("dw_mix_out_gm", gm, dh2_bf, 1),
                                          dense_tn("dw_mix_out_sb", sb, dh2_bf, 1)], axis=0)
    dzuv, dgv, dws, db_t = _gmlp_bwd(zuv, d_gm, gv, ws, b_t)
    dq, dk, dv, *got = _attn_bwd(qkv, d_sb, ltot, rider=traffic.scatter_rider(grads))
    traffic.scattered(list(grads), got)
    dqkv = jnp.concatenate([dq, dk, dv], axis=1)
    dw_mi = jnp.concatenate([dense_tn("dw_mix_in_uv", n2, dzuv, 2), dense_tn("dw_mix_in_qkv", n2, dqkv, 3)], axis=1)
    dh1, dgmix, dout1 = _mix_in_bwd(dzuv, dqkv, w["w_mix_in"], h1, gmix, dh2)

    def dw_out_riding(name, a, d_out, rider):
        return _matmul_tn(name, a, d_out, nh, (1, tt, FB), lambda j, t: (j, t, 0), (tt, D), lambda j, t: (t, 0),
                          (nh, FB, D), (1, FB, D), lambda j, t: (j, 0, 0), rider=rider)

    def both(*riders):
        return [x for r in riders for x in r[0]], [k for r in riders for k in r[1]]

    dw_out1, got = dw_out_riding("ffn1_dw_out", a1, dout1, traffic.pair_rider("w_mix_in", dw_mi))
    traffic.paired("w_mix_in", got)
    dG1, dU1, got_mi, got = _ffn_bwd_gates("ffn1_bwd_gates", dout1, G1, U1, w["ffn1_w_out"],
                                           rider=both(traffic.chip_rider("w_mix_in"),
                                                      traffic.pair_rider("ffn1_w_out", dw_out1)))
    traffic.chipped("w_mix_in", got_mi)
    traffic.paired("ffn1_w_out", got)
    half = D // 2
    top, got = _dw_in("ffn1_dw_in_top", n1[:, :half], dG1, dU1, rider=traffic.chip_rider("ffn1_w_out"))
    traffic.chipped("ffn1_w_out", got)
    bottom, got = _dw_in("ffn1_dw_in_bottom", n1[:, half:], dG1, dU1, rider=traffic.pair_rider("ffn1_w_in/0", top))
    traffic.paired("ffn1_w_in/0", got)
    dx, dg1, got_top, got = _ffn_bwd_input("ffn1_bwd_input", dh1, x, g1, dG1, dU1, w["ffn1_w_in"],
                                           rider=both(traffic.chip_rider("ffn1_w_in/0"),
                                                      traffic.pair_rider("ffn1_w_in/1", bottom)))
    traffic.chipped("ffn1_w_in/0", got_top)
    traffic.paired("ffn1_w_in/1", got)

    small = pack_small(dict(ffn1_norm=dg1, mix_norm=dgmix, gmlp_v_norm=dgv, gmlp_w_s=dws, gmlp_b=jnp.transpose(db_t),
                            ffn2_norm=dg2, ple_norm=dg_ple, final_norm=dg_fin), loss)
    return dx, small, traffic.chip_rider("ffn1_w_in/1")


def _my_index():
    return 4 * lax.axis_index("x") + 2 * lax.axis_index("y") + lax.axis_index("c")


def _add_bf16(name, a, b):
    n, R, C = a.shape

    def body(a_ref, b_ref, o_ref):
        o_ref[...] = (a_ref[...].astype(F32) + b_ref[...].astype(F32)).astype(BF16)

    spec = pl.BlockSpec((1, R, C), lambda i: (i, 0, 0))
    return pl.pallas_call(body, name=name, grid=(n,), in_specs=[spec, spec], out_specs=spec,
                          out_shape=jax.ShapeDtypeStruct((n, R, C), BF16), compiler_params=_params("arbitrary"))(a, b)


def _peer(d):
    x, y, c = lax.axis_index("x"), lax.axis_index("y"), lax.axis_index("c")
    px = 1 - x if d & 4 else x
    py = 1 - y if d & 2 else y
    pc = 1 - c if d & 1 else c
    return (px, py, pc), 4 * px + 2 * py + pc


N_CHIPS_AWAY = 3


GATHER, SCATTER, PAIR, CHIP = "gather", "scatter", "pair", "chip"


class _ExchangePlan:
    def __init__(self, ins, outs, send, recv, local, kinds):
        self.ins, self.outs, self.send, self.recv, self.local, self.kinds = ins, outs, send, recv, local, kinds
        self.scatter = [k == SCATTER for k in kinds]
        self.me = _peer(0)[1]

    def _remote(self, t, sem, src, slot, peer):
        return pltpu.make_async_remote_copy(
            src_ref=src, dst_ref=self.outs[t].at[slot], send_sem=self.send.at[t, sem], recv_sem=self.recv.at[t, sem],
            device_id=peer, device_id_type=MESH)

    def _own(self, t):
        src = self.ins[t].at[self.me] if self.scatter[t] else self.ins[t]
        return pltpu.make_async_copy(src, self.outs[t].at[self.me], self.local.at[t])

    def _n_direct(self, t):
        return {SCATTER: N_DEV - 1, GATHER: N_CHIPS_AWAY + 1, PAIR: N_CHIPS_AWAY + 1, CHIP: N_CHIPS_AWAY}[self.kinds[t]]

    def _direct(self, t, k):
        kind = self.kinds[t]
        if kind == SCATTER:
            peer, slot = _peer(k + 1)
            return self._remote(t, k, self.ins[t].at[slot], self.me, peer)
        if kind == GATHER:
            return self._remote(t, k, self.ins[t], self.me, _peer(2 * k if k else 1)[0])
        if kind == PAIR:
            block = _peer(2 * (k + 1) + 1 if k < N_CHIPS_AWAY else 1)[1]
            return self._remote(t, k, self.ins[t].at[block], k, _peer(1)[0])
        return self._remote(t, k, self.ins[t].at[k], k, _peer(2 * (k + 1))[0])

    def _has_own(self, t):
        return self.kinds[t] in (SCATTER, GATHER)

    def _relay(self, t, c):
        slot = _peer(2 * c)[1]
        return self._remote(t, N_CHIPS_AWAY + c, self.outs[t].at[slot], slot, _peer(1)[0])

    def start(self):
        for t in range(len(self.ins)):
            if self._has_own(t):
                self._own(t).start()
            for k in range(self._n_direct(t)):
                self._direct(t, k).start()

    def relay(self):
        for t in self._gathers():
            for c in range(1, N_CHIPS_AWAY + 1):
                self._direct(t, c).wait_recv()
                self._relay(t, c).start()

    def _gathers(self):
        return [t for t in range(len(self.ins)) if self.kinds[t] == GATHER]

    def finish(self):
        for t in range(len(self.ins)):
            if self._has_own(t):
                self._own(t).wait()
            for k in range(self._n_direct(t)):
                self._direct(t, k).wait_send()
                if self.kinds[t] != GATHER or k == 0:
                    self._direct(t, k).wait_recv()
        for t in self._gathers():
            for c in range(1, N_CHIPS_AWAY + 1):
                self._relay(t, c).wait()


def _exchange_shapes(arrays, kinds):
    shape = {GATHER: lambda a: (N_DEV,) + a.shape, SCATTER: lambda a: a.shape, CHIP: lambda a: a.shape,
             PAIR: lambda a: (N_CHIPS_AWAY + 1,) + a.shape[1:]}
    return [jax.ShapeDtypeStruct(shape[k](a), a.dtype) for a, k in zip(arrays, kinds)]


def _exchange_sems(n):
    return [pltpu.SemaphoreType.DMA((n, N_DEV - 1)), pltpu.SemaphoreType.DMA((n, N_DEV - 1)),
            pltpu.SemaphoreType.DMA((n,))]


_ANY = pl.BlockSpec(memory_space=pl.ANY)


def _exchange(name, arrays, scatter):
    n = len(arrays)

    def body(*refs):
        plan = _ExchangePlan(refs[:n], refs[n:2 * n], *refs[2 * n:], scatter)
        plan.start()
        plan.relay()
        plan.finish()

    return pl.pallas_call(
        body, name=name, in_specs=[_ANY] * n, out_specs=[_ANY] * n, out_shape=_exchange_shapes(arrays, scatter),
        scratch_shapes=_exchange_sems(n),
    )(*arrays)


def _pallas(body, rider, *, name, grid, in_specs, out_specs, out_shape, scratch_shapes=(), compiler_params=None):
    if rider is None:
        return pl.pallas_call(body, name=name, grid=grid, in_specs=in_specs, out_specs=out_specs, out_shape=out_shape,
                              scratch_shapes=list(scratch_shapes), compiler_params=compiler_params)
    arrays, scatter = rider
    n, ni, no, ns = len(arrays), len(in_specs), len(out_specs), len(scratch_shapes)

    def carried(*refs):
        ins, r_in = refs[:ni], refs[ni:ni + n]
        outs, r_out = refs[ni + n:ni + n + no], refs[ni + n + no:ni + 2 * n + no]
        scratch, sems = refs[ni + 2 * n + no:ni + 2 * n + no + ns], refs[ni + 2 * n + no + ns:]
        step = 0
        for ax, g in enumerate(grid):
            step = step * g + pl.program_id(ax)
        steps = functools.reduce(lambda a, b: a * b, grid)

        @pl.when(step == 0)
        def _():
            _ExchangePlan(r_in, r_out, *sems, scatter).start()

        @pl.when(step == steps // 2)
        def _():
            _ExchangePlan(r_in, r_out, *sems, scatter).relay()

        body(*ins, *outs, *scratch)

        @pl.when(step == steps - 1)
        def _():
            _ExchangePlan(r_in, r_out, *sems, scatter).finish()

    call = pl.pallas_call(
        carried, name=name, grid=grid, in_specs=list(in_specs) + [_ANY] * n, out_specs=list(out_specs) + [_ANY] * n,
        out_shape=list(out_shape) + _exchange_shapes(arrays, scatter),
        scratch_shapes=list(scratch_shapes) + _exchange_sems(n), compiler_params=compiler_params)
    return lambda *args: call(*args, *arrays)


def _adamw_math(g, w, m, v):
    m_new = ADAM_B1 * m + (1.0 - ADAM_B1) * g
    v_new = ADAM_B2 * v + (1.0 - ADAM_B2) * (g * g)
    m_hat = m_new / (1.0 - ADAM_B1 ** ADAM_STEP)
    v_hat = v_new / (1.0 - ADAM_B2 ** ADAM_STEP)
    delta = -ADAM_LR * (m_hat / (jnp.sqrt(v_hat) + ADAM_EPS) + ADAM_WD * w)
    return delta, m_new, v_new


def _adamw(name, parts, w, m, v, rider=None):
    R, C = w.shape
    slots = parts.shape[0]
    tr = R
    for cand in (256, 128, 64, 32, 16, 8):
        if R % cand == 0:
            tr = cand
            break

    def body(p_ref, w_ref, m_ref, v_ref, g_ref, d_ref, nm_ref, nv_ref):
        g = p_ref[0].astype(F32)
        for j in range(1, slots):
            g = g + p_ref[j].astype(F32)
        g_ref[...] = g
        d_ref[...], nm_ref[...], nv_ref[...] = _adamw_math(g, w_ref[...], m_ref[...], v_ref[...])

    row = lambda i: (i, 0)
    spec = pl.BlockSpec((tr, C), row)
    return _pallas(
        body, rider, name=name, grid=(R // tr,),
        in_specs=[pl.BlockSpec((slots, tr, C), lambda i: (0, i, 0)), spec, spec, spec],
        out_specs=[spec] * 4,
        out_shape=[jax.ShapeDtypeStruct((R, C), F32)] * 4,
        compiler_params=_params("arbitrary"),
    )(parts, w, m, v)


def _rows128(a):
    flat = a.reshape(-1, LANES)
    pad = (-flat.shape[0]) % SMALL_ROWS_ALIGN
    return jnp.pad(flat, ((0, pad), (0, 0))) if pad else flat


def _unrows(packed, like):
    n = like.size // LANES
    return packed[:n].reshape(like.shape)


def kernel(x, p, ffn1_norm, ffn1_w_in, ffn1_w_out, mix_norm, w_mix_in, gmlp_v_norm, gmlp_w_s, gmlp_b, w_mix_out, ffn2_norm, ffn2_w_in, ffn2_w_out, ple_norm, ple_w_gate, ple_w_proj, final_norm, loss_target, m_ffn1_norm, m_ffn1_w_in, m_ffn1_w_out, m_mix_norm, m_w_mix_in, m_gmlp_v_norm, m_gmlp_w_s, m_gmlp_b, m_w_mix_out, m_ffn2_norm, m_ffn2_w_in, m_ffn2_w_out, m_ple_norm, m_ple_w_gate, m_ple_w_proj, m_final_norm, v_ffn1_norm, v_ffn1_w_in, v_ffn1_w_out, v_mix_norm, v_w_mix_in, v_gmlp_v_norm, v_gmlp_w_s, v_gmlp_b, v_w_mix_out, v_ffn2_norm, v_ffn2_w_in, v_ffn2_w_out, v_ple_norm, v_ple_w_gate, v_ple_w_proj, v_final_norm):
    names = ["ffn1_norm", "ffn1_w_in", "ffn1_w_out", "mix_norm", "w_mix_in", "gmlp_v_norm", "gmlp_w_s", "gmlp_b",
             "w_mix_out", "ffn2_norm", "ffn2_w_in", "ffn2_w_out", "ple_norm", "ple_w_gate", "ple_w_proj", "final_norm"]
    W = dict(zip(names, [ffn1_norm, ffn1_w_in, ffn1_w_out, mix_norm, w_mix_in, gmlp_v_norm, gmlp_w_s, gmlp_b,
                         w_mix_out, ffn2_norm, ffn2_w_in, ffn2_w_out, ple_norm, ple_w_gate, ple_w_proj, final_norm]))
    M = dict(zip(names, [m_ffn1_norm, m_ffn1_w_in, m_ffn1_w_out, m_mix_norm, m_w_mix_in, m_gmlp_v_norm, m_gmlp_w_s,
                         m_gmlp_b, m_w_mix_out, m_ffn2_norm, m_ffn2_w_in, m_ffn2_w_out, m_ple_norm, m_ple_w_gate,
                         m_ple_w_proj, m_final_norm]))
    V = dict(zip(names, [v_ffn1_norm, v_ffn1_w_in, v_ffn1_w_out, v_mix_norm, v_w_mix_in, v_gmlp_v_norm, v_gmlp_w_s,
                         v_gmlp_b, v_w_mix_out, v_ffn2_norm, v_ffn2_w_in, v_ffn2_w_out, v_ple_norm, v_ple_w_gate,
                         v_ple_w_proj, v_final_norm]))
    small = [n for n in names if n not in BIG_W]
    D = x.shape[-1]

    def pack(src, last):
        return jnp.concatenate([_rows128(src[n]) for n in small] + [last], axis=0)

    offs = [0]
    for n in small:
        offs.append(offs[-1] + _rows128(W[n]).shape[0])

    traffic = _Traffic({n: W[n][0].astype(BF16) for n in BIG_W})
    dx, small_mine, last_rider = _local_step(
        traffic, x[0], p[0, 0], loss_target[0],
        W["ffn1_norm"], W["mix_norm"], W["gmlp_v_norm"], W["gmlp_w_s"][0], jnp.transpose(W["gmlp_b"][0]),
        W["ffn2_norm"], W["ple_norm"], W["final_norm"].reshape(1, D),
        lambda grads, loss_part: pack(grads, jnp.broadcast_to(loss_part, (SMALL_ROWS_ALIGN, LANES))))

    out = {}
    parts = traffic.parts
    carrier = "ffn2_w_out"
    *out[carrier], got_bottom, small_parts = _adamw(
        "adamw_" + carrier, parts[carrier], W[carrier][0], M[carrier][0], V[carrier][0],
        rider=(last_rider[0] + [small_mine], last_rider[1] + [GATHER]))
    traffic.chipped("ffn1_w_in/1", got_bottom)
    parts["ffn1_w_in"] = jnp.concatenate([parts["ffn1_w_in/0"], parts["ffn1_w_in/1"]], axis=1)
    for n in BIG_W:
        if n != carrier:
            out[n] = _adamw("adamw_" + n, parts[n], W[n][0], M[n][0], V[n][0])
    zeros = jnp.zeros((SMALL_ROWS_ALIGN, LANES), F32)
    sg, sd, sm, sv = _adamw("adamw_small", small_parts, pack(W, zeros), pack(M, zeros), pack(V, zeros))
    for k, n in enumerate(small):
        out[n] = tuple(_unrows(arr[offs[k]:offs[k + 1]], W[n]) for arr in (sg, sd, sm, sv))
    loss = sg[offs[len(small)], 0]

    res = [loss, dx[None]]
    for k in range(4):
        res += [out[n][k].reshape(W[n].shape) for n in names]
    return tuple(res)
```

```python
import functools

import jax
import jax.numpy as jnp
from jax import lax
from jax.experimental import pallas as pl
from jax.experimental.pallas import tpu as pltpu

F32 = jnp.float32
BF16 = jnp.bfloat16
MESH = pl.DeviceIdType.MESH

N_DEV = 8
N_CHIPS_AWAY = 3
PART_SLOTS = N_CHIPS_AWAY + 2
EPS = 1e-6
ADAM_LR = 0.001
ADAM_B1 = 0.9
ADAM_B2 = 0.999
ADAM_EPS = 1e-08
ADAM_WD = 0.01
ADAM_STEP = 10

GM_WIDTH = 512
GM_HEADS = 4
CHUNK = 128
SB_WIDTH = 512
SB_HEAD_DIM = 64
SB_SCALE = 0.125
LANES = 128
SMALL_ROWS_ALIGN = 8

ROW_TILE = 512
GRAD_ROW_TILE = 2048
DW_IN_ROW_TILE = 4096
FFN_FWD_ROW_TILE = 1024
ATTN_Q_ROWS = 512
ATTN_BWD_Q_ROWS = 512
ATTN_KEY_BLOCK = 256
ATTN_UNROLL = 2
VMEM_LIMIT = 56 * 1024 * 1024


def _params(*sem):
    return pltpu.CompilerParams(dimension_semantics=sem, vmem_limit_bytes=VMEM_LIMIT)


def _dot(a, b):
    return jnp.dot(a, b, preferred_element_type=F32)


def _dot_nt(a, b):
    return lax.dot_general(a, b, (((1,), (1,)), ((), ())), preferred_element_type=F32)


def _dot_tn(a, b):
    return lax.dot_general(a, b, (((0,), (0,)), ((), ())), preferred_element_type=F32)


def _rms_parts(x):
    r = lax.rsqrt(jnp.mean(x * x, axis=-1, keepdims=True) + EPS)
    return r, x * r


def _rms_bwd(x, g, dy):
    r, xh = _rms_parts(x)
    dyg = dy * g
    dx = r * (dyg - xh * jnp.mean(dyg * xh, axis=-1, keepdims=True))
    return dx, jnp.sum(dy * xh, axis=0, keepdims=True)


def _sigmoid(x):
    return 1.0 / (1.0 + jnp.exp(-x))


_SQRT_HALF = 0.7071067811865476
_INV_SQRT_2PI = 0.3989422804014327


def _gelu(x):
    return 0.5 * x * (1.0 + lax.erf(x * _SQRT_HALF))


def _gelu_grad(x):
    return 0.5 * (1.0 + lax.erf(x * _SQRT_HALF)) + x * (_INV_SQRT_2PI * jnp.exp(-0.5 * x * x))


def _split_bf16(x):
    hi = x.astype(BF16)
    lo = (x - hi.astype(F32)).astype(BF16)
    return hi, lo


def _ffn_fwd(name, h, gain, w_in, w_out, rider=None):
    T, D = h.shape
    nb, _, FB = w_in.shape
    nh = nb // 2
    tm = min(FFN_FWD_ROW_TILE, T)

    def body(h_ref, g_ref, wg_ref, wu_ref, wo_ref, ho_ref, n_ref, G_ref, U_ref, a_ref, n_s, acc):
        jj = pl.program_id(1)

        @pl.when(jj == 0)
        def _():
            _, xh = _rms_parts(h_ref[...])
            n = (xh * g_ref[...]).astype(BF16)
            n_s[...] = n
            n_ref[...] = n
            acc[...] = jnp.zeros_like(acc)

        n = n_s[...]
        G = _dot(n, wg_ref[0])
        U = _dot(n, wu_ref[0])
        G_ref[0] = G.astype(BF16)
        U_ref[0] = U.astype(BF16)
        a = (G * _sigmoid(G) * U).astype(BF16)
        a_ref[0] = a
        acc[...] += _dot(a, wo_ref[...])

        @pl.when(jj == nh - 1)
        def _():
            ho_ref[...] = h_ref[...] + 0.5 * acc[...]

    row = lambda i, j: (i, 0)
    blk = lambda i, j: (j, i, 0)
    return _pallas(
        body, rider, name=name, grid=(T // tm, nh),
        in_specs=[pl.BlockSpec((tm, D), row),
                  pl.BlockSpec((1, D), lambda i, j: (0, 0)),
                  pl.BlockSpec((1, D, FB), lambda i, j: (j, 0, 0)),
                  pl.BlockSpec((1, D, FB), lambda i, j: (j + nh, 0, 0)),
                  pl.BlockSpec((FB, D), lambda i, j: (j, 0))],
        out_specs=[pl.BlockSpec((tm, D), row), pl.BlockSpec((tm, D), row),
                   pl.BlockSpec((1, tm, FB), blk), pl.BlockSpec((1, tm, FB), blk),
                   pl.BlockSpec((1, tm, FB), blk)],
        out_shape=[jax.ShapeDtypeStruct((T, D), F32), jax.ShapeDtypeStruct((T, D), BF16),
                   jax.ShapeDtypeStruct((nh, T, FB), BF16), jax.ShapeDtypeStruct((nh, T, FB), BF16),
                   jax.ShapeDtypeStruct((nh, T, FB), BF16)],
        scratch_shapes=[pltpu.VMEM((tm, D), BF16), pltpu.VMEM((tm, D), F32)],
        compiler_params=_params("arbitrary", "arbitrary"),
    )(h, gain, w_in, w_in, w_out)


def _ffn_bwd(name, dh, h_in, gain, G, U, w_in, w_out, rider=None):
    T, D = dh.shape
    nb, _, FB = w_in.shape
    nh = nb // 2
    tm = min(ROW_TILE, T)

    def body(dh_ref, h_ref, g_ref, G_ref, U_ref, wg_ref, wu_ref, wo_ref,
             dhin_ref, dg_ref, dG_ref, dU_ref, do_ref, dn_acc, do_s):
        i = pl.program_id(0)
        jj = pl.program_id(1)

        @pl.when(jj == 0)
        def _():
            d_out = (0.5 * dh_ref[...]).astype(BF16)
            do_s[...] = d_out
            do_ref[...] = d_out
            dn_acc[...] = jnp.zeros_like(dn_acc)

        @pl.when((i == 0) & (jj == 0))
        def _():
            dg_ref[...] = jnp.zeros_like(dg_ref)

        halves = [slice(0, tm // 2), slice(tm // 2, tm)]
        da = [_dot_nt(do_s[rows, :], wo_ref[...]) for rows in halves]
        dGU = []
        for rows, dav in zip(halves, da):
            dG, dU = _gate_grads(dav, G_ref[0, rows, :].astype(F32), U_ref[0, rows, :].astype(F32))
            dG_ref[0, rows, :] = dG
            dU_ref[0, rows, :] = dU
            dGU.append((dG, dU))
        dn = [_dot_nt(dG, wg_ref[0]) for dG, _ in dGU]
        dn = [d + _dot_nt(dU, wu_ref[0]) for d, (_, dU) in zip(dn, dGU)]
        for rows, d in zip(halves, dn):
            dn_acc[rows, :] += d

        @pl.when(jj == nh - 1)
        def _():
            dx, dg = _rms_bwd(h_ref[...], g_ref[...], dn_acc[...])
            dhin_ref[...] = dh_ref[...] + dx
            dg_ref[...] += dg

    row = lambda i, j: (i, 0)
    blk = lambda i, j: (j, i, 0)
    one = lambda i, j: (0, 0)
    return _pallas(
        body, rider, name=name, grid=(T // tm, nh),
        in_specs=[pl.BlockSpec((tm, D), row), pl.BlockSpec((tm, D), row), pl.BlockSpec((1, D), one),
                  pl.BlockSpec((1, tm, FB), blk), pl.BlockSpec((1, tm, FB), blk),
                  pl.BlockSpec((1, D, FB), lambda i, j: (j, 0, 0)),
                  pl.BlockSpec((1, D, FB), lambda i, j: (j + nh, 0, 0)),
                  pl.BlockSpec((FB, D), lambda i, j: (j, 0))],
        out_specs=[pl.BlockSpec((tm, D), row), pl.BlockSpec((1, D), one),
                   pl.BlockSpec((1, tm, FB), blk), pl.BlockSpec((1, tm, FB), blk),
                   pl.BlockSpec((tm, D), row)],
        out_shape=[jax.ShapeDtypeStruct((T, D), F32), jax.ShapeDtypeStruct((1, D), F32),
                   jax.ShapeDtypeStruct((nh, T, FB), BF16), jax.ShapeDtypeStruct((nh, T, FB), BF16),
                   jax.ShapeDtypeStruct((T, D), BF16)],
        scratch_shapes=[pltpu.VMEM((tm, D), F32), pltpu.VMEM((tm, D), BF16)],
        compiler_params=_params("arbitrary", "arbitrary"),
    )(dh, h_in, gain, G, U, w_in, w_in, w_out)


def _gate_grads(dav, Gv, Uv):
    sig = _sigmoid(Gv)
    return (dav * Uv * (sig * (1.0 + Gv * (1.0 - sig)))).astype(BF16), (dav * (Gv * sig)).astype(BF16)


def _ffn_bwd_gates(name, d_out, G, U, w_out, rider=None):
    T, D = d_out.shape
    nh, _, FB = G.shape
    tm = min(FFN_FWD_ROW_TILE, T)

    def body(do_ref, G_ref, U_ref, wo_ref, dG_ref, dU_ref):
        halves = [slice(0, tm // 2), slice(tm // 2, tm)]
        da = [_dot_nt(do_ref[rows, :], wo_ref[...]) for rows in halves]
        for rows, dav in zip(halves, da):
            dG_ref[0, rows, :], dU_ref[0, rows, :] = _gate_grads(
                dav, G_ref[0, rows, :].astype(F32), U_ref[0, rows, :].astype(F32))

    blk = lambda i, j: (j, i, 0)
    return _pallas(
        body, rider, name=name, grid=(T // tm, nh),
        in_specs=[pl.BlockSpec((tm, D), lambda i, j: (i, 0)), pl.BlockSpec((1, tm, FB), blk),
                  pl.BlockSpec((1, tm, FB), blk), pl.BlockSpec((FB, D), lambda i, j: (j, 0))],
        out_specs=[pl.BlockSpec((1, tm, FB), blk), pl.BlockSpec((1, tm, FB), blk)],
        out_shape=[jax.ShapeDtypeStruct((nh, T, FB), BF16), jax.ShapeDtypeStruct((nh, T, FB), BF16)],
        compiler_params=_params("arbitrary", "arbitrary"),
    )(d_out, G, U, w_out)


def _ffn_bwd_input(name, dh, h_in, gain, dG, dU, w_in, rider=None):
    T, D = dh.shape
    nb, _, FB = w_in.shape
    nh = nb // 2
    tm = min(FFN_FWD_ROW_TILE, T)

    def body(dh_ref, h_ref, g_ref, dG_ref, dU_ref, wg_ref, wu_ref, dhin_ref, dg_ref, dn_acc):
        i = pl.program_id(0)
        jj = pl.program_id(1)

        @pl.when(jj == 0)
        def _():
            dn_acc[...] = jnp.zeros_like(dn_acc)

        @pl.when((i == 0) & (jj == 0))
        def _():
            dg_ref[...] = jnp.zeros_like(dg_ref)

        halves = [slice(0, tm // 2), slice(tm // 2, tm)]
        dn = [_dot_nt(dG_ref[0, rows, :], wg_ref[0]) for rows in halves]
        dn = [d + _dot_nt(dU_ref[0, rows, :], wu_ref[0]) for d, rows in zip(dn, halves)]
        for rows, d in zip(halves, dn):
            dn_acc[rows, :] += d

        @pl.when(jj == nh - 1)
        def _():
            dx, dg = _rms_bwd(h_ref[...], g_ref[...], dn_acc[...])
            dhin_ref[...] = dh_ref[...] + dx
            dg_ref[...] += dg

    row = lambda i, j: (i, 0)
    blk = lambda i, j: (j, i, 0)
    one = lambda i, j: (0, 0)
    return _pallas(
        body, rider, name=name, grid=(T // tm, nh),
        in_specs=[pl.BlockSpec((tm, D), row), pl.BlockSpec((tm, D), row), pl.BlockSpec((1, D), one),
                  pl.BlockSpec((1, tm, FB), blk), pl.BlockSpec((1, tm, FB), blk),
                  pl.BlockSpec((1, D, FB), lambda i, j: (j, 0, 0)),
                  pl.BlockSpec((1, D, FB), lambda i, j: (j + nh, 0, 0))],
        out_specs=[pl.BlockSpec((tm, D), row), pl.BlockSpec((1, D), one)],
        out_shape=[jax.ShapeDtypeStruct((T, D), F32), jax.ShapeDtypeStruct((1, D), F32)],
        scratch_shapes=[pltpu.VMEM((tm, D), F32)],
        compiler_params=_params("arbitrary", "arbitrary"),
    )(dh, h_in, gain, dG, dU, w_in, w_in)


def _matmul_tn(name, a, b, nj, a_block, a_map, b_block, b_map, out_shape, out_block, out_map, rider=None):
    T = a.shape[-2]
    tt = a_block[-2]
    nt = T // tt
    kb, nbk = out_block[-2], out_block[-1]

    def body(a_ref, b_ref, o_ref, acc):
        t = pl.program_id(1)

        @pl.when(t == 0)
        def _():
            acc[...] = jnp.zeros_like(acc)

        av = (a_ref[0] if len(a_block) == 3 else a_ref[...]).astype(BF16)
        bv = b_ref[0] if len(b_block) == 3 else b_ref[...]
        acc[...] += _dot_tn(av, bv)

        @pl.when(t == nt - 1)
        def _():
            if len(out_block) == 3:
                o_ref[0] = acc[...].astype(o_ref.dtype)
            else:
                o_ref[...] = acc[...].astype(o_ref.dtype)

    got = _pallas(
        body, rider, name=name, grid=(nj, nt),
        in_specs=[pl.BlockSpec(a_block, a_map), pl.BlockSpec(b_block, b_map)],
        out_specs=[pl.BlockSpec(out_block, out_map)],
        out_shape=[jax.ShapeDtypeStruct(out_shape, BF16)],
        scratch_shapes=[pltpu.VMEM((kb, nbk), F32)],
        compiler_params=_params("arbitrary", "arbitrary"),
    )(a, b)
    return got[0] if rider is None else got


def _dw_in(name, n, dG, dU, rider=None):
    T, kr = n.shape
    nh, _, FB = dG.shape
    tt = min(DW_IN_ROW_TILE, T)
    nt = T // tt
    cut = LANES * ((kr // LANES + 1) // 2)

    def body(n_ref, dg_ref, du_ref, o_ref, acc):
        j = pl.program_id(0)
        t = pl.program_id(1)

        @pl.when(t == 0)
        def _():
            acc[...] = jnp.zeros_like(acc)

        def add(dz_ref):
            for rows in ((slice(0, cut), slice(cut, kr)) if cut < kr else (slice(0, kr),)):
                acc[rows, :] += _dot_tn(n_ref[:, rows], dz_ref[0])

        @pl.when(j < nh)
        def _():
            add(dg_ref)

        @pl.when(j >= nh)
        def _():
            add(du_ref)

        @pl.when(t == nt - 1)
        def _():
            o_ref[0] = acc[...].astype(BF16)

    return _pallas(
        body, rider, name=name, grid=(2 * nh, nt),
        in_specs=[pl.BlockSpec((tt, kr), lambda j, t: (t, 0)),
                  pl.BlockSpec((1, tt, FB), lambda j, t: (jnp.minimum(j, nh - 1), t, 0)),
                  pl.BlockSpec((1, tt, FB), lambda j, t: (jnp.maximum(j - nh, 0), t, 0))],
        out_specs=[pl.BlockSpec((1, kr, FB), lambda j, t: (j, 0, 0))],
        out_shape=[jax.ShapeDtypeStruct((2 * nh, kr, FB), BF16)],
        scratch_shapes=[pltpu.VMEM((kr, FB), F32)],
        compiler_params=_params("arbitrary", "arbitrary"),
    )(n, dG, dU)


def _mix_in_fwd(h, gain, w):
    T, D = h.shape
    W = w.shape[1]
    nuv = 2 * GM_WIDTH
    tm = min(ROW_TILE, T)

    def body(h_ref, g_ref, w_ref, n_ref, zuv_ref, qkv_ref):
        _, xh = _rms_parts(h_ref[...])
        n = (xh * g_ref[...]).astype(BF16)
        n_ref[...] = n
        z = _dot(n, w_ref[...])
        zuv_ref[...] = z[:, :nuv]
        qkv_ref[...] = z[:, nuv:].astype(BF16)

    row = lambda i: (i, 0)
    return pl.pallas_call(
        body, name="mix_in_fwd", grid=(T // tm,),
        in_specs=[pl.BlockSpec((tm, D), row), pl.BlockSpec((1, D), lambda i: (0, 0)),
                  pl.BlockSpec((D, W), lambda i: (0, 0))],
        out_specs=[pl.BlockSpec((tm, D), row), pl.BlockSpec((tm, nuv), row),
                   pl.BlockSpec((tm, W - nuv), row)],
        out_shape=[jax.ShapeDtypeStruct((T, D), BF16), jax.ShapeDtypeStruct((T, nuv), F32),
                   jax.ShapeDtypeStruct((T, W - nuv), BF16)],
        compiler_params=_params("arbitrary"),
    )(h, gain, w)


def _mix_in_bwd(dzuv, dqkv, w, h, gain, dh):
    T, D = h.shape
    W = w.shape[1]
    nuv = dzuv.shape[1]
    tm = min(ROW_TILE, T)

    def body(dzuv_ref, dqkv_ref, w_ref, h_ref, g_ref, dh_ref, dhin_ref, dg_ref, half_ref):
        @pl.when(pl.program_id(0) == 0)
        def _():
            dg_ref[...] = jnp.zeros_like(dg_ref)

        dn = _dot_nt(dzuv_ref[...], w_ref[:, :nuv]) + _dot_nt(dqkv_ref[...], w_ref[:, nuv:])
        dx, dg = _rms_bwd(h_ref[...], g_ref[...], dn)
        dh_in = dh_ref[...] + dx
        dhin_ref[...] = dh_in
        half_ref[...] = (0.5 * dh_in).astype(BF16)
        dg_ref[...] += dg

    row = lambda i: (i, 0)
    one = lambda i: (0, 0)
    return pl.pallas_call(
        body, name="mix_in_bwd", grid=(T // tm,),
        in_specs=[pl.BlockSpec((tm, nuv), row), pl.BlockSpec((tm, W - nuv), row),
                  pl.BlockSpec((D, W), one), pl.BlockSpec((tm, D), row), pl.BlockSpec((1, D), one),
                  pl.BlockSpec((tm, D), row)],
        out_specs=[pl.BlockSpec((tm, D), row), pl.BlockSpec((1, D), one), pl.BlockSpec((tm, D), row)],
        out_shape=[jax.ShapeDtypeStruct((T, D), F32), jax.ShapeDtypeStruct((1, D), F32),
                   jax.ShapeDtypeStruct((T, D), BF16)],
        compiler_params=_params("arbitrary"),
    )(dzuv, dqkv, w, h, gain, dh)


def _gmlp_norm(zv, gv):
    v = _gelu(zv)
    r, vh = _rms_parts(v)
    return r, vh, (vh * gv).astype(BF16)


def _causal_ws(ws_ref, hd):
    r = lax.broadcasted_iota(jnp.int32, (CHUNK, CHUNK), 0)
    c = lax.broadcasted_iota(jnp.int32, (CHUNK, CHUNK), 1)
    return jnp.where(r >= c, ws_ref[hd], 0.0).astype(BF16)


def _gmlp_fwd(zuv, gv, ws, b_t):
    T = zuv.shape[0]
    tg = min(ROW_TILE, T)

    def body(zu_ref, zv_ref, gv_ref, ws_ref, bt_ref, o_ref):
        u = _gelu(zu_ref[...])
        _, _, vn = _gmlp_norm(zv_ref[...], gv_ref[...])
        for hd in range(GM_HEADS):
            wc = _causal_ws(ws_ref, hd)
            cols = slice(hd * CHUNK, (hd + 1) * CHUNK)
            for c in range(tg // CHUNK):
                rows = slice(c * CHUNK, (c + 1) * CHUNK)
                sv = _dot(wc, vn[rows, cols]) + bt_ref[:, hd:hd + 1]
                o_ref[rows, cols] = (u[rows, cols] * sv).astype(BF16)

    return pl.pallas_call(
        body, name="gmlp_fwd", grid=(T // tg,),
        in_specs=[pl.BlockSpec((tg, GM_WIDTH), lambda i: (i, 0)), pl.BlockSpec((tg, GM_WIDTH), lambda i: (i, 1)),
                  pl.BlockSpec((1, GM_WIDTH), lambda i: (0, 0)),
                  pl.BlockSpec((GM_HEADS, CHUNK, CHUNK), lambda i: (0, 0, 0)),
                  pl.BlockSpec((CHUNK, GM_HEADS), lambda i: (0, 0))],
        out_specs=pl.BlockSpec((tg, GM_WIDTH), lambda i: (i, 0)),
        out_shape=jax.ShapeDtypeStruct((T, GM_WIDTH), BF16),
        compiler_params=_params("arbitrary"),
    )(zuv, zuv, gv, ws, b_t)


def _gmlp_bwd(zuv, d_gm, gv, ws, b_t):
    T = zuv.shape[0]
    tg = min(ROW_TILE, T)
    ng = T // tg

    def body(zu_ref, zv_ref, dgm_ref, gv_ref, ws_ref, bt_ref, dz_ref, dgv_ref, dws_ref, dbt_ref, dsv_acc, dvn_s):
        i = pl.program_id(0)

        @pl.when(i == 0)
        def _():
            dgv_ref[...] = jnp.zeros_like(dgv_ref)
            dws_ref[...] = jnp.zeros_like(dws_ref)
            dsv_acc[...] = jnp.zeros_like(dsv_acc)

        zu = zu_ref[...]
        zv = zv_ref[...]
        dgm = dgm_ref[...]
        gvv = gv_ref[...]
        u = _gelu(zu)
        rv, vh, vn = _gmlp_norm(zv, gvv)
        dsv = dgm * u
        dsv_b = dsv.astype(BF16)
        for hd in range(GM_HEADS):
            wc = _causal_ws(ws_ref, hd)
            cols = slice(hd * CHUNK, (hd + 1) * CHUNK)
            dws = jnp.zeros((CHUNK, CHUNK), F32)
            dsv_sum = jnp.zeros((CHUNK, CHUNK), F32)
            for c in range(tg // CHUNK):
                rows = slice(c * CHUNK, (c + 1) * CHUNK)
                vch = vn[rows, cols]
                sv = _dot(wc, vch) + bt_ref[:, hd:hd + 1]
                dz_ref[rows, cols] = (dgm[rows, cols] * sv * _gelu_grad(zu[rows, cols])).astype(BF16)
                dws += _dot_nt(dsv_b[rows, cols], vch)
                dsv_sum += dsv[rows, cols]
                dvn_s[rows, cols] = _dot_tn(wc, dsv_b[rows, cols])
            dws_ref[hd] += dws
            dsv_acc[:, cols] += dsv_sum
        dvn = dvn_s[...]
        dvh = dvn * gvv
        dv = rv * (dvh - vh * jnp.mean(dvh * vh, axis=-1, keepdims=True))
        dgv_ref[...] += jnp.sum(dvn * vh, axis=0, keepdims=True)
        dz_ref[:, GM_WIDTH:] = (dv * _gelu_grad(zv)).astype(BF16)

        @pl.when(i == ng - 1)
        def _():
            r = lax.broadcasted_iota(jnp.int32, (CHUNK, CHUNK), 0)
            c = lax.broadcasted_iota(jnp.int32, (CHUNK, CHUNK), 1)
            for hd in range(GM_HEADS):
                dws_ref[hd] = jnp.where(r >= c, dws_ref[hd], 0.0)
                dbt_ref[:, hd:hd + 1] = jnp.sum(dsv_acc[:, hd * CHUNK:(hd + 1) * CHUNK], axis=1, keepdims=True)

    return pl.pallas_call(
        body, name="gmlp_bwd", grid=(ng,),
        in_specs=[pl.BlockSpec((tg, GM_WIDTH), lambda i: (i, 0)), pl.BlockSpec((tg, GM_WIDTH), lambda i: (i, 1)),
                  pl.BlockSpec((tg, GM_WIDTH), lambda i: (i, 0)),
                  pl.BlockSpec((1, GM_WIDTH), lambda i: (0, 0)),
                  pl.BlockSpec((GM_HEADS, CHUNK, CHUNK), lambda i: (0, 0, 0)),
                  pl.BlockSpec((CHUNK, GM_HEADS), lambda i: (0, 0))],
        out_specs=[pl.BlockSpec((tg, 2 * GM_WIDTH), lambda i: (i, 0)),
                   pl.BlockSpec((1, GM_WIDTH), lambda i: (0, 0)),
                   pl.BlockSpec((GM_HEADS, CHUNK, CHUNK), lambda i: (0, 0, 0)),
                   pl.BlockSpec((CHUNK, GM_HEADS), lambda i: (0, 0))],
        out_shape=[jax.ShapeDtypeStruct((T, 2 * GM_WIDTH), BF16), jax.ShapeDtypeStruct((1, GM_WIDTH), F32),
                   jax.ShapeDtypeStruct((GM_HEADS, CHUNK, CHUNK), F32),
                   jax.ShapeDtypeStruct((CHUNK, GM_HEADS), F32)],
        scratch_shapes=[pltpu.VMEM((CHUNK, GM_WIDTH), F32), pltpu.VMEM((tg, GM_WIDTH), F32)],
        compiler_params=_params("arbitrary"),
    )(zuv, zuv, d_gm, gv, ws, b_t)


def _scan_matrix(blk, keep):
    r = lax.broadcasted_iota(jnp.int32, (blk, blk), 0)
    c = lax.broadcasted_iota(jnp.int32, (blk, blk), 1)
    return jnp.where(keep(r, c), 1.0, 0.0).astype(BF16)


def _scan_matrix2(blk, keep, value):
    m = _scan_matrix(blk, keep) * value
    return jnp.concatenate([m, m], axis=0)


def _scan(x, mat2):
    hi, lo = _split_bf16(x)
    return _dot(jnp.concatenate([hi, lo], axis=1), mat2)


def _head_masks(q):
    lane = lax.broadcasted_iota(jnp.int32, q.shape, 1)
    m0 = lane < SB_HEAD_DIM
    zero = jnp.zeros_like(q)
    return m0, jnp.where(m0, q, zero), jnp.where(m0, zero, q)


_LOG2E = 1.4426950408889634


def _softplus_parts(z):
    e = jnp.exp2(jnp.abs(z) * (-_LOG2E))
    ope = 1.0 + e
    return e, ope, jnp.maximum(z, 0.0) + jnp.log(ope)


def _attn_fwd(qkv, rider=None):
    T = qkv.shape[0]
    tk = ATTN_KEY_BLOCK
    tq = min(ATTN_Q_ROWS, T)
    band = tq // tk
    assert band % ATTN_UNROLL == 0 or T == tq
    ngrp = SB_WIDTH // LANES

    def body(q_ref, k_ref, v_ref, o_ref, l_ref, acc, run):
        i = pl.program_id(1)
        suffix = _scan_matrix2(tk, lambda r, c: r >= c, -1.0)
        row = lax.broadcasted_iota(jnp.int32, (tq, tk), 0)
        col = lax.broadcasted_iota(jnp.int32, (tq, tk), 1)
        m0, q0, q1 = _head_masks(q_ref[...] * SB_SCALE)
        acc[...] = jnp.zeros_like(acc)
        run[...] = jnp.zeros_like(run)

        def tiles(work):
            heads = (q0, q1)
            kv = []
            for j, _ in work:
                start = pl.multiple_of(j * tk, tk)
                kv.append((k_ref[pl.ds(start, tk), :], v_ref[pl.ds(start, tk), :]))
            z = [[_dot_nt(qh, kj) for qh in heads] for kj, _ in kv]
            sp = [[_softplus_parts(zz)[2] for zz in zt] for zt in z]
            sp = [[s if m is None else jnp.where(m, s, 0.0) for s in st] for st, (_, m) in zip(sp, work)]
            res = [[_scan(s, suffix) for s in st] for st in sp]
            runs = [run[hd] for hd in range(len(heads))]
            a = []
            for t, (_, m) in enumerate(work):
                at = []
                for hd in range(len(heads)):
                    av = jnp.exp(z[t][hd] + (runs[hd] + res[t][hd]))
                    at.append(av if m is None else jnp.where(m, av, 0.0))
                    runs[hd] = runs[hd] + res[t][hd][:, 0:1]
                a.append(at)
            for hd in range(len(heads)):
                run[hd] = runs[hd]
                upd = _dot(a[0][hd].astype(BF16), kv[0][1])
                for t in range(1, len(work)):
                    upd = upd + _dot(a[t][hd].astype(BF16), kv[t][1])
                acc[hd] += upd

        tiles([(i * band + jb, jb * tk + col < row) for jb in reversed(range(band))])

        def full_step(it, carry):
            tiles([(i * band - 1 - ATTN_UNROLL * it - u, None) for u in range(ATTN_UNROLL)])
            return carry

        lax.fori_loop(0, i * (band // ATTN_UNROLL), full_step, 0)
        o_ref[...] = jnp.where(m0, acc[0], acc[1]).astype(BF16)
        l_ref[...] = jnp.where(m0, jnp.broadcast_to(run[0], (tq, LANES)), jnp.broadcast_to(run[1], (tq, LANES)))

    return _pallas(
        body, rider, name="attn_fwd", grid=(ngrp, T // tq),
        in_specs=[pl.BlockSpec((tq, LANES), lambda g, i: (i, g)),
                  pl.BlockSpec((T, LANES), lambda g, i: (0, ngrp + g)),
                  pl.BlockSpec((T, LANES), lambda g, i: (0, 2 * ngrp + g))],
        out_specs=[pl.BlockSpec((tq, LANES), lambda g, i: (i, g)),
                   pl.BlockSpec((tq, LANES), lambda g, i: (i, g))],
        out_shape=[jax.ShapeDtypeStruct((T, SB_WIDTH), BF16), jax.ShapeDtypeStruct((T, SB_WIDTH), F32)],
        scratch_shapes=[pltpu.VMEM((2, tq, LANES), F32), pltpu.VMEM((2, tq, 1), F32)],
        compiler_params=_params("arbitrary", "arbitrary"),
    )(qkv, qkv, qkv)


def _attn_bwd(qkv, d_o, ltot, rider=None):
    T = qkv.shape[0]
    tk = ATTN_KEY_BLOCK
    tq = min(ATTN_BWD_Q_ROWS, T)
    band = tq // tk
    nq = T // tq
    ngrp = SB_WIDTH // LANES

    def body(q_ref, k_ref, v_ref, do_ref, l_ref, dq_ref, dk_ref, dv_ref, dq_acc, dk_acc, dv_acc, lpre, ppre):
        i = pl.program_id(1)

        @pl.when(i == 0)
        def _():
            dk_acc[...] = jnp.zeros_like(dk_acc)
            dv_acc[...] = jnp.zeros_like(dv_acc)

        excl = _scan_matrix(tk, lambda r, c: r < c)
        excl2 = jnp.concatenate([excl, excl], axis=0)
        row = lax.broadcasted_iota(jnp.int32, (tq, tk), 0)
        col = lax.broadcasted_iota(jnp.int32, (tq, tk), 1)
        m0, q0, q1 = _head_masks(q_ref[...] * SB_SCALE)
        _, d0, d1 = _head_masks(do_ref[...].astype(BF16))
        lt = l_ref[...]
        ltots = (lt[:, 0:1], lt[:, SB_HEAD_DIM:SB_HEAD_DIM + 1])
        dq_acc[...] = jnp.zeros_like(dq_acc)
        lpre[...] = jnp.zeros_like(lpre)
        ppre[...] = jnp.zeros_like(ppre)

        def tiles(work):
            heads = ((q0, d0), (q1, d1))
            nhd = len(heads)
            starts = [pl.multiple_of(j * tk, tk) for j, _ in work]
            kv = [(k_ref[pl.ds(st, tk), :], v_ref[pl.ds(st, tk), :]) for st in starts]
            masks = [m for _, m in work]
            every = [(t, hd) for t in range(len(work)) for hd in range(nhd)]
            z = {(t, hd): _dot_nt(heads[hd][0], kv[t][0]) for t, hd in every}
            da = {(t, hd): _dot_nt(heads[hd][1], kv[t][1]) for t, hd in every}
            sp, beta = {}, {}
            for key in every:
                s = _softplus_parts(z[key])[2]
                beta[key] = jnp.exp(z[key] - s)
                sp[key] = s if masks[key[0]] is None else jnp.where(masks[key[0]], s, 0.0)
            res = {key: _scan(sp[key], excl2) for key in every}
            lp = [lpre[hd] for hd in range(nhd)]
            a, p = {}, {}
            for t, hd in every:
                av = jnp.exp(z[t, hd] + ((ltots[hd] + lp[hd]) + res[t, hd]))
                a[t, hd] = av if masks[t] is None else jnp.where(masks[t], av, 0.0)
                p[t, hd] = a[t, hd] * da[t, hd]
                lp[hd] = lp[hd] + (res[t, hd][:, tk - 1:tk] + sp[t, hd][:, tk - 1:tk])
            resp = {key: _dot(p[key].astype(BF16), excl) for key in every}
            pp = [ppre[hd] for hd in range(nhd)]
            dzb = {}
            for t, hd in every:
                dz = p[t, hd] - beta[t, hd] * (p[t, hd] + (pp[hd] + resp[t, hd]))
                if masks[t] is not None:
                    dz = jnp.where(masks[t], dz, 0.0)
                dzb[t, hd] = dz.astype(BF16)
                pp[hd] = pp[hd] + (resp[t, hd][:, tk - 1:tk] + p[t, hd][:, tk - 1:tk])
            for hd in range(nhd):
                lpre[hd] = lp[hd]
                ppre[hd] = pp[hd]
                upd = _dot(dzb[0, hd], kv[0][0])
                for t in range(1, len(work)):
                    upd = upd + _dot(dzb[t, hd], kv[t][0])
                dq_acc[hd] += upd
            for t, st in enumerate(starts):
                dk = _dot_tn(dzb[t, 0], heads[0][0])
                dv = _dot_tn(a[t, 0].astype(BF16), heads[0][1])
                for hd in range(1, nhd):
                    dk = dk + _dot_tn(dzb[t, hd], heads[hd][0])
                    dv = dv + _dot_tn(a[t, hd].astype(BF16), heads[hd][1])
                dk_acc[pl.ds(st, tk), :] += dk
                dv_acc[pl.ds(st, tk), :] += dv

        def full_step(j, carry):
            tiles([(j, None)])
            return carry

        lax.fori_loop(0, i * band, full_step, 0)
        for jb in range(band):
            tiles([(i * band + jb, jb * tk + col < row)])
        dq_ref[...] = (jnp.where(m0, dq_acc[0], dq_acc[1]) * SB_SCALE).astype(BF16)

        @pl.when(i == nq - 1)
        def _():
            dk_ref[...] = dk_acc[...].astype(BF16)
            dv_ref[...] = dv_acc[...].astype(BF16)

    qmap = lambda g, i: (i, g)
    return _pallas(
        body, rider, name="attn_bwd", grid=(ngrp, nq),
        in_specs=[pl.BlockSpec((tq, LANES), qmap),
                  pl.BlockSpec((T, LANES), lambda g, i: (0, ngrp + g)),
                  pl.BlockSpec((T, LANES), lambda g, i: (0, 2 * ngrp + g)),
                  pl.BlockSpec((tq, LANES), qmap), pl.BlockSpec((tq, LANES), qmap)],
        out_specs=[pl.BlockSpec((tq, LANES), qmap),
                   pl.BlockSpec((T, LANES), lambda g, i: (0, g)),
                   pl.BlockSpec((T, LANES), lambda g, i: (0, g))],
        out_shape=[jax.ShapeDtypeStruct((T, SB_WIDTH), BF16)] * 3,
        scratch_shapes=[pltpu.VMEM((2, tq, LANES), F32), pltpu.VMEM((T, LANES), F32),
                        pltpu.VMEM((T, LANES), F32), pltpu.VMEM((2, tq, 1), F32),
                        pltpu.VMEM((2, tq, 1), F32)],
        compiler_params=_params("arbitrary", "arbitrary"),
    )(qkv, qkv, qkv, d_o, ltot)


def _mix_out_fwd(h, gm, sb, w):
    T, D = h.shape
    tm = min(ROW_TILE, T)

    def body(h_ref, gm_ref, sb_ref, w_ref, o_ref):
        o_ref[...] = h_ref[...] + _dot(gm_ref[...], w_ref[:GM_WIDTH, :]) + _dot(sb_ref[...], w_ref[GM_WIDTH:, :])

    row = lambda i: (i, 0)
    return pl.pallas_call(
        body, name="mix_out_fwd", grid=(T // tm,),
        in_specs=[pl.BlockSpec((tm, D), row), pl.BlockSpec((tm, GM_WIDTH), row), pl.BlockSpec((tm, SB_WIDTH), row),
                  pl.BlockSpec((GM_WIDTH + SB_WIDTH, D), lambda i: (0, 0))],
        out_specs=pl.BlockSpec((tm, D), row),
        out_shape=jax.ShapeDtypeStruct((T, D), F32),
        compiler_params=_params("arbitrary"),
    )(h, gm, sb, w)


def _mix_out_bwd(dh, w):
    T, D = dh.shape
    tm = min(ROW_TILE, T)

    def body(dh_ref, w_ref, dgm_ref, dsb_ref, dhb_ref):
        dhb = dh_ref[...].astype(BF16)
        dhb_ref[...] = dhb
        dgm_ref[...] = _dot_nt(dhb, w_ref[:GM_WIDTH, :])
        dsb_ref[...] = _dot_nt(dhb, w_ref[GM_WIDTH:, :])

    row = lambda i: (i, 0)
    return pl.pallas_call(
        body, name="mix_out_bwd", grid=(T // tm,),
        in_specs=[pl.BlockSpec((tm, D), row), pl.BlockSpec((GM_WIDTH + SB_WIDTH, D), lambda i: (0, 0))],
        out_specs=[pl.BlockSpec((tm, GM_WIDTH), row), pl.BlockSpec((tm, SB_WIDTH), row), pl.BlockSpec((tm, D), row)],
        out_shape=[jax.ShapeDtypeStruct((T, GM_WIDTH), F32), jax.ShapeDtypeStruct((T, SB_WIDTH), F32),
                   jax.ShapeDtypeStruct((T, D), BF16)],
        compiler_params=_params("arbitrary"),
    )(dh, w)


def _tail(h3, p, target, g_ple, g_fin, w_gate, w_proj):
    T, D = h3.shape
    PD = p.shape[1]
    tm = min(ROW_TILE, T)

    def body(h_ref, p_ref, t_ref, gp_ref, gf_ref, wg_ref, wp_ref,
             loss_ref, dh_ref, n4_ref, dgl_ref, dpp_ref, dgp_ref, dgf_ref):
        @pl.when(pl.program_id(0) == 0)
        def _():
            loss_ref[...] = jnp.zeros_like(loss_ref)
            dgp_ref[...] = jnp.zeros_like(dgp_ref)
            dgf_ref[...] = jnp.zeros_like(dgf_ref)

        h3v = h_ref[...]
        gp = gp_ref[...]
        gf = gf_ref[...]
        r3, xh3 = _rms_parts(h3v)
        n4 = (xh3 * gp).astype(BF16)
        n4_ref[...] = n4
        gate = _sigmoid(_dot(n4, wg_ref[...]))
        pp = _dot(p_ref[...].astype(BF16), wp_ref[...])
        h4 = h3v + gate * pp
        r4, xh4 = _rms_parts(h4)
        err = xh4 * gf - t_ref[...]
        loss_ref[...] += jnp.full(loss_ref.shape, (0.5 / D) * jnp.sum(err * err), F32)
        dy = err * (1.0 / D)
        dgf_ref[...] += jnp.sum(dy * xh4, axis=0, keepdims=True)
        dyg = dy * gf
        dh4 = r4 * (dyg - xh4 * jnp.mean(dyg * xh4, axis=-1, keepdims=True))
        dpp_ref[...] = (dh4 * gate).astype(BF16)
        dgl = (dh4 * pp * gate * (1.0 - gate)).astype(BF16)
        dgl_ref[...] = dgl
        dn4 = _dot_nt(dgl, wg_ref[...])
        dgp_ref[...] += jnp.sum(dn4 * xh3, axis=0, keepdims=True)
        dn4g = dn4 * gp
        dh_ref[...] = dh4 + r3 * (dn4g - xh3 * jnp.mean(dn4g * xh3, axis=-1, keepdims=True))

    row = lambda i: (i, 0)
    one = lambda i: (0, 0)
    return pl.pallas_call(
        body, name="tail", grid=(T // tm,),
        in_specs=[pl.BlockSpec((tm, D), row), pl.BlockSpec((tm, PD), row), pl.BlockSpec((tm, D), row),
                  pl.BlockSpec((1, D), one), pl.BlockSpec((1, D), one),
                  pl.BlockSpec((D, D), one), pl.BlockSpec((PD, D), one)],
        out_specs=[pl.BlockSpec((1, LANES), one), pl.BlockSpec((tm, D), row), pl.BlockSpec((tm, D), row),
                   pl.BlockSpec((tm, D), row), pl.BlockSpec((tm, D), row),
                   pl.BlockSpec((1, D), one), pl.BlockSpec((1, D), one)],
        out_shape=[jax.ShapeDtypeStruct((1, LANES), F32), jax.ShapeDtypeStruct((T, D), F32),
                   jax.ShapeDtypeStruct((T, D), BF16), jax.ShapeDtypeStruct((T, D), BF16),
                   jax.ShapeDtypeStruct((T, D), BF16),
                   jax.ShapeDtypeStruct((1, D), F32), jax.ShapeDtypeStruct((1, D), F32)],
        compiler_params=_params("arbitrary"),
    )(h3, p, target, g_ple, g_fin, w_gate, w_proj)


FFN1_W = ("ffn1_w_in", "ffn1_w_out")
MIX_W = ("w_mix_in", "w_mix_out")
REST_W = ("ffn2_w_in", "ffn2_w_out", "ple_w_gate", "ple_w_proj")
BIG_W = FFN1_W + MIX_W + REST_W
COLUMN_SHARDED = ("w_mix_in", "ple_w_proj")


class _Traffic:
    def __init__(self, shards):
        self.shards = shards
        self.parts = {}
        self.blocks, self.sums = {}, {}

    @staticmethod
    def _full(name, gathered):
        if name in COLUMN_SHARDED:
            return jnp.transpose(gathered, (1, 0, 2)).reshape(gathered.shape[1], -1)
        if name.endswith("_w_in"):
            return gathered
        return gathered.reshape(-1, gathered.shape[-1])

    @staticmethod
    def _blocks(name, grad):
        name = name.split("/")[0]
        if name in COLUMN_SHARDED:
            return jnp.transpose(grad.reshape(grad.shape[0], N_DEV, -1), (1, 0, 2))
        if name.endswith("_w_in"):
            return grad
        return grad.reshape(N_DEV, -1, grad.shape[-1])

    def gather_now(self, names):
        got = _exchange("gather_" + names[0], [self.shards[n] for n in names], [GATHER] * len(names))
        return self.gathered(names, got)

    def gather_rider(self, names):
        return [self.shards[n] for n in names], [GATHER] * len(names)

    def gathered(self, names, got):
        return {n: self._full(n, g) for n, g in zip(names, got)}

    def scatter_rider(self, grads, gather=()):
        return ([self._blocks(n, g) for n, g in grads.items()] + list(gather),
                [SCATTER] * len(grads) + [GATHER] * len(gather))

    def scattered(self, names, got):
        self.parts.update(zip(names, got))
        return got[len(names):]


    def pair_rider(self, name, grad):
        self.blocks[name] = self._blocks(name, grad)
        return [self.blocks[name]], [PAIR]

    def paired(self, name, got):
        self.sums[name] = _stage_pair_sums("pair_sum_" + name.replace("/", "_"), self.blocks[name], got)

    def chip_rider(self, name):
        return [self.sums[name]], [CHIP]

    def chipped(self, name, got):
        self.parts[name] = got


def _local_step(traffic, x, p, target, g1, gmix, gv, ws, b_t, g2, gple, gfin, pack_small):
    T, D = x.shape
    tm = min(ROW_TILE, T)

    w = traffic.gather_now(FFN1_W)
    h1, n1, G1, U1, a1, *got = _ffn_fwd("ffn1_fwd", x, g1, w["ffn1_w_in"], w["ffn1_w_out"],
                                        rider=traffic.gather_rider(MIX_W))
    w.update(traffic.gathered(MIX_W, got))
    n2, zuv, qkv = _mix_in_fwd(h1, gmix, w["w_mix_in"])
    gm = _gmlp_fwd(zuv, gv, ws, b_t)
    sb, ltot, *got = _attn_fwd(qkv, rider=traffic.gather_rider(REST_W))
    w.update(traffic.gathered(REST_W, got))
    h2 = _mix_out_fwd(h1, gm, sb, w["w_mix_out"])
    h3, n3, G2, U2, a2 = _ffn_fwd("ffn2_fwd", h2, g2, w["ffn2_w_in"], w["ffn2_w_out"])
    loss, dh3, n4, d_gl, d_pp, dg_ple, dg_fin = _tail(h3, p, target, gple, gfin, w["ple_w_gate"], w["ple_w_proj"])

    nb, _, FB = w["ffn1_w_in"].shape
    nh = nb // 2

    tt = min(GRAD_ROW_TILE, T)

    def dw_out(name, a, d_out):
        return _matmul_tn(name, a, d_out, nh, (1, tt, FB), lambda j, t: (j, t, 0), (tt, D), lambda j, t: (t, 0),
                          (nh, FB, D), (1, FB, D), lambda j, t: (j, 0, 0))

    def dense_tn(name, a, b, ncol):
        ka, nbw = a.shape[1], b.shape[1] // ncol
        return _matmul_tn(name, a, b, ncol, (tt, ka), lambda j, t: (t, 0), (tt, nbw), lambda j, t: (t, j),
                          (ka, b.shape[1]), (ka, nbw), lambda j, t: (0, j))

    grads = dict(ple_w_gate=dense_tn("dw_ple_gate", n4, d_gl, 2), ple_w_proj=dense_tn("dw_ple_proj", p, d_pp, 1))
    dh2, dg2, dG2, dU2, dout2 = _ffn_bwd("ffn2_bwd", dh3, h2, g2, G2, U2, w["ffn2_w_in"], w["ffn2_w_out"])
    grads["ffn2_w_in"], = _dw_in("ffn2_dw_in", n3, dG2, dU2)
    grads["ffn2_w_out"] = dw_out("ffn2_dw_out", a2, dout2)
    grads = {n: grads[n] for n in REST_W}

    d_gm, d_sb, dh2_bf = _mix_out_bwd(dh2, w["w_mix_out"])
    grads["w_mix_out"] = jnp.concatenate([dense_tn("dw_mix_out_gm", gm, dh2_bf, 1),
                                          dense_tn("dw_mix_out_sb", sb, dh2_bf, 1)], axis=0)
    dzuv, dgv, dws, db_t = _gmlp_bwd(zuv, d_gm, gv, ws, b_t)
    dq, dk, dv, *got = _attn_bwd(qkv, d_sb, ltot, rider=traffic.scatter_rider(grads))
    traffic.scattered(list(grads), got)
    dqkv = jnp.concatenate([dq, dk, dv], axis=1)
    dw_mi = jnp.concatenate([dense_tn("dw_mix_in_uv", n2, dzuv, 2), dense_tn("dw_mix_in_qkv", n2, dqkv, 3)], axis=1)
    dh1, dgmix, dout1 = _mix_in_bwd(dzuv, dqkv, w["w_mix_in"], h1, gmix, dh2)

    def dw_out_riding(name, a, d_out, rider):
        return _matmul_tn(name, a, d_out, nh, (1, tt, FB), lambda j, t: (j, t, 0), (tt, D), lambda j, t: (t, 0),
                          (nh, FB, D), (1, FB, D), lambda j, t: (j, 0, 0), rider=rider)

    def both(*riders):
        return [x for r in riders for x in r[0]], [k for r in riders for k in r[1]]

    dw_out1, got = dw_out_riding("ffn1_dw_out", a1, dout1, traffic.pair_rider("w_mix_in", dw_mi))
    traffic.paired("w_mix_in", got)
    dG1, dU1, got_mi, got = _ffn_bwd_gates("ffn1_bwd_gates", dout1, G1, U1, w["ffn1_w_out"],
                                           rider=both(traffic.chip_rider("w_mix_in"),
                                                      traffic.pair_rider("ffn1_w_out", dw_out1)))
    traffic.chipped("w_mix_in", got_mi)
    traffic.paired("ffn1_w_out", got)
    half = D // 2
    top, got = _dw_in("ffn1_dw_in_top", n1[:, :half], dG1, dU1, rider=traffic.chip_rider("ffn1_w_out"))
    traffic.chipped("ffn1_w_out", got)
    bottom, got = _dw_in("ffn1_dw_in_bottom", n1[:, half:], dG1, dU1, rider=traffic.pair_rider("ffn1_w_in/0", top))
    traffic.paired("ffn1_w_in/0", got)
    dx, dg1, got_top, got = _ffn_bwd_input("ffn1_bwd_input", dh1, x, g1, dG1, dU1, w["ffn1_w_in"],
                                           rider=both(traffic.chip_rider("ffn1_w_in/0"),
                                                      traffic.pair_rider("ffn1_w_in/1", bottom)))
    traffic.chipped("ffn1_w_in/0", got_top)
    traffic.paired("ffn1_w_in/1", got)

    small = pack_small(dict(ffn1_norm=dg1, mix_norm=dgmix, gmlp_v_norm=dgv, gmlp_w_s=dws, gmlp_b=jnp.transpose(db_t),
                            ffn2_norm=dg2, ple_norm=dg_ple, final_norm=dg_fin), loss)
    return dx, small, traffic.chip_rider("ffn1_w_in/1")


def _my_index():
    return 4 * lax.axis_index("x") + 2 * lax.axis_index("y") + lax.axis_index("c")


def _stage_pair_sums(name, blocks, got):
    _, R, C = blocks.shape
    me = _my_index()
    index = jnp.stack([jnp.bitwise_xor(me, 2 * c) for c in range(1, N_CHIPS_AWAY + 1)] + [me, me]).astype(jnp.int32)

    def body(index_ref, b_ref, g_ref, o_ref):
        i = pl.program_id(0)

        @pl.when(i < N_CHIPS_AWAY)
        def _():
            o_ref[...] = (b_ref[...].astype(F32) + g_ref[...].astype(F32)).astype(BF16)

        @pl.when(i == N_CHIPS_AWAY)
        def _():
            o_ref[...] = g_ref[...]

        @pl.when(i == N_CHIPS_AWAY + 1)
        def _():
            o_ref[...] = b_ref[...]

    return pl.pallas_call(
        body, name=name,
        grid_spec=pltpu.PrefetchScalarGridSpec(
            num_scalar_prefetch=1, grid=(PART_SLOTS,),
            in_specs=[pl.BlockSpec((1, R, C), lambda i, idx: (idx[i], 0, 0)),
                      pl.BlockSpec((1, R, C), lambda i, idx: (jnp.minimum(i, N_CHIPS_AWAY), 0, 0))],
            out_specs=pl.BlockSpec((1, R, C), lambda i, idx: (i, 0, 0))),
        out_shape=jax.ShapeDtypeStruct((PART_SLOTS, R, C), BF16), compiler_params=_params("arbitrary"),
    )(index, blocks, got)


def _peer(d):
    x, y, c = lax.axis_index("x"), lax.axis_index("y"), lax.axis_index("c")
    px = 1 - x if d & 4 else x
    py = 1 - y if d & 2 else y
    pc = 1 - c if d & 1 else c
    return (px, py, pc), 4 * px + 2 * py + pc


GATHER, SCATTER, PAIR, CHIP = "gather", "scatter", "pair", "chip"


class _ExchangePlan:
    def __init__(self, ins, outs, send, recv, local, kinds):
        self.ins, self.outs, self.send, self.recv, self.local, self.kinds = ins, outs, send, recv, local, kinds
        self.scatter = [k == SCATTER for k in kinds]
        self.me = _peer(0)[1]

    def _remote(self, t, sem, src, slot, peer):
        return pltpu.make_async_remote_copy(
            src_ref=src, dst_ref=self.outs[t].at[slot], send_sem=self.send.at[t, sem], recv_sem=self.recv.at[t, sem],
            device_id=peer, device_id_type=MESH)

    def _own(self, t):
        if self.kinds[t] == CHIP:
            kept = pl.ds(N_CHIPS_AWAY, PART_SLOTS - N_CHIPS_AWAY)
            return pltpu.make_async_copy(self.ins[t].at[kept], self.outs[t].at[kept], self.local.at[t])
        src = self.ins[t].at[self.me] if self.scatter[t] else self.ins[t]
        return pltpu.make_async_copy(src, self.outs[t].at[self.me], self.local.at[t])

    def _n_direct(self, t):
        return {SCATTER: N_DEV - 1, GATHER: N_CHIPS_AWAY + 1, PAIR: N_CHIPS_AWAY + 1, CHIP: N_CHIPS_AWAY}[self.kinds[t]]

    def _direct(self, t, k):
        kind = self.kinds[t]
        if kind == SCATTER:
            peer, slot = _peer(k + 1)
            return self._remote(t, k, self.ins[t].at[slot], self.me, peer)
        if kind == GATHER:
            return self._remote(t, k, self.ins[t], self.me, _peer(2 * k if k else 1)[0])
        if kind == PAIR:
            block = _peer(2 * (k + 1) + 1 if k < N_CHIPS_AWAY else 1)[1]
            return self._remote(t, k, self.ins[t].at[block], k, _peer(1)[0])
        return self._remote(t, k, self.ins[t].at[k], k, _peer(2 * (k + 1))[0])

    def _has_own(self, t):
        return self.kinds[t] != PAIR

    def _relay(self, t, c):
        slot = _peer(2 * c)[1]
        return self._remote(t, N_CHIPS_AWAY + c, self.outs[t].at[slot], slot, _peer(1)[0])

    def start(self):
        for t in range(len(self.ins)):
            if self._has_own(t):
                self._own(t).start()
            for k in range(self._n_direct(t)):
                self._direct(t, k).start()

    def relay(self):
        for t in self._gathers():
            for c in range(1, N_CHIPS_AWAY + 1):
                self._direct(t, c).wait_recv()
                self._relay(t, c).start()

    def _gathers(self):
        return [t for t in range(len(self.ins)) if self.kinds[t] == GATHER]

    def finish(self):
        for t in range(len(self.ins)):
            if self._has_own(t):
                self._own(t).wait()
            for k in range(self._n_direct(t)):
                self._direct(t, k).wait_send()
                if self.kinds[t] != GATHER or k == 0:
                    self._direct(t, k).wait_recv()
        for t in self._gathers():
            for c in range(1, N_CHIPS_AWAY + 1):
                self._relay(t, c).wait()


def _exchange_shapes(arrays, kinds):
    shape = {GATHER: lambda a: (N_DEV,) + a.shape, SCATTER: lambda a: a.shape, CHIP: lambda a: a.shape,
             PAIR: lambda a: (N_CHIPS_AWAY + 1,) + a.shape[1:]}
    return [jax.ShapeDtypeStruct(shape[k](a), a.dtype) for a, k in zip(arrays, kinds)]


def _exchange_sems(n):
    return [pltpu.SemaphoreType.DMA((n, N_DEV - 1)), pltpu.SemaphoreType.DMA((n, N_DEV - 1)),
            pltpu.SemaphoreType.DMA((n,))]


_ANY = pl.BlockSpec(memory_space=pl.ANY)


def _exchange(name, arrays, scatter):
    n = len(arrays)

    def body(*refs):
        plan = _ExchangePlan(refs[:n], refs[n:2 * n], *refs[2 * n:], scatter)
        plan.start()
        plan.relay()
        plan.finish()

    return pl.pallas_call(
        body, name=name, in_specs=[_ANY] * n, out_specs=[_ANY] * n, out_shape=_exchange_shapes(arrays, scatter),
        scratch_shapes=_exchange_sems(n),
    )(*arrays)


def _pallas(body, rider, *, name, grid, in_specs, out_specs, out_shape, scratch_shapes=(), compiler_params=None):
    if rider is None:
        return pl.pallas_call(body, name=name, grid=grid, in_specs=in_specs, out_specs=out_specs, out_shape=out_shape,
                              scratch_shapes=list(scratch_shapes), compiler_params=compiler_params)
    arrays, scatter = rider
    n, ni, no, ns = len(arrays), len(in_specs), len(out_specs), len(scratch_shapes)

    def carried(*refs):
        ins, r_in = refs[:ni], refs[ni:ni + n]
        outs, r_out = refs[ni + n:ni + n + no], refs[ni + n + no:ni + 2 * n + no]
        scratch, sems = refs[ni + 2 * n + no:ni + 2 * n + no + ns], refs[ni + 2 * n + no + ns:]
        step = 0
        for ax, g in enumerate(grid):
            step = step * g + pl.program_id(ax)
        steps = functools.reduce(lambda a, b: a * b, grid)

        @pl.when(step == 0)
        def _():
            _ExchangePlan(r_in, r_out, *sems, scatter).start()

        @pl.when(step == steps // 2)
        def _():
            _ExchangePlan(r_in, r_out, *sems, scatter).relay()

        body(*ins, *outs, *scratch)

        @pl.when(step == steps - 1)
        def _():
            _ExchangePlan(r_in, r_out, *sems, scatter).finish()

    call = pl.pallas_call(
        carried, name=name, grid=grid, in_specs=list(in_specs) + [_ANY] * n, out_specs=list(out_specs) + [_ANY] * n,
        out_shape=list(out_shape) + _exchange_shapes(arrays, scatter),
        scratch_shapes=list(scratch_shapes) + _exchange_sems(n), compiler_params=compiler_params)
    return lambda *args: call(*args, *arrays)


def _adamw_math(g, w, m, v):
    m_new = ADAM_B1 * m + (1.0 - ADAM_B1) * g
    v_new = ADAM_B2 * v + (1.0 - ADAM_B2) * (g * g)
    m_hat = m_new / (1.0 - ADAM_B1 ** ADAM_STEP)
    v_hat = v_new / (1.0 - ADAM_B2 ** ADAM_STEP)
    delta = -ADAM_LR * (m_hat / (jnp.sqrt(v_hat) + ADAM_EPS) + ADAM_WD * w)
    return delta, m_new, v_new


def _adamw(name, parts, w, m, v, rider=None):
    R, C = w.shape
    slots = parts.shape[0]
    tr = R
    for cand in (256, 128, 64, 32, 16, 8):
        if R % cand == 0:
            tr = cand
            break

    def body(p_ref, w_ref, m_ref, v_ref, g_ref, d_ref, nm_ref, nv_ref):
        g = p_ref[0].astype(F32)
        for j in range(1, slots):
            g = g + p_ref[j].astype(F32)
        g_ref[...] = g
        d_ref[...], nm_ref[...], nv_ref[...] = _adamw_math(g, w_ref[...], m_ref[...], v_ref[...])

    row = lambda i: (i, 0)
    spec = pl.BlockSpec((tr, C), row)
    return _pallas(
        body, rider, name=name, grid=(R // tr,),
        in_specs=[pl.BlockSpec((slots, tr, C), lambda i: (0, i, 0)), spec, spec, spec],
        out_specs=[spec] * 4,
        out_shape=[jax.ShapeDtypeStruct((R, C), F32)] * 4,
        compiler_params=_params("arbitrary"),
    )(parts, w, m, v)


def _rows128(a):
    flat = a.reshape(-1, LANES)
    pad = (-flat.shape[0]) % SMALL_ROWS_ALIGN
    return jnp.pad(flat, ((0, pad), (0, 0))) if pad else flat


def _unrows(packed, like):
    n = like.size // LANES
    return packed[:n].reshape(like.shape)


def kernel(x, p, ffn1_norm, ffn1_w_in, ffn1_w_out, mix_norm, w_mix_in, gmlp_v_norm, gmlp_w_s, gmlp_b, w_mix_out, ffn2_norm, ffn2_w_in, ffn2_w_out, ple_norm, ple_w_gate, ple_w_proj, final_norm, loss_target, m_ffn1_norm, m_ffn1_w_in, m_ffn1_w_out, m_mix_norm, m_w_mix_in, m_gmlp_v_norm, m_gmlp_w_s, m_gmlp_b, m_w_mix_out, m_ffn2_norm, m_ffn2_w_in, m_ffn2_w_out, m_ple_norm, m_ple_w_gate, m_ple_w_proj, m_final_norm, v_ffn1_norm, v_ffn1_w_in, v_ffn1_w_out, v_mix_norm, v_w_mix_in, v_gmlp_v_norm, v_gmlp_w_s, v_gmlp_b, v_w_mix_out, v_ffn2_norm, v_ffn2_w_in, v_ffn2_w_out, v_ple_norm, v_ple_w_gate, v_ple_w_proj, v_final_norm):
    names = ["ffn1_norm", "ffn1_w_in", "ffn1_w_out", "mix_norm", "w_mix_in", "gmlp_v_norm", "gmlp_w_s", "gmlp_b",
             "w_mix_out", "ffn2_norm", "ffn2_w_in", "ffn2_w_out", "ple_norm", "ple_w_gate", "ple_w_proj", "final_norm"]
    W = dict(zip(names, [ffn1_norm, ffn1_w_in, ffn1_w_out, mix_norm, w_mix_in, gmlp_v_norm, gmlp_w_s, gmlp_b,
                         w_mix_out, ffn2_norm, ffn2_w_in, ffn2_w_out, ple_norm, ple_w_gate, ple_w_proj, final_norm]))
    M = dict(zip(names, [m_ffn1_norm, m_ffn1_w_in, m_ffn1_w_out, m_mix_norm, m_w_mix_in, m_gmlp_v_norm, m_gmlp_w_s,
                         m_gmlp_b, m_w_mix_out, m_ffn2_norm, m_ffn2_w_in, m_ffn2_w_out, m_ple_norm, m_ple_w_gate,
                         m_ple_w_proj, m_final_norm]))
    V = dict(zip(names, [v_ffn1_norm, v_ffn1_w_in, v_ffn1_w_out, v_mix_norm, v_w_mix_in, v_gmlp_v_norm, v_gmlp_w_s,
                         v_gmlp_b, v_w_mix_out, v_ffn2_norm, v_ffn2_w_in, v_ffn2_w_out, v_ple_norm, v_ple_w_gate,
                         v_ple_w_proj, v_final_norm]))
    small = [n for n in names if n not in BIG_W]
    D = x.shape[-1]

    def pack(src, last):
        return jnp.concatenate([_rows128(src[n]) for n in small] + [last], axis=0)

    offs = [0]
    for n in small:
        offs.append(offs[-1] + _rows128(W[n]).shape[0])

    traffic = _Traffic({n: W[n][0].astype(BF16) for n in BIG_W})
    dx, small_mine, last_rider = _local_step(
        traffic, x[0], p[0, 0], loss_target[0],
        W["ffn1_norm"], W["mix_norm"], W["gmlp_v_norm"], W["gmlp_w_s"][0], jnp.transpose(W["gmlp_b"][0]),
        W["ffn2_norm"], W["ple_norm"], W["final_norm"].reshape(1, D),
        lambda grads, loss_part: pack(grads, jnp.broadcast_to(loss_part, (SMALL_ROWS_ALIGN, LANES))))

    out = {}
    parts = traffic.parts
    carrier = "ffn2_w_out"
    *out[carrier], got_bottom, small_parts = _adamw(
        "adamw_" + carrier, parts[carrier], W[carrier][0], M[carrier][0], V[carrier][0],
        rider=(last_rider[0] + [small_mine], last_rider[1] + [GATHER]))
    traffic.chipped("ffn1_w_in/1", got_bottom)
    parts["ffn1_w_in"] = jnp.concatenate([parts["ffn1_w_in/0"], parts["ffn1_w_in/1"]], axis=1)
    for n in BIG_W:
        if n != carrier:
            out[n] = _adamw("adamw_" + n, parts[n], W[n][0], M[n][0], V[n][0])
    zeros = jnp.zeros((SMALL_ROWS_ALIGN, LANES), F32)
    sg, sd, sm, sv = _adamw("adamw_small", small_parts, pack(W, zeros), pack(M, zeros), pack(V, zeros))
    for k, n in enumerate(small):
        out[n] = tuple(_unrows(arr[offs[k]:offs[k + 1]], W[n]) for arr in (sg, sd, sm, sv))
    loss = sg[offs[len(small)], 0]

    res = [loss, dx[None]]
    for k in range(4):
        res += [out[n][k].reshape(W[n].shape) for n in names]
    return tuple(res)
```

```python
import functools

import jax
import jax.numpy as jnp
from jax import lax
from jax.experimental import pallas as pl
from jax.experimental.pallas import tpu as pltpu

F32 = jnp.float32
BF16 = jnp.bfloat16
MESH = pl.DeviceIdType.MESH

N_DEV = 8
N_CHIPS_AWAY = 3
PART_SLOTS = N_CHIPS_AWAY + 2
EPS = 1e-6
ADAM_LR = 0.001
ADAM_B1 = 0.9
ADAM_B2 = 0.999
ADAM_EPS = 1e-08
ADAM_WD = 0.01
ADAM_STEP = 10

GM_WIDTH = 512
GM_HEADS = 4
CHUNK = 128
SB_WIDTH = 512
SB_HEAD_DIM = 64
SB_SCALE = 0.125
LANES = 128
SMALL_ROWS_ALIGN = 8

ROW_TILE = 512
GRAD_ROW_TILE = 2048
DW_IN_ROW_TILE = 4096
FFN_FWD_ROW_TILE = 1024
ATTN_Q_ROWS = 512
ATTN_BWD_Q_ROWS = 512
ATTN_KEY_BLOCK = 256
ATTN_UNROLL = 2
VMEM_LIMIT = 56 * 1024 * 1024


def _params(*sem):
    return pltpu.CompilerParams(dimension_semantics=sem, vmem_limit_bytes=VMEM_LIMIT)


def _dot(a, b):
    return jnp.dot(a, b, preferred_element_type=F32)


def _dot_nt(a, b):
    return lax.dot_general(a, b, (((1,), (1,)), ((), ())), preferred_element_type=F32)


def _dot_tn(a, b):
    return lax.dot_general(a, b, (((0,), (0,)), ((), ())), preferred_element_type=F32)


def _rms_parts(x):
    r = lax.rsqrt(jnp.mean(x * x, axis=-1, keepdims=True) + EPS)
    return r, x * r


def _rms_bwd(x, g, dy):
    r, xh = _rms_parts(x)
    dyg = dy * g
    dx = r * (dyg - xh * jnp.mean(dyg * xh, axis=-1, keepdims=True))
    return dx, jnp.sum(dy * xh, axis=0, keepdims=True)


def _sigmoid(x):
    return 1.0 / (1.0 + jnp.exp(-x))


_SQRT_HALF = 0.7071067811865476
_INV_SQRT_2PI = 0.3989422804014327


def _gelu(x):
    return 0.5 * x * (1.0 + lax.erf(x * _SQRT_HALF))


def _gelu_grad(x):
    return 0.5 * (1.0 + lax.erf(x * _SQRT_HALF)) + x * (_INV_SQRT_2PI * jnp.exp(-0.5 * x * x))


def _split_bf16(x):
    hi = x.astype(BF16)
    lo = (x - hi.astype(F32)).astype(BF16)
    return hi, lo


def _ffn_fwd(name, h, gain, w_in, w_out, rider=None):
    T, D = h.shape
    nb, _, FB = w_in.shape
    nh = nb // 2
    tm = min(FFN_FWD_ROW_TILE, T)

    def body(h_ref, g_ref, wg_ref, wu_ref, wo_ref, ho_ref, n_ref, G_ref, U_ref, a_ref, n_s, acc):
        jj = pl.program_id(1)

        @pl.when(jj == 0)
        def _():
            _, xh = _rms_parts(h_ref[...])
            n = (xh * g_ref[...]).astype(BF16)
            n_s[...] = n
            n_ref[...] = n
            acc[...] = jnp.zeros_like(acc)

        n = n_s[...]
        G = _dot(n, wg_ref[0])
        U = _dot(n, wu_ref[0])
        G_ref[0] = G.astype(BF16)
        U_ref[0] = U.astype(BF16)
        a = (G * _sigmoid(G) * U).astype(BF16)
        a_ref[0] = a
        acc[...] += _dot(a, wo_ref[...])

        @pl.when(jj == nh - 1)
        def _():
            ho_ref[...] = h_ref[...] + 0.5 * acc[...]

    row = lambda i, j: (i, 0)
    blk = lambda i, j: (j, i, 0)
    return _pallas(
        body, rider, name=name, grid=(T // tm, nh),
        in_specs=[pl.BlockSpec((tm, D), row),
                  pl.BlockSpec((1, D), lambda i, j: (0, 0)),
                  pl.BlockSpec((1, D, FB), lambda i, j: (j, 0, 0)),
                  pl.BlockSpec((1, D, FB), lambda i, j: (j + nh, 0, 0)),
                  pl.BlockSpec((FB, D), lambda i, j: (j, 0))],
        out_specs=[pl.BlockSpec((tm, D), row), pl.BlockSpec((tm, D), row),
                   pl.BlockSpec((1, tm, FB), blk), pl.BlockSpec((1, tm, FB), blk),
                   pl.BlockSpec((1, tm, FB), blk)],
        out_shape=[jax.ShapeDtypeStruct((T, D), F32), jax.ShapeDtypeStruct((T, D), BF16),
                   jax.ShapeDtypeStruct((nh, T, FB), BF16), jax.ShapeDtypeStruct((nh, T, FB), BF16),
                   jax.ShapeDtypeStruct((nh, T, FB), BF16)],
        scratch_shapes=[pltpu.VMEM((tm, D), BF16), pltpu.VMEM((tm, D), F32)],
        compiler_params=_params("arbitrary", "arbitrary"),
    )(h, gain, w_in, w_in, w_out)


def _ffn_bwd(name, dh, h_in, gain, G, U, w_in, w_out, rider=None):
    T, D = dh.shape
    nb, _, FB = w_in.shape
    nh = nb // 2
    tm = min(ROW_TILE, T)

    def body(dh_ref, h_ref, g_ref, G_ref, U_ref, wg_ref, wu_ref, wo_ref,
             dhin_ref, dg_ref, dG_ref, dU_ref, do_ref, dn_acc, do_s):
        i = pl.program_id(0)
        jj = pl.program_id(1)

        @pl.when(jj == 0)
        def _():
            d_out = (0.5 * dh_ref[...]).astype(BF16)
            do_s[...] = d_out
            do_ref[...] = d_out
            dn_acc[...] = jnp.zeros_like(dn_acc)

        @pl.when((i == 0) & (jj == 0))
        def _():
            dg_ref[...] = jnp.zeros_like(dg_ref)

        halves = [slice(0, tm // 2), slice(tm // 2, tm)]
        da = [_dot_nt(do_s[rows, :], wo_ref[...]) for rows in halves]
        dGU = []
        for rows, dav in zip(halves, da):
            dG, dU = _gate_grads(dav, G_ref[0, rows, :].astype(F32), U_ref[0, rows, :].astype(F32))
            dG_ref[0, rows, :] = dG
            dU_ref[0, rows, :] = dU
            dGU.append((dG, dU))
        dn = [_dot_nt(dG, wg_ref[0]) for dG, _ in dGU]
        dn = [d + _dot_nt(dU, wu_ref[0]) for d, (_, dU) in zip(dn, dGU)]
        for rows, d in zip(halves, dn):
            dn_acc[rows, :] += d

        @pl.when(jj == nh - 1)
        def _():
            dx, dg = _rms_bwd(h_ref[...], g_ref[...], dn_acc[...])
            dhin_ref[...] = dh_ref[...] + dx
            dg_ref[...] += dg

    row = lambda i, j: (i, 0)
    blk = lambda i, j: (j, i, 0)
    one = lambda i, j: (0, 0)
    return _pallas(
        body, rider, name=name, grid=(T // tm, nh),
        in_specs=[pl.BlockSpec((tm, D), row), pl.BlockSpec((tm, D), row), pl.BlockSpec((1, D), one),
                  pl.BlockSpec((1, tm, FB), blk), pl.BlockSpec((1, tm, FB), blk),
                  pl.BlockSpec((1, D, FB), lambda i, j: (j, 0, 0)),
                  pl.BlockSpec((1, D, FB), lambda i, j: (j + nh, 0, 0)),
                  pl.BlockSpec((FB, D), lambda i, j: (j, 0))],
        out_specs=[pl.BlockSpec((tm, D), row), pl.BlockSpec((1, D), one),
                   pl.BlockSpec((1, tm, FB), blk), pl.BlockSpec((1, tm, FB), blk),
                   pl.BlockSpec((tm, D), row)],
        out_shape=[jax.ShapeDtypeStruct((T, D), F32), jax.ShapeDtypeStruct((1, D), F32),
                   jax.ShapeDtypeStruct((nh, T, FB), BF16), jax.ShapeDtypeStruct((nh, T, FB), BF16),
                   jax.ShapeDtypeStruct((T, D), BF16)],
        scratch_shapes=[pltpu.VMEM((tm, D), F32), pltpu.VMEM((tm, D), BF16)],
        compiler_params=_params("arbitrary", "arbitrary"),
    )(dh, h_in, gain, G, U, w_in, w_in, w_out)


def _gate_grads(dav, Gv, Uv):
    sig = _sigmoid(Gv)
    return (dav * Uv * (sig * (1.0 + Gv * (1.0 - sig)))).astype(BF16), (dav * (Gv * sig)).astype(BF16)


def _ffn_bwd_gates(name, d_out, G, U, w_out, rider=None):
    T, D = d_out.shape
    nh, _, FB = G.shape
    tm = min(FFN_FWD_ROW_TILE, T)

    def body(do_ref, G_ref, U_ref, wo_ref, dG_ref, dU_ref):
        halves = [slice(0, tm // 2), slice(tm // 2, tm)]
        da = [_dot_nt(do_ref[rows, :], wo_ref[...]) for rows in halves]
        for rows, dav in zip(halves, da):
            dG_ref[0, rows, :], dU_ref[0, rows, :] = _gate_grads(
                dav, G_ref[0, rows, :].astype(F32), U_ref[0, rows, :].astype(F32))

    blk = lambda i, j: (j, i, 0)
    return _pallas(
        body, rider, name=name, grid=(T // tm, nh),
        in_specs=[pl.BlockSpec((tm, D), lambda i, j: (i, 0)), pl.BlockSpec((1, tm, FB), blk),
                  pl.BlockSpec((1, tm, FB), blk), pl.BlockSpec((FB, D), lambda i, j: (j, 0))],
        out_specs=[pl.BlockSpec((1, tm, FB), blk), pl.BlockSpec((1, tm, FB), blk)],
        out_shape=[jax.ShapeDtypeStruct((nh, T, FB), BF16), jax.ShapeDtypeStruct((nh, T, FB), BF16)],
        compiler_params=_params("arbitrary", "arbitrary"),
    )(d_out, G, U, w_out)


def _ffn_bwd_input(name, dh, h_in, gain, dG, dU, w_in, rider=None):
    T, D = dh.shape
    nb, _, FB = w_in.shape
    nh = nb // 2
    tm = min(FFN_FWD_ROW_TILE, T)

    def body(dh_ref, h_ref, g_ref, dG_ref, dU_ref, wg_ref, wu_ref, dhin_ref, dg_ref, dn_acc):
        i = pl.program_id(0)
        jj = pl.program_id(1)

        @pl.when(jj == 0)
        def _():
            dn_acc[...] = jnp.zeros_like(dn_acc)

        @pl.when((i == 0) & (jj == 0))
        def _():
            dg_ref[...] = jnp.zeros_like(dg_ref)

        halves = [slice(0, tm // 2), slice(tm // 2, tm)]
        dn = [_dot_nt(dG_ref[0, rows, :], wg_ref[0]) for rows in halves]
        dn = [d + _dot_nt(dU_ref[0, rows, :], wu_ref[0]) for d, rows in zip(dn, halves)]
        for rows, d in zip(halves, dn):
            dn_acc[rows, :] += d

        @pl.when(jj == nh - 1)
        def _():
            dx, dg = _rms_bwd(h_ref[...], g_ref[...], dn_acc[...])
            dhin_ref[...] = dh_ref[...] + dx
            dg_ref[...] += dg

    row = lambda i, j: (i, 0)
    blk = lambda i, j: (j, i, 0)
    one = lambda i, j: (0, 0)
    return _pallas(
        body, rider, name=name, grid=(T // tm, nh),
        in_specs=[pl.BlockSpec((tm, D), row), pl.BlockSpec((tm, D), row), pl.BlockSpec((1, D), one),
                  pl.BlockSpec((1, tm, FB), blk), pl.BlockSpec((1, tm, FB), blk),
                  pl.BlockSpec((1, D, FB), lambda i, j: (j, 0, 0)),
                  pl.BlockSpec((1, D, FB), lambda i, j: (j + nh, 0, 0))],
        out_specs=[pl.BlockSpec((tm, D), row), pl.BlockSpec((1, D), one)],
        out_shape=[jax.ShapeDtypeStruct((T, D), F32), jax.ShapeDtypeStruct((1, D), F32)],
        scratch_shapes=[pltpu.VMEM((tm, D), F32)],
        compiler_params=_params("arbitrary", "arbitrary"),
    )(dh, h_in, gain, dG, dU, w_in, w_in)


def _matmul_tn(name, a, b, nj, a_block, a_map, b_block, b_map, out_shape, out_block, out_map, rider=None):
    T = a.shape[-2]
    tt = a_block[-2]
    nt = T // tt
    kb, nbk = out_block[-2], out_block[-1]

    def body(a_ref, b_ref, o_ref, acc):
        t = pl.program_id(1)

        @pl.when(t == 0)
        def _():
            acc[...] = jnp.zeros_like(acc)

        av = (a_ref[0] if len(a_block) == 3 else a_ref[...]).astype(BF16)
        bv = b_ref[0] if len(b_block) == 3 else b_ref[...]
        acc[...] += _dot_tn(av, bv)

        @pl.when(t == nt - 1)
        def _():
            if len(out_block) == 3:
                o_ref[0] = acc[...].astype(o_ref.dtype)
            else:
                o_ref[...] = acc[...].astype(o_ref.dtype)

    got = _pallas(
        body, rider, name=name, grid=(nj, nt),
        in_specs=[pl.BlockSpec(a_block, a_map), pl.BlockSpec(b_block, b_map)],
        out_specs=[pl.BlockSpec(out_block, out_map)],
        out_shape=[jax.ShapeDtypeStruct(out_shape, BF16)],
        scratch_shapes=[pltpu.VMEM((kb, nbk), F32)],
        compiler_params=_params("arbitrary", "arbitrary"),
    )(a, b)
    return got[0] if rider is None else got


def _dw_in(name, n, dG, dU, rider=None):
    T, kr = n.shape
    nh, _, FB = dG.shape
    tt = min(DW_IN_ROW_TILE, T)
    nt = T // tt
    cut = LANES * ((kr // LANES + 1) // 2)

    def body(n_ref, dg_ref, du_ref, o_ref, acc):
        j = pl.program_id(0)
        t = pl.program_id(1)

        @pl.when(t == 0)
        def _():
            acc[...] = jnp.zeros_like(acc)

        def add(dz_ref):
            for rows in ((slice(0, cut), slice(cut, kr)) if cut < kr else (slice(0, kr),)):
                acc[rows, :] += _dot_tn(n_ref[:, rows], dz_ref[0])

        @pl.when(j < nh)
        def _():
            add(dg_ref)

        @pl.when(j >= nh)
        def _():
            add(du_ref)

        @pl.when(t == nt - 1)
        def _():
            o_ref[0] = acc[...].astype(BF16)

    return _pallas(
        body, rider, name=name, grid=(2 * nh, nt),
        in_specs=[pl.BlockSpec((tt, kr), lambda j, t: (t, 0)),
                  pl.BlockSpec((1, tt, FB), lambda j, t: (jnp.minimum(j, nh - 1), t, 0)),
                  pl.BlockSpec((1, tt, FB), lambda j, t: (jnp.maximum(j - nh, 0), t, 0))],
        out_specs=[pl.BlockSpec((1, kr, FB), lambda j, t: (j, 0, 0))],
        out_shape=[jax.ShapeDtypeStruct((2 * nh, kr, FB), BF16)],
        scratch_shapes=[pltpu.VMEM((kr, FB), F32)],
        compiler_params=_params("arbitrary", "arbitrary"),
    )(n, dG, dU)


def _mix_in_fwd(h, gain, w):
    T, D = h.shape
    W = w.shape[1]
    nuv = 2 * GM_WIDTH
    tm = min(ROW_TILE, T)

    def body(h_ref, g_ref, w_ref, n_ref, zuv_ref, qkv_ref):
        _, xh = _rms_parts(h_ref[...])
        n = (xh * g_ref[...]).astype(BF16)
        n_ref[...] = n
        z = _dot(n, w_ref[...])
        zuv_ref[...] = z[:, :nuv]
        qkv_ref[...] = z[:, nuv:].astype(BF16)

    row = lambda i: (i, 0)
    return pl.pallas_call(
        body, name="mix_in_fwd", grid=(T // tm,),
        in_specs=[pl.BlockSpec((tm, D), row), pl.BlockSpec((1, D), lambda i: (0, 0)),
                  pl.BlockSpec((D, W), lambda i: (0, 0))],
        out_specs=[pl.BlockSpec((tm, D), row), pl.BlockSpec((tm, nuv), row),
                   pl.BlockSpec((tm, W - nuv), row)],
        out_shape=[jax.ShapeDtypeStruct((T, D), BF16), jax.ShapeDtypeStruct((T, nuv), F32),
                   jax.ShapeDtypeStruct((T, W - nuv), BF16)],
        compiler_params=_params("arbitrary"),
    )(h, gain, w)


def _mix_in_bwd(dzuv, dqkv, w, h, gain, dh):
    T, D = h.shape
    W = w.shape[1]
    nuv = dzuv.shape[1]
    tm = min(ROW_TILE, T)

    def body(dzuv_ref, dqkv_ref, w_ref, h_ref, g_ref, dh_ref, dhin_ref, dg_ref, half_ref):
        @pl.when(pl.program_id(0) == 0)
        def _():
            dg_ref[...] = jnp.zeros_like(dg_ref)

        dn = _dot_nt(dzuv_ref[...], w_ref[:, :nuv]) + _dot_nt(dqkv_ref[...], w_ref[:, nuv:])
        dx, dg = _rms_bwd(h_ref[...], g_ref[...], dn)
        dh_in = dh_ref[...] + dx
        dhin_ref[...] = dh_in
        half_ref[...] = (0.5 * dh_in).astype(BF16)
        dg_ref[...] += dg

    row = lambda i: (i, 0)
    one = lambda i: (0, 0)
    return pl.pallas_call(
        body, name="mix_in_bwd", grid=(T // tm,),
        in_specs=[pl.BlockSpec((tm, nuv), row), pl.BlockSpec((tm, W - nuv), row),
                  pl.BlockSpec((D, W), one), pl.BlockSpec((tm, D), row), pl.BlockSpec((1, D), one),
                  pl.BlockSpec((tm, D), row)],
        out_specs=[pl.BlockSpec((tm, D), row), pl.BlockSpec((1, D), one), pl.BlockSpec((tm, D), row)],
        out_shape=[jax.ShapeDtypeStruct((T, D), F32), jax.ShapeDtypeStruct((1, D), F32),
                   jax.ShapeDtypeStruct((T, D), BF16)],
        compiler_params=_params("arbitrary"),
    )(dzuv, dqkv, w, h, gain, dh)


def _gmlp_norm(zv, gv):
    v = _gelu(zv)
    r, vh = _rms_parts(v)
    return r, vh, (vh * gv).astype(BF16)


def _causal_ws(ws_ref, hd):
    r = lax.broadcasted_iota(jnp.int32, (CHUNK, CHUNK), 0)
    c = lax.broadcasted_iota(jnp.int32, (CHUNK, CHUNK), 1)
    return jnp.where(r >= c, ws_ref[hd], 0.0).astype(BF16)


def _gmlp_fwd(zuv, gv, ws, b_t):
    T = zuv.shape[0]
    tg = min(ROW_TILE, T)

    def body(zu_ref, zv_ref, gv_ref, ws_ref, bt_ref, o_ref):
        u = _gelu(zu_ref[...])
        _, _, vn = _gmlp_norm(zv_ref[...], gv_ref[...])
        for hd in range(GM_HEADS):
            wc = _causal_ws(ws_ref, hd)
            cols = slice(hd * CHUNK, (hd + 1) * CHUNK)
            for c in range(tg // CHUNK):
                rows = slice(c * CHUNK, (c + 1) * CHUNK)
                sv = _dot(wc, vn[rows, cols]) + bt_ref[:, hd:hd + 1]
                o_ref[rows, cols] = (u[rows, cols] * sv).astype(BF16)

    return pl.pallas_call(
        body, name="gmlp_fwd", grid=(T // tg,),
        in_specs=[pl.BlockSpec((tg, GM_WIDTH), lambda i: (i, 0)), pl.BlockSpec((tg, GM_WIDTH), lambda i: (i, 1)),
                  pl.BlockSpec((1, GM_WIDTH), lambda i: (0, 0)),
                  pl.BlockSpec((GM_HEADS, CHUNK, CHUNK), lambda i: (0, 0, 0)),
                  pl.BlockSpec((CHUNK, GM_HEADS), lambda i: (0, 0))],
        out_specs=pl.BlockSpec((tg, GM_WIDTH), lambda i: (i, 0)),
        out_shape=jax.ShapeDtypeStruct((T, GM_WIDTH), BF16),
        compiler_params=_params("arbitrary"),
    )(zuv, zuv, gv, ws, b_t)


def _gmlp_bwd(zuv, d_gm, gv, ws, b_t):
    T = zuv.shape[0]
    tg = min(ROW_TILE, T)
    ng = T // tg

    def body(zu_ref, zv_ref, dgm_ref, gv_ref, ws_ref, bt_ref, dz_ref, dgv_ref, dws_ref, dbt_ref, dsv_acc, dvn_s):
        i = pl.program_id(0)

        @pl.when(i == 0)
        def _():
            dgv_ref[...] = jnp.zeros_like(dgv_ref)
            dws_ref[...] = jnp.zeros_like(dws_ref)
            dsv_acc[...] = jnp.zeros_like(dsv_acc)

        zu = zu_ref[...]
        zv = zv_ref[...]
        dgm = dgm_ref[...]
        gvv = gv_ref[...]
        u = _gelu(zu)
        rv, vh, vn = _gmlp_norm(zv, gvv)
        dsv = dgm * u
        dsv_b = dsv.astype(BF16)
        for hd in range(GM_HEADS):
            wc = _causal_ws(ws_ref, hd)
            cols = slice(hd * CHUNK, (hd + 1) * CHUNK)
            dws = jnp.zeros((CHUNK, CHUNK), F32)
            dsv_sum = jnp.zeros((CHUNK, CHUNK), F32)
            for c in range(tg // CHUNK):
                rows = slice(c * CHUNK, (c + 1) * CHUNK)
                vch = vn[rows, cols]
                sv = _dot(wc, vch) + bt_ref[:, hd:hd + 1]
                dz_ref[rows, cols] = (dgm[rows, cols] * sv * _gelu_grad(zu[rows, cols])).astype(BF16)
                dws += _dot_nt(dsv_b[rows, cols], vch)
                dsv_sum += dsv[rows, cols]
                dvn_s[rows, cols] = _dot_tn(wc, dsv_b[rows, cols])
            dws_ref[hd] += dws
            dsv_acc[:, cols] += dsv_sum
        dvn = dvn_s[...]
        dvh = dvn * gvv
        dv = rv * (dvh - vh * jnp.mean(dvh * vh, axis=-1, keepdims=True))
        dgv_ref[...] += jnp.sum(dvn * vh, axis=0, keepdims=True)
        dz_ref[:, GM_WIDTH:] = (dv * _gelu_grad(zv)).astype(BF16)

        @pl.when(i == ng - 1)
        def _():
            r = lax.broadcasted_iota(jnp.int32, (CHUNK, CHUNK), 0)
            c = lax.broadcasted_iota(jnp.int32, (CHUNK, CHUNK), 1)
            for hd in range(GM_HEADS):
                dws_ref[hd] = jnp.where(r >= c, dws_ref[hd], 0.0)
                dbt_ref[:, hd:hd + 1] = jnp.sum(dsv_acc[:, hd * CHUNK:(hd + 1) * CHUNK], axis=1, keepdims=True)

    return pl.pallas_call(
        body, name="gmlp_bwd", grid=(ng,),
        in_specs=[pl.BlockSpec((tg, GM_WIDTH), lambda i: (i, 0)), pl.BlockSpec((tg, GM_WIDTH), lambda i: (i, 1)),
                  pl.BlockSpec((tg, GM_WIDTH), lambda i: (i, 0)),
                  pl.BlockSpec((1, GM_WIDTH), lambda i: (0, 0)),
                  pl.BlockSpec((GM_HEADS, CHUNK, CHUNK), lambda i: (0, 0, 0)),
                  pl.BlockSpec((CHUNK, GM_HEADS), lambda i: (0, 0))],
        out_specs=[pl.BlockSpec((tg, 2 * GM_WIDTH), lambda i: (i, 0)),
                   pl.BlockSpec((1, GM_WIDTH), lambda i: (0, 0)),
                   pl.BlockSpec((GM_HEADS, CHUNK, CHUNK), lambda i: (0, 0, 0)),
                   pl.BlockSpec((CHUNK, GM_HEADS), lambda i: (0, 0))],
        out_shape=[jax.ShapeDtypeStruct((T, 2 * GM_WIDTH), BF16), jax.ShapeDtypeStruct((1, GM_WIDTH), F32),
                   jax.ShapeDtypeStruct((GM_HEADS, CHUNK, CHUNK), F32),
                   jax.ShapeDtypeStruct((CHUNK, GM_HEADS), F32)],
        scratch_shapes=[pltpu.VMEM((CHUNK, GM_WIDTH), F32), pltpu.VMEM((tg, GM_WIDTH), F32)],
        compiler_params=_params("arbitrary"),
    )(zuv, zuv, d_gm, gv, ws, b_t)


def _scan_matrix(blk, keep):
    r = lax.broadcasted_iota(jnp.int32, (blk, blk), 0)
    c = lax.broadcasted_iota(jnp.int32, (blk, blk), 1)
    return jnp.where(keep(r, c), 1.0, 0.0).astype(BF16)


def _scan_matrix2(blk, keep, value):
    m = _scan_matrix(blk, keep) * value
    return jnp.concatenate([m, m], axis=0)


def _scan(x, mat2):
    hi, lo = _split_bf16(x)
    return _dot(jnp.concatenate([hi, lo], axis=1), mat2)


def _head_masks(q):
    lane = lax.broadcasted_iota(jnp.int32, q.shape, 1)
    m0 = lane < SB_HEAD_DIM
    zero = jnp.zeros_like(q)
    return m0, jnp.where(m0, q, zero), jnp.where(m0, zero, q)


_LOG2E = 1.4426950408889634


def _softplus_parts(z):
    e = jnp.exp2(jnp.abs(z) * (-_LOG2E))
    ope = 1.0 + e
    return e, ope, jnp.maximum(z, 0.0) + jnp.log(ope)


def _attn_fwd(qkv, rider=None):
    T = qkv.shape[0]
    tk = ATTN_KEY_BLOCK
    tq = min(ATTN_Q_ROWS, T)
    band = tq // tk
    assert band % ATTN_UNROLL == 0 or T == tq
    ngrp = SB_WIDTH // LANES

    def body(q_ref, k_ref, v_ref, o_ref, l_ref, acc, run):
        i = pl.program_id(1)
        suffix = _scan_matrix2(tk, lambda r, c: r >= c, -1.0)
        row = lax.broadcasted_iota(jnp.int32, (tq, tk), 0)
        col = lax.broadcasted_iota(jnp.int32, (tq, tk), 1)
        m0, q0, q1 = _head_masks(q_ref[...] * SB_SCALE)
        acc[...] = jnp.zeros_like(acc)
        run[...] = jnp.zeros_like(run)

        def tiles(work):
            heads = (q0, q1)
            kv = []
            for j, _ in work:
                start = pl.multiple_of(j * tk, tk)
                kv.append((k_ref[pl.ds(start, tk), :], v_ref[pl.ds(start, tk), :]))
            z = [[_dot_nt(qh, kj) for qh in heads] for kj, _ in kv]
            sp = [[_softplus_parts(zz)[2] for zz in zt] for zt in z]
            sp = [[s if m is None else jnp.where(m, s, 0.0) for s in st] for st, (_, m) in zip(sp, work)]
            res = [[_scan(s, suffix) for s in st] for st in sp]
            runs = [run[hd] for hd in range(len(heads))]
            a = []
            for t, (_, m) in enumerate(work):
                at = []
                for hd in range(len(heads)):
                    av = jnp.exp(z[t][hd] + (runs[hd] + res[t][hd]))
                    at.append(av if m is None else jnp.where(m, av, 0.0))
                    runs[hd] = runs[hd] + res[t][hd][:, 0:1]
                a.append(at)
            for hd in range(len(heads)):
                run[hd] = runs[hd]
                upd = _dot(a[0][hd].astype(BF16), kv[0][1])
                for t in range(1, len(work)):
                    upd = upd + _dot(a[t][hd].astype(BF16), kv[t][1])
                acc[hd] += upd

        tiles([(i * band + jb, jb * tk + col < row) for jb in reversed(range(band))])

        def full_step(it, carry):
            tiles([(i * band - 1 - ATTN_UNROLL * it - u, None) for u in range(ATTN_UNROLL)])
            return carry

        lax.fori_loop(0, i * (band // ATTN_UNROLL), full_step, 0)
        o_ref[...] = jnp.where(m0, acc[0], acc[1]).astype(BF16)
        l_ref[...] = jnp.where(m0, jnp.broadcast_to(run[0], (tq, LANES)), jnp.broadcast_to(run[1], (tq, LANES)))

    return _pallas(
        body, rider, name="attn_fwd", grid=(ngrp, T // tq),
        in_specs=[pl.BlockSpec((tq, LANES), lambda g, i: (i, g)),
                  pl.BlockSpec((T, LANES), lambda g, i: (0, ngrp + g)),
                  pl.BlockSpec((T, LANES), lambda g, i: (0, 2 * ngrp + g))],
        out_specs=[pl.BlockSpec((tq, LANES), lambda g, i: (i, g)),
                   pl.BlockSpec((tq, LANES), lambda g, i: (i, g))],
        out_shape=[jax.ShapeDtypeStruct((T, SB_WIDTH), BF16), jax.ShapeDtypeStruct((T, SB_WIDTH), F32)],
        scratch_shapes=[pltpu.VMEM((2, tq, LANES), F32), pltpu.VMEM((2, tq, 1), F32)],
        compiler_params=_params("arbitrary", "arbitrary"),
    )(qkv, qkv, qkv)


def _attn_bwd(qkv, d_o, ltot, rider=None):
    T = qkv.shape[0]
    tk = ATTN_KEY_BLOCK
    tq = min(ATTN_BWD_Q_ROWS, T)
    band = tq // tk
    nq = T // tq
    ngrp = SB_WIDTH // LANES

    def body(q_ref, k_ref, v_ref, do_ref, l_ref, dq_ref, dk_ref, dv_ref, dq_acc, dk_acc, dv_acc, lpre, ppre):
        i = pl.program_id(1)

        @pl.when(i == 0)
        def _():
            dk_acc[...] = jnp.zeros_like(dk_acc)
            dv_acc[...] = jnp.zeros_like(dv_acc)

        excl = _scan_matrix(tk, lambda r, c: r < c)
        excl2 = jnp.concatenate([excl, excl], axis=0)
        row = lax.broadcasted_iota(jnp.int32, (tq, tk), 0)
        col = lax.broadcasted_iota(jnp.int32, (tq, tk), 1)
        m0, q0, q1 = _head_masks(q_ref[...] * SB_SCALE)
        _, d0, d1 = _head_masks(do_ref[...].astype(BF16))
        lt = l_ref[...]
        ltots = (lt[:, 0:1], lt[:, SB_HEAD_DIM:SB_HEAD_DIM + 1])
        dq_acc[...] = jnp.zeros_like(dq_acc)
        lpre[...] = jnp.zeros_like(lpre)
        ppre[...] = jnp.zeros_like(ppre)

        def tiles(work):
            heads = ((q0, d0), (q1, d1))
            nhd = len(heads)
            starts = [pl.multiple_of(j * tk, tk) for j, _ in work]
            kv = [(k_ref[pl.ds(st, tk), :], v_ref[pl.ds(st, tk), :]) for st in starts]
            masks = [m for _, m in work]
            every = [(t, hd) for t in range(len(work)) for hd in range(nhd)]
            z = {(t, hd): _dot_nt(heads[hd][0], kv[t][0]) for t, hd in every}
            da = {(t, hd): _dot_nt(heads[hd][1], kv[t][1]) for t, hd in every}
            sp, beta = {}, {}
            for key in every:
                s = _softplus_parts(z[key])[2]
                beta[key] = jnp.exp(z[key] - s)
                sp[key] = s if masks[key[0]] is None else jnp.where(masks[key[0]], s, 0.0)
            res = {key: _scan(sp[key], excl2) for key in every}
            lp = [lpre[hd] for hd in range(nhd)]
            a, p = {}, {}
            for t, hd in every:
                av = jnp.exp(z[t, hd] + ((ltots[hd] + lp[hd]) + res[t, hd]))
                a[t, hd] = av if masks[t] is None else jnp.where(masks[t], av, 0.0)
                p[t, hd] = a[t, hd] * da[t, hd]
                lp[hd] = lp[hd] + (res[t, hd][:, tk - 1:tk] + sp[t, hd][:, tk - 1:tk])
            resp = {key: _dot(p[key].astype(BF16), excl) for key in every}
            pp = [ppre[hd] for hd in range(nhd)]
            dzb = {}
            for t, hd in every:
                dz = p[t, hd] - beta[t, hd] * (p[t, hd] + (pp[hd] + resp[t, hd]))
                if masks[t] is not None:
                    dz = jnp.where(masks[t], dz, 0.0)
                dzb[t, hd] = dz.astype(BF16)
                pp[hd] = pp[hd] + (resp[t, hd][:, tk - 1:tk] + p[t, hd][:, tk - 1:tk])
            for hd in range(nhd):
                lpre[hd] = lp[hd]
                ppre[hd] = pp[hd]
                upd = _dot(dzb[0, hd], kv[0][0])
                for t in range(1, len(work)):
                    upd = upd + _dot(dzb[t, hd], kv[t][0])
                dq_acc[hd] += upd
            for t, st in enumerate(starts):
                dk = _dot_tn(dzb[t, 0], heads[0][0])
                dv = _dot_tn(a[t, 0].astype(BF16), heads[0][1])
                for hd in range(1, nhd):
                    dk = dk + _dot_tn(dzb[t, hd], heads[hd][0])
                    dv = dv + _dot_tn(a[t, hd].astype(BF16), heads[hd][1])
                dk_acc[pl.ds(st, tk), :] += dk
                dv_acc[pl.ds(st, tk), :] += dv

        def full_step(j, carry):
            tiles([(j, None)])
            return carry

        lax.fori_loop(0, i * band, full_step, 0)
        for jb in range(band):
            tiles([(i * band + jb, jb * tk + col < row)])
        dq_ref[...] = (jnp.where(m0, dq_acc[0], dq_acc[1]) * SB_SCALE).astype(BF16)

        @pl.when(i == nq - 1)
        def _():
            dk_ref[...] = dk_acc[...].astype(BF16)
            dv_ref[...] = dv_acc[...].astype(BF16)

    qmap = lambda g, i: (i, g)
    return _pallas(
        body, rider, name="attn_bwd", grid=(ngrp, nq),
        in_specs=[pl.BlockSpec((tq, LANES), qmap),
                  pl.BlockSpec((T, LANES), lambda g, i: (0, ngrp + g)),
                  pl.BlockSpec((T, LANES), lambda g, i: (0, 2 * ngrp + g)),
                  pl.BlockSpec((tq, LANES), qmap), pl.BlockSpec((tq, LANES), qmap)],
        out_specs=[pl.BlockSpec((tq, LANES), qmap),
                   pl.BlockSpec((T, LANES), lambda g, i: (0, g)),
                   pl.BlockSpec((T, LANES), lambda g, i: (0, g))],
        out_shape=[jax.ShapeDtypeStruct((T, SB_WIDTH), BF16)] * 3,
        scratch_shapes=[pltpu.VMEM((2, tq, LANES), F32), pltpu.VMEM((T, LANES), F32),
                        pltpu.VMEM((T, LANES), F32), pltpu.VMEM((2, tq, 1), F32),
                        pltpu.VMEM((2, tq, 1), F32)],
        compiler_params=_params("arbitrary", "arbitrary"),
    )(qkv, qkv, qkv, d_o, ltot)


def _mix_out_fwd(h, gm, sb, w):
    T, D = h.shape
    tm = min(ROW_TILE, T)

    def body(h_ref, gm_ref, sb_ref, w_ref, o_ref):
        o_ref[...] = h_ref[...] + _dot(gm_ref[...], w_ref[:GM_WIDTH, :]) + _dot(sb_ref[...], w_ref[GM_WIDTH:, :])

    row = lambda i: (i, 0)
    return pl.pallas_call(
        body, name="mix_out_fwd", grid=(T // tm,),
        in_specs=[pl.BlockSpec((tm, D), row), pl.BlockSpec((tm, GM_WIDTH), row), pl.BlockSpec((tm, SB_WIDTH), row),
                  pl.BlockSpec((GM_WIDTH + SB_WIDTH, D), lambda i: (0, 0))],
        out_specs=pl.BlockSpec((tm, D), row),
        out_shape=jax.ShapeDtypeStruct((T, D), F32),
        compiler_params=_params("arbitrary"),
    )(h, gm, sb, w)


def _mix_out_bwd(dh, w):
    T, D = dh.shape
    tm = min(ROW_TILE, T)

    def body(dh_ref, w_ref, dgm_ref, dsb_ref, dhb_ref):
        dhb = dh_ref[...].astype(BF16)
        dhb_ref[...] = dhb
        dgm_ref[...] = _dot_nt(dhb, w_ref[:GM_WIDTH, :])
        dsb_ref[...] = _dot_nt(dhb, w_ref[GM_WIDTH:, :])

    row = lambda i: (i, 0)
    return pl.pallas_call(
        body, name="mix_out_bwd", grid=(T // tm,),
        in_specs=[pl.BlockSpec((tm, D), row), pl.BlockSpec((GM_WIDTH + SB_WIDTH, D), lambda i: (0, 0))],
        out_specs=[pl.BlockSpec((tm, GM_WIDTH), row), pl.BlockSpec((tm, SB_WIDTH), row), pl.BlockSpec((tm, D), row)],
        out_shape=[jax.ShapeDtypeStruct((T, GM_WIDTH), F32), jax.ShapeDtypeStruct((T, SB_WIDTH), F32),
                   jax.ShapeDtypeStruct((T, D), BF16)],
        compiler_params=_params("arbitrary"),
    )(dh, w)


def _tail(h3, p, target, g_ple, g_fin, w_gate, w_proj):
    T, D = h3.shape
    PD = p.shape[1]
    tm = min(ROW_TILE, T)

    def body(h_ref, p_ref, t_ref, gp_ref, gf_ref, wg_ref, wp_ref,
             loss_ref, dh_ref, n4_ref, dgl_ref, dpp_ref, dgp_ref, dgf_ref):
        @pl.when(pl.program_id(0) == 0)
        def _():
            loss_ref[...] = jnp.zeros_like(loss_ref)
            dgp_ref[...] = jnp.zeros_like(dgp_ref)
            dgf_ref[...] = jnp.zeros_like(dgf_ref)

        h3v = h_ref[...]
        gp = gp_ref[...]
        gf = gf_ref[...]
        r3, xh3 = _rms_parts(h3v)
        n4 = (xh3 * gp).astype(BF16)
        n4_ref[...] = n4
        gate = _sigmoid(_dot(n4, wg_ref[...]))
        pp = _dot(p_ref[...].astype(BF16), wp_ref[...])
        h4 = h3v + gate * pp
        r4, xh4 = _rms_parts(h4)
        err = xh4 * gf - t_ref[...]
        loss_ref[...] += jnp.full(loss_ref.shape, (0.5 / D) * jnp.sum(err * err), F32)
        dy = err * (1.0 / D)
        dgf_ref[...] += jnp.sum(dy * xh4, axis=0, keepdims=True)
        dyg = dy * gf
        dh4 = r4 * (dyg - xh4 * jnp.mean(dyg * xh4, axis=-1, keepdims=True))
        dpp_ref[...] = (dh4 * gate).astype(BF16)
        dgl = (dh4 * pp * gate * (1.0 - gate)).astype(BF16)
        dgl_ref[...] = dgl
        dn4 = _dot_nt(dgl, wg_ref[...])
        dgp_ref[...] += jnp.sum(dn4 * xh3, axis=0, keepdims=True)
        dn4g = dn4 * gp
        dh_ref[...] = dh4 + r3 * (dn4g - xh3 * jnp.mean(dn4g * xh3, axis=-1, keepdims=True))

    row = lambda i: (i, 0)
    one = lambda i: (0, 0)
    return pl.pallas_call(
        body, name="tail", grid=(T // tm,),
        in_specs=[pl.BlockSpec((tm, D), row), pl.BlockSpec((tm, PD), row), pl.BlockSpec((tm, D), row),
                  pl.BlockSpec((1, D), one), pl.BlockSpec((1, D), one),
                  pl.BlockSpec((D, D), one), pl.BlockSpec((PD, D), one)],
        out_specs=[pl.BlockSpec((1, LANES), one), pl.BlockSpec((tm, D), row), pl.BlockSpec((tm, D), row),
                   pl.BlockSpec((tm, D), row), pl.BlockSpec((tm, D), row),
                   pl.BlockSpec((1, D), one), pl.BlockSpec((1, D), one)],
        out_shape=[jax.ShapeDtypeStruct((1, LANES), F32), jax.ShapeDtypeStruct((T, D), F32),
                   jax.ShapeDtypeStruct((T, D), BF16), jax.ShapeDtypeStruct((T, D), BF16),
                   jax.ShapeDtypeStruct((T, D), BF16),
                   jax.ShapeDtypeStruct((1, D), F32), jax.ShapeDtypeStruct((1, D), F32)],
        compiler_params=_params("arbitrary"),
    )(h3, p, target, g_ple, g_fin, w_gate, w_proj)


FFN1_W = ("ffn1_w_in", "ffn1_w_out")
MIX_W = ("w_mix_in", "w_mix_out")
REST_W = ("ffn2_w_in", "ffn2_w_out", "ple_w_gate", "ple_w_proj")
BIG_W = FFN1_W + MIX_W + REST_W
COLUMN_SHARDED = ("w_mix_in", "ple_w_proj")


class _Traffic:
    def __init__(self, shards):
        self.shards = shards
        self.parts = {}
        self.blocks, self.sums = {}, {}

    @staticmethod
    def _full(name, gathered):
        if name in COLUMN_SHARDED:
            return jnp.transpose(gathered, (1, 0, 2)).reshape(gathered.shape[1], -1)
        if name.endswith("_w_in"):
            return gathered
        return gathered.reshape(-1, gathered.shape[-1])

    @staticmethod
    def _blocks(name, grad):
        name = name.split("/")[0]
        if name in COLUMN_SHARDED:
            return jnp.transpose(grad.reshape(grad.shape[0], N_DEV, -1), (1, 0, 2))
        if name.endswith("_w_in"):
            return grad
        return grad.reshape(N_DEV, -1, grad.shape[-1])

    def gather_now(self, names):
        got = _exchange("gather_" + names[0], [self.shards[n] for n in names], [GATHER] * len(names))
        return self.gathered(names, got)

    def gather_rider(self, names):
        return [self.shards[n] for n in names], [GATHER] * len(names)

    def gathered(self, names, got):
        return {n: self._full(n, g) for n, g in zip(names, got)}

    def scatter_rider(self, grads, gather=()):
        return ([self._blocks(n, g) for n, g in grads.items()] + list(gather),
                [SCATTER] * len(grads) + [GATHER] * len(gather))

    def scattered(self, names, got):
        self.parts.update(zip(names, got))
        return got[len(names):]


    def pair_rider(self, name, grad):
        self.blocks[name] = self._blocks(name, grad)
        return [self.blocks[name]], [PAIR]

    def paired(self, name, got):
        self.sums[name] = _stage_pair_sums("pair_sum_" + name.replace("/", "_"), self.blocks[name], got)

    def chip_rider(self, name):
        return [self.sums[name]], [CHIP]

    def chipped(self, name, got):
        self.parts[name] = got


def _local_step(traffic, x, p, target, g1, gmix, gv, ws, b_t, g2, gple, gfin, pack_small):
    T, D = x.shape
    tm = min(ROW_TILE, T)

    w = traffic.gather_now(FFN1_W)
    h1, n1, G1, U1, a1, *got = _ffn_fwd("ffn1_fwd", x, g1, w["ffn1_w_in"], w["ffn1_w_out"],
                                        rider=traffic.gather_rider(MIX_W))
    w.update(traffic.gathered(MIX_W, got))
    n2, zuv, qkv = _mix_in_fwd(h1, gmix, w["w_mix_in"])
    gm = _gmlp_fwd(zuv, gv, ws, b_t)
    sb, ltot, *got = _attn_fwd(qkv, rider=traffic.gather_rider(REST_W))
    w.update(traffic.gathered(REST_W, got))
    h2 = _mix_out_fwd(h1, gm, sb, w["w_mix_out"])
    h3, n3, G2, U2, a2 = _ffn_fwd("ffn2_fwd", h2, g2, w["ffn2_w_in"], w["ffn2_w_out"])
    loss, dh3, n4, d_gl, d_pp, dg_ple, dg_fin = _tail(h3, p, target, gple, gfin, w["ple_w_gate"], w["ple_w_proj"])

    nb, _, FB = w["ffn1_w_in"].shape
    nh = nb // 2

    tt = min(GRAD_ROW_TILE, T)

    def dw_out(name, a, d_out):
        return _matmul_tn(name, a, d_out, nh, (1, tt, FB), lambda j, t: (j, t, 0), (tt, D), lambda j, t: (t, 0),
                          (nh, FB, D), (1, FB, D), lambda j, t: (j, 0, 0))

    def dense_tn(name, a, b, ncol):
        ka, nbw = a.shape[1], b.shape[1] // ncol
        return _matmul_tn(name, a, b, ncol, (tt, ka), lambda j, t: (t, 0), (tt, nbw), lambda j, t: (t, j),
                          (ka, b.shape[1]), (ka, nbw), lambda j, t: (0, j))

    grads = dict(ple_w_gate=dense_tn("dw_ple_gate", n4, d_gl, 2), ple_w_proj=dense_tn("dw_ple_proj", p, d_pp, 1))
    dh2, dg2, dG2, dU2, dout2 = _ffn_bwd("ffn2_bwd", dh3, h2, g2, G2, U2, w["ffn2_w_in"], w["ffn2_w_out"])
    grads["ffn2_w_in"], = _dw_in("ffn2_dw_in", n3, dG2, dU2)
    grads["ffn2_w_out"] = dw_out("ffn2_dw_out", a2, dout2)
    grads = {n: grads[n] for n in REST_W}

    d_gm, d_sb, dh2_bf = _mix_out_bwd(dh2, w["w_mix_out"])
    grads["w_mix_out"] = jnp.concatenate([dense_tn("dw_mix_out_gm", gm, dh2_bf, 1),
                                          dense_tn("dw_mix_out_sb", sb, dh2_bf, 1)], axis=0)
    dzuv, dgv, dws, db_t = _gmlp_bwd(zuv, d_gm, gv, ws, b_t)
    dq, dk, dv, *got = _attn_bwd(qkv, d_sb, ltot, rider=traffic.scatter_rider(grads))
    traffic.scattered(list(grads), got)
    dqkv = jnp.concatenate([dq, dk, dv], axis=1)
    dw_mi = jnp.concatenate([dense_tn("dw_mix_in_uv", n2, dzuv, 2), dense_tn("dw_mix_in_qkv", n2, dqkv, 3)], axis=1)
    dh1, dgmix, dout1 = _mix_in_bwd(dzuv, dqkv, w["w_mix_in"], h1, gmix, dh2)

    def dw_out_riding(name, a, d_out, rider):
        return _matmul_tn(name, a, d_out, nh, (1, tt, FB), lambda j, t: (j, t, 0), (tt, D), lambda j, t: (t, 0),
                          (nh, FB, D), (1, FB, D), lambda j, t: (j, 0, 0), rider=rider)

    def both(*riders):
        return [x for r in riders for x in r[0]], [k for r in riders for k in r[1]]

    dw_out1, got = dw_out_riding("ffn1_dw_out", a1, dout1, traffic.pair_rider("w_mix_in", dw_mi))
    traffic.paired("w_mix_in", got)
    dG1, dU1, got_mi, got = _ffn_bwd_gates("ffn1_bwd_gates", dout1, G1, U1, w["ffn1_w_out"],
                                           rider=both(traffic.chip_rider("w_mix_in"),
                                                      traffic.pair_rider("ffn1_w_out", dw_out1)))
    traffic.chipped("w_mix_in", got_mi)
    traffic.paired("ffn1_w_out", got)
    half = D // 2
    top, got = _dw_in("ffn1_dw_in_top", n1[:, :half], dG1, dU1, rider=traffic.chip_rider("ffn1_w_out"))
    traffic.chipped("ffn1_w_out", got)
    bottom, got = _dw_in("ffn1_dw_in_bottom", n1[:, half:], dG1, dU1, rider=traffic.pair_rider("ffn1_w_in/0", top))
    traffic.paired("ffn1_w_in/0", got)
    traffic.paired("ffn1_w_in/1", _exchange("pair_last", *traffic.pair_rider("ffn1_w_in/1", bottom))[0])
    dx, dg1, got_top, got_bottom = _ffn_bwd_input("ffn1_bwd_input", dh1, x, g1, dG1, dU1, w["ffn1_w_in"],
                                                  rider=both(traffic.chip_rider("ffn1_w_in/0"),
                                                             traffic.chip_rider("ffn1_w_in/1")))
    traffic.chipped("ffn1_w_in/0", got_top)
    traffic.chipped("ffn1_w_in/1", got_bottom)

    small = pack_small(dict(ffn1_norm=dg1, mix_norm=dgmix, gmlp_v_norm=dgv, gmlp_w_s=dws, gmlp_b=jnp.transpose(db_t),
                            ffn2_norm=dg2, ple_norm=dg_ple, final_norm=dg_fin), loss)
    return dx, small


def _my_index():
    return 4 * lax.axis_index("x") + 2 * lax.axis_index("y") + lax.axis_index("c")


def _stage_pair_sums(name, blocks, got):
    _, R, C = blocks.shape
    me = _my_index()
    index = jnp.stack([jnp.bitwise_xor(me, 2 * c) for c in range(1, N_CHIPS_AWAY + 1)] + [me, me]).astype(jnp.int32)

    def body(index_ref, b_ref, g_ref, o_ref):
        i = pl.program_id(0)

        @pl.when(i < N_CHIPS_AWAY)
        def _():
            o_ref[...] = (b_ref[...].astype(F32) + g_ref[...].astype(F32)).astype(BF16)

        @pl.when(i == N_CHIPS_AWAY)
        def _():
            o_ref[...] = g_ref[...]

        @pl.when(i == N_CHIPS_AWAY + 1)
        def _():
            o_ref[...] = b_ref[...]

    return pl.pallas_call(
        body, name=name,
        grid_spec=pltpu.PrefetchScalarGridSpec(
            num_scalar_prefetch=1, grid=(PART_SLOTS,),
            in_specs=[pl.BlockSpec((1, R, C), lambda i, idx: (idx[i], 0, 0)),
                      pl.BlockSpec((1, R, C), lambda i, idx: (jnp.minimum(i, N_CHIPS_AWAY), 0, 0))],
            out_specs=pl.BlockSpec((1, R, C), lambda i, idx: (i, 0, 0))),
        out_shape=jax.ShapeDtypeStruct((PART_SLOTS, R, C), BF16), compiler_params=_params("arbitrary"),
    )(index, blocks, got)


def _peer(d):
    x, y, c = lax.axis_index("x"), lax.axis_index("y"), lax.axis_index("c")
    px = 1 - x if d & 4 else x
    py = 1 - y if d & 2 else y
    pc = 1 - c if d & 1 else c
    return (px, py, pc), 4 * px + 2 * py + pc


GATHER, SCATTER, PAIR, CHIP = "gather", "scatter", "pair", "chip"


class _ExchangePlan:
    def __init__(self, ins, outs, send, recv, local, kinds):
        self.ins, self.outs, self.send, self.recv, self.local, self.kinds = ins, outs, send, recv, local, kinds
        self.scatter = [k == SCATTER for k in kinds]
        self.me = _peer(0)[1]

    def _remote(self, t, sem, src, slot, peer):
        return pltpu.make_async_remote_copy(
            src_ref=src, dst_ref=self.outs[t].at[slot], send_sem=self.send.at[t, sem], recv_sem=self.recv.at[t, sem],
            device_id=peer, device_id_type=MESH)

    def _own(self, t):
        if self.kinds[t] == CHIP:
            kept = pl.ds(N_CHIPS_AWAY, PART_SLOTS - N_CHIPS_AWAY)
            return pltpu.make_async_copy(self.ins[t].at[kept], self.outs[t].at[kept], self.local.at[t])
        src = self.ins[t].at[self.me] if self.scatter[t] else self.ins[t]
        return pltpu.make_async_copy(src, self.outs[t].at[self.me], self.local.at[t])

    def _n_direct(self, t):
        return {SCATTER: N_DEV - 1, GATHER: N_CHIPS_AWAY + 1, PAIR: N_CHIPS_AWAY + 1, CHIP: N_CHIPS_AWAY}[self.kinds[t]]

    def _direct(self, t, k):
        kind = self.kinds[t]
        if kind == SCATTER:
            peer, slot = _peer(k + 1)
            return self._remote(t, k, self.ins[t].at[slot], self.me, peer)
        if kind == GATHER:
            return self._remote(t, k, self.ins[t], self.me, _peer(2 * k if k else 1)[0])
        if kind == PAIR:
            block = _peer(2 * (k + 1) + 1 if k < N_CHIPS_AWAY else 1)[1]
            return self._remote(t, k, self.ins[t].at[block], k, _peer(1)[0])
        return self._remote(t, k, self.ins[t].at[k], k, _peer(2 * (k + 1))[0])

    def _has_own(self, t):
        return self.kinds[t] != PAIR

    def _relay(self, t, c):
        slot = _peer(2 * c)[1]
        return self._remote(t, N_CHIPS_AWAY + c, self.outs[t].at[slot], slot, _peer(1)[0])

    def start(self):
        for t in range(len(self.ins)):
            if self._has_own(t):
                self._own(t).start()
            for k in range(self._n_direct(t)):
                self._direct(t, k).start()

    def relay(self):
        for t in self._gathers():
            for c in range(1, N_CHIPS_AWAY + 1):
                self._direct(t, c).wait_recv()
                self._relay(t, c).start()

    def _gathers(self):
        return [t for t in range(len(self.ins)) if self.kinds[t] == GATHER]

    def finish(self):
        for t in range(len(self.ins)):
            if self._has_own(t):
                self._own(t).wait()
            for k in range(self._n_direct(t)):
                self._direct(t, k).wait_send()
                if self.kinds[t] != GATHER or k == 0:
                    self._direct(t, k).wait_recv()
        for t in self._gathers():
            for c in range(1, N_CHIPS_AWAY + 1):
                self._relay(t, c).wait()


def _exchange_shapes(arrays, kinds):
    shape = {GATHER: lambda a: (N_DEV,) + a.shape, SCATTER: lambda a: a.shape, CHIP: lambda a: a.shape,
             PAIR: lambda a: (N_CHIPS_AWAY + 1,) + a.shape[1:]}
    return [jax.ShapeDtypeStruct(shape[k](a), a.dtype) for a, k in zip(arrays, kinds)]


def _exchange_sems(n):
    return [pltpu.SemaphoreType.DMA((n, N_DEV - 1)), pltpu.SemaphoreType.DMA((n, N_DEV - 1)),
            pltpu.SemaphoreType.DMA((n,))]


_ANY = pl.BlockSpec(memory_space=pl.ANY)


def _exchange(name, arrays, scatter):
    n = len(arrays)

    def body(*refs):
        plan = _ExchangePlan(refs[:n], refs[n:2 * n], *refs[2 * n:], scatter)
        plan.start()
        plan.relay()
        plan.finish()

    return pl.pallas_call(
        body, name=name, in_specs=[_ANY] * n, out_specs=[_ANY] * n, out_shape=_exchange_shapes(arrays, scatter),
        scratch_shapes=_exchange_sems(n),
    )(*arrays)


def _pallas(body, rider, *, name, grid, in_specs, out_specs, out_shape, scratch_shapes=(), compiler_params=None):
    if rider is None:
        return pl.pallas_call(body, name=name, grid=grid, in_specs=in_specs, out_specs=out_specs, out_shape=out_shape,
                              scratch_shapes=list(scratch_shapes), compiler_params=compiler_params)
    arrays, scatter = rider
    n, ni, no, ns = len(arrays), len(in_specs), len(out_specs), len(scratch_shapes)

    def carried(*refs):
        ins, r_in = refs[:ni], refs[ni:ni + n]
        outs, r_out = refs[ni + n:ni + n + no], refs[ni + n + no:ni + 2 * n + no]
        scratch, sems = refs[ni + 2 * n + no:ni + 2 * n + no + ns], refs[ni + 2 * n + no + ns:]
        step = 0
        for ax, g in enumerate(grid):
            step = step * g + pl.program_id(ax)
        steps = functools.reduce(lambda a, b: a * b, grid)

        @pl.when(step == 0)
        def _():
            _ExchangePlan(r_in, r_out, *sems, scatter).start()

        @pl.when(step == steps // 2)
        def _():
            _ExchangePlan(r_in, r_out, *sems, scatter).relay()

        body(*ins, *outs, *scratch)

        @pl.when(step == steps - 1)
        def _():
            _ExchangePlan(r_in, r_out, *sems, scatter).finish()

    call = pl.pallas_call(
        carried, name=name, grid=grid, in_specs=list(in_specs) + [_ANY] * n, out_specs=list(out_specs) + [_ANY] * n,
        out_shape=list(out_shape) + _exchange_shapes(arrays, scatter),
        scratch_shapes=list(scratch_shapes) + _exchange_sems(n), compiler_params=compiler_params)
    return lambda *args: call(*args, *arrays)


def _adamw_math(g, w, m, v):
    m_new = ADAM_B1 * m + (1.0 - ADAM_B1) * g
    v_new = ADAM_B2 * v + (1.0 - ADAM_B2) * (g * g)
    m_hat = m_new / (1.0 - ADAM_B1 ** ADAM_STEP)
    v_hat = v_new / (1.0 - ADAM_B2 ** ADAM_STEP)
    delta = -ADAM_LR * (m_hat / (jnp.sqrt(v_hat) + ADAM_EPS) + ADAM_WD * w)
    return delta, m_new, v_new


def _adamw(name, parts, w, m, v, rider=None):
    R, C = w.shape
    slots = parts.shape[0]
    tr = R
    for cand in (256, 128, 64, 32, 16, 8):
        if R % cand == 0:
            tr = cand
            break

    def body(p_ref, w_ref, m_ref, v_ref, g_ref, d_ref, nm_ref, nv_ref):
        g = p_ref[0].astype(F32)
        for j in range(1, slots):
            g = g + p_ref[j].astype(F32)
        g_ref[...] = g
        d_ref[...], nm_ref[...], nv_ref[...] = _adamw_math(g, w_ref[...], m_ref[...], v_ref[...])

    row = lambda i: (i, 0)
    spec = pl.BlockSpec((tr, C), row)
    return _pallas(
        body, rider, name=name, grid=(R // tr,),
        in_specs=[pl.BlockSpec((slots, tr, C), lambda i: (0, i, 0)), spec, spec, spec],
        out_specs=[spec] * 4,
        out_shape=[jax.ShapeDtypeStruct((R, C), F32)] * 4,
        compiler_params=_params("arbitrary"),
    )(parts, w, m, v)


def _rows128(a):
    flat = a.reshape(-1, LANES)
    pad = (-flat.shape[0]) % SMALL_ROWS_ALIGN
    return jnp.pad(flat, ((0, pad), (0, 0))) if pad else flat


def _unrows(packed, like):
    n = like.size // LANES
    return packed[:n].reshape(like.shape)


def kernel(x, p, ffn1_norm, ffn1_w_in, ffn1_w_out, mix_norm, w_mix_in, gmlp_v_norm, gmlp_w_s, gmlp_b, w_mix_out, ffn2_norm, ffn2_w_in, ffn2_w_out, ple_norm, ple_w_gate, ple_w_proj, final_norm, loss_target, m_ffn1_norm, m_ffn1_w_in, m_ffn1_w_out, m_mix_norm, m_w_mix_in, m_gmlp_v_norm, m_gmlp_w_s, m_gmlp_b, m_w_mix_out, m_ffn2_norm, m_ffn2_w_in, m_ffn2_w_out, m_ple_norm, m_ple_w_gate, m_ple_w_proj, m_final_norm, v_ffn1_norm, v_ffn1_w_in, v_ffn1_w_out, v_mix_norm, v_w_mix_in, v_gmlp_v_norm, v_gmlp_w_s, v_gmlp_b, v_w_mix_out, v_ffn2_norm, v_ffn2_w_in, v_ffn2_w_out, v_ple_norm, v_ple_w_gate, v_ple_w_proj, v_final_norm):
    names = ["ffn1_norm", "ffn1_w_in", "ffn1_w_out", "mix_norm", "w_mix_in", "gmlp_v_norm", "gmlp_w_s", "gmlp_b",
             "w_mix_out", "ffn2_norm", "ffn2_w_in", "ffn2_w_out", "ple_norm", "ple_w_gate", "ple_w_proj", "final_norm"]
    W = dict(zip(names, [ffn1_norm, ffn1_w_in, ffn1_w_out, mix_norm, w_mix_in, gmlp_v_norm, gmlp_w_s, gmlp_b,
                         w_mix_out, ffn2_norm, ffn2_w_in, ffn2_w_out, ple_norm, ple_w_gate, ple_w_proj, final_norm]))
    M = dict(zip(names, [m_ffn1_norm, m_ffn1_w_in, m_ffn1_w_out, m_mix_norm, m_w_mix_in, m_gmlp_v_norm, m_gmlp_w_s,
                         m_gmlp_b, m_w_mix_out, m_ffn2_norm, m_ffn2_w_in, m_ffn2_w_out, m_ple_norm, m_ple_w_gate,
                         m_ple_w_proj, m_final_norm]))
    V = dict(zip(names, [v_ffn1_norm, v_ffn1_w_in, v_ffn1_w_out, v_mix_norm, v_w_mix_in, v_gmlp_v_norm, v_gmlp_w_s,
                         v_gmlp_b, v_w_mix_out, v_ffn2_norm, v_ffn2_w_in, v_ffn2_w_out, v_ple_norm, v_ple_w_gate,
                         v_ple_w_proj, v_final_norm]))
    small = [n for n in names if n not in BIG_W]
    D = x.shape[-1]

    def pack(src, last):
        return jnp.concatenate([_rows128(src[n]) for n in small] + [last], axis=0)

    offs = [0]
    for n in small:
        offs.append(offs[-1] + _rows128(W[n]).shape[0])

    traffic = _Traffic({n: W[n][0].astype(BF16) for n in BIG_W})
    dx, small_mine = _local_step(
        traffic, x[0], p[0, 0], loss_target[0],
        W["ffn1_norm"], W["mix_norm"], W["gmlp_v_norm"], W["gmlp_w_s"][0], jnp.transpose(W["gmlp_b"][0]),
        W["ffn2_norm"], W["ple_norm"], W["final_norm"].reshape(1, D),
        lambda grads, loss_part: pack(grads, jnp.broadcast_to(loss_part, (SMALL_ROWS_ALIGN, LANES))))

    out = {}
    parts = traffic.parts
    parts["ffn1_w_in"] = jnp.concatenate([parts["ffn1_w_in/0"], parts["ffn1_w_in/1"]], axis=1)
    carrier = "ffn2_w_out"
    *out[carrier], small_parts = _adamw("adamw_" + carrier, parts[carrier], W[carrier][0], M[carrier][0], V[carrier][0],
                                        rider=([small_mine], [GATHER]))
    for n in BIG_W:
        if n != carrier:
            out[n] = _adamw("adamw_" + n, parts[n], W[n][0], M[n][0], V[n][0])
    zeros = jnp.zeros((SMALL_ROWS_ALIGN, LANES), F32)
    sg, sd, sm, sv = _adamw("adamw_small", small_parts, pack(W, zeros), pack(M, zeros), pack(V, zeros))
    for k, n in enumerate(small):
        out[n] = tuple(_unrows(arr[offs[k]:offs[k + 1]], W[n]) for arr in (sg, sd, sm, sv))
    loss = sg[offs[len(small)], 0]

    res = [loss, dx[None]]
    for k in range(4):
        res += [out[n][k].reshape(W[n].shape) for n in names]
    return tuple(res)
```

```python
import functools

import jax
import jax.numpy as jnp
from jax import lax
from jax.experimental import pallas as pl
from jax.experimental.pallas import tpu as pltpu

F32 = jnp.float32
BF16 = jnp.bfloat16
MESH = pl.DeviceIdType.MESH

N_DEV = 8
N_CHIPS_AWAY = 3
PART_SLOTS = N_CHIPS_AWAY + 2
EPS = 1e-6
ADAM_LR = 0.001
ADAM_B1 = 0.9
ADAM_B2 = 0.999
ADAM_EPS = 1e-08
ADAM_WD = 0.01
ADAM_STEP = 10

GM_WIDTH = 512
GM_HEADS = 4
CHUNK = 128
SB_WIDTH = 512
SB_HEAD_DIM = 64
SB_SCALE = 0.125
LANES = 128
SMALL_ROWS_ALIGN = 8

ROW_TILE = 512
GRAD_ROW_TILE = 2048
DW_IN_ROW_TILE = 4096
FFN_FWD_ROW_TILE = 1024
ATTN_Q_ROWS = 512
ATTN_BWD_Q_ROWS = 512
ATTN_KEY_BLOCK = 256
ATTN_UNROLL = 2
VMEM_LIMIT = 56 * 1024 * 1024


def _params(*sem):
    return pltpu.CompilerParams(dimension_semantics=sem, vmem_limit_bytes=VMEM_LIMIT)


def _dot(a, b):
    return jnp.dot(a, b, preferred_element_type=F32)


def _dot_nt(a, b):
    return lax.dot_general(a, b, (((1,), (1,)), ((), ())), preferred_element_type=F32)


def _dot_tn(a, b):
    return lax.dot_general(a, b, (((0,), (0,)), ((), ())), preferred_element_type=F32)


def _rms_parts(x):
    r = lax.rsqrt(jnp.mean(x * x, axis=-1, keepdims=True) + EPS)
    return r, x * r


def _rms_bwd(x, g, dy):
    r, xh = _rms_parts(x)
    dyg = dy * g
    dx = r * (dyg - xh * jnp.mean(dyg * xh, axis=-1, keepdims=True))
    return dx, jnp.sum(dy * xh, axis=0, keepdims=True)


def _sigmoid(x):
    return 1.0 / (1.0 + jnp.exp(-x))


_SQRT_HALF = 0.7071067811865476
_INV_SQRT_2PI = 0.3989422804014327


def _gelu(x):
    return 0.5 * x * (1.0 + lax.erf(x * _SQRT_HALF))


def _gelu_grad(x):
    return 0.5 * (1.0 + lax.erf(x * _SQRT_HALF)) + x * (_INV_SQRT_2PI * jnp.exp(-0.5 * x * x))


def _split_bf16(x):
    hi = x.astype(BF16)
    lo = (x - hi.astype(F32)).astype(BF16)
    return hi, lo


def _ffn_fwd(name, h, gain, w_in, w_out, rider=None):
    T, D = h.shape
    nb, _, FB = w_in.shape
    nh = nb // 2
    tm = min(FFN_FWD_ROW_TILE, T)

    def body(h_ref, g_ref, wg_ref, wu_ref, wo_ref, ho_ref, n_ref, G_ref, U_ref, a_ref, n_s, acc):
        jj = pl.program_id(1)

        @pl.when(jj == 0)
        def _():
            _, xh = _rms_parts(h_ref[...])
            n = (xh * g_ref[...]).astype(BF16)
            n_s[...] = n
            n_ref[...] = n
            acc[...] = jnp.zeros_like(acc)

        n = n_s[...]
        G = _dot(n, wg_ref[0])
        U = _dot(n, wu_ref[0])
        G_ref[0] = G.astype(BF16)
        U_ref[0] = U.astype(BF16)
        a = (G * _sigmoid(G) * U).astype(BF16)
        a_ref[0] = a
        acc[...] += _dot(a, wo_ref[...])

        @pl.when(jj == nh - 1)
        def _():
            ho_ref[...] = h_ref[...] + 0.5 * acc[...]

    row = lambda i, j: (i, 0)
    blk = lambda i, j: (j, i, 0)
    return _pallas(
        body, rider, name=name, grid=(T // tm, nh),
        in_specs=[pl.BlockSpec((tm, D), row),
                  pl.BlockSpec((1, D), lambda i, j: (0, 0)),
                  pl.BlockSpec((1, D, FB), lambda i, j: (j, 0, 0)),
                  pl.BlockSpec((1, D, FB), lambda i, j: (j + nh, 0, 0)),
                  pl.BlockSpec((FB, D), lambda i, j: (j, 0))],
        out_specs=[pl.BlockSpec((tm, D), row), pl.BlockSpec((tm, D), row),
                   pl.BlockSpec((1, tm, FB), blk), pl.BlockSpec((1, tm, FB), blk),
                   pl.BlockSpec((1, tm, FB), blk)],
        out_shape=[jax.ShapeDtypeStruct((T, D), F32), jax.ShapeDtypeStruct((T, D), BF16),
                   jax.ShapeDtypeStruct((nh, T, FB), BF16), jax.ShapeDtypeStruct((nh, T, FB), BF16),
                   jax.ShapeDtypeStruct((nh, T, FB), BF16)],
        scratch_shapes=[pltpu.VMEM((tm, D), BF16), pltpu.VMEM((tm, D), F32)],
        compiler_params=_params("arbitrary", "arbitrary"),
    )(h, gain, w_in, w_in, w_out)


def _ffn_bwd(name, dh, h_in, gain, G, U, w_in, w_out, rider=None):
    T, D = dh.shape
    nb, _, FB = w_in.shape
    nh = nb // 2
    tm = min(ROW_TILE, T)

    def body(dh_ref, h_ref, g_ref, G_ref, U_ref, wg_ref, wu_ref, wo_ref,
             dhin_ref, dg_ref, dG_ref, dU_ref, do_ref, dn_acc, do_s):
        i = pl.program_id(0)
        jj = pl.program_id(1)

        @pl.when(jj == 0)
        def _():
            d_out = (0.5 * dh_ref[...]).astype(BF16)
            do_s[...] = d_out
            do_ref[...] = d_out
            dn_acc[...] = jnp.zeros_like(dn_acc)

        @pl.when((i == 0) & (jj == 0))
        def _():
            dg_ref[...] = jnp.zeros_like(dg_ref)

        halves = [slice(0, tm // 2), slice(tm // 2, tm)]
        da = [_dot_nt(do_s[rows, :], wo_ref[...]) for rows in halves]
        dGU = []
        for rows, dav in zip(halves, da):
            dG, dU = _gate_grads(dav, G_ref[0, rows, :].astype(F32), U_ref[0, rows, :].astype(F32))
            dG_ref[0, rows, :] = dG
            dU_ref[0, rows, :] = dU
            dGU.append((dG, dU))
        dn = [_dot_nt(dG, wg_ref[0]) for dG, _ in dGU]
        dn = [d + _dot_nt(dU, wu_ref[0]) for d, (_, dU) in zip(dn, dGU)]
        for rows, d in zip(halves, dn):
            dn_acc[rows, :] += d

        @pl.when(jj == nh - 1)
        def _():
            dx, dg = _rms_bwd(h_ref[...], g_ref[...], dn_acc[...])
            dhin_ref[...] = dh_ref[...] + dx
            dg_ref[...] += dg

    row = lambda i, j: (i, 0)
    blk = lambda i, j: (j, i, 0)
    one = lambda i, j: (0, 0)
    return _pallas(
        body, rider, name=name, grid=(T // tm, nh),
        in_specs=[pl.BlockSpec((tm, D), row), pl.BlockSpec((tm, D), row), pl.BlockSpec((1, D), one),
                  pl.BlockSpec((1, tm, FB), blk), pl.BlockSpec((1, tm, FB), blk),
                  pl.BlockSpec((1, D, FB), lambda i, j: (j, 0, 0)),
                  pl.BlockSpec((1, D, FB), lambda i, j: (j + nh, 0, 0)),
                  pl.BlockSpec((FB, D), lambda i, j: (j, 0))],
        out_specs=[pl.BlockSpec((tm, D), row), pl.BlockSpec((1, D), one),
                   pl.BlockSpec((1, tm, FB), blk), pl.BlockSpec((1, tm, FB), blk),
                   pl.BlockSpec((tm, D), row)],
        out_shape=[jax.ShapeDtypeStruct((T, D), F32), jax.ShapeDtypeStruct((1, D), F32),
                   jax.ShapeDtypeStruct((nh, T, FB), BF16), jax.ShapeDtypeStruct((nh, T, FB), BF16),
                   jax.ShapeDtypeStruct((T, D), BF16)],
        scratch_shapes=[pltpu.VMEM((tm, D), F32), pltpu.VMEM((tm, D), BF16)],
        compiler_params=_params("arbitrary", "arbitrary"),
    )(dh, h_in, gain, G, U, w_in, w_in, w_out)


def _gate_grads(dav, Gv, Uv):
    sig = _sigmoid(Gv)
    return (dav * Uv * (sig * (1.0 + Gv * (1.0 - sig)))).astype(BF16), (dav * (Gv * sig)).astype(BF16)


def _ffn_bwd_gates(name, d_out, G, U, w_out, rider=None):
    T, D = d_out.shape
    nh, _, FB = G.shape
    tm = min(FFN_FWD_ROW_TILE, T)

    def body(do_ref, G_ref, U_ref, wo_ref, dG_ref, dU_ref):
        halves = [slice(0, tm // 2), slice(tm // 2, tm)]
        da = [_dot_nt(do_ref[rows, :], wo_ref[...]) for rows in halves]
        for rows, dav in zip(halves, da):
            dG_ref[0, rows, :], dU_ref[0, rows, :] = _gate_grads(
                dav, G_ref[0, rows, :].astype(F32), U_ref[0, rows, :].astype(F32))

    blk = lambda i, j: (j, i, 0)
    return _pallas(
        body, rider, name=name, grid=(T // tm, nh),
        in_specs=[pl.BlockSpec((tm, D), lambda i, j: (i, 0)), pl.BlockSpec((1, tm, FB), blk),
                  pl.BlockSpec((1, tm, FB), blk), pl.BlockSpec((FB, D), lambda i, j: (j, 0))],
        out_specs=[pl.BlockSpec((1, tm, FB), blk), pl.BlockSpec((1, tm, FB), blk)],
        out_shape=[jax.ShapeDtypeStruct((nh, T, FB), BF16), jax.ShapeDtypeStruct((nh, T, FB), BF16)],
        compiler_params=_params("arbitrary", "arbitrary"),
    )(d_out, G, U, w_out)


def _ffn_bwd_input(name, dh, h_in, gain, dG, dU, w_in, rider=None):
    T, D = dh.shape
    nb, _, FB = w_in.shape
    nh = nb // 2
    tm = min(FFN_FWD_ROW_TILE, T)

    def body(dh_ref, h_ref, g_ref, dG_ref, dU_ref, wg_ref, wu_ref, dhin_ref, dg_ref, dn_acc):
        i = pl.program_id(0)
        jj = pl.program_id(1)

        @pl.when(jj == 0)
        def _():
            dn_acc[...] = jnp.zeros_like(dn_acc)

        @pl.when((i == 0) & (jj == 0))
        def _():
            dg_ref[...] = jnp.zeros_like(dg_ref)

        halves = [slice(0, tm // 2), slice(tm // 2, tm)]
        dn = [_dot_nt(dG_ref[0, rows, :], wg_ref[0]) for rows in halves]
        dn = [d + _dot_nt(dU_ref[0, rows, :], wu_ref[0]) for d, rows in zip(dn, halves)]
        for rows, d in zip(halves, dn):
            dn_acc[rows, :] += d

        @pl.when(jj == nh - 1)
        def _():
            dx, dg = _rms_bwd(h_ref[...], g_ref[...], dn_acc[...])
            dhin_ref[...] = dh_ref[...] + dx
            dg_ref[...] += dg

    row = lambda i, j: (i, 0)
    blk = lambda i, j: (j, i, 0)
    one = lambda i, j: (0, 0)
    return _pallas(
        body, rider, name=name, grid=(T // tm, nh),
        in_specs=[pl.BlockSpec((tm, D), row), pl.BlockSpec((tm, D), row), pl.BlockSpec((1, D), one),
                  pl.BlockSpec((1, tm, FB), blk), pl.BlockSpec((1, tm, FB), blk),
                  pl.BlockSpec((1, D, FB), lambda i, j: (j, 0, 0)),
                  pl.BlockSpec((1, D, FB), lambda i, j: (j + nh, 0, 0))],
        out_specs=[pl.BlockSpec((tm, D), row), pl.BlockSpec((1, D), one)],
        out_shape=[jax.ShapeDtypeStruct((T, D), F32), jax.ShapeDtypeStruct((1, D), F32)],
        scratch_shapes=[pltpu.VMEM((tm, D), F32)],
        compiler_params=_params("arbitrary", "arbitrary"),
    )(dh, h_in, gain, dG, dU, w_in, w_in)


def _matmul_tn(name, a, b, nj, a_block, a_map, b_block, b_map, out_shape, out_block, out_map, rider=None):
    T = a.shape[-2]
    tt = a_block[-2]
    nt = T // tt
    kb, nbk = out_block[-2], out_block[-1]

    def body(a_ref, b_ref, o_ref, acc):
        t = pl.program_id(1)

        @pl.when(t == 0)
        def _():
            acc[...] = jnp.zeros_like(acc)

        av = (a_ref[0] if len(a_block) == 3 else a_ref[...]).astype(BF16)
        bv = b_ref[0] if len(b_block) == 3 else b_ref[...]
        acc[...] += _dot_tn(av, bv)

        @pl.when(t == nt - 1)
        def _():
            if len(out_block) == 3:
                o_ref[0] = acc[...].astype(o_ref.dtype)
            else:
                o_ref[...] = acc[...].astype(o_ref.dtype)

    got = _pallas(
        body, rider, name=name, grid=(nj, nt),
        in_specs=[pl.BlockSpec(a_block, a_map), pl.BlockSpec(b_block, b_map)],
        out_specs=[pl.BlockSpec(out_block, out_map)],
        out_shape=[jax.ShapeDtypeStruct(out_shape, BF16)],
        scratch_shapes=[pltpu.VMEM((kb, nbk), F32)],
        compiler_params=_params("arbitrary", "arbitrary"),
    )(a, b)
    return got[0] if rider is None else got


def _dw_in(name, n, dG, dU, rider=None):
    T, kr = n.shape
    nh, _, FB = dG.shape
    tt = min(DW_IN_ROW_TILE, T)
    nt = T // tt
    cut = LANES * ((kr // LANES + 1) // 2)

    def body(n_ref, dg_ref, du_ref, o_ref, acc):
        j = pl.program_id(0)
        t = pl.program_id(1)

        @pl.when(t == 0)
        def _():
            acc[...] = jnp.zeros_like(acc)

        def add(dz_ref):
            for rows in ((slice(0, cut), slice(cut, kr)) if cut < kr else (slice(0, kr),)):
                acc[rows, :] += _dot_tn(n_ref[:, rows], dz_ref[0])

        @pl.when(j < nh)
        def _():
            add(dg_ref)

        @pl.when(j >= nh)
        def _():
            add(du_ref)

        @pl.when(t == nt - 1)
        def _():
            o_ref[0] = acc[...].astype(BF16)

    return _pallas(
        body, rider, name=name, grid=(2 * nh, nt),
        in_specs=[pl.BlockSpec((tt, kr), lambda j, t: (t, 0)),
                  pl.BlockSpec((1, tt, FB), lambda j, t: (jnp.minimum(j, nh - 1), t, 0)),
                  pl.BlockSpec((1, tt, FB), lambda j, t: (jnp.maximum(j - nh, 0), t, 0))],
        out_specs=[pl.BlockSpec((1, kr, FB), lambda j, t: (j, 0, 0))],
        out_shape=[jax.ShapeDtypeStruct((2 * nh, kr, FB), BF16)],
        scratch_shapes=[pltpu.VMEM((kr, FB), F32)],
        compiler_params=_params("arbitrary", "arbitrary"),
    )(n, dG, dU)


def _mix_in_fwd(h, gain, w):
    T, D = h.shape
    W = w.shape[1]
    nuv = 2 * GM_WIDTH
    tm = min(ROW_TILE, T)

    def body(h_ref, g_ref, w_ref, n_ref, zuv_ref, qkv_ref):
        _, xh = _rms_parts(h_ref[...])
        n = (xh * g_ref[...]).astype(BF16)
        n_ref[...] = n
        z = _dot(n, w_ref[...])
        zuv_ref[...] = z[:, :nuv]
        qkv_ref[...] = z[:, nuv:].astype(BF16)

    row = lambda i: (i, 0)
    return pl.pallas_call(
        body, name="mix_in_fwd", grid=(T // tm,),
        in_specs=[pl.BlockSpec((tm, D), row), pl.BlockSpec((1, D), lambda i: (0, 0)),
                  pl.BlockSpec((D, W), lambda i: (0, 0))],
        out_specs=[pl.BlockSpec((tm, D), row), pl.BlockSpec((tm, nuv), row),
                   pl.BlockSpec((tm, W - nuv), row)],
        out_shape=[jax.ShapeDtypeStruct((T, D), BF16), jax.ShapeDtypeStruct((T, nuv), F32),
                   jax.ShapeDtypeStruct((T, W - nuv), BF16)],
        compiler_params=_params("arbitrary"),
    )(h, gain, w)


def _mix_in_bwd(dzuv, dqkv, w, h, gain, dh):
    T, D = h.shape
    W = w.shape[1]
    nuv = dzuv.shape[1]
    tm = min(ROW_TILE, T)

    def body(dzuv_ref, dqkv_ref, w_ref, h_ref, g_ref, dh_ref, dhin_ref, dg_ref, half_ref):
        @pl.when(pl.program_id(0) == 0)
        def _():
            dg_ref[...] = jnp.zeros_like(dg_ref)

        dn = _dot_nt(dzuv_ref[...], w_ref[:, :nuv]) + _dot_nt(dqkv_ref[...], w_ref[:, nuv:])
        dx, dg = _rms_bwd(h_ref[...], g_ref[...], dn)
        dh_in = dh_ref[...] + dx
        dhin_ref[...] = dh_in
        half_ref[...] = (0.5 * dh_in).astype(BF16)
        dg_ref[...] += dg

    row = lambda i: (i, 0)
    one = lambda i: (0, 0)
    return pl.pallas_call(
        body, name="mix_in_bwd", grid=(T // tm,),
        in_specs=[pl.BlockSpec((tm, nuv), row), pl.BlockSpec((tm, W - nuv), row),
                  pl.BlockSpec((D, W), one), pl.BlockSpec((tm, D), row), pl.BlockSpec((1, D), one),
                  pl.BlockSpec((tm, D), row)],
        out_specs=[pl.BlockSpec((tm, D), row), pl.BlockSpec((1, D), one), pl.BlockSpec((tm, D), row)],
        out_shape=[jax.ShapeDtypeStruct((T, D), F32), jax.ShapeDtypeStruct((1, D), F32),
                   jax.ShapeDtypeStruct((T, D), BF16)],
        compiler_params=_params("arbitrary"),
    )(dzuv, dqkv, w, h, gain, dh)


def _gmlp_norm(zv, gv):
    v = _gelu(zv)
    r, vh = _rms_parts(v)
    return r, vh, (vh * gv).astype(BF16)


def _causal_ws(ws_ref, hd):
    r = lax.broadcasted_iota(jnp.int32, (CHUNK, CHUNK), 0)
    c = lax.broadcasted_iota(jnp.int32, (CHUNK, CHUNK), 1)
    return jnp.where(r >= c, ws_ref[hd], 0.0).astype(BF16)


def _gmlp_fwd(zuv, gv, ws, b_t):
    T = zuv.shape[0]
    tg = min(ROW_TILE, T)

    def body(zu_ref, zv_ref, gv_ref, ws_ref, bt_ref, o_ref):
        u = _gelu(zu_ref[...])
        _, _, vn = _gmlp_norm(zv_ref[...], gv_ref[...])
        for hd in range(GM_HEADS):
            wc = _causal_ws(ws_ref, hd)
            cols = slice(hd * CHUNK, (hd + 1) * CHUNK)
            for c in range(tg // CHUNK):
                rows = slice(c * CHUNK, (c + 1) * CHUNK)
                sv = _dot(wc, vn[rows, cols]) + bt_ref[:, hd:hd + 1]
                o_ref[rows, cols] = (u[rows, cols] * sv).astype(BF16)

    return pl.pallas_call(
        body, name="gmlp_fwd", grid=(T // tg,),
        in_specs=[pl.BlockSpec((tg, GM_WIDTH), lambda i: (i, 0)), pl.BlockSpec((tg, GM_WIDTH), lambda i: (i, 1)),
                  pl.BlockSpec((1, GM_WIDTH), lambda i: (0, 0)),
                  pl.BlockSpec((GM_HEADS, CHUNK, CHUNK), lambda i: (0, 0, 0)),
                  pl.BlockSpec((CHUNK, GM_HEADS), lambda i: (0, 0))],
        out_specs=pl.BlockSpec((tg, GM_WIDTH), lambda i: (i, 0)),
        out_shape=jax.ShapeDtypeStruct((T, GM_WIDTH), BF16),
        compiler_params=_params("arbitrary"),
    )(zuv, zuv, gv, ws, b_t)


def _gmlp_bwd(zuv, d_gm, gv, ws, b_t):
    T = zuv.shape[0]
    tg = min(ROW_TILE, T)
    ng = T // tg

    def body(zu_ref, zv_ref, dgm_ref, gv_ref, ws_ref, bt_ref, dz_ref, dgv_ref, dws_ref, dbt_ref, dsv_acc, dvn_s):
        i = pl.program_id(0)

        @pl.when(i == 0)
        def _():
            dgv_ref[...] = jnp.zeros_like(dgv_ref)
            dws_ref[...] = jnp.zeros_like(dws_ref)
            dsv_acc[...] = jnp.zeros_like(dsv_acc)

        zu = zu_ref[...]
        zv = zv_ref[...]
        dgm = dgm_ref[...]
        gvv = gv_ref[...]
        u = _gelu(zu)
        rv, vh, vn = _gmlp_norm(zv, gvv)
        dsv = dgm * u
        dsv_b = dsv.astype(BF16)
        for hd in range(GM_HEADS):
            wc = _causal_ws(ws_ref, hd)
            cols = slice(hd * CHUNK, (hd + 1) * CHUNK)
            dws = jnp.zeros((CHUNK, CHUNK), F32)
            dsv_sum = jnp.zeros((CHUNK, CHUNK), F32)
            for c in range(tg // CHUNK):
                rows = slice(c * CHUNK, (c + 1) * CHUNK)
                vch = vn[rows, cols]
                sv = _dot(wc, vch) + bt_ref[:, hd:hd + 1]
                dz_ref[rows, cols] = (dgm[rows, cols] * sv * _gelu_grad(zu[rows, cols])).astype(BF16)
                dws += _dot_nt(dsv_b[rows, cols], vch)
                dsv_sum += dsv[rows, cols]
                dvn_s[rows, cols] = _dot_tn(wc, dsv_b[rows, cols])
            dws_ref[hd] += dws
            dsv_acc[:, cols] += dsv_sum
        dvn = dvn_s[...]
        dvh = dvn * gvv
        dv = rv * (dvh - vh * jnp.mean(dvh * vh, axis=-1, keepdims=True))
        dgv_ref[...] += jnp.sum(dvn * vh, axis=0, keepdims=True)
        dz_ref[:, GM_WIDTH:] = (dv * _gelu_grad(zv)).astype(BF16)

        @pl.when(i == ng - 1)
        def _():
            r = lax.broadcasted_iota(jnp.int32, (CHUNK, CHUNK), 0)
            c = lax.broadcasted_iota(jnp.int32, (CHUNK, CHUNK), 1)
            for hd in range(GM_HEADS):
                dws_ref[hd] = jnp.where(r >= c, dws_ref[hd], 0.0)
                dbt_ref[:, hd:hd + 1] = jnp.sum(dsv_acc[:, hd * CHUNK:(hd + 1) * CHUNK], axis=1, keepdims=True)

    return pl.pallas_call(
        body, name="gmlp_bwd", grid=(ng,),
        in_specs=[pl.BlockSpec((tg, GM_WIDTH), lambda i: (i, 0)), pl.BlockSpec((tg, GM_WIDTH), lambda i: (i, 1)),
                  pl.BlockSpec((tg, GM_WIDTH), lambda i: (i, 0)),
                  pl.BlockSpec((1, GM_WIDTH), lambda i: (0, 0)),
                  pl.BlockSpec((GM_HEADS, CHUNK, CHUNK), lambda i: (0, 0, 0)),
                  pl.BlockSpec((CHUNK, GM_HEADS), lambda i: (0, 0))],
        out_specs=[pl.BlockSpec((tg, 2 * GM_WIDTH), lambda i: (i, 0)),
                   pl.BlockSpec((1, GM_WIDTH), lambda i: (0, 0)),
                   pl.BlockSpec((GM_HEADS, CHUNK, CHUNK), lambda i: (0, 0, 0)),
                   pl.BlockSpec((CHUNK, GM_HEADS), lambda i: (0, 0))],
        out_shape=[jax.ShapeDtypeStruct((T, 2 * GM_WIDTH), BF16), jax.ShapeDtypeStruct((1, GM_WIDTH), F32),
                   jax.ShapeDtypeStruct((GM_HEADS, CHUNK, CHUNK), F32),
                   jax.ShapeDtypeStruct((CHUNK, GM_HEADS), F32)],
        scratch_shapes=[pltpu.VMEM((CHUNK, GM_WIDTH), F32), pltpu.VMEM((tg, GM_WIDTH), F32)],
        compiler_params=_params("arbitrary"),
    )(zuv, zuv, d_gm, gv, ws, b_t)


def _scan_matrix(blk, keep):
    r = lax.broadcasted_iota(jnp.int32, (blk, blk), 0)
    c = lax.broadcasted_iota(jnp.int32, (blk, blk), 1)
    return jnp.where(keep(r, c), 1.0, 0.0).astype(BF16)


def _scan_matrix2(blk, keep, value):
    m = _scan_matrix(blk, keep) * value
    return jnp.concatenate([m, m], axis=0)


def _scan(x, mat2):
    hi, lo = _split_bf16(x)
    return _dot(jnp.concatenate([hi, lo], axis=1), mat2)


def _head_masks(q):
    lane = lax.broadcasted_iota(jnp.int32, q.shape, 1)
    m0 = lane < SB_HEAD_DIM
    zero = jnp.zeros_like(q)
    return m0, jnp.where(m0, q, zero), jnp.where(m0, zero, q)


_LOG2E = 1.4426950408889634


def _softplus_parts(z):
    e = jnp.exp2(jnp.abs(z) * (-_LOG2E))
    ope = 1.0 + e
    return e, ope, jnp.maximum(z, 0.0) + jnp.log(ope)


def _attn_fwd(qkv, rider=None):
    T = qkv.shape[0]
    tk = ATTN_KEY_BLOCK
    tq = min(ATTN_Q_ROWS, T)
    band = tq // tk
    assert band % ATTN_UNROLL == 0 or T == tq
    ngrp = SB_WIDTH // LANES

    def body(q_ref, k_ref, v_ref, o_ref, l_ref, acc, run):
        i = pl.program_id(1)
        suffix = _scan_matrix2(tk, lambda r, c: r >= c, -1.0)
        row = lax.broadcasted_iota(jnp.int32, (tq, tk), 0)
        col = lax.broadcasted_iota(jnp.int32, (tq, tk), 1)
        m0, q0, q1 = _head_masks(q_ref[...] * SB_SCALE)
        acc[...] = jnp.zeros_like(acc)
        run[...] = jnp.zeros_like(run)

        def tiles(work, rows=slice(None)):
            heads = (q0[rows], q1[rows])
            kv = []
            for j, _ in work:
                start = pl.multiple_of(j * tk, tk)
                kv.append((k_ref[pl.ds(start, tk), :], v_ref[pl.ds(start, tk), :]))
            z = [[_dot_nt(qh, kj) for qh in heads] for kj, _ in kv]
            sp = [[_softplus_parts(zz)[2] for zz in zt] for zt in z]
            sp = [[s if m is None else jnp.where(m, s, 0.0) for s in st] for st, (_, m) in zip(sp, work)]
            res = [[_scan(s, suffix) for s in st] for st in sp]
            runs = [run[hd, rows, :] for hd in range(len(heads))]
            a = []
            for t, (_, m) in enumerate(work):
                at = []
                for hd in range(len(heads)):
                    av = jnp.exp(z[t][hd] + (runs[hd] + res[t][hd]))
                    at.append(av if m is None else jnp.where(m, av, 0.0))
                    runs[hd] = runs[hd] + res[t][hd][:, 0:1]
                a.append(at)
            for hd in range(len(heads)):
                run[hd, rows, :] = runs[hd]
                upd = _dot(a[0][hd].astype(BF16), kv[0][1])
                for t in range(1, len(work)):
                    upd = upd + _dot(a[t][hd].astype(BF16), kv[t][1])
                acc[hd, rows, :] += upd

        for jb in reversed(range(band)):
            rows = slice(jb * tk, tq)
            tiles([(i * band + jb, (jb * tk + col < row)[rows])], rows)

        def full_step(it, carry):
            tiles([(i * band - 1 - ATTN_UNROLL * it - u, None) for u in range(ATTN_UNROLL)])
            return carry

        lax.fori_loop(0, i * (band // ATTN_UNROLL), full_step, 0)
        o_ref[...] = jnp.where(m0, acc[0], acc[1]).astype(BF16)
        l_ref[...] = jnp.where(m0, jnp.broadcast_to(run[0], (tq, LANES)), jnp.broadcast_to(run[1], (tq, LANES)))

    return _pallas(
        body, rider, name="attn_fwd", grid=(ngrp, T // tq),
        in_specs=[pl.BlockSpec((tq, LANES), lambda g, i: (i, g)),
                  pl.BlockSpec((T, LANES), lambda g, i: (0, ngrp + g)),
                  pl.BlockSpec((T, LANES), lambda g, i: (0, 2 * ngrp + g))],
        out_specs=[pl.BlockSpec((tq, LANES), lambda g, i: (i, g)),
                   pl.BlockSpec((tq, LANES), lambda g, i: (i, g))],
        out_shape=[jax.ShapeDtypeStruct((T, SB_WIDTH), BF16), jax.ShapeDtypeStruct((T, SB_WIDTH), F32)],
        scratch_shapes=[pltpu.VMEM((2, tq, LANES), F32), pltpu.VMEM((2, tq, 1), F32)],
        compiler_params=_params("arbitrary", "arbitrary"),
    )(qkv, qkv, qkv)


def _attn_bwd(qkv, d_o, ltot, rider=None):
    T = qkv.shape[0]
    tk = ATTN_KEY_BLOCK
    tq = min(ATTN_BWD_Q_ROWS, T)
    band = tq // tk
    nq = T // tq
    ngrp = SB_WIDTH // LANES

    def body(q_ref, k_ref, v_ref, do_ref, l_ref, dq_ref, dk_ref, dv_ref, dq_acc, dk_acc, dv_acc, lpre, ppre):
        i = pl.program_id(1)

        @pl.when(i == 0)
        def _():
            dk_acc[...] = jnp.zeros_like(dk_acc)
            dv_acc[...] = jnp.zeros_like(dv_acc)

        excl = _scan_matrix(tk, lambda r, c: r < c)
        excl2 = jnp.concatenate([excl, excl], axis=0)
        row = lax.broadcasted_iota(jnp.int32, (tq, tk), 0)
        col = lax.broadcasted_iota(jnp.int32, (tq, tk), 1)
        m0, q0, q1 = _head_masks(q_ref[...] * SB_SCALE)
        _, d0, d1 = _head_masks(do_ref[...].astype(BF16))
        lt = l_ref[...]
        ltots = (lt[:, 0:1], lt[:, SB_HEAD_DIM:SB_HEAD_DIM + 1])
        dq_acc[...] = jnp.zeros_like(dq_acc)
        lpre[...] = jnp.zeros_like(lpre)
        ppre[...] = jnp.zeros_like(ppre)

        def tiles(work, rows=slice(None)):
            heads = ((q0[rows], d0[rows]), (q1[rows], d1[rows]))
            lts = [lt[rows] for lt in ltots]
            nhd = len(heads)
            starts = [pl.multiple_of(j * tk, tk) for j, _ in work]
            kv = [(k_ref[pl.ds(st, tk), :], v_ref[pl.ds(st, tk), :]) for st in starts]
            masks = [m for _, m in work]
            every = [(t, hd) for t in range(len(work)) for hd in range(nhd)]
            z = {(t, hd): _dot_nt(heads[hd][0], kv[t][0]) for t, hd in every}
            da = {(t, hd): _dot_nt(heads[hd][1], kv[t][1]) for t, hd in every}
            sp, beta = {}, {}
            for key in every:
                s = _softplus_parts(z[key])[2]
                beta[key] = jnp.exp(z[key] - s)
                sp[key] = s if masks[key[0]] is None else jnp.where(masks[key[0]], s, 0.0)
            res = {key: _scan(sp[key], excl2) for key in every}
            lp = [lpre[hd, rows, :] for hd in range(nhd)]
            a, p = {}, {}
            for t, hd in every:
                av = jnp.exp(z[t, hd] + ((lts[hd] + lp[hd]) + res[t, hd]))
                a[t, hd] = av if masks[t] is None else jnp.where(masks[t], av, 0.0)
                p[t, hd] = a[t, hd] * da[t, hd]
                lp[hd] = lp[hd] + (res[t, hd][:, tk - 1:tk] + sp[t, hd][:, tk - 1:tk])
            resp = {key: _dot(p[key].astype(BF16), excl) for key in every}
            pp = [ppre[hd, rows, :] for hd in range(nhd)]
            dzb = {}
            for t, hd in every:
                dz = p[t, hd] - beta[t, hd] * (p[t, hd] + (pp[hd] + resp[t, hd]))
                if masks[t] is not None:
                    dz = jnp.where(masks[t], dz, 0.0)
                dzb[t, hd] = dz.astype(BF16)
                pp[hd] = pp[hd] + (resp[t, hd][:, tk - 1:tk] + p[t, hd][:, tk - 1:tk])
            for hd in range(nhd):
                lpre[hd, rows, :] = lp[hd]
                ppre[hd, rows, :] = pp[hd]
                upd = _dot(dzb[0, hd], kv[0][0])
                for t in range(1, len(work)):
                    upd = upd + _dot(dzb[t, hd], kv[t][0])
                dq_acc[hd, rows, :] += upd
            for t, st in enumerate(starts):
                dk = _dot_tn(dzb[t, 0], heads[0][0])
                dv = _dot_tn(a[t, 0].astype(BF16), heads[0][1])
                for hd in range(1, nhd):
                    dk = dk + _dot_tn(dzb[t, hd], heads[hd][0])
                    dv = dv + _dot_tn(a[t, hd].astype(BF16), heads[hd][1])
                dk_acc[pl.ds(st, tk), :] += dk
                dv_acc[pl.ds(st, tk), :] += dv

        def full_step(j, carry):
            tiles([(j, None)])
            return carry

        lax.fori_loop(0, i * band, full_step, 0)
        for jb in range(band):
            rows = slice(jb * tk, tq)
            tiles([(i * band + jb, (jb * tk + col < row)[rows])], rows)
        dq_ref[...] = (jnp.where(m0, dq_acc[0], dq_acc[1]) * SB_SCALE).astype(BF16)

        @pl.when(i == nq - 1)
        def _():
            dk_ref[...] = dk_acc[...].astype(BF16)
            dv_ref[...] = dv_acc[...].astype(BF16)

    qmap = lambda g, i: (i, g)
    return _pallas(
        body, rider, name="attn_bwd", grid=(ngrp, nq),
        in_specs=[pl.BlockSpec((tq, LANES), qmap),
                  pl.BlockSpec((T, LANES), lambda g, i: (0, ngrp + g)),
                  pl.BlockSpec((T, LANES), lambda g, i: (0, 2 * ngrp + g)),
                  pl.BlockSpec((tq, LANES), qmap), pl.BlockSpec((tq, LANES), qmap)],
        out_specs=[pl.BlockSpec((tq, LANES), qmap),
                   pl.BlockSpec((T, LANES), lambda g, i: (0, g)),
                   pl.BlockSpec((T, LANES), lambda g, i: (0, g))],
        out_shape=[jax.ShapeDtypeStruct((T, SB_WIDTH), BF16)] * 3,
        scratch_shapes=[pltpu.VMEM((2, tq, LANES), F32), pltpu.VMEM((T, LANES), F32),
                        pltpu.VMEM((T, LANES), F32), pltpu.VMEM((2, tq, 1), F32),
                        pltpu.VMEM((2, tq, 1), F32)],
        compiler_params=_params("arbitrary", "arbitrary"),
    )(qkv, qkv, qkv, d_o, ltot)


def _mix_out_fwd(h, gm, sb, w):
    T, D = h.shape
    tm = min(ROW_TILE, T)

    def body(h_ref, gm_ref, sb_ref, w_ref, o_ref):
        o_ref[...] = h_ref[...] + _dot(gm_ref[...], w_ref[:GM_WIDTH, :]) + _dot(sb_ref[...], w_ref[GM_WIDTH:, :])

    row = lambda i: (i, 0)
    return pl.pallas_call(
        body, name="mix_out_fwd", grid=(T // tm,),
        in_specs=[pl.BlockSpec((tm, D), row), pl.BlockSpec((tm, GM_WIDTH), row), pl.BlockSpec((tm, SB_WIDTH), row),
                  pl.BlockSpec((GM_WIDTH + SB_WIDTH, D), lambda i: (0, 0))],
        out_specs=pl.BlockSpec((tm, D), row),
        out_shape=jax.ShapeDtypeStruct((T, D), F32),
        compiler_params=_params("arbitrary"),
    )(h, gm, sb, w)


def _mix_out_bwd(dh, w):
    T, D = dh.shape
    tm = min(ROW_TILE, T)

    def body(dh_ref, w_ref, dgm_ref, dsb_ref, dhb_ref):
        dhb = dh_ref[...].astype(BF16)
        dhb_ref[...] = dhb
        dgm_ref[...] = _dot_nt(dhb, w_ref[:GM_WIDTH, :])
        dsb_ref[...] = _dot_nt(dhb, w_ref[GM_WIDTH:, :])

    row = lambda i: (i, 0)
    return pl.pallas_call(
        body, name="mix_out_bwd", grid=(T // tm,),
        in_specs=[pl.BlockSpec((tm, D), row), pl.BlockSpec((GM_WIDTH + SB_WIDTH, D), lambda i: (0, 0))],
        out_specs=[pl.BlockSpec((tm, GM_WIDTH), row), pl.BlockSpec((tm, SB_WIDTH), row), pl.BlockSpec((tm, D), row)],
        out_shape=[jax.ShapeDtypeStruct((T, GM_WIDTH), F32), jax.ShapeDtypeStruct((T, SB_WIDTH), F32),
                   jax.ShapeDtypeStruct((T, D), BF16)],
        compiler_params=_params("arbitrary"),
    )(dh, w)


def _tail(h3, p, target, g_ple, g_fin, w_gate, w_proj):
    T, D = h3.shape
    PD = p.shape[1]
    tm = min(ROW_TILE, T)

    def body(h_ref, p_ref, t_ref, gp_ref, gf_ref, wg_ref, wp_ref,
             loss_ref, dh_ref, n4_ref, dgl_ref, dpp_ref, dgp_ref, dgf_ref):
        @pl.when(pl.program_id(0) == 0)
        def _():
            loss_ref[...] = jnp.zeros_like(loss_ref)
            dgp_ref[...] = jnp.zeros_like(dgp_ref)
            dgf_ref[...] = jnp.zeros_like(dgf_ref)

        h3v = h_ref[...]
        gp = gp_ref[...]
        gf = gf_ref[...]
        r3, xh3 = _rms_parts(h3v)
        n4 = (xh3 * gp).astype(BF16)
        n4_ref[...] = n4
        gate = _sigmoid(_dot(n4, wg_ref[...]))
        pp = _dot(p_ref[...].astype(BF16), wp_ref[...])
        h4 = h3v + gate * pp
        r4, xh4 = _rms_parts(h4)
        err = xh4 * gf - t_ref[...]
        loss_ref[...] += jnp.full(loss_ref.shape, (0.5 / D) * jnp.sum(err * err), F32)
        dy = err * (1.0 / D)
        dgf_ref[...] += jnp.sum(dy * xh4, axis=0, keepdims=True)
        dyg = dy * gf
        dh4 = r4 * (dyg - xh4 * jnp.mean(dyg * xh4, axis=-1, keepdims=True))
        dpp_ref[...] = (dh4 * gate).astype(BF16)
        dgl = (dh4 * pp * gate * (1.0 - gate)).astype(BF16)
        dgl_ref[...] = dgl
        dn4 = _dot_nt(dgl, wg_ref[...])
        dgp_ref[...] += jnp.sum(dn4 * xh3, axis=0, keepdims=True)
        dn4g = dn4 * gp
        dh_ref[...] = dh4 + r3 * (dn4g - xh3 * jnp.mean(dn4g * xh3, axis=-1, keepdims=True))

    row = lambda i: (i, 0)
    one = lambda i: (0, 0)
    return pl.pallas_call(
        body, name="tail", grid=(T // tm,),
        in_specs=[pl.BlockSpec((tm, D), row), pl.BlockSpec((tm, PD), row), pl.BlockSpec((tm, D), row),
                  pl.BlockSpec((1, D), one), pl.BlockSpec((1, D), one),
                  pl.BlockSpec((D, D), one), pl.BlockSpec((PD, D), one)],
        out_specs=[pl.BlockSpec((1, LANES), one), pl.BlockSpec((tm, D), row), pl.BlockSpec((tm, D), row),
                   pl.BlockSpec((tm, D), row), pl.BlockSpec((tm, D), row),
                   pl.BlockSpec((1, D), one), pl.BlockSpec((1, D), one)],
        out_shape=[jax.ShapeDtypeStruct((1, LANES), F32), jax.ShapeDtypeStruct((T, D), F32),
                   jax.ShapeDtypeStruct((T, D), BF16), jax.ShapeDtypeStruct((T, D), BF16),
                   jax.ShapeDtypeStruct((T, D), BF16),
                   jax.ShapeDtypeStruct((1, D), F32), jax.ShapeDtypeStruct((1, D), F32)],
        compiler_params=_params("arbitrary"),
    )(h3, p, target, g_ple, g_fin, w_gate, w_proj)


FFN1_W = ("ffn1_w_in", "ffn1_w_out")
MIX_W = ("w_mix_in", "w_mix_out")
REST_W = ("ffn2_w_in", "ffn2_w_out", "ple_w_gate", "ple_w_proj")
BIG_W = FFN1_W + MIX_W + REST_W
COLUMN_SHARDED = ("w_mix_in", "ple_w_proj")


class _Traffic:
    def __init__(self, shards):
        self.shards = shards
        self.parts = {}
        self.blocks, self.sums = {}, {}

    @staticmethod
    def _full(name, gathered):
        if name in COLUMN_SHARDED:
            return jnp.transpose(gathered, (1, 0, 2)).reshape(gathered.shape[1], -1)
        if name.endswith("_w_in"):
            return gathered
        return gathered.reshape(-1, gathered.shape[-1])

    @staticmethod
    def _blocks(name, grad):
        name = name.split("/")[0]
        if name in COLUMN_SHARDED:
            return jnp.transpose(grad.reshape(grad.shape[0], N_DEV, -1), (1, 0, 2))
        if name.endswith("_w_in"):
            return grad
        return grad.reshape(N_DEV, -1, grad.shape[-1])

    def gather_now(self, names):
        got = _exchange("gather_" + names[0], [self.shards[n] for n in names], [GATHER] * len(names))
        return self.gathered(names, got)

    def gather_rider(self, names):
        return [self.shards[n] for n in names], [GATHER] * len(names)

    def gathered(self, names, got):
        return {n: self._full(n, g) for n, g in zip(names, got)}

    def scatter_rider(self, grads, gather=()):
        return ([self._blocks(n, g) for n, g in grads.items()] + list(gather),
                [SCATTER] * len(grads) + [GATHER] * len(gather))

    def scattered(self, names, got):
        self.parts.update(zip(names, got))
        return got[len(names):]


    def pair_rider(self, name, grad):
        self.blocks[name] = self._blocks(name, grad)
        return [self.blocks[name]], [PAIR]

    def paired(self, name, got):
        self.sums[name] = _stage_pair_sums("pair_sum_" + name.replace("/", "_"), self.blocks[name], got)

    def chip_rider(self, name):
        return [self.sums[name]], [CHIP]

    def chipped(self, name, got):
        self.parts[name] = got


def _local_step(traffic, x, p, target, g1, gmix, gv, ws, b_t, g2, gple, gfin, pack_small):
    T, D = x.shape
    tm = min(ROW_TILE, T)

    w = traffic.gather_now(FFN1_W)
    h1, n1, G1, U1, a1, *got = _ffn_fwd("ffn1_fwd", x, g1, w["ffn1_w_in"], w["ffn1_w_out"],
                                        rider=traffic.gather_rider(MIX_W))
    w.update(traffic.gathered(MIX_W, got))
    n2, zuv, qkv = _mix_in_fwd(h1, gmix, w["w_mix_in"])
    gm = _gmlp_fwd(zuv, gv, ws, b_t)
    sb, ltot, *got = _attn_fwd(qkv, rider=traffic.gather_rider(REST_W))
    w.update(traffic.gathered(REST_W, got))
    h2 = _mix_out_fwd(h1, gm, sb, w["w_mix_out"])
    h3, n3, G2, U2, a2 = _ffn_fwd("ffn2_fwd", h2, g2, w["ffn2_w_in"], w["ffn2_w_out"])
    loss, dh3, n4, d_gl, d_pp, dg_ple, dg_fin = _tail(h3, p, target, gple, gfin, w["ple_w_gate"], w["ple_w_proj"])

    nb, _, FB = w["ffn1_w_in"].shape
    nh = nb // 2

    tt = min(GRAD_ROW_TILE, T)

    def dw_out(name, a, d_out):
        return _matmul_tn(name, a, d_out, nh, (1, tt, FB), lambda j, t: (j, t, 0), (tt, D), lambda j, t: (t, 0),
                          (nh, FB, D), (1, FB, D), lambda j, t: (j, 0, 0))

    def dense_tn(name, a, b, ncol):
        ka, nbw = a.shape[1], b.shape[1] // ncol
        return _matmul_tn(name, a, b, ncol, (tt, ka), lambda j, t: (t, 0), (tt, nbw), lambda j, t: (t, j),
                          (ka, b.shape[1]), (ka, nbw), lambda j, t: (0, j))

    grads = dict(ple_w_gate=dense_tn("dw_ple_gate", n4, d_gl, 2), ple_w_proj=dense_tn("dw_ple_proj", p, d_pp, 1))
    dh2, dg2, dG2, dU2, dout2 = _ffn_bwd("ffn2_bwd", dh3, h2, g2, G2, U2, w["ffn2_w_in"], w["ffn2_w_out"])
    grads["ffn2_w_in"], = _dw_in("ffn2_dw_in", n3, dG2, dU2)
    grads["ffn2_w_out"] = dw_out("ffn2_dw_out", a2, dout2)
    grads = {n: grads[n] for n in REST_W}

    d_gm, d_sb, dh2_bf = _mix_out_bwd(dh2, w["w_mix_out"])
    grads["w_mix_out"] = jnp.concatenate([dense_tn("dw_mix_out_gm", gm, dh2_bf, 1),
                                          dense_tn("dw_mix_out_sb", sb, dh2_bf, 1)], axis=0)
    dzuv, dgv, dws, db_t = _gmlp_bwd(zuv, d_gm, gv, ws, b_t)
    dq, dk, dv, *got = _attn_bwd(qkv, d_sb, ltot, rider=traffic.scatter_rider(grads))
    traffic.scattered(list(grads), got)
    dqkv = jnp.concatenate([dq, dk, dv], axis=1)
    dw_mi = jnp.concatenate([dense_tn("dw_mix_in_uv", n2, dzuv, 2), dense_tn("dw_mix_in_qkv", n2, dqkv, 3)], axis=1)
    dh1, dgmix, dout1 = _mix_in_bwd(dzuv, dqkv, w["w_mix_in"], h1, gmix, dh2)

    def dw_out_riding(name, a, d_out, rider):
        return _matmul_tn(name, a, d_out, nh, (1, tt, FB), lambda j, t: (j, t, 0), (tt, D), lambda j, t: (t, 0),
                          (nh, FB, D), (1, FB, D), lambda j, t: (j, 0, 0), rider=rider)

    def both(*riders):
        return [x for r in riders for x in r[0]], [k for r in riders for k in r[1]]

    dw_out1, got = dw_out_riding("ffn1_dw_out", a1, dout1, traffic.pair_rider("w_mix_in", dw_mi))
    traffic.paired("w_mix_in", got)
    dG1, dU1, got_mi, got = _ffn_bwd_gates("ffn1_bwd_gates", dout1, G1, U1, w["ffn1_w_out"],
                                           rider=both(traffic.chip_rider("w_mix_in"),
                                                      traffic.pair_rider("ffn1_w_out", dw_out1)))
    traffic.chipped("w_mix_in", got_mi)
    traffic.paired("ffn1_w_out", got)
    half = D // 2
    top, got = _dw_in("ffn1_dw_in_top", n1[:, :half], dG1, dU1, rider=traffic.chip_rider("ffn1_w_out"))
    traffic.chipped("ffn1_w_out", got)
    bottom, got = _dw_in("ffn1_dw_in_bottom", n1[:, half:], dG1, dU1, rider=traffic.pair_rider("ffn1_w_in/0", top))
    traffic.paired("ffn1_w_in/0", got)
    traffic.paired("ffn1_w_in/1", _exchange("pair_last", *traffic.pair_rider("ffn1_w_in/1", bottom))[0])
    dx, dg1, got_top, got_bottom = _ffn_bwd_input("ffn1_bwd_input", dh1, x, g1, dG1, dU1, w["ffn1_w_in"],
                                                  rider=both(traffic.chip_rider("ffn1_w_in/0"),
                                                             traffic.chip_rider("ffn1_w_in/1")))
    traffic.chipped("ffn1_w_in/0", got_top)
    traffic.chipped("ffn1_w_in/1", got_bottom)

    small = pack_small(dict(ffn1_norm=dg1, mix_norm=dgmix, gmlp_v_norm=dgv, gmlp_w_s=dws, gmlp_b=jnp.transpose(db_t),
                            ffn2_norm=dg2, ple_norm=dg_ple, final_norm=dg_fin), loss)
    return dx, small


def _my_index():
    return 4 * lax.axis_index("x") + 2 * lax.axis_index("y") + lax.axis_index("c")


def _stage_pair_sums(name, blocks, got):
    _, R, C = blocks.shape
    me = _my_index()
    index = jnp.stack([jnp.bitwise_xor(me, 2 * c) for c in range(1, N_CHIPS_AWAY + 1)] + [me, me]).astype(jnp.int32)

    def body(index_ref, b_ref, g_ref, o_ref):
        i = pl.program_id(0)

        @pl.when(i < N_CHIPS_AWAY)
        def _():
            o_ref[...] = (b_ref[...].astype(F32) + g_ref[...].astype(F32)).astype(BF16)

        @pl.when(i == N_CHIPS_AWAY)
        def _():
            o_ref[...] = g_ref[...]

        @pl.when(i == N_CHIPS_AWAY + 1)
        def _():
            o_ref[...] = b_ref[...]

    return pl.pallas_call(
        body, name=name,
        grid_spec=pltpu.PrefetchScalarGridSpec(
            num_scalar_prefetch=1, grid=(PART_SLOTS,),
            in_specs=[pl.BlockSpec((1, R, C), lambda i, idx: (idx[i], 0, 0)),
                      pl.BlockSpec((1, R, C), lambda i, idx: (jnp.minimum(i, N_CHIPS_AWAY), 0, 0))],
            out_specs=pl.BlockSpec((1, R, C), lambda i, idx: (i, 0, 0))),
        out_shape=jax.ShapeDtypeStruct((PART_SLOTS, R, C), BF16), compiler_params=_params("arbitrary"),
    )(index, blocks, got)


def _peer(d):
    x, y, c = lax.axis_index("x"), lax.axis_index("y"), lax.axis_index("c")
    px = 1 - x if d & 4 else x
    py = 1 - y if d & 2 else y
    pc = 1 - c if d & 1 else c
    return (px, py, pc), 4 * px + 2 * py + pc


GATHER, SCATTER, PAIR, CHIP = "gather", "scatter", "pair", "chip"


class _ExchangePlan:
    def __init__(self, ins, outs, send, recv, local, kinds):
        self.ins, self.outs, self.send, self.recv, self.local, self.kinds = ins, outs, send, recv, local, kinds
        self.scatter = [k == SCATTER for k in kinds]
        self.me = _peer(0)[1]

    def _remote(self, t, sem, src, slot, peer):
        return pltpu.make_async_remote_copy(
            src_ref=src, dst_ref=self.outs[t].at[slot], send_sem=self.send.at[t, sem], recv_sem=self.recv.at[t, sem],
            device_id=peer, device_id_type=MESH)

    def _own(self, t):
        if self.kinds[t] == CHIP:
            kept = pl.ds(N_CHIPS_AWAY, PART_SLOTS - N_CHIPS_AWAY)
            return pltpu.make_async_copy(self.ins[t].at[kept], self.outs[t].at[kept], self.local.at[t])
        src = self.ins[t].at[self.me] if self.scatter[t] else self.ins[t]
        return pltpu.make_async_copy(src, self.outs[t].at[self.me], self.local.at[t])

    def _n_direct(self, t):
        return {SCATTER: N_DEV - 1, GATHER: N_CHIPS_AWAY + 1, PAIR: N_CHIPS_AWAY + 1, CHIP: N_CHIPS_AWAY}[self.kinds[t]]

    def _direct(self, t, k):
        kind = self.kinds[t]
        if kind == SCATTER:
            peer, slot = _peer(k + 1)
            return self._remote(t, k, self.ins[t].at[slot], self.me, peer)
        if kind == GATHER:
            return self._remote(t, k, self.ins[t], self.me, _peer(2 * k if k else 1)[0])
        if kind == PAIR:
            block = _peer(2 * (k + 1) + 1 if k < N_CHIPS_AWAY else 1)[1]
            return self._remote(t, k, self.ins[t].at[block], k, _peer(1)[0])
        return self._remote(t, k, self.ins[t].at[k], k, _peer(2 * (k + 1))[0])

    def _has_own(self, t):
        return self.kinds[t] != PAIR

    def _relay(self, t, c):
        slot = _peer(2 * c)[1]
        return self._remote(t, N_CHIPS_AWAY + c, self.outs[t].at[slot], slot, _peer(1)[0])

    def start(self):
        for t in range(len(self.ins)):
            if self._has_own(t):
                self._own(t).start()
            for k in range(self._n_direct(t)):
                self._direct(t, k).start()

    def relay(self):
        for t in self._gathers():
            for c in range(1, N_CHIPS_AWAY + 1):
                self._direct(t, c).wait_recv()
                self._relay(t, c).start()

    def _gathers(self):
        return [t for t in range(len(self.ins)) if self.kinds[t] == GATHER]

    def finish(self):
        for t in range(len(self.ins)):
            if self._has_own(t):
                self._own(t).wait()
            for k in range(self._n_direct(t)):
                self._direct(t, k).wait_send()
                if self.kinds[t] != GATHER or k == 0:
                    self._direct(t, k).wait_recv()
        for t in self._gathers():
            for c in range(1, N_CHIPS_AWAY + 1):
                self._relay(t, c).wait()


def _exchange_shapes(arrays, kinds):
    shape = {GATHER: lambda a: (N_DEV,) + a.shape, SCATTER: lambda a: a.shape, CHIP: lambda a: a.shape,
             PAIR: lambda a: (N_CHIPS_AWAY + 1,) + a.shape[1:]}
    return [jax.ShapeDtypeStruct(shape[k](a), a.dtype) for a, k in zip(arrays, kinds)]


def _exchange_sems(n):
    return [pltpu.SemaphoreType.DMA((n, N_DEV - 1)), pltpu.SemaphoreType.DMA((n, N_DEV - 1)),
            pltpu.SemaphoreType.DMA((n,))]


_ANY = pl.BlockSpec(memory_space=pl.ANY)


def _exchange(name, arrays, scatter):
    n = len(arrays)

    def body(*refs):
        plan = _ExchangePlan(refs[:n], refs[n:2 * n], *refs[2 * n:], scatter)
        plan.start()
        plan.relay()
        plan.finish()

    return pl.pallas_call(
        body, name=name, in_specs=[_ANY] * n, out_specs=[_ANY] * n, out_shape=_exchange_shapes(arrays, scatter),
        scratch_shapes=_exchange_sems(n),
    )(*arrays)


def _pallas(body, rider, *, name, grid, in_specs, out_specs, out_shape, scratch_shapes=(), compiler_params=None):
    if rider is None:
        return pl.pallas_call(body, name=name, grid=grid, in_specs=in_specs, out_specs=out_specs, out_shape=out_shape,
                              scratch_shapes=list(scratch_shapes), compiler_params=compiler_params)
    arrays, scatter = rider
    n, ni, no, ns = len(arrays), len(in_specs), len(out_specs), len(scratch_shapes)

    def carried(*refs):
        ins, r_in = refs[:ni], refs[ni:ni + n]
        outs, r_out = refs[ni + n:ni + n + no], refs[ni + n + no:ni + 2 * n + no]
        scratch, sems = refs[ni + 2 * n + no:ni + 2 * n + no + ns], refs[ni + 2 * n + no + ns:]
        step = 0
        for ax, g in enumerate(grid):
            step = step * g + pl.program_id(ax)
        steps = functools.reduce(lambda a, b: a * b, grid)

        @pl.when(step == 0)
        def _():
            _ExchangePlan(r_in, r_out, *sems, scatter).start()

        @pl.when(step == steps // 2)
        def _():
            _ExchangePlan(r_in, r_out, *sems, scatter).relay()

        body(*ins, *outs, *scratch)

        @pl.when(step == steps - 1)
        def _():
            _ExchangePlan(r_in, r_out, *sems, scatter).finish()

    call = pl.pallas_call(
        carried, name=name, grid=grid, in_specs=list(in_specs) + [_ANY] * n, out_specs=list(out_specs) + [_ANY] * n,
        out_shape=list(out_shape) + _exchange_shapes(arrays, scatter),
        scratch_shapes=list(scratch_shapes) + _exchange_sems(n), compiler_params=compiler_params)
    return lambda *args: call(*args, *arrays)


def _adamw_math(g, w, m, v):
    m_new = ADAM_B1 * m + (1.0 - ADAM_B1) * g
    v_new = ADAM_B2 * v + (1.0 - ADAM_B2) * (g * g)
    m_hat = m_new / (1.0 - ADAM_B1 ** ADAM_STEP)
    v_hat = v_new / (1.0 - ADAM_B2 ** ADAM_STEP)
    delta = -ADAM_LR * (m_hat / (jnp.sqrt(v_hat) + ADAM_EPS) + ADAM_WD * w)
    return delta, m_new, v_new


def _adamw(name, parts, w, m, v, rider=None):
    R, C = w.shape
    slots = parts.shape[0]
    tr = R
    for cand in (256, 128, 64, 32, 16, 8):
        if R % cand == 0:
            tr = cand
            break

    def body(p_ref, w_ref, m_ref, v_ref, g_ref, d_ref, nm_ref, nv_ref):
        g = p_ref[0].astype(F32)
        for j in range(1, slots):
            g = g + p_ref[j].astype(F32)
        g_ref[...] = g
        d_ref[...], nm_ref[...], nv_ref[...] = _adamw_math(g, w_ref[...], m_ref[...], v_ref[...])

    row = lambda i: (i, 0)
    spec = pl.BlockSpec((tr, C), row)
    return _pallas(
        body, rider, name=name, grid=(R // tr,),
        in_specs=[pl.BlockSpec((slots, tr, C), lambda i: (0, i, 0)), spec, spec, spec],
        out_specs=[spec] * 4,
        out_shape=[jax.ShapeDtypeStruct((R, C), F32)] * 4,
        compiler_params=_params("arbitrary"),
    )(parts, w, m, v)


def _rows128(a):
    flat = a.reshape(-1, LANES)
    pad = (-flat.shape[0]) % SMALL_ROWS_ALIGN
    return jnp.pad(flat, ((0, pad), (0, 0))) if pad else flat


def _unrows(packed, like):
    n = like.size // LANES
    return packed[:n].reshape(like.shape)


def kernel(x, p, ffn1_norm, ffn1_w_in, ffn1_w_out, mix_norm, w_mix_in, gmlp_v_norm, gmlp_w_s, gmlp_b, w_mix_out, ffn2_norm, ffn2_w_in, ffn2_w_out, ple_norm, ple_w_gate, ple_w_proj, final_norm, loss_target, m_ffn1_norm, m_ffn1_w_in, m_ffn1_w_out, m_mix_norm, m_w_mix_in, m_gmlp_v_norm, m_gmlp_w_s, m_gmlp_b, m_w_mix_out, m_ffn2_norm, m_ffn2_w_in, m_ffn2_w_out, m_ple_norm, m_ple_w_gate, m_ple_w_proj, m_final_norm, v_ffn1_norm, v_ffn1_w_in, v_ffn1_w_out, v_mix_norm, v_w_mix_in, v_gmlp_v_norm, v_gmlp_w_s, v_gmlp_b, v_w_mix_out, v_ffn2_norm, v_ffn2_w_in, v_ffn2_w_out, v_ple_norm, v_ple_w_gate, v_ple_w_proj, v_final_norm):
    names = ["ffn1_norm", "ffn1_w_in", "ffn1_w_out", "mix_norm", "w_mix_in", "gmlp_v_norm", "gmlp_w_s", "gmlp_b",
             "w_mix_out", "ffn2_norm", "ffn2_w_in", "ffn2_w_out", "ple_norm", "ple_w_gate", "ple_w_proj", "final_norm"]
    W = dict(zip(names, [ffn1_norm, ffn1_w_in, ffn1_w_out, mix_norm, w_mix_in, gmlp_v_norm, gmlp_w_s, gmlp_b,
                         w_mix_out, ffn2_norm, ffn2_w_in, ffn2_w_out, ple_norm, ple_w_gate, ple_w_proj, final_norm]))
    M = dict(zip(names, [m_ffn1_norm, m_ffn1_w_in, m_ffn1_w_out, m_mix_norm, m_w_mix_in, m_gmlp_v_norm, m_gmlp_w_s,
                         m_gmlp_b, m_w_mix_out, m_ffn2_norm, m_ffn2_w_in, m_ffn2_w_out, m_ple_norm, m_ple_w_gate,
                         m_ple_w_proj, m_final_norm]))
    V = dict(zip(names, [v_ffn1_norm, v_ffn1_w_in, v_ffn1_w_out, v_mix_norm, v_w_mix_in, v_gmlp_v_norm, v_gmlp_w_s,
                         v_gmlp_b, v_w_mix_out, v_ffn2_norm, v_ffn2_w_in, v_ffn2_w_out, v_ple_norm, v_ple_w_gate,
                         v_ple_w_proj, v_final_norm]))
    small = [n for n in names if n not in BIG_W]
    D = x.shape[-1]

    def pack(src, last):
        return jnp.concatenate([_rows128(src[n]) for n in small] + [last], axis=0)

    offs = [0]
    for n in small:
        offs.append(offs[-1] + _rows128(W[n]).shape[0])

    traffic = _Traffic({n: W[n][0].astype(BF16) for n in BIG_W})
    dx, small_mine = _local_step(
        traffic, x[0], p[0, 0], loss_target[0],
        W["ffn1_norm"], W["mix_norm"], W["gmlp_v_norm"], W["gmlp_w_s"][0], jnp.transpose(W["gmlp_b"][0]),
        W["ffn2_norm"], W["ple_norm"], W["final_norm"].reshape(1, D),
        lambda grads, loss_part: pack(grads, jnp.broadcast_to(loss_part, (SMALL_ROWS_ALIGN, LANES))))

    out = {}
    parts = traffic.parts
    parts["ffn1_w_in"] = jnp.concatenate([parts["ffn1_w_in/0"], parts["ffn1_w_in/1"]], axis=1)
    carrier = "ffn2_w_out"
    *out[carrier], small_parts = _adamw("adamw_" + carrier, parts[carrier], W[carrier][0], M[carrier][0], V[carrier][0],
                                        rider=([small_mine], [GATHER]))
    for n in BIG_W:
        if n != carrier:
            out[n] = _adamw("adamw_" + n, parts[n], W[n][0], M[n][0], V[n][0])
    zeros = jnp.zeros((SMALL_ROWS_ALIGN, LANES), F32)
    sg, sd, sm, sv = _adamw("adamw_small", small_parts, pack(W, zeros), pack(M, zeros), pack(V, zeros))
    for k, n in enumerate(small):
        out[n] = tuple(_unrows(arr[offs[k]:offs[k + 1]], W[n]) for arr in (sg, sd, sm, sv))
    loss = sg[offs[len(small)], 0]

    res = [loss, dx[None]]
    for k in range(4):
        res += [out[n][k].reshape(W[n].shape) for n in names]
    return tuple(res)
```

```python
import functools

import jax
import jax.numpy as jnp
from jax import lax
from jax.experimental import pallas as pl
from jax.experimental.pallas import tpu as pltpu

F32 = jnp.float32
BF16 = jnp.bfloat16
MESH = pl.DeviceIdType.MESH

N_DEV = 8
N_CHIPS_AWAY = 3
PART_SLOTS = N_CHIPS_AWAY + 2
EPS = 1e-6
ADAM_LR = 0.001
ADAM_B1 = 0.9
ADAM_B2 = 0.999
ADAM_EPS = 1e-08
ADAM_WD = 0.01
ADAM_STEP = 10

GM_WIDTH = 512
GM_HEADS = 4
CHUNK = 128
SB_WIDTH = 512
SB_HEAD_DIM = 64
SB_SCALE = 0.125
LANES = 128
SMALL_ROWS_ALIGN = 8

ROW_TILE = 512
GRAD_ROW_TILE = 2048
DW_IN_ROW_TILE = 4096
FFN_FWD_ROW_TILE = 1024
ATTN_Q_ROWS = 512
ATTN_BWD_Q_ROWS = 512
ATTN_KEY_BLOCK = 256
ATTN_UNROLL = 2
VMEM_LIMIT = 56 * 1024 * 1024


def _params(*sem):
    return pltpu.CompilerParams(dimension_semantics=sem, vmem_limit_bytes=VMEM_LIMIT)


def _dot(a, b):
    return jnp.dot(a, b, preferred_element_type=F32)


def _dot_nt(a, b):
    return lax.dot_general(a, b, (((1,), (1,)), ((), ())), preferred_element_type=F32)


def _dot_tn(a, b):
    return lax.dot_general(a, b, (((0,), (0,)), ((), ())), preferred_element_type=F32)


def _rms_parts(x):
    r = lax.rsqrt(jnp.mean(x * x, axis=-1, keepdims=True) + EPS)
    return r, x * r


def _rms_bwd(x, g, dy):
    r, xh = _rms_parts(x)
    dyg = dy * g
    dx = r * (dyg - xh * jnp.mean(dyg * xh, axis=-1, keepdims=True))
    return dx, jnp.sum(dy * xh, axis=0, keepdims=True)


def _sigmoid(x):
    return 1.0 / (1.0 + jnp.exp(-x))


_SQRT_HALF = 0.7071067811865476
_INV_SQRT_2PI = 0.3989422804014327


def _gelu(x):
    return 0.5 * x * (1.0 + lax.erf(x * _SQRT_HALF))


def _gelu_grad(x):
    return 0.5 * (1.0 + lax.erf(x * _SQRT_HALF)) + x * (_INV_SQRT_2PI * jnp.exp(-0.5 * x * x))


def _split_bf16(x):
    hi = x.astype(BF16)
    lo = (x - hi.astype(F32)).astype(BF16)
    return hi, lo


def _ffn_fwd(name, h, gain, w_in, w_out, rider=None):
    T, D = h.shape
    nb, _, FB = w_in.shape
    nh = nb // 2
    tm = min(FFN_FWD_ROW_TILE, T)

    def body(h_ref, g_ref, wg_ref, wu_ref, wo_ref, ho_ref, n_ref, G_ref, U_ref, a_ref, n_s, acc):
        jj = pl.program_id(1)

        @pl.when(jj == 0)
        def _():
            _, xh = _rms_parts(h_ref[...])
            n = (xh * g_ref[...]).astype(BF16)
            n_s[...] = n
            n_ref[...] = n
            acc[...] = jnp.zeros_like(acc)

        n = n_s[...]
        G = _dot(n, wg_ref[0])
        U = _dot(n, wu_ref[0])
        G_ref[0] = G.astype(BF16)
        U_ref[0] = U.astype(BF16)
        a = (G * _sigmoid(G) * U).astype(BF16)
        a_ref[0] = a
        acc[...] += _dot(a, wo_ref[...])

        @pl.when(jj == nh - 1)
        def _():
            ho_ref[...] = h_ref[...] + 0.5 * acc[...]

    row = lambda i, j: (i, 0)
    blk = lambda i, j: (j, i, 0)
    return _pallas(
        body, rider, name=name, grid=(T // tm, nh),
        in_specs=[pl.BlockSpec((tm, D), row),
                  pl.BlockSpec((1, D), lambda i, j: (0, 0)),
                  pl.BlockSpec((1, D, FB), lambda i, j: (j, 0, 0)),
                  pl.BlockSpec((1, D, FB), lambda i, j: (j + nh, 0, 0)),
                  pl.BlockSpec((FB, D), lambda i, j: (j, 0))],
        out_specs=[pl.BlockSpec((tm, D), row), pl.BlockSpec((tm, D), row),
                   pl.BlockSpec((1, tm, FB), blk), pl.BlockSpec((1, tm, FB), blk),
                   pl.BlockSpec((1, tm, FB), blk)],
        out_shape=[jax.ShapeDtypeStruct((T, D), F32), jax.ShapeDtypeStruct((T, D), BF16),
                   jax.ShapeDtypeStruct((nh, T, FB), BF16), jax.ShapeDtypeStruct((nh, T, FB), BF16),
                   jax.ShapeDtypeStruct((nh, T, FB), BF16)],
        scratch_shapes=[pltpu.VMEM((tm, D), BF16), pltpu.VMEM((tm, D), F32)],
        compiler_params=_params("arbitrary", "arbitrary"),
    )(h, gain, w_in, w_in, w_out)


def _gate_grads(dav, Gv, Uv):
    sig = _sigmoid(Gv)
    return (dav * Uv * (sig * (1.0 + Gv * (1.0 - sig)))).astype(BF16), (dav * (Gv * sig)).astype(BF16)


def _ffn_bwd_gates(name, d_out, G, U, w_out, rider=None):
    T, D = d_out.shape
    nh, _, FB = G.shape
    tm = min(FFN_FWD_ROW_TILE, T)

    def body(do_ref, G_ref, U_ref, wo_ref, dG_ref, dU_ref):
        halves = [slice(0, tm // 2), slice(tm // 2, tm)]
        da = [_dot_nt(do_ref[rows, :], wo_ref[...]) for rows in halves]
        for rows, dav in zip(halves, da):
            dG_ref[0, rows, :], dU_ref[0, rows, :] = _gate_grads(
                dav, G_ref[0, rows, :].astype(F32), U_ref[0, rows, :].astype(F32))

    blk = lambda i, j: (j, i, 0)
    return _pallas(
        body, rider, name=name, grid=(T // tm, nh),
        in_specs=[pl.BlockSpec((tm, D), lambda i, j: (i, 0)), pl.BlockSpec((1, tm, FB), blk),
                  pl.BlockSpec((1, tm, FB), blk), pl.BlockSpec((FB, D), lambda i, j: (j, 0))],
        out_specs=[pl.BlockSpec((1, tm, FB), blk), pl.BlockSpec((1, tm, FB), blk)],
        out_shape=[jax.ShapeDtypeStruct((nh, T, FB), BF16), jax.ShapeDtypeStruct((nh, T, FB), BF16)],
        compiler_params=_params("arbitrary", "arbitrary"),
    )(d_out, G, U, w_out)


def _ffn_bwd_input(name, dh, h_in, gain, dG, dU, w_in, rider=None):
    T, D = dh.shape
    nb, _, FB = w_in.shape
    nh = nb // 2
    tm = min(FFN_FWD_ROW_TILE, T)

    def body(dh_ref, h_ref, g_ref, dG_ref, dU_ref, wg_ref, wu_ref, dhin_ref, dg_ref, dn_acc):
        i = pl.program_id(0)
        jj = pl.program_id(1)

        @pl.when(jj == 0)
        def _():
            dn_acc[...] = jnp.zeros_like(dn_acc)

        @pl.when((i == 0) & (jj == 0))
        def _():
            dg_ref[...] = jnp.zeros_like(dg_ref)

        halves = [slice(0, tm // 2), slice(tm // 2, tm)]
        dn = [_dot_nt(dG_ref[0, rows, :], wg_ref[0]) for rows in halves]
        dn = [d + _dot_nt(dU_ref[0, rows, :], wu_ref[0]) for d, rows in zip(dn, halves)]
        for rows, d in zip(halves, dn):
            dn_acc[rows, :] += d

        @pl.when(jj == nh - 1)
        def _():
            dx, dg = _rms_bwd(h_ref[...], g_ref[...], dn_acc[...])
            dhin_ref[...] = dh_ref[...] + dx
            dg_ref[...] += dg

    row = lambda i, j: (i, 0)
    blk = lambda i, j: (j, i, 0)
    one = lambda i, j: (0, 0)
    return _pallas(
        body, rider, name=name, grid=(T // tm, nh),
        in_specs=[pl.BlockSpec((tm, D), row), pl.BlockSpec((tm, D), row), pl.BlockSpec((1, D), one),
                  pl.BlockSpec((1, tm, FB), blk), pl.BlockSpec((1, tm, FB), blk),
                  pl.BlockSpec((1, D, FB), lambda i, j: (j, 0, 0)),
                  pl.BlockSpec((1, D, FB), lambda i, j: (j + nh, 0, 0))],
        out_specs=[pl.BlockSpec((tm, D), row), pl.BlockSpec((1, D), one)],
        out_shape=[jax.ShapeDtypeStruct((T, D), F32), jax.ShapeDtypeStruct((1, D), F32)],
        scratch_shapes=[pltpu.VMEM((tm, D), F32)],
        compiler_params=_params("arbitrary", "arbitrary"),
    )(dh, h_in, gain, dG, dU, w_in, w_in)


def _matmul_tn(name, a, b, nj, a_block, a_map, b_block, b_map, out_shape, out_block, out_map, rider=None):
    T = a.shape[-2]
    tt = a_block[-2]
    nt = T // tt
    kb, nbk = out_block[-2], out_block[-1]

    def body(a_ref, b_ref, o_ref, acc):
        t = pl.program_id(1)

        @pl.when(t == 0)
        def _():
            acc[...] = jnp.zeros_like(acc)

        av = (a_ref[0] if len(a_block) == 3 else a_ref[...]).astype(BF16)
        bv = b_ref[0] if len(b_block) == 3 else b_ref[...]
        acc[...] += _dot_tn(av, bv)

        @pl.when(t == nt - 1)
        def _():
            if len(out_block) == 3:
                o_ref[0] = acc[...].astype(o_ref.dtype)
            else:
                o_ref[...] = acc[...].astype(o_ref.dtype)

    got = _pallas(
        body, rider, name=name, grid=(nj, nt),
        in_specs=[pl.BlockSpec(a_block, a_map), pl.BlockSpec(b_block, b_map)],
        out_specs=[pl.BlockSpec(out_block, out_map)],
        out_shape=[jax.ShapeDtypeStruct(out_shape, BF16)],
        scratch_shapes=[pltpu.VMEM((kb, nbk), F32)],
        compiler_params=_params("arbitrary", "arbitrary"),
    )(a, b)
    return got[0] if rider is None else got


def _dw_in(name, n, dG, dU, rider=None):
    T, kr = n.shape
    nh, _, FB = dG.shape
    tt = min(DW_IN_ROW_TILE, T)
    nt = T // tt
    cut = LANES * ((kr // LANES + 1) // 2)

    def body(n_ref, dg_ref, du_ref, o_ref, acc):
        j = pl.program_id(0)
        t = pl.program_id(1)

        @pl.when(t == 0)
        def _():
            acc[...] = jnp.zeros_like(acc)

        def add(dz_ref):
            for rows in ((slice(0, cut), slice(cut, kr)) if cut < kr else (slice(0, kr),)):
                acc[rows, :] += _dot_tn(n_ref[:, rows], dz_ref[0])

        @pl.when(j < nh)
        def _():
            add(dg_ref)

        @pl.when(j >= nh)
        def _():
            add(du_ref)

        @pl.when(t == nt - 1)
        def _():
            o_ref[0] = acc[...].astype(BF16)

    return _pallas(
        body, rider, name=name, grid=(2 * nh, nt),
        in_specs=[pl.BlockSpec((tt, kr), lambda j, t: (t, 0)),
                  pl.BlockSpec((1, tt, FB), lambda j, t: (jnp.minimum(j, nh - 1), t, 0)),
                  pl.BlockSpec((1, tt, FB), lambda j, t: (jnp.maximum(j - nh, 0), t, 0))],
        out_specs=[pl.BlockSpec((1, kr, FB), lambda j, t: (j, 0, 0))],
        out_shape=[jax.ShapeDtypeStruct((2 * nh, kr, FB), BF16)],
        scratch_shapes=[pltpu.VMEM((kr, FB), F32)],
        compiler_params=_params("arbitrary", "arbitrary"),
    )(n, dG, dU)


def _mix_in_fwd(h, gain, w):
    T, D = h.shape
    W = w.shape[1]
    nuv = 2 * GM_WIDTH
    tm = min(ROW_TILE, T)

    def body(h_ref, g_ref, w_ref, n_ref, zuv_ref, qkv_ref):
        _, xh = _rms_parts(h_ref[...])
        n = (xh * g_ref[...]).astype(BF16)
        n_ref[...] = n
        z = _dot(n, w_ref[...])
        zuv_ref[...] = z[:, :nuv]
        qkv_ref[...] = z[:, nuv:].astype(BF16)

    row = lambda i: (i, 0)
    return pl.pallas_call(
        body, name="mix_in_fwd", grid=(T // tm,),
        in_specs=[pl.BlockSpec((tm, D), row), pl.BlockSpec((1, D), lambda i: (0, 0)),
                  pl.BlockSpec((D, W), lambda i: (0, 0))],
        out_specs=[pl.BlockSpec((tm, D), row), pl.BlockSpec((tm, nuv), row),
                   pl.BlockSpec((tm, W - nuv), row)],
        out_shape=[jax.ShapeDtypeStruct((T, D), BF16), jax.ShapeDtypeStruct((T, nuv), F32),
                   jax.ShapeDtypeStruct((T, W - nuv), BF16)],
        compiler_params=_params("arbitrary"),
    )(h, gain, w)


def _mix_in_bwd(dzuv, dqkv, w, h, gain, dh):
    T, D = h.shape
    W = w.shape[1]
    nuv = dzuv.shape[1]
    tm = min(ROW_TILE, T)

    def body(dzuv_ref, dqkv_ref, w_ref, h_ref, g_ref, dh_ref, dhin_ref, dg_ref, half_ref):
        @pl.when(pl.program_id(0) == 0)
        def _():
            dg_ref[...] = jnp.zeros_like(dg_ref)

        dn = _dot_nt(dzuv_ref[...], w_ref[:, :nuv]) + _dot_nt(dqkv_ref[...], w_ref[:, nuv:])
        dx, dg = _rms_bwd(h_ref[...], g_ref[...], dn)
        dh_in = dh_ref[...] + dx
        dhin_ref[...] = dh_in
        half_ref[...] = (0.5 * dh_in).astype(BF16)
        dg_ref[...] += dg

    row = lambda i: (i, 0)
    one = lambda i: (0, 0)
    return pl.pallas_call(
        body, name="mix_in_bwd", grid=(T // tm,),
        in_specs=[pl.BlockSpec((tm, nuv), row), pl.BlockSpec((tm, W - nuv), row),
                  pl.BlockSpec((D, W), one), pl.BlockSpec((tm, D), row), pl.BlockSpec((1, D), one),
                  pl.BlockSpec((tm, D), row)],
        out_specs=[pl.BlockSpec((tm, D), row), pl.BlockSpec((1, D), one), pl.BlockSpec((tm, D), row)],
        out_shape=[jax.ShapeDtypeStruct((T, D), F32), jax.ShapeDtypeStruct((1, D), F32),
                   jax.ShapeDtypeStruct((T, D), BF16)],
        compiler_params=_params("arbitrary"),
    )(dzuv, dqkv, w, h, gain, dh)


def _gmlp_norm(zv, gv):
    v = _gelu(zv)
    r, vh = _rms_parts(v)
    return r, vh, (vh * gv).astype(BF16)


def _causal_ws(ws_ref, hd):
    r = lax.broadcasted_iota(jnp.int32, (CHUNK, CHUNK), 0)
    c = lax.broadcasted_iota(jnp.int32, (CHUNK, CHUNK), 1)
    return jnp.where(r >= c, ws_ref[hd], 0.0).astype(BF16)


def _gmlp_fwd(zuv, gv, ws, b_t):
    T = zuv.shape[0]
    tg = min(ROW_TILE, T)

    def body(zu_ref, zv_ref, gv_ref, ws_ref, bt_ref, o_ref):
        u = _gelu(zu_ref[...])
        _, _, vn = _gmlp_norm(zv_ref[...], gv_ref[...])
        for hd in range(GM_HEADS):
            wc = _causal_ws(ws_ref, hd)
            cols = slice(hd * CHUNK, (hd + 1) * CHUNK)
            for c in range(tg // CHUNK):
                rows = slice(c * CHUNK, (c + 1) * CHUNK)
                sv = _dot(wc, vn[rows, cols]) + bt_ref[:, hd:hd + 1]
                o_ref[rows, cols] = (u[rows, cols] * sv).astype(BF16)

    return pl.pallas_call(
        body, name="gmlp_fwd", grid=(T // tg,),
        in_specs=[pl.BlockSpec((tg, GM_WIDTH), lambda i: (i, 0)), pl.BlockSpec((tg, GM_WIDTH), lambda i: (i, 1)),
                  pl.BlockSpec((1, GM_WIDTH), lambda i: (0, 0)),
                  pl.BlockSpec((GM_HEADS, CHUNK, CHUNK), lambda i: (0, 0, 0)),
                  pl.BlockSpec((CHUNK, GM_HEADS), lambda i: (0, 0))],
        out_specs=pl.BlockSpec((tg, GM_WIDTH), lambda i: (i, 0)),
        out_shape=jax.ShapeDtypeStruct((T, GM_WIDTH), BF16),
        compiler_params=_params("arbitrary"),
    )(zuv, zuv, gv, ws, b_t)


def _gmlp_bwd(zuv, d_gm, gv, ws, b_t):
    T = zuv.shape[0]
    tg = min(ROW_TILE, T)
    ng = T // tg

    def body(zu_ref, zv_ref, dgm_ref, gv_ref, ws_ref, bt_ref, dz_ref, dgv_ref, dws_ref, dbt_ref, dsv_acc, dvn_s):
        i = pl.program_id(0)

        @pl.when(i == 0)
        def _():
            dgv_ref[...] = jnp.zeros_like(dgv_ref)
            dws_ref[...] = jnp.zeros_like(dws_ref)
            dsv_acc[...] = jnp.zeros_like(dsv_acc)

        zu = zu_ref[...]
        zv = zv_ref[...]
        dgm = dgm_ref[...]
        gvv = gv_ref[...]
        u = _gelu(zu)
        rv, vh, vn = _gmlp_norm(zv, gvv)
        dsv = dgm * u
        dsv_b = dsv.astype(BF16)
        for hd in range(GM_HEADS):
            wc = _causal_ws(ws_ref, hd)
            cols = slice(hd * CHUNK, (hd + 1) * CHUNK)
            dws = jnp.zeros((CHUNK, CHUNK), F32)
            dsv_sum = jnp.zeros((CHUNK, CHUNK), F32)
            for c in range(tg // CHUNK):
                rows = slice(c * CHUNK, (c + 1) * CHUNK)
                vch = vn[rows, cols]
                sv = _dot(wc, vch) + bt_ref[:, hd:hd + 1]
                dz_ref[rows, cols] = (dgm[rows, cols] * sv * _gelu_grad(zu[rows, cols])).astype(BF16)
                dws += _dot_nt(dsv_b[rows, cols], vch)
                dsv_sum += dsv[rows, cols]
                dvn_s[rows, cols] = _dot_tn(wc, dsv_b[rows, cols])
            dws_ref[hd] += dws
            dsv_acc[:, cols] += dsv_sum
        dvn = dvn_s[...]
        dvh = dvn * gvv
        dv = rv * (dvh - vh * jnp.mean(dvh * vh, axis=-1, keepdims=True))
        dgv_ref[...] += jnp.sum(dvn * vh, axis=0, keepdims=True)
        dz_ref[:, GM_WIDTH:] = (dv * _gelu_grad(zv)).astype(BF16)

        @pl.when(i == ng - 1)
        def _():
            r = lax.broadcasted_iota(jnp.int32, (CHUNK, CHUNK), 0)
            c = lax.broadcasted_iota(jnp.int32, (CHUNK, CHUNK), 1)
            for hd in range(GM_HEADS):
                dws_ref[hd] = jnp.where(r >= c, dws_ref[hd], 0.0)
                dbt_ref[:, hd:hd + 1] = jnp.sum(dsv_acc[:, hd * CHUNK:(hd + 1) * CHUNK], axis=1, keepdims=True)

    return pl.pallas_call(
        body, name="gmlp_bwd", grid=(ng,),
        in_specs=[pl.BlockSpec((tg, GM_WIDTH), lambda i: (i, 0)), pl.BlockSpec((tg, GM_WIDTH), lambda i: (i, 1)),
                  pl.BlockSpec((tg, GM_WIDTH), lambda i: (i, 0)),
                  pl.BlockSpec((1, GM_WIDTH), lambda i: (0, 0)),
                  pl.BlockSpec((GM_HEADS, CHUNK, CHUNK), lambda i: (0, 0, 0)),
                  pl.BlockSpec((CHUNK, GM_HEADS), lambda i: (0, 0))],
        out_specs=[pl.BlockSpec((tg, 2 * GM_WIDTH), lambda i: (i, 0)),
                   pl.BlockSpec((1, GM_WIDTH), lambda i: (0, 0)),
                   pl.BlockSpec((GM_HEADS, CHUNK, CHUNK), lambda i: (0, 0, 0)),
                   pl.BlockSpec((CHUNK, GM_HEADS), lambda i: (0, 0))],
        out_shape=[jax.ShapeDtypeStruct((T, 2 * GM_WIDTH), BF16), jax.ShapeDtypeStruct((1, GM_WIDTH), F32),
                   jax.ShapeDtypeStruct((GM_HEADS, CHUNK, CHUNK), F32),
                   jax.ShapeDtypeStruct((CHUNK, GM_HEADS), F32)],
        scratch_shapes=[pltpu.VMEM((CHUNK, GM_WIDTH), F32), pltpu.VMEM((tg, GM_WIDTH), F32)],
        compiler_params=_params("arbitrary"),
    )(zuv, zuv, d_gm, gv, ws, b_t)


def _scan_matrix(blk, keep):
    r = lax.broadcasted_iota(jnp.int32, (blk, blk), 0)
    c = lax.broadcasted_iota(jnp.int32, (blk, blk), 1)
    return jnp.where(keep(r, c), 1.0, 0.0).astype(BF16)


def _scan_matrix2(blk, keep, value):
    m = _scan_matrix(blk, keep) * value
    return jnp.concatenate([m, m], axis=0)


def _scan(x, mat2):
    hi, lo = _split_bf16(x)
    return _dot(jnp.concatenate([hi, lo], axis=1), mat2)


def _head_masks(q):
    lane = lax.broadcasted_iota(jnp.int32, q.shape, 1)
    m0 = lane < SB_HEAD_DIM
    zero = jnp.zeros_like(q)
    return m0, jnp.where(m0, q, zero), jnp.where(m0, zero, q)


_LOG2E = 1.4426950408889634


def _softplus_parts(z):
    e = jnp.exp2(jnp.abs(z) * (-_LOG2E))
    ope = 1.0 + e
    return e, ope, jnp.maximum(z, 0.0) + jnp.log(ope)


def _attn_fwd(qkv, rider=None):
    T = qkv.shape[0]
    tk = ATTN_KEY_BLOCK
    tq = min(ATTN_Q_ROWS, T)
    band = tq // tk
    assert band % ATTN_UNROLL == 0 or T == tq
    ngrp = SB_WIDTH // LANES

    def body(q_ref, k_ref, v_ref, o_ref, l_ref, acc, run):
        i = pl.program_id(1)
        suffix = _scan_matrix2(tk, lambda r, c: r >= c, -1.0)
        row = lax.broadcasted_iota(jnp.int32, (tq, tk), 0)
        col = lax.broadcasted_iota(jnp.int32, (tq, tk), 1)
        m0, q0, q1 = _head_masks(q_ref[...] * SB_SCALE)
        acc[...] = jnp.zeros_like(acc)
        run[...] = jnp.zeros_like(run)

        def tiles(work, rows=slice(None)):
            heads = (q0[rows], q1[rows])
            kv = []
            for j, _ in work:
                start = pl.multiple_of(j * tk, tk)
                kv.append((k_ref[pl.ds(start, tk), :], v_ref[pl.ds(start, tk), :]))
            z = [[_dot_nt(qh, kj) for qh in heads] for kj, _ in kv]
            sp = [[_softplus_parts(zz)[2] for zz in zt] for zt in z]
            sp = [[s if m is None else jnp.where(m, s, 0.0) for s in st] for st, (_, m) in zip(sp, work)]
            res = [[_scan(s, suffix) for s in st] for st in sp]
            runs = [run[hd, rows, :] for hd in range(len(heads))]
            a = []
            for t, (_, m) in enumerate(work):
                at = []
                for hd in range(len(heads)):
                    av = jnp.exp(z[t][hd] + (runs[hd] + res[t][hd]))
                    at.append(av if m is None else jnp.where(m, av, 0.0))
                    runs[hd] = runs[hd] + res[t][hd][:, 0:1]
                a.append(at)
            for hd in range(len(heads)):
                run[hd, rows, :] = runs[hd]
                upd = _dot(a[0][hd].astype(BF16), kv[0][1])
                for t in range(1, len(work)):
                    upd = upd + _dot(a[t][hd].astype(BF16), kv[t][1])
                acc[hd, rows, :] += upd

        for jb in reversed(range(band)):
            rows = slice(jb * tk, tq)
            tiles([(i * band + jb, (jb * tk + col < row)[rows])], rows)

        def full_step(it, carry):
            tiles([(i * band - 1 - ATTN_UNROLL * it - u, None) for u in range(ATTN_UNROLL)])
            return carry

        lax.fori_loop(0, i * (band // ATTN_UNROLL), full_step, 0)
        o_ref[...] = jnp.where(m0, acc[0], acc[1]).astype(BF16)
        l_ref[...] = jnp.where(m0, jnp.broadcast_to(run[0], (tq, LANES)), jnp.broadcast_to(run[1], (tq, LANES)))

    return _pallas(
        body, rider, name="attn_fwd", grid=(ngrp, T // tq),
        in_specs=[pl.BlockSpec((tq, LANES), lambda g, i: (i, g)),
                  pl.BlockSpec((T, LANES), lambda g, i: (0, ngrp + g)),
                  pl.BlockSpec((T, LANES), lambda g, i: (0, 2 * ngrp + g))],
        out_specs=[pl.BlockSpec((tq, LANES), lambda g, i: (i, g)),
                   pl.BlockSpec((tq, LANES), lambda g, i: (i, g))],
        out_shape=[jax.ShapeDtypeStruct((T, SB_WIDTH), BF16), jax.ShapeDtypeStruct((T, SB_WIDTH), F32)],
        scratch_shapes=[pltpu.VMEM((2, tq, LANES), F32), pltpu.VMEM((2, tq, 1), F32)],
        compiler_params=_params("arbitrary", "arbitrary"),
    )(qkv, qkv, qkv)


def _attn_bwd(qkv, d_o, ltot, rider=None):
    T = qkv.shape[0]
    tk = ATTN_KEY_BLOCK
    tq = min(ATTN_BWD_Q_ROWS, T)
    band = tq // tk
    nq = T // tq
    ngrp = SB_WIDTH // LANES

    def body(q_ref, k_ref, v_ref, do_ref, l_ref, dq_ref, dk_ref, dv_ref, dq_acc, dk_acc, dv_acc, lpre, ppre):
        i = pl.program_id(1)

        @pl.when(i == 0)
        def _():
            dk_acc[...] = jnp.zeros_like(dk_acc)
            dv_acc[...] = jnp.zeros_like(dv_acc)

        excl = _scan_matrix(tk, lambda r, c: r < c)
        excl2 = jnp.concatenate([excl, excl], axis=0)
        row = lax.broadcasted_iota(jnp.int32, (tq, tk), 0)
        col = lax.broadcasted_iota(jnp.int32, (tq, tk), 1)
        m0, q0, q1 = _head_masks(q_ref[...] * SB_SCALE)
        _, d0, d1 = _head_masks(do_ref[...].astype(BF16))
        lt = l_ref[...]
        ltots = (lt[:, 0:1], lt[:, SB_HEAD_DIM:SB_HEAD_DIM + 1])
        dq_acc[...] = jnp.zeros_like(dq_acc)
        lpre[...] = jnp.zeros_like(lpre)
        ppre[...] = jnp.zeros_like(ppre)

        def tiles(work, rows=slice(None)):
            heads = ((q0[rows], d0[rows]), (q1[rows], d1[rows]))
            lts = [lt[rows] for lt in ltots]
            nhd = len(heads)
            starts = [pl.multiple_of(j * tk, tk) for j, _ in work]
            kv = [(k_ref[pl.ds(st, tk), :], v_ref[pl.ds(st, tk), :]) for st in starts]
            masks = [m for _, m in work]
            every = [(t, hd) for t in range(len(work)) for hd in range(nhd)]
            z = {(t, hd): _dot_nt(heads[hd][0], kv[t][0]) for t, hd in every}
            da = {(t, hd): _dot_nt(heads[hd][1], kv[t][1]) for t, hd in every}
            sp, beta = {}, {}
            for key in every:
                s = _softplus_parts(z[key])[2]
                beta[key] = jnp.exp(z[key] - s)
                sp[key] = s if masks[key[0]] is None else jnp.where(masks[key[0]], s, 0.0)
            res = {key: _scan(sp[key], excl2) for key in every}
            lp = [lpre[hd, rows, :] for hd in range(nhd)]
            a, p = {}, {}
            for t, hd in every:
                av = jnp.exp(z[t, hd] + ((lts[hd] + lp[hd]) + res[t, hd]))
                a[t, hd] = av if masks[t] is None else jnp.where(masks[t], av, 0.0)
                p[t, hd] = a[t, hd] * da[t, hd]
                lp[hd] = lp[hd] + (res[t, hd][:, tk - 1:tk] + sp[t, hd][:, tk - 1:tk])
            resp = {key: _dot(p[key].astype(BF16), excl) for key in every}
            pp = [ppre[hd, rows, :] for hd in range(nhd)]
            dzb = {}
            for t, hd in every:
                dz = p[t, hd] - beta[t, hd] * (p[t, hd] + (pp[hd] + resp[t, hd]))
                if masks[t] is not None:
                    dz = jnp.where(masks[t], dz, 0.0)
                dzb[t, hd] = dz.astype(BF16)
                pp[hd] = pp[hd] + (resp[t, hd][:, tk - 1:tk] + p[t, hd][:, tk - 1:tk])
            for hd in range(nhd):
                lpre[hd, rows, :] = lp[hd]
                ppre[hd, rows, :] = pp[hd]
                upd = _dot(dzb[0, hd], kv[0][0])
                for t in range(1, len(work)):
                    upd = upd + _dot(dzb[t, hd], kv[t][0])
                dq_acc[hd, rows, :] += upd
            for t, st in enumerate(starts):
                dk = _dot_tn(dzb[t, 0], heads[0][0])
                dv = _dot_tn(a[t, 0].astype(BF16), heads[0][1])
                for hd in range(1, nhd):
                    dk = dk + _dot_tn(dzb[t, hd], heads[hd][0])
                    dv = dv + _dot_tn(a[t, hd].astype(BF16), heads[hd][1])
                dk_acc[pl.ds(st, tk), :] += dk
                dv_acc[pl.ds(st, tk), :] += dv

        def full_step(j, carry):
            tiles([(j, None)])
            return carry

        lax.fori_loop(0, i * band, full_step, 0)
        for jb in range(band):
            rows = slice(jb * tk, tq)
            tiles([(i * band + jb, (jb * tk + col < row)[rows])], rows)
        dq_ref[...] = (jnp.where(m0, dq_acc[0], dq_acc[1]) * SB_SCALE).astype(BF16)

        @pl.when(i == nq - 1)
        def _():
            dk_ref[...] = dk_acc[...].astype(BF16)
            dv_ref[...] = dv_acc[...].astype(BF16)

    qmap = lambda g, i: (i, g)
    return _pallas(
        body, rider, name="attn_bwd", grid=(ngrp, nq),
        in_specs=[pl.BlockSpec((tq, LANES), qmap),
                  pl.BlockSpec((T, LANES), lambda g, i: (0, ngrp + g)),
                  pl.BlockSpec((T, LANES), lambda g, i: (0, 2 * ngrp + g)),
                  pl.BlockSpec((tq, LANES), qmap), pl.BlockSpec((tq, LANES), qmap)],
        out_specs=[pl.BlockSpec((tq, LANES), qmap),
                   pl.BlockSpec((T, LANES), lambda g, i: (0, g)),
                   pl.BlockSpec((T, LANES), lambda g, i: (0, g))],
        out_shape=[jax.ShapeDtypeStruct((T, SB_WIDTH), BF16)] * 3,
        scratch_shapes=[pltpu.VMEM((2, tq, LANES), F32), pltpu.VMEM((T, LANES), F32),
                        pltpu.VMEM((T, LANES), F32), pltpu.VMEM((2, tq, 1), F32),
                        pltpu.VMEM((2, tq, 1), F32)],
        compiler_params=_params("arbitrary", "arbitrary"),
    )(qkv, qkv, qkv, d_o, ltot)


def _mix_out_fwd(h, gm, sb, w):
    T, D = h.shape
    tm = min(ROW_TILE, T)

    def body(h_ref, gm_ref, sb_ref, w_ref, o_ref):
        o_ref[...] = h_ref[...] + _dot(gm_ref[...], w_ref[:GM_WIDTH, :]) + _dot(sb_ref[...], w_ref[GM_WIDTH:, :])

    row = lambda i: (i, 0)
    return pl.pallas_call(
        body, name="mix_out_fwd", grid=(T // tm,),
        in_specs=[pl.BlockSpec((tm, D), row), pl.BlockSpec((tm, GM_WIDTH), row), pl.BlockSpec((tm, SB_WIDTH), row),
                  pl.BlockSpec((GM_WIDTH + SB_WIDTH, D), lambda i: (0, 0))],
        out_specs=pl.BlockSpec((tm, D), row),
        out_shape=jax.ShapeDtypeStruct((T, D), F32),
        compiler_params=_params("arbitrary"),
    )(h, gm, sb, w)


def _mix_out_bwd(dh, w):
    T, D = dh.shape
    tm = min(ROW_TILE, T)

    def body(dh_ref, w_ref, dgm_ref, dsb_ref, dhb_ref):
        dhb = dh_ref[...].astype(BF16)
        dhb_ref[...] = dhb
        dgm_ref[...] = _dot_nt(dhb, w_ref[:GM_WIDTH, :])
        dsb_ref[...] = _dot_nt(dhb, w_ref[GM_WIDTH:, :])

    row = lambda i: (i, 0)
    return pl.pallas_call(
        body, name="mix_out_bwd", grid=(T // tm,),
        in_specs=[pl.BlockSpec((tm, D), row), pl.BlockSpec((GM_WIDTH + SB_WIDTH, D), lambda i: (0, 0))],
        out_specs=[pl.BlockSpec((tm, GM_WIDTH), row), pl.BlockSpec((tm, SB_WIDTH), row), pl.BlockSpec((tm, D), row)],
        out_shape=[jax.ShapeDtypeStruct((T, GM_WIDTH), F32), jax.ShapeDtypeStruct((T, SB_WIDTH), F32),
                   jax.ShapeDtypeStruct((T, D), BF16)],
        compiler_params=_params("arbitrary"),
    )(dh, w)


def _tail(h3, p, target, g_ple, g_fin, w_gate, w_proj):
    T, D = h3.shape
    PD = p.shape[1]
    tm = min(ROW_TILE, T)

    def body(h_ref, p_ref, t_ref, gp_ref, gf_ref, wg_ref, wp_ref,
             loss_ref, dh_ref, n4_ref, dgl_ref, dpp_ref, dgp_ref, dgf_ref, half_ref):
        @pl.when(pl.program_id(0) == 0)
        def _():
            loss_ref[...] = jnp.zeros_like(loss_ref)
            dgp_ref[...] = jnp.zeros_like(dgp_ref)
            dgf_ref[...] = jnp.zeros_like(dgf_ref)

        h3v = h_ref[...]
        gp = gp_ref[...]
        gf = gf_ref[...]
        r3, xh3 = _rms_parts(h3v)
        n4 = (xh3 * gp).astype(BF16)
        n4_ref[...] = n4
        gate = _sigmoid(_dot(n4, wg_ref[...]))
        pp = _dot(p_ref[...].astype(BF16), wp_ref[...])
        h4 = h3v + gate * pp
        r4, xh4 = _rms_parts(h4)
        err = xh4 * gf - t_ref[...]
        loss_ref[...] += jnp.full(loss_ref.shape, (0.5 / D) * jnp.sum(err * err), F32)
        dy = err * (1.0 / D)
        dgf_ref[...] += jnp.sum(dy * xh4, axis=0, keepdims=True)
        dyg = dy * gf
        dh4 = r4 * (dyg - xh4 * jnp.mean(dyg * xh4, axis=-1, keepdims=True))
        dpp_ref[...] = (dh4 * gate).astype(BF16)
        dgl = (dh4 * pp * gate * (1.0 - gate)).astype(BF16)
        dgl_ref[...] = dgl
        dn4 = _dot_nt(dgl, wg_ref[...])
        dgp_ref[...] += jnp.sum(dn4 * xh3, axis=0, keepdims=True)
        dn4g = dn4 * gp
        dh3 = dh4 + r3 * (dn4g - xh3 * jnp.mean(dn4g * xh3, axis=-1, keepdims=True))
        dh_ref[...] = dh3
        half_ref[...] = (0.5 * dh3).astype(BF16)

    row = lambda i: (i, 0)
    one = lambda i: (0, 0)
    return pl.pallas_call(
        body, name="tail", grid=(T // tm,),
        in_specs=[pl.BlockSpec((tm, D), row), pl.BlockSpec((tm, PD), row), pl.BlockSpec((tm, D), row),
                  pl.BlockSpec((1, D), one), pl.BlockSpec((1, D), one),
                  pl.BlockSpec((D, D), one), pl.BlockSpec((PD, D), one)],
        out_specs=[pl.BlockSpec((1, LANES), one), pl.BlockSpec((tm, D), row), pl.BlockSpec((tm, D), row),
                   pl.BlockSpec((tm, D), row), pl.BlockSpec((tm, D), row),
                   pl.BlockSpec((1, D), one), pl.BlockSpec((1, D), one), pl.BlockSpec((tm, D), row)],
        out_shape=[jax.ShapeDtypeStruct((1, LANES), F32), jax.ShapeDtypeStruct((T, D), F32),
                   jax.ShapeDtypeStruct((T, D), BF16), jax.ShapeDtypeStruct((T, D), BF16),
                   jax.ShapeDtypeStruct((T, D), BF16),
                   jax.ShapeDtypeStruct((1, D), F32), jax.ShapeDtypeStruct((1, D), F32),
                   jax.ShapeDtypeStruct((T, D), BF16)],
        compiler_params=_params("arbitrary"),
    )(h3, p, target, g_ple, g_fin, w_gate, w_proj)


FFN1_W = ("ffn1_w_in", "ffn1_w_out")
MIX_W = ("w_mix_in", "w_mix_out")
REST_W = ("ffn2_w_in", "ffn2_w_out", "ple_w_gate", "ple_w_proj")
BIG_W = FFN1_W + MIX_W + REST_W
COLUMN_SHARDED = ("w_mix_in", "ple_w_proj")


class _Traffic:
    def __init__(self, shards):
        self.shards = shards
        self.parts = {}
        self.blocks, self.sums = {}, {}

    @staticmethod
    def _full(name, gathered):
        if name in COLUMN_SHARDED:
            return jnp.transpose(gathered, (1, 0, 2)).reshape(gathered.shape[1], -1)
        if name.endswith("_w_in"):
            return gathered
        return gathered.reshape(-1, gathered.shape[-1])

    @staticmethod
    def _blocks(name, grad):
        name = name.split("/")[0]
        if name in COLUMN_SHARDED:
            return jnp.transpose(grad.reshape(grad.shape[0], N_DEV, -1), (1, 0, 2))
        if name.endswith("_w_in"):
            return grad
        return grad.reshape(N_DEV, -1, grad.shape[-1])

    def gather_now(self, names):
        got = _exchange("gather_" + names[0], [self.shards[n] for n in names], [GATHER] * len(names))
        return self.gathered(names, got)

    def gather_rider(self, names):
        return [self.shards[n] for n in names], [GATHER] * len(names)

    def gathered(self, names, got):
        return {n: self._full(n, g) for n, g in zip(names, got)}

    def scatter_rider(self, grads):
        return [self._blocks(n, g) for n, g in grads.items()], [SCATTER] * len(grads)

    def scattered(self, names, got):
        self.parts.update(zip(names, got))


    def pair_rider(self, name, grad):
        self.blocks[name] = self._blocks(name, grad)
        return [self.blocks[name]], [PAIR]

    def paired(self, name, got):
        self.sums[name] = _stage_pair_sums("pair_sum_" + name.replace("/", "_"), self.blocks[name], got)

    def chip_rider(self, name):
        return [self.sums[name]], [CHIP]

    def chipped(self, name, got):
        self.parts[name] = got


def _local_step(traffic, x, p, target, g1, gmix, gv, ws, b_t, g2, gple, gfin, pack_small):
    T, D = x.shape
    tm = min(ROW_TILE, T)

    w = traffic.gather_now(FFN1_W)
    h1, n1, G1, U1, a1, *got = _ffn_fwd("ffn1_fwd", x, g1, w["ffn1_w_in"], w["ffn1_w_out"],
                                        rider=traffic.gather_rider(MIX_W))
    w.update(traffic.gathered(MIX_W, got))
    n2, zuv, qkv = _mix_in_fwd(h1, gmix, w["w_mix_in"])
    gm = _gmlp_fwd(zuv, gv, ws, b_t)
    sb, ltot, *got = _attn_fwd(qkv, rider=traffic.gather_rider(REST_W))
    w.update(traffic.gathered(REST_W, got))
    h2 = _mix_out_fwd(h1, gm, sb, w["w_mix_out"])
    h3, n3, G2, U2, a2 = _ffn_fwd("ffn2_fwd", h2, g2, w["ffn2_w_in"], w["ffn2_w_out"])
    loss, dh3, n4, d_gl, d_pp, dg_ple, dg_fin, dout2 = _tail(h3, p, target, gple, gfin, w["ple_w_gate"], w["ple_w_proj"])

    nb, _, FB = w["ffn1_w_in"].shape
    nh = nb // 2

    tt = min(GRAD_ROW_TILE, T)

    def dw_out(name, a, d_out):
        return _matmul_tn(name, a, d_out, nh, (1, tt, FB), lambda j, t: (j, t, 0), (tt, D), lambda j, t: (t, 0),
                          (nh, FB, D), (1, FB, D), lambda j, t: (j, 0, 0))

    def dense_tn(name, a, b, ncol):
        ka, nbw = a.shape[1], b.shape[1] // ncol
        return _matmul_tn(name, a, b, ncol, (tt, ka), lambda j, t: (t, 0), (tt, nbw), lambda j, t: (t, j),
                          (ka, b.shape[1]), (ka, nbw), lambda j, t: (0, j))

    grads = dict(ple_w_gate=dense_tn("dw_ple_gate", n4, d_gl, 2), ple_w_proj=dense_tn("dw_ple_proj", p, d_pp, 1))
    dG2, dU2 = _ffn_bwd_gates("ffn2_bwd_gates", dout2, G2, U2, w["ffn2_w_out"])
    dh2, dg2 = _ffn_bwd_input("ffn2_bwd_input", dh3, h2, g2, dG2, dU2, w["ffn2_w_in"])
    grads["ffn2_w_in"], = _dw_in("ffn2_dw_in", n3, dG2, dU2)
    grads["ffn2_w_out"] = dw_out("ffn2_dw_out", a2, dout2)
    grads = {n: grads[n] for n in REST_W}

    d_gm, d_sb, dh2_bf = _mix_out_bwd(dh2, w["w_mix_out"])
    grads["w_mix_out"] = jnp.concatenate([dense_tn("dw_mix_out_gm", gm, dh2_bf, 1),
                                          dense_tn("dw_mix_out_sb", sb, dh2_bf, 1)], axis=0)
    dzuv, dgv, dws, db_t = _gmlp_bwd(zuv, d_gm, gv, ws, b_t)
    dq, dk, dv, *got = _attn_bwd(qkv, d_sb, ltot, rider=traffic.scatter_rider(grads))
    traffic.scattered(list(grads), got)
    dqkv = jnp.concatenate([dq, dk, dv], axis=1)
    dw_mi = jnp.concatenate([dense_tn("dw_mix_in_uv", n2, dzuv, 2), dense_tn("dw_mix_in_qkv", n2, dqkv, 3)], axis=1)
    dh1, dgmix, dout1 = _mix_in_bwd(dzuv, dqkv, w["w_mix_in"], h1, gmix, dh2)

    def dw_out_riding(name, a, d_out, rider):
        return _matmul_tn(name, a, d_out, nh, (1, tt, FB), lambda j, t: (j, t, 0), (tt, D), lambda j, t: (t, 0),
                          (nh, FB, D), (1, FB, D), lambda j, t: (j, 0, 0), rider=rider)

    def both(*riders):
        return [x for r in riders for x in r[0]], [k for r in riders for k in r[1]]

    dw_out1, got = dw_out_riding("ffn1_dw_out", a1, dout1, traffic.pair_rider("w_mix_in", dw_mi))
    traffic.paired("w_mix_in", got)
    dG1, dU1, got_mi, got = _ffn_bwd_gates("ffn1_bwd_gates", dout1, G1, U1, w["ffn1_w_out"],
                                           rider=both(traffic.chip_rider("w_mix_in"),
                                                      traffic.pair_rider("ffn1_w_out", dw_out1)))
    traffic.chipped("w_mix_in", got_mi)
    traffic.paired("ffn1_w_out", got)
    half = D // 2
    top, got = _dw_in("ffn1_dw_in_top", n1[:, :half], dG1, dU1, rider=traffic.chip_rider("ffn1_w_out"))
    traffic.chipped("ffn1_w_out", got)
    bottom, got = _dw_in("ffn1_dw_in_bottom", n1[:, half:], dG1, dU1, rider=traffic.pair_rider("ffn1_w_in/0", top))
    traffic.paired("ffn1_w_in/0", got)
    traffic.paired("ffn1_w_in/1", _exchange("pair_last", *traffic.pair_rider("ffn1_w_in/1", bottom))[0])
    dx, dg1, got_top, got_bottom = _ffn_bwd_input("ffn1_bwd_input", dh1, x, g1, dG1, dU1, w["ffn1_w_in"],
                                                  rider=both(traffic.chip_rider("ffn1_w_in/0"),
                                                             traffic.chip_rider("ffn1_w_in/1")))
    traffic.chipped("ffn1_w_in/0", got_top)
    traffic.chipped("ffn1_w_in/1", got_bottom)

    small = pack_small(dict(ffn1_norm=dg1, mix_norm=dgmix, gmlp_v_norm=dgv, gmlp_w_s=dws, gmlp_b=jnp.transpose(db_t),
                            ffn2_norm=dg2, ple_norm=dg_ple, final_norm=dg_fin), loss)
    return dx, small


def _my_index():
    return 4 * lax.axis_index("x") + 2 * lax.axis_index("y") + lax.axis_index("c")


def _stage_pair_sums(name, blocks, got):
    _, R, C = blocks.shape
    me = _my_index()
    index = jnp.stack([jnp.bitwise_xor(me, 2 * c) for c in range(1, N_CHIPS_AWAY + 1)] + [me, me]).astype(jnp.int32)

    def body(index_ref, b_ref, g_ref, o_ref):
        i = pl.program_id(0)

        @pl.when(i < N_CHIPS_AWAY)
        def _():
            o_ref[...] = (b_ref[...].astype(F32) + g_ref[...].astype(F32)).astype(BF16)

        @pl.when(i == N_CHIPS_AWAY)
        def _():
            o_ref[...] = g_ref[...]

        @pl.when(i == N_CHIPS_AWAY + 1)
        def _():
            o_ref[...] = b_ref[...]

    return pl.pallas_call(
        body, name=name,
        grid_spec=pltpu.PrefetchScalarGridSpec(
            num_scalar_prefetch=1, grid=(PART_SLOTS,),
            in_specs=[pl.BlockSpec((1, R, C), lambda i, idx: (idx[i], 0, 0)),
                      pl.BlockSpec((1, R, C), lambda i, idx: (jnp.minimum(i, N_CHIPS_AWAY), 0, 0))],
            out_specs=pl.BlockSpec((1, R, C), lambda i, idx: (i, 0, 0))),
        out_shape=jax.ShapeDtypeStruct((PART_SLOTS, R, C), BF16), compiler_params=_params("arbitrary"),
    )(index, blocks, got)


def _peer(d):
    x, y, c = lax.axis_index("x"), lax.axis_index("y"), lax.axis_index("c")
    px = 1 - x if d & 4 else x
    py = 1 - y if d & 2 else y
    pc = 1 - c if d & 1 else c
    return (px, py, pc), 4 * px + 2 * py + pc


GATHER, SCATTER, PAIR, CHIP = "gather", "scatter", "pair", "chip"


class _ExchangePlan:
    def __init__(self, ins, outs, send, recv, local, kinds):
        self.ins, self.outs, self.send, self.recv, self.local, self.kinds = ins, outs, send, recv, local, kinds
        self.scatter = [k == SCATTER for k in kinds]
        self.me = _peer(0)[1]

    def _remote(self, t, sem, src, slot, peer):
        return pltpu.make_async_remote_copy(
            src_ref=src, dst_ref=self.outs[t].at[slot], send_sem=self.send.at[t, sem], recv_sem=self.recv.at[t, sem],
            device_id=peer, device_id_type=MESH)

    def _own(self, t):
        if self.kinds[t] == CHIP:
            kept = pl.ds(N_CHIPS_AWAY, PART_SLOTS - N_CHIPS_AWAY)
            return pltpu.make_async_copy(self.ins[t].at[kept], self.outs[t].at[kept], self.local.at[t])
        src = self.ins[t].at[self.me] if self.scatter[t] else self.ins[t]
        return pltpu.make_async_copy(src, self.outs[t].at[self.me], self.local.at[t])

    def _n_direct(self, t):
        return {SCATTER: N_DEV - 1, GATHER: N_CHIPS_AWAY + 1, PAIR: N_CHIPS_AWAY + 1, CHIP: N_CHIPS_AWAY}[self.kinds[t]]

    def _direct(self, t, k):
        kind = self.kinds[t]
        if kind == SCATTER:
            peer, slot = _peer(k + 1)
            return self._remote(t, k, self.ins[t].at[slot], self.me, peer)
        if kind == GATHER:
            return self._remote(t, k, self.ins[t], self.me, _peer(2 * k if k else 1)[0])
        if kind == PAIR:
            block = _peer(2 * (k + 1) + 1 if k < N_CHIPS_AWAY else 1)[1]
            return self._remote(t, k, self.ins[t].at[block], k, _peer(1)[0])
        return self._remote(t, k, self.ins[t].at[k], k, _peer(2 * (k + 1))[0])

    def _has_own(self, t):
        return self.kinds[t] != PAIR

    def _relay(self, t, c):
        slot = _peer(2 * c)[1]
        return self._remote(t, N_CHIPS_AWAY + c, self.outs[t].at[slot], slot, _peer(1)[0])

    def start(self):
        for t in range(len(self.ins)):
            if self._has_own(t):
                self._own(t).start()
            for k in range(self._n_direct(t)):
                self._direct(t, k).start()

    def relay(self):
        for t in self._gathers():
            for c in range(1, N_CHIPS_AWAY + 1):
                self._direct(t, c).wait_recv()
                self._relay(t, c).start()

    def _gathers(self):
        return [t for t in range(len(self.ins)) if self.kinds[t] == GATHER]

    def finish(self):
        for t in range(len(self.ins)):
            if self._has_own(t):
                self._own(t).wait()
            for k in range(self._n_direct(t)):
                self._direct(t, k).wait_send()
                if self.kinds[t] != GATHER or k == 0:
                    self._direct(t, k).wait_recv()
        for t in self._gathers():
            for c in range(1, N_CHIPS_AWAY + 1):
                self._relay(t, c).wait()


def _exchange_shapes(arrays, kinds):
    shape = {GATHER: lambda a: (N_DEV,) + a.shape, SCATTER: lambda a: a.shape, CHIP: lambda a: a.shape,
             PAIR: lambda a: (N_CHIPS_AWAY + 1,) + a.shape[1:]}
    return [jax.ShapeDtypeStruct(shape[k](a), a.dtype) for a, k in zip(arrays, kinds)]


def _exchange_sems(n):
    return [pltpu.SemaphoreType.DMA((n, N_DEV - 1)), pltpu.SemaphoreType.DMA((n, N_DEV - 1)),
            pltpu.SemaphoreType.DMA((n,))]


_ANY = pl.BlockSpec(memory_space=pl.ANY)


def _exchange(name, arrays, scatter):
    n = len(arrays)

    def body(*refs):
        plan = _ExchangePlan(refs[:n], refs[n:2 * n], *refs[2 * n:], scatter)
        plan.start()
        plan.relay()
        plan.finish()

    return pl.pallas_call(
        body, name=name, in_specs=[_ANY] * n, out_specs=[_ANY] * n, out_shape=_exchange_shapes(arrays, scatter),
        scratch_shapes=_exchange_sems(n),
    )(*arrays)


def _pallas(body, rider, *, name, grid, in_specs, out_specs, out_shape, scratch_shapes=(), compiler_params=None):
    if rider is None:
        return pl.pallas_call(body, name=name, grid=grid, in_specs=in_specs, out_specs=out_specs, out_shape=out_shape,
                              scratch_shapes=list(scratch_shapes), compiler_params=compiler_params)
    arrays, scatter = rider
    n, ni, no, ns = len(arrays), len(in_specs), len(out_specs), len(scratch_shapes)

    def carried(*refs):
        ins, r_in = refs[:ni], refs[ni:ni + n]
        outs, r_out = refs[ni + n:ni + n + no], refs[ni + n + no:ni + 2 * n + no]
        scratch, sems = refs[ni + 2 * n + no:ni + 2 * n + no + ns], refs[ni + 2 * n + no + ns:]
        step = 0
        for ax, g in enumerate(grid):
            step = step * g + pl.program_id(ax)
        steps = functools.reduce(lambda a, b: a * b, grid)

        @pl.when(step == 0)
        def _():
            _ExchangePlan(r_in, r_out, *sems, scatter).start()

        @pl.when(step == steps // 2)
        def _():
            _ExchangePlan(r_in, r_out, *sems, scatter).relay()

        body(*ins, *outs, *scratch)

        @pl.when(step == steps - 1)
        def _():
            _ExchangePlan(r_in, r_out, *sems, scatter).finish()

    call = pl.pallas_call(
        carried, name=name, grid=grid, in_specs=list(in_specs) + [_ANY] * n, out_specs=list(out_specs) + [_ANY] * n,
        out_shape=list(out_shape) + _exchange_shapes(arrays, scatter),
        scratch_shapes=list(scratch_shapes) + _exchange_sems(n), compiler_params=compiler_params)
    return lambda *args: call(*args, *arrays)


def _adamw_math(g, w, m, v):
    m_new = ADAM_B1 * m + (1.0 - ADAM_B1) * g
    v_new = ADAM_B2 * v + (1.0 - ADAM_B2) * (g * g)
    m_hat = m_new / (1.0 - ADAM_B1 ** ADAM_STEP)
    v_hat = v_new / (1.0 - ADAM_B2 ** ADAM_STEP)
    delta = -ADAM_LR * (m_hat / (jnp.sqrt(v_hat) + ADAM_EPS) + ADAM_WD * w)
    return delta, m_new, v_new


def _adamw(name, parts, w, m, v, rider=None):
    R, C = w.shape
    slots = parts.shape[0]
    tr = R
    for cand in (256, 128, 64, 32, 16, 8):
        if R % cand == 0:
            tr = cand
            break

    def body(p_ref, w_ref, m_ref, v_ref, g_ref, d_ref, nm_ref, nv_ref):
        g = p_ref[0].astype(F32)
        for j in range(1, slots):
            g = g + p_ref[j].astype(F32)
        g_ref[...] = g
        d_ref[...], nm_ref[...], nv_ref[...] = _adamw_math(g, w_ref[...], m_ref[...], v_ref[...])

    row = lambda i: (i, 0)
    spec = pl.BlockSpec((tr, C), row)
    return _pallas(
        body, rider, name=name, grid=(R // tr,),
        in_specs=[pl.BlockSpec((slots, tr, C), lambda i: (0, i, 0)), spec, spec, spec],
        out_specs=[spec] * 4,
        out_shape=[jax.ShapeDtypeStruct((R, C), F32)] * 4,
        compiler_params=_params("arbitrary"),
    )(parts, w, m, v)


def _rows128(a):
    flat = a.reshape(-1, LANES)
    pad = (-flat.shape[0]) % SMALL_ROWS_ALIGN
    return jnp.pad(flat, ((0, pad), (0, 0))) if pad else flat


def _unrows(packed, like):
    n = like.size // LANES
    return packed[:n].reshape(like.shape)


def kernel(x, p, ffn1_norm, ffn1_w_in, ffn1_w_out, mix_norm, w_mix_in, gmlp_v_norm, gmlp_w_s, gmlp_b, w_mix_out, ffn2_norm, ffn2_w_in, ffn2_w_out, ple_norm, ple_w_gate, ple_w_proj, final_norm, loss_target, m_ffn1_norm, m_ffn1_w_in, m_ffn1_w_out, m_mix_norm, m_w_mix_in, m_gmlp_v_norm, m_gmlp_w_s, m_gmlp_b, m_w_mix_out, m_ffn2_norm, m_ffn2_w_in, m_ffn2_w_out, m_ple_norm, m_ple_w_gate, m_ple_w_proj, m_final_norm, v_ffn1_norm, v_ffn1_w_in, v_ffn1_w_out, v_mix_norm, v_w_mix_in, v_gmlp_v_norm, v_gmlp_w_s, v_gmlp_b, v_w_mix_out, v_ffn2_norm, v_ffn2_w_in, v_ffn2_w_out, v_ple_norm, v_ple_w_gate, v_ple_w_proj, v_final_norm):
    names = ["ffn1_norm", "ffn1_w_in", "ffn1_w_out", "mix_norm", "w_mix_in", "gmlp_v_norm", "gmlp_w_s", "gmlp_b",
             "w_mix_out", "ffn2_norm", "ffn2_w_in", "ffn2_w_out", "ple_norm", "ple_w_gate", "ple_w_proj", "final_norm"]
    W = dict(zip(names, [ffn1_norm, ffn1_w_in, ffn1_w_out, mix_norm, w_mix_in, gmlp_v_norm, gmlp_w_s, gmlp_b,
                         w_mix_out, ffn2_norm, ffn2_w_in, ffn2_w_out, ple_norm, ple_w_gate, ple_w_proj, final_norm]))
    M = dict(zip(names, [m_ffn1_norm, m_ffn1_w_in, m_ffn1_w_out, m_mix_norm, m_w_mix_in, m_gmlp_v_norm, m_gmlp_w_s,
                         m_gmlp_b, m_w_mix_out, m_ffn2_norm, m_ffn2_w_in, m_ffn2_w_out, m_ple_norm, m_ple_w_gate,
                         m_ple_w_proj, m_final_norm]))
    V = dict(zip(names, [v_ffn1_norm, v_ffn1_w_in, v_ffn1_w_out, v_mix_norm, v_w_mix_in, v_gmlp_v_norm, v_gmlp_w_s,
                         v_gmlp_b, v_w_mix_out, v_ffn2_norm, v_ffn2_w_in, v_ffn2_w_out, v_ple_norm, v_ple_w_gate,
                         v_ple_w_proj, v_final_norm]))
    small = [n for n in names if n not in BIG_W]
    D = x.shape[-1]

    def pack(src, last):
        return jnp.concatenate([_rows128(src[n]) for n in small] + [last], axis=0)

    offs = [0]
    for n in small:
        offs.append(offs[-1] + _rows128(W[n]).shape[0])

    traffic = _Traffic({n: W[n][0].astype(BF16) for n in BIG_W})
    dx, small_mine = _local_step(
        traffic, x[0], p[0, 0], loss_target[0],
        W["ffn1_norm"], W["mix_norm"], W["gmlp_v_norm"], W["gmlp_w_s"][0], jnp.transpose(W["gmlp_b"][0]),
        W["ffn2_norm"], W["ple_norm"], W["final_norm"].reshape(1, D),
        lambda grads, loss_part: pack(grads, jnp.broadcast_to(loss_part, (SMALL_ROWS_ALIGN, LANES))))

    out = {}
    parts = traffic.parts
    parts["ffn1_w_in"] = jnp.concatenate([parts["ffn1_w_in/0"], parts["ffn1_w_in/1"]], axis=1)
    carrier = "ffn2_w_out"
    *out[carrier], small_parts = _adamw("adamw_" + carrier, parts[carrier], W[carrier][0], M[carrier][0], V[carrier][0],
                                        rider=([small_mine], [GATHER]))
    for n in BIG_W:
        if n != carrier:
            out[n] = _adamw("adamw_" + n, parts[n], W[n][0], M[n][0], V[n][0])
    zeros = jnp.zeros((SMALL_ROWS_ALIGN, LANES), F32)
    sg, sd, sm, sv = _adamw("adamw_small", small_parts, pack(W, zeros), pack(M, zeros), pack(V, zeros))
    for k, n in enumerate(small):
        out[n] = tuple(_unrows(arr[offs[k]:offs[k + 1]], W[n]) for arr in (sg, sd, sm, sv))
    loss = sg[offs[len(small)], 0]

    res = [loss, dx[None]]
    for k in range(4):
        res += [out[n][k].reshape(W[n].shape) for n in names]
    return tuple(res)
```

```python
import functools

import jax
import jax.numpy as jnp
from jax import lax
from jax.experimental import pallas as pl
from jax.experimental.pallas import tpu as pltpu

F32 = jnp.float32
BF16 = jnp.bfloat16
MESH = pl.DeviceIdType.MESH

N_DEV = 8
N_CHIPS_AWAY = 3
PART_SLOTS = N_CHIPS_AWAY + 2
EPS = 1e-6
ADAM_LR = 0.001
ADAM_B1 = 0.9
ADAM_B2 = 0.999
ADAM_EPS = 1e-08
ADAM_WD = 0.01
ADAM_STEP = 10

GM_WIDTH = 512
GM_HEADS = 4
CHUNK = 128
SB_WIDTH = 512
SB_HEAD_DIM = 64
SB_SCALE = 0.125
LANES = 128
SMALL_ROWS_ALIGN = 8

ROW_TILE = 512
GRAD_ROW_TILE = 4096
DW_IN_ROW_TILE = 4096
FFN_FWD_ROW_TILE = 1024
ATTN_Q_ROWS = 512
ATTN_BWD_Q_ROWS = 512
ATTN_KEY_BLOCK = 256
ATTN_UNROLL = 2
VMEM_LIMIT = 56 * 1024 * 1024


def _params(*sem):
    return pltpu.CompilerParams(dimension_semantics=sem, vmem_limit_bytes=VMEM_LIMIT)


def _dot(a, b):
    return jnp.dot(a, b, preferred_element_type=F32)


def _dot_nt(a, b):
    return lax.dot_general(a, b, (((1,), (1,)), ((), ())), preferred_element_type=F32)


def _dot_tn(a, b):
    return lax.dot_general(a, b, (((0,), (0,)), ((), ())), preferred_element_type=F32)


def _rms_parts(x):
    r = lax.rsqrt(jnp.mean(x * x, axis=-1, keepdims=True) + EPS)
    return r, x * r


def _rms_bwd(x, g, dy):
    r, xh = _rms_parts(x)
    dyg = dy * g
    dx = r * (dyg - xh * jnp.mean(dyg * xh, axis=-1, keepdims=True))
    return dx, jnp.sum(dy * xh, axis=0, keepdims=True)


def _sigmoid(x):
    return 1.0 / (1.0 + jnp.exp(-x))


_SQRT_HALF = 0.7071067811865476
_INV_SQRT_2PI = 0.3989422804014327


def _gelu(x):
    return 0.5 * x * (1.0 + lax.erf(x * _SQRT_HALF))


def _gelu_grad(x):
    return 0.5 * (1.0 + lax.erf(x * _SQRT_HALF)) + x * (_INV_SQRT_2PI * jnp.exp(-0.5 * x * x))


def _split_bf16(x):
    hi = x.astype(BF16)
    lo = (x - hi.astype(F32)).astype(BF16)
    return hi, lo


def _ffn_fwd(name, h, gain, w_in, w_out, rider=None):
    T, D = h.shape
    nb, _, FB = w_in.shape
    nh = nb // 2
    tm = min(FFN_FWD_ROW_TILE, T)

    def body(h_ref, g_ref, wg_ref, wu_ref, wo_ref, ho_ref, n_ref, G_ref, U_ref, a_ref, n_s, acc):
        jj = pl.program_id(1)

        @pl.when(jj == 0)
        def _():
            _, xh = _rms_parts(h_ref[...])
            n = (xh * g_ref[...]).astype(BF16)
            n_s[...] = n
            n_ref[...] = n
            acc[...] = jnp.zeros_like(acc)

        n = n_s[...]
        G = _dot(n, wg_ref[0])
        U = _dot(n, wu_ref[0])
        G_ref[0] = G.astype(BF16)
        U_ref[0] = U.astype(BF16)
        a = (G * _sigmoid(G) * U).astype(BF16)
        a_ref[0] = a
        acc[...] += _dot(a, wo_ref[...])

        @pl.when(jj == nh - 1)
        def _():
            ho_ref[...] = h_ref[...] + 0.5 * acc[...]

    row = lambda i, j: (i, 0)
    blk = lambda i, j: (j, i, 0)
    return _pallas(
        body, rider, name=name, grid=(T // tm, nh),
        in_specs=[pl.BlockSpec((tm, D), row),
                  pl.BlockSpec((1, D), lambda i, j: (0, 0)),
                  pl.BlockSpec((1, D, FB), lambda i, j: (j, 0, 0)),
                  pl.BlockSpec((1, D, FB), lambda i, j: (j + nh, 0, 0)),
                  pl.BlockSpec((FB, D), lambda i, j: (j, 0))],
        out_specs=[pl.BlockSpec((tm, D), row), pl.BlockSpec((tm, D), row),
                   pl.BlockSpec((1, tm, FB), blk), pl.BlockSpec((1, tm, FB), blk),
                   pl.BlockSpec((1, tm, FB), blk)],
        out_shape=[jax.ShapeDtypeStruct((T, D), F32), jax.ShapeDtypeStruct((T, D), BF16),
                   jax.ShapeDtypeStruct((nh, T, FB), BF16), jax.ShapeDtypeStruct((nh, T, FB), BF16),
                   jax.ShapeDtypeStruct((nh, T, FB), BF16)],
        scratch_shapes=[pltpu.VMEM((tm, D), BF16), pltpu.VMEM((tm, D), F32)],
        compiler_params=_params("arbitrary", "arbitrary"),
    )(h, gain, w_in, w_in, w_out)


def _ffn_bwd(name, dh, h_in, gain, G, U, w_in, w_out, rider=None):
    T, D = dh.shape
    nb, _, FB = w_in.shape
    nh = nb // 2
    tm = min(ROW_TILE, T)

    def body(dh_ref, h_ref, g_ref, G_ref, U_ref, wg_ref, wu_ref, wo_ref,
             dhin_ref, dg_ref, dG_ref, dU_ref, do_ref, dn_acc, do_s):
        i = pl.program_id(0)
        jj = pl.program_id(1)

        @pl.when(jj == 0)
        def _():
            d_out = (0.5 * dh_ref[...]).astype(BF16)
            do_s[...] = d_out
            do_ref[...] = d_out
            dn_acc[...] = jnp.zeros_like(dn_acc)

        @pl.when((i == 0) & (jj == 0))
        def _():
            dg_ref[...] = jnp.zeros_like(dg_ref)

        halves = [slice(0, tm // 2), slice(tm // 2, tm)]
        da = [_dot_nt(do_s[rows, :], wo_ref[...]) for rows in halves]
        dGU = []
        for rows, dav in zip(halves, da):
            dG, dU = _gate_grads(dav, G_ref[0, rows, :].astype(F32), U_ref[0, rows, :].astype(F32))
            dG_ref[0, rows, :] = dG
            dU_ref[0, rows, :] = dU
            dGU.append((dG, dU))
        dn = [_dot_nt(dG, wg_ref[0]) for dG, _ in dGU]
        dn = [d + _dot_nt(dU, wu_ref[0]) for d, (_, dU) in zip(dn, dGU)]
        for rows, d in zip(halves, dn):
            dn_acc[rows, :] += d

        @pl.when(jj == nh - 1)
        def _():
            dx, dg = _rms_bwd(h_ref[...], g_ref[...], dn_acc[...])
            dhin_ref[...] = dh_ref[...] + dx
            dg_ref[...] += dg

    row = lambda i, j: (i, 0)
    blk = lambda i, j: (j, i, 0)
    one = lambda i, j: (0, 0)
    return _pallas(
        body, rider, name=name, grid=(T // tm, nh),
        in_specs=[pl.BlockSpec((tm, D), row), pl.BlockSpec((tm, D), row), pl.BlockSpec((1, D), one),
                  pl.BlockSpec((1, tm, FB), blk), pl.BlockSpec((1, tm, FB), blk),
                  pl.BlockSpec((1, D, FB), lambda i, j: (j, 0, 0)),
                  pl.BlockSpec((1, D, FB), lambda i, j: (j + nh, 0, 0)),
                  pl.BlockSpec((FB, D), lambda i, j: (j, 0))],
        out_specs=[pl.BlockSpec((tm, D), row), pl.BlockSpec((1, D), one),
                   pl.BlockSpec((1, tm, FB), blk), pl.BlockSpec((1, tm, FB), blk),
                   pl.BlockSpec((tm, D), row)],
        out_shape=[jax.ShapeDtypeStruct((T, D), F32), jax.ShapeDtypeStruct((1, D), F32),
                   jax.ShapeDtypeStruct((nh, T, FB), BF16), jax.ShapeDtypeStruct((nh, T, FB), BF16),
                   jax.ShapeDtypeStruct((T, D), BF16)],
        scratch_shapes=[pltpu.VMEM((tm, D), F32), pltpu.VMEM((tm, D), BF16)],
        compiler_params=_params("arbitrary", "arbitrary"),
    )(dh, h_in, gain, G, U, w_in, w_in, w_out)


def _gate_grads(dav, Gv, Uv):
    sig = _sigmoid(Gv)
    return (dav * Uv * (sig * (1.0 + Gv * (1.0 - sig)))).astype(BF16), (dav * (Gv * sig)).astype(BF16)


def _ffn_bwd_gates(name, d_out, G, U, w_out, rider=None):
    T, D = d_out.shape
    nh, _, FB = G.shape
    tm = min(FFN_FWD_ROW_TILE, T)

    def body(do_ref, G_ref, U_ref, wo_ref, dG_ref, dU_ref):
        halves = [slice(0, tm // 2), slice(tm // 2, tm)]
        da = [_dot_nt(do_ref[rows, :], wo_ref[...]) for rows in halves]
        for rows, dav in zip(halves, da):
            dG_ref[0, rows, :], dU_ref[0, rows, :] = _gate_grads(
                dav, G_ref[0, rows, :].astype(F32), U_ref[0, rows, :].astype(F32))

    blk = lambda i, j: (j, i, 0)
    return _pallas(
        body, rider, name=name, grid=(T // tm, nh),
        in_specs=[pl.BlockSpec((tm, D), lambda i, j: (i, 0)), pl.BlockSpec((1, tm, FB), blk),
                  pl.BlockSpec((1, tm, FB), blk), pl.BlockSpec((FB, D), lambda i, j: (j, 0))],
        out_specs=[pl.BlockSpec((1, tm, FB), blk), pl.BlockSpec((1, tm, FB), blk)],
        out_shape=[jax.ShapeDtypeStruct((nh, T, FB), BF16), jax.ShapeDtypeStruct((nh, T, FB), BF16)],
        compiler_params=_params("arbitrary", "arbitrary"),
    )(d_out, G, U, w_out)


def _ffn_bwd_input(name, dh, h_in, gain, dG, dU, w_in, rider=None):
    T, D = dh.shape
    nb, _, FB = w_in.shape
    nh = nb // 2
    tm = min(FFN_FWD_ROW_TILE, T)

    def body(dh_ref, h_ref, g_ref, dG_ref, dU_ref, wg_ref, wu_ref, dhin_ref, dg_ref, dn_acc):
        i = pl.program_id(0)
        jj = pl.program_id(1)

        @pl.when(jj == 0)
        def _():
            dn_acc[...] = jnp.zeros_like(dn_acc)

        @pl.when((i == 0) & (jj == 0))
        def _():
            dg_ref[...] = jnp.zeros_like(dg_ref)

        halves = [slice(0, tm // 2), slice(tm // 2, tm)]
        dn = [_dot_nt(dG_ref[0, rows, :], wg_ref[0]) for rows in halves]
        dn = [d + _dot_nt(dU_ref[0, rows, :], wu_ref[0]) for d, rows in zip(dn, halves)]
        for rows, d in zip(halves, dn):
            dn_acc[rows, :] += d

        @pl.when(jj == nh - 1)
        def _():
            dx, dg = _rms_bwd(h_ref[...], g_ref[...], dn_acc[...])
            dhin_ref[...] = dh_ref[...] + dx
            dg_ref[...] += dg

    row = lambda i, j: (i, 0)
    blk = lambda i, j: (j, i, 0)
    one = lambda i, j: (0, 0)
    return _pallas(
        body, rider, name=name, grid=(T // tm, nh),
        in_specs=[pl.BlockSpec((tm, D), row), pl.BlockSpec((tm, D), row), pl.BlockSpec((1, D), one),
                  pl.BlockSpec((1, tm, FB), blk), pl.BlockSpec((1, tm, FB), blk),
                  pl.BlockSpec((1, D, FB), lambda i, j: (j, 0, 0)),
                  pl.BlockSpec((1, D, FB), lambda i, j: (j + nh, 0, 0))],
        out_specs=[pl.BlockSpec((tm, D), row), pl.BlockSpec((1, D), one)],
        out_shape=[jax.ShapeDtypeStruct((T, D), F32), jax.ShapeDtypeStruct((1, D), F32)],
        scratch_shapes=[pltpu.VMEM((tm, D), F32)],
        compiler_params=_params("arbitrary", "arbitrary"),
    )(dh, h_in, gain, dG, dU, w_in, w_in)


def _matmul_tn(name, a, b, nj, a_block, a_map, b_block, b_map, out_shape, out_block, out_map, rider=None):
    T = a.shape[-2]
    tt = a_block[-2]
    nt = T // tt
    kb, nbk = out_block[-2], out_block[-1]

    def body(a_ref, b_ref, o_ref, acc):
        t = pl.program_id(1)

        @pl.when(t == 0)
        def _():
            acc[...] = jnp.zeros_like(acc)

        av = (a_ref[0] if len(a_block) == 3 else a_ref[...]).astype(BF16)
        bv = b_ref[0] if len(b_block) == 3 else b_ref[...]
        acc[...] += _dot_tn(av, bv)

        @pl.when(t == nt - 1)
        def _():
            if len(out_block) == 3:
                o_ref[0] = acc[...].astype(o_ref.dtype)
            else:
                o_ref[...] = acc[...].astype(o_ref.dtype)

    got = _pallas(
        body, rider, name=name, grid=(nj, nt),
        in_specs=[pl.BlockSpec(a_block, a_map), pl.BlockSpec(b_block, b_map)],
        out_specs=[pl.BlockSpec(out_block, out_map)],
        out_shape=[jax.ShapeDtypeStruct(out_shape, BF16)],
        scratch_shapes=[pltpu.VMEM((kb, nbk), F32)],
        compiler_params=_params("arbitrary", "arbitrary"),
    )(a, b)
    return got[0] if rider is None else got


def _dw_in(name, n, dG, dU, rider=None):
    T, kr = n.shape
    nh, _, FB = dG.shape
    tt = min(DW_IN_ROW_TILE, T)
    nt = T // tt
    cut = LANES * ((kr // LANES + 1) // 2)

    def body(n_ref, dg_ref, du_ref, o_ref, acc):
        j = pl.program_id(0)
        t = pl.program_id(1)

        @pl.when(t == 0)
        def _():
            acc[...] = jnp.zeros_like(acc)

        def add(dz_ref):
            for rows in ((slice(0, cut), slice(cut, kr)) if cut < kr else (slice(0, kr),)):
                acc[rows, :] += _dot_tn(n_ref[:, rows], dz_ref[0])

        @pl.when(j < nh)
        def _():
            add(dg_ref)

        @pl.when(j >= nh)
        def _():
            add(du_ref)

        @pl.when(t == nt - 1)
        def _():
            o_ref[0] = acc[...].astype(BF16)

    return _pallas(
        body, rider, name=name, grid=(2 * nh, nt),
        in_specs=[pl.BlockSpec((tt, kr), lambda j, t: (t, 0)),
                  pl.BlockSpec((1, tt, FB), lambda j, t: (jnp.minimum(j, nh - 1), t, 0)),
                  pl.BlockSpec((1, tt, FB), lambda j, t: (jnp.maximum(j - nh, 0), t, 0))],
        out_specs=[pl.BlockSpec((1, kr, FB), lambda j, t: (j, 0, 0))],
        out_shape=[jax.ShapeDtypeStruct((2 * nh, kr, FB), BF16)],
        scratch_shapes=[pltpu.VMEM((kr, FB), F32)],
        compiler_params=_params("arbitrary", "arbitrary"),
    )(n, dG, dU)


def _mix_in_fwd(h, gain, w):
    T, D = h.shape
    W = w.shape[1]
    nuv = 2 * GM_WIDTH
    tm = min(ROW_TILE, T)

    def body(h_ref, g_ref, w_ref, n_ref, zuv_ref, qkv_ref):
        _, xh = _rms_parts(h_ref[...])
        n = (xh * g_ref[...]).astype(BF16)
        n_ref[...] = n
        z = _dot(n, w_ref[...])
        zuv_ref[...] = z[:, :nuv]
        qkv_ref[...] = z[:, nuv:].astype(BF16)

    row = lambda i: (i, 0)
    return pl.pallas_call(
        body, name="mix_in_fwd", grid=(T // tm,),
        in_specs=[pl.BlockSpec((tm, D), row), pl.BlockSpec((1, D), lambda i: (0, 0)),
                  pl.BlockSpec((D, W), lambda i: (0, 0))],
        out_specs=[pl.BlockSpec((tm, D), row), pl.BlockSpec((tm, nuv), row),
                   pl.BlockSpec((tm, W - nuv), row)],
        out_shape=[jax.ShapeDtypeStruct((T, D), BF16), jax.ShapeDtypeStruct((T, nuv), F32),
                   jax.ShapeDtypeStruct((T, W - nuv), BF16)],
        compiler_params=_params("arbitrary"),
    )(h, gain, w)


def _mix_in_bwd(dzuv, dqkv, w, h, gain, dh):
    T, D = h.shape
    W = w.shape[1]
    nuv = dzuv.shape[1]
    tm = min(ROW_TILE, T)

    def body(dzuv_ref, dqkv_ref, w_ref, h_ref, g_ref, dh_ref, dhin_ref, dg_ref, half_ref):
        @pl.when(pl.program_id(0) == 0)
        def _():
            dg_ref[...] = jnp.zeros_like(dg_ref)

        dn = _dot_nt(dzuv_ref[...], w_ref[:, :nuv]) + _dot_nt(dqkv_ref[...], w_ref[:, nuv:])
        dx, dg = _rms_bwd(h_ref[...], g_ref[...], dn)
        dh_in = dh_ref[...] + dx
        dhin_ref[...] = dh_in
        half_ref[...] = (0.5 * dh_in).astype(BF16)
        dg_ref[...] += dg

    row = lambda i: (i, 0)
    one = lambda i: (0, 0)
    return pl.pallas_call(
        body, name="mix_in_bwd", grid=(T // tm,),
        in_specs=[pl.BlockSpec((tm, nuv), row), pl.BlockSpec((tm, W - nuv), row),
                  pl.BlockSpec((D, W), one), pl.BlockSpec((tm, D), row), pl.BlockSpec((1, D), one),
                  pl.BlockSpec((tm, D), row)],
        out_specs=[pl.BlockSpec((tm, D), row), pl.BlockSpec((1, D), one), pl.BlockSpec((tm, D), row)],
        out_shape=[jax.ShapeDtypeStruct((T, D), F32), jax.ShapeDtypeStruct((1, D), F32),
                   jax.ShapeDtypeStruct((T, D), BF16)],
        compiler_params=_params("arbitrary"),
    )(dzuv, dqkv, w, h, gain, dh)


def _gmlp_norm(zv, gv):
    v = _gelu(zv)
    r, vh = _rms_parts(v)
    return r, vh, (vh * gv).astype(BF16)


def _causal_ws(ws_ref, hd):
    r = lax.broadcasted_iota(jnp.int32, (CHUNK, CHUNK), 0)
    c = lax.broadcasted_iota(jnp.int32, (CHUNK, CHUNK), 1)
    return jnp.where(r >= c, ws_ref[hd], 0.0).astype(BF16)


def _gmlp_fwd(zuv, gv, ws, b_t):
    T = zuv.shape[0]
    tg = min(ROW_TILE, T)

    def body(zu_ref, zv_ref, gv_ref, ws_ref, bt_ref, o_ref):
        u = _gelu(zu_ref[...])
        _, _, vn = _gmlp_norm(zv_ref[...], gv_ref[...])
        for hd in range(GM_HEADS):
            wc = _causal_ws(ws_ref, hd)
            cols = slice(hd * CHUNK, (hd + 1) * CHUNK)
            for c in range(tg // CHUNK):
                rows = slice(c * CHUNK, (c + 1) * CHUNK)
                sv = _dot(wc, vn[rows, cols]) + bt_ref[:, hd:hd + 1]
                o_ref[rows, cols] = (u[rows, cols] * sv).astype(BF16)

    return pl.pallas_call(
        body, name="gmlp_fwd", grid=(T // tg,),
        in_specs=[pl.BlockSpec((tg, GM_WIDTH), lambda i: (i, 0)), pl.BlockSpec((tg, GM_WIDTH), lambda i: (i, 1)),
                  pl.BlockSpec((1, GM_WIDTH), lambda i: (0, 0)),
                  pl.BlockSpec((GM_HEADS, CHUNK, CHUNK), lambda i: (0, 0, 0)),
                  pl.BlockSpec((CHUNK, GM_HEADS), lambda i: (0, 0))],
        out_specs=pl.BlockSpec((tg, GM_WIDTH), lambda i: (i, 0)),
        out_shape=jax.ShapeDtypeStruct((T, GM_WIDTH), BF16),
        compiler_params=_params("arbitrary"),
    )(zuv, zuv, gv, ws, b_t)


def _gmlp_bwd(zuv, d_gm, gv, ws, b_t):
    T = zuv.shape[0]
    tg = min(ROW_TILE, T)
    ng = T // tg

    def body(zu_ref, zv_ref, dgm_ref, gv_ref, ws_ref, bt_ref, dz_ref, dgv_ref, dws_ref, dbt_ref, dsv_acc, dvn_s):
        i = pl.program_id(0)

        @pl.when(i == 0)
        def _():
            dgv_ref[...] = jnp.zeros_like(dgv_ref)
            dws_ref[...] = jnp.zeros_like(dws_ref)
            dsv_acc[...] = jnp.zeros_like(dsv_acc)

        zu = zu_ref[...]
        zv = zv_ref[...]
        dgm = dgm_ref[...]
        gvv = gv_ref[...]
        u = _gelu(zu)
        rv, vh, vn = _gmlp_norm(zv, gvv)
        dsv = dgm * u
        dsv_b = dsv.astype(BF16)
        for hd in range(GM_HEADS):
            wc = _causal_ws(ws_ref, hd)
            cols = slice(hd * CHUNK, (hd + 1) * CHUNK)
            dws = jnp.zeros((CHUNK, CHUNK), F32)
            dsv_sum = jnp.zeros((CHUNK, CHUNK), F32)
            for c in range(tg // CHUNK):
                rows = slice(c * CHUNK, (c + 1) * CHUNK)
                vch = vn[rows, cols]
                sv = _dot(wc, vch) + bt_ref[:, hd:hd + 1]
                dz_ref[rows, cols] = (dgm[rows, cols] * sv * _gelu_grad(zu[rows, cols])).astype(BF16)
                dws += _dot_nt(dsv_b[rows, cols], vch)
                dsv_sum += dsv[rows, cols]
                dvn_s[rows, cols] = _dot_tn(wc, dsv_b[rows, cols])
            dws_ref[hd] += dws
            dsv_acc[:, cols] += dsv_sum
        dvn = dvn_s[...]
        dvh = dvn * gvv
        dv = rv * (dvh - vh * jnp.mean(dvh * vh, axis=-1, keepdims=True))
        dgv_ref[...] += jnp.sum(dvn * vh, axis=0, keepdims=True)
        dz_ref[:, GM_WIDTH:] = (dv * _gelu_grad(zv)).astype(BF16)

        @pl.when(i == ng - 1)
        def _():
            r = lax.broadcasted_iota(jnp.int32, (CHUNK, CHUNK), 0)
            c = lax.broadcasted_iota(jnp.int32, (CHUNK, CHUNK), 1)
            for hd in range(GM_HEADS):
                dws_ref[hd] = jnp.where(r >= c, dws_ref[hd], 0.0)
                dbt_ref[:, hd:hd + 1] = jnp.sum(dsv_acc[:, hd * CHUNK:(hd + 1) * CHUNK], axis=1, keepdims=True)

    return pl.pallas_call(
        body, name="gmlp_bwd", grid=(ng,),
        in_specs=[pl.BlockSpec((tg, GM_WIDTH), lambda i: (i, 0)), pl.BlockSpec((tg, GM_WIDTH), lambda i: (i, 1)),
                  pl.BlockSpec((tg, GM_WIDTH), lambda i: (i, 0)),
                  pl.BlockSpec((1, GM_WIDTH), lambda i: (0, 0)),
                  pl.BlockSpec((GM_HEADS, CHUNK, CHUNK), lambda i: (0, 0, 0)),
                  pl.BlockSpec((CHUNK, GM_HEADS), lambda i: (0, 0))],
        out_specs=[pl.BlockSpec((tg, 2 * GM_WIDTH), lambda i: (i, 0)),
                   pl.BlockSpec((1, GM_WIDTH), lambda i: (0, 0)),
                   pl.BlockSpec((GM_HEADS, CHUNK, CHUNK), lambda i: (0, 0, 0)),
                   pl.BlockSpec((CHUNK, GM_HEADS), lambda i: (0, 0))],
        out_shape=[jax.ShapeDtypeStruct((T, 2 * GM_WIDTH), BF16), jax.ShapeDtypeStruct((1, GM_WIDTH), F32),
                   jax.ShapeDtypeStruct((GM_HEADS, CHUNK, CHUNK), F32),
                   jax.ShapeDtypeStruct((CHUNK, GM_HEADS), F32)],
        scratch_shapes=[pltpu.VMEM((CHUNK, GM_WIDTH), F32), pltpu.VMEM((tg, GM_WIDTH), F32)],
        compiler_params=_params("arbitrary"),
    )(zuv, zuv, d_gm, gv, ws, b_t)


def _scan_matrix(blk, keep):
    r = lax.broadcasted_iota(jnp.int32, (blk, blk), 0)
    c = lax.broadcasted_iota(jnp.int32, (blk, blk), 1)
    return jnp.where(keep(r, c), 1.0, 0.0).astype(BF16)


def _scan_matrix2(blk, keep, value):
    m = _scan_matrix(blk, keep) * value
    return jnp.concatenate([m, m], axis=0)


def _scan(x, mat2):
    hi, lo = _split_bf16(x)
    return _dot(jnp.concatenate([hi, lo], axis=1), mat2)


def _head_masks(q):
    lane = lax.broadcasted_iota(jnp.int32, q.shape, 1)
    m0 = lane < SB_HEAD_DIM
    zero = jnp.zeros_like(q)
    return m0, jnp.where(m0, q, zero), jnp.where(m0, zero, q)


_LOG2E = 1.4426950408889634


def _softplus_parts(z):
    e = jnp.exp2(jnp.abs(z) * (-_LOG2E))
    ope = 1.0 + e
    return e, ope, jnp.maximum(z, 0.0) + jnp.log(ope)


def _attn_fwd(qkv, rider=None):
    T = qkv.shape[0]
    tk = ATTN_KEY_BLOCK
    tq = min(ATTN_Q_ROWS, T)
    band = tq // tk
    assert band % ATTN_UNROLL == 0 or T == tq
    ngrp = SB_WIDTH // LANES

    def body(q_ref, k_ref, v_ref, o_ref, l_ref, acc, run):
        i = pl.program_id(1)
        suffix = _scan_matrix2(tk, lambda r, c: r >= c, -1.0)
        row = lax.broadcasted_iota(jnp.int32, (tq, tk), 0)
        col = lax.broadcasted_iota(jnp.int32, (tq, tk), 1)
        m0, q0, q1 = _head_masks(q_ref[...] * SB_SCALE)
        acc[...] = jnp.zeros_like(acc)
        run[...] = jnp.zeros_like(run)

        def tiles(work, rows=slice(None)):
            heads = (q0[rows], q1[rows])
            kv = []
            for j, _ in work:
                start = pl.multiple_of(j * tk, tk)
                kv.append((k_ref[pl.ds(start, tk), :], v_ref[pl.ds(start, tk), :]))
            z = [[_dot_nt(qh, kj) for qh in heads] for kj, _ in kv]
            sp = [[_softplus_parts(zz)[2] for zz in zt] for zt in z]
            sp = [[s if m is None else jnp.where(m, s, 0.0) for s in st] for st, (_, m) in zip(sp, work)]
            res = [[_scan(s, suffix) for s in st] for st in sp]
            runs = [run[hd, rows, :] for hd in range(len(heads))]
            a = []
            for t, (_, m) in enumerate(work):
                at = []
                for hd in range(len(heads)):
                    av = jnp.exp(z[t][hd] + (runs[hd] + res[t][hd]))
                    at.append(av if m is None else jnp.where(m, av, 0.0))
                    runs[hd] = runs[hd] + res[t][hd][:, 0:1]
                a.append(at)
            for hd in range(len(heads)):
                run[hd, rows, :] = runs[hd]
                upd = _dot(a[0][hd].astype(BF16), kv[0][1])
                for t in range(1, len(work)):
                    upd = upd + _dot(a[t][hd].astype(BF16), kv[t][1])
                acc[hd, rows, :] += upd

        for jb in reversed(range(band)):
            rows = slice(jb * tk, tq)
            tiles([(i * band + jb, (jb * tk + col < row)[rows])], rows)

        def full_step(it, carry):
            tiles([(i * band - 1 - ATTN_UNROLL * it - u, None) for u in range(ATTN_UNROLL)])
            return carry

        lax.fori_loop(0, i * (band // ATTN_UNROLL), full_step, 0)
        o_ref[...] = jnp.where(m0, acc[0], acc[1]).astype(BF16)
        l_ref[...] = jnp.where(m0, jnp.broadcast_to(run[0], (tq, LANES)), jnp.broadcast_to(run[1], (tq, LANES)))

    return _pallas(
        body, rider, name="attn_fwd", grid=(ngrp, T // tq),
        in_specs=[pl.BlockSpec((tq, LANES), lambda g, i: (i, g)),
                  pl.BlockSpec((T, LANES), lambda g, i: (0, ngrp + g)),
                  pl.BlockSpec((T, LANES), lambda g, i: (0, 2 * ngrp + g))],
        out_specs=[pl.BlockSpec((tq, LANES), lambda g, i: (i, g)),
                   pl.BlockSpec((tq, LANES), lambda g, i: (i, g))],
        out_shape=[jax.ShapeDtypeStruct((T, SB_WIDTH), BF16), jax.ShapeDtypeStruct((T, SB_WIDTH), F32)],
        scratch_shapes=[pltpu.VMEM((2, tq, LANES), F32), pltpu.VMEM((2, tq, 1), F32)],
        compiler_params=_params("arbitrary", "arbitrary"),
    )(qkv, qkv, qkv)


def _attn_bwd(qkv, d_o, ltot, rider=None):
    T = qkv.shape[0]
    tk = ATTN_KEY_BLOCK
    tq = min(ATTN_BWD_Q_ROWS, T)
    band = tq // tk
    nq = T // tq
    ngrp = SB_WIDTH // LANES

    def body(q_ref, k_ref, v_ref, do_ref, l_ref, dq_ref, dk_ref, dv_ref, dq_acc, dk_acc, dv_acc, lpre, ppre):
        i = pl.program_id(1)

        @pl.when(i == 0)
        def _():
            dk_acc[...] = jnp.zeros_like(dk_acc)
            dv_acc[...] = jnp.zeros_like(dv_acc)

        excl = _scan_matrix(tk, lambda r, c: r < c)
        excl2 = jnp.concatenate([excl, excl], axis=0)
        row = lax.broadcasted_iota(jnp.int32, (tq, tk), 0)
        col = lax.broadcasted_iota(jnp.int32, (tq, tk), 1)
        m0, q0, q1 = _head_masks(q_ref[...] * SB_SCALE)
        _, d0, d1 = _head_masks(do_ref[...].astype(BF16))
        lt = l_ref[...]
        ltots = (lt[:, 0:1], lt[:, SB_HEAD_DIM:SB_HEAD_DIM + 1])
        dq_acc[...] = jnp.zeros_like(dq_acc)
        lpre[...] = jnp.zeros_like(lpre)
        ppre[...] = jnp.zeros_like(ppre)

        def tiles(work, rows=slice(None)):
            heads = ((q0[rows], d0[rows]), (q1[rows], d1[rows]))
            lts = [lt[rows] for lt in ltots]
            nhd = len(heads)
            starts = [pl.multiple_of(j * tk, tk) for j, _ in work]
            kv = [(k_ref[pl.ds(st, tk), :], v_ref[pl.ds(st, tk), :]) for st in starts]
            masks = [m for _, m in work]
            every = [(t, hd) for t in range(len(work)) for hd in range(nhd)]
            z = {(t, hd): _dot_nt(heads[hd][0], kv[t][0]) for t, hd in every}
            da = {(t, hd): _dot_nt(heads[hd][1], kv[t][1]) for t, hd in every}
            sp, beta = {}, {}
            for key in every:
                s = _softplus_parts(z[key])[2]
                beta[key] = jnp.exp(z[key] - s)
                sp[key] = s if masks[key[0]] is None else jnp.where(masks[key[0]], s, 0.0)
            res = {key: _scan(sp[key], excl2) for key in every}
            lp = [lpre[hd, rows, :] for hd in range(nhd)]
            a, p = {}, {}
            for t, hd in every:
                av = jnp.exp(z[t, hd] + ((lts[hd] + lp[hd]) + res[t, hd]))
                a[t, hd] = av if masks[t] is None else jnp.where(masks[t], av, 0.0)
                p[t, hd] = a[t, hd] * da[t, hd]
                lp[hd] = lp[hd] + (res[t, hd][:, tk - 1:tk] + sp[t, hd][:, tk - 1:tk])
            resp = {key: _dot(p[key].astype(BF16), excl) for key in every}
            pp = [ppre[hd, rows, :] for hd in range(nhd)]
            dzb = {}
            for t, hd in every:
                dz = p[t, hd] - beta[t, hd] * (p[t, hd] + (pp[hd] + resp[t, hd]))
                if masks[t] is not None:
                    dz = jnp.where(masks[t], dz, 0.0)
                dzb[t, hd] = dz.astype(BF16)
                pp[hd] = pp[hd] + (resp[t, hd][:, tk - 1:tk] + p[t, hd][:, tk - 1:tk])
            for hd in range(nhd):
                lpre[hd, rows, :] = lp[hd]
                ppre[hd, rows, :] = pp[hd]
                upd = _dot(dzb[0, hd], kv[0][0])
                for t in range(1, len(work)):
                    upd = upd + _dot(dzb[t, hd], kv[t][0])
                dq_acc[hd, rows, :] += upd
            for t, st in enumerate(starts):
                dk = _dot_tn(dzb[t, 0], heads[0][0])
                dv = _dot_tn(a[t, 0].astype(BF16), heads[0][1])
                for hd in range(1, nhd):
                    dk = dk + _dot_tn(dzb[t, hd], heads[hd][0])
                    dv = dv + _dot_tn(a[t, hd].astype(BF16), heads[hd][1])
                dk_acc[pl.ds(st, tk), :] += dk
                dv_acc[pl.ds(st, tk), :] += dv

        def full_step(j, carry):
            tiles([(j, None)])
            return carry

        lax.fori_loop(0, i * band, full_step, 0)
        for jb in range(band):
            rows = slice(jb * tk, tq)
            tiles([(i * band + jb, (jb * tk + col < row)[rows])], rows)
        dq_ref[...] = (jnp.where(m0, dq_acc[0], dq_acc[1]) * SB_SCALE).astype(BF16)

        @pl.when(i == nq - 1)
        def _():
            dk_ref[...] = dk_acc[...].astype(BF16)
            dv_ref[...] = dv_acc[...].astype(BF16)

    qmap = lambda g, i: (i, g)
    return _pallas(
        body, rider, name="attn_bwd", grid=(ngrp, nq),
        in_specs=[pl.BlockSpec((tq, LANES), qmap),
                  pl.BlockSpec((T, LANES), lambda g, i: (0, ngrp + g)),
                  pl.BlockSpec((T, LANES), lambda g, i: (0, 2 * ngrp + g)),
                  pl.BlockSpec((tq, LANES), qmap), pl.BlockSpec((tq, LANES), qmap)],
        out_specs=[pl.BlockSpec((tq, LANES), qmap),
                   pl.BlockSpec((T, LANES), lambda g, i: (0, g)),
                   pl.BlockSpec((T, LANES), lambda g, i: (0, g))],
        out_shape=[jax.ShapeDtypeStruct((T, SB_WIDTH), BF16)] * 3,
        scratch_shapes=[pltpu.VMEM((2, tq, LANES), F32), pltpu.VMEM((T, LANES), F32),
                        pltpu.VMEM((T, LANES), F32), pltpu.VMEM((2, tq, 1), F32),
                        pltpu.VMEM((2, tq, 1), F32)],
        compiler_params=_params("arbitrary", "arbitrary"),
    )(qkv, qkv, qkv, d_o, ltot)


def _mix_out_fwd(h, gm, sb, w):
    T, D = h.shape
    tm = min(ROW_TILE, T)

    def body(h_ref, gm_ref, sb_ref, w_ref, o_ref):
        o_ref[...] = h_ref[...] + _dot(gm_ref[...], w_ref[:GM_WIDTH, :]) + _dot(sb_ref[...], w_ref[GM_WIDTH:, :])

    row = lambda i: (i, 0)
    return pl.pallas_call(
        body, name="mix_out_fwd", grid=(T // tm,),
        in_specs=[pl.BlockSpec((tm, D), row), pl.BlockSpec((tm, GM_WIDTH), row), pl.BlockSpec((tm, SB_WIDTH), row),
                  pl.BlockSpec((GM_WIDTH + SB_WIDTH, D), lambda i: (0, 0))],
        out_specs=pl.BlockSpec((tm, D), row),
        out_shape=jax.ShapeDtypeStruct((T, D), F32),
        compiler_params=_params("arbitrary"),
    )(h, gm, sb, w)


def _mix_out_bwd(dh, w):
    T, D = dh.shape
    tm = min(ROW_TILE, T)

    def body(dh_ref, w_ref, dgm_ref, dsb_ref, dhb_ref):
        dhb = dh_ref[...].astype(BF16)
        dhb_ref[...] = dhb
        dgm_ref[...] = _dot_nt(dhb, w_ref[:GM_WIDTH, :])
        dsb_ref[...] = _dot_nt(dhb, w_ref[GM_WIDTH:, :])

    row = lambda i: (i, 0)
    return pl.pallas_call(
        body, name="mix_out_bwd", grid=(T // tm,),
        in_specs=[pl.BlockSpec((tm, D), row), pl.BlockSpec((GM_WIDTH + SB_WIDTH, D), lambda i: (0, 0))],
        out_specs=[pl.BlockSpec((tm, GM_WIDTH), row), pl.BlockSpec((tm, SB_WIDTH), row), pl.BlockSpec((tm, D), row)],
        out_shape=[jax.ShapeDtypeStruct((T, GM_WIDTH), F32), jax.ShapeDtypeStruct((T, SB_WIDTH), F32),
                   jax.ShapeDtypeStruct((T, D), BF16)],
        compiler_params=_params("arbitrary"),
    )(dh, w)


def _tail(h3, p, target, g_ple, g_fin, w_gate, w_proj):
    T, D = h3.shape
    PD = p.shape[1]
    tm = min(ROW_TILE, T)

    def body(h_ref, p_ref, t_ref, gp_ref, gf_ref, wg_ref, wp_ref,
             loss_ref, dh_ref, n4_ref, dgl_ref, dpp_ref, dgp_ref, dgf_ref):
        @pl.when(pl.program_id(0) == 0)
        def _():
            loss_ref[...] = jnp.zeros_like(loss_ref)
            dgp_ref[...] = jnp.zeros_like(dgp_ref)
            dgf_ref[...] = jnp.zeros_like(dgf_ref)

        h3v = h_ref[...]
        gp = gp_ref[...]
        gf = gf_ref[...]
        r3, xh3 = _rms_parts(h3v)
        n4 = (xh3 * gp).astype(BF16)
        n4_ref[...] = n4
        gate = _sigmoid(_dot(n4, wg_ref[...]))
        pp = _dot(p_ref[...].astype(BF16), wp_ref[...])
        h4 = h3v + gate * pp
        r4, xh4 = _rms_parts(h4)
        err = xh4 * gf - t_ref[...]
        loss_ref[...] += jnp.full(loss_ref.shape, (0.5 / D) * jnp.sum(err * err), F32)
        dy = err * (1.0 / D)
        dgf_ref[...] += jnp.sum(dy * xh4, axis=0, keepdims=True)
        dyg = dy * gf
        dh4 = r4 * (dyg - xh4 * jnp.mean(dyg * xh4, axis=-1, keepdims=True))
        dpp_ref[...] = (dh4 * gate).astype(BF16)
        dgl = (dh4 * pp * gate * (1.0 - gate)).astype(BF16)
        dgl_ref[...] = dgl
        dn4 = _dot_nt(dgl, wg_ref[...])
        dgp_ref[...] += jnp.sum(dn4 * xh3, axis=0, keepdims=True)
        dn4g = dn4 * gp
        dh_ref[...] = dh4 + r3 * (dn4g - xh3 * jnp.mean(dn4g * xh3, axis=-1, keepdims=True))

    row = lambda i: (i, 0)
    one = lambda i: (0, 0)
    return pl.pallas_call(
        body, name="tail", grid=(T // tm,),
        in_specs=[pl.BlockSpec((tm, D), row), pl.BlockSpec((tm, PD), row), pl.BlockSpec((tm, D), row),
                  pl.BlockSpec((1, D), one), pl.BlockSpec((1, D), one),
                  pl.BlockSpec((D, D), one), pl.BlockSpec((PD, D), one)],
        out_specs=[pl.BlockSpec((1, LANES), one), pl.BlockSpec((tm, D), row), pl.BlockSpec((tm, D), row),
                   pl.BlockSpec((tm, D), row), pl.BlockSpec((tm, D), row),
                   pl.BlockSpec((1, D), one), pl.BlockSpec((1, D), one)],
        out_shape=[jax.ShapeDtypeStruct((1, LANES), F32), jax.ShapeDtypeStruct((T, D), F32),
                   jax.ShapeDtypeStruct((T, D), BF16), jax.ShapeDtypeStruct((T, D), BF16),
                   jax.ShapeDtypeStruct((T, D), BF16),
                   jax.ShapeDtypeStruct((1, D), F32), jax.ShapeDtypeStruct((1, D), F32)],
        compiler_params=_params("arbitrary"),
    )(h3, p, target, g_ple, g_fin, w_gate, w_proj)


FFN1_W = ("ffn1_w_in", "ffn1_w_out")
MIX_W = ("w_mix_in", "w_mix_out")
REST_W = ("ffn2_w_in", "ffn2_w_out", "ple_w_gate", "ple_w_proj")
BIG_W = FFN1_W + MIX_W + REST_W
COLUMN_SHARDED = ("w_mix_in", "ple_w_proj")


class _Traffic:
    def __init__(self, shards):
        self.shards = shards
        self.parts = {}
        self.blocks, self.sums = {}, {}

    @staticmethod
    def _full(name, gathered):
        if name in COLUMN_SHARDED:
            return jnp.transpose(gathered, (1, 0, 2)).reshape(gathered.shape[1], -1)
        if name.endswith("_w_in"):
            return gathered
        return gathered.reshape(-1, gathered.shape[-1])

    @staticmethod
    def _blocks(name, grad):
        name = name.split("/")[0]
        if name in COLUMN_SHARDED:
            return jnp.transpose(grad.reshape(grad.shape[0], N_DEV, -1), (1, 0, 2))
        if name.endswith("_w_in"):
            return grad
        return grad.reshape(N_DEV, -1, grad.shape[-1])

    def gather_now(self, names):
        got = _exchange("gather_" + names[0], [self.shards[n] for n in names], [GATHER] * len(names))
        return self.gathered(names, got)

    def gather_rider(self, names):
        return [self.shards[n] for n in names], [GATHER] * len(names)

    def gathered(self, names, got):
        return {n: self._full(n, g) for n, g in zip(names, got)}

    def scatter_rider(self, grads):
        return [self._blocks(n, g) for n, g in grads.items()], [SCATTER] * len(grads)

    def scattered(self, names, got):
        self.parts.update(zip(names, got))


    def pair_rider(self, name, grad):
        self.blocks[name] = self._blocks(name, grad)
        return [self.blocks[name]], [PAIR]

    def paired(self, name, got):
        self.sums[name] = _stage_pair_sums("pair_sum_" + name.replace("/", "_"), self.blocks[name], got)

    def chip_rider(self, name):
        return [self.sums[name]], [CHIP]

    def chipped(self, name, got):
        self.parts[name] = got


def _local_step(traffic, x, p, target, g1, gmix, gv, ws, b_t, g2, gple, gfin, pack_small):
    T, D = x.shape
    tm = min(ROW_TILE, T)

    w = traffic.gather_now(FFN1_W)
    h1, n1, G1, U1, a1, *got = _ffn_fwd("ffn1_fwd", x, g1, w["ffn1_w_in"], w["ffn1_w_out"],
                                        rider=traffic.gather_rider(MIX_W))
    w.update(traffic.gathered(MIX_W, got))
    n2, zuv, qkv = _mix_in_fwd(h1, gmix, w["w_mix_in"])
    gm = _gmlp_fwd(zuv, gv, ws, b_t)
    sb, ltot, *got = _attn_fwd(qkv, rider=traffic.gather_rider(REST_W))
    w.update(traffic.gathered(REST_W, got))
    h2 = _mix_out_fwd(h1, gm, sb, w["w_mix_out"])
    h3, n3, G2, U2, a2 = _ffn_fwd("ffn2_fwd", h2, g2, w["ffn2_w_in"], w["ffn2_w_out"])
    loss, dh3, n4, d_gl, d_pp, dg_ple, dg_fin = _tail(h3, p, target, gple, gfin, w["ple_w_gate"], w["ple_w_proj"])

    nb, _, FB = w["ffn1_w_in"].shape
    nh = nb // 2

    tt = min(GRAD_ROW_TILE, T)

    def dw_out(name, a, d_out):
        return _matmul_tn(name, a, d_out, nh, (1, tt, FB), lambda j, t: (j, t, 0), (tt, D), lambda j, t: (t, 0),
                          (nh, FB, D), (1, FB, D), lambda j, t: (j, 0, 0))

    def dense_tn(name, a, b, ncol):
        ka, nbw = a.shape[1], b.shape[1] // ncol
        return _matmul_tn(name, a, b, ncol, (tt, ka), lambda j, t: (t, 0), (tt, nbw), lambda j, t: (t, j),
                          (ka, b.shape[1]), (ka, nbw), lambda j, t: (0, j))

    grads = dict(ple_w_gate=dense_tn("dw_ple_gate", n4, d_gl, 2), ple_w_proj=dense_tn("dw_ple_proj", p, d_pp, 1))
    dh2, dg2, dG2, dU2, dout2 = _ffn_bwd("ffn2_bwd", dh3, h2, g2, G2, U2, w["ffn2_w_in"], w["ffn2_w_out"])
    grads["ffn2_w_in"], = _dw_in("ffn2_dw_in", n3, dG2, dU2)
    grads["ffn2_w_out"] = dw_out("ffn2_dw_out", a2, dout2)
    grads = {n: grads[n] for n in REST_W}

    d_gm, d_sb, dh2_bf = _mix_out_bwd(dh2, w["w_mix_out"])
    grads["w_mix_out"] = jnp.concatenate([dense_tn("dw_mix_out_gm", gm, dh2_bf, 1),
                                          dense_tn("dw_mix_out_sb", sb, dh2_bf, 1)], axis=0)
    dzuv, dgv, dws, db_t = _gmlp_bwd(zuv, d_gm, gv, ws, b_t)
    dq, dk, dv, *got = _attn_bwd(qkv, d_sb, ltot, rider=traffic.scatter_rider(grads))
    traffic.scattered(list(grads), got)
    dqkv = jnp.concatenate([dq, dk, dv], axis=1)
    dw_mi = jnp.concatenate([dense_tn("dw_mix_in_uv", n2, dzuv, 2), dense_tn("dw_mix_in_qkv", n2, dqkv, 3)], axis=1)
    dh1, dgmix, dout1 = _mix_in_bwd(dzuv, dqkv, w["w_mix_in"], h1, gmix, dh2)

    def dw_out_riding(name, a, d_out, rider):
        return _matmul_tn(name, a, d_out, nh, (1, tt, FB), lambda j, t: (j, t, 0), (tt, D), lambda j, t: (t, 0),
                          (nh, FB, D), (1, FB, D), lambda j, t: (j, 0, 0), rider=rider)

    def both(*riders):
        return [x for r in riders for x in r[0]], [k for r in riders for k in r[1]]

    dw_out1, got = dw_out_riding("ffn1_dw_out", a1, dout1, traffic.pair_rider("w_mix_in", dw_mi))
    traffic.paired("w_mix_in", got)
    dG1, dU1, got_mi, got = _ffn_bwd_gates("ffn1_bwd_gates", dout1, G1, U1, w["ffn1_w_out"],
                                           rider=both(traffic.chip_rider("w_mix_in"),
                                                      traffic.pair_rider("ffn1_w_out", dw_out1)))
    traffic.chipped("w_mix_in", got_mi)
    traffic.paired("ffn1_w_out", got)
    half = D // 2
    top, got = _dw_in("ffn1_dw_in_top", n1[:, :half], dG1, dU1, rider=traffic.chip_rider("ffn1_w_out"))
    traffic.chipped("ffn1_w_out", got)
    bottom, got = _dw_in("ffn1_dw_in_bottom", n1[:, half:], dG1, dU1, rider=traffic.pair_rider("ffn1_w_in/0", top))
    traffic.paired("ffn1_w_in/0", got)
    traffic.paired("ffn1_w_in/1", _exchange("pair_last", *traffic.pair_rider("ffn1_w_in/1", bottom))[0])
    dx, dg1, got_top, got_bottom = _ffn_bwd_input("ffn1_bwd_input", dh1, x, g1, dG1, dU1, w["ffn1_w_in"],
                                                  rider=both(traffic.chip_rider("ffn1_w_in/0"),
                                                             traffic.chip_rider("ffn1_w_in/1")))
    traffic.chipped("ffn1_w_in/0", got_top)
    traffic.chipped("ffn1_w_in/1", got_bottom)

    small = pack_small(dict(ffn1_norm=dg1, mix_norm=dgmix, gmlp_v_norm=dgv, gmlp_w_s=dws, gmlp_b=jnp.transpose(db_t),
                            ffn2_norm=dg2, ple_norm=dg_ple, final_norm=dg_fin), loss)
    return dx, small


def _my_index():
    return 4 * lax.axis_index("x") + 2 * lax.axis_index("y") + lax.axis_index("c")


def _stage_pair_sums(name, blocks, got):
    _, R, C = blocks.shape
    me = _my_index()
    index = jnp.stack([jnp.bitwise_xor(me, 2 * c) for c in range(1, N_CHIPS_AWAY + 1)] + [me, me]).astype(jnp.int32)

    def body(index_ref, b_ref, g_ref, o_ref):
        i = pl.program_id(0)

        @pl.when(i < N_CHIPS_AWAY)
        def _():
            o_ref[...] = (b_ref[...].astype(F32) + g_ref[...].astype(F32)).astype(BF16)

        @pl.when(i == N_CHIPS_AWAY)
        def _():
            o_ref[...] = g_ref[...]

        @pl.when(i == N_CHIPS_AWAY + 1)
        def _():
            o_ref[...] = b_ref[...]

    return pl.pallas_call(
        body, name=name,
        grid_spec=pltpu.PrefetchScalarGridSpec(
            num_scalar_prefetch=1, grid=(PART_SLOTS,),
            in_specs=[pl.BlockSpec((1, R, C), lambda i, idx: (idx[i], 0, 0)),
                      pl.BlockSpec((1, R, C), lambda i, idx: (jnp.minimum(i, N_CHIPS_AWAY), 0, 0))],
            out_specs=pl.BlockSpec((1, R, C), lambda i, idx: (i, 0, 0))),
        out_shape=jax.ShapeDtypeStruct((PART_SLOTS, R, C), BF16), compiler_params=_params("arbitrary"),
    )(index, blocks, got)


def _peer(d):
    x, y, c = lax.axis_index("x"), lax.axis_index("y"), lax.axis_index("c")
    px = 1 - x if d & 4 else x
    py = 1 - y if d & 2 else y
    pc = 1 - c if d & 1 else c
    return (px, py, pc), 4 * px + 2 * py + pc


GATHER, SCATTER, PAIR, CHIP = "gather", "scatter", "pair", "chip"


class _ExchangePlan:
    def __init__(self, ins, outs, send, recv, local, kinds):
        self.ins, self.outs, self.send, self.recv, self.local, self.kinds = ins, outs, send, recv, local, kinds
        self.scatter = [k == SCATTER for k in kinds]
        self.me = _peer(0)[1]

    def _remote(self, t, sem, src, slot, peer):
        return pltpu.make_async_remote_copy(
            src_ref=src, dst_ref=self.outs[t].at[slot], send_sem=self.send.at[t, sem], recv_sem=self.recv.at[t, sem],
            device_id=peer, device_id_type=MESH)

    def _own(self, t):
        if self.kinds[t] == CHIP:
            kept = pl.ds(N_CHIPS_AWAY, PART_SLOTS - N_CHIPS_AWAY)
            return pltpu.make_async_copy(self.ins[t].at[kept], self.outs[t].at[kept], self.local.at[t])
        src = self.ins[t].at[self.me] if self.scatter[t] else self.ins[t]
        return pltpu.make_async_copy(src, self.outs[t].at[self.me], self.local.at[t])

    def _n_direct(self, t):
        return {SCATTER: N_DEV - 1, GATHER: N_CHIPS_AWAY + 1, PAIR: N_CHIPS_AWAY + 1, CHIP: N_CHIPS_AWAY}[self.kinds[t]]

    def _direct(self, t, k):
        kind = self.kinds[t]
        if kind == SCATTER:
            peer, slot = _peer(k + 1)
            return self._remote(t, k, self.ins[t].at[slot], self.me, peer)
        if kind == GATHER:
            return self._remote(t, k, self.ins[t], self.me, _peer(2 * k if k else 1)[0])
        if kind == PAIR:
            block = _peer(2 * (k + 1) + 1 if k < N_CHIPS_AWAY else 1)[1]
            return self._remote(t, k, self.ins[t].at[block], k, _peer(1)[0])
        return self._remote(t, k, self.ins[t].at[k], k, _peer(2 * (k + 1))[0])

    def _has_own(self, t):
        return self.kinds[t] != PAIR

    def _relay(self, t, c):
        slot = _peer(2 * c)[1]
        return self._remote(t, N_CHIPS_AWAY + c, self.outs[t].at[slot], slot, _peer(1)[0])

    def start(self):
        for t in range(len(self.ins)):
            if self._has_own(t):
                self._own(t).start()
            for k in range(self._n_direct(t)):
                self._direct(t, k).start()

    def relay(self):
        for t in self._gathers():
            for c in range(1, N_CHIPS_AWAY + 1):
                self._direct(t, c).wait_recv()
                self._relay(t, c).start()

    def _gathers(self):
        return [t for t in range(len(self.ins)) if self.kinds[t] == GATHER]

    def finish(self):
        for t in range(len(self.ins)):
            if self._has_own(t):
                self._own(t).wait()
            for k in range(self._n_direct(t)):
                self._direct(t, k).wait_send()
                if self.kinds[t] != GATHER or k == 0:
                    self._direct(t, k).wait_recv()
        for t in self._gathers():
            for c in range(1, N_CHIPS_AWAY + 1):
                self._relay(t, c).wait()


def _exchange_shapes(arrays, kinds):
    shape = {GATHER: lambda a: (N_DEV,) + a.shape, SCATTER: lambda a: a.shape, CHIP: lambda a: a.shape,
             PAIR: lambda a: (N_CHIPS_AWAY + 1,) + a.shape[1:]}
    return [jax.ShapeDtypeStruct(shape[k](a), a.dtype) for a, k in zip(arrays, kinds)]


def _exchange_sems(n):
    return [pltpu.SemaphoreType.DMA((n, N_DEV - 1)), pltpu.SemaphoreType.DMA((n, N_DEV - 1)),
            pltpu.SemaphoreType.DMA((n,))]


_ANY = pl.BlockSpec(memory_space=pl.ANY)


def _exchange(name, arrays, scatter):
    n = len(arrays)

    def body(*refs):
        plan = _ExchangePlan(refs[:n], refs[n:2 * n], *refs[2 * n:], scatter)
        plan.start()
        plan.relay()
        plan.finish()

    return pl.pallas_call(
        body, name=name, in_specs=[_ANY] * n, out_specs=[_ANY] * n, out_shape=_exchange_shapes(arrays, scatter),
        scratch_shapes=_exchange_sems(n),
    )(*arrays)


def _pallas(body, rider, *, name, grid, in_specs, out_specs, out_shape, scratch_shapes=(), compiler_params=None):
    if rider is None:
        return pl.pallas_call(body, name=name, grid=grid, in_specs=in_specs, out_specs=out_specs, out_shape=out_shape,
                              scratch_shapes=list(scratch_shapes), compiler_params=compiler_params)
    arrays, scatter = rider
    n, ni, no, ns = len(arrays), len(in_specs), len(out_specs), len(scratch_shapes)

    def carried(*refs):
        ins, r_in = refs[:ni], refs[ni:ni + n]
        outs, r_out = refs[ni + n:ni + n + no], refs[ni + n + no:ni + 2 * n + no]
        scratch, sems = refs[ni + 2 * n + no:ni + 2 * n + no + ns], refs[ni + 2 * n + no + ns:]
        step = 0
        for ax, g in enumerate(grid):
            step = step * g + pl.program_id(ax)
        steps = functools.reduce(lambda a, b: a * b, grid)

        @pl.when(step == 0)
        def _():
            _ExchangePlan(r_in, r_out, *sems, scatter).start()

        @pl.when(step == steps // 2)
        def _():
            _ExchangePlan(r_in, r_out, *sems, scatter).relay()

        body(*ins, *outs, *scratch)

        @pl.when(step == steps - 1)
        def _():
            _ExchangePlan(r_in, r_out, *sems, scatter).finish()

    call = pl.pallas_call(
        carried, name=name, grid=grid, in_specs=list(in_specs) + [_ANY] * n, out_specs=list(out_specs) + [_ANY] * n,
        out_shape=list(out_shape) + _exchange_shapes(arrays, scatter),
        scratch_shapes=list(scratch_shapes) + _exchange_sems(n), compiler_params=compiler_params)
    return lambda *args: call(*args, *arrays)


def _adamw_math(g, w, m, v):
    m_new = ADAM_B1 * m + (1.0 - ADAM_B1) * g
    v_new = ADAM_B2 * v + (1.0 - ADAM_B2) * (g * g)
    m_hat = m_new / (1.0 - ADAM_B1 ** ADAM_STEP)
    v_hat = v_new / (1.0 - ADAM_B2 ** ADAM_STEP)
    delta = -ADAM_LR * (m_hat / (jnp.sqrt(v_hat) + ADAM_EPS) + ADAM_WD * w)
    return delta, m_new, v_new


def _adamw(name, parts, w, m, v, rider=None):
    R, C = w.shape
    slots = parts.shape[0]
    tr = R
    for cand in (256, 128, 64, 32, 16, 8):
        if R % cand == 0:
            tr = cand
            break

    def body(p_ref, w_ref, m_ref, v_ref, g_ref, d_ref, nm_ref, nv_ref):
        g = p_ref[0].astype(F32)
        for j in range(1, slots):
            g = g + p_ref[j].astype(F32)
        g_ref[...] = g
        d_ref[...], nm_ref[...], nv_ref[...] = _adamw_math(g, w_ref[...], m_ref[...], v_ref[...])

    row = lambda i: (i, 0)
    spec = pl.BlockSpec((tr, C), row)
    return _pallas(
        body, rider, name=name, grid=(R // tr,),
        in_specs=[pl.BlockSpec((slots, tr, C), lambda i: (0, i, 0)), spec, spec, spec],
        out_specs=[spec] * 4,
        out_shape=[jax.ShapeDtypeStruct((R, C), F32)] * 4,
        compiler_params=_params("arbitrary"),
    )(parts, w, m, v)


def _rows128(a):
    flat = a.reshape(-1, LANES)
    pad = (-flat.shape[0]) % SMALL_ROWS_ALIGN
    return jnp.pad(flat, ((0, pad), (0, 0))) if pad else flat


def _unrows(packed, like):
    n = like.size // LANES
    return packed[:n].reshape(like.shape)


def kernel(x, p, ffn1_norm, ffn1_w_in, ffn1_w_out, mix_norm, w_mix_in, gmlp_v_norm, gmlp_w_s, gmlp_b, w_mix_out, ffn2_norm, ffn2_w_in, ffn2_w_out, ple_norm, ple_w_gate, ple_w_proj, final_norm, loss_target, m_ffn1_norm, m_ffn1_w_in, m_ffn1_w_out, m_mix_norm, m_w_mix_in, m_gmlp_v_norm, m_gmlp_w_s, m_gmlp_b, m_w_mix_out, m_ffn2_norm, m_ffn2_w_in, m_ffn2_w_out, m_ple_norm, m_ple_w_gate, m_ple_w_proj, m_final_norm, v_ffn1_norm, v_ffn1_w_in, v_ffn1_w_out, v_mix_norm, v_w_mix_in, v_gmlp_v_norm, v_gmlp_w_s, v_gmlp_b, v_w_mix_out, v_ffn2_norm, v_ffn2_w_in, v_ffn2_w_out, v_ple_norm, v_ple_w_gate, v_ple_w_proj, v_final_norm):
    names = ["ffn1_norm", "ffn1_w_in", "ffn1_w_out", "mix_norm", "w_mix_in", "gmlp_v_norm", "gmlp_w_s", "gmlp_b",
             "w_mix_out", "ffn2_norm", "ffn2_w_in", "ffn2_w_out", "ple_norm", "ple_w_gate", "ple_w_proj", "final_norm"]
    W = dict(zip(names, [ffn1_norm, ffn1_w_in, ffn1_w_out, mix_norm, w_mix_in, gmlp_v_norm, gmlp_w_s, gmlp_b,
                         w_mix_out, ffn2_norm, ffn2_w_in, ffn2_w_out, ple_norm, ple_w_gate, ple_w_proj, final_norm]))
    M = dict(zip(names, [m_ffn1_norm, m_ffn1_w_in, m_ffn1_w_out, m_mix_norm, m_w_mix_in, m_gmlp_v_norm, m_gmlp_w_s,
                         m_gmlp_b, m_w_mix_out, m_ffn2_norm, m_ffn2_w_in, m_ffn2_w_out, m_ple_norm, m_ple_w_gate,
                         m_ple_w_proj, m_final_norm]))
    V = dict(zip(names, [v_ffn1_norm, v_ffn1_w_in, v_ffn1_w_out, v_mix_norm, v_w_mix_in, v_gmlp_v_norm, v_gmlp_w_s,
                         v_gmlp_b, v_w_mix_out, v_ffn2_norm, v_ffn2_w_in, v_ffn2_w_out, v_ple_norm, v_ple_w_gate,
                         v_ple_w_proj, v_final_norm]))
    small = [n for n in names if n not in BIG_W]
    D = x.shape[-1]

    def pack(src, last):
        return jnp.concatenate([_rows128(src[n]) for n in small] + [last], axis=0)

    offs = [0]
    for n in small:
        offs.append(offs[-1] + _rows128(W[n]).shape[0])

    traffic = _Traffic({n: W[n][0].astype(BF16) for n in BIG_W})
    dx, small_mine = _local_step(
        traffic, x[0], p[0, 0], loss_target[0],
        W["ffn1_norm"], W["mix_norm"], W["gmlp_v_norm"], W["gmlp_w_s"][0], jnp.transpose(W["gmlp_b"][0]),
        W["ffn2_norm"], W["ple_norm"], W["final_norm"].reshape(1, D),
        lambda grads, loss_part: pack(grads, jnp.broadcast_to(loss_part, (SMALL_ROWS_ALIGN, LANES))))

    out = {}
    parts = traffic.parts
    parts["ffn1_w_in"] = jnp.concatenate([parts["ffn1_w_in/0"], parts["ffn1_w_in/1"]], axis=1)
    carrier = "ffn2_w_out"
    *out[carrier], small_parts = _adamw("adamw_" + carrier, parts[carrier], W[carrier][0], M[carrier][0], V[carrier][0],
                                        rider=([small_mine], [GATHER]))
    for n in BIG_W:
        if n != carrier:
            out[n] = _adamw("adamw_" + n, parts[n], W[n][0], M[n][0], V[n][0])
    zeros = jnp.zeros((SMALL_ROWS_ALIGN, LANES), F32)
    sg, sd, sm, sv = _adamw("adamw_small", small_parts, pack(W, zeros), pack(M, zeros), pack(V, zeros))
    for k, n in enumerate(small):
        out[n] = tuple(_unrows(arr[offs[k]:offs[k + 1]], W[n]) for arr in (sg, sd, sm, sv))
    loss = sg[offs[len(small)], 0]

    res = [loss, dx[None]]
    for k in range(4):
        res += [out[n][k].reshape(W[n].shape) for n in names]
    return tuple(res)
```

```python
import functools

import jax
import jax.numpy as jnp
from jax import lax
from jax.experimental import pallas as pl
from jax.experimental.pallas import tpu as pltpu

F32 = jnp.float32
BF16 = jnp.bfloat16
MESH = pl.DeviceIdType.MESH

N_DEV = 8
N_CHIPS_AWAY = 3
PART_SLOTS = N_CHIPS_AWAY + 2
EPS = 1e-6
ADAM_LR = 0.001
ADAM_B1 = 0.9
ADAM_B2 = 0.999
ADAM_EPS = 1e-08
ADAM_WD = 0.01
ADAM_STEP = 10

GM_WIDTH = 512
GM_HEADS = 4
CHUNK = 128
SB_WIDTH = 512
SB_HEAD_DIM = 64
SB_SCALE = 0.125
LANES = 128
SMALL_ROWS_ALIGN = 8

ROW_TILE = 512
GRAD_ROW_TILE = 4096
DW_IN_ROW_TILE = 4096
FFN_FWD_ROW_TILE = 1024
ATTN_Q_ROWS = 512
ATTN_BWD_Q_ROWS = 512
ATTN_KEY_BLOCK = 256
ATTN_UNROLL = 2
VMEM_LIMIT = 56 * 1024 * 1024


def _params(*sem):
    return pltpu.CompilerParams(dimension_semantics=sem, vmem_limit_bytes=VMEM_LIMIT)


def _dot(a, b):
    return jnp.dot(a, b, preferred_element_type=F32)


def _dot_nt(a, b):
    return lax.dot_general(a, b, (((1,), (1,)), ((), ())), preferred_element_type=F32)


def _dot_tn(a, b):
    return lax.dot_general(a, b, (((0,), (0,)), ((), ())), preferred_element_type=F32)


def _rms_parts(x):
    r = lax.rsqrt(jnp.mean(x * x, axis=-1, keepdims=True) + EPS)
    return r, x * r


def _rms_bwd(x, g, dy):
    r, xh = _rms_parts(x)
    dyg = dy * g
    dx = r * (dyg - xh * jnp.mean(dyg * xh, axis=-1, keepdims=True))
    return dx, jnp.sum(dy * xh, axis=0, keepdims=True)


def _sigmoid(x):
    return 1.0 / (1.0 + jnp.exp(-x))


_SQRT_HALF = 0.7071067811865476
_INV_SQRT_2PI = 0.3989422804014327


def _gelu(x):
    return 0.5 * x * (1.0 + lax.erf(x * _SQRT_HALF))


def _gelu_grad(x):
    return 0.5 * (1.0 + lax.erf(x * _SQRT_HALF)) + x * (_INV_SQRT_2PI * jnp.exp(-0.5 * x * x))


def _split_bf16(x):
    hi = x.astype(BF16)
    lo = (x - hi.astype(F32)).astype(BF16)
    return hi, lo


def _ffn_fwd(name, h, gain, w_in, w_out, rider=None):
    T, D = h.shape
    nb, _, FB = w_in.shape
    nh = nb // 2
    tm = min(FFN_FWD_ROW_TILE, T)

    def body(h_ref, g_ref, wg_ref, wu_ref, wo_ref, ho_ref, n_ref, G_ref, U_ref, a_ref, n_s, acc):
        jj = pl.program_id(1)

        @pl.when(jj == 0)
        def _():
            _, xh = _rms_parts(h_ref[...])
            n = (xh * g_ref[...]).astype(BF16)
            n_s[...] = n
            n_ref[...] = n
            acc[...] = jnp.zeros_like(acc)

        n = n_s[...]
        G = _dot(n, wg_ref[0])
        U = _dot(n, wu_ref[0])
        G_ref[0] = G.astype(BF16)
        U_ref[0] = U.astype(BF16)
        a = (G * _sigmoid(G) * U).astype(BF16)
        a_ref[0] = a
        acc[...] += _dot(a, wo_ref[...])

        @pl.when(jj == nh - 1)
        def _():
            ho_ref[...] = h_ref[...] + 0.5 * acc[...]

    row = lambda i, j: (i, 0)
    blk = lambda i, j: (j, i, 0)
    return _pallas(
        body, rider, name=name, grid=(T // tm, nh),
        in_specs=[pl.BlockSpec((tm, D), row),
                  pl.BlockSpec((1, D), lambda i, j: (0, 0)),
                  pl.BlockSpec((1, D, FB), lambda i, j: (j, 0, 0)),
                  pl.BlockSpec((1, D, FB), lambda i, j: (j + nh, 0, 0)),
                  pl.BlockSpec((FB, D), lambda i, j: (j, 0))],
        out_specs=[pl.BlockSpec((tm, D), row), pl.BlockSpec((tm, D), row),
                   pl.BlockSpec((1, tm, FB), blk), pl.BlockSpec((1, tm, FB), blk),
                   pl.BlockSpec((1, tm, FB), blk)],
        out_shape=[jax.ShapeDtypeStruct((T, D), F32), jax.ShapeDtypeStruct((T, D), BF16),
                   jax.ShapeDtypeStruct((nh, T, FB), BF16), jax.ShapeDtypeStruct((nh, T, FB), BF16),
                   jax.ShapeDtypeStruct((nh, T, FB), BF16)],
        scratch_shapes=[pltpu.VMEM((tm, D), BF16), pltpu.VMEM((tm, D), F32)],
        compiler_params=_params("arbitrary", "arbitrary"),
    )(h, gain, w_in, w_in, w_out)


def _ffn_bwd(name, dh, h_in, gain, G, U, w_in, w_out, rider=None):
    T, D = dh.shape
    nb, _, FB = w_in.shape
    nh = nb // 2
    tm = min(ROW_TILE, T)

    def body(dh_ref, h_ref, g_ref, G_ref, U_ref, wg_ref, wu_ref, wo_ref,
             dhin_ref, dg_ref, dG_ref, dU_ref, do_ref, dn_acc, do_s):
        i = pl.program_id(0)
        jj = pl.program_id(1)

        @pl.when(jj == 0)
        def _():
            d_out = (0.5 * dh_ref[...]).astype(BF16)
            do_s[...] = d_out
            do_ref[...] = d_out
            dn_acc[...] = jnp.zeros_like(dn_acc)

        @pl.when((i == 0) & (jj == 0))
        def _():
            dg_ref[...] = jnp.zeros_like(dg_ref)

        halves = [slice(0, tm // 2), slice(tm // 2, tm)]
        da = [_dot_nt(do_s[rows, :], wo_ref[...]) for rows in halves]
        dGU = []
        for rows, dav in zip(halves, da):
            dG, dU = _gate_grads(dav, G_ref[0, rows, :].astype(F32), U_ref[0, rows, :].astype(F32))
            dG_ref[0, rows, :] = dG
            dU_ref[0, rows, :] = dU
            dGU.append((dG, dU))
        dn = [_dot_nt(dG, wg_ref[0]) for dG, _ in dGU]
        dn = [d + _dot_nt(dU, wu_ref[0]) for d, (_, dU) in zip(dn, dGU)]
        for rows, d in zip(halves, dn):
            dn_acc[rows, :] += d

        @pl.when(jj == nh - 1)
        def _():
            dx, dg = _rms_bwd(h_ref[...], g_ref[...], dn_acc[...])
            dhin_ref[...] = dh_ref[...] + dx
            dg_ref[...] += dg

    row = lambda i, j: (i, 0)
    blk = lambda i, j: (j, i, 0)
    one = lambda i, j: (0, 0)
    return _pallas(
        body, rider, name=name, grid=(T // tm, nh),
        in_specs=[pl.BlockSpec((tm, D), row), pl.BlockSpec((tm, D), row), pl.BlockSpec((1, D), one),
                  pl.BlockSpec((1, tm, FB), blk), pl.BlockSpec((1, tm, FB), blk),
                  pl.BlockSpec((1, D, FB), lambda i, j: (j, 0, 0)),
                  pl.BlockSpec((1, D, FB), lambda i, j: (j + nh, 0, 0)),
                  pl.BlockSpec((FB, D), lambda i, j: (j, 0))],
        out_specs=[pl.BlockSpec((tm, D), row), pl.BlockSpec((1, D), one),
                   pl.BlockSpec((1, tm, FB), blk), pl.BlockSpec((1, tm, FB), blk),
                   pl.BlockSpec((tm, D), row)],
        out_shape=[jax.ShapeDtypeStruct((T, D), F32), jax.ShapeDtypeStruct((1, D), F32),
                   jax.ShapeDtypeStruct((nh, T, FB), BF16), jax.ShapeDtypeStruct((nh, T, FB), BF16),
                   jax.ShapeDtypeStruct((T, D), BF16)],
        scratch_shapes=[pltpu.VMEM((tm, D), F32), pltpu.VMEM((tm, D), BF16)],
        compiler_params=_params("arbitrary", "arbitrary"),
    )(dh, h_in, gain, G, U, w_in, w_in, w_out)


def _gate_grads(dav, Gv, Uv):
    sig = _sigmoid(Gv)
    return (dav * Uv * (sig * (1.0 + Gv * (1.0 - sig)))).astype(BF16), (dav * (Gv * sig)).astype(BF16)


def _ffn_bwd_gates(name, d_out, G, U, w_out, rider=None):
    T, D = d_out.shape
    nh, _, FB = G.shape
    tm = min(FFN_FWD_ROW_TILE, T)

    def body(do_ref, G_ref, U_ref, wo_ref, dG_ref, dU_ref):
        halves = [slice(0, tm // 2), slice(tm // 2, tm)]
        da = [_dot_nt(do_ref[rows, :], wo_ref[...]) for rows in halves]
        for rows, dav in zip(halves, da):
            dG_ref[0, rows, :], dU_ref[0, rows, :] = _gate_grads(
                dav, G_ref[0, rows, :].astype(F32), U_ref[0, rows, :].astype(F32))

    blk = lambda i, j: (j, i, 0)
    return _pallas(
        body, rider, name=name, grid=(T // tm, nh),
        in_specs=[pl.BlockSpec((tm, D), lambda i, j: (i, 0)), pl.BlockSpec((1, tm, FB), blk),
                  pl.BlockSpec((1, tm, FB), blk), pl.BlockSpec((FB, D), lambda i, j: (j, 0))],
        out_specs=[pl.BlockSpec((1, tm, FB), blk), pl.BlockSpec((1, tm, FB), blk)],
        out_shape=[jax.ShapeDtypeStruct((nh, T, FB), BF16), jax.ShapeDtypeStruct((nh, T, FB), BF16)],
        compiler_params=_params("arbitrary", "arbitrary"),
    )(d_out, G, U, w_out)


def _ffn_bwd_input(name, dh, h_in, gain, dG, dU, w_in, rider=None):
    T, D = dh.shape
    nb, _, FB = w_in.shape
    nh = nb // 2
    tm = min(FFN_FWD_ROW_TILE, T)

    def body(dh_ref, h_ref, g_ref, dG_ref, dU_ref, wg_ref, wu_ref, dhin_ref, dg_ref, dn_acc):
        i = pl.program_id(0)
        jj = pl.program_id(1)

        @pl.when(jj == 0)
        def _():
            dn_acc[...] = jnp.zeros_like(dn_acc)

        @pl.when((i == 0) & (jj == 0))
        def _():
            dg_ref[...] = jnp.zeros_like(dg_ref)

        halves = [slice(0, tm // 2), slice(tm // 2, tm)]
        dn = [_dot_nt(dG_ref[0, rows, :], wg_ref[0]) for rows in halves]
        dn = [d + _dot_nt(dU_ref[0, rows, :], wu_ref[0]) for d, rows in zip(dn, halves)]
        for rows, d in zip(halves, dn):
            dn_acc[rows, :] += d

        @pl.when(jj == nh - 1)
        def _():
            dx, dg = _rms_bwd(h_ref[...], g_ref[...], dn_acc[...])
            dhin_ref[...] = dh_ref[...] + dx
            dg_ref[...] += dg

    row = lambda i, j: (i, 0)
    blk = lambda i, j: (j, i, 0)
    one = lambda i, j: (0, 0)
    return _pallas(
        body, rider, name=name, grid=(T // tm, nh),
        in_specs=[pl.BlockSpec((tm, D), row), pl.BlockSpec((tm, D), row), pl.BlockSpec((1, D), one),
                  pl.BlockSpec((1, tm, FB), blk), pl.BlockSpec((1, tm, FB), blk),
                  pl.BlockSpec((1, D, FB), lambda i, j: (j, 0, 0)),
                  pl.BlockSpec((1, D, FB), lambda i, j: (j + nh, 0, 0))],
        out_specs=[pl.BlockSpec((tm, D), row), pl.BlockSpec((1, D), one)],
        out_shape=[jax.ShapeDtypeStruct((T, D), F32), jax.ShapeDtypeStruct((1, D), F32)],
        scratch_shapes=[pltpu.VMEM((tm, D), F32)],
        compiler_params=_params("arbitrary", "arbitrary"),
    )(dh, h_in, gain, dG, dU, w_in, w_in)


def _matmul_tn(name, a, b, nj, a_block, a_map, b_block, b_map, out_shape, out_block, out_map, rider=None):
    T = a.shape[-2]
    tt = a_block[-2]
    nt = T // tt
    kb, nbk = out_block[-2], out_block[-1]

    def body(a_ref, b_ref, o_ref, acc):
        t = pl.program_id(1)

        @pl.when(t == 0)
        def _():
            acc[...] = jnp.zeros_like(acc)

        av = (a_ref[0] if len(a_block) == 3 else a_ref[...]).astype(BF16)
        bv = b_ref[0] if len(b_block) == 3 else b_ref[...]
        acc[...] += _dot_tn(av, bv)

        @pl.when(t == nt - 1)
        def _():
            if len(out_block) == 3:
                o_ref[0] = acc[...].astype(o_ref.dtype)
            else:
                o_ref[...] = acc[...].astype(o_ref.dtype)

    got = _pallas(
        body, rider, name=name, grid=(nj, nt),
        in_specs=[pl.BlockSpec(a_block, a_map), pl.BlockSpec(b_block, b_map)],
        out_specs=[pl.BlockSpec(out_block, out_map)],
        out_shape=[jax.ShapeDtypeStruct(out_shape, BF16)],
        scratch_shapes=[pltpu.VMEM((kb, nbk), F32)],
        compiler_params=_params("arbitrary", "arbitrary"),
    )(a, b)
    return got[0] if rider is None else got


def _dw_in(name, n, dG, dU, rider=None):
    T, kr = n.shape
    nh, _, FB = dG.shape
    tt = min(DW_IN_ROW_TILE, T)
    nt = T // tt
    cut = LANES * ((kr // LANES + 1) // 2)

    def body(n_ref, dg_ref, du_ref, o_ref, acc):
        j = pl.program_id(0)
        t = pl.program_id(1)

        @pl.when(t == 0)
        def _():
            acc[...] = jnp.zeros_like(acc)

        def add(dz_ref):
            for rows in ((slice(0, cut), slice(cut, kr)) if cut < kr else (slice(0, kr),)):
                acc[rows, :] += _dot_tn(n_ref[:, rows], dz_ref[0])

        @pl.when(j < nh)
        def _():
            add(dg_ref)

        @pl.when(j >= nh)
        def _():
            add(du_ref)

        @pl.when(t == nt - 1)
        def _():
            o_ref[0] = acc[...].astype(BF16)

    return _pallas(
        body, rider, name=name, grid=(2 * nh, nt),
        in_specs=[pl.BlockSpec((tt, kr), lambda j, t: (t, 0)),
                  pl.BlockSpec((1, tt, FB), lambda j, t: (jnp.minimum(j, nh - 1), t, 0)),
                  pl.BlockSpec((1, tt, FB), lambda j, t: (jnp.maximum(j - nh, 0), t, 0))],
        out_specs=[pl.BlockSpec((1, kr, FB), lambda j, t: (j, 0, 0))],
        out_shape=[jax.ShapeDtypeStruct((2 * nh, kr, FB), BF16)],
        scratch_shapes=[pltpu.VMEM((kr, FB), F32)],
        compiler_params=_params("arbitrary", "arbitrary"),
    )(n, dG, dU)


def _mix_in_fwd(h, gain, w):
    T, D = h.shape
    W = w.shape[1]
    nuv = 2 * GM_WIDTH
    tm = min(ROW_TILE, T)

    def body(h_ref, g_ref, w_ref, n_ref, zuv_ref, qkv_ref):
        _, xh = _rms_parts(h_ref[...])
        n = (xh * g_ref[...]).astype(BF16)
        n_ref[...] = n
        z = _dot(n, w_ref[...])
        zuv_ref[...] = z[:, :nuv]
        qkv_ref[...] = z[:, nuv:].astype(BF16)

    row = lambda i: (i, 0)
    return pl.pallas_call(
        body, name="mix_in_fwd", grid=(T // tm,),
        in_specs=[pl.BlockSpec((tm, D), row), pl.BlockSpec((1, D), lambda i: (0, 0)),
                  pl.BlockSpec((D, W), lambda i: (0, 0))],
        out_specs=[pl.BlockSpec((tm, D), row), pl.BlockSpec((tm, nuv), row),
                   pl.BlockSpec((tm, W - nuv), row)],
        out_shape=[jax.ShapeDtypeStruct((T, D), BF16), jax.ShapeDtypeStruct((T, nuv), F32),
                   jax.ShapeDtypeStruct((T, W - nuv), BF16)],
        compiler_params=_params("arbitrary"),
    )(h, gain, w)


def _mix_in_bwd(dzuv, dqkv, w, h, gain, dh):
    T, D = h.shape
    W = w.shape[1]
    nuv = dzuv.shape[1]
    tm = min(ROW_TILE, T)

    def body(dzuv_ref, dqkv_ref, w_ref, h_ref, g_ref, dh_ref, dhin_ref, dg_ref, half_ref):
        @pl.when(pl.program_id(0) == 0)
        def _():
            dg_ref[...] = jnp.zeros_like(dg_ref)

        dn = _dot_nt(dzuv_ref[...], w_ref[:, :nuv]) + _dot_nt(dqkv_ref[...], w_ref[:, nuv:])
        dx, dg = _rms_bwd(h_ref[...], g_ref[...], dn)
        dh_in = dh_ref[...] + dx
        dhin_ref[...] = dh_in
        half_ref[...] = (0.5 * dh_in).astype(BF16)
        dg_ref[...] += dg

    row = lambda i: (i, 0)
    one = lambda i: (0, 0)
    return pl.pallas_call(
        body, name="mix_in_bwd", grid=(T // tm,),
        in_specs=[pl.BlockSpec((tm, nuv), row), pl.BlockSpec((tm, W - nuv), row),
                  pl.BlockSpec((D, W), one), pl.BlockSpec((tm, D), row), pl.BlockSpec((1, D), one),
                  pl.BlockSpec((tm, D), row)],
        out_specs=[pl.BlockSpec((tm, D), row), pl.BlockSpec((1, D), one), pl.BlockSpec((tm, D), row)],
        out_shape=[jax.ShapeDtypeStruct((T, D), F32), jax.ShapeDtypeStruct((1, D), F32),
                   jax.ShapeDtypeStruct((T, D), BF16)],
        compiler_params=_params("arbitrary"),
    )(dzuv, dqkv, w, h, gain, dh)


def _gmlp_norm(zv, gv):
    v = _gelu(zv)
    r, vh = _rms_parts(v)
    return r, vh, (vh * gv).astype(BF16)


def _causal_ws(ws_ref, hd):
    r = lax.broadcasted_iota(jnp.int32, (CHUNK, CHUNK), 0)
    c = lax.broadcasted_iota(jnp.int32, (CHUNK, CHUNK), 1)
    return jnp.where(r >= c, ws_ref[hd], 0.0).astype(BF16)


def _gmlp_fwd(zuv, gv, ws, b_t):
    T = zuv.shape[0]
    tg = min(ROW_TILE, T)

    def body(zu_ref, zv_ref, gv_ref, ws_ref, bt_ref, o_ref):
        u = _gelu(zu_ref[...])
        _, _, vn = _gmlp_norm(zv_ref[...], gv_ref[...])
        for hd in range(GM_HEADS):
            wc = _causal_ws(ws_ref, hd)
            cols = slice(hd * CHUNK, (hd + 1) * CHUNK)
            for c in range(tg // CHUNK):
                rows = slice(c * CHUNK, (c + 1) * CHUNK)
                sv = _dot(wc, vn[rows, cols]) + bt_ref[:, hd:hd + 1]
                o_ref[rows, cols] = (u[rows, cols] * sv).astype(BF16)

    return pl.pallas_call(
        body, name="gmlp_fwd", grid=(T // tg,),
        in_specs=[pl.BlockSpec((tg, GM_WIDTH), lambda i: (i, 0)), pl.BlockSpec((tg, GM_WIDTH), lambda i: (i, 1)),
                  pl.BlockSpec((1, GM_WIDTH), lambda i: (0, 0)),
                  pl.BlockSpec((GM_HEADS, CHUNK, CHUNK), lambda i: (0, 0, 0)),
                  pl.BlockSpec((CHUNK, GM_HEADS), lambda i: (0, 0))],
        out_specs=pl.BlockSpec((tg, GM_WIDTH), lambda i: (i, 0)),
        out_shape=jax.ShapeDtypeStruct((T, GM_WIDTH), BF16),
        compiler_params=_params("arbitrary"),
    )(zuv, zuv, gv, ws, b_t)


def _gmlp_bwd(zuv, d_gm, gv, ws, b_t):
    T = zuv.shape[0]
    tg = min(ROW_TILE, T)
    ng = T // tg

    def body(zu_ref, zv_ref, dgm_ref, gv_ref, ws_ref, bt_ref, dz_ref, dgv_ref, dws_ref, dbt_ref, dsv_acc, dvn_s):
        i = pl.program_id(0)

        @pl.when(i == 0)
        def _():
            dgv_ref[...] = jnp.zeros_like(dgv_ref)
            dws_ref[...] = jnp.zeros_like(dws_ref)
            dsv_acc[...] = jnp.zeros_like(dsv_acc)

        zu = zu_ref[...]
        zv = zv_ref[...]
        dgm = dgm_ref[...]
        gvv = gv_ref[...]
        u = _gelu(zu)
        rv, vh, vn = _gmlp_norm(zv, gvv)
        dsv = dgm * u
        dsv_b = dsv.astype(BF16)
        for hd in range(GM_HEADS):
            wc = _causal_ws(ws_ref, hd)
            cols = slice(hd * CHUNK, (hd + 1) * CHUNK)
            dws = jnp.zeros((CHUNK, CHUNK), F32)
            dsv_sum = jnp.zeros((CHUNK, CHUNK), F32)
            for c in range(tg // CHUNK):
                rows = slice(c * CHUNK, (c + 1) * CHUNK)
                vch = vn[rows, cols]
                sv = _dot(wc, vch) + bt_ref[:, hd:hd + 1]
                dz_ref[rows, cols] = (dgm[rows, cols] * sv * _gelu_grad(zu[rows, cols])).astype(BF16)
                dws += _dot_nt(dsv_b[rows, cols], vch)
                dsv_sum += dsv[rows, cols]
                dvn_s[rows, cols] = _dot_tn(wc, dsv_b[rows, cols])
            dws_ref[hd] += dws
            dsv_acc[:, cols] += dsv_sum
        dvn = dvn_s[...]
        dvh = dvn * gvv
        dv = rv * (dvh - vh * jnp.mean(dvh * vh, axis=-1, keepdims=True))
        dgv_ref[...] += jnp.sum(dvn * vh, axis=0, keepdims=True)
        dz_ref[:, GM_WIDTH:] = (dv * _gelu_grad(zv)).astype(BF16)

        @pl.when(i == ng - 1)
        def _():
            r = lax.broadcasted_iota(jnp.int32, (CHUNK, CHUNK), 0)
            c = lax.broadcasted_iota(jnp.int32, (CHUNK, CHUNK), 1)
            for hd in range(GM_HEADS):
                dws_ref[hd] = jnp.where(r >= c, dws_ref[hd], 0.0)
                dbt_ref[:, hd:hd + 1] = jnp.sum(dsv_acc[:, hd * CHUNK:(hd + 1) * CHUNK], axis=1, keepdims=True)

    return pl.pallas_call(
        body, name="gmlp_bwd", grid=(ng,),
        in_specs=[pl.BlockSpec((tg, GM_WIDTH), lambda i: (i, 0)), pl.BlockSpec((tg, GM_WIDTH), lambda i: (i, 1)),
                  pl.BlockSpec((tg, GM_WIDTH), lambda i: (i, 0)),
                  pl.BlockSpec((1, GM_WIDTH), lambda i: (0, 0)),
                  pl.BlockSpec((GM_HEADS, CHUNK, CHUNK), lambda i: (0, 0, 0)),
                  pl.BlockSpec((CHUNK, GM_HEADS), lambda i: (0, 0))],
        out_specs=[pl.BlockSpec((tg, 2 * GM_WIDTH), lambda i: (i, 0)),
                   pl.BlockSpec((1, GM_WIDTH), lambda i: (0, 0)),
                   pl.BlockSpec((GM_HEADS, CHUNK, CHUNK), lambda i: (0, 0, 0)),
                   pl.BlockSpec((CHUNK, GM_HEADS), lambda i: (0, 0))],
        out_shape=[jax.ShapeDtypeStruct((T, 2 * GM_WIDTH), BF16), jax.ShapeDtypeStruct((1, GM_WIDTH), F32),
                   jax.ShapeDtypeStruct((GM_HEADS, CHUNK, CHUNK), F32),
                   jax.ShapeDtypeStruct((CHUNK, GM_HEADS), F32)],
        scratch_shapes=[pltpu.VMEM((CHUNK, GM_WIDTH), F32), pltpu.VMEM((tg, GM_WIDTH), F32)],
        compiler_params=_params("arbitrary"),
    )(zuv, zuv, d_gm, gv, ws, b_t)


def _scan_matrix(blk, keep):
    r = lax.broadcasted_iota(jnp.int32, (blk, blk), 0)
    c = lax.broadcasted_iota(jnp.int32, (blk, blk), 1)
    return jnp.where(keep(r, c), 1.0, 0.0).astype(BF16)


def _scan_matrix2(blk, keep, value):
    m = _scan_matrix(blk, keep) * value
    return jnp.concatenate([m, m], axis=0)


def _scan(x, mat2):
    hi, lo = _split_bf16(x)
    return _dot(jnp.concatenate([hi, lo], axis=1), mat2)


def _head_masks(q):
    lane = lax.broadcasted_iota(jnp.int32, q.shape, 1)
    m0 = lane < SB_HEAD_DIM
    zero = jnp.zeros_like(q)
    return m0, jnp.where(m0, q, zero), jnp.where(m0, zero, q)


_LOG2E = 1.4426950408889634


def _softplus_parts(z):
    e = jnp.exp2(jnp.abs(z) * (-_LOG2E))
    ope = 1.0 + e
    return e, ope, jnp.maximum(z, 0.0) + jnp.log(ope)


def _attn_fwd(qkv, rider=None):
    T = qkv.shape[0]
    tk = ATTN_KEY_BLOCK
    tq = min(ATTN_Q_ROWS, T)
    band = tq // tk
    assert band % ATTN_UNROLL == 0 or T == tq
    ngrp = SB_WIDTH // LANES

    def body(q_ref, k_ref, v_ref, o_ref, l_ref, acc, run):
        i = pl.program_id(1)
        suffix = _scan_matrix2(tk, lambda r, c: r >= c, -1.0)
        row = lax.broadcasted_iota(jnp.int32, (tq, tk), 0)
        col = lax.broadcasted_iota(jnp.int32, (tq, tk), 1)
        m0, q0, q1 = _head_masks(q_ref[...] * SB_SCALE)
        acc[...] = jnp.zeros_like(acc)
        run[...] = jnp.zeros_like(run)

        def tiles(work, rows=slice(None)):
            heads = (q0[rows], q1[rows])
            kv = []
            for j, _ in work:
                start = pl.multiple_of(j * tk, tk)
                kv.append((k_ref[pl.ds(start, tk), :], v_ref[pl.ds(start, tk), :]))
            z = [[_dot_nt(qh, kj) for qh in heads] for kj, _ in kv]
            sp = [[_softplus_parts(zz)[2] for zz in zt] for zt in z]
            sp = [[s if m is None else jnp.where(m, s, 0.0) for s in st] for st, (_, m) in zip(sp, work)]
            res = [[_scan(s, suffix) for s in st] for st in sp]
            runs = [run[hd, rows, :] for hd in range(len(heads))]
            a = []
            for t, (_, m) in enumerate(work):
                at = []
                for hd in range(len(heads)):
                    av = jnp.exp(z[t][hd] + (runs[hd] + res[t][hd]))
                    at.append(av if m is None else jnp.where(m, av, 0.0))
                    runs[hd] = runs[hd] + res[t][hd][:, 0:1]
                a.append(at)
            for hd in range(len(heads)):
                run[hd, rows, :] = runs[hd]
                upd = _dot(a[0][hd].astype(BF16), kv[0][1])
                for t in range(1, len(work)):
                    upd = upd + _dot(a[t][hd].astype(BF16), kv[t][1])
                acc[hd, rows, :] += upd

        for jb in reversed(range(band)):
            rows = slice(jb * tk, tq)
            tiles([(i * band + jb, (jb * tk + col < row)[rows])], rows)

        def full_step(it, carry):
            tiles([(i * band - 1 - ATTN_UNROLL * it - u, None) for u in range(ATTN_UNROLL)])
            return carry

        lax.fori_loop(0, i * (band // ATTN_UNROLL), full_step, 0)
        o_ref[...] = jnp.where(m0, acc[0], acc[1]).astype(BF16)
        l_ref[...] = jnp.where(m0, jnp.broadcast_to(run[0], (tq, LANES)), jnp.broadcast_to(run[1], (tq, LANES)))

    return _pallas(
        body, rider, name="attn_fwd", grid=(ngrp, T // tq),
        in_specs=[pl.BlockSpec((tq, LANES), lambda g, i: (i, g)),
                  pl.BlockSpec((T, LANES), lambda g, i: (0, ngrp + g)),
                  pl.BlockSpec((T, LANES), lambda g, i: (0, 2 * ngrp + g))],
        out_specs=[pl.BlockSpec((tq, LANES), lambda g, i: (i, g)),
                   pl.BlockSpec((tq, LANES), lambda g, i: (i, g))],
        out_shape=[jax.ShapeDtypeStruct((T, SB_WIDTH), BF16), jax.ShapeDtypeStruct((T, SB_WIDTH), F32)],
        scratch_shapes=[pltpu.VMEM((2, tq, LANES), F32), pltpu.VMEM((2, tq, 1), F32)],
        compiler_params=_params("arbitrary", "arbitrary"),
    )(qkv, qkv, qkv)


def _attn_bwd(qkv, d_o, ltot, rider=None):
    T = qkv.shape[0]
    tk = ATTN_KEY_BLOCK
    tq = min(ATTN_BWD_Q_ROWS, T)
    band = tq // tk
    nq = T // tq
    ngrp = SB_WIDTH // LANES

    def body(q_ref, k_ref, v_ref, do_ref, l_ref, dq_ref, dk_ref, dv_ref, dq_acc, dk_acc, dv_acc, lpre, ppre):
        i = pl.program_id(1)

        @pl.when(i == 0)
        def _():
            dk_acc[...] = jnp.zeros_like(dk_acc)
            dv_acc[...] = jnp.zeros_like(dv_acc)

        excl = _scan_matrix(tk, lambda r, c: r < c)
        excl2 = jnp.concatenate([excl, excl], axis=0)
        row = lax.broadcasted_iota(jnp.int32, (tq, tk), 0)
        col = lax.broadcasted_iota(jnp.int32, (tq, tk), 1)
        m0, q0, q1 = _head_masks(q_ref[...] * SB_SCALE)
        _, d0, d1 = _head_masks(do_ref[...].astype(BF16))
        lt = l_ref[...]
        ltots = (lt[:, 0:1], lt[:, SB_HEAD_DIM:SB_HEAD_DIM + 1])
        dq_acc[...] = jnp.zeros_like(dq_acc)
        lpre[...] = jnp.zeros_like(lpre)
        ppre[...] = jnp.zeros_like(ppre)

        def tiles(work, rows=slice(None)):
            heads = ((q0[rows], d0[rows]), (q1[rows], d1[rows]))
            lts = [lt[rows] for lt in ltots]
            nhd = len(heads)
            starts = [pl.multiple_of(j * tk, tk) for j, _ in work]
            kv = [(k_ref[pl.ds(st, tk), :], v_ref[pl.ds(st, tk), :]) for st in starts]
            masks = [m for _, m in work]
            every = [(t, hd) for t in range(len(work)) for hd in range(nhd)]
            z = {(t, hd): _dot_nt(heads[hd][0], kv[t][0]) for t, hd in every}
            da = {(t, hd): _dot_nt(heads[hd][1], kv[t][1]) for t, hd in every}
            sp, beta = {}, {}
            for key in every:
                s = _softplus_parts(z[key])[2]
                beta[key] = jnp.exp(z[key] - s)
                sp[key] = s if masks[key[0]] is None else jnp.where(masks[key[0]], s, 0.0)
            res = {key: _scan(sp[key], excl2) for key in every}
            lp = [lpre[hd, rows, :] for hd in range(nhd)]
            a, p = {}, {}
            for t, hd in every:
                av = jnp.exp(z[t, hd] + ((lts[hd] + lp[hd]) + res[t, hd]))
                a[t, hd] = av if masks[t] is None else jnp.where(masks[t], av, 0.0)
                p[t, hd] = a[t, hd] * da[t, hd]
                lp[hd] = lp[hd] + (res[t, hd][:, tk - 1:tk] + sp[t, hd][:, tk - 1:tk])
            resp = {key: _dot(p[key].astype(BF16), excl) for key in every}
            pp = [ppre[hd, rows, :] for hd in range(nhd)]
            dzb = {}
            for t, hd in every:
                dz = p[t, hd] - beta[t, hd] * (p[t, hd] + (pp[hd] + resp[t, hd]))
                if masks[t] is not None:
                    dz = jnp.where(masks[t], dz, 0.0)
                dzb[t, hd] = dz.astype(BF16)
                pp[hd] = pp[hd] + (resp[t, hd][:, tk - 1:tk] + p[t, hd][:, tk - 1:tk])
            for hd in range(nhd):
                lpre[hd, rows, :] = lp[hd]
                ppre[hd, rows, :] = pp[hd]
                upd = _dot(dzb[0, hd], kv[0][0])
                for t in range(1, len(work)):
                    upd = upd + _dot(dzb[t, hd], kv[t][0])
                dq_acc[hd, rows, :] += upd
            for t, st in enumerate(starts):
                dk = _dot_tn(dzb[t, 0], heads[0][0])
                dv = _dot_tn(a[t, 0].astype(BF16), heads[0][1])
                for hd in range(1, nhd):
                    dk = dk + _dot_tn(dzb[t, hd], heads[hd][0])
                    dv = dv + _dot_tn(a[t, hd].astype(BF16), heads[hd][1])
                dk_acc[pl.ds(st, tk), :] += dk
                dv_acc[pl.ds(st, tk), :] += dv

        def full_step(j, carry):
            tiles([(j, None)])
            return carry

        lax.fori_loop(0, i * band, full_step, 0)
        for jb in range(band):
            rows = slice(jb * tk, tq)
            tiles([(i * band + jb, (jb * tk + col < row)[rows])], rows)
        dq_ref[...] = (jnp.where(m0, dq_acc[0], dq_acc[1]) * SB_SCALE).astype(BF16)

        @pl.when(i == nq - 1)
        def _():
            dk_ref[...] = dk_acc[...].astype(BF16)
            dv_ref[...] = dv_acc[...].astype(BF16)

    qmap = lambda g, i: (i, g)
    return _pallas(
        body, rider, name="attn_bwd", grid=(ngrp, nq),
        in_specs=[pl.BlockSpec((tq, LANES), qmap),
                  pl.BlockSpec((T, LANES), lambda g, i: (0, ngrp + g)),
                  pl.BlockSpec((T, LANES), lambda g, i: (0, 2 * ngrp + g)),
                  pl.BlockSpec((tq, LANES), qmap), pl.BlockSpec((tq, LANES), qmap)],
        out_specs=[pl.BlockSpec((tq, LANES), qmap),
                   pl.BlockSpec((T, LANES), lambda g, i: (0, g)),
                   pl.BlockSpec((T, LANES), lambda g, i: (0, g))],
        out_shape=[jax.ShapeDtypeStruct((T, SB_WIDTH), BF16)] * 3,
        scratch_shapes=[pltpu.VMEM((2, tq, LANES), F32), pltpu.VMEM((T, LANES), F32),
                        pltpu.VMEM((T, LANES), F32), pltpu.VMEM((2, tq, 1), F32),
                        pltpu.VMEM((2, tq, 1), F32)],
        compiler_params=_params("arbitrary", "arbitrary"),
    )(qkv, qkv, qkv, d_o, ltot)


def _mix_out_fwd(h, gm, sb, w):
    T, D = h.shape
    tm = min(ROW_TILE, T)

    def body(h_ref, gm_ref, sb_ref, w_ref, o_ref):
        o_ref[...] = h_ref[...] + _dot(gm_ref[...], w_ref[:GM_WIDTH, :]) + _dot(sb_ref[...], w_ref[GM_WIDTH:, :])

    row = lambda i: (i, 0)
    return pl.pallas_call(
        body, name="mix_out_fwd", grid=(T // tm,),
        in_specs=[pl.BlockSpec((tm, D), row), pl.BlockSpec((tm, GM_WIDTH), row), pl.BlockSpec((tm, SB_WIDTH), row),
                  pl.BlockSpec((GM_WIDTH + SB_WIDTH, D), lambda i: (0, 0))],
        out_specs=pl.BlockSpec((tm, D), row),
        out_shape=jax.ShapeDtypeStruct((T, D), F32),
        compiler_params=_params("arbitrary"),
    )(h, gm, sb, w)


def _mix_out_bwd(dh, w):
    T, D = dh.shape
    tm = min(ROW_TILE, T)

    def body(dh_ref, w_ref, dgm_ref, dsb_ref, dhb_ref):
        dhb = dh_ref[...].astype(BF16)
        dhb_ref[...] = dhb
        dgm_ref[...] = _dot_nt(dhb, w_ref[:GM_WIDTH, :])
        dsb_ref[...] = _dot_nt(dhb, w_ref[GM_WIDTH:, :])

    row = lambda i: (i, 0)
    return pl.pallas_call(
        body, name="mix_out_bwd", grid=(T // tm,),
        in_specs=[pl.BlockSpec((tm, D), row), pl.BlockSpec((GM_WIDTH + SB_WIDTH, D), lambda i: (0, 0))],
        out_specs=[pl.BlockSpec((tm, GM_WIDTH), row), pl.BlockSpec((tm, SB_WIDTH), row), pl.BlockSpec((tm, D), row)],
        out_shape=[jax.ShapeDtypeStruct((T, GM_WIDTH), F32), jax.ShapeDtypeStruct((T, SB_WIDTH), F32),
                   jax.ShapeDtypeStruct((T, D), BF16)],
        compiler_params=_params("arbitrary"),
    )(dh, w)


def _tail(h3, p, target, g_ple, g_fin, w_gate, w_proj):
    T, D = h3.shape
    PD = p.shape[1]
    tm = min(ROW_TILE, T)

    def body(h_ref, p_ref, t_ref, gp_ref, gf_ref, wg_ref, wp_ref,
             loss_ref, dh_ref, n4_ref, dgl_ref, dpp_ref, dgp_ref, dgf_ref):
        @pl.when(pl.program_id(0) == 0)
        def _():
            loss_ref[...] = jnp.zeros_like(loss_ref)
            dgp_ref[...] = jnp.zeros_like(dgp_ref)
            dgf_ref[...] = jnp.zeros_like(dgf_ref)

        h3v = h_ref[...]
        gp = gp_ref[...]
        gf = gf_ref[...]
        r3, xh3 = _rms_parts(h3v)
        n4 = (xh3 * gp).astype(BF16)
        n4_ref[...] = n4
        gate = _sigmoid(_dot(n4, wg_ref[...]))
        pp = _dot(p_ref[...].astype(BF16), wp_ref[...])
        h4 = h3v + gate * pp
        r4, xh4 = _rms_parts(h4)
        err = xh4 * gf - t_ref[...]
        loss_ref[...] += jnp.full(loss_ref.shape, (0.5 / D) * jnp.sum(err * err), F32)
        dy = err * (1.0 / D)
        dgf_ref[...] += jnp.sum(dy * xh4, axis=0, keepdims=True)
        dyg = dy * gf
        dh4 = r4 * (dyg - xh4 * jnp.mean(dyg * xh4, axis=-1, keepdims=True))
        dpp_ref[...] = (dh4 * gate).astype(BF16)
        dgl = (dh4 * pp * gate * (1.0 - gate)).astype(BF16)
        dgl_ref[...] = dgl
        dn4 = _dot_nt(dgl, wg_ref[...])
        dgp_ref[...] += jnp.sum(dn4 * xh3, axis=0, keepdims=True)
        dn4g = dn4 * gp
        dh_ref[...] = dh4 + r3 * (dn4g - xh3 * jnp.mean(dn4g * xh3, axis=-1, keepdims=True))

    row = lambda i: (i, 0)
    one = lambda i: (0, 0)
    return pl.pallas_call(
        body, name="tail", grid=(T // tm,),
        in_specs=[pl.BlockSpec((tm, D), row), pl.BlockSpec((tm, PD), row), pl.BlockSpec((tm, D), row),
                  pl.BlockSpec((1, D), one), pl.BlockSpec((1, D), one),
                  pl.BlockSpec((D, D), one), pl.BlockSpec((PD, D), one)],
        out_specs=[pl.BlockSpec((1, LANES), one), pl.BlockSpec((tm, D), row), pl.BlockSpec((tm, D), row),
                   pl.BlockSpec((tm, D), row), pl.BlockSpec((tm, D), row),
                   pl.BlockSpec((1, D), one), pl.BlockSpec((1, D), one)],
        out_shape=[jax.ShapeDtypeStruct((1, LANES), F32), jax.ShapeDtypeStruct((T, D), F32),
                   jax.ShapeDtypeStruct((T, D), BF16), jax.ShapeDtypeStruct((T, D), BF16),
                   jax.ShapeDtypeStruct((T, D), BF16),
                   jax.ShapeDtypeStruct((1, D), F32), jax.ShapeDtypeStruct((1, D), F32)],
        compiler_params=_params("arbitrary"),
    )(h3, p, target, g_ple, g_fin, w_gate, w_proj)


FFN1_W = ("ffn1_w_in", "ffn1_w_out")
MIX_W = ("w_mix_in", "w_mix_out")
REST_W = ("ffn2_w_in", "ffn2_w_out", "ple_w_gate", "ple_w_proj")
BIG_W = FFN1_W + MIX_W + REST_W
COLUMN_SHARDED = ("w_mix_in", "ple_w_proj")


class _Traffic:
    def __init__(self, shards):
        self.shards = shards
        self.parts = {}
        self.blocks, self.sums = {}, {}

    @staticmethod
    def _full(name, gathered):
        if name in COLUMN_SHARDED:
            return jnp.transpose(gathered, (1, 0, 2)).reshape(gathered.shape[1], -1)
        if name.endswith("_w_in"):
            return gathered
        return gathered.reshape(-1, gathered.shape[-1])

    @staticmethod
    def _blocks(name, grad):
        name = name.split("/")[0]
        if name in COLUMN_SHARDED:
            return jnp.transpose(grad.reshape(grad.shape[0], N_DEV, -1), (1, 0, 2))
        if name.endswith("_w_in"):
            return grad
        return grad.reshape(N_DEV, -1, grad.shape[-1])

    def gather_now(self, names):
        got = _exchange("gather_" + names[0], [self.shards[n] for n in names], [GATHER] * len(names))
        return self.gathered(names, got)

    def gather_rider(self, names):
        return [self.shards[n] for n in names], [GATHER] * len(names)

    def gathered(self, names, got):
        return {n: self._full(n, g) for n, g in zip(names, got)}

    def scatter_rider(self, grads):
        return [self._blocks(n, g) for n, g in grads.items()], [SCATTER] * len(grads)

    def scattered(self, names, got):
        self.parts.update(zip(names, got))


    def pair_rider(self, name, grad):
        self.blocks[name] = self._blocks(name, grad)
        return [self.blocks[name]], [PAIR]

    def paired(self, name, got):
        self.sums[name] = _stage_pair_sums("pair_sum_" + name.replace("/", "_"), self.blocks[name], got)

    def chip_rider(self, name):
        return [self.sums[name]], [CHIP]

    def chipped(self, name, got):
        self.parts[name] = got


def _local_step(traffic, x, p, target, g1, gmix, gv, ws, b_t, g2, gple, gfin, pack_small):
    T, D = x.shape
    tm = min(ROW_TILE, T)

    w = traffic.gather_now(FFN1_W)
    h1, n1, G1, U1, a1, *got = _ffn_fwd("ffn1_fwd", x, g1, w["ffn1_w_in"], w["ffn1_w_out"],
                                        rider=traffic.gather_rider(MIX_W))
    w.update(traffic.gathered(MIX_W, got))
    n2, zuv, qkv = _mix_in_fwd(h1, gmix, w["w_mix_in"])
    gm = _gmlp_fwd(zuv, gv, ws, b_t)
    sb, ltot, *got = _attn_fwd(qkv, rider=traffic.gather_rider(REST_W))
    w.update(traffic.gathered(REST_W, got))
    h2 = _mix_out_fwd(h1, gm, sb, w["w_mix_out"])
    h3, n3, G2, U2, a2 = _ffn_fwd("ffn2_fwd", h2, g2, w["ffn2_w_in"], w["ffn2_w_out"])
    loss, dh3, n4, d_gl, d_pp, dg_ple, dg_fin = _tail(h3, p, target, gple, gfin, w["ple_w_gate"], w["ple_w_proj"])

    nb, _, FB = w["ffn1_w_in"].shape
    nh = nb // 2

    tt = min(GRAD_ROW_TILE, T)

    def dw_out(name, a, d_out):
        return _matmul_tn(name, a, d_out, nh, (1, tt, FB), lambda j, t: (j, t, 0), (tt, D), lambda j, t: (t, 0),
                          (nh, FB, D), (1, FB, D), lambda j, t: (j, 0, 0))

    def dense_tn(name, a, b, ncol):
        ka, nbw = a.shape[1], b.shape[1] // ncol
        return _matmul_tn(name, a, b, ncol, (tt, ka), lambda j, t: (t, 0), (tt, nbw), lambda j, t: (t, j),
                          (ka, b.shape[1]), (ka, nbw), lambda j, t: (0, j))

    grads = dict(ple_w_gate=dense_tn("dw_ple_gate", n4, d_gl, 2), ple_w_proj=dense_tn("dw_ple_proj", p, d_pp, 1))
    dh2, dg2, dG2, dU2, dout2 = _ffn_bwd("ffn2_bwd", dh3, h2, g2, G2, U2, w["ffn2_w_in"], w["ffn2_w_out"])
    grads["ffn2_w_in"], = _dw_in("ffn2_dw_in", n3, dG2, dU2)
    grads["ffn2_w_out"] = dw_out("ffn2_dw_out", a2, dout2)
    grads = {n: grads[n] for n in REST_W}

    d_gm, d_sb, dh2_bf = _mix_out_bwd(dh2, w["w_mix_out"])
    grads["w_mix_out"] = jnp.concatenate([dense_tn("dw_mix_out_gm", gm, dh2_bf, 1),
                                          dense_tn("dw_mix_out_sb", sb, dh2_bf, 1)], axis=0)
    dzuv, dgv, dws, db_t = _gmlp_bwd(zuv, d_gm, gv, ws, b_t)
    dq, dk, dv, *got = _attn_bwd(qkv, d_sb, ltot, rider=traffic.scatter_rider(grads))
    traffic.scattered(list(grads), got)
    dqkv = jnp.concatenate([dq, dk, dv], axis=1)
    dw_mi = jnp.concatenate([dense_tn("dw_mix_in_uv", n2, dzuv, 2), dense_tn("dw_mix_in_qkv", n2, dqkv, 3)], axis=1)
    dh1, dgmix, dout1 = _mix_in_bwd(dzuv, dqkv, w["w_mix_in"], h1, gmix, dh2)

    def dw_out_riding(name, a, d_out, rider):
        return _matmul_tn(name, a, d_out, nh, (1, tt, FB), lambda j, t: (j, t, 0), (tt, D), lambda j, t: (t, 0),
                          (nh, FB, D), (1, FB, D), lambda j, t: (j, 0, 0), rider=rider)

    def both(*riders):
        return [x for r in riders for x in r[0]], [k for r in riders for k in r[1]]

    dw_out1, got = dw_out_riding("ffn1_dw_out", a1, dout1, traffic.pair_rider("w_mix_in", dw_mi))
    traffic.paired("w_mix_in", got)
    dG1, dU1, got_mi, got = _ffn_bwd_gates("ffn1_bwd_gates", dout1, G1, U1, w["ffn1_w_out"],
                                           rider=both(traffic.chip_rider("w_mix_in"),
                                                      traffic.pair_rider("ffn1_w_out", dw_out1)))
    traffic.chipped("w_mix_in", got_mi)
    traffic.paired("ffn1_w_out", got)
    half = D // 2
    top, got = _dw_in("ffn1_dw_in_top", n1[:, :half], dG1, dU1, rider=traffic.chip_rider("ffn1_w_out"))
    traffic.chipped("ffn1_w_out", got)
    traffic.paired("ffn1_w_in/0", _exchange("pair_top", *traffic.pair_rider("ffn1_w_in/0", top))[0])
    bottom, got = _dw_in("ffn1_dw_in_bottom", n1[:, half:], dG1, dU1, rider=traffic.chip_rider("ffn1_w_in/0"))
    traffic.chipped("ffn1_w_in/0", got)
    traffic.paired("ffn1_w_in/1", _exchange("pair_last", *traffic.pair_rider("ffn1_w_in/1", bottom))[0])
    dx, dg1, got = _ffn_bwd_input("ffn1_bwd_input", dh1, x, g1, dG1, dU1, w["ffn1_w_in"],
                                  rider=traffic.chip_rider("ffn1_w_in/1"))
    traffic.chipped("ffn1_w_in/1", got)

    small = pack_small(dict(ffn1_norm=dg1, mix_norm=dgmix, gmlp_v_norm=dgv, gmlp_w_s=dws, gmlp_b=jnp.transpose(db_t),
                            ffn2_norm=dg2, ple_norm=dg_ple, final_norm=dg_fin), loss)
    return dx, small


def _my_index():
    return 4 * lax.axis_index("x") + 2 * lax.axis_index("y") + lax.axis_index("c")


def _stage_pair_sums(name, blocks, got):
    _, R, C = blocks.shape
    me = _my_index()
    index = jnp.stack([jnp.bitwise_xor(me, 2 * c) for c in range(1, N_CHIPS_AWAY + 1)] + [me, me]).astype(jnp.int32)

    def body(index_ref, b_ref, g_ref, o_ref):
        i = pl.program_id(0)

        @pl.when(i < N_CHIPS_AWAY)
        def _():
            o_ref[...] = (b_ref[...].astype(F32) + g_ref[...].astype(F32)).astype(BF16)

        @pl.when(i == N_CHIPS_AWAY)
        def _():
            o_ref[...] = g_ref[...]

        @pl.when(i == N_CHIPS_AWAY + 1)
        def _():
            o_ref[...] = b_ref[...]

    return pl.pallas_call(
        body, name=name,
        grid_spec=pltpu.PrefetchScalarGridSpec(
            num_scalar_prefetch=1, grid=(PART_SLOTS,),
            in_specs=[pl.BlockSpec((1, R, C), lambda i, idx: (idx[i], 0, 0)),
                      pl.BlockSpec((1, R, C), lambda i, idx: (jnp.minimum(i, N_CHIPS_AWAY), 0, 0))],
            out_specs=pl.BlockSpec((1, R, C), lambda i, idx: (i, 0, 0))),
        out_shape=jax.ShapeDtypeStruct((PART_SLOTS, R, C), BF16), compiler_params=_params("arbitrary"),
    )(index, blocks, got)


def _peer(d):
    x, y, c = lax.axis_index("x"), lax.axis_index("y"), lax.axis_index("c")
    px = 1 - x if d & 4 else x
    py = 1 - y if d & 2 else y
    pc = 1 - c if d & 1 else c
    return (px, py, pc), 4 * px + 2 * py + pc


GATHER, SCATTER, PAIR, CHIP = "gather", "scatter", "pair", "chip"


class _ExchangePlan:
    def __init__(self, ins, outs, send, recv, local, kinds):
        self.ins, self.outs, self.send, self.recv, self.local, self.kinds = ins, outs, send, recv, local, kinds
        self.scatter = [k == SCATTER for k in kinds]
        self.me = _peer(0)[1]

    def _remote(self, t, sem, src, slot, peer):
        return pltpu.make_async_remote_copy(
            src_ref=src, dst_ref=self.outs[t].at[slot], send_sem=self.send.at[t, sem], recv_sem=self.recv.at[t, sem],
            device_id=peer, device_id_type=MESH)

    def _own(self, t):
        if self.kinds[t] == CHIP:
            kept = pl.ds(N_CHIPS_AWAY, PART_SLOTS - N_CHIPS_AWAY)
            return pltpu.make_async_copy(self.ins[t].at[kept], self.outs[t].at[kept], self.local.at[t])
        src = self.ins[t].at[self.me] if self.scatter[t] else self.ins[t]
        return pltpu.make_async_copy(src, self.outs[t].at[self.me], self.local.at[t])

    def _n_direct(self, t):
        return {SCATTER: N_DEV - 1, GATHER: N_CHIPS_AWAY + 1, PAIR: N_CHIPS_AWAY + 1, CHIP: N_CHIPS_AWAY}[self.kinds[t]]

    def _direct(self, t, k):
        kind = self.kinds[t]
        if kind == SCATTER:
            peer, slot = _peer(k + 1)
            return self._remote(t, k, self.ins[t].at[slot], self.me, peer)
        if kind == GATHER:
            return self._remote(t, k, self.ins[t], self.me, _peer(2 * k if k else 1)[0])
        if kind == PAIR:
            block = _peer(2 * (k + 1) + 1 if k < N_CHIPS_AWAY else 1)[1]
            return self._remote(t, k, self.ins[t].at[block], k, _peer(1)[0])
        return self._remote(t, k, self.ins[t].at[k], k, _peer(2 * (k + 1))[0])

    def _has_own(self, t):
        return self.kinds[t] != PAIR

    def _relay(self, t, c):
        slot = _peer(2 * c)[1]
        return self._remote(t, N_CHIPS_AWAY + c, self.outs[t].at[slot], slot, _peer(1)[0])

    def start(self):
        for t in range(len(self.ins)):
            if self._has_own(t):
                self._own(t).start()
            for k in range(self._n_direct(t)):
                self._direct(t, k).start()

    def relay(self):
        for t in self._gathers():
            for c in range(1, N_CHIPS_AWAY + 1):
                self._direct(t, c).wait_recv()
                self._relay(t, c).start()

    def _gathers(self):
        return [t for t in range(len(self.ins)) if self.kinds[t] == GATHER]

    def finish(self):
        for t in range(len(self.ins)):
            if self._has_own(t):
                self._own(t).wait()
            for k in range(self._n_direct(t)):
                self._direct(t, k).wait_send()
                if self.kinds[t] != GATHER or k == 0:
                    self._direct(t, k).wait_recv()
        for t in self._gathers():
            for c in range(1, N_CHIPS_AWAY + 1):
                self._relay(t, c).wait()


def _exchange_shapes(arrays, kinds):
    shape = {GATHER: lambda a: (N_DEV,) + a.shape, SCATTER: lambda a: a.shape, CHIP: lambda a: a.shape,
             PAIR: lambda a: (N_CHIPS_AWAY + 1,) + a.shape[1:]}
    return [jax.ShapeDtypeStruct(shape[k](a), a.dtype) for a, k in zip(arrays, kinds)]


def _exchange_sems(n):
    return [pltpu.SemaphoreType.DMA((n, N_DEV - 1)), pltpu.SemaphoreType.DMA((n, N_DEV - 1)),
            pltpu.SemaphoreType.DMA((n,))]


_ANY = pl.BlockSpec(memory_space=pl.ANY)


def _exchange(name, arrays, scatter):
    n = len(arrays)

    def body(*refs):
        plan = _ExchangePlan(refs[:n], refs[n:2 * n], *refs[2 * n:], scatter)
        plan.start()
        plan.relay()
        plan.finish()

    return pl.pallas_call(
        body, name=name, in_specs=[_ANY] * n, out_specs=[_ANY] * n, out_shape=_exchange_shapes(arrays, scatter),
        scratch_shapes=_exchange_sems(n),
    )(*arrays)


def _pallas(body, rider, *, name, grid, in_specs, out_specs, out_shape, scratch_shapes=(), compiler_params=None):
    if rider is None:
        return pl.pallas_call(body, name=name, grid=grid, in_specs=in_specs, out_specs=out_specs, out_shape=out_shape,
                              scratch_shapes=list(scratch_shapes), compiler_params=compiler_params)
    arrays, scatter = rider
    n, ni, no, ns = len(arrays), len(in_specs), len(out_specs), len(scratch_shapes)

    def carried(*refs):
        ins, r_in = refs[:ni], refs[ni:ni + n]
        outs, r_out = refs[ni + n:ni + n + no], refs[ni + n + no:ni + 2 * n + no]
        scratch, sems = refs[ni + 2 * n + no:ni + 2 * n + no + ns], refs[ni + 2 * n + no + ns:]
        step = 0
        for ax, g in enumerate(grid):
            step = step * g + pl.program_id(ax)
        steps = functools.reduce(lambda a, b: a * b, grid)

        @pl.when(step == 0)
        def _():
            _ExchangePlan(r_in, r_out, *sems, scatter).start()

        @pl.when(step == steps // 2)
        def _():
            _ExchangePlan(r_in, r_out, *sems, scatter).relay()

        body(*ins, *outs, *scratch)

        @pl.when(step == steps - 1)
        def _():
            _ExchangePlan(r_in, r_out, *sems, scatter).finish()

    call = pl.pallas_call(
        carried, name=name, grid=grid, in_specs=list(in_specs) + [_ANY] * n, out_specs=list(out_specs) + [_ANY] * n,
        out_shape=list(out_shape) + _exchange_shapes(arrays, scatter),
        scratch_shapes=list(scratch_shapes) + _exchange_sems(n), compiler_params=compiler_params)
    return lambda *args: call(*args, *arrays)


def _adamw_math(g, w, m, v):
    m_new = ADAM_B1 * m + (1.0 - ADAM_B1) * g
    v_new = ADAM_B2 * v + (1.0 - ADAM_B2) * (g * g)
    m_hat = m_new / (1.0 - ADAM_B1 ** ADAM_STEP)
    v_hat = v_new / (1.0 - ADAM_B2 ** ADAM_STEP)
    delta = -ADAM_LR * (m_hat / (jnp.sqrt(v_hat) + ADAM_EPS) + ADAM_WD * w)
    return delta, m_new, v_new


def _adamw(name, parts, w, m, v, rider=None):
    R, C = w.shape
    slots = parts.shape[0]
    tr = R
    for cand in (256, 128, 64, 32, 16, 8):
        if R % cand == 0:
            tr = cand
            break

    def body(p_ref, w_ref, m_ref, v_ref, g_ref, d_ref, nm_ref, nv_ref):
        g = p_ref[0].astype(F32)
        for j in range(1, slots):
            g = g + p_ref[j].astype(F32)
        g_ref[...] = g
        d_ref[...], nm_ref[...], nv_ref[...] = _adamw_math(g, w_ref[...], m_ref[...], v_ref[...])

    row = lambda i: (i, 0)
    spec = pl.BlockSpec((tr, C), row)
    return _pallas(
        body, rider, name=name, grid=(R // tr,),
        in_specs=[pl.BlockSpec((slots, tr, C), lambda i: (0, i, 0)), spec, spec, spec],
        out_specs=[spec] * 4,
        out_shape=[jax.ShapeDtypeStruct((R, C), F32)] * 4,
        compiler_params=_params("arbitrary"),
    )(parts, w, m, v)


def _rows128(a):
    flat = a.reshape(-1, LANES)
    pad = (-flat.shape[0]) % SMALL_ROWS_ALIGN
    return jnp.pad(flat, ((0, pad), (0, 0))) if pad else flat


def _unrows(packed, like):
    n = like.size // LANES
    return packed[:n].reshape(like.shape)


def kernel(x, p, ffn1_norm, ffn1_w_in, ffn1_w_out, mix_norm, w_mix_in, gmlp_v_norm, gmlp_w_s, gmlp_b, w_mix_out, ffn2_norm, ffn2_w_in, ffn2_w_out, ple_norm, ple_w_gate, ple_w_proj, final_norm, loss_target, m_ffn1_norm, m_ffn1_w_in, m_ffn1_w_out, m_mix_norm, m_w_mix_in, m_gmlp_v_norm, m_gmlp_w_s, m_gmlp_b, m_w_mix_out, m_ffn2_norm, m_ffn2_w_in, m_ffn2_w_out, m_ple_norm, m_ple_w_gate, m_ple_w_proj, m_final_norm, v_ffn1_norm, v_ffn1_w_in, v_ffn1_w_out, v_mix_norm, v_w_mix_in, v_gmlp_v_norm, v_gmlp_w_s, v_gmlp_b, v_w_mix_out, v_ffn2_norm, v_ffn2_w_in, v_ffn2_w_out, v_ple_norm, v_ple_w_gate, v_ple_w_proj, v_final_norm):
    names = ["ffn1_norm", "ffn1_w_in", "ffn1_w_out", "mix_norm", "w_mix_in", "gmlp_v_norm", "gmlp_w_s", "gmlp_b",
             "w_mix_out", "ffn2_norm", "ffn2_w_in", "ffn2_w_out", "ple_norm", "ple_w_gate", "ple_w_proj", "final_norm"]
    W = dict(zip(names, [ffn1_norm, ffn1_w_in, ffn1_w_out, mix_norm, w_mix_in, gmlp_v_norm, gmlp_w_s, gmlp_b,
                         w_mix_out, ffn2_norm, ffn2_w_in, ffn2_w_out, ple_norm, ple_w_gate, ple_w_proj, final_norm]))
    M = dict(zip(names, [m_ffn1_norm, m_ffn1_w_in, m_ffn1_w_out, m_mix_norm, m_w_mix_in, m_gmlp_v_norm, m_gmlp_w_s,
                         m_gmlp_b, m_w_mix_out, m_ffn2_norm, m_ffn2_w_in, m_ffn2_w_out, m_ple_norm, m_ple_w_gate,
                         m_ple_w_proj, m_final_norm]))
    V = dict(zip(names, [v_ffn1_norm, v_ffn1_w_in, v_ffn1_w_out, v_mix_norm, v_w_mix_in, v_gmlp_v_norm, v_gmlp_w_s,
                         v_gmlp_b, v_w_mix_out, v_ffn2_norm, v_ffn2_w_in, v_ffn2_w_out, v_ple_norm, v_ple_w_gate,
                         v_ple_w_proj, v_final_norm]))
    small = [n for n in names if n not in BIG_W]
    D = x.shape[-1]

    def pack(src, last):
        return jnp.concatenate([_rows128(src[n]) for n in small] + [last], axis=0)

    offs = [0]
    for n in small:
        offs.append(offs[-1] + _rows128(W[n]).shape[0])

    traffic = _Traffic({n: W[n][0].astype(BF16) for n in BIG_W})
    dx, small_mine = _local_step(
        traffic, x[0], p[0, 0], loss_target[0],
        W["ffn1_norm"], W["mix_norm"], W["gmlp_v_norm"], W["gmlp_w_s"][0], jnp.transpose(W["gmlp_b"][0]),
        W["ffn2_norm"], W["ple_norm"], W["final_norm"].reshape(1, D),
        lambda grads, loss_part: pack(grads, jnp.broadcast_to(loss_part, (SMALL_ROWS_ALIGN, LANES))))

    out = {}
    parts = traffic.parts
    parts["ffn1_w_in"] = jnp.concatenate([parts["ffn1_w_in/0"], parts["ffn1_w_in/1"]], axis=1)
    carrier = "ffn2_w_out"
    *out[carrier], small_parts = _adamw("adamw_" + carrier, parts[carrier], W[carrier][0], M[carrier][0], V[carrier][0],
                                        rider=([small_mine], [GATHER]))
    for n in BIG_W:
        if n != carrier:
            out[n] = _adamw("adamw_" + n, parts[n], W[n][0], M[n][0], V[n][0])
    zeros = jnp.zeros((SMALL_ROWS_ALIGN, LANES), F32)
    sg, sd, sm, sv = _adamw("adamw_small", small_parts, pack(W, zeros), pack(M, zeros), pack(V, zeros))
    for k, n in enumerate(small):
        out[n] = tuple(_unrows(arr[offs[k]:offs[k + 1]], W[n]) for arr in (sg, sd, sm, sv))
    loss = sg[offs[len(small)], 0]

    res = [loss, dx[None]]
    for k in range(4):
        res += [out[n][k].reshape(W[n].shape) for n in names]
    return tuple(res)
```
